```python
import functools
import jax, jax.numpy as jnp
from jax import lax
import numpy as np

D_MODEL = 1024
BATCH = 4
SEQ = 4096
DEPTH = 2

GRID_W = 64
CTX_LEN = 256
HEAD_DIM = 64
N_CONV_GROUPS = D_MODEL // 256
N_RET_HEADS = (D_MODEL // HEAD_DIM - N_CONV_GROUPS) // 2
N_NA_HEADS = D_MODEL // HEAD_DIM - N_CONV_GROUPS - N_RET_HEADS
D_RET = N_RET_HEADS * HEAD_DIM
D_CONV = N_CONV_GROUPS * HEAD_DIM
D_NA = N_NA_HEADS * HEAD_DIM
D_MIX = D_RET + D_CONV + D_NA
D_IN_PROJ = 4 * D_RET + 3 * D_CONV + 3 * D_NA
RET_CHUNK = 128
CONV_WIDTH = 3
NA_ROWS = 8
NA_COLS = 16
D_FF = 2816
N_EXPERTS = 8
TOP_K = 2
D_FF_EXPERT = 2816
N_DENSE = (DEPTH + 1) // 2
N_MOE = DEPTH // 2
ROPE_BASE = 10000.0
NORM_EPS = 1e-6

kernel_name = 'hybrid_retention_shortconv_natten_moe_dit'


def rms_norm(x, g):
    xf = x.astype(jnp.float32)
    y = xf * lax.rsqrt(jnp.mean(xf * xf, axis=-1, keepdims=True) + NORM_EPS)
    return (y * g.astype(jnp.float32)).astype(x.dtype)


def modulate(h, shift, scale):
    return h * (1.0 + scale) + shift


def split_heads(t, n_heads):
    b, n, _ = t.shape
    return t.reshape(b, n, n_heads, HEAD_DIM).transpose(0, 2, 1, 3)


def merge_heads(t):
    b, h, n, d = t.shape
    return t.transpose(0, 2, 1, 3).reshape(b, n, h * d)


def flip_seq(t):
    return jnp.flip(t, axis=2)


def split_in_proj(t):
    sizes = (D_RET,) * 4 + (D_CONV,) * 3 + (D_NA,) * 3
    points = np.cumsum(sizes)[:-1].tolist()
    return jnp.split(t, points, axis=-1)


def axial_rope_tables(n_tok):
    t = jnp.arange(n_tok, dtype=jnp.int32)
    row = (t // GRID_W).astype(jnp.float32)
    col = (t % GRID_W).astype(jnp.float32)
    n_freq = HEAD_DIM // 4
    inv_freq = ROPE_BASE ** (-jnp.arange(n_freq, dtype=jnp.float32) / n_freq)
    ang_r = row[:, None] * inv_freq
    ang_c = col[:, None] * inv_freq
    return (jnp.cos(ang_r), jnp.sin(ang_r), jnp.cos(ang_c), jnp.sin(ang_c))


def _rotate(x, cos, sin):
    x1, x2 = jnp.split(x, 2, axis=-1)
    return jnp.concatenate([x1 * cos - x2 * sin, x1 * sin + x2 * cos], axis=-1)


def apply_axial_rope(x, rope):
    cos_r, sin_r, cos_c, sin_c = (r.astype(x.dtype) for r in rope)
    xr, xc = jnp.split(x, 2, axis=-1)
    return jnp.concatenate([_rotate(xr, cos_r, sin_r), _rotate(xc, cos_c, sin_c)], axis=-1)


def retention_final_state(k, v, log_gamma):
    n_tok = k.shape[2]
    w = jnp.exp(log_gamma[:, None] * (n_tok - 1.0 - jnp.arange(n_tok, dtype=jnp.float32)))
    return jnp.einsum('bhnd,bhne->bhde', k.astype(jnp.float32) * w[:, :, None], v.astype(jnp.float32))


def retention_chunkwise(q, k, v, log_gamma, s0):
    b, h, n_tok, dh = q.shape
    n = n_tok // RET_CHUNK
    qc, kc, vc = (t.astype(jnp.float32).reshape(b, h, n, RET_CHUNK, dh) for t in (q, k, v))
    pos = jnp.arange(RET_CHUNK, dtype=jnp.float32)
    lg = log_gamma[:, None]
    diff = pos[:, None] - pos[None, :]
    intra_decay = jnp.where(diff >= 0, jnp.exp(lg[:, :, None] * jnp.maximum(diff, 0.0)), 0.0)
    q_decay = jnp.exp(lg * (pos + 1.0))
    k_decay = jnp.exp(lg * (RET_CHUNK - 1.0 - pos))
    chunk_decay = jnp.exp(log_gamma * RET_CHUNK)[None, :, None, None]
    scores = jnp.einsum('bhnid,bhnjd->bhnij', qc, kc) * intra_decay[:, None]
    o_intra = jnp.einsum('bhnij,bhnjd->bhnid', scores, vc)
    u = jnp.einsum('bhnjd,bhnje->nbhde', kc * k_decay[:, None, :, None], vc)

    def step(state, u_n):
        return chunk_decay * state + u_n, state

    _, s_before = lax.scan(step, s0, u)
    o_inter = jnp.einsum('bhnid,nbhde->bhnie', qc * q_decay[:, None, :, None], s_before)
    return (o_intra + o_inter).reshape(b, h, n_tok, dh)


def bidir_retention(q, k, v, log_gamma, s_fwd0, s_bwd0):
    o_f = retention_chunkwise(q, k, v, log_gamma[0], s_fwd0)
    o_b = retention_chunkwise(flip_seq(q), flip_seq(k), flip_seq(v), log_gamma[1], s_bwd0)
    return o_f + flip_seq(o_b)


def retention_output(o, gate, gn_g):
    mu = jnp.mean(o, axis=-1, keepdims=True)
    var = jnp.mean(jnp.square(o - mu), axis=-1, keepdims=True)
    o = merge_heads((o - mu) * lax.rsqrt(var + NORM_EPS)) * gn_g.astype(jnp.float32)
    return (jax.nn.silu(gate.astype(jnp.float32)) * o).astype(gate.dtype)


def short_gated_conv(b_gate, c_gate, x_in, conv_w):
    u = c_gate * x_in
    w = conv_w[:, None, :].astype(u.dtype)
    y = lax.conv_general_dilated(u, w, window_strides=(1,), padding=((CONV_WIDTH // 2, CONV_WIDTH // 2),),
                                 dimension_numbers=('NWC', 'WIO', 'NWC'), feature_group_count=u.shape[-1])
    return b_gate * y


def neighbourhood_attention(q, k, v, k_ctx, v_ctx, rpb):
    b, h, n_tok, dh = q.shape
    rows = n_tok // GRID_W
    kr = min(NA_ROWS, rows)
    scale = dh ** -0.5
    r = jnp.arange(rows)
    cidx = jnp.arange(GRID_W)
    r0 = jnp.clip(r - kr // 2, 0, rows - kr)
    key_rows = r0[:, None] + jnp.arange(kr)[None, :]
    c0 = jnp.clip(cidx - NA_COLS // 2, 0, GRID_W - NA_COLS)
    col_in = (cidx[None, :] >= c0[:, None]) & (cidx[None, :] < c0[:, None] + NA_COLS)
    qg = q.reshape(b, h, rows, GRID_W, dh)
    kg = k.reshape(b, h, rows, GRID_W, dh)[:, :, key_rows]
    vg = v.reshape(b, h, rows, GRID_W, dh)[:, :, key_rows]
    s_loc = jnp.einsum('bhrqd,bhrikd->bhrqik', qg, kg).astype(jnp.float32) * scale
    dr = key_rows - r[:, None] + (NA_ROWS - 1)
    dc = jnp.clip(cidx[None, :] - cidx[:, None], -(NA_COLS - 1), NA_COLS - 1) + (NA_COLS - 1)
    bias = rpb[:, dr[:, None, :, None], dc[None, :, None, :]].astype(jnp.float32)
    s_loc = jnp.where(col_in[None, None, None, :, None, :], s_loc + bias[None], -jnp.inf)
    s_loc = s_loc.reshape(b, h, rows, GRID_W, kr * GRID_W)
    s_ctx = jnp.einsum('bhrqd,bhld->bhrql', qg, k_ctx).astype(jnp.float32) * scale
    p = jax.nn.softmax(jnp.concatenate([s_loc, s_ctx], axis=-1), axis=-1)
    p_loc = p[..., :kr * GRID_W].reshape(b, h, rows, GRID_W, kr, GRID_W).astype(v.dtype)
    p_ctx = p[..., kr * GRID_W:].astype(v.dtype)
    o = jnp.einsum('bhrqik,bhrikd->bhrqd', p_loc, vg) + jnp.einsum('bhrql,bhld->bhrqd', p_ctx, v_ctx)
    return o.reshape(b, h, n_tok, dh)


def context_attention(q, k, v):
    s = jnp.einsum('bhqd,bhkd->bhqk', q, k).astype(jnp.float32) * (q.shape[-1] ** -0.5)
    p = jax.nn.softmax(s, axis=-1).astype(v.dtype)
    return jnp.einsum('bhqk,bhkd->bhqd', p, v)


def mixer_block(a_lat, a_ctx, w_in, w_out, ret_decay_logit, ret_gn_g, conv_w, na_rpb, rope, with_ctx_out):
    rq, rk, rv, rg, cb, cc, cx, nq, nk, nv = split_in_proj(a_lat @ w_in)
    crq, crk, crv, crg, ccb, ccc, ccx, cnq, cnk, cnv = split_in_proj(a_ctx @ w_in)
    log_gamma = jax.nn.log_sigmoid(ret_decay_logit.astype(jnp.float32))
    k_scale = HEAD_DIM ** -0.5
    rk_c = split_heads(crk, N_RET_HEADS) * k_scale
    rv_c = split_heads(crv, N_RET_HEADS)
    s_fwd = retention_final_state(rk_c, rv_c, log_gamma[0])
    s_bwd = retention_final_state(flip_seq(rk_c), flip_seq(rv_c), log_gamma[1])
    nk_c = split_heads(cnk, N_NA_HEADS)
    nv_c = split_heads(cnv, N_NA_HEADS)
    rq_l = apply_axial_rope(split_heads(rq, N_RET_HEADS), rope)
    rk_l = apply_axial_rope(split_heads(rk, N_RET_HEADS), rope) * k_scale
    rv_l = split_heads(rv, N_RET_HEADS)
    y_ret = retention_output(bidir_retention(rq_l, rk_l, rv_l, log_gamma, s_fwd, s_bwd), rg, ret_gn_g)
    y_conv = short_gated_conv(cb, cc, cx, conv_w)
    y_na = merge_heads(neighbourhood_attention(split_heads(nq, N_NA_HEADS), split_heads(nk, N_NA_HEADS),
                                               split_heads(nv, N_NA_HEADS), nk_c, nv_c, na_rpb))
    y_lat = jnp.concatenate([y_ret, y_conv, y_na], axis=-1) @ w_out
    if not with_ctx_out:
        return y_lat, None
    zero = jnp.zeros_like(s_fwd)
    y_ret_c = retention_output(bidir_retention(split_heads(crq, N_RET_HEADS), rk_c, rv_c, log_gamma, zero, zero),
                               crg, ret_gn_g)
    y_conv_c = short_gated_conv(ccb, ccc, ccx, conv_w)
    y_na_c = merge_heads(context_attention(split_heads(cnq, N_NA_HEADS), nk_c, nv_c))
    y_ctx = jnp.concatenate([y_ret_c, y_conv_c, y_na_c], axis=-1) @ w_out
    return y_lat, y_ctx


def swiglu(h, w_in, w_out):
    gate, up = jnp.split(h @ w_in, 2, axis=-1)
    return (jax.nn.silu(gate) * up) @ w_out


def moe_swiglu(h, router_w, router_b, w_in, w_out):
    logits = (h @ router_w).astype(jnp.float32) + router_b.astype(jnp.float32)
    top_v, top_i = lax.top_k(logits, TOP_K)
    wts = jax.nn.softmax(top_v, axis=-1)
    gates = jnp.sum(jax.nn.one_hot(top_i, N_EXPERTS, dtype=jnp.float32) * wts[..., None], axis=-2)
    y = jnp.zeros_like(h)
    for e in range(N_EXPERTS):
        y = y + gates[..., e:e + 1].astype(h.dtype) * swiglu(h, w_in[e], w_out[e])
    return y


def setup_inputs(seed: int = 0) -> dict:
    key = jax.random.key(seed)
    ks = jax.random.split(key, 24)
    f32 = jnp.float32

    def nrm(k, shape, scale):
        return scale * jax.random.normal(k, shape, f32)

    base_logit = jnp.log(2.0 ** (5.0 + jnp.arange(N_RET_HEADS, dtype=f32)) - 1.0)
    return {
        'x': nrm(ks[0], (BATCH, SEQ, D_MODEL), 1.0),
        'c': nrm(ks[1], (BATCH, D_MODEL), 1.0),
        'ctx': nrm(ks[2], (BATCH, CTX_LEN, D_MODEL), 1.0),
        'c_ctx': nrm(ks[3], (D_MODEL,), 1.0),
        'ada_w': nrm(ks[4], (DEPTH, D_MODEL, 6 * D_MODEL), 0.5 * D_MODEL ** -0.5),
        'ada_b': nrm(ks[5], (DEPTH, 6 * D_MODEL), 0.02),
        'norm1_g': 1.0 + nrm(ks[6], (DEPTH, D_MODEL), 0.02),
        'norm2_g': 1.0 + nrm(ks[7], (DEPTH, D_MODEL), 0.02),
        'w_in': nrm(ks[8], (DEPTH, D_MODEL, D_IN_PROJ), D_MODEL ** -0.5),
        'w_out': nrm(ks[9], (DEPTH, D_MIX, D_MODEL), D_MIX ** -0.5),
        'ret_decay_logit': base_logit + nrm(ks[10], (DEPTH, 2, N_RET_HEADS), 0.1),
        'ret_gn_g': 1.0 + nrm(ks[11], (DEPTH, D_RET), 0.02),
        'conv_w': nrm(ks[12], (DEPTH, CONV_WIDTH, D_CONV), CONV_WIDTH ** -0.5),
        'na_rpb': nrm(ks[13], (DEPTH, N_NA_HEADS, 2 * NA_ROWS - 1, 2 * NA_COLS - 1), 0.1),
        'ffn_w_in': nrm(ks[14], (N_DENSE, D_MODEL, 2 * D_FF), D_MODEL ** -0.5),
        'ffn_w_out': nrm(ks[15], (N_DENSE, D_FF, D_MODEL), D_FF ** -0.5),
        'moe_router_w': nrm(ks[16], (N_MOE, D_MODEL, N_EXPERTS), D_MODEL ** -0.5),
        'moe_router_b': nrm(ks[17], (N_MOE, N_EXPERTS), 0.01),
        'moe_w_in': nrm(ks[18], (N_MOE, N_EXPERTS, D_MODEL, 2 * D_FF_EXPERT), D_MODEL ** -0.5),
        'moe_w_out': nrm(ks[19], (N_MOE, N_EXPERTS, D_FF_EXPERT, D_MODEL), D_FF_EXPERT ** -0.5),
        'final_g': 1.0 + nrm(ks[20], (D_MODEL,), 0.02),
    }


def reference(x, c, ctx, c_ctx, ada_w, ada_b, norm1_g, norm2_g, w_in, w_out, ret_decay_logit, ret_gn_g,
              conv_w, na_rpb, ffn_w_in, ffn_w_out, moe_router_w, moe_router_b, moe_w_in, moe_w_out, final_g):
    rope = axial_rope_tables(x.shape[1])
    c_act = jax.nn.silu(c)
    cctx_act = jax.nn.silu(c_ctx)
    h, hc = x, ctx
    for layer in range(DEPTH):
        update_ctx = layer < DEPTH - 1
        sh1, sc1, g1, sh2, sc2, g2 = jnp.split((c_act @ ada_w[layer] + ada_b[layer])[:, None, :], 6, axis=-1)
        csh1, csc1, cg1, csh2, csc2, cg2 = jnp.split(cctx_act @ ada_w[layer] + ada_b[layer], 6, axis=-1)
        a_lat = modulate(rms_norm(h, norm1_g[layer]), sh1, sc1)
        a_ctx = modulate(rms_norm(hc, norm1_g[layer]), csh1, csc1)
        y_lat, y_ctx = mixer_block(a_lat, a_ctx, w_in[layer], w_out[layer], ret_decay_logit[layer], ret_gn_g[layer],
                                   conv_w[layer], na_rpb[layer], rope, update_ctx)
        h = h + g1 * y_lat
        if layer % 2 == 0:
            ffn = functools.partial(swiglu, w_in=ffn_w_in[layer // 2], w_out=ffn_w_out[layer // 2])
        else:
            e = layer // 2
            ffn = functools.partial(moe_swiglu, router_w=moe_router_w[e], router_b=moe_router_b[e],
                                    w_in=moe_w_in[e], w_out=moe_w_out[e])
        h = h + g2 * ffn(modulate(rms_norm(h, norm2_g[layer]), sh2, sc2))
        if update_ctx:
            hc = hc + cg1 * y_ctx
            hc = hc + cg2 * ffn(modulate(rms_norm(hc, norm2_g[layer]), csh2, csc2))
    return rms_norm(h, final_g)
```

```python
import functools

import numpy as np
import jax
import jax.numpy as jnp
from jax import lax
from jax.experimental import pallas as pl
from jax.experimental.pallas import tpu as pltpu

F32 = jnp.float32
BF16 = jnp.bfloat16

LANES = 128
HEAD_DIM = 64
GRID_W = 64
N_CONV_GROUPS = 4
N_RET_HEADS = 6
N_NA_HEADS = 6
D_RET = N_RET_HEADS * HEAD_DIM
D_CONV = N_CONV_GROUPS * HEAD_DIM
D_NA = N_NA_HEADS * HEAD_DIM
D_IN_PROJ = 4 * D_RET + 3 * D_CONV + 3 * D_NA
N_PAIRS = D_RET // LANES
NA_ROWS = 8
NA_COLS = 16
N_EXPERTS = 8
ROPE_BASE = 10000.0
NORM_EPS = 1e-6
NEG_BIG = -1e30

RET_CHUNK = 256
NA_QROWS = 4
NA_KROWS = 12
MOE_TS = 512
VMEM_LIMIT = 56 * 1024 * 1024

CB_RQ, CB_RK, CB_RV, CB_RG = 0, 3, 6, 9
CB_CB, CB_CC, CB_CX = 12, 14, 16
CB_NQ, CB_NK, CB_NV = 18, 21, 24


def _cp(sem, vmem=VMEM_LIMIT):
    return pltpu.CompilerParams(dimension_semantics=sem, vmem_limit_bytes=vmem)


def _silu(x):
    return x * (1.0 / (1.0 + jnp.exp(-x)))


def _dot(a, b):
    return jnp.dot(a, b, preferred_element_type=F32)


def _dot_nt(a, b):
    return lax.dot_general(a, b, (((1,), (1,)), ((), ())), preferred_element_type=F32)


def _dot_tn(a, b):
    return lax.dot_general(a, b, (((0,), (0,)), ((), ())), preferred_element_type=F32)


def _split_bf16(x):
    hi = x.astype(BF16)
    lo = (x - hi.astype(F32)).astype(BF16)
    return hi, lo


def _ada_kernel(c_ref, w_ref, b_ref, o_ref):
    x = _silu(c_ref[...]).astype(BF16)
    o_ref[0] = _dot(x, w_ref[0].astype(BF16)) + b_ref[0]


def _ada(c8, ada_w, ada_b):
    depth, d, n = ada_w.shape
    tn = n // 4
    return pl.pallas_call(
        _ada_kernel,
        grid=(depth, n // tn),
        in_specs=[pl.BlockSpec((8, d), lambda l, j: (0, 0)),
                  pl.BlockSpec((1, d, tn), lambda l, j: (l, 0, j)),
                  pl.BlockSpec((1, 1, tn), lambda l, j: (l, 0, j))],
        out_specs=pl.BlockSpec((1, 8, tn), lambda l, j: (l, 0, j)),
        out_shape=jax.ShapeDtypeStruct((depth, 8, n), F32),
        compiler_params=_cp(("parallel", "parallel")),
        name="ada_mod",
    )(c8, ada_w, ada_b.reshape(depth, 1, n))


def _norm_mod(x, g, sh, sc):
    ms = jnp.mean(x * x, axis=-1, keepdims=True)
    y = x * lax.rsqrt(ms + NORM_EPS) * g
    return y * (1.0 + sc) + sh


def _inproj_kernel(*refs, rope):
    if rope:
        h_ref, g_ref, sh_ref, sc_ref, w_ref, cos_ref, sa_ref, sb_ref, o_ref, xn_ref = refs
    else:
        h_ref, g_ref, sh_ref, sc_ref, w_ref, o_ref, xn_ref = refs
    xn_ref[...] = _norm_mod(h_ref[...], g_ref[...], sh_ref[0], sc_ref[0]).astype(BF16)
    cw = 3 * LANES
    for c in range(D_IN_PROJ // cw):
        acc = _dot(xn_ref[...], w_ref[:, c * cw:(c + 1) * cw])
        for j in range(3):
            blk = acc[:, j * LANES:(j + 1) * LANES]
            cb = 3 * c + j
            if rope and cb < CB_RV:
                blk = (blk * cos_ref[...] + pltpu.roll(blk, 16, 1) * sa_ref[...]
                       + pltpu.roll(blk, LANES - 16, 1) * sb_ref[...])
            if CB_RK <= cb < CB_RV or CB_NQ <= cb < CB_NK:
                blk = blk * (HEAD_DIM ** -0.5)
            o_ref[:, cb * LANES:(cb + 1) * LANES] = blk.astype(BF16)


def _in_proj(h2, g, sh, sc, w_bf, tables, seq, tm):
    t, d = h2.shape
    tiles_per_seq = seq // tm
    rope = tables is not None
    in_specs = [pl.BlockSpec((tm, d), lambda i: (i, 0)),
                pl.BlockSpec((1, d), lambda i: (0, 0)),
                pl.BlockSpec((1, 1, d), lambda i: (i // tiles_per_seq, 0, 0)),
                pl.BlockSpec((1, 1, d), lambda i: (i // tiles_per_seq, 0, 0)),
                pl.BlockSpec((d, D_IN_PROJ), lambda i: (0, 0))]
    args = [h2, g.reshape(1, d), sh, sc, w_bf]
    if rope:
        in_specs += [pl.BlockSpec((tm, LANES), lambda i: (i % tiles_per_seq, 0))] * 3
        args += list(tables)
    return pl.pallas_call(
        functools.partial(_inproj_kernel, rope=rope),
        grid=(t // tm,),
        in_specs=in_specs,
        out_specs=pl.BlockSpec((tm, D_IN_PROJ), lambda i: (i, 0)),
        out_shape=jax.ShapeDtypeStruct((t, D_IN_PROJ), BF16),
        scratch_shapes=[pltpu.VMEM((tm, d), BF16)],
        compiler_params=_cp(("parallel",)),
        name="in_proj_rope" if rope else "in_proj_ctx",
    )(*args)


def _rope_tables(seq):
    t = np.arange(seq)
    row = (t // GRID_W).astype(np.float32)
    col = (t % GRID_W).astype(np.float32)
    n_freq = HEAD_DIM // 4
    inv_freq = (ROPE_BASE ** (-np.arange(n_freq, dtype=np.float32) / n_freq)).astype(np.float32)
    ang_r = row[:, None] * inv_freq
    ang_c = col[:, None] * inv_freq
    cos_h = np.concatenate([np.cos(ang_r), np.cos(ang_r), np.cos(ang_c), np.cos(ang_c)], axis=1)
    sin_h = np.concatenate([np.sin(ang_r), np.sin(ang_r), np.sin(ang_c), np.sin(ang_c)], axis=1)
    lane = np.arange(HEAD_DIM)
    second = (lane % 32) >= 16
    sa = np.where(second[None, :], sin_h, 0.0)
    sb = np.where(second[None, :], 0.0, -sin_h)
    tile2 = lambda a: jnp.asarray(np.concatenate([a, a], axis=1), F32)
    return tile2(cos_h), tile2(sa), tile2(sb)


def _ret_kernel(q_ref, k_ref, v_ref, g_ref, lgt_ref, gn_ref, sf0_ref, sb0_ref,
                y_ref, sfo_ref, sbo_ref, sbs_ref, *, seq):
    c = RET_CHUNK
    n_chunks = seq // c
    lg = jax.nn.log_sigmoid(lgt_ref[0])
    lgf, lgb = lg[0:1, :], lg[1:2, :]
    pos = lax.broadcasted_iota(jnp.int32, (c, 1), 0).astype(F32)
    dkf = jnp.exp(lgf * (c - 1.0 - pos))
    dkb = jnp.exp(lgb * pos)
    dqf = jnp.exp(lgf * (pos + 1.0))
    dqb = jnp.exp(lgb * (c - pos))
    cdf = jnp.exp(lgf * float(c))
    cdb = jnp.exp(lgb * float(c))
    lane = lax.broadcasted_iota(jnp.int32, (1, LANES), 1)
    first = lane < HEAD_DIM
    ri = lax.broadcasted_iota(jnp.int32, (LANES, LANES), 0)
    ci = lax.broadcasted_iota(jnp.int32, (LANES, LANES), 1)
    same = (ri < HEAD_DIM) == (ci < HEAD_DIM)
    bd = same.astype(F32)
    avg = (bd * (1.0 / HEAD_DIM)).astype(BF16)
    ii = lax.broadcasted_iota(jnp.int32, (c, c), 0)
    jj = lax.broadcasted_iota(jnp.int32, (c, c), 1)
    dif = (ii - jj).astype(F32)

    def decay_mask(h0):
        lf = lgf[:, h0:h0 + 1]
        lb = lgb[:, h0:h0 + 1]
        return jnp.where(dif > 0, jnp.exp(lf * jnp.maximum(dif, 0.0)),
                         jnp.where(dif < 0, jnp.exp(lb * jnp.maximum(-dif, 0.0)), 2.0))

    dm = (decay_mask(0), decay_mask(HEAD_DIM))
    mfirst = first.astype(BF16)
    msecond = (1.0 - first.astype(F32)).astype(BF16)

    def back(i, s_b):
        n = n_chunks - 1 - i
        sl = pl.ds(pl.multiple_of(n * c, c), c)
        sbs_ref[n] = s_b
        k = k_ref[0, sl, :].astype(F32)
        u = _dot_tn((k * dkb).astype(BF16), v_ref[0, sl, :]) * bd
        return cdb * s_b + u

    s_b = lax.fori_loop(0, n_chunks, back, sb0_ref[0, 0])
    sbo_ref[0, 0] = s_b

    def fwd(n, s_f):
        sl = pl.ds(pl.multiple_of(n * c, c), c)
        q = q_ref[0, sl, :]
        k = k_ref[0, sl, :]
        v = v_ref[0, sl, :]
        qf = q.astype(F32)
        kf = k.astype(F32)
        o = None
        for hh, mk in enumerate((mfirst, msecond)):
            s = _dot_nt(q * mk, k) * dm[hh]
            oh = _dot(s.astype(BF16), v)
            o = oh if o is None else jnp.where(first, o, oh)
        o = o + _dot((qf * dqf).astype(BF16), s_f.astype(BF16))
        o = o + _dot((qf * dqb).astype(BF16), sbs_ref[n].astype(BF16))
        s_f = cdf * s_f + _dot_tn((kf * dkf).astype(BF16), v) * bd
        hi, lo = _split_bf16(o)
        mu = _dot(hi, avg) + _dot(lo, avg)
        dlt = o - mu
        hi, lo = _split_bf16(dlt * dlt)
        var = _dot(hi, avg) + _dot(lo, avg)
        on = dlt * lax.rsqrt(var + NORM_EPS) * gn_ref[0]
        y_ref[0, sl, :] = (_silu(g_ref[0, sl, :].astype(F32)) * on).astype(BF16)
        return s_f

    s_f = lax.fori_loop(0, n_chunks, fwd, sf0_ref[0, 0])
    sfo_ref[0, 0] = s_f


def _retention(proj, lgt, gn, sf0, sb0):
    b, seq, _ = proj.shape
    col = lambda off: pl.BlockSpec((1, seq, LANES), lambda bi, p: (bi, 0, off + p))
    st = pl.BlockSpec((1, 1, LANES, LANES), lambda bi, p: (bi, p, 0, 0))
    return pl.pallas_call(
        functools.partial(_ret_kernel, seq=seq),
        grid=(b, N_PAIRS),
        in_specs=[col(CB_RQ), col(CB_RK), col(CB_RV), col(CB_RG),
                  pl.BlockSpec((1, 2, LANES), lambda bi, p: (p, 0, 0)),
                  pl.BlockSpec((1, 1, LANES), lambda bi, p: (p, 0, 0)),
                  st, st],
        out_specs=[pl.BlockSpec((1, seq, LANES), lambda bi, p: (bi, 0, p)), st, st],
        out_shape=[jax.ShapeDtypeStruct((b, seq, D_RET), BF16),
                   jax.ShapeDtypeStruct((b, N_PAIRS, LANES, LANES), F32),
                   jax.ShapeDtypeStruct((b, N_PAIRS, LANES, LANES), F32)],
        scratch_shapes=[pltpu.VMEM((seq // RET_CHUNK, LANES, LANES), F32)],
        compiler_params=_cp(("parallel", "parallel")),
        name="retention_s%d" % seq,
    )(proj, proj, proj, proj, lgt, gn, sf0, sb0)


def _conv_kernel(b_ref, c_ref, x_ref, w_ref, y_ref, *, seq):
    u = c_ref[0].astype(F32) * x_ref[0].astype(F32)
    row = lax.broadcasted_iota(jnp.int32, (seq, 1), 0)
    prev = jnp.where(row == 0, 0.0, pltpu.roll(u, 1, 0))
    nxt = jnp.where(row == seq - 1, 0.0, pltpu.roll(u, seq - 1, 0))
    y = w_ref[0:1, :] * prev + w_ref[1:2, :] * u + w_ref[2:3, :] * nxt
    y_ref[0] = (b_ref[0].astype(F32) * y).astype(BF16)


def _short_conv(proj, conv_w):
    b, seq, _ = proj.shape
    nblk = D_CONV // LANES
    col = lambda off: pl.BlockSpec((1, seq, LANES), lambda bi, j: (bi, 0, off + j))
    return pl.pallas_call(
        functools.partial(_conv_kernel, seq=seq),
        grid=(b, nblk),
        in_specs=[col(CB_CB), col(CB_CC), col(CB_CX),
                  pl.BlockSpec((3, LANES), lambda bi, j: (0, j))],
        out_specs=pl.BlockSpec((1, seq, LANES), lambda bi, j: (bi, 0, j)),
        out_shape=jax.ShapeDtypeStruct((b, seq, D_CONV), BF16),
        compiler_params=_cp(("parallel", "parallel")),
        name="short_conv_s%d" % seq,
    )(proj, proj, proj, conv_w)


def _rpb_kernel(r_ref, oh_ref, o_ref):
    hi, lo = _split_bf16(r_ref[...])
    o_ref[...] = _dot(hi, oh_ref[...]) + _dot(lo, oh_ref[...])


def _na_bias(rpb, rows):
    h = rpb.shape[0]
    ndr, ndc = 2 * NA_ROWS - 1, 2 * NA_COLS - 1
    w = GRID_W
    cidx = np.arange(w)
    dc = np.clip(cidx[None, :] - cidx[:, None], -(NA_COLS - 1), NA_COLS - 1) + (NA_COLS - 1)
    onehot = np.zeros((LANES, w * w), np.float32)
    onehot[dc.reshape(-1), np.arange(w * w)] = 1.0
    r2 = jnp.zeros((96, LANES), F32).at[:h * ndr, :ndc].set(rpb.reshape(h * ndr, ndc))
    t = pl.pallas_call(
        _rpb_kernel,
        out_shape=jax.ShapeDtypeStruct((96, w * w), F32),
        name="na_rpb_expand",
    )(r2, jnp.asarray(onehot, BF16))
    t = t[:h * ndr].reshape(h, ndr, w, w)
    c0 = np.clip(cidx - NA_COLS // 2, 0, w - NA_COLS)
    col_in = (cidx[None, :] >= c0[:, None]) & (cidx[None, :] < c0[:, None] + NA_COLS)
    t = jnp.where(jnp.asarray(col_in)[None, None], t, NEG_BIG)
    dr_idx = np.zeros((3, NA_QROWS, NA_KROWS), np.int32)
    valid = np.zeros((3, NA_QROWS, NA_KROWS), bool)
    for cl, rb in enumerate((0, NA_QROWS, rows - NA_QROWS)):
        ws = int(np.clip(rb - NA_ROWS // 2, 0, rows - NA_KROWS))
        for i in range(NA_QROWS):
            r = rb + i
            r0 = int(np.clip(r - NA_ROWS // 2, 0, rows - NA_ROWS))
            for j in range(NA_KROWS):
                kr = ws + j
                if r0 <= kr < r0 + NA_ROWS:
                    valid[cl, i, j] = True
                    dr_idx[cl, i, j] = kr - r + (NA_ROWS - 1)
    big = t[:, dr_idx]
    big = jnp.where(jnp.asarray(valid)[None, :, :, :, None, None], big, NEG_BIG)
    big = big.transpose(0, 1, 2, 4, 3, 5).reshape(h, 3, NA_QROWS * w, NA_KROWS * w)
    return big


def _na_kernel(q_ref, k_ref, v_ref, kc_ref, vc_ref, bias_ref, o_ref, *, rows):
    qb = pl.program_id(2)
    ws = jnp.clip(qb * NA_QROWS - NA_ROWS // 2, 0, rows - NA_KROWS)
    sl = pl.ds(pl.multiple_of(ws * GRID_W, GRID_W), NA_KROWS * GRID_W)
    q = q_ref[0]
    kw = k_ref[0, sl, :]
    vw = v_ref[0, sl, :]
    kc = kc_ref[0]
    vc = vc_ref[0]
    lane = lax.broadcasted_iota(jnp.int32, (1, LANES), 1)
    first = lane < HEAD_DIM
    o = None
    for hh in range(2):
        mk = (first if hh == 0 else jnp.logical_not(first)).astype(BF16)
        qh = q * mk
        s_loc = _dot_nt(qh, kw) + bias_ref[hh, 0]
        s_ctx = _dot_nt(qh, kc)
        m = jnp.maximum(jnp.max(s_loc, axis=-1, keepdims=True), jnp.max(s_ctx, axis=-1, keepdims=True))
        p_loc = jnp.exp(s_loc - m)
        p_ctx = jnp.exp(s_ctx - m)
        l = jnp.sum(p_loc, axis=-1, keepdims=True) + jnp.sum(p_ctx, axis=-1, keepdims=True)
        oh = (_dot(p_loc.astype(BF16), vw) + _dot(p_ctx.astype(BF16), vc)) / l
        o = oh if o is None else jnp.where(first, o, oh)
    o_ref[0] = o.astype(BF16)


def _na(proj, proj_ctx, bias):
    b, seq, _ = proj.shape
    ctx_len = proj_ctx.shape[1]
    rows = seq // GRID_W
    nqb = rows // NA_QROWS
    tq = NA_QROWS * GRID_W

    def cls(qb):
        return jnp.where(qb == 0, 0, jnp.where(qb == nqb - 1, 2, 1))

    return pl.pallas_call(
        functools.partial(_na_kernel, rows=rows),
        grid=(b, N_PAIRS, nqb),
        in_specs=[pl.BlockSpec((1, tq, LANES), lambda bi, p, qb: (bi, qb, CB_NQ + p)),
                  pl.BlockSpec((1, seq, LANES), lambda bi, p, qb: (bi, 0, CB_NK + p)),
                  pl.BlockSpec((1, seq, LANES), lambda bi, p, qb: (bi, 0, CB_NV + p)),
                  pl.BlockSpec((1, ctx_len, LANES), lambda bi, p, qb: (bi, 0, CB_NK + p)),
                  pl.BlockSpec((1, ctx_len, LANES), lambda bi, p, qb: (bi, 0, CB_NV + p)),
                  pl.BlockSpec((2, 1, tq, NA_KROWS * GRID_W), lambda bi, p, qb: (p, cls(qb), 0, 0))],
        out_specs=pl.BlockSpec((1, tq, LANES), lambda bi, p, qb: (bi, qb, p)),
        out_shape=jax.ShapeDtypeStruct((b, seq, D_NA), BF16),
        compiler_params=_cp(("parallel", "parallel", "arbitrary")),
        name="na_attn",
    )(proj, proj, proj, proj_ctx, proj_ctx, bias)


def _ctx_attn_kernel(q_ref, k_ref, v_ref, o_ref):
    q = q_ref[0]
    k = k_ref[0]
    v = v_ref[0]
    lane = lax.broadcasted_iota(jnp.int32, (1, LANES), 1)
    first = lane < HEAD_DIM
    o = None
    for hh in range(2):
        mk = (first if hh == 0 else jnp.logical_not(first)).astype(BF16)
        s = _dot_nt(q * mk, k)
        m = jnp.max(s, axis=-1, keepdims=True)
        p = jnp.exp(s - m)
        l = jnp.sum(p, axis=-1, keepdims=True)
        oh = _dot(p.astype(BF16), v) / l
        o = oh if o is None else jnp.where(first, o, oh)
    o_ref[0] = o.astype(BF16)


def _ctx_attn(proj_ctx):
    b, ctx_len, _ = proj_ctx.shape
    col = lambda off: pl.BlockSpec((1, ctx_len, LANES), lambda bi, p: (bi, 0, off + p))
    return pl.pallas_call(
        _ctx_attn_kernel,
        grid=(b, N_PAIRS),
        in_specs=[col(CB_NQ), col(CB_NK), col(CB_NV)],
        out_specs=pl.BlockSpec((1, ctx_len, LANES), lambda bi, p: (bi, 0, p)),
        out_shape=jax.ShapeDtypeStruct((b, ctx_len, D_NA), BF16),
        compiler_params=_cp(("parallel", "parallel")),
        name="ctx_attn",
    )(proj_ctx, proj_ctx, proj_ctx)


def _outproj_kernel(*refs, route):
    if route:
        (yr_ref, yc_ref, yn_ref, h_ref, w_ref, g1_ref, n2_ref, sh_ref, sc_ref, rw_ref, rb_ref,
         ho_ref, xn_ref, idx_ref, wt_ref) = refs
    else:
        yr_ref, yc_ref, yn_ref, h_ref, w_ref, g1_ref, n2_ref, sh_ref, sc_ref, ho_ref, xn_ref = refs
    ycat = jnp.concatenate([yr_ref[...], yc_ref[...], yn_ref[...]], axis=-1)
    h = h_ref[...] + g1_ref[0] * _dot(ycat, w_ref[...])
    ho_ref[...] = h
    a = _norm_mod(h, n2_ref[...], sh_ref[0], sc_ref[0])
    xn_ref[...] = a.astype(BF16)
    if route:
        a_hi, a_lo = _split_bf16(a)
        r_hi, r_lo = _split_bf16(rw_ref[...])
        logits = _dot_nt(r_hi, a_hi) + _dot_nt(r_hi, a_lo) + _dot_nt(r_lo, a_hi) + rb_ref[...]
        eidx = lax.broadcasted_iota(jnp.int32, logits.shape, 0)
        m1 = jnp.max(logits, axis=0, keepdims=True)
        i1 = jnp.min(jnp.where(logits == m1, eidx, N_EXPERTS), axis=0, keepdims=True)
        rest = jnp.where(eidx == i1, -jnp.inf, logits)
        m2 = jnp.max(rest, axis=0, keepdims=True)
        i2 = jnp.min(jnp.where(rest == m2, eidx, N_EXPERTS), axis=0, keepdims=True)
        e2 = jnp.exp(m2 - m1)
        den = 1.0 + e2
        idx_ref[...] = jnp.concatenate([i1, i2], axis=0)
        wt_ref[...] = jnp.concatenate([1.0 / den, e2 / den], axis=0)


def _out_proj(yr, yc, yn, h2, w_bf, g1, n2g, sh2, sc2, seq, tm, router=None):
    t, d = h2.shape
    tps = seq // tm
    route = router is not None
    row = lambda wdt: pl.BlockSpec((tm, wdt), lambda i: (i, 0))
    mod = pl.BlockSpec((1, 1, d), lambda i: (i // tps, 0, 0))
    in_specs = [row(D_RET), row(D_CONV), row(D_NA), row(d),
                pl.BlockSpec((d, d), lambda i: (0, 0)), mod,
                pl.BlockSpec((1, d), lambda i: (0, 0)), mod, mod]
    args = [yr, yc, yn, h2, w_bf, g1, n2g.reshape(1, d), sh2, sc2]
    out_specs = [row(d), row(d)]
    out_shape = [jax.ShapeDtypeStruct((t, d), F32), jax.ShapeDtypeStruct((t, d), BF16)]
    if route:
        rw, rb = router
        in_specs += [pl.BlockSpec((N_EXPERTS, d), lambda i: (0, 0)),
                     pl.BlockSpec((N_EXPERTS, 1), lambda i: (0, 0))]
        args += [rw.T, rb.reshape(N_EXPERTS, 1)]
        out_specs += [pl.BlockSpec((2, tm), lambda i: (0, i))] * 2
        out_shape += [jax.ShapeDtypeStruct((2, t), jnp.int32), jax.ShapeDtypeStruct((2, t), F32)]
    return pl.pallas_call(
        functools.partial(_outproj_kernel, route=route),
        grid=(t // tm,),
        in_specs=in_specs,
        out_specs=out_specs,
        out_shape=out_shape,
        compiler_params=_cp(("parallel",)),
        name="out_proj_route" if route else "out_proj_t%d" % t,
    )(*args)


def _ffn_kernel(x_ref, h_ref, wg_ref, wu_ref, wo_ref, g2_ref, o_ref, acc_ref):
    f = pl.program_id(1)

    @pl.when(f == 0)
    def _():
        acc_ref[...] = jnp.zeros_like(acc_ref)

    x = x_ref[...]
    a = _silu(_dot(x, wg_ref[...])) * _dot(x, wu_ref[...])
    acc_ref[...] += _dot(a.astype(BF16), wo_ref[...])

    @pl.when(f == pl.num_programs(1) - 1)
    def _():
        o_ref[...] = h_ref[...] + g2_ref[0] * acc_ref[...]


def _ffn(xn, h2, w_in_bf, w_out_bf, g2, seq, tm, tf):
    t, d = h2.shape
    ff = w_out_bf.shape[0]
    nf = ff // tf
    tps = seq // tm
    return pl.pallas_call(
        _ffn_kernel,
        grid=(t // tm, nf),
        in_specs=[pl.BlockSpec((tm, d), lambda i, f: (i, 0)),
                  pl.BlockSpec((tm, d), lambda i, f: (i, 0)),
                  pl.BlockSpec((d, tf), lambda i, f: (0, f)),
                  pl.BlockSpec((d, tf), lambda i, f: (0, nf + f)),
                  pl.BlockSpec((tf, d), lambda i, f: (f, 0)),
                  pl.BlockSpec((1, 1, d), lambda i, f: (i // tps, 0, 0))],
        out_specs=pl.BlockSpec((tm, d), lambda i, f: (i, 0)),
        out_shape=jax.ShapeDtypeStruct((t, d), F32),
        scratch_shapes=[pltpu.VMEM((tm, d), F32)],
        compiler_params=_cp(("parallel", "arbitrary")),
        name="ffn_t%d" % t,
    )(xn, h2, w_in_bf, w_in_bf, w_out_bf, g2)


def _gather_kernel(idx_ref, src_ref, o_ref, sem, *, rows):
    def issue(r, carry):
        pltpu.make_async_copy(src_ref.at[idx_ref[r]], o_ref.at[r], sem).start()
        return carry

    lax.fori_loop(0, rows, issue, 0)
    pltpu.make_async_copy(src_ref.at[pl.ds(0, rows)], o_ref, sem).wait()


def _gather_rows(src3, idx, rows):
    n = idx.shape[0]
    return pl.pallas_call(
        functools.partial(_gather_kernel, rows=rows),
        grid=(n // rows,),
        in_specs=[pl.BlockSpec((rows,), lambda i: (i,), memory_space=pltpu.SMEM),
                  pl.BlockSpec(memory_space=pl.ANY)],
        out_specs=pl.BlockSpec((rows,) + src3.shape[1:], lambda i: (i, 0, 0)),
        out_shape=jax.ShapeDtypeStruct((n,) + src3.shape[1:], src3.dtype),
        scratch_shapes=[pltpu.SemaphoreType.DMA(())],
        compiler_params=_cp(("arbitrary",)),
        name="moe_gather",
    )(idx, src3)


def _expert_kernel(be_ref, nu_ref, x_ref, ws_ref, wg_ref, wu_ref, wo_ref, o_ref, acc_ref):
    j = pl.program_id(0)
    f = pl.program_id(1)
    last = pl.num_programs(1) - 1
    used = j < nu_ref[0]

    @pl.when(jnp.logical_and(used, f == 0))
    def _():
        acc_ref[...] = jnp.zeros_like(acc_ref)

    @pl.when(used)
    def _():
        x = x_ref[...]
        a = _silu(_dot(x, wg_ref[0])) * _dot(x, wu_ref[0])
        acc_ref[...] += _dot(a.astype(BF16), wo_ref[0])

    @pl.when(jnp.logical_and(used, f == last))
    def _():
        o_ref[...] = ws_ref[...] * acc_ref[...]

    @pl.when(jnp.logical_and(jnp.logical_not(used), f == last))
    def _():
        o_ref[...] = jnp.zeros_like(o_ref)


def _experts(xs, wslot, block_expert, n_used, w_in_bf, w_out_bf, tf):
    n, d = xs.shape
    ff = w_out_bf.shape[1]
    nf = ff // tf
    ts = MOE_TS

    def fe(j, f):
        return jnp.where(j % 2 == 0, f, nf - 1 - f)

    grid_spec = pltpu.PrefetchScalarGridSpec(
        num_scalar_prefetch=2,
        grid=(n // ts, nf),
        in_specs=[pl.BlockSpec((ts, d), lambda j, f, be, nu: (j, 0)),
                  pl.BlockSpec((ts, 1), lambda j, f, be, nu: (j, 0)),
                  pl.BlockSpec((1, d, tf), lambda j, f, be, nu: (be[j], 0, fe(j, f))),
                  pl.BlockSpec((1, d, tf), lambda j, f, be, nu: (be[j], 0, nf + fe(j, f))),
                  pl.BlockSpec((1, tf, d), lambda j, f, be, nu: (be[j], fe(j, f), 0))],
        out_specs=pl.BlockSpec((ts, d), lambda j, f, be, nu: (j, 0)),
        scratch_shapes=[pltpu.VMEM((ts, d), F32)],
    )
    return pl.pallas_call(
        _expert_kernel,
        grid_spec=grid_spec,
        out_shape=jax.ShapeDtypeStruct((n, d), F32),
        compiler_params=_cp(("arbitrary", "arbitrary")),
        name="moe_experts",
    )(block_expert, n_used, xs, wslot, w_in_bf, w_in_bf, w_out_bf)


def _combine_kernel(p0_ref, p1_ref, ys_ref, h_ref, g2_ref, fg_ref, o_ref, y0_ref, y1_ref, sem, *, rows):
    def issue(r, carry):
        pltpu.make_async_copy(ys_ref.at[p0_ref[r]], y0_ref.at[r], sem.at[0]).start()
        pltpu.make_async_copy(ys_ref.at[p1_ref[r]], y1_ref.at[r], sem.at[1]).start()
        return carry

    lax.fori_loop(0, rows, issue, 0)
    pltpu.make_async_copy(ys_ref.at[pl.ds(0, rows)], y0_ref, sem.at[0]).wait()
    pltpu.make_async_copy(ys_ref.at[pl.ds(0, rows)], y1_ref, sem.at[1]).wait()
    h = h_ref[...] + g2_ref[0] * (y0_ref[...] + y1_ref[...])
    ms = jnp.sum(jnp.sum(h * h, axis=2, keepdims=True), axis=1, keepdims=True) * (1.0 / (8 * LANES))
    o_ref[...] = h * lax.rsqrt(ms + NORM_EPS) * fg_ref[...]


def _combine(pos0, pos1, ys3, h3, g2, final_g, seq, rows):
    t = h3.shape[0]
    tps = seq // rows
    blk = pl.BlockSpec((rows, 8, LANES), lambda i: (i, 0, 0))
    return pl.pallas_call(
        functools.partial(_combine_kernel, rows=rows),
        grid=(t // rows,),
        in_specs=[pl.BlockSpec((rows,), lambda i: (i,), memory_space=pltpu.SMEM),
                  pl.BlockSpec((rows,), lambda i: (i,), memory_space=pltpu.SMEM),
                  pl.BlockSpec(memory_space=pl.ANY),
                  blk,
                  pl.BlockSpec((1, 8, LANES), lambda i: (i // tps, 0, 0)),
                  pl.BlockSpec((1, 8, LANES), lambda i: (0, 0, 0))],
        out_specs=blk,
        out_shape=jax.ShapeDtypeStruct(h3.shape, F32),
        scratch_shapes=[pltpu.VMEM((rows, 8, LANES), F32), pltpu.VMEM((rows, 8, LANES), F32),
                        pltpu.SemaphoreType.DMA((2,))],
        compiler_params=_cp(("arbitrary",)),
        name="moe_combine_norm",
    )(pos0, pos1, ys3, h3, g2, final_g.reshape(1, 8, LANES))


def _route_plan(idx, wts):
    t = idx.shape[1]
    ts = MOE_TS
    n_slots = 2 * t + N_EXPERTS * ts
    e_flat = idx.reshape(-1)
    onehot = (e_flat[:, None] == jnp.arange(N_EXPERTS, dtype=jnp.int32)[None, :]).astype(jnp.int32)
    csum = jnp.cumsum(onehot, axis=0)
    rank = jnp.sum((csum - onehot) * onehot, axis=1)
    counts = csum[-1]
    padded = ((counts + ts - 1) // ts) * ts
    ends = jnp.cumsum(padded)
    offs = ends - padded
    slot = (jnp.sum(onehot * offs[None, :], axis=1) + rank).astype(jnp.int32)
    tok = jnp.tile(jnp.arange(t, dtype=jnp.int32), 2)
    tok_of_slot = jnp.zeros((n_slots,), jnp.int32).at[slot].set(tok)
    w_of_slot = jnp.zeros((n_slots,), F32).at[slot].set(wts.reshape(-1))
    starts = jnp.arange(n_slots // ts, dtype=jnp.int32) * ts
    block_expert = jnp.minimum(jnp.sum((starts[:, None] >= ends[None, :]).astype(jnp.int32), axis=1),
                               N_EXPERTS - 1).astype(jnp.int32)
    n_used = (ends[-1] // ts).astype(jnp.int32).reshape(1)
    return slot[:t], slot[t:], tok_of_slot, w_of_slot.reshape(n_slots, 1), block_expert, n_used


def kernel(x, c, ctx, c_ctx, ada_w, ada_b, norm1_g, norm2_g, w_in, w_out, ret_decay_logit, ret_gn_g,
           conv_w, na_rpb, ffn_w_in, ffn_w_out, moe_router_w, moe_router_b, moe_w_in, moe_w_out, final_g):
    b, seq, d = x.shape
    ctx_len = ctx.shape[1]
    depth = ada_w.shape[0]
    assert depth == 2, "the final norm is fused into the last (MoE) layer's combine step"
    rows = seq // GRID_W
    t_lat, t_ctx = b * seq, b * ctx_len
    tm_lat = min(1024, seq)
    tm_ctx = ctx_len
    ff_tile = ffn_w_out.shape[1] // 2

    c8 = jnp.zeros((8, d), F32).at[:b].set(c).at[b].set(c_ctx)
    mods = _ada(c8, ada_w, ada_b)
    tables = _rope_tables(seq)

    h = x.reshape(t_lat, d)
    hc = ctx.reshape(t_ctx, d)
    zero_state = jnp.zeros((b, N_PAIRS, LANES, LANES), F32)
    out = None
    for layer in range(depth):
        update_ctx = layer < depth - 1
        m = mods[layer].reshape(8, 6, d)
        lat = [m[:b, i].reshape(b, 1, d) for i in range(6)]
        cx = [jnp.broadcast_to(m[b, i].reshape(1, 1, d), (b, 1, d)) for i in range(6)]
        w_in_bf = w_in[layer].astype(BF16)
        w_out_bf = w_out[layer].astype(BF16)
        lgt = ret_decay_logit[layer].reshape(2, N_PAIRS, LANES // HEAD_DIM)
        lgt = jnp.repeat(lgt, HEAD_DIM, axis=2).transpose(1, 0, 2)
        gn = ret_gn_g[layer].reshape(N_PAIRS, 1, LANES)

        proj_c = _in_proj(hc, norm1_g[layer], cx[0], cx[1], w_in_bf, None, ctx_len, tm_ctx)
        proj_c = proj_c.reshape(b, ctx_len, D_IN_PROJ)
        proj = _in_proj(h, norm1_g[layer], lat[0], lat[1], w_in_bf, tables, seq, tm_lat)
        proj = proj.reshape(b, seq, D_IN_PROJ)

        y_ret_c, s_fwd, s_bwd = _retention(proj_c, lgt, gn, zero_state, zero_state)
        y_ret, _, _ = _retention(proj, lgt, gn, s_fwd, s_bwd)
        y_conv = _short_conv(proj, conv_w[layer])
        y_na = _na(proj, proj_c, _na_bias(na_rpb[layer], rows))

        if layer % 2 == 0:
            h, xn = _out_proj(y_ret.reshape(t_lat, D_RET), y_conv.reshape(t_lat, D_CONV),
                              y_na.reshape(t_lat, D_NA), h, w_out_bf, lat[2], norm2_g[layer],
                              lat[3], lat[4], seq, tm_lat)
            fw_in = ffn_w_in[layer // 2].astype(BF16)
            fw_out = ffn_w_out[layer // 2].astype(BF16)
            h = _ffn(xn, h, fw_in, fw_out, lat[5], seq, min(512, seq), ff_tile)
            if update_ctx:
                y_conv_c = _short_conv(proj_c, conv_w[layer])
                y_na_c = _ctx_attn(proj_c)
                hc, xnc = _out_proj(y_ret_c.reshape(t_ctx, D_RET), y_conv_c.reshape(t_ctx, D_CONV),
                                    y_na_c.reshape(t_ctx, D_NA), hc, w_out_bf, cx[2], norm2_g[layer],
                                    cx[3], cx[4], ctx_len, tm_ctx)
                hc = _ffn(xnc, hc, fw_in, fw_out, cx[5], ctx_len, tm_ctx, ff_tile)
        else:
            e = layer // 2
            h, xn, idx, wts = _out_proj(y_ret.reshape(t_lat, D_RET), y_conv.reshape(t_lat, D_CONV),
                                        y_na.reshape(t_lat, D_NA), h, w_out_bf, lat[2], norm2_g[layer],
                                        lat[3], lat[4], seq, tm_lat,
                                        router=(moe_router_w[e], moe_router_b[e]))
            pos0, pos1, tok_of_slot, w_of_slot, block_expert, n_used = _route_plan(idx, wts)
            xs = _gather_rows(xn.reshape(t_lat, 8, LANES), tok_of_slot, MOE_TS)
            ys = _experts(xs.reshape(-1, d), w_of_slot, block_expert, n_used,
                          moe_w_in[e].astype(BF16), moe_w_out[e].astype(BF16), ff_tile)
            out = _combine(pos0, pos1, ys.reshape(-1, 8, LANES), h.reshape(t_lat, 8, LANES),
                           lat[5].reshape(b, 8, LANES), final_g, seq, min(512, seq))
            h = out.reshape(t_lat, d)
    return out.reshape(b, seq, d)
```

```python
import functools

import numpy as np
import jax
import jax.numpy as jnp
from jax import lax
from jax.experimental import pallas as pl
from jax.experimental.pallas import tpu as pltpu

F32 = jnp.float32
BF16 = jnp.bfloat16

LANES = 128
HEAD_DIM = 64
GRID_W = 64
N_CONV_GROUPS = 4
N_RET_HEADS = 6
N_NA_HEADS = 6
D_RET = N_RET_HEADS * HEAD_DIM
D_CONV = N_CONV_GROUPS * HEAD_DIM
D_NA = N_NA_HEADS * HEAD_DIM
D_IN_PROJ = 4 * D_RET + 3 * D_CONV + 3 * D_NA
N_PAIRS = D_RET // LANES
NA_ROWS = 8
NA_COLS = 16
N_EXPERTS = 8
ROPE_BASE = 10000.0
NORM_EPS = 1e-6
NEG_BIG = -1e30

RET_CHUNK = 256
NA_QROWS = 4
NA_KROWS = 12
MOE_TS = 512
DMA_UNROLL = 8
VMEM_LIMIT = 56 * 1024 * 1024

CB_RQ, CB_RK, CB_RV, CB_RG = 0, 3, 6, 9
CB_CB, CB_CC, CB_CX = 12, 14, 16
CB_NQ, CB_NK, CB_NV = 18, 21, 24


def _cp(sem, vmem=VMEM_LIMIT):
    return pltpu.CompilerParams(dimension_semantics=sem, vmem_limit_bytes=vmem)


def _silu(x):
    return x * (1.0 / (1.0 + jnp.exp(-x)))


def _dot(a, b):
    return jnp.dot(a, b, preferred_element_type=F32)


def _dot_nt(a, b):
    return lax.dot_general(a, b, (((1,), (1,)), ((), ())), preferred_element_type=F32)


def _dot_tn(a, b):
    return lax.dot_general(a, b, (((0,), (0,)), ((), ())), preferred_element_type=F32)


def _split_bf16(x):
    hi = x.astype(BF16)
    lo = (x - hi.astype(F32)).astype(BF16)
    return hi, lo


def _ada_kernel(c_ref, w_ref, b_ref, o_ref):
    x = _silu(c_ref[...]).astype(BF16)
    o_ref[0] = _dot(x, w_ref[0].astype(BF16)) + b_ref[0]


def _ada(c8, ada_w, ada_b):
    depth, d, n = ada_w.shape
    tn = n // 4
    return pl.pallas_call(
        _ada_kernel,
        grid=(depth, n // tn),
        in_specs=[pl.BlockSpec((8, d), lambda l, j: (0, 0)),
                  pl.BlockSpec((1, d, tn), lambda l, j: (l, 0, j)),
                  pl.BlockSpec((1, 1, tn), lambda l, j: (l, 0, j))],
        out_specs=pl.BlockSpec((1, 8, tn), lambda l, j: (l, 0, j)),
        out_shape=jax.ShapeDtypeStruct((depth, 8, n), F32),
        compiler_params=_cp(("parallel", "parallel")),
        name="ada_mod",
    )(c8, ada_w, ada_b.reshape(depth, 1, n))


def _norm_mod(x, g, sh, sc):
    ms = jnp.mean(x * x, axis=-1, keepdims=True)
    y = x * lax.rsqrt(ms + NORM_EPS) * g
    return y * (1.0 + sc) + sh


def _inproj_kernel(*refs, rope):
    if rope:
        h_ref, g_ref, sh_ref, sc_ref, w_ref, cos_ref, sa_ref, sb_ref, o_ref, xn_ref = refs
    else:
        h_ref, g_ref, sh_ref, sc_ref, w_ref, o_ref, xn_ref = refs
    xn_ref[...] = _norm_mod(h_ref[...], g_ref[...], sh_ref[0], sc_ref[0]).astype(BF16)
    cw = 3 * LANES
    for c in range(D_IN_PROJ // cw):
        acc = _dot(xn_ref[...], w_ref[:, c * cw:(c + 1) * cw])
        for j in range(3):
            blk = acc[:, j * LANES:(j + 1) * LANES]
            cb = 3 * c + j
            if rope and cb < CB_RV:
                blk = (blk * cos_ref[...] + pltpu.roll(blk, 16, 1) * sa_ref[...]
                       + pltpu.roll(blk, LANES - 16, 1) * sb_ref[...])
            if CB_RK <= cb < CB_RV or CB_NQ <= cb < CB_NK:
                blk = blk * (HEAD_DIM ** -0.5)
            o_ref[:, cb * LANES:(cb + 1) * LANES] = blk.astype(BF16)


def _in_proj(h2, g, sh, sc, w_bf, tables, seq, tm):
    t, d = h2.shape
    tiles_per_seq = seq // tm
    rope = tables is not None
    in_specs = [pl.BlockSpec((tm, d), lambda i: (i, 0)),
                pl.BlockSpec((1, d), lambda i: (0, 0)),
                pl.BlockSpec((1, 1, d), lambda i: (i // tiles_per_seq, 0, 0)),
                pl.BlockSpec((1, 1, d), lambda i: (i // tiles_per_seq, 0, 0)),
                pl.BlockSpec((d, D_IN_PROJ), lambda i: (0, 0))]
    args = [h2, g.reshape(1, d), sh, sc, w_bf]
    if rope:
        in_specs += [pl.BlockSpec((tm, LANES), lambda i: (i % tiles_per_seq, 0))] * 3
        args += list(tables)
    return pl.pallas_call(
        functools.partial(_inproj_kernel, rope=rope),
        grid=(t // tm,),
        in_specs=in_specs,
        out_specs=pl.BlockSpec((tm, D_IN_PROJ), lambda i: (i, 0)),
        out_shape=jax.ShapeDtypeStruct((t, D_IN_PROJ), BF16),
        scratch_shapes=[pltpu.VMEM((tm, d), BF16)],
        compiler_params=_cp(("parallel",)),
        name="in_proj_rope" if rope else "in_proj_ctx",
    )(*args)


def _rope_tables(seq):
    t = np.arange(seq)
    row = (t // GRID_W).astype(np.float32)
    col = (t % GRID_W).astype(np.float32)
    n_freq = HEAD_DIM // 4
    inv_freq = (ROPE_BASE ** (-np.arange(n_freq, dtype=np.float32) / n_freq)).astype(np.float32)
    ang_r = row[:, None] * inv_freq
    ang_c = col[:, None] * inv_freq
    cos_h = np.concatenate([np.cos(ang_r), np.cos(ang_r), np.cos(ang_c), np.cos(ang_c)], axis=1)
    sin_h = np.concatenate([np.sin(ang_r), np.sin(ang_r), np.sin(ang_c), np.sin(ang_c)], axis=1)
    lane = np.arange(HEAD_DIM)
    second = (lane % 32) >= 16
    sa = np.where(second[None, :], sin_h, 0.0)
    sb = np.where(second[None, :], 0.0, -sin_h)
    tile2 = lambda a: jnp.asarray(np.concatenate([a, a], axis=1), F32)
    return tile2(cos_h), tile2(sa), tile2(sb)


def _ret_kernel(q_ref, k_ref, v_ref, g_ref, lgt_ref, gn_ref, sf0_ref, sb0_ref,
                y_ref, sfo_ref, sbo_ref, sbs_ref, *, seq):
    c = RET_CHUNK
    n_chunks = seq // c
    lg = jax.nn.log_sigmoid(lgt_ref[0])
    lgf, lgb = lg[0:1, :], lg[1:2, :]
    pos = lax.broadcasted_iota(jnp.int32, (c, 1), 0).astype(F32)
    dkf = jnp.exp(lgf * (c - 1.0 - pos))
    dkb = jnp.exp(lgb * pos)
    dqf = jnp.exp(lgf * (pos + 1.0))
    dqb = jnp.exp(lgb * (c - pos))
    cdf = jnp.exp(lgf * float(c))
    cdb = jnp.exp(lgb * float(c))
    lane = lax.broadcasted_iota(jnp.int32, (1, LANES), 1)
    first = lane < HEAD_DIM
    ri = lax.broadcasted_iota(jnp.int32, (LANES, LANES), 0)
    ci = lax.broadcasted_iota(jnp.int32, (LANES, LANES), 1)
    same = (ri < HEAD_DIM) == (ci < HEAD_DIM)
    bd = same.astype(F32)
    avg = (bd * (1.0 / HEAD_DIM)).astype(BF16)
    ii = lax.broadcasted_iota(jnp.int32, (c, c), 0)
    jj = lax.broadcasted_iota(jnp.int32, (c, c), 1)
    dif = (ii - jj).astype(F32)

    def decay_mask(h0):
        lf = lgf[:, h0:h0 + 1]
        lb = lgb[:, h0:h0 + 1]
        return jnp.where(dif > 0, jnp.exp(lf * jnp.maximum(dif, 0.0)),
                         jnp.where(dif < 0, jnp.exp(lb * jnp.maximum(-dif, 0.0)), 2.0))

    dm = (decay_mask(0), decay_mask(HEAD_DIM))
    mfirst = first.astype(BF16)
    msecond = (1.0 - first.astype(F32)).astype(BF16)

    def back(i, s_b):
        n = n_chunks - 1 - i
        sl = pl.ds(pl.multiple_of(n * c, c), c)
        sbs_ref[n] = s_b
        k = k_ref[0, sl, :].astype(F32)
        u = _dot_tn((k * dkb).astype(BF16), v_ref[0, sl, :]) * bd
        return cdb * s_b + u

    s_b = lax.fori_loop(0, n_chunks, back, sb0_ref[0, 0])
    sbo_ref[0, 0] = s_b

    def fwd(n, s_f):
        sl = pl.ds(pl.multiple_of(n * c, c), c)
        q = q_ref[0, sl, :]
        k = k_ref[0, sl, :]
        v = v_ref[0, sl, :]
        qf = q.astype(F32)
        kf = k.astype(F32)
        o = None
        for hh, mk in enumerate((mfirst, msecond)):
            s = _dot_nt(q * mk, k) * dm[hh]
            oh = _dot(s.astype(BF16), v)
            o = oh if o is None else jnp.where(first, o, oh)
        o = o + _dot((qf * dqf).astype(BF16), s_f.astype(BF16))
        o = o + _dot((qf * dqb).astype(BF16), sbs_ref[n].astype(BF16))
        s_f = cdf * s_f + _dot_tn((kf * dkf).astype(BF16), v) * bd
        hi, lo = _split_bf16(o)
        mu = _dot(hi, avg) + _dot(lo, avg)
        dlt = o - mu
        hi, lo = _split_bf16(dlt * dlt)
        var = _dot(hi, avg) + _dot(lo, avg)
        on = dlt * lax.rsqrt(var + NORM_EPS) * gn_ref[0]
        y_ref[0, sl, :] = (_silu(g_ref[0, sl, :].astype(F32)) * on).astype(BF16)
        return s_f

    s_f = lax.fori_loop(0, n_chunks, fwd, sf0_ref[0, 0])
    sfo_ref[0, 0] = s_f


def _retention(proj, lgt, gn, sf0, sb0):
    b, seq, _ = proj.shape
    col = lambda off: pl.BlockSpec((1, seq, LANES), lambda bi, p: (bi, 0, off + p))
    st = pl.BlockSpec((1, 1, LANES, LANES), lambda bi, p: (bi, p, 0, 0))
    return pl.pallas_call(
        functools.partial(_ret_kernel, seq=seq),
        grid=(b, N_PAIRS),
        in_specs=[col(CB_RQ), col(CB_RK), col(CB_RV), col(CB_RG),
                  pl.BlockSpec((1, 2, LANES), lambda bi, p: (p, 0, 0)),
                  pl.BlockSpec((1, 1, LANES), lambda bi, p: (p, 0, 0)),
                  st, st],
        out_specs=[pl.BlockSpec((1, seq, LANES), lambda bi, p: (bi, 0, p)), st, st],
        out_shape=[jax.ShapeDtypeStruct((b, seq, D_RET), BF16),
                   jax.ShapeDtypeStruct((b, N_PAIRS, LANES, LANES), F32),
                   jax.ShapeDtypeStruct((b, N_PAIRS, LANES, LANES), F32)],
        scratch_shapes=[pltpu.VMEM((seq // RET_CHUNK, LANES, LANES), F32)],
        compiler_params=_cp(("parallel", "parallel")),
        name="retention_s%d" % seq,
    )(proj, proj, proj, proj, lgt, gn, sf0, sb0)


def _conv_kernel(b_ref, c_ref, x_ref, w_ref, y_ref, *, seq):
    u = c_ref[0].astype(F32) * x_ref[0].astype(F32)
    row = lax.broadcasted_iota(jnp.int32, (seq, 1), 0)
    prev = jnp.where(row == 0, 0.0, pltpu.roll(u, 1, 0))
    nxt = jnp.where(row == seq - 1, 0.0, pltpu.roll(u, seq - 1, 0))
    y = w_ref[0:1, :] * prev + w_ref[1:2, :] * u + w_ref[2:3, :] * nxt
    y_ref[0] = (b_ref[0].astype(F32) * y).astype(BF16)


def _short_conv(proj, conv_w):
    b, seq, _ = proj.shape
    nblk = D_CONV // LANES
    col = lambda off: pl.BlockSpec((1, seq, LANES), lambda bi, j: (bi, 0, off + j))
    return pl.pallas_call(
        functools.partial(_conv_kernel, seq=seq),
        grid=(b, nblk),
        in_specs=[col(CB_CB), col(CB_CC), col(CB_CX),
                  pl.BlockSpec((3, LANES), lambda bi, j: (0, j))],
        out_specs=pl.BlockSpec((1, seq, LANES), lambda bi, j: (bi, 0, j)),
        out_shape=jax.ShapeDtypeStruct((b, seq, D_CONV), BF16),
        compiler_params=_cp(("parallel", "parallel")),
        name="short_conv_s%d" % seq,
    )(proj, proj, proj, conv_w)


def _rpb_kernel(r_ref, oh_ref, o_ref):
    hi, lo = _split_bf16(r_ref[...])
    o_ref[...] = _dot(hi, oh_ref[...]) + _dot(lo, oh_ref[...])


def _na_bias(rpb, rows):
    h = rpb.shape[0]
    ndr, ndc = 2 * NA_ROWS - 1, 2 * NA_COLS - 1
    w = GRID_W
    cidx = np.arange(w)
    dc = np.clip(cidx[None, :] - cidx[:, None], -(NA_COLS - 1), NA_COLS - 1) + (NA_COLS - 1)
    onehot = np.zeros((LANES, w * w), np.float32)
    onehot[dc.reshape(-1), np.arange(w * w)] = 1.0
    r2 = jnp.zeros((96, LANES), F32).at[:h * ndr, :ndc].set(rpb.reshape(h * ndr, ndc))
    t = pl.pallas_call(
        _rpb_kernel,
        out_shape=jax.ShapeDtypeStruct((96, w * w), F32),
        name="na_rpb_expand",
    )(r2, jnp.asarray(onehot, BF16))
    t = t[:h * ndr].reshape(h, ndr, w, w)
    c0 = np.clip(cidx - NA_COLS // 2, 0, w - NA_COLS)
    col_in = (cidx[None, :] >= c0[:, None]) & (cidx[None, :] < c0[:, None] + NA_COLS)
    t = jnp.where(jnp.asarray(col_in)[None, None], t, NEG_BIG)
    dr_idx = np.zeros((3, NA_QROWS, NA_KROWS), np.int32)
    valid = np.zeros((3, NA_QROWS, NA_KROWS), bool)
    for cl, rb in enumerate((0, NA_QROWS, rows - NA_QROWS)):
        ws = int(np.clip(rb - NA_ROWS // 2, 0, rows - NA_KROWS))
        for i in range(NA_QROWS):
            r = rb + i
            r0 = int(np.clip(r - NA_ROWS // 2, 0, rows - NA_ROWS))
            for j in range(NA_KROWS):
                kr = ws + j
                if r0 <= kr < r0 + NA_ROWS:
                    valid[cl, i, j] = True
                    dr_idx[cl, i, j] = kr - r + (NA_ROWS - 1)
    big = t[:, dr_idx]
    big = jnp.where(jnp.asarray(valid)[None, :, :, :, None, None], big, NEG_BIG)
    big = big.transpose(0, 1, 2, 4, 3, 5).reshape(h, 3, NA_QROWS * w, NA_KROWS * w)
    return big


def _na_kernel(q_ref, k_ref, v_ref, kc_ref, vc_ref, bias_ref, o_ref, *, rows):
    qb = pl.program_id(2)
    ws = jnp.clip(qb * NA_QROWS - NA_ROWS // 2, 0, rows - NA_KROWS)
    sl = pl.ds(pl.multiple_of(ws * GRID_W, GRID_W), NA_KROWS * GRID_W)
    q = q_ref[0]
    kw = k_ref[0, sl, :]
    vw = v_ref[0, sl, :]
    kc = kc_ref[0]
    vc = vc_ref[0]
    lane = lax.broadcasted_iota(jnp.int32, (1, LANES), 1)
    first = lane < HEAD_DIM
    o = None
    for hh in range(2):
        mk = (first if hh == 0 else jnp.logical_not(first)).astype(BF16)
        qh = q * mk
        s_loc = _dot_nt(qh, kw) + bias_ref[hh, 0]
        s_ctx = _dot_nt(qh, kc)
        m = jnp.maximum(jnp.max(s_loc, axis=-1, keepdims=True), jnp.max(s_ctx, axis=-1, keepdims=True))
        p_loc = jnp.exp(s_loc - m)
        p_ctx = jnp.exp(s_ctx - m)
        l = jnp.sum(p_loc, axis=-1, keepdims=True) + jnp.sum(p_ctx, axis=-1, keepdims=True)
        oh = (_dot(p_loc.astype(BF16), vw) + _dot(p_ctx.astype(BF16), vc)) / l
        o = oh if o is None else jnp.where(first, o, oh)
    o_ref[0] = o.astype(BF16)


def _na(proj, proj_ctx, bias):
    b, seq, _ = proj.shape
    ctx_len = proj_ctx.shape[1]
    rows = seq // GRID_W
    nqb = rows // NA_QROWS
    tq = NA_QROWS * GRID_W

    def cls(qb):
        return jnp.where(qb == 0, 0, jnp.where(qb == nqb - 1, 2, 1))

    return pl.pallas_call(
        functools.partial(_na_kernel, rows=rows),
        grid=(b, N_PAIRS, nqb),
        in_specs=[pl.BlockSpec((1, tq, LANES), lambda bi, p, qb: (bi, qb, CB_NQ + p)),
                  pl.BlockSpec((1, seq, LANES), lambda bi, p, qb: (bi, 0, CB_NK + p)),
                  pl.BlockSpec((1, seq, LANES), lambda bi, p, qb: (bi, 0, CB_NV + p)),
                  pl.BlockSpec((1, ctx_len, LANES), lambda bi, p, qb: (bi, 0, CB_NK + p)),
                  pl.BlockSpec((1, ctx_len, LANES), lambda bi, p, qb: (bi, 0, CB_NV + p)),
                  pl.BlockSpec((2, 1, tq, NA_KROWS * GRID_W), lambda bi, p, qb: (p, cls(qb), 0, 0))],
        out_specs=pl.BlockSpec((1, tq, LANES), lambda bi, p, qb: (bi, qb, p)),
        out_shape=jax.ShapeDtypeStruct((b, seq, D_NA), BF16),
        compiler_params=_cp(("parallel", "parallel", "arbitrary")),
        name="na_attn",
    )(proj, proj, proj, proj_ctx, proj_ctx, bias)


def _ctx_attn_kernel(q_ref, k_ref, v_ref, o_ref):
    q = q_ref[0]
    k = k_ref[0]
    v = v_ref[0]
    lane = lax.broadcasted_iota(jnp.int32, (1, LANES), 1)
    first = lane < HEAD_DIM
    o = None
    for hh in range(2):
        mk = (first if hh == 0 else jnp.logical_not(first)).astype(BF16)
        s = _dot_nt(q * mk, k)
        m = jnp.max(s, axis=-1, keepdims=True)
        p = jnp.exp(s - m)
        l = jnp.sum(p, axis=-1, keepdims=True)
        oh = _dot(p.astype(BF16), v) / l
        o = oh if o is None else jnp.where(first, o, oh)
    o_ref[0] = o.astype(BF16)


def _ctx_attn(proj_ctx):
    b, ctx_len, _ = proj_ctx.shape
    col = lambda off: pl.BlockSpec((1, ctx_len, LANES), lambda bi, p: (bi, 0, off + p))
    return pl.pallas_call(
        _ctx_attn_kernel,
        grid=(b, N_PAIRS),
        in_specs=[col(CB_NQ), col(CB_NK), col(CB_NV)],
        out_specs=pl.BlockSpec((1, ctx_len, LANES), lambda bi, p: (bi, 0, p)),
        out_shape=jax.ShapeDtypeStruct((b, ctx_len, D_NA), BF16),
        compiler_params=_cp(("parallel", "parallel")),
        name="ctx_attn",
    )(proj_ctx, proj_ctx, proj_ctx)


def _outproj_kernel(*refs, route):
    if route:
        (yr_ref, yc_ref, yn_ref, h_ref, w_ref, g1_ref, n2_ref, sh_ref, sc_ref, rw_ref, rb_ref,
         ho_ref, xn_ref, idx_ref, wt_ref) = refs
    else:
        yr_ref, yc_ref, yn_ref, h_ref, w_ref, g1_ref, n2_ref, sh_ref, sc_ref, ho_ref, xn_ref = refs
    ycat = jnp.concatenate([yr_ref[...], yc_ref[...], yn_ref[...]], axis=-1)
    h = h_ref[...] + g1_ref[0] * _dot(ycat, w_ref[...])
    ho_ref[...] = h
    a = _norm_mod(h, n2_ref[...], sh_ref[0], sc_ref[0])
    xn_ref[...] = a.astype(xn_ref.dtype)
    if route:
        a_hi, a_lo = _split_bf16(a)
        r_hi, r_lo = _split_bf16(rw_ref[...])
        logits = _dot_nt(r_hi, a_hi) + _dot_nt(r_hi, a_lo) + _dot_nt(r_lo, a_hi) + rb_ref[...]
        eidx = lax.broadcasted_iota(jnp.int32, logits.shape, 0)
        m1 = jnp.max(logits, axis=0, keepdims=True)
        i1 = jnp.min(jnp.where(logits == m1, eidx, N_EXPERTS), axis=0, keepdims=True)
        rest = jnp.where(eidx == i1, -jnp.inf, logits)
        m2 = jnp.max(rest, axis=0, keepdims=True)
        i2 = jnp.min(jnp.where(rest == m2, eidx, N_EXPERTS), axis=0, keepdims=True)
        e2 = jnp.exp(m2 - m1)
        den = 1.0 + e2
        idx_ref[...] = jnp.concatenate([i1, i2], axis=0)
        wt_ref[...] = jnp.concatenate([1.0 / den, e2 / den], axis=0)


def _out_proj(yr, yc, yn, h2, w_bf, g1, n2g, sh2, sc2, seq, tm, router=None):
    t, d = h2.shape
    tps = seq // tm
    route = router is not None
    row = lambda wdt: pl.BlockSpec((tm, wdt), lambda i: (i, 0))
    mod = pl.BlockSpec((1, 1, d), lambda i: (i // tps, 0, 0))
    in_specs = [row(D_RET), row(D_CONV), row(D_NA), row(d),
                pl.BlockSpec((d, d), lambda i: (0, 0)), mod,
                pl.BlockSpec((1, d), lambda i: (0, 0)), mod, mod]
    args = [yr, yc, yn, h2, w_bf, g1, n2g.reshape(1, d), sh2, sc2]
    out_specs = [row(d), row(d)]
    out_shape = [jax.ShapeDtypeStruct((t, d), F32), jax.ShapeDtypeStruct((t, d), F32 if route else BF16)]
    if route:
        rw, rb = router
        in_specs += [pl.BlockSpec((N_EXPERTS, d), lambda i: (0, 0)),
                     pl.BlockSpec((N_EXPERTS, 1), lambda i: (0, 0))]
        args += [rw.T, rb.reshape(N_EXPERTS, 1)]
        out_specs += [pl.BlockSpec((2, tm), lambda i: (0, i))] * 2
        out_shape += [jax.ShapeDtypeStruct((2, t), jnp.int32), jax.ShapeDtypeStruct((2, t), F32)]
    return pl.pallas_call(
        functools.partial(_outproj_kernel, route=route),
        grid=(t // tm,),
        in_specs=in_specs,
        out_specs=out_specs,
        out_shape=out_shape,
        compiler_params=_cp(("parallel",)),
        name="out_proj_route" if route else "out_proj_t%d" % t,
    )(*args)


def _ffn_kernel(x_ref, h_ref, wg_ref, wu_ref, wo_ref, g2_ref, o_ref, acc_ref):
    f = pl.program_id(1)

    @pl.when(f == 0)
    def _():
        acc_ref[...] = jnp.zeros_like(acc_ref)

    x = x_ref[...]
    a = _silu(_dot(x, wg_ref[...])) * _dot(x, wu_ref[...])
    acc_ref[...] += _dot(a.astype(BF16), wo_ref[...])

    @pl.when(f == pl.num_programs(1) - 1)
    def _():
        o_ref[...] = h_ref[...] + g2_ref[0] * acc_ref[...]


def _ffn(xn, h2, w_in_bf, w_out_bf, g2, seq, tm, tf):
    t, d = h2.shape
    ff = w_out_bf.shape[0]
    nf = ff // tf
    tps = seq // tm
    return pl.pallas_call(
        _ffn_kernel,
        grid=(t // tm, nf),
        in_specs=[pl.BlockSpec((tm, d), lambda i, f: (i, 0)),
                  pl.BlockSpec((tm, d), lambda i, f: (i, 0)),
                  pl.BlockSpec((d, tf), lambda i, f: (0, f)),
                  pl.BlockSpec((d, tf), lambda i, f: (0, nf + f)),
                  pl.BlockSpec((tf, d), lambda i, f: (f, 0)),
                  pl.BlockSpec((1, 1, d), lambda i, f: (i // tps, 0, 0))],
        out_specs=pl.BlockSpec((tm, d), lambda i, f: (i, 0)),
        out_shape=jax.ShapeDtypeStruct((t, d), F32),
        scratch_shapes=[pltpu.VMEM((tm, d), F32)],
        compiler_params=_cp(("parallel", "arbitrary")),
        name="ffn_t%d" % t,
    )(xn, h2, w_in_bf, w_in_bf, w_out_bf, g2)


def _dispatch_kernel(s0_ref, s1_ref, zs_ref, zv_ref, x_ref, xs_ref, zbuf_ref, sem, *, rows):
    @pl.when(pl.program_id(0) == 0)
    def _():
        zbuf_ref[...] = jnp.zeros_like(zbuf_ref)
        for e in range(2 * N_EXPERTS):
            @pl.when(zv_ref[e] != 0)
            def _():
                dst = xs_ref.at[pl.ds(pl.multiple_of(zs_ref[e], MOE_TS), MOE_TS)]
                pltpu.make_async_copy(zbuf_ref, dst, sem.at[1]).start()
        for e in range(2 * N_EXPERTS):
            @pl.when(zv_ref[e] != 0)
            def _():
                pltpu.make_async_copy(zbuf_ref, xs_ref.at[pl.ds(0, MOE_TS)], sem.at[1]).wait()

    def issue(g, carry):
        for u in range(DMA_UNROLL):
            r = g * DMA_UNROLL + u
            src = x_ref.at[pl.ds(r, 1)]
            pltpu.make_async_copy(src, xs_ref.at[pl.ds(s0_ref[r], 1)], sem.at[0]).start(priority=u % 2)
            pltpu.make_async_copy(src, xs_ref.at[pl.ds(s1_ref[r], 1)], sem.at[0]).start(priority=(u + 1) % 2)
        return carry

    lax.fori_loop(0, rows // DMA_UNROLL, issue, 0)
    for _ in range(2):
        pltpu.make_async_copy(x_ref, xs_ref.at[pl.ds(0, rows)], sem.at[0]).wait()


def _dispatch(xn, slot0, slot1, zplan, n_slots, rows):
    t, d = xn.shape
    zstart, zvalid = zplan
    return pl.pallas_call(
        functools.partial(_dispatch_kernel, rows=rows),
        grid=(t // rows,),
        in_specs=[pl.BlockSpec((rows,), lambda i: (i,), memory_space=pltpu.SMEM),
                  pl.BlockSpec((rows,), lambda i: (i,), memory_space=pltpu.SMEM),
                  pl.BlockSpec((2 * N_EXPERTS,), lambda i: (0,), memory_space=pltpu.SMEM),
                  pl.BlockSpec((2 * N_EXPERTS,), lambda i: (0,), memory_space=pltpu.SMEM),
                  pl.BlockSpec((rows, d), lambda i: (i, 0))],
        out_specs=pl.BlockSpec(memory_space=pl.ANY),
        out_shape=jax.ShapeDtypeStruct((n_slots, d), xn.dtype),
        scratch_shapes=[pltpu.VMEM((MOE_TS, d), xn.dtype), pltpu.SemaphoreType.DMA((2,))],
        compiler_params=_cp(("arbitrary",)),
        name="moe_dispatch",
    )(slot0, slot1, zstart, zvalid, xn)


def _expert_kernel(be_ref, nu_ref, x_ref, wg_ref, wu_ref, wo_ref, o_ref, acc_ref):
    j = pl.program_id(0)
    f = pl.program_id(1)
    last = pl.num_programs(1) - 1
    used = j < nu_ref[0]

    @pl.when(jnp.logical_and(used, f == 0))
    def _():
        acc_ref[...] = jnp.zeros_like(acc_ref)

    @pl.when(used)
    def _():
        x = x_ref[...].astype(BF16)
        a = _silu(_dot(x, wg_ref[0])) * _dot(x, wu_ref[0])
        acc_ref[...] += _dot(a.astype(BF16), wo_ref[0])

    @pl.when(jnp.logical_and(used, f == last))
    def _():
        o_ref[...] = acc_ref[...]

    @pl.when(jnp.logical_and(jnp.logical_not(used), f == last))
    def _():
        o_ref[...] = jnp.zeros_like(o_ref)


def _experts(xs, block_expert, n_used, w_in_bf, w_out_bf, tf):
    n, d = xs.shape
    ff = w_out_bf.shape[1]
    nf = ff // tf
    ts = MOE_TS

    def fe(j, f):
        return jnp.where(j % 2 == 0, f, nf - 1 - f)

    grid_spec = pltpu.PrefetchScalarGridSpec(
        num_scalar_prefetch=2,
        grid=(n // ts, nf),
        in_specs=[pl.BlockSpec((ts, d), lambda j, f, be, nu: (j, 0)),
                  pl.BlockSpec((1, d, tf), lambda j, f, be, nu: (be[j], 0, fe(j, f))),
                  pl.BlockSpec((1, d, tf), lambda j, f, be, nu: (be[j], 0, nf + fe(j, f))),
                  pl.BlockSpec((1, tf, d), lambda j, f, be, nu: (be[j], fe(j, f), 0))],
        out_specs=pl.BlockSpec((ts, d), lambda j, f, be, nu: (j, 0)),
        scratch_shapes=[pltpu.VMEM((ts, d), F32)],
    )
    return pl.pallas_call(
        _expert_kernel,
        grid_spec=grid_spec,
        out_shape=jax.ShapeDtypeStruct((n, d), F32),
        compiler_params=_cp(("arbitrary", "arbitrary")),
        name="moe_experts",
    )(block_expert, n_used, xs, w_in_bf, w_in_bf, w_out_bf)


def _combine_kernel(p0_ref, p1_ref, ys_ref, h_ref, w0_ref, w1_ref, g2_ref, fg_ref, o_ref,
                    y0_ref, y1_ref, sem, *, rows):
    def issue(g, carry):
        for u in range(DMA_UNROLL):
            r = g * DMA_UNROLL + u
            pltpu.make_async_copy(ys_ref.at[pl.ds(p0_ref[r], 1)], y0_ref.at[pl.ds(r, 1)],
                                  sem.at[0]).start(priority=u % 2)
            pltpu.make_async_copy(ys_ref.at[pl.ds(p1_ref[r], 1)], y1_ref.at[pl.ds(r, 1)],
                                  sem.at[1]).start(priority=(u + 1) % 2)
        return carry

    lax.fori_loop(0, rows // DMA_UNROLL, issue, 0)
    pltpu.make_async_copy(ys_ref.at[pl.ds(0, rows)], y0_ref, sem.at[0]).wait()
    pltpu.make_async_copy(ys_ref.at[pl.ds(0, rows)], y1_ref, sem.at[1]).wait()
    h = h_ref[...] + g2_ref[0] * (w0_ref[...] * y0_ref[...] + w1_ref[...] * y1_ref[...])
    ms = jnp.mean(h * h, axis=-1, keepdims=True)
    o_ref[...] = h * lax.rsqrt(ms + NORM_EPS) * fg_ref[...]


def _combine(pos0, pos1, ys, h2, w0, w1, g2, final_g, seq, rows):
    t, d = h2.shape
    tps = seq // rows
    blk = pl.BlockSpec((rows, d), lambda i: (i, 0))
    col = pl.BlockSpec((rows, 1), lambda i: (i, 0))
    return pl.pallas_call(
        functools.partial(_combine_kernel, rows=rows),
        grid=(t // rows,),
        in_specs=[pl.BlockSpec((rows,), lambda i: (i,), memory_space=pltpu.SMEM),
                  pl.BlockSpec((rows,), lambda i: (i,), memory_space=pltpu.SMEM),
                  pl.BlockSpec(memory_space=pl.ANY),
                  blk, col, col,
                  pl.BlockSpec((1, 1, d), lambda i: (i // tps, 0, 0)),
                  pl.BlockSpec((1, d), lambda i: (0, 0))],
        out_specs=blk,
        out_shape=jax.ShapeDtypeStruct((t, d), F32),
        scratch_shapes=[pltpu.VMEM((rows, d), F32), pltpu.VMEM((rows, d), F32),
                        pltpu.SemaphoreType.DMA((2,))],
        compiler_params=_cp(("arbitrary",)),
        name="moe_combine_norm",
    )(pos0, pos1, ys, h2, w0, w1, g2, final_g.reshape(1, d))


def _route_plan(idx):
    t = idx.shape[1]
    ts = MOE_TS
    n_slots = 2 * t + N_EXPERTS * ts
    e_flat = idx.reshape(-1)
    onehot = (e_flat[:, None] == jnp.arange(N_EXPERTS, dtype=jnp.int32)[None, :]).astype(jnp.int32)
    csum = jnp.cumsum(onehot, axis=0)
    rank = jnp.sum((csum - onehot) * onehot, axis=1)
    counts = csum[-1]
    padded = ((counts + ts - 1) // ts) * ts
    ends = jnp.cumsum(padded)
    offs = ends - padded
    slot = (jnp.sum(onehot * offs[None, :], axis=1) + rank).astype(jnp.int32)
    starts = jnp.arange(n_slots // ts, dtype=jnp.int32) * ts
    block_expert = jnp.minimum(jnp.sum((starts[:, None] >= ends[None, :]).astype(jnp.int32), axis=1),
                               N_EXPERTS - 1).astype(jnp.int32)
    n_used = (ends[-1] // ts).astype(jnp.int32).reshape(1)
    tail = ends[-1] + jnp.arange(N_EXPERTS, dtype=jnp.int32) * ts
    zstart = jnp.concatenate([ends - ts, tail])
    zvalid = jnp.concatenate([padded > 0, tail < n_slots]).astype(jnp.int32)
    zstart = jnp.where(zvalid != 0, zstart, 0).astype(jnp.int32)
    return slot[:t], slot[t:], block_expert, n_used, (zstart, zvalid), n_slots


def kernel(x, c, ctx, c_ctx, ada_w, ada_b, norm1_g, norm2_g, w_in, w_out, ret_decay_logit, ret_gn_g,
           conv_w, na_rpb, ffn_w_in, ffn_w_out, moe_router_w, moe_router_b, moe_w_in, moe_w_out, final_g):
    b, seq, d = x.shape
    ctx_len = ctx.shape[1]
    depth = ada_w.shape[0]
    assert depth == 2, "the final norm is fused into the last (MoE) layer's combine step"
    rows = seq // GRID_W
    t_lat, t_ctx = b * seq, b * ctx_len
    tm_lat = min(1024, seq)
    tm_ctx = ctx_len
    ff_tile = ffn_w_out.shape[1] // 2

    c8 = jnp.zeros((8, d), F32).at[:b].set(c).at[b].set(c_ctx)
    mods = _ada(c8, ada_w, ada_b)
    tables = _rope_tables(seq)

    h = x.reshape(t_lat, d)
    hc = ctx.reshape(t_ctx, d)
    zero_state = jnp.zeros((b, N_PAIRS, LANES, LANES), F32)
    out = None
    for layer in range(depth):
        update_ctx = layer < depth - 1
        m = mods[layer].reshape(8, 6, d)
        lat = [m[:b, i].reshape(b, 1, d) for i in range(6)]
        cx = [jnp.broadcast_to(m[b, i].reshape(1, 1, d), (b, 1, d)) for i in range(6)]
        w_in_bf = w_in[layer].astype(BF16)
        w_out_bf = w_out[layer].astype(BF16)
        lgt = ret_decay_logit[layer].reshape(2, N_PAIRS, LANES // HEAD_DIM)
        lgt = jnp.repeat(lgt, HEAD_DIM, axis=2).transpose(1, 0, 2)
        gn = ret_gn_g[layer].reshape(N_PAIRS, 1, LANES)

        proj_c = _in_proj(hc, norm1_g[layer], cx[0], cx[1], w_in_bf, None, ctx_len, tm_ctx)
        proj_c = proj_c.reshape(b, ctx_len, D_IN_PROJ)
        proj = _in_proj(h, norm1_g[layer], lat[0], lat[1], w_in_bf, tables, seq, tm_lat)
        proj = proj.reshape(b, seq, D_IN_PROJ)

        y_ret_c, s_fwd, s_bwd = _retention(proj_c, lgt, gn, zero_state, zero_state)
        y_ret, _, _ = _retention(proj, lgt, gn, s_fwd, s_bwd)
        y_conv = _short_conv(proj, conv_w[layer])
        y_na = _na(proj, proj_c, _na_bias(na_rpb[layer], rows))

        if layer % 2 == 0:
            h, xn = _out_proj(y_ret.reshape(t_lat, D_RET), y_conv.reshape(t_lat, D_CONV),
                              y_na.reshape(t_lat, D_NA), h, w_out_bf, lat[2], norm2_g[layer],
                              lat[3], lat[4], seq, tm_lat)
            fw_in = ffn_w_in[layer // 2].astype(BF16)
            fw_out = ffn_w_out[layer // 2].astype(BF16)
            h = _ffn(xn, h, fw_in, fw_out, lat[5], seq, min(512, seq), ff_tile)
            if update_ctx:
                y_conv_c = _short_conv(proj_c, conv_w[layer])
                y_na_c = _ctx_attn(proj_c)
                hc, xnc = _out_proj(y_ret_c.reshape(t_ctx, D_RET), y_conv_c.reshape(t_ctx, D_CONV),
                                    y_na_c.reshape(t_ctx, D_NA), hc, w_out_bf, cx[2], norm2_g[layer],
                                    cx[3], cx[4], ctx_len, tm_ctx)
                hc = _ffn(xnc, hc, fw_in, fw_out, cx[5], ctx_len, tm_ctx, ff_tile)
        else:
            e = layer // 2
            h, xn, idx, wts = _out_proj(y_ret.reshape(t_lat, D_RET), y_conv.reshape(t_lat, D_CONV),
                                        y_na.reshape(t_lat, D_NA), h, w_out_bf, lat[2], norm2_g[layer],
                                        lat[3], lat[4], seq, tm_lat,
                                        router=(moe_router_w[e], moe_router_b[e]))
            pos0, pos1, block_expert, n_used, zstart, n_slots = _route_plan(idx)
            xs = _dispatch(xn, pos0, pos1, zstart, n_slots, min(1024, seq))
            ys = _experts(xs, block_expert, n_used, moe_w_in[e].astype(BF16), moe_w_out[e].astype(BF16), ff_tile)
            out = _combine(pos0, pos1, ys, h, wts[0].reshape(t_lat, 1), wts[1].reshape(t_lat, 1),
                           lat[5], final_g, seq, min(512, seq))
            h = out
    return out.reshape(b, seq, d)
```

```python
import functools

import numpy as np
import jax
import jax.numpy as jnp
from jax import lax
from jax.experimental import pallas as pl
from jax.experimental.pallas import tpu as pltpu

F32 = jnp.float32
BF16 = jnp.bfloat16

LANES = 128
HEAD_DIM = 64
GRID_W = 64
N_CONV_GROUPS = 4
N_RET_HEADS = 6
N_NA_HEADS = 6
D_RET = N_RET_HEADS * HEAD_DIM
D_CONV = N_CONV_GROUPS * HEAD_DIM
D_NA = N_NA_HEADS * HEAD_DIM
D_IN_PROJ = 4 * D_RET + 3 * D_CONV + 3 * D_NA
N_PAIRS = D_RET // LANES
NA_ROWS = 8
NA_COLS = 16
N_EXPERTS = 8
ROPE_BASE = 10000.0
NORM_EPS = 1e-6
NEG_BIG = -1e30

RET_CHUNK = 256
NA_QROWS = 4
NA_KROWS = 12
MOE_TS = 512
DMA_UNROLL = 8
VMEM_LIMIT = 56 * 1024 * 1024

CB_RQ, CB_RK, CB_RV, CB_RG = 0, 3, 6, 9
CB_CB, CB_CC, CB_CX = 12, 14, 16
CB_NQ, CB_NK, CB_NV = 18, 21, 24


def _cp(sem, vmem=VMEM_LIMIT):
    return pltpu.CompilerParams(dimension_semantics=sem, vmem_limit_bytes=vmem)


def _silu(x):
    return x * (1.0 / (1.0 + jnp.exp(-x)))


def _dot(a, b):
    return jnp.dot(a, b, preferred_element_type=F32)


def _dot_nt(a, b):
    return lax.dot_general(a, b, (((1,), (1,)), ((), ())), preferred_element_type=F32)


def _dot_tn(a, b):
    return lax.dot_general(a, b, (((0,), (0,)), ((), ())), preferred_element_type=F32)


def _split_bf16(x):
    hi = x.astype(BF16)
    lo = (x - hi.astype(F32)).astype(BF16)
    return hi, lo


def _ada_kernel(c_ref, w_ref, b_ref, o_ref):
    x = _silu(c_ref[...]).astype(BF16)
    o_ref[0] = _dot(x, w_ref[0].astype(BF16)) + b_ref[0]


def _ada(c8, ada_w, ada_b):
    depth, d, n = ada_w.shape
    tn = n // 4
    return pl.pallas_call(
        _ada_kernel,
        grid=(depth, n // tn),
        in_specs=[pl.BlockSpec((8, d), lambda l, j: (0, 0)),
                  pl.BlockSpec((1, d, tn), lambda l, j: (l, 0, j)),
                  pl.BlockSpec((1, 1, tn), lambda l, j: (l, 0, j))],
        out_specs=pl.BlockSpec((1, 8, tn), lambda l, j: (l, 0, j)),
        out_shape=jax.ShapeDtypeStruct((depth, 8, n), F32),
        compiler_params=_cp(("parallel", "parallel")),
        name="ada_mod",
    )(c8, ada_w, ada_b.reshape(depth, 1, n))


def _norm_mod(x, g, sh, sc):
    ms = jnp.mean(x * x, axis=-1, keepdims=True)
    y = x * lax.rsqrt(ms + NORM_EPS) * g
    return y * (1.0 + sc) + sh


def _inproj_kernel(*refs, rope):
    if rope:
        h_ref, g_ref, sh_ref, sc_ref, w_ref, cos_ref, sa_ref, sb_ref, o_ref, xn_ref = refs
    else:
        h_ref, g_ref, sh_ref, sc_ref, w_ref, o_ref, xn_ref = refs
    xn_ref[...] = _norm_mod(h_ref[...], g_ref[...], sh_ref[0], sc_ref[0]).astype(BF16)
    cw = 3 * LANES
    for c in range(D_IN_PROJ // cw):
        acc = _dot(xn_ref[...], w_ref[:, c * cw:(c + 1) * cw])
        for j in range(3):
            blk = acc[:, j * LANES:(j + 1) * LANES]
            cb = 3 * c + j
            if rope and cb < CB_RV:
                blk = (blk * cos_ref[...] + pltpu.roll(blk, 16, 1) * sa_ref[...]
                       + pltpu.roll(blk, LANES - 16, 1) * sb_ref[...])
            if CB_RK <= cb < CB_RV or CB_NQ <= cb < CB_NK:
                blk = blk * (HEAD_DIM ** -0.5)
            o_ref[:, cb * LANES:(cb + 1) * LANES] = blk.astype(BF16)


def _in_proj(h2, g, sh, sc, w_bf, tables, seq, tm):
    t, d = h2.shape
    tiles_per_seq = seq // tm
    rope = tables is not None
    in_specs = [pl.BlockSpec((tm, d), lambda i: (i, 0)),
                pl.BlockSpec((1, d), lambda i: (0, 0)),
                pl.BlockSpec((1, 1, d), lambda i: (i // tiles_per_seq, 0, 0)),
                pl.BlockSpec((1, 1, d), lambda i: (i // tiles_per_seq, 0, 0)),
                pl.BlockSpec((d, D_IN_PROJ), lambda i: (0, 0))]
    args = [h2, g.reshape(1, d), sh, sc, w_bf]
    if rope:
        in_specs += [pl.BlockSpec((tm, LANES), lambda i: (i % tiles_per_seq, 0))] * 3
        args += list(tables)
    return pl.pallas_call(
        functools.partial(_inproj_kernel, rope=rope),
        grid=(t // tm,),
        in_specs=in_specs,
        out_specs=pl.BlockSpec((tm, D_IN_PROJ), lambda i: (i, 0)),
        out_shape=jax.ShapeDtypeStruct((t, D_IN_PROJ), BF16),
        scratch_shapes=[pltpu.VMEM((tm, d), BF16)],
        compiler_params=_cp(("parallel",)),
        name="in_proj_rope" if rope else "in_proj_ctx",
    )(*args)


def _rope_tables(seq):
    t = np.arange(seq)
    row = (t // GRID_W).astype(np.float32)
    col = (t % GRID_W).astype(np.float32)
    n_freq = HEAD_DIM // 4
    inv_freq = (ROPE_BASE ** (-np.arange(n_freq, dtype=np.float32) / n_freq)).astype(np.float32)
    ang_r = row[:, None] * inv_freq
    ang_c = col[:, None] * inv_freq
    cos_h = np.concatenate([np.cos(ang_r), np.cos(ang_r), np.cos(ang_c), np.cos(ang_c)], axis=1)
    sin_h = np.concatenate([np.sin(ang_r), np.sin(ang_r), np.sin(ang_c), np.sin(ang_c)], axis=1)
    lane = np.arange(HEAD_DIM)
    second = (lane % 32) >= 16
    sa = np.where(second[None, :], sin_h, 0.0)
    sb = np.where(second[None, :], 0.0, -sin_h)
    tile2 = lambda a: jnp.asarray(np.concatenate([a, a], axis=1), F32)
    return tile2(cos_h), tile2(sa), tile2(sb)


def _ret_kernel(q_ref, k_ref, v_ref, g_ref, lgt_ref, gn_ref, sf0_ref, sb0_ref,
                y_ref, sfo_ref, sbo_ref, sfs_ref, sbs_ref, *, seq):
    c = RET_CHUNK
    n_chunks = seq // c
    lg = jax.nn.log_sigmoid(lgt_ref[0])
    lgf, lgb = lg[0:1, :], lg[1:2, :]
    pos = lax.broadcasted_iota(jnp.int32, (c, 1), 0).astype(F32)
    dkf = jnp.exp(lgf * (c - 1.0 - pos))
    dkb = jnp.exp(lgb * pos)
    dqf = jnp.exp(lgf * (pos + 1.0))
    dqb = jnp.exp(lgb * (c - pos))
    cdf = jnp.exp(lgf * float(c))
    cdb = jnp.exp(lgb * float(c))
    lane = lax.broadcasted_iota(jnp.int32, (1, LANES), 1)
    first = lane < HEAD_DIM
    ri = lax.broadcasted_iota(jnp.int32, (LANES, LANES), 0)
    ci = lax.broadcasted_iota(jnp.int32, (LANES, LANES), 1)
    same = (ri < HEAD_DIM) == (ci < HEAD_DIM)
    bd = same.astype(F32)
    avg = (bd * (1.0 / HEAD_DIM)).astype(BF16)
    ii = lax.broadcasted_iota(jnp.int32, (c, c), 0)
    jj = lax.broadcasted_iota(jnp.int32, (c, c), 1)
    dif = (ii - jj).astype(F32)

    def decay_mask(h0):
        lf = lgf[:, h0:h0 + 1]
        lb = lgb[:, h0:h0 + 1]
        return jnp.where(dif > 0, jnp.exp(lf * jnp.maximum(dif, 0.0)),
                         jnp.where(dif < 0, jnp.exp(lb * jnp.maximum(-dif, 0.0)), 2.0))

    dm = (decay_mask(0), decay_mask(HEAD_DIM))
    mfirst = first.astype(BF16)
    msecond = (1.0 - first.astype(F32)).astype(BF16)

    unroll = min(2, n_chunks)

    def incr(n, carry):
        sl = pl.ds(pl.multiple_of(n * c, c), c)
        kf = k_ref[0, sl, :].astype(F32)
        v = v_ref[0, sl, :]
        sfs_ref[n] = _dot_tn((kf * dkf).astype(BF16), v) * bd
        sbs_ref[n] = _dot_tn((kf * dkb).astype(BF16), v) * bd
        return carry

    lax.fori_loop(0, n_chunks, incr, 0, unroll=unroll)

    def scan_f(n, s):
        u = sfs_ref[n]
        sfs_ref[n] = s
        return cdf * s + u

    def scan_b(i, s):
        n = n_chunks - 1 - i
        u = sbs_ref[n]
        sbs_ref[n] = s
        return cdb * s + u

    sfo_ref[0, 0] = lax.fori_loop(0, n_chunks, scan_f, sf0_ref[0, 0])
    sbo_ref[0, 0] = lax.fori_loop(0, n_chunks, scan_b, sb0_ref[0, 0])

    def outp(n, carry):
        sl = pl.ds(pl.multiple_of(n * c, c), c)
        q = q_ref[0, sl, :]
        k = k_ref[0, sl, :]
        v = v_ref[0, sl, :]
        qf = q.astype(F32)
        o = None
        for hh, mk in enumerate((mfirst, msecond)):
            s = _dot_nt(q * mk, k) * dm[hh]
            oh = _dot(s.astype(BF16), v)
            o = oh if o is None else jnp.where(first, o, oh)
        o = o + _dot((qf * dqf).astype(BF16), sfs_ref[n].astype(BF16))
        o = o + _dot((qf * dqb).astype(BF16), sbs_ref[n].astype(BF16))
        hi, lo = _split_bf16(o)
        mu = _dot(hi, avg) + _dot(lo, avg)
        dlt = o - mu
        hi, lo = _split_bf16(dlt * dlt)
        var = _dot(hi, avg) + _dot(lo, avg)
        on = dlt * lax.rsqrt(var + NORM_EPS) * gn_ref[0]
        y_ref[0, sl, :] = (_silu(g_ref[0, sl, :].astype(F32)) * on).astype(BF16)
        return carry

    lax.fori_loop(0, n_chunks, outp, 0, unroll=unroll)


def _retention(proj, lgt, gn, sf0, sb0):
    b, seq, _ = proj.shape
    col = lambda off: pl.BlockSpec((1, seq, LANES), lambda bi, p: (bi, 0, off + p))
    st = pl.BlockSpec((1, 1, LANES, LANES), lambda bi, p: (bi, p, 0, 0))
    return pl.pallas_call(
        functools.partial(_ret_kernel, seq=seq),
        grid=(b, N_PAIRS),
        in_specs=[col(CB_RQ), col(CB_RK), col(CB_RV), col(CB_RG),
                  pl.BlockSpec((1, 2, LANES), lambda bi, p: (p, 0, 0)),
                  pl.BlockSpec((1, 1, LANES), lambda bi, p: (p, 0, 0)),
                  st, st],
        out_specs=[pl.BlockSpec((1, seq, LANES), lambda bi, p: (bi, 0, p)), st, st],
        out_shape=[jax.ShapeDtypeStruct((b, seq, D_RET), BF16),
                   jax.ShapeDtypeStruct((b, N_PAIRS, LANES, LANES), F32),
                   jax.ShapeDtypeStruct((b, N_PAIRS, LANES, LANES), F32)],
        scratch_shapes=[pltpu.VMEM((seq // RET_CHUNK, LANES, LANES), F32),
                        pltpu.VMEM((seq // RET_CHUNK, LANES, LANES), F32)],
        compiler_params=_cp(("parallel", "parallel")),
        name="retention_s%d" % seq,
    )(proj, proj, proj, proj, lgt, gn, sf0, sb0)


def _conv_kernel(b_ref, c_ref, x_ref, w_ref, y_ref, *, seq):
    u = c_ref[0].astype(F32) * x_ref[0].astype(F32)
    row = lax.broadcasted_iota(jnp.int32, (seq, 1), 0)
    prev = jnp.where(row == 0, 0.0, pltpu.roll(u, 1, 0))
    nxt = jnp.where(row == seq - 1, 0.0, pltpu.roll(u, seq - 1, 0))
    y = w_ref[0:1, :] * prev + w_ref[1:2, :] * u + w_ref[2:3, :] * nxt
    y_ref[0] = (b_ref[0].astype(F32) * y).astype(BF16)


def _short_conv(proj, conv_w):
    b, seq, _ = proj.shape
    nblk = D_CONV // LANES
    col = lambda off: pl.BlockSpec((1, seq, LANES), lambda bi, j: (bi, 0, off + j))
    return pl.pallas_call(
        functools.partial(_conv_kernel, seq=seq),
        grid=(b, nblk),
        in_specs=[col(CB_CB), col(CB_CC), col(CB_CX),
                  pl.BlockSpec((3, LANES), lambda bi, j: (0, j))],
        out_specs=pl.BlockSpec((1, seq, LANES), lambda bi, j: (bi, 0, j)),
        out_shape=jax.ShapeDtypeStruct((b, seq, D_CONV), BF16),
        compiler_params=_cp(("parallel", "parallel")),
        name="short_conv_s%d" % seq,
    )(proj, proj, proj, conv_w)


def _na_bias_plan(rows):
    plan = np.full((3, NA_QROWS, NA_KROWS), -1, np.int64)
    for cl, rb in enumerate((0, NA_QROWS, rows - NA_QROWS)):
        ws = int(np.clip(rb - NA_ROWS // 2, 0, rows - NA_KROWS))
        for i in range(NA_QROWS):
            r = rb + i
            r0 = int(np.clip(r - NA_ROWS // 2, 0, rows - NA_ROWS))
            for j in range(NA_KROWS):
                kr = ws + j
                if r0 <= kr < r0 + NA_ROWS:
                    plan[cl, i, j] = kr - r + (NA_ROWS - 1)
    return plan


def _na_bias_kernel(rpb_ref, o_ref, *, plan):
    w = GRID_W
    ndr, ndc = 2 * NA_ROWS - 1, 2 * NA_COLS - 1
    base = pl.program_id(0) * (ndr * ndc)
    c = lax.broadcasted_iota(jnp.int32, (w, w), 0)
    kc = lax.broadcasted_iota(jnp.int32, (w, w), 1)
    dcidx = jnp.clip(kc - c, -(NA_COLS - 1), NA_COLS - 1) + (NA_COLS - 1)
    c0 = jnp.clip(c - NA_COLS // 2, 0, w - NA_COLS)
    col_in = jnp.logical_and(kc >= c0, kc < c0 + NA_COLS)
    neg = jnp.full((w, w), NEG_BIG, F32)
    tiles = []
    for dr in range(ndr):
        t = neg
        for dcv in range(ndc):
            t = jnp.where(dcidx == dcv, rpb_ref[base + dr * ndc + dcv], t)
        tiles.append(jnp.where(col_in, t, NEG_BIG))
    for cl in range(plan.shape[0]):
        for i in range(plan.shape[1]):
            for j in range(0, plan.shape[2], 2):
                pair = [tiles[int(d)] if d >= 0 else neg for d in plan[cl, i, j:j + 2]]
                o_ref[0, cl, i * w:(i + 1) * w, j * w:(j + 2) * w] = jnp.concatenate(pair, axis=1)


def _na_bias(rpb, rows):
    h = rpb.shape[0]
    tq, tk = NA_QROWS * GRID_W, NA_KROWS * GRID_W
    return pl.pallas_call(
        functools.partial(_na_bias_kernel, plan=_na_bias_plan(rows)),
        grid=(h,),
        in_specs=[pl.BlockSpec(memory_space=pltpu.SMEM)],
        out_specs=pl.BlockSpec((1, 3, tq, tk), lambda hh: (hh, 0, 0, 0)),
        out_shape=jax.ShapeDtypeStruct((h, 3, tq, tk), F32),
        compiler_params=_cp(("parallel",)),
        name="na_bias_expand",
    )(rpb.reshape(-1))


def _na_kernel(q_ref, k_ref, v_ref, kc_ref, vc_ref, bias_ref, o_ref, *, rows):
    qb = pl.program_id(2)
    ws = jnp.clip(qb * NA_QROWS - NA_ROWS // 2, 0, rows - NA_KROWS)
    sl = pl.ds(pl.multiple_of(ws * GRID_W, GRID_W), NA_KROWS * GRID_W)
    q = q_ref[0]
    kw = k_ref[0, sl, :]
    vw = v_ref[0, sl, :]
    kc = kc_ref[0]
    vc = vc_ref[0]
    lane = lax.broadcasted_iota(jnp.int32, (1, LANES), 1)
    first = lane < HEAD_DIM
    o = None
    for hh in range(2):
        mk = (first if hh == 0 else jnp.logical_not(first)).astype(BF16)
        qh = q * mk
        s_loc = _dot_nt(qh, kw) + bias_ref[hh, 0]
        s_ctx = _dot_nt(qh, kc)
        m = jnp.maximum(jnp.max(s_loc, axis=-1, keepdims=True), jnp.max(s_ctx, axis=-1, keepdims=True))
        p_loc = jnp.exp(s_loc - m)
        p_ctx = jnp.exp(s_ctx - m)
        l = jnp.sum(p_loc, axis=-1, keepdims=True) + jnp.sum(p_ctx, axis=-1, keepdims=True)
        oh = (_dot(p_loc.astype(BF16), vw) + _dot(p_ctx.astype(BF16), vc)) / l
        o = oh if o is None else jnp.where(first, o, oh)
    o_ref[0] = o.astype(BF16)


def _na(proj, proj_ctx, bias):
    b, seq, _ = proj.shape
    ctx_len = proj_ctx.shape[1]
    rows = seq // GRID_W
    nqb = rows // NA_QROWS
    tq = NA_QROWS * GRID_W

    def cls(qb):
        return jnp.where(qb == 0, 0, jnp.where(qb == nqb - 1, 2, 1))

    return pl.pallas_call(
        functools.partial(_na_kernel, rows=rows),
        grid=(b, N_PAIRS, nqb),
        in_specs=[pl.BlockSpec((1, tq, LANES), lambda bi, p, qb: (bi, qb, CB_NQ + p)),
                  pl.BlockSpec((1, seq, LANES), lambda bi, p, qb: (bi, 0, CB_NK + p)),
                  pl.BlockSpec((1, seq, LANES), lambda bi, p, qb: (bi, 0, CB_NV + p)),
                  pl.BlockSpec((1, ctx_len, LANES), lambda bi, p, qb: (bi, 0, CB_NK + p)),
                  pl.BlockSpec((1, ctx_len, LANES), lambda bi, p, qb: (bi, 0, CB_NV + p)),
                  pl.BlockSpec((2, 1, tq, NA_KROWS * GRID_W), lambda bi, p, qb: (p, cls(qb), 0, 0))],
        out_specs=pl.BlockSpec((1, tq, LANES), lambda bi, p, qb: (bi, qb, p)),
        out_shape=jax.ShapeDtypeStruct((b, seq, D_NA), BF16),
        compiler_params=_cp(("parallel", "parallel", "arbitrary")),
        name="na_attn",
    )(proj, proj, proj, proj_ctx, proj_ctx, bias)


def _ctx_attn_kernel(q_ref, k_ref, v_ref, o_ref):
    q = q_ref[0]
    k = k_ref[0]
    v = v_ref[0]
    lane = lax.broadcasted_iota(jnp.int32, (1, LANES), 1)
    first = lane < HEAD_DIM
    o = None
    for hh in range(2):
        mk = (first if hh == 0 else jnp.logical_not(first)).astype(BF16)
        s = _dot_nt(q * mk, k)
        m = jnp.max(s, axis=-1, keepdims=True)
        p = jnp.exp(s - m)
        l = jnp.sum(p, axis=-1, keepdims=True)
        oh = _dot(p.astype(BF16), v) / l
        o = oh if o is None else jnp.where(first, o, oh)
    o_ref[0] = o.astype(BF16)


def _ctx_attn(proj_ctx):
    b, ctx_len, _ = proj_ctx.shape
    col = lambda off: pl.BlockSpec((1, ctx_len, LANES), lambda bi, p: (bi, 0, off + p))
    return pl.pallas_call(
        _ctx_attn_kernel,
        grid=(b, N_PAIRS),
        in_specs=[col(CB_NQ), col(CB_NK), col(CB_NV)],
        out_specs=pl.BlockSpec((1, ctx_len, LANES), lambda bi, p: (bi, 0, p)),
        out_shape=jax.ShapeDtypeStruct((b, ctx_len, D_NA), BF16),
        compiler_params=_cp(("parallel", "parallel")),
        name="ctx_attn",
    )(proj_ctx, proj_ctx, proj_ctx)


def _outproj_kernel(*refs, route):
    if route:
        (yr_ref, yc_ref, yn_ref, h_ref, w_ref, g1_ref, n2_ref, sh_ref, sc_ref, rw_ref, rb_ref,
         ho_ref, xn_ref, idx_ref, wt_ref) = refs
    else:
        yr_ref, yc_ref, yn_ref, h_ref, w_ref, g1_ref, n2_ref, sh_ref, sc_ref, ho_ref, xn_ref = refs
    ycat = jnp.concatenate([yr_ref[...], yc_ref[...], yn_ref[...]], axis=-1)
    h = h_ref[...] + g1_ref[0] * _dot(ycat, w_ref[...])
    ho_ref[...] = h
    a = _norm_mod(h, n2_ref[...], sh_ref[0], sc_ref[0])
    xn_ref[...] = a.astype(xn_ref.dtype)
    if route:
        a_hi, a_lo = _split_bf16(a)
        r_hi, r_lo = _split_bf16(rw_ref[...])
        logits = _dot_nt(r_hi, a_hi) + _dot_nt(r_hi, a_lo) + _dot_nt(r_lo, a_hi) + rb_ref[...]
        eidx = lax.broadcasted_iota(jnp.int32, logits.shape, 0)
        m1 = jnp.max(logits, axis=0, keepdims=True)
        i1 = jnp.min(jnp.where(logits == m1, eidx, N_EXPERTS), axis=0, keepdims=True)
        rest = jnp.where(eidx == i1, -jnp.inf, logits)
        m2 = jnp.max(rest, axis=0, keepdims=True)
        i2 = jnp.min(jnp.where(rest == m2, eidx, N_EXPERTS), axis=0, keepdims=True)
        e2 = jnp.exp(m2 - m1)
        den = 1.0 + e2
        idx_ref[...] = jnp.concatenate([i1, i2], axis=0)
        wt_ref[...] = jnp.concatenate([1.0 / den, e2 / den], axis=0)


def _out_proj(yr, yc, yn, h2, w_bf, g1, n2g, sh2, sc2, seq, tm, router=None):
    t, d = h2.shape
    tps = seq // tm
    route = router is not None
    row = lambda wdt: pl.BlockSpec((tm, wdt), lambda i: (i, 0))
    mod = pl.BlockSpec((1, 1, d), lambda i: (i // tps, 0, 0))
    in_specs = [row(D_RET), row(D_CONV), row(D_NA), row(d),
                pl.BlockSpec((d, d), lambda i: (0, 0)), mod,
                pl.BlockSpec((1, d), lambda i: (0, 0)), mod, mod]
    args = [yr, yc, yn, h2, w_bf, g1, n2g.reshape(1, d), sh2, sc2]
    out_specs = [row(d), row(d)]
    out_shape = [jax.ShapeDtypeStruct((t, d), F32), jax.ShapeDtypeStruct((t, d), F32 if route else BF16)]
    if route:
        rw, rb = router
        in_specs += [pl.BlockSpec((N_EXPERTS, d), lambda i: (0, 0)),
                     pl.BlockSpec((N_EXPERTS, 1), lambda i: (0, 0))]
        args += [rw.T, rb.reshape(N_EXPERTS, 1)]
        out_specs += [pl.BlockSpec((2, tm), lambda i: (0, i))] * 2
        out_shape += [jax.ShapeDtypeStruct((2, t), jnp.int32), jax.ShapeDtypeStruct((2, t), F32)]
    return pl.pallas_call(
        functools.partial(_outproj_kernel, route=route),
        grid=(t // tm,),
        in_specs=in_specs,
        out_specs=out_specs,
        out_shape=out_shape,
        compiler_params=_cp(("parallel",)),
        name="out_proj_route" if route else "out_proj_t%d" % t,
    )(*args)


def _ffn_kernel(x_ref, h_ref, wg_ref, wu_ref, wo_ref, g2_ref, o_ref, acc_ref):
    f = pl.program_id(1)

    @pl.when(f == 0)
    def _():
        acc_ref[...] = jnp.zeros_like(acc_ref)

    x = x_ref[...]
    a = _silu(_dot(x, wg_ref[...])) * _dot(x, wu_ref[...])
    acc_ref[...] += _dot(a.astype(BF16), wo_ref[...])

    @pl.when(f == pl.num_programs(1) - 1)
    def _():
        o_ref[...] = h_ref[...] + g2_ref[0] * acc_ref[...]


def _ffn(xn, h2, w_in_bf, w_out_bf, g2, seq, tm, tf):
    t, d = h2.shape
    ff = w_out_bf.shape[0]
    nf = ff // tf
    tps = seq // tm
    return pl.pallas_call(
        _ffn_kernel,
        grid=(t // tm, nf),
        in_specs=[pl.BlockSpec((tm, d), lambda i, f: (i, 0)),
                  pl.BlockSpec((tm, d), lambda i, f: (i, 0)),
                  pl.BlockSpec((d, tf), lambda i, f: (0, f)),
                  pl.BlockSpec((d, tf), lambda i, f: (0, nf + f)),
                  pl.BlockSpec((tf, d), lambda i, f: (f, 0)),
                  pl.BlockSpec((1, 1, d), lambda i, f: (i // tps, 0, 0))],
        out_specs=pl.BlockSpec((tm, d), lambda i, f: (i, 0)),
        out_shape=jax.ShapeDtypeStruct((t, d), F32),
        scratch_shapes=[pltpu.VMEM((tm, d), F32)],
        compiler_params=_cp(("parallel", "arbitrary")),
        name="ffn_t%d" % t,
    )(xn, h2, w_in_bf, w_in_bf, w_out_bf, g2)


def _dispatch_kernel(s0_ref, s1_ref, zs_ref, zv_ref, x_ref, xs_ref, zbuf_ref, sem, *, rows):
    @pl.when(pl.program_id(0) == 0)
    def _():
        zbuf_ref[...] = jnp.zeros_like(zbuf_ref)
        for e in range(2 * N_EXPERTS):
            @pl.when(zv_ref[e] != 0)
            def _():
                dst = xs_ref.at[pl.ds(pl.multiple_of(zs_ref[e], MOE_TS), MOE_TS)]
                pltpu.make_async_copy(zbuf_ref, dst, sem.at[1]).start()
        for e in range(2 * N_EXPERTS):
            @pl.when(zv_ref[e] != 0)
            def _():
                pltpu.make_async_copy(zbuf_ref, xs_ref.at[pl.ds(0, MOE_TS)], sem.at[1]).wait()

    def issue(g, carry):
        for u in range(DMA_UNROLL):
            r = g * DMA_UNROLL + u
            src = x_ref.at[pl.ds(r, 1)]
            pltpu.make_async_copy(src, xs_ref.at[pl.ds(s0_ref[r], 1)], sem.at[0]).start(priority=u % 2)
            pltpu.make_async_copy(src, xs_ref.at[pl.ds(s1_ref[r], 1)], sem.at[0]).start(priority=(u + 1) % 2)
        return carry

    lax.fori_loop(0, rows // DMA_UNROLL, issue, 0)
    for _ in range(2):
        pltpu.make_async_copy(x_ref, xs_ref.at[pl.ds(0, rows)], sem.at[0]).wait()


def _dispatch(xn, slot0, slot1, zplan, n_slots, rows):
    t, d = xn.shape
    zstart, zvalid = zplan
    return pl.pallas_call(
        functools.partial(_dispatch_kernel, rows=rows),
        grid=(t // rows,),
        in_specs=[pl.BlockSpec((rows,), lambda i: (i,), memory_space=pltpu.SMEM),
                  pl.BlockSpec((rows,), lambda i: (i,), memory_space=pltpu.SMEM),
                  pl.BlockSpec((2 * N_EXPERTS,), lambda i: (0,), memory_space=pltpu.SMEM),
                  pl.BlockSpec((2 * N_EXPERTS,), lambda i: (0,), memory_space=pltpu.SMEM),
                  pl.BlockSpec((rows, d), lambda i: (i, 0))],
        out_specs=pl.BlockSpec(memory_space=pl.ANY),
        out_shape=jax.ShapeDtypeStruct((n_slots, d), xn.dtype),
        scratch_shapes=[pltpu.VMEM((MOE_TS, d), xn.dtype), pltpu.SemaphoreType.DMA((2,))],
        compiler_params=_cp(("arbitrary",)),
        name="moe_dispatch",
    )(slot0, slot1, zstart, zvalid, xn)


def _expert_kernel(be_ref, nu_ref, x_ref, wg_ref, wu_ref, wo_ref, o_ref, acc_ref):
    j = pl.program_id(0)
    f = pl.program_id(1)
    last = pl.num_programs(1) - 1
    used = j < nu_ref[0]

    @pl.when(jnp.logical_and(used, f == 0))
    def _():
        acc_ref[...] = jnp.zeros_like(acc_ref)

    @pl.when(used)
    def _():
        x = x_ref[...].astype(BF16)
        a = _silu(_dot(x, wg_ref[0])) * _dot(x, wu_ref[0])
        acc_ref[...] += _dot(a.astype(BF16), wo_ref[0])

    @pl.when(jnp.logical_and(used, f == last))
    def _():
        o_ref[...] = acc_ref[...]

    @pl.when(jnp.logical_and(jnp.logical_not(used), f == last))
    def _():
        o_ref[...] = jnp.zeros_like(o_ref)


def _experts(xs, block_expert, n_used, w_in_bf, w_out_bf, tf):
    n, d = xs.shape
    ff = w_out_bf.shape[1]
    nf = ff // tf
    ts = MOE_TS

    def fe(j, f):
        return jnp.where(j % 2 == 0, f, nf - 1 - f)

    grid_spec = pltpu.PrefetchScalarGridSpec(
        num_scalar_prefetch=2,
        grid=(n // ts, nf),
        in_specs=[pl.BlockSpec((ts, d), lambda j, f, be, nu: (j, 0)),
                  pl.BlockSpec((1, d, tf), lambda j, f, be, nu: (be[j], 0, fe(j, f))),
                  pl.BlockSpec((1, d, tf), lambda j, f, be, nu: (be[j], 0, nf + fe(j, f))),
                  pl.BlockSpec((1, tf, d), lambda j, f, be, nu: (be[j], fe(j, f), 0))],
        out_specs=pl.BlockSpec((ts, d), lambda j, f, be, nu: (j, 0)),
        scratch_shapes=[pltpu.VMEM((ts, d), F32)],
    )
    return pl.pallas_call(
        _expert_kernel,
        grid_spec=grid_spec,
        out_shape=jax.ShapeDtypeStruct((n, d), F32),
        compiler_params=_cp(("arbitrary", "arbitrary")),
        name="moe_experts",
    )(block_expert, n_used, xs, w_in_bf, w_in_bf, w_out_bf)


def _combine_kernel(p0_ref, p1_ref, ys_ref, h_ref, w0_ref, w1_ref, g2_ref, fg_ref, o_ref,
                    y0_ref, y1_ref, sem, *, rows):
    def issue(g, carry):
        for u in range(DMA_UNROLL):
            r = g * DMA_UNROLL + u
            pltpu.make_async_copy(ys_ref.at[pl.ds(p0_ref[r], 1)], y0_ref.at[pl.ds(r, 1)],
                                  sem.at[0]).start(priority=u % 2)
            pltpu.make_async_copy(ys_ref.at[pl.ds(p1_ref[r], 1)], y1_ref.at[pl.ds(r, 1)],
                                  sem.at[1]).start(priority=(u + 1) % 2)
        return carry

    lax.fori_loop(0, rows // DMA_UNROLL, issue, 0)
    pltpu.make_async_copy(ys_ref.at[pl.ds(0, rows)], y0_ref, sem.at[0]).wait()
    pltpu.make_async_copy(ys_ref.at[pl.ds(0, rows)], y1_ref, sem.at[1]).wait()
    h = h_ref[...] + g2_ref[0] * (w0_ref[...] * y0_ref[...] + w1_ref[...] * y1_ref[...])
    ms = jnp.mean(h * h, axis=-1, keepdims=True)
    o_ref[...] = h * lax.rsqrt(ms + NORM_EPS) * fg_ref[...]


def _combine(pos0, pos1, ys, h2, w0, w1, g2, final_g, seq, rows):
    t, d = h2.shape
    tps = seq // rows
    blk = pl.BlockSpec((rows, d), lambda i: (i, 0))
    col = pl.BlockSpec((rows, 1), lambda i: (i, 0))
    return pl.pallas_call(
        functools.partial(_combine_kernel, rows=rows),
        grid=(t // rows,),
        in_specs=[pl.BlockSpec((rows,), lambda i: (i,), memory_space=pltpu.SMEM),
                  pl.BlockSpec((rows,), lambda i: (i,), memory_space=pltpu.SMEM),
                  pl.BlockSpec(memory_space=pl.ANY),
                  blk, col, col,
                  pl.BlockSpec((1, 1, d), lambda i: (i // tps, 0, 0)),
                  pl.BlockSpec((1, d), lambda i: (0, 0))],
        out_specs=blk,
        out_shape=jax.ShapeDtypeStruct((t, d), F32),
        scratch_shapes=[pltpu.VMEM((rows, d), F32), pltpu.VMEM((rows, d), F32),
                        pltpu.SemaphoreType.DMA((2,))],
        compiler_params=_cp(("arbitrary",)),
        name="moe_combine_norm",
    )(pos0, pos1, ys, h2, w0, w1, g2, final_g.reshape(1, d))


def _route_plan(idx):
    t = idx.shape[1]
    ts = MOE_TS
    n_slots = 2 * t + N_EXPERTS * ts
    e_flat = idx.reshape(-1)
    onehot = (e_flat[:, None] == jnp.arange(N_EXPERTS, dtype=jnp.int32)[None, :]).astype(jnp.int32)
    csum = jnp.cumsum(onehot, axis=0)
    rank = jnp.sum((csum - onehot) * onehot, axis=1)
    counts = csum[-1]
    padded = ((counts + ts - 1) // ts) * ts
    ends = jnp.cumsum(padded)
    offs = ends - padded
    slot = (jnp.sum(onehot * offs[None, :], axis=1) + rank).astype(jnp.int32)
    starts = jnp.arange(n_slots // ts, dtype=jnp.int32) * ts
    block_expert = jnp.minimum(jnp.sum((starts[:, None] >= ends[None, :]).astype(jnp.int32), axis=1),
                               N_EXPERTS - 1).astype(jnp.int32)
    n_used = (ends[-1] // ts).astype(jnp.int32).reshape(1)
    tail = ends[-1] + jnp.arange(N_EXPERTS, dtype=jnp.int32) * ts
    zstart = jnp.concatenate([ends - ts, tail])
    zvalid = jnp.concatenate([padded > 0, tail < n_slots]).astype(jnp.int32)
    zstart = jnp.where(zvalid != 0, zstart, 0).astype(jnp.int32)
    return slot[:t], slot[t:], block_expert, n_used, (zstart, zvalid), n_slots


def kernel(x, c, ctx, c_ctx, ada_w, ada_b, norm1_g, norm2_g, w_in, w_out, ret_decay_logit, ret_gn_g,
           conv_w, na_rpb, ffn_w_in, ffn_w_out, moe_router_w, moe_router_b, moe_w_in, moe_w_out, final_g):
    b, seq, d = x.shape
    ctx_len = ctx.shape[1]
    depth = ada_w.shape[0]
    assert depth == 2, "the final norm is fused into the last (MoE) layer's combine step"
    rows = seq // GRID_W
    t_lat, t_ctx = b * seq, b * ctx_len
    tm_lat = min(1024, seq)
    tm_ctx = ctx_len
    ff_tile = ffn_w_out.shape[1] // 2

    c8 = jnp.zeros((8, d), F32).at[:b].set(c).at[b].set(c_ctx)
    mods = _ada(c8, ada_w, ada_b)
    tables = _rope_tables(seq)

    h = x.reshape(t_lat, d)
    hc = ctx.reshape(t_ctx, d)
    zero_state = jnp.zeros((b, N_PAIRS, LANES, LANES), F32)
    out = None
    for layer in range(depth):
        update_ctx = layer < depth - 1
        m = mods[layer].reshape(8, 6, d)
        lat = [m[:b, i].reshape(b, 1, d) for i in range(6)]
        cx = [jnp.broadcast_to(m[b, i].reshape(1, 1, d), (b, 1, d)) for i in range(6)]
        w_in_bf = w_in[layer].astype(BF16)
        w_out_bf = w_out[layer].astype(BF16)
        lgt = ret_decay_logit[layer].reshape(2, N_PAIRS, LANES // HEAD_DIM)
        lgt = jnp.repeat(lgt, HEAD_DIM, axis=2).transpose(1, 0, 2)
        gn = ret_gn_g[layer].reshape(N_PAIRS, 1, LANES)

        proj_c = _in_proj(hc, norm1_g[layer], cx[0], cx[1], w_in_bf, None, ctx_len, tm_ctx)
        proj_c = proj_c.reshape(b, ctx_len, D_IN_PROJ)
        proj = _in_proj(h, norm1_g[layer], lat[0], lat[1], w_in_bf, tables, seq, tm_lat)
        proj = proj.reshape(b, seq, D_IN_PROJ)

        y_ret_c, s_fwd, s_bwd = _retention(proj_c, lgt, gn, zero_state, zero_state)
        y_ret, _, _ = _retention(proj, lgt, gn, s_fwd, s_bwd)
        y_conv = _short_conv(proj, conv_w[layer])
        y_na = _na(proj, proj_c, _na_bias(na_rpb[layer], rows))

        if layer % 2 == 0:
            h, xn = _out_proj(y_ret.reshape(t_lat, D_RET), y_conv.reshape(t_lat, D_CONV),
                              y_na.reshape(t_lat, D_NA), h, w_out_bf, lat[2], norm2_g[layer],
                              lat[3], lat[4], seq, tm_lat)
            fw_in = ffn_w_in[layer // 2].astype(BF16)
            fw_out = ffn_w_out[layer // 2].astype(BF16)
            h = _ffn(xn, h, fw_in, fw_out, lat[5], seq, min(512, seq), ff_tile)
            if update_ctx:
                y_conv_c = _short_conv(proj_c, conv_w[layer])
                y_na_c = _ctx_attn(proj_c)
                hc, xnc = _out_proj(y_ret_c.reshape(t_ctx, D_RET), y_conv_c.reshape(t_ctx, D_CONV),
                                    y_na_c.reshape(t_ctx, D_NA), hc, w_out_bf, cx[2], norm2_g[layer],
                                    cx[3], cx[4], ctx_len, tm_ctx)
                hc = _ffn(xnc, hc, fw_in, fw_out, cx[5], ctx_len, tm_ctx, ff_tile)
        else:
            e = layer // 2
            h, xn, idx, wts = _out_proj(y_ret.reshape(t_lat, D_RET), y_conv.reshape(t_lat, D_CONV),
                                        y_na.reshape(t_lat, D_NA), h, w_out_bf, lat[2], norm2_g[layer],
                                        lat[3], lat[4], seq, tm_lat,
                                        router=(moe_router_w[e], moe_router_b[e]))
            pos0, pos1, block_expert, n_used, zstart, n_slots = _route_plan(idx)
            xs = _dispatch(xn, pos0, pos1, zstart, n_slots, min(1024, seq))
            ys = _experts(xs, block_expert, n_used, moe_w_in[e].astype(BF16), moe_w_out[e].astype(BF16), ff_tile)
            out = _combine(pos0, pos1, ys, h, wts[0].reshape(t_lat, 1), wts[1].reshape(t_lat, 1),
                           lat[5], final_g, seq, min(512, seq))
            h = out
    return out.reshape(b, seq, d)
```

```python
import functools

import numpy as np
import jax
import jax.numpy as jnp
from jax import lax
from jax.experimental import pallas as pl
from jax.experimental.pallas import tpu as pltpu

F32 = jnp.float32
BF16 = jnp.bfloat16

LANES = 128
HEAD_DIM = 64
GRID_W = 64
N_CONV_GROUPS = 4
N_RET_HEADS = 6
N_NA_HEADS = 6
D_RET = N_RET_HEADS * HEAD_DIM
D_CONV = N_CONV_GROUPS * HEAD_DIM
D_NA = N_NA_HEADS * HEAD_DIM
D_IN_PROJ = 4 * D_RET + 3 * D_CONV + 3 * D_NA
N_PAIRS = D_RET // LANES
NA_ROWS = 8
NA_COLS = 16
N_EXPERTS = 8
ROPE_BASE = 10000.0
NORM_EPS = 1e-6
NEG_BIG = -1e30

RET_CHUNK = 256
NA_QROWS = 4
NA_KROWS = 12
MOE_TS = 512
DMA_UNROLL = 8
FFN_CHUNKS = 2
IN_PROJ_SUB = 512
MXU_COLS = 256
IN_PROJ_CHUNK = 3 * MXU_COLS
VMEM_LIMIT = 56 * 1024 * 1024

CB_RQ, CB_RK, CB_RV, CB_RG = 0, 3, 6, 9
CB_CB, CB_CC, CB_CX = 12, 14, 16
CB_NQ, CB_NK, CB_NV = 18, 21, 24


def _cp(sem, vmem=VMEM_LIMIT):
    return pltpu.CompilerParams(dimension_semantics=sem, vmem_limit_bytes=vmem)


def _silu(x):
    return x * (1.0 / (1.0 + jnp.exp(-x)))


def _dot(a, b):
    return jnp.dot(a, b, preferred_element_type=F32)


def _dot_nt(a, b):
    return lax.dot_general(a, b, (((1,), (1,)), ((), ())), preferred_element_type=F32)


def _dot_tn(a, b):
    return lax.dot_general(a, b, (((0,), (0,)), ((), ())), preferred_element_type=F32)


def _split_bf16(x):
    hi = x.astype(BF16)
    lo = (x - hi.astype(F32)).astype(BF16)
    return hi, lo


def _ada_kernel(c_ref, w_ref, b_ref, o_ref):
    x = _silu(c_ref[...]).astype(BF16)
    o_ref[0] = _dot(x, w_ref[0].astype(BF16)) + b_ref[0]


def _ada(c8, ada_w, ada_b):
    depth, d, n = ada_w.shape
    tn = n // 4
    return pl.pallas_call(
        _ada_kernel,
        grid=(depth, n // tn),
        in_specs=[pl.BlockSpec((8, d), lambda l, j: (0, 0)),
                  pl.BlockSpec((1, d, tn), lambda l, j: (l, 0, j)),
                  pl.BlockSpec((1, 1, tn), lambda l, j: (l, 0, j))],
        out_specs=pl.BlockSpec((1, 8, tn), lambda l, j: (l, 0, j)),
        out_shape=jax.ShapeDtypeStruct((depth, 8, n), F32),
        compiler_params=_cp(("parallel", "parallel")),
        name="ada_mod",
    )(c8, ada_w, ada_b.reshape(depth, 1, n))


def _norm_mod(x, g, sh, sc):
    ms = jnp.mean(x * x, axis=-1, keepdims=True)
    y = x * lax.rsqrt(ms + NORM_EPS) * g
    return y * (1.0 + sc) + sh


def _inproj_kernel(*refs, rope):
    if rope:
        h_ref, g_ref, sh_ref, sc_ref, w_ref, cos_ref, sa_ref, sb_ref, o_ref = refs
    else:
        h_ref, g_ref, sh_ref, sc_ref, w_ref, o_ref = refs
    tm = h_ref.shape[0]
    sub = min(IN_PROJ_SUB, tm)
    cw = IN_PROJ_CHUNK
    for r in range(tm // sub):
        rs = slice(r * sub, (r + 1) * sub)
        xn = _norm_mod(h_ref[rs, :], g_ref[...], sh_ref[0], sc_ref[0]).astype(BF16)
        for c0 in range(0, D_IN_PROJ, cw):
            c1 = min(c0 + cw, D_IN_PROJ)
            acc = _dot(xn, w_ref[:, c0:c1])
            for j in range((c1 - c0) // LANES):
                blk = acc[:, j * LANES:(j + 1) * LANES]
                cb = c0 // LANES + j
                if rope and cb < CB_RV:
                    blk = (blk * cos_ref[rs, :] + pltpu.roll(blk, 16, 1) * sa_ref[rs, :]
                           + pltpu.roll(blk, LANES - 16, 1) * sb_ref[rs, :])
                if CB_RK <= cb < CB_RV or CB_NQ <= cb < CB_NK:
                    blk = blk * (HEAD_DIM ** -0.5)
                o_ref[rs, cb * LANES:(cb + 1) * LANES] = blk.astype(BF16)


def _in_proj(h2, g, sh, sc, w_bf, tables, seq, tm):
    t, d = h2.shape
    tiles_per_seq = seq // tm
    rope = tables is not None
    in_specs = [pl.BlockSpec((tm, d), lambda i: (i, 0)),
                pl.BlockSpec((1, d), lambda i: (0, 0)),
                pl.BlockSpec((1, 1, d), lambda i: (i // tiles_per_seq, 0, 0)),
                pl.BlockSpec((1, 1, d), lambda i: (i // tiles_per_seq, 0, 0)),
                pl.BlockSpec((d, D_IN_PROJ), lambda i: (0, 0))]
    args = [h2, g.reshape(1, d), sh, sc, w_bf]
    if rope:
        in_specs += [pl.BlockSpec((tm, LANES), lambda i: (i % tiles_per_seq, 0))] * 3
        args += list(tables)
    return pl.pallas_call(
        functools.partial(_inproj_kernel, rope=rope),
        grid=(t // tm,),
        in_specs=in_specs,
        out_specs=pl.BlockSpec((tm, D_IN_PROJ), lambda i: (i, 0)),
        out_shape=jax.ShapeDtypeStruct((t, D_IN_PROJ), BF16),
        compiler_params=_cp(("parallel",)),
        name="in_proj_rope" if rope else "in_proj_ctx",
    )(*args)


def _rope_tables(seq):
    t = np.arange(seq)
    row = (t // GRID_W).astype(np.float32)
    col = (t % GRID_W).astype(np.float32)
    n_freq = HEAD_DIM // 4
    inv_freq = (ROPE_BASE ** (-np.arange(n_freq, dtype=np.float32) / n_freq)).astype(np.float32)
    ang_r = row[:, None] * inv_freq
    ang_c = col[:, None] * inv_freq
    cos_h = np.concatenate([np.cos(ang_r), np.cos(ang_r), np.cos(ang_c), np.cos(ang_c)], axis=1)
    sin_h = np.concatenate([np.sin(ang_r), np.sin(ang_r), np.sin(ang_c), np.sin(ang_c)], axis=1)
    lane = np.arange(HEAD_DIM)
    second = (lane % 32) >= 16
    sa = np.where(second[None, :], sin_h, 0.0)
    sb = np.where(second[None, :], 0.0, -sin_h)
    tile2 = lambda a: jnp.asarray(np.concatenate([a, a], axis=1), F32)
    return tile2(cos_h), tile2(sa), tile2(sb)


def _ret_kernel(q_ref, k_ref, v_ref, g_ref, lgt_ref, gn_ref, sf0_ref, sb0_ref,
                y_ref, sfo_ref, sbo_ref, sfs_ref, sbs_ref, *, seq):
    c = RET_CHUNK
    n_chunks = seq // c
    lg = jax.nn.log_sigmoid(lgt_ref[0])
    lgf, lgb = lg[0:1, :], lg[1:2, :]
    pos = lax.broadcasted_iota(jnp.int32, (c, 1), 0).astype(F32)
    dkf = jnp.exp(lgf * (c - 1.0 - pos))
    dkb = jnp.exp(lgb * pos)
    dqf = jnp.exp(lgf * (pos + 1.0))
    dqb = jnp.exp(lgb * (c - pos))
    cdf = jnp.exp(lgf * float(c))
    cdb = jnp.exp(lgb * float(c))
    lane = lax.broadcasted_iota(jnp.int32, (1, LANES), 1)
    first = lane < HEAD_DIM
    ri = lax.broadcasted_iota(jnp.int32, (LANES, LANES), 0)
    ci = lax.broadcasted_iota(jnp.int32, (LANES, LANES), 1)
    same = (ri < HEAD_DIM) == (ci < HEAD_DIM)
    bd = same.astype(F32)
    avg = (bd * (1.0 / HEAD_DIM)).astype(BF16)
    ii = lax.broadcasted_iota(jnp.int32, (c, c), 0)
    jj = lax.broadcasted_iota(jnp.int32, (c, c), 1)
    dif = (ii - jj).astype(F32)

    def decay_mask(h0):
        lf = lgf[:, h0:h0 + 1]
        lb = lgb[:, h0:h0 + 1]
        return jnp.where(dif > 0, jnp.exp(lf * jnp.maximum(dif, 0.0)),
                         jnp.where(dif < 0, jnp.exp(lb * jnp.maximum(-dif, 0.0)), 2.0))

    dm = (decay_mask(0), decay_mask(HEAD_DIM))
    mfirst = first.astype(BF16)
    msecond = (1.0 - first.astype(F32)).astype(BF16)

    unroll = min(2, n_chunks)

    def incr(n, carry):
        sl = pl.ds(pl.multiple_of(n * c, c), c)
        kf = k_ref[0, sl, :].astype(F32)
        v = v_ref[0, sl, :]
        sfs_ref[n] = _dot_tn((kf * dkf).astype(BF16), v) * bd
        sbs_ref[n] = _dot_tn((kf * dkb).astype(BF16), v) * bd
        return carry

    lax.fori_loop(0, n_chunks, incr, 0, unroll=unroll)

    def scan_f(n, s):
        u = sfs_ref[n]
        sfs_ref[n] = s
        return cdf * s + u

    def scan_b(i, s):
        n = n_chunks - 1 - i
        u = sbs_ref[n]
        sbs_ref[n] = s
        return cdb * s + u

    sfo_ref[0, 0] = lax.fori_loop(0, n_chunks, scan_f, sf0_ref[0, 0])
    sbo_ref[0, 0] = lax.fori_loop(0, n_chunks, scan_b, sb0_ref[0, 0])

    def outp(n, carry):
        sl = pl.ds(pl.multiple_of(n * c, c), c)
        q = q_ref[0, sl, :]
        k = k_ref[0, sl, :]
        v = v_ref[0, sl, :]
        qf = q.astype(F32)
        o = None
        for hh, mk in enumerate((mfirst, msecond)):
            s = _dot_nt(q * mk, k) * dm[hh]
            oh = _dot(s.astype(BF16), v)
            o = oh if o is None else jnp.where(first, o, oh)
        o = o + _dot((qf * dqf).astype(BF16), sfs_ref[n].astype(BF16))
        o = o + _dot((qf * dqb).astype(BF16), sbs_ref[n].astype(BF16))
        hi, lo = _split_bf16(o)
        mu = _dot(hi, avg) + _dot(lo, avg)
        dlt = o - mu
        hi, lo = _split_bf16(dlt * dlt)
        var = _dot(hi, avg) + _dot(lo, avg)
        on = dlt * lax.rsqrt(var + NORM_EPS) * gn_ref[0]
        y_ref[0, sl, :] = (_silu(g_ref[0, sl, :].astype(F32)) * on).astype(BF16)
        return carry

    lax.fori_loop(0, n_chunks, outp, 0, unroll=unroll)


def _retention(proj, lgt, gn, sf0, sb0):
    b, seq, _ = proj.shape
    col = lambda off: pl.BlockSpec((1, seq, LANES), lambda bi, p: (bi, 0, off + p))
    st = pl.BlockSpec((1, 1, LANES, LANES), lambda bi, p: (bi, p, 0, 0))
    return pl.pallas_call(
        functools.partial(_ret_kernel, seq=seq),
        grid=(b, N_PAIRS),
        in_specs=[col(CB_RQ), col(CB_RK), col(CB_RV), col(CB_RG),
                  pl.BlockSpec((1, 2, LANES), lambda bi, p: (p, 0, 0)),
                  pl.BlockSpec((1, 1, LANES), lambda bi, p: (p, 0, 0)),
                  st, st],
        out_specs=[pl.BlockSpec((1, seq, LANES), lambda bi, p: (bi, 0, p)), st, st],
        out_shape=[jax.ShapeDtypeStruct((b, seq, D_RET), BF16),
                   jax.ShapeDtypeStruct((b, N_PAIRS, LANES, LANES), F32),
                   jax.ShapeDtypeStruct((b, N_PAIRS, LANES, LANES), F32)],
        scratch_shapes=[pltpu.VMEM((seq // RET_CHUNK, LANES, LANES), F32),
                        pltpu.VMEM((seq // RET_CHUNK, LANES, LANES), F32)],
        compiler_params=_cp(("parallel", "parallel")),
        name="retention_s%d" % seq,
    )(proj, proj, proj, proj, lgt, gn, sf0, sb0)


def _conv_kernel(b_ref, c_ref, x_ref, w_ref, y_ref, *, seq):
    u = c_ref[0].astype(F32) * x_ref[0].astype(F32)
    row = lax.broadcasted_iota(jnp.int32, (seq, 1), 0)
    prev = jnp.where(row == 0, 0.0, pltpu.roll(u, 1, 0))
    nxt = jnp.where(row == seq - 1, 0.0, pltpu.roll(u, seq - 1, 0))
    y = w_ref[0:1, :] * prev + w_ref[1:2, :] * u + w_ref[2:3, :] * nxt
    y_ref[0] = (b_ref[0].astype(F32) * y).astype(BF16)


def _short_conv(proj, conv_w):
    b, seq, _ = proj.shape
    nblk = D_CONV // LANES
    col = lambda off: pl.BlockSpec((1, seq, LANES), lambda bi, j: (bi, 0, off + j))
    return pl.pallas_call(
        functools.partial(_conv_kernel, seq=seq),
        grid=(b, nblk),
        in_specs=[col(CB_CB), col(CB_CC), col(CB_CX),
                  pl.BlockSpec((3, LANES), lambda bi, j: (0, j))],
        out_specs=pl.BlockSpec((1, seq, LANES), lambda bi, j: (bi, 0, j)),
        out_shape=jax.ShapeDtypeStruct((b, seq, D_CONV), BF16),
        compiler_params=_cp(("parallel", "parallel")),
        name="short_conv_s%d" % seq,
    )(proj, proj, proj, conv_w)


def _na_bias_plan(rows):
    plan = np.full((3, NA_QROWS, NA_KROWS), -1, np.int64)
    for cl, rb in enumerate((0, NA_QROWS, rows - NA_QROWS)):
        ws = int(np.clip(rb - NA_ROWS // 2, 0, rows - NA_KROWS))
        for i in range(NA_QROWS):
            r = rb + i
            r0 = int(np.clip(r - NA_ROWS // 2, 0, rows - NA_ROWS))
            for j in range(NA_KROWS):
                kr = ws + j
                if r0 <= kr < r0 + NA_ROWS:
                    plan[cl, i, j] = kr - r + (NA_ROWS - 1)
    return plan


def _na_bias_kernel(rpb_ref, o_ref, *, plan):
    w = GRID_W
    ndr, ndc = 2 * NA_ROWS - 1, 2 * NA_COLS - 1
    base = pl.program_id(0) * (ndr * ndc)
    c = lax.broadcasted_iota(jnp.int32, (w, w), 0)
    kc = lax.broadcasted_iota(jnp.int32, (w, w), 1)
    dcidx = jnp.clip(kc - c, -(NA_COLS - 1), NA_COLS - 1) + (NA_COLS - 1)
    c0 = jnp.clip(c - NA_COLS // 2, 0, w - NA_COLS)
    col_in = jnp.logical_and(kc >= c0, kc < c0 + NA_COLS)
    neg = jnp.full((w, w), NEG_BIG, F32)
    tiles = []
    for dr in range(ndr):
        t = neg
        for dcv in range(ndc):
            t = jnp.where(dcidx == dcv, rpb_ref[base + dr * ndc + dcv], t)
        tiles.append(jnp.where(col_in, t, NEG_BIG))
    for cl in range(plan.shape[0]):
        for i in range(plan.shape[1]):
            for j in range(0, plan.shape[2], 2):
                pair = [tiles[int(d)] if d >= 0 else neg for d in plan[cl, i, j:j + 2]]
                o_ref[0, cl, i * w:(i + 1) * w, j * w:(j + 2) * w] = jnp.concatenate(pair, axis=1)


def _na_bias(rpb, rows):
    h = rpb.shape[0]
    tq, tk = NA_QROWS * GRID_W, NA_KROWS * GRID_W
    return pl.pallas_call(
        functools.partial(_na_bias_kernel, plan=_na_bias_plan(rows)),
        grid=(h,),
        in_specs=[pl.BlockSpec(memory_space=pltpu.SMEM)],
        out_specs=pl.BlockSpec((1, 3, tq, tk), lambda hh: (hh, 0, 0, 0)),
        out_shape=jax.ShapeDtypeStruct((h, 3, tq, tk), F32),
        compiler_params=_cp(("parallel",)),
        name="na_bias_expand",
    )(rpb.reshape(-1))


def _na_kernel(q_ref, k_ref, v_ref, kc_ref, vc_ref, bias_ref, o_ref, *, rows):
    qb = pl.program_id(2)
    ws = jnp.clip(qb * NA_QROWS - NA_ROWS // 2, 0, rows - NA_KROWS)
    sl = pl.ds(pl.multiple_of(ws * GRID_W, GRID_W), NA_KROWS * GRID_W)
    q = q_ref[0]
    kw = k_ref[0, sl, :]
    vw = v_ref[0, sl, :]
    kc = kc_ref[0]
    vc = vc_ref[0]
    lane = lax.broadcasted_iota(jnp.int32, (1, LANES), 1)
    first = lane < HEAD_DIM
    o = None
    for hh in range(2):
        mk = (first if hh == 0 else jnp.logical_not(first)).astype(BF16)
        qh = q * mk
        s_loc = _dot_nt(qh, kw) + bias_ref[hh, 0]
        s_ctx = _dot_nt(qh, kc)
        m = jnp.maximum(jnp.max(s_loc, axis=-1, keepdims=True), jnp.max(s_ctx, axis=-1, keepdims=True))
        p_loc = jnp.exp(s_loc - m)
        p_ctx = jnp.exp(s_ctx - m)
        l = jnp.sum(p_loc, axis=-1, keepdims=True) + jnp.sum(p_ctx, axis=-1, keepdims=True)
        oh = (_dot(p_loc.astype(BF16), vw) + _dot(p_ctx.astype(BF16), vc)) / l
        o = oh if o is None else jnp.where(first, o, oh)
    o_ref[0] = o.astype(BF16)


def _na(proj, proj_ctx, bias):
    b, seq, _ = proj.shape
    ctx_len = proj_ctx.shape[1]
    rows = seq // GRID_W
    nqb = rows // NA_QROWS
    tq = NA_QROWS * GRID_W

    def cls(qb):
        return jnp.where(qb == 0, 0, jnp.where(qb == nqb - 1, 2, 1))

    return pl.pallas_call(
        functools.partial(_na_kernel, rows=rows),
        grid=(b, N_PAIRS, nqb),
        in_specs=[pl.BlockSpec((1, tq, LANES), lambda bi, p, qb: (bi, qb, CB_NQ + p)),
                  pl.BlockSpec((1, seq, LANES), lambda bi, p, qb: (bi, 0, CB_NK + p)),
                  pl.BlockSpec((1, seq, LANES), lambda bi, p, qb: (bi, 0, CB_NV + p)),
                  pl.BlockSpec((1, ctx_len, LANES), lambda bi, p, qb: (bi, 0, CB_NK + p)),
                  pl.BlockSpec((1, ctx_len, LANES), lambda bi, p, qb: (bi, 0, CB_NV + p)),
                  pl.BlockSpec((2, 1, tq, NA_KROWS * GRID_W), lambda bi, p, qb: (p, cls(qb), 0, 0))],
        out_specs=pl.BlockSpec((1, tq, LANES), lambda bi, p, qb: (bi, qb, p)),
        out_shape=jax.ShapeDtypeStruct((b, seq, D_NA), BF16),
        compiler_params=_cp(("parallel", "parallel", "arbitrary")),
        name="na_attn",
    )(proj, proj, proj, proj_ctx, proj_ctx, bias)


def _ctx_attn_kernel(q_ref, k_ref, v_ref, o_ref):
    q = q_ref[0]
    k = k_ref[0]
    v = v_ref[0]
    lane = lax.broadcasted_iota(jnp.int32, (1, LANES), 1)
    first = lane < HEAD_DIM
    o = None
    for hh in range(2):
        mk = (first if hh == 0 else jnp.logical_not(first)).astype(BF16)
        s = _dot_nt(q * mk, k)
        m = jnp.max(s, axis=-1, keepdims=True)
        p = jnp.exp(s - m)
        l = jnp.sum(p, axis=-1, keepdims=True)
        oh = _dot(p.astype(BF16), v) / l
        o = oh if o is None else jnp.where(first, o, oh)
    o_ref[0] = o.astype(BF16)


def _ctx_attn(proj_ctx):
    b, ctx_len, _ = proj_ctx.shape
    col = lambda off: pl.BlockSpec((1, ctx_len, LANES), lambda bi, p: (bi, 0, off + p))
    return pl.pallas_call(
        _ctx_attn_kernel,
        grid=(b, N_PAIRS),
        in_specs=[col(CB_NQ), col(CB_NK), col(CB_NV)],
        out_specs=pl.BlockSpec((1, ctx_len, LANES), lambda bi, p: (bi, 0, p)),
        out_shape=jax.ShapeDtypeStruct((b, ctx_len, D_NA), BF16),
        compiler_params=_cp(("parallel", "parallel")),
        name="ctx_attn",
    )(proj_ctx, proj_ctx, proj_ctx)


def _outproj_kernel(*refs, route):
    if route:
        (yr_ref, yc_ref, yn_ref, h_ref, w_ref, g1_ref, n2_ref, sh_ref, sc_ref, rw_ref, rb_ref,
         ho_ref, xn_ref, idx_ref, wt_ref) = refs
    else:
        yr_ref, yc_ref, yn_ref, h_ref, w_ref, g1_ref, n2_ref, sh_ref, sc_ref, ho_ref, xn_ref = refs
    ycat = jnp.concatenate([yr_ref[...], yc_ref[...], yn_ref[...]], axis=-1)
    h = h_ref[...] + g1_ref[0] * _dot(ycat, w_ref[...])
    ho_ref[...] = h
    a = _norm_mod(h, n2_ref[...], sh_ref[0], sc_ref[0])
    xn_ref[...] = a.astype(xn_ref.dtype)
    if route:
        a_hi, a_lo = _split_bf16(a)
        r_hi, r_lo = _split_bf16(rw_ref[...])
        logits = _dot_nt(r_hi, a_hi) + _dot_nt(r_hi, a_lo) + _dot_nt(r_lo, a_hi) + rb_ref[...]
        eidx = lax.broadcasted_iota(jnp.int32, logits.shape, 0)
        m1 = jnp.max(logits, axis=0, keepdims=True)
        i1 = jnp.min(jnp.where(logits == m1, eidx, N_EXPERTS), axis=0, keepdims=True)
        rest = jnp.where(eidx == i1, -jnp.inf, logits)
        m2 = jnp.max(rest, axis=0, keepdims=True)
        i2 = jnp.min(jnp.where(rest == m2, eidx, N_EXPERTS), axis=0, keepdims=True)
        e2 = jnp.exp(m2 - m1)
        den = 1.0 + e2
        idx_ref[...] = jnp.concatenate([i1, i2], axis=0)
        wt_ref[...] = jnp.concatenate([1.0 / den, e2 / den], axis=0)


def _out_proj(yr, yc, yn, h2, w_bf, g1, n2g, sh2, sc2, seq, tm, router=None):
    t, d = h2.shape
    tps = seq // tm
    route = router is not None
    row = lambda wdt: pl.BlockSpec((tm, wdt), lambda i: (i, 0))
    mod = pl.BlockSpec((1, 1, d), lambda i: (i // tps, 0, 0))
    in_specs = [row(D_RET), row(D_CONV), row(D_NA), row(d),
                pl.BlockSpec((d, d), lambda i: (0, 0)), mod,
                pl.BlockSpec((1, d), lambda i: (0, 0)), mod, mod]
    args = [yr, yc, yn, h2, w_bf, g1, n2g.reshape(1, d), sh2, sc2]
    out_specs = [row(d), row(d)]
    out_shape = [jax.ShapeDtypeStruct((t, d), F32), jax.ShapeDtypeStruct((t, d), F32 if route else BF16)]
    if route:
        rw, rb = router
        in_specs += [pl.BlockSpec((N_EXPERTS, d), lambda i: (0, 0)),
                     pl.BlockSpec((N_EXPERTS, 1), lambda i: (0, 0))]
        args += [rw.T, rb.reshape(N_EXPERTS, 1)]
        out_specs += [pl.BlockSpec((2, tm), lambda i: (0, i))] * 2
        out_shape += [jax.ShapeDtypeStruct((2, t), jnp.int32), jax.ShapeDtypeStruct((2, t), F32)]
    return pl.pallas_call(
        functools.partial(_outproj_kernel, route=route),
        grid=(t // tm,),
        in_specs=in_specs,
        out_specs=out_specs,
        out_shape=out_shape,
        compiler_params=_cp(("parallel",)),
        name="out_proj_route" if route else "out_proj_t%d" % t,
    )(*args)


def _swiglu_chunks(x, w_in_ref, w_out_ref, ff, n_chunks):
    tf = -(-ff // (n_chunks * MXU_COLS)) * MXU_COLS
    y = None
    for c0 in range(0, ff, tf):
        c1 = min(c0 + tf, ff)
        gate = _dot(x, w_in_ref[:, c0:c1])
        up = _dot(x, w_in_ref[:, ff + c0:ff + c1])
        part = _dot((_silu(gate) * up).astype(BF16), w_out_ref[c0:c1, :])
        y = part if y is None else y + part
    return y


def _ffn_kernel(x_ref, h_ref, wi_ref, wo_ref, g2_ref, o_ref, *, ff, n_chunks):
    y = _swiglu_chunks(x_ref[...], wi_ref, wo_ref, ff, n_chunks)
    o_ref[...] = h_ref[...] + g2_ref[0] * y


def _ffn(xn, h2, w_in_bf, w_out_bf, g2, seq, tm, n_chunks):
    t, d = h2.shape
    ff = w_out_bf.shape[0]
    tps = seq // tm
    resident = pl.Buffered(1)
    return pl.pallas_call(
        functools.partial(_ffn_kernel, ff=ff, n_chunks=n_chunks),
        grid=(t // tm,),
        in_specs=[pl.BlockSpec((tm, d), lambda i: (i, 0)),
                  pl.BlockSpec((tm, d), lambda i: (i, 0)),
                  pl.BlockSpec((d, 2 * ff), lambda i: (0, 0), pipeline_mode=resident),
                  pl.BlockSpec((ff, d), lambda i: (0, 0), pipeline_mode=resident),
                  pl.BlockSpec((1, 1, d), lambda i: (i // tps, 0, 0))],
        out_specs=pl.BlockSpec((tm, d), lambda i: (i, 0)),
        out_shape=jax.ShapeDtypeStruct((t, d), F32),
        compiler_params=_cp(("parallel",)),
        name="ffn_t%d" % t,
    )(xn, h2, w_in_bf, w_out_bf, g2)


def _dispatch_kernel(s0_ref, s1_ref, zs_ref, zv_ref, x_ref, xs_ref, zbuf_ref, sem, *, rows):
    @pl.when(pl.program_id(0) == 0)
    def _():
        zbuf_ref[...] = jnp.zeros_like(zbuf_ref)
        for e in range(2 * N_EXPERTS):
            @pl.when(zv_ref[e] != 0)
            def _():
                dst = xs_ref.at[pl.ds(pl.multiple_of(zs_ref[e], MOE_TS), MOE_TS)]
                pltpu.make_async_copy(zbuf_ref, dst, sem.at[1]).start()
        for e in range(2 * N_EXPERTS):
            @pl.when(zv_ref[e] != 0)
            def _():
                pltpu.make_async_copy(zbuf_ref, xs_ref.at[pl.ds(0, MOE_TS)], sem.at[1]).wait()

    def issue(g, carry):
        for u in range(DMA_UNROLL):
            r = g * DMA_UNROLL + u
            src = x_ref.at[pl.ds(r, 1)]
            pltpu.make_async_copy(src, xs_ref.at[pl.ds(s0_ref[r], 1)], sem.at[0]).start(priority=u % 2)
            pltpu.make_async_copy(src, xs_ref.at[pl.ds(s1_ref[r], 1)], sem.at[0]).start(priority=(u + 1) % 2)
        return carry

    lax.fori_loop(0, rows // DMA_UNROLL, issue, 0)
    for _ in range(2):
        pltpu.make_async_copy(x_ref, xs_ref.at[pl.ds(0, rows)], sem.at[0]).wait()


def _dispatch(xn, slot0, slot1, zplan, n_slots, rows):
    t, d = xn.shape
    zstart, zvalid = zplan
    return pl.pallas_call(
        functools.partial(_dispatch_kernel, rows=rows),
        grid=(t // rows,),
        in_specs=[pl.BlockSpec((rows,), lambda i: (i,), memory_space=pltpu.SMEM),
                  pl.BlockSpec((rows,), lambda i: (i,), memory_space=pltpu.SMEM),
                  pl.BlockSpec((2 * N_EXPERTS,), lambda i: (0,), memory_space=pltpu.SMEM),
                  pl.BlockSpec((2 * N_EXPERTS,), lambda i: (0,), memory_space=pltpu.SMEM),
                  pl.BlockSpec((rows, d), lambda i: (i, 0))],
        out_specs=pl.BlockSpec(memory_space=pl.ANY),
        out_shape=jax.ShapeDtypeStruct((n_slots, d), xn.dtype),
        scratch_shapes=[pltpu.VMEM((MOE_TS, d), xn.dtype), pltpu.SemaphoreType.DMA((2,))],
        compiler_params=_cp(("arbitrary",)),
        name="moe_dispatch",
    )(slot0, slot1, zstart, zvalid, xn)


def _expert_kernel(be_ref, nu_ref, x_ref, wi_ref, wo_ref, o_ref, *, ff, n_chunks):
    used = pl.program_id(0) < nu_ref[0]

    @pl.when(used)
    def _():
        o_ref[...] = _swiglu_chunks(x_ref[...].astype(BF16), wi_ref.at[0], wo_ref.at[0], ff, n_chunks)

    @pl.when(jnp.logical_not(used))
    def _():
        o_ref[...] = jnp.zeros_like(o_ref)


def _experts(xs, block_expert, n_used, w_in_bf, w_out_bf, n_chunks):
    n, d = xs.shape
    ff = w_out_bf.shape[1]
    ts = MOE_TS
    grid_spec = pltpu.PrefetchScalarGridSpec(
        num_scalar_prefetch=2,
        grid=(n // ts,),
        in_specs=[pl.BlockSpec((ts, d), lambda j, be, nu: (j, 0)),
                  pl.BlockSpec((1, d, 2 * ff), lambda j, be, nu: (be[j], 0, 0)),
                  pl.BlockSpec((1, ff, d), lambda j, be, nu: (be[j], 0, 0))],
        out_specs=pl.BlockSpec((ts, d), lambda j, be, nu: (j, 0)),
    )
    return pl.pallas_call(
        functools.partial(_expert_kernel, ff=ff, n_chunks=n_chunks),
        grid_spec=grid_spec,
        out_shape=jax.ShapeDtypeStruct((n, d), F32),
        compiler_params=_cp(("arbitrary",)),
        name="moe_experts",
    )(block_expert, n_used, xs, w_in_bf, w_out_bf)


def _combine_kernel(p0_ref, p1_ref, ys_ref, h_ref, w0_ref, w1_ref, g2_ref, fg_ref, o_ref,
                    y0_ref, y1_ref, sem, *, rows):
    def issue(g, carry):
        for u in range(DMA_UNROLL):
            r = g * DMA_UNROLL + u
            pltpu.make_async_copy(ys_ref.at[pl.ds(p0_ref[r], 1)], y0_ref.at[pl.ds(r, 1)],
                                  sem.at[0]).start(priority=u % 2)
            pltpu.make_async_copy(ys_ref.at[pl.ds(p1_ref[r], 1)], y1_ref.at[pl.ds(r, 1)],
                                  sem.at[1]).start(priority=(u + 1) % 2)
        return carry

    lax.fori_loop(0, rows // DMA_UNROLL, issue, 0)
    pltpu.make_async_copy(ys_ref.at[pl.ds(0, rows)], y0_ref, sem.at[0]).wait()
    pltpu.make_async_copy(ys_ref.at[pl.ds(0, rows)], y1_ref, sem.at[1]).wait()
    h = h_ref[...] + g2_ref[0] * (w0_ref[...] * y0_ref[...] + w1_ref[...] * y1_ref[...])
    ms = jnp.mean(h * h, axis=-1, keepdims=True)
    o_ref[...] = h * lax.rsqrt(ms + NORM_EPS) * fg_ref[...]


def _combine(pos0, pos1, ys, h2, w0, w1, g2, final_g, seq, rows):
    t, d = h2.shape
    tps = seq // rows
    blk = pl.BlockSpec((rows, d), lambda i: (i, 0))
    col = pl.BlockSpec((rows, 1), lambda i: (i, 0))
    return pl.pallas_call(
        functools.partial(_combine_kernel, rows=rows),
        grid=(t // rows,),
        in_specs=[pl.BlockSpec((rows,), lambda i: (i,), memory_space=pltpu.SMEM),
                  pl.BlockSpec((rows,), lambda i: (i,), memory_space=pltpu.SMEM),
                  pl.BlockSpec(memory_space=pl.ANY),
                  blk, col, col,
                  pl.BlockSpec((1, 1, d), lambda i: (i // tps, 0, 0)),
                  pl.BlockSpec((1, d), lambda i: (0, 0))],
        out_specs=blk,
        out_shape=jax.ShapeDtypeStruct((t, d), F32),
        scratch_shapes=[pltpu.VMEM((rows, d), F32), pltpu.VMEM((rows, d), F32),
                        pltpu.SemaphoreType.DMA((2,))],
        compiler_params=_cp(("arbitrary",)),
        name="moe_combine_norm",
    )(pos0, pos1, ys, h2, w0, w1, g2, final_g.reshape(1, d))


def _route_plan(idx):
    t = idx.shape[1]
    ts = MOE_TS
    n_slots = 2 * t + N_EXPERTS * ts
    e_flat = idx.reshape(-1)
    onehot = (e_flat[:, None] == jnp.arange(N_EXPERTS, dtype=jnp.int32)[None, :]).astype(jnp.int32)
    csum = jnp.cumsum(onehot, axis=0)
    rank = jnp.sum((csum - onehot) * onehot, axis=1)
    counts = csum[-1]
    padded = ((counts + ts - 1) // ts) * ts
    ends = jnp.cumsum(padded)
    offs = ends - padded
    slot = (jnp.sum(onehot * offs[None, :], axis=1) + rank).astype(jnp.int32)
    starts = jnp.arange(n_slots // ts, dtype=jnp.int32) * ts
    block_expert = jnp.minimum(jnp.sum((starts[:, None] >= ends[None, :]).astype(jnp.int32), axis=1),
                               N_EXPERTS - 1).astype(jnp.int32)
    n_used = (ends[-1] // ts).astype(jnp.int32).reshape(1)
    tail = ends[-1] + jnp.arange(N_EXPERTS, dtype=jnp.int32) * ts
    zstart = jnp.concatenate([ends - ts, tail])
    zvalid = jnp.concatenate([padded > 0, tail < n_slots]).astype(jnp.int32)
    zstart = jnp.where(zvalid != 0, zstart, 0).astype(jnp.int32)
    return slot[:t], slot[t:], block_expert, n_used, (zstart, zvalid), n_slots


def kernel(x, c, ctx, c_ctx, ada_w, ada_b, norm1_g, norm2_g, w_in, w_out, ret_decay_logit, ret_gn_g,
           conv_w, na_rpb, ffn_w_in, ffn_w_out, moe_router_w, moe_router_b, moe_w_in, moe_w_out, final_g):
    b, seq, d = x.shape
    ctx_len = ctx.shape[1]
    depth = ada_w.shape[0]
    assert depth == 2, "the final norm is fused into the last (MoE) layer's combine step"
    rows = seq // GRID_W
    t_lat, t_ctx = b * seq, b * ctx_len
    tm_lat = min(1024, seq)
    tm_ctx = ctx_len

    c8 = jnp.zeros((8, d), F32).at[:b].set(c).at[b].set(c_ctx)
    mods = _ada(c8, ada_w, ada_b)
    tables = _rope_tables(seq)

    h = x.reshape(t_lat, d)
    hc = ctx.reshape(t_ctx, d)
    zero_state = jnp.zeros((b, N_PAIRS, LANES, LANES), F32)
    out = None
    for layer in range(depth):
        update_ctx = layer < depth - 1
        m = mods[layer].reshape(8, 6, d)
        lat = [m[:b, i].reshape(b, 1, d) for i in range(6)]
        cx = [jnp.broadcast_to(m[b, i].reshape(1, 1, d), (b, 1, d)) for i in range(6)]
        w_in_bf = w_in[layer].astype(BF16)
        w_out_bf = w_out[layer].astype(BF16)
        lgt = ret_decay_logit[layer].reshape(2, N_PAIRS, LANES // HEAD_DIM)
        lgt = jnp.repeat(lgt, HEAD_DIM, axis=2).transpose(1, 0, 2)
        gn = ret_gn_g[layer].reshape(N_PAIRS, 1, LANES)

        proj_c = _in_proj(hc, norm1_g[layer], cx[0], cx[1], w_in_bf, None, ctx_len, tm_ctx)
        proj_c = proj_c.reshape(b, ctx_len, D_IN_PROJ)
        proj = _in_proj(h, norm1_g[layer], lat[0], lat[1], w_in_bf, tables, seq, tm_lat)
        proj = proj.reshape(b, seq, D_IN_PROJ)

        y_ret_c, s_fwd, s_bwd = _retention(proj_c, lgt, gn, zero_state, zero_state)
        y_ret, _, _ = _retention(proj, lgt, gn, s_fwd, s_bwd)
        y_conv = _short_conv(proj, conv_w[layer])
        y_na = _na(proj, proj_c, _na_bias(na_rpb[layer], rows))

        if layer % 2 == 0:
            h, xn = _out_proj(y_ret.reshape(t_lat, D_RET), y_conv.reshape(t_lat, D_CONV),
                              y_na.reshape(t_lat, D_NA), h, w_out_bf, lat[2], norm2_g[layer],
                              lat[3], lat[4], seq, tm_lat)
            fw_in = ffn_w_in[layer // 2].astype(BF16)
            fw_out = ffn_w_out[layer // 2].astype(BF16)
            h = _ffn(xn, h, fw_in, fw_out, lat[5], seq, min(512, seq), FFN_CHUNKS)
            if update_ctx:
                y_conv_c = _short_conv(proj_c, conv_w[layer])
                y_na_c = _ctx_attn(proj_c)
                hc, xnc = _out_proj(y_ret_c.reshape(t_ctx, D_RET), y_conv_c.reshape(t_ctx, D_CONV),
                                    y_na_c.reshape(t_ctx, D_NA), hc, w_out_bf, cx[2], norm2_g[layer],
                                    cx[3], cx[4], ctx_len, tm_ctx)
                hc = _ffn(xnc, hc, fw_in, fw_out, cx[5], ctx_len, tm_ctx, FFN_CHUNKS)
        else:
            e = layer // 2
            h, xn, idx, wts = _out_proj(y_ret.reshape(t_lat, D_RET), y_conv.reshape(t_lat, D_CONV),
                                        y_na.reshape(t_lat, D_NA), h, w_out_bf, lat[2], norm2_g[layer],
                                        lat[3], lat[4], seq, tm_lat,
                                        router=(moe_router_w[e], moe_router_b[e]))
            pos0, pos1, block_expert, n_used, zstart, n_slots = _route_plan(idx)
            xs = _dispatch(xn, pos0, pos1, zstart, n_slots, min(1024, seq))
            ys = _experts(xs, block_expert, n_used, moe_w_in[e].astype(BF16), moe_w_out[e].astype(BF16),
                          FFN_CHUNKS)
            out = _combine(pos0, pos1, ys, h, wts[0].reshape(t_lat, 1), wts[1].reshape(t_lat, 1),
                           lat[5], final_g, seq, min(512, seq))
            h = out
    return out.reshape(b, seq, d)
```

```python
import functools

import numpy as np
import jax
import jax.numpy as jnp
from jax import lax
from jax.experimental import pallas as pl
from jax.experimental.pallas import tpu as pltpu

F32 = jnp.float32
BF16 = jnp.bfloat16

LANES = 128
HEAD_DIM = 64
GRID_W = 64
N_CONV_GROUPS = 4
N_RET_HEADS = 6
N_NA_HEADS = 6
D_RET = N_RET_HEADS * HEAD_DIM
D_CONV = N_CONV_GROUPS * HEAD_DIM
D_NA = N_NA_HEADS * HEAD_DIM
D_IN_PROJ = 4 * D_RET + 3 * D_CONV + 3 * D_NA
N_PAIRS = D_RET // LANES
NA_ROWS = 8
NA_COLS = 16
N_EXPERTS = 8
ROPE_BASE = 10000.0
NORM_EPS = 1e-6
NEG_BIG = -1e30
LOG2E = 1.4426950408889634

RET_CHUNK = 256
NA_QROWS = 4
NA_KROWS = 12
MOE_TS = 512
DMA_UNROLL = 8
FFN_CHUNKS = 2
IN_PROJ_SUB = 512
MXU_COLS = 256
IN_PROJ_CHUNK = 3 * MXU_COLS
VMEM_LIMIT = 56 * 1024 * 1024

CB_RQ, CB_RK, CB_RV, CB_RG = 0, 3, 6, 9
CB_CB, CB_CC, CB_CX = 12, 14, 16
CB_NQ, CB_NK, CB_NV = 18, 21, 24


def _cp(sem, vmem=VMEM_LIMIT):
    return pltpu.CompilerParams(dimension_semantics=sem, vmem_limit_bytes=vmem)


def _silu(x):
    return x * (1.0 / (1.0 + jnp.exp(-x)))


def _dot(a, b):
    return jnp.dot(a, b, preferred_element_type=F32)


def _dot_nt(a, b):
    return lax.dot_general(a, b, (((1,), (1,)), ((), ())), preferred_element_type=F32)


def _dot_tn(a, b):
    return lax.dot_general(a, b, (((0,), (0,)), ((), ())), preferred_element_type=F32)


def _split_bf16(x):
    hi = x.astype(BF16)
    lo = (x - hi.astype(F32)).astype(BF16)
    return hi, lo


def _ada_kernel(c_ref, w_ref, b_ref, o_ref):
    x = _silu(c_ref[...]).astype(BF16)
    o_ref[0] = _dot(x, w_ref[0].astype(BF16)) + b_ref[0]


def _ada(c8, ada_w, ada_b):
    depth, d, n = ada_w.shape
    tn = n // 4
    return pl.pallas_call(
        _ada_kernel,
        grid=(depth, n // tn),
        in_specs=[pl.BlockSpec((8, d), lambda l, j: (0, 0)),
                  pl.BlockSpec((1, d, tn), lambda l, j: (l, 0, j)),
                  pl.BlockSpec((1, 1, tn), lambda l, j: (l, 0, j))],
        out_specs=pl.BlockSpec((1, 8, tn), lambda l, j: (l, 0, j)),
        out_shape=jax.ShapeDtypeStruct((depth, 8, n), F32),
        compiler_params=_cp(("parallel", "parallel")),
        name="ada_mod",
    )(c8, ada_w, ada_b.reshape(depth, 1, n))


def _norm_mod(x, g, sh, sc):
    ms = jnp.mean(x * x, axis=-1, keepdims=True)
    y = x * lax.rsqrt(ms + NORM_EPS) * g
    return y * (1.0 + sc) + sh


def _inproj_kernel(*refs, rope):
    if rope:
        h_ref, g_ref, sh_ref, sc_ref, w_ref, cos_ref, sa_ref, sb_ref, o_ref = refs
    else:
        h_ref, g_ref, sh_ref, sc_ref, w_ref, o_ref = refs
    tm = h_ref.shape[0]
    sub = min(IN_PROJ_SUB, tm)
    cw = IN_PROJ_CHUNK
    for r in range(tm // sub):
        rs = slice(r * sub, (r + 1) * sub)
        xn = _norm_mod(h_ref[rs, :], g_ref[...], sh_ref[0], sc_ref[0]).astype(BF16)
        for c0 in range(0, D_IN_PROJ, cw):
            c1 = min(c0 + cw, D_IN_PROJ)
            acc = _dot(xn, w_ref[:, c0:c1])
            for j in range((c1 - c0) // LANES):
                blk = acc[:, j * LANES:(j + 1) * LANES]
                cb = c0 // LANES + j
                if rope and cb < CB_RV:
                    blk = (blk * cos_ref[rs, :] + pltpu.roll(blk, 16, 1) * sa_ref[rs, :]
                           + pltpu.roll(blk, LANES - 16, 1) * sb_ref[rs, :])
                if CB_RK <= cb < CB_RV:
                    blk = blk * (HEAD_DIM ** -0.5)
                elif CB_NQ <= cb < CB_NK:
                    blk = blk * (HEAD_DIM ** -0.5 * (LOG2E if rope else 1.0))
                o_ref[rs, cb * LANES:(cb + 1) * LANES] = blk.astype(BF16)


def _in_proj(h2, g, sh, sc, w_bf, tables, seq, tm):
    t, d = h2.shape
    tiles_per_seq = seq // tm
    rope = tables is not None
    in_specs = [pl.BlockSpec((tm, d), lambda i: (i, 0)),
                pl.BlockSpec((1, d), lambda i: (0, 0)),
                pl.BlockSpec((1, 1, d), lambda i: (i // tiles_per_seq, 0, 0)),
                pl.BlockSpec((1, 1, d), lambda i: (i // tiles_per_seq, 0, 0)),
                pl.BlockSpec((d, D_IN_PROJ), lambda i: (0, 0))]
    args = [h2, g.reshape(1, d), sh, sc, w_bf]
    if rope:
        in_specs += [pl.BlockSpec((tm, LANES), lambda i: (i % tiles_per_seq, 0))] * 3
        args += list(tables)
    return pl.pallas_call(
        functools.partial(_inproj_kernel, rope=rope),
        grid=(t // tm,),
        in_specs=in_specs,
        out_specs=pl.BlockSpec((tm, D_IN_PROJ), lambda i: (i, 0)),
        out_shape=jax.ShapeDtypeStruct((t, D_IN_PROJ), BF16),
        compiler_params=_cp(("parallel",)),
        name="in_proj_rope" if rope else "in_proj_ctx",
    )(*args)


def _rope_tables(seq):
    t = np.arange(seq)
    row = (t // GRID_W).astype(np.float32)
    col = (t % GRID_W).astype(np.float32)
    n_freq = HEAD_DIM // 4
    inv_freq = (ROPE_BASE ** (-np.arange(n_freq, dtype=np.float32) / n_freq)).astype(np.float32)
    ang_r = row[:, None] * inv_freq
    ang_c = col[:, None] * inv_freq
    cos_h = np.concatenate([np.cos(ang_r), np.cos(ang_r), np.cos(ang_c), np.cos(ang_c)], axis=1)
    sin_h = np.concatenate([np.sin(ang_r), np.sin(ang_r), np.sin(ang_c), np.sin(ang_c)], axis=1)
    lane = np.arange(HEAD_DIM)
    second = (lane % 32) >= 16
    sa = np.where(second[None, :], sin_h, 0.0)
    sb = np.where(second[None, :], 0.0, -sin_h)
    tile2 = lambda a: jnp.asarray(np.concatenate([a, a], axis=1), F32)
    return tile2(cos_h), tile2(sa), tile2(sb)


def _ret_kernel(q_ref, k_ref, v_ref, g_ref, lgt_ref, gn_ref, sf0_ref, sb0_ref,
                y_ref, sfo_ref, sbo_ref, sfs_ref, sbs_ref, *, seq):
    c = RET_CHUNK
    n_chunks = seq // c
    lg = jax.nn.log_sigmoid(lgt_ref[0])
    lgf, lgb = lg[0:1, :], lg[1:2, :]
    pos = lax.broadcasted_iota(jnp.int32, (c, 1), 0).astype(F32)
    dkf = jnp.exp(lgf * (c - 1.0 - pos))
    dkb = jnp.exp(lgb * pos)
    dqf = jnp.exp(lgf * (pos + 1.0))
    dqb = jnp.exp(lgb * (c - pos))
    cdf = jnp.exp(lgf * float(c))
    cdb = jnp.exp(lgb * float(c))
    lane = lax.broadcasted_iota(jnp.int32, (1, LANES), 1)
    first = lane < HEAD_DIM
    ri = lax.broadcasted_iota(jnp.int32, (LANES, LANES), 0)
    ci = lax.broadcasted_iota(jnp.int32, (LANES, LANES), 1)
    same = (ri < HEAD_DIM) == (ci < HEAD_DIM)
    bd = same.astype(F32)
    avg = (bd * (1.0 / HEAD_DIM)).astype(BF16)
    ii = lax.broadcasted_iota(jnp.int32, (c, c), 0)
    jj = lax.broadcasted_iota(jnp.int32, (c, c), 1)
    dif = (ii - jj).astype(F32)

    def decay_mask(h0):
        lf = lgf[:, h0:h0 + 1]
        lb = lgb[:, h0:h0 + 1]
        return jnp.where(dif > 0, jnp.exp(lf * jnp.maximum(dif, 0.0)),
                         jnp.where(dif < 0, jnp.exp(lb * jnp.maximum(-dif, 0.0)), 2.0))

    dm = (decay_mask(0), decay_mask(HEAD_DIM))
    mfirst = first.astype(BF16)
    msecond = (1.0 - first.astype(F32)).astype(BF16)

    unroll = min(8, n_chunks)

    def incr(n, carry):
        sl = pl.ds(pl.multiple_of(n * c, c), c)
        kf = k_ref[0, sl, :].astype(F32)
        v = v_ref[0, sl, :]
        sfs_ref[n] = _dot_tn((kf * dkf).astype(BF16), v) * bd
        sbs_ref[n] = _dot_tn((kf * dkb).astype(BF16), v) * bd
        return carry

    lax.fori_loop(0, n_chunks, incr, 0, unroll=unroll)

    def scan_f(n, s):
        u = sfs_ref[n]
        sfs_ref[n] = s
        return cdf * s + u

    def scan_b(i, s):
        n = n_chunks - 1 - i
        u = sbs_ref[n]
        sbs_ref[n] = s
        return cdb * s + u

    sfo_ref[0, 0] = lax.fori_loop(0, n_chunks, scan_f, sf0_ref[0, 0])
    sbo_ref[0, 0] = lax.fori_loop(0, n_chunks, scan_b, sb0_ref[0, 0])

    def outp(n, carry):
        sl = pl.ds(pl.multiple_of(n * c, c), c)
        q = q_ref[0, sl, :]
        k = k_ref[0, sl, :]
        v = v_ref[0, sl, :]
        qf = q.astype(F32)
        o = None
        for hh, mk in enumerate((mfirst, msecond)):
            s = _dot_nt(q * mk, k) * dm[hh]
            oh = _dot(s.astype(BF16), v)
            o = oh if o is None else jnp.where(first, o, oh)
        o = o + _dot((qf * dqf).astype(BF16), sfs_ref[n].astype(BF16))
        o = o + _dot((qf * dqb).astype(BF16), sbs_ref[n].astype(BF16))
        hi, lo = _split_bf16(o)
        mu = _dot(hi, avg) + _dot(lo, avg)
        dlt = o - mu
        hi, lo = _split_bf16(dlt * dlt)
        var = _dot(hi, avg) + _dot(lo, avg)
        on = dlt * lax.rsqrt(var + NORM_EPS) * gn_ref[0]
        y_ref[0, sl, :] = (_silu(g_ref[0, sl, :].astype(F32)) * on).astype(BF16)
        return carry

    lax.fori_loop(0, n_chunks, outp, 0, unroll=unroll)


def _retention(proj, lgt, gn, sf0, sb0):
    b, seq, _ = proj.shape
    col = lambda off: pl.BlockSpec((1, seq, LANES), lambda bi, p: (bi, 0, off + p))
    st = pl.BlockSpec((1, 1, LANES, LANES), lambda bi, p: (bi, p, 0, 0))
    return pl.pallas_call(
        functools.partial(_ret_kernel, seq=seq),
        grid=(b, N_PAIRS),
        in_specs=[col(CB_RQ), col(CB_RK), col(CB_RV), col(CB_RG),
                  pl.BlockSpec((1, 2, LANES), lambda bi, p: (p, 0, 0)),
                  pl.BlockSpec((1, 1, LANES), lambda bi, p: (p, 0, 0)),
                  st, st],
        out_specs=[pl.BlockSpec((1, seq, LANES), lambda bi, p: (bi, 0, p)), st, st],
        out_shape=[jax.ShapeDtypeStruct((b, seq, D_RET), BF16),
                   jax.ShapeDtypeStruct((b, N_PAIRS, LANES, LANES), F32),
                   jax.ShapeDtypeStruct((b, N_PAIRS, LANES, LANES), F32)],
        scratch_shapes=[pltpu.VMEM((seq // RET_CHUNK, LANES, LANES), F32),
                        pltpu.VMEM((seq // RET_CHUNK, LANES, LANES), F32)],
        compiler_params=_cp(("parallel", "parallel")),
        name="retention_s%d" % seq,
    )(proj, proj, proj, proj, lgt, gn, sf0, sb0)


def _conv_kernel(b_ref, c_ref, x_ref, w_ref, y_ref, *, seq):
    u = c_ref[0].astype(F32) * x_ref[0].astype(F32)
    row = lax.broadcasted_iota(jnp.int32, (seq, 1), 0)
    prev = jnp.where(row == 0, 0.0, pltpu.roll(u, 1, 0))
    nxt = jnp.where(row == seq - 1, 0.0, pltpu.roll(u, seq - 1, 0))
    y = w_ref[0:1, :] * prev + w_ref[1:2, :] * u + w_ref[2:3, :] * nxt
    y_ref[0] = (b_ref[0].astype(F32) * y).astype(BF16)


def _short_conv(proj, conv_w):
    b, seq, _ = proj.shape
    nblk = D_CONV // LANES
    col = lambda off: pl.BlockSpec((1, seq, LANES), lambda bi, j: (bi, 0, off + j))
    return pl.pallas_call(
        functools.partial(_conv_kernel, seq=seq),
        grid=(b, nblk),
        in_specs=[col(CB_CB), col(CB_CC), col(CB_CX),
                  pl.BlockSpec((3, LANES), lambda bi, j: (0, j))],
        out_specs=pl.BlockSpec((1, seq, LANES), lambda bi, j: (bi, 0, j)),
        out_shape=jax.ShapeDtypeStruct((b, seq, D_CONV), BF16),
        compiler_params=_cp(("parallel", "parallel")),
        name="short_conv_s%d" % seq,
    )(proj, proj, proj, conv_w)


def _na_bias_plan(rows):
    plan = np.full((3, NA_QROWS, NA_KROWS), -1, np.int64)
    for cl, rb in enumerate((0, NA_QROWS, rows - NA_QROWS)):
        ws = int(np.clip(rb - NA_ROWS // 2, 0, rows - NA_KROWS))
        for i in range(NA_QROWS):
            r = rb + i
            r0 = int(np.clip(r - NA_ROWS // 2, 0, rows - NA_ROWS))
            for j in range(NA_KROWS):
                kr = ws + j
                if r0 <= kr < r0 + NA_ROWS:
                    plan[cl, i, j] = kr - r + (NA_ROWS - 1)
    return plan


def _na_bias_kernel(rpb_ref, o_ref, *, plan):
    w = GRID_W
    ndr, ndc = 2 * NA_ROWS - 1, 2 * NA_COLS - 1
    base = pl.program_id(0) * (ndr * ndc)
    c = lax.broadcasted_iota(jnp.int32, (w, w), 0)
    kc = lax.broadcasted_iota(jnp.int32, (w, w), 1)
    dcidx = jnp.clip(kc - c, -(NA_COLS - 1), NA_COLS - 1) + (NA_COLS - 1)
    c0 = jnp.clip(c - NA_COLS // 2, 0, w - NA_COLS)
    col_in = jnp.logical_and(kc >= c0, kc < c0 + NA_COLS)
    neg = jnp.full((w, w), NEG_BIG, F32)
    tiles = []
    for dr in range(ndr):
        t = neg
        for dcv in range(ndc):
            t = jnp.where(dcidx == dcv, rpb_ref[base + dr * ndc + dcv], t)
        tiles.append(jnp.where(col_in, t * LOG2E, NEG_BIG))
    for cl in range(plan.shape[0]):
        for i in range(plan.shape[1]):
            for j in range(0, plan.shape[2], 2):
                pair = [tiles[int(d)] if d >= 0 else neg for d in plan[cl, i, j:j + 2]]
                o_ref[0, cl, i * w:(i + 1) * w, j * w:(j + 2) * w] = jnp.concatenate(pair, axis=1)


def _na_bias(rpb, rows):
    h = rpb.shape[0]
    tq, tk = NA_QROWS * GRID_W, NA_KROWS * GRID_W
    return pl.pallas_call(
        functools.partial(_na_bias_kernel, plan=_na_bias_plan(rows)),
        grid=(h,),
        in_specs=[pl.BlockSpec(memory_space=pltpu.SMEM)],
        out_specs=pl.BlockSpec((1, 3, tq, tk), lambda hh: (hh, 0, 0, 0)),
        out_shape=jax.ShapeDtypeStruct((h, 3, tq, tk), F32),
        compiler_params=_cp(("parallel",)),
        name="na_bias_expand",
    )(rpb.reshape(-1))


def _na_kernel(q_ref, k_ref, v_ref, kc_ref, vc_ref, bias_ref, o_ref, v1_ref, vc1_ref, *, rows):
    qb = pl.program_id(1)
    lane = lax.broadcasted_iota(jnp.int32, (1, LANES), 1)
    first = lane < HEAD_DIM

    @pl.when(qb == 0)
    def _():
        one = jnp.ones((1, LANES), BF16)
        for p in range(N_PAIRS):
            cols = slice(p * LANES, (p + 1) * LANES)
            v1_ref[2 * p] = jnp.where(first, v_ref[0, :, cols], one)
            v1_ref[2 * p + 1] = jnp.where(first, one, v_ref[0, :, cols])
            vc1_ref[2 * p] = jnp.where(first, vc_ref[0, :, cols], one)
            vc1_ref[2 * p + 1] = jnp.where(first, one, vc_ref[0, :, cols])

    ws = jnp.clip(qb * NA_QROWS - NA_ROWS // 2, 0, rows - NA_KROWS)
    sl = pl.ds(pl.multiple_of(ws * GRID_W, GRID_W), NA_KROWS * GRID_W)
    for p in range(N_PAIRS):
        cols = slice(p * LANES, (p + 1) * LANES)
        q = q_ref[0, :, cols]
        kw = k_ref[0, sl, cols]
        kc = kc_ref[0, :, cols]
        pv = []
        for hh in range(2):
            mk = (first if hh == 0 else jnp.logical_not(first)).astype(BF16)
            qh = q * mk
            s_loc = _dot_nt(qh, kw) + bias_ref[2 * p + hh, 0]
            s_ctx = _dot_nt(qh, kc)
            m = jnp.maximum(jnp.max(s_loc, axis=-1, keepdims=True), jnp.max(s_ctx, axis=-1, keepdims=True))
            p_loc = jnp.exp2(s_loc - m).astype(BF16)
            p_ctx = jnp.exp2(s_ctx - m).astype(BF16)
            pv.append(_dot(p_loc, v1_ref[2 * p + hh, sl, :]) + _dot(p_ctx, vc1_ref[2 * p + hh]))
        num = jnp.where(first, pv[0], pv[1])
        den = jnp.where(first, pltpu.roll(pv[0], HEAD_DIM, 1), pltpu.roll(pv[1], HEAD_DIM, 1))
        o_ref[0, :, cols] = (num / den).astype(BF16)


def _na(proj, proj_ctx, bias):
    b, seq, _ = proj.shape
    ctx_len = proj_ctx.shape[1]
    rows = seq // GRID_W
    nqb = rows // NA_QROWS
    tq = NA_QROWS * GRID_W
    tk = NA_KROWS * GRID_W
    nh = 2 * N_PAIRS
    cq, ck, cv = (CB_NQ * LANES // D_NA, CB_NK * LANES // D_NA, CB_NV * LANES // D_NA)

    def cls(qb):
        return jnp.where(qb == 0, 0, jnp.where(qb == nqb - 1, 2, 1))

    return pl.pallas_call(
        functools.partial(_na_kernel, rows=rows),
        grid=(b, nqb),
        in_specs=[pl.BlockSpec((1, tq, D_NA), lambda bi, qb: (bi, qb, cq)),
                  pl.BlockSpec((1, seq, D_NA), lambda bi, qb: (bi, 0, ck)),
                  pl.BlockSpec((1, seq, D_NA), lambda bi, qb: (bi, 0, cv)),
                  pl.BlockSpec((1, ctx_len, D_NA), lambda bi, qb: (bi, 0, ck)),
                  pl.BlockSpec((1, ctx_len, D_NA), lambda bi, qb: (bi, 0, cv)),
                  pl.BlockSpec((nh, 1, tq, tk), lambda bi, qb: (0, cls(qb), 0, 0))],
        out_specs=pl.BlockSpec((1, tq, D_NA), lambda bi, qb: (bi, qb, 0)),
        out_shape=jax.ShapeDtypeStruct((b, seq, D_NA), BF16),
        scratch_shapes=[pltpu.VMEM((nh, seq, LANES), BF16), pltpu.VMEM((nh, ctx_len, LANES), BF16)],
        compiler_params=_cp(("parallel", "arbitrary")),
        name="na_attn",
    )(proj, proj, proj, proj_ctx, proj_ctx, bias)


def _ctx_attn_kernel(q_ref, k_ref, v_ref, o_ref):
    q = q_ref[0]
    k = k_ref[0]
    v = v_ref[0]
    lane = lax.broadcasted_iota(jnp.int32, (1, LANES), 1)
    first = lane < HEAD_DIM
    o = None
    for hh in range(2):
        mk = (first if hh == 0 else jnp.logical_not(first)).astype(BF16)
        s = _dot_nt(q * mk, k)
        m = jnp.max(s, axis=-1, keepdims=True)
        p = jnp.exp(s - m)
        l = jnp.sum(p, axis=-1, keepdims=True)
        oh = _dot(p.astype(BF16), v) / l
        o = oh if o is None else jnp.where(first, o, oh)
    o_ref[0] = o.astype(BF16)


def _ctx_attn(proj_ctx):
    b, ctx_len, _ = proj_ctx.shape
    col = lambda off: pl.BlockSpec((1, ctx_len, LANES), lambda bi, p: (bi, 0, off + p))
    return pl.pallas_call(
        _ctx_attn_kernel,
        grid=(b, N_PAIRS),
        in_specs=[col(CB_NQ), col(CB_NK), col(CB_NV)],
        out_specs=pl.BlockSpec((1, ctx_len, LANES), lambda bi, p: (bi, 0, p)),
        out_shape=jax.ShapeDtypeStruct((b, ctx_len, D_NA), BF16),
        compiler_params=_cp(("parallel", "parallel")),
        name="ctx_attn",
    )(proj_ctx, proj_ctx, proj_ctx)


def _outproj_kernel(*refs, route):
    if route:
        (yr_ref, yc_ref, yn_ref, h_ref, w_ref, g1_ref, n2_ref, sh_ref, sc_ref, rw_ref, rb_ref,
         ho_ref, xn_ref, idx_ref, wt_ref) = refs
    else:
        yr_ref, yc_ref, yn_ref, h_ref, w_ref, g1_ref, n2_ref, sh_ref, sc_ref, ho_ref, xn_ref = refs
    ycat = jnp.concatenate([yr_ref[...], yc_ref[...], yn_ref[...]], axis=-1)
    h = h_ref[...] + g1_ref[0] * _dot(ycat, w_ref[...])
    ho_ref[...] = h
    a = _norm_mod(h, n2_ref[...], sh_ref[0], sc_ref[0])
    xn_ref[...] = a.astype(xn_ref.dtype)
    if route:
        a_hi, a_lo = _split_bf16(a)
        r_hi, r_lo = _split_bf16(rw_ref[...])
        logits = _dot_nt(r_hi, a_hi) + _dot_nt(r_hi, a_lo) + _dot_nt(r_lo, a_hi) + rb_ref[...]
        eidx = lax.broadcasted_iota(jnp.int32, logits.shape, 0)
        m1 = jnp.max(logits, axis=0, keepdims=True)
        i1 = jnp.min(jnp.where(logits == m1, eidx, N_EXPERTS), axis=0, keepdims=True)
        rest = jnp.where(eidx == i1, -jnp.inf, logits)
        m2 = jnp.max(rest, axis=0, keepdims=True)
        i2 = jnp.min(jnp.where(rest == m2, eidx, N_EXPERTS), axis=0, keepdims=True)
        e2 = jnp.exp(m2 - m1)
        den = 1.0 + e2
        idx_ref[...] = jnp.concatenate([i1, i2], axis=0)
        wt_ref[...] = jnp.concatenate([1.0 / den, e2 / den], axis=0)


def _out_proj(yr, yc, yn, h2, w_bf, g1, n2g, sh2, sc2, seq, tm, router=None):
    t, d = h2.shape
    tps = seq // tm
    route = router is not None
    row = lambda wdt: pl.BlockSpec((tm, wdt), lambda i: (i, 0))
    mod = pl.BlockSpec((1, 1, d), lambda i: (i // tps, 0, 0))
    in_specs = [row(D_RET), row(D_CONV), row(D_NA), row(d),
                pl.BlockSpec((d, d), lambda i: (0, 0)), mod,
                pl.BlockSpec((1, d), lambda i: (0, 0)), mod, mod]
    args = [yr, yc, yn, h2, w_bf, g1, n2g.reshape(1, d), sh2, sc2]
    out_specs = [row(d), row(d)]
    out_shape = [jax.ShapeDtypeStruct((t, d), F32), jax.ShapeDtypeStruct((t, d), F32 if route else BF16)]
    if route:
        rw, rb = router
        in_specs += [pl.BlockSpec((N_EXPERTS, d), lambda i: (0, 0)),
                     pl.BlockSpec((N_EXPERTS, 1), lambda i: (0, 0))]
        args += [rw.T, rb.reshape(N_EXPERTS, 1)]
        out_specs += [pl.BlockSpec((2, tm), lambda i: (0, i))] * 2
        out_shape += [jax.ShapeDtypeStruct((2, t), jnp.int32), jax.ShapeDtypeStruct((2, t), F32)]
    return pl.pallas_call(
        functools.partial(_outproj_kernel, route=route),
        grid=(t // tm,),
        in_specs=in_specs,
        out_specs=out_specs,
        out_shape=out_shape,
        compiler_params=_cp(("parallel",)),
        name="out_proj_route" if route else "out_proj_t%d" % t,
    )(*args)


def _swiglu_chunks(x, w_in_ref, w_out_ref, ff, n_chunks):
    tf = -(-ff // (n_chunks * MXU_COLS)) * MXU_COLS
    y = None
    for c0 in range(0, ff, tf):
        c1 = min(c0 + tf, ff)
        gate = _dot(x, w_in_ref[:, c0:c1])
        up = _dot(x, w_in_ref[:, ff + c0:ff + c1])
        part = _dot((_silu(gate) * up).astype(BF16), w_out_ref[c0:c1, :])
        y = part if y is None else y + part
    return y


def _ffn_kernel(x_ref, h_ref, wi_ref, wo_ref, g2_ref, o_ref, *, ff, n_chunks):
    y = _swiglu_chunks(x_ref[...], wi_ref, wo_ref, ff, n_chunks)
    o_ref[...] = h_ref[...] + g2_ref[0] * y


def _ffn(xn, h2, w_in_bf, w_out_bf, g2, seq, tm, n_chunks):
    t, d = h2.shape
    ff = w_out_bf.shape[0]
    tps = seq // tm
    resident = pl.Buffered(1)
    return pl.pallas_call(
        functools.partial(_ffn_kernel, ff=ff, n_chunks=n_chunks),
        grid=(t // tm,),
        in_specs=[pl.BlockSpec((tm, d), lambda i: (i, 0)),
                  pl.BlockSpec((tm, d), lambda i: (i, 0)),
                  pl.BlockSpec((d, 2 * ff), lambda i: (0, 0), pipeline_mode=resident),
                  pl.BlockSpec((ff, d), lambda i: (0, 0), pipeline_mode=resident),
                  pl.BlockSpec((1, 1, d), lambda i: (i // tps, 0, 0))],
        out_specs=pl.BlockSpec((tm, d), lambda i: (i, 0)),
        out_shape=jax.ShapeDtypeStruct((t, d), F32),
        compiler_params=_cp(("parallel",)),
        name="ffn_t%d" % t,
    )(xn, h2, w_in_bf, w_out_bf, g2)


def _dispatch_kernel(s0_ref, s1_ref, zs_ref, zv_ref, x_ref, xs_ref, zbuf_ref, sem, *, rows):
    @pl.when(pl.program_id(0) == 0)
    def _():
        zbuf_ref[...] = jnp.zeros_like(zbuf_ref)
        for e in range(2 * N_EXPERTS):
            @pl.when(zv_ref[e] != 0)
            def _():
                dst = xs_ref.at[pl.ds(pl.multiple_of(zs_ref[e], MOE_TS), MOE_TS)]
                pltpu.make_async_copy(zbuf_ref, dst, sem.at[1]).start()
        for e in range(2 * N_EXPERTS):
            @pl.when(zv_ref[e] != 0)
            def _():
                pltpu.make_async_copy(zbuf_ref, xs_ref.at[pl.ds(0, MOE_TS)], sem.at[1]).wait()

    def issue(g, carry):
        for u in range(DMA_UNROLL):
            r = g * DMA_UNROLL + u
            src = x_ref.at[pl.ds(r, 1)]
            pltpu.make_async_copy(src, xs_ref.at[pl.ds(s0_ref[r], 1)], sem.at[0]).start(priority=u % 2)
            pltpu.make_async_copy(src, xs_ref.at[pl.ds(s1_ref[r], 1)], sem.at[0]).start(priority=(u + 1) % 2)
        return carry

    lax.fori_loop(0, rows // DMA_UNROLL, issue, 0)
    for _ in range(2):
        pltpu.make_async_copy(x_ref, xs_ref.at[pl.ds(0, rows)], sem.at[0]).wait()


def _dispatch(xn, slot0, slot1, zplan, n_slots, rows):
    t, d = xn.shape
    zstart, zvalid = zplan
    return pl.pallas_call(
        functools.partial(_dispatch_kernel, rows=rows),
        grid=(t // rows,),
        in_specs=[pl.BlockSpec((rows,), lambda i: (i,), memory_space=pltpu.SMEM),
                  pl.BlockSpec((rows,), lambda i: (i,), memory_space=pltpu.SMEM),
                  pl.BlockSpec((2 * N_EXPERTS,), lambda i: (0,), memory_space=pltpu.SMEM),
                  pl.BlockSpec((2 * N_EXPERTS,), lambda i: (0,), memory_space=pltpu.SMEM),
                  pl.BlockSpec((rows, d), lambda i: (i, 0))],
        out_specs=pl.BlockSpec(memory_space=pl.ANY),
        out_shape=jax.ShapeDtypeStruct((n_slots, d), xn.dtype),
        scratch_shapes=[pltpu.VMEM((MOE_TS, d), xn.dtype), pltpu.SemaphoreType.DMA((2,))],
        compiler_params=_cp(("arbitrary",)),
        name="moe_dispatch",
    )(slot0, slot1, zstart, zvalid, xn)


def _expert_kernel(be_ref, nu_ref, x_ref, wi_ref, wo_ref, o_ref, *, ff, n_chunks):
    used = pl.program_id(0) < nu_ref[0]

    @pl.when(used)
    def _():
        o_ref[...] = _swiglu_chunks(x_ref[...].astype(BF16), wi_ref.at[0], wo_ref.at[0], ff, n_chunks)

    @pl.when(jnp.logical_not(used))
    def _():
        o_ref[...] = jnp.zeros_like(o_ref)


def _experts(xs, block_expert, n_used, w_in_bf, w_out_bf, n_chunks):
    n, d = xs.shape
    ff = w_out_bf.shape[1]
    ts = MOE_TS
    grid_spec = pltpu.PrefetchScalarGridSpec(
        num_scalar_prefetch=2,
        grid=(n // ts,),
        in_specs=[pl.BlockSpec((ts, d), lambda j, be, nu: (j, 0)),
                  pl.BlockSpec((1, d, 2 * ff), lambda j, be, nu: (be[j], 0, 0)),
                  pl.BlockSpec((1, ff, d), lambda j, be, nu: (be[j], 0, 0))],
        out_specs=pl.BlockSpec((ts, d), lambda j, be, nu: (j, 0)),
    )
    return pl.pallas_call(
        functools.partial(_expert_kernel, ff=ff, n_chunks=n_chunks),
        grid_spec=grid_spec,
        out_shape=jax.ShapeDtypeStruct((n, d), F32),
        compiler_params=_cp(("arbitrary",)),
        name="moe_experts",
    )(block_expert, n_used, xs, w_in_bf, w_out_bf)


def _combine_kernel(p0_ref, p1_ref, ys_ref, h_ref, w0_ref, w1_ref, g2_ref, fg_ref, o_ref,
                    y0_ref, y1_ref, sem, *, rows):
    def issue(g, carry):
        for u in range(DMA_UNROLL):
            r = g * DMA_UNROLL + u
            pltpu.make_async_copy(ys_ref.at[pl.ds(p0_ref[r], 1)], y0_ref.at[pl.ds(r, 1)],
                                  sem.at[0]).start(priority=u % 2)
            pltpu.make_async_copy(ys_ref.at[pl.ds(p1_ref[r], 1)], y1_ref.at[pl.ds(r, 1)],
                                  sem.at[1]).start(priority=(u + 1) % 2)
        return carry

    lax.fori_loop(0, rows // DMA_UNROLL, issue, 0)
    pltpu.make_async_copy(ys_ref.at[pl.ds(0, rows)], y0_ref, sem.at[0]).wait()
    pltpu.make_async_copy(ys_ref.at[pl.ds(0, rows)], y1_ref, sem.at[1]).wait()
    h = h_ref[...] + g2_ref[0] * (w0_ref[...] * y0_ref[...] + w1_ref[...] * y1_ref[...])
    ms = jnp.mean(h * h, axis=-1, keepdims=True)
    o_ref[...] = h * lax.rsqrt(ms + NORM_EPS) * fg_ref[...]


def _combine(pos0, pos1, ys, h2, w0, w1, g2, final_g, seq, rows):
    t, d = h2.shape
    tps = seq // rows
    blk = pl.BlockSpec((rows, d), lambda i: (i, 0))
    col = pl.BlockSpec((rows, 1), lambda i: (i, 0))
    return pl.pallas_call(
        functools.partial(_combine_kernel, rows=rows),
        grid=(t // rows,),
        in_specs=[pl.BlockSpec((rows,), lambda i: (i,), memory_space=pltpu.SMEM),
                  pl.BlockSpec((rows,), lambda i: (i,), memory_space=pltpu.SMEM),
                  pl.BlockSpec(memory_space=pl.ANY),
                  blk, col, col,
                  pl.BlockSpec((1, 1, d), lambda i: (i // tps, 0, 0)),
                  pl.BlockSpec((1, d), lambda i: (0, 0))],
        out_specs=blk,
        out_shape=jax.ShapeDtypeStruct((t, d), F32),
        scratch_shapes=[pltpu.VMEM((rows, d), F32), pltpu.VMEM((rows, d), F32),
                        pltpu.SemaphoreType.DMA((2,))],
        compiler_params=_cp(("arbitrary",)),
        name="moe_combine_norm",
    )(pos0, pos1, ys, h2, w0, w1, g2, final_g.reshape(1, d))


def _route_plan(idx):
    t = idx.shape[1]
    ts = MOE_TS
    n_slots = 2 * t + N_EXPERTS * ts
    e_flat = idx.reshape(-1)
    onehot = (e_flat[:, None] == jnp.arange(N_EXPERTS, dtype=jnp.int32)[None, :]).astype(jnp.int32)
    csum = jnp.cumsum(onehot, axis=0)
    rank = jnp.sum((csum - onehot) * onehot, axis=1)
    counts = csum[-1]
    padded = ((counts + ts - 1) // ts) * ts
    ends = jnp.cumsum(padded)
    offs = ends - padded
    slot = (jnp.sum(onehot * offs[None, :], axis=1) + rank).astype(jnp.int32)
    starts = jnp.arange(n_slots // ts, dtype=jnp.int32) * ts
    block_expert = jnp.minimum(jnp.sum((starts[:, None] >= ends[None, :]).astype(jnp.int32), axis=1),
                               N_EXPERTS - 1).astype(jnp.int32)
    n_used = (ends[-1] // ts).astype(jnp.int32).reshape(1)
    tail = ends[-1] + jnp.arange(N_EXPERTS, dtype=jnp.int32) * ts
    zstart = jnp.concatenate([ends - ts, tail])
    zvalid = jnp.concatenate([padded > 0, tail < n_slots]).astype(jnp.int32)
    zstart = jnp.where(zvalid != 0, zstart, 0).astype(jnp.int32)
    return slot[:t], slot[t:], block_expert, n_used, (zstart, zvalid), n_slots


def kernel(x, c, ctx, c_ctx, ada_w, ada_b, norm1_g, norm2_g, w_in, w_out, ret_decay_logit, ret_gn_g,
           conv_w, na_rpb, ffn_w_in, ffn_w_out, moe_router_w, moe_router_b, moe_w_in, moe_w_out, final_g):
    b, seq, d = x.shape
    ctx_len = ctx.shape[1]
    depth = ada_w.shape[0]
    assert depth == 2, "the final norm is fused into the last (MoE) layer's combine step"
    rows = seq // GRID_W
    t_lat, t_ctx = b * seq, b * ctx_len
    tm_lat = min(1024, seq)
    tm_ctx = ctx_len

    c8 = jnp.zeros((8, d), F32).at[:b].set(c).at[b].set(c_ctx)
    mods = _ada(c8, ada_w, ada_b)
    tables = _rope_tables(seq)

    h = x.reshape(t_lat, d)
    hc = ctx.reshape(t_ctx, d)
    zero_state = jnp.zeros((b, N_PAIRS, LANES, LANES), F32)
    out = None
    for layer in range(depth):
        update_ctx = layer < depth - 1
        m = mods[layer].reshape(8, 6, d)
        lat = [m[:b, i].reshape(b, 1, d) for i in range(6)]
        cx = [jnp.broadcast_to(m[b, i].reshape(1, 1, d), (b, 1, d)) for i in range(6)]
        w_in_bf = w_in[layer].astype(BF16)
        w_out_bf = w_out[layer].astype(BF16)
        lgt = ret_decay_logit[layer].reshape(2, N_PAIRS, LANES // HEAD_DIM)
        lgt = jnp.repeat(lgt, HEAD_DIM, axis=2).transpose(1, 0, 2)
        gn = ret_gn_g[layer].reshape(N_PAIRS, 1, LANES)

        proj_c = _in_proj(hc, norm1_g[layer], cx[0], cx[1], w_in_bf, None, ctx_len, tm_ctx)
        proj_c = proj_c.reshape(b, ctx_len, D_IN_PROJ)
        proj = _in_proj(h, norm1_g[layer], lat[0], lat[1], w_in_bf, tables, seq, tm_lat)
        proj = proj.reshape(b, seq, D_IN_PROJ)

        y_ret_c, s_fwd, s_bwd = _retention(proj_c, lgt, gn, zero_state, zero_state)
        y_ret, _, _ = _retention(proj, lgt, gn, s_fwd, s_bwd)
        y_conv = _short_conv(proj, conv_w[layer])
        y_na = _na(proj, proj_c, _na_bias(na_rpb[layer], rows))

        if layer % 2 == 0:
            h, xn = _out_proj(y_ret.reshape(t_lat, D_RET), y_conv.reshape(t_lat, D_CONV),
                              y_na.reshape(t_lat, D_NA), h, w_out_bf, lat[2], norm2_g[layer],
                              lat[3], lat[4], seq, tm_lat)
            fw_in = ffn_w_in[layer // 2].astype(BF16)
            fw_out = ffn_w_out[layer // 2].astype(BF16)
            h = _ffn(xn, h, fw_in, fw_out, lat[5], seq, min(512, seq), FFN_CHUNKS)
            if update_ctx:
                y_conv_c = _short_conv(proj_c, conv_w[layer])
                y_na_c = _ctx_attn(proj_c)
                hc, xnc = _out_proj(y_ret_c.reshape(t_ctx, D_RET), y_conv_c.reshape(t_ctx, D_CONV),
                                    y_na_c.reshape(t_ctx, D_NA), hc, w_out_bf, cx[2], norm2_g[layer],
                                    cx[3], cx[4], ctx_len, tm_ctx)
                hc = _ffn(xnc, hc, fw_in, fw_out, cx[5], ctx_len, tm_ctx, FFN_CHUNKS)
        else:
            e = layer // 2
            h, xn, idx, wts = _out_proj(y_ret.reshape(t_lat, D_RET), y_conv.reshape(t_lat, D_CONV),
                                        y_na.reshape(t_lat, D_NA), h, w_out_bf, lat[2], norm2_g[layer],
                                        lat[3], lat[4], seq, tm_lat,
                                        router=(moe_router_w[e], moe_router_b[e]))
            pos0, pos1, block_expert, n_used, zstart, n_slots = _route_plan(idx)
            xs = _dispatch(xn, pos0, pos1, zstart, n_slots, min(1024, seq))
            ys = _experts(xs, block_expert, n_used, moe_w_in[e].astype(BF16), moe_w_out[e].astype(BF16),
                          FFN_CHUNKS)
            out = _combine(pos0, pos1, ys, h, wts[0].reshape(t_lat, 1), wts[1].reshape(t_lat, 1),
                           lat[5], final_g, seq, min(512, seq))
            h = out
    return out.reshape(b, seq, d)
```

```python
import functools

import numpy as np
import jax
import jax.numpy as jnp
from jax import lax
from jax.experimental import pallas as pl
from jax.experimental.pallas import tpu as pltpu

F32 = jnp.float32
BF16 = jnp.bfloat16

LANES = 128
HEAD_DIM = 64
GRID_W = 64
N_CONV_GROUPS = 4
N_RET_HEADS = 6
N_NA_HEADS = 6
D_RET = N_RET_HEADS * HEAD_DIM
D_CONV = N_CONV_GROUPS * HEAD_DIM
D_NA = N_NA_HEADS * HEAD_DIM
D_IN_PROJ = 4 * D_RET + 3 * D_CONV + 3 * D_NA
N_PAIRS = D_RET // LANES
NA_ROWS = 8
NA_COLS = 16
N_EXPERTS = 8
ROPE_BASE = 10000.0
NORM_EPS = 1e-6
NEG_BIG = -1e30
LOG2E = 1.4426950408889634

RET_CHUNK = 256
NA_QROWS = 4
NA_KROWS = 12
MOE_TS = 512
DMA_UNROLL = 8
FFN_CHUNKS = 2
IN_PROJ_SUB = 512
OUT_PROJ_SUB = 512
MXU_COLS = 256
IN_PROJ_CHUNK = 3 * MXU_COLS
VMEM_LIMIT = 56 * 1024 * 1024

CB_RQ, CB_RK, CB_RV, CB_RG = 0, 3, 6, 9
CB_CB, CB_CC, CB_CX = 12, 14, 16
CB_NQ, CB_NK, CB_NV = 18, 21, 24


def _cp(sem, vmem=VMEM_LIMIT):
    return pltpu.CompilerParams(dimension_semantics=sem, vmem_limit_bytes=vmem)


def _silu(x):
    return x * (1.0 / (1.0 + jnp.exp(-x)))


def _dot(a, b):
    return jnp.dot(a, b, preferred_element_type=F32)


def _dot_nt(a, b):
    return lax.dot_general(a, b, (((1,), (1,)), ((), ())), preferred_element_type=F32)


def _dot_tn(a, b):
    return lax.dot_general(a, b, (((0,), (0,)), ((), ())), preferred_element_type=F32)


def _split_bf16(x):
    hi = x.astype(BF16)
    lo = (x - hi.astype(F32)).astype(BF16)
    return hi, lo


def _ada_kernel(c_ref, w_ref, b_ref, o_ref):
    x = _silu(c_ref[...]).astype(BF16)
    o_ref[0] = _dot(x, w_ref[0].astype(BF16)) + b_ref[0]


def _ada(c8, ada_w, ada_b):
    depth, d, n = ada_w.shape
    tn = n // 4
    return pl.pallas_call(
        _ada_kernel,
        grid=(depth, n // tn),
        in_specs=[pl.BlockSpec((8, d), lambda l, j: (0, 0)),
                  pl.BlockSpec((1, d, tn), lambda l, j: (l, 0, j)),
                  pl.BlockSpec((1, 1, tn), lambda l, j: (l, 0, j))],
        out_specs=pl.BlockSpec((1, 8, tn), lambda l, j: (l, 0, j)),
        out_shape=jax.ShapeDtypeStruct((depth, 8, n), F32),
        compiler_params=_cp(("parallel", "parallel")),
        name="ada_mod",
    )(c8, ada_w, ada_b.reshape(depth, 1, n))


def _norm_mod(x, g, sh, sc):
    ms = jnp.mean(x * x, axis=-1, keepdims=True)
    y = x * lax.rsqrt(ms + NORM_EPS) * g
    return y * (1.0 + sc) + sh


def _inproj_kernel(*refs, rope):
    if rope:
        h_ref, g_ref, sh_ref, sc_ref, w_ref, cos_ref, sa_ref, sb_ref, o_ref = refs
    else:
        h_ref, g_ref, sh_ref, sc_ref, w_ref, o_ref = refs
    tm = h_ref.shape[0]
    sub = min(IN_PROJ_SUB, tm)
    cw = IN_PROJ_CHUNK
    for r in range(tm // sub):
        rs = slice(r * sub, (r + 1) * sub)
        xn = _norm_mod(h_ref[rs, :], g_ref[...], sh_ref[0], sc_ref[0]).astype(BF16)
        for c0 in range(0, D_IN_PROJ, cw):
            c1 = min(c0 + cw, D_IN_PROJ)
            acc = _dot(xn, w_ref[:, c0:c1])
            for j in range((c1 - c0) // LANES):
                blk = acc[:, j * LANES:(j + 1) * LANES]
                cb = c0 // LANES + j
                if rope and cb < CB_RV:
                    blk = (blk * cos_ref[rs, :] + pltpu.roll(blk, 16, 1) * sa_ref[rs, :]
                           + pltpu.roll(blk, LANES - 16, 1) * sb_ref[rs, :])
                if CB_RK <= cb < CB_RV:
                    blk = blk * (HEAD_DIM ** -0.5)
                elif CB_NQ <= cb < CB_NK:
                    blk = blk * (HEAD_DIM ** -0.5 * (LOG2E if rope else 1.0))
                o_ref[rs, cb * LANES:(cb + 1) * LANES] = blk.astype(BF16)


def _in_proj(h2, g, sh, sc, w_bf, tables, seq, tm):
    t, d = h2.shape
    tiles_per_seq = seq // tm
    rope = tables is not None
    in_specs = [pl.BlockSpec((tm, d), lambda i: (i, 0)),
                pl.BlockSpec((1, d), lambda i: (0, 0)),
                pl.BlockSpec((1, 1, d), lambda i: (i // tiles_per_seq, 0, 0)),
                pl.BlockSpec((1, 1, d), lambda i: (i // tiles_per_seq, 0, 0)),
                pl.BlockSpec((d, D_IN_PROJ), lambda i: (0, 0))]
    args = [h2, g.reshape(1, d), sh, sc, w_bf]
    if rope:
        in_specs += [pl.BlockSpec((tm, LANES), lambda i: (i % tiles_per_seq, 0))] * 3
        args += list(tables)
    return pl.pallas_call(
        functools.partial(_inproj_kernel, rope=rope),
        grid=(t // tm,),
        in_specs=in_specs,
        out_specs=pl.BlockSpec((tm, D_IN_PROJ), lambda i: (i, 0)),
        out_shape=jax.ShapeDtypeStruct((t, D_IN_PROJ), BF16),
        compiler_params=_cp(("parallel",)),
        name="in_proj_rope" if rope else "in_proj_ctx",
    )(*args)


def _rope_tables(seq):
    t = np.arange(seq)
    row = (t // GRID_W).astype(np.float32)
    col = (t % GRID_W).astype(np.float32)
    n_freq = HEAD_DIM // 4
    inv_freq = (ROPE_BASE ** (-np.arange(n_freq, dtype=np.float32) / n_freq)).astype(np.float32)
    ang_r = row[:, None] * inv_freq
    ang_c = col[:, None] * inv_freq
    cos_h = np.concatenate([np.cos(ang_r), np.cos(ang_r), np.cos(ang_c), np.cos(ang_c)], axis=1)
    sin_h = np.concatenate([np.sin(ang_r), np.sin(ang_r), np.sin(ang_c), np.sin(ang_c)], axis=1)
    lane = np.arange(HEAD_DIM)
    second = (lane % 32) >= 16
    sa = np.where(second[None, :], sin_h, 0.0)
    sb = np.where(second[None, :], 0.0, -sin_h)
    tile2 = lambda a: jnp.asarray(np.concatenate([a, a], axis=1), F32)
    return tile2(cos_h), tile2(sa), tile2(sb)


def _ret_kernel(q_ref, k_ref, v_ref, g_ref, lgt_ref, gn_ref, sf0_ref, sb0_ref,
                y_ref, sfo_ref, sbo_ref, sfs_ref, sbs_ref, *, seq):
    c = RET_CHUNK
    n_chunks = seq // c
    lg = jax.nn.log_sigmoid(lgt_ref[0])
    lgf, lgb = lg[0:1, :], lg[1:2, :]
    pos = lax.broadcasted_iota(jnp.int32, (c, 1), 0).astype(F32)
    dkf = jnp.exp(lgf * (c - 1.0 - pos))
    dkb = jnp.exp(lgb * pos)
    dqf = jnp.exp(lgf * (pos + 1.0))
    dqb = jnp.exp(lgb * (c - pos))
    cdf = jnp.exp(lgf * float(c))
    cdb = jnp.exp(lgb * float(c))
    lane = lax.broadcasted_iota(jnp.int32, (1, LANES), 1)
    first = lane < HEAD_DIM
    ri = lax.broadcasted_iota(jnp.int32, (LANES, LANES), 0)
    ci = lax.broadcasted_iota(jnp.int32, (LANES, LANES), 1)
    same = (ri < HEAD_DIM) == (ci < HEAD_DIM)
    bd = same.astype(F32)

    def head_mean(x):
        a = jnp.sum(jnp.where(first, x, 0.0), axis=-1, keepdims=True)
        b = jnp.sum(jnp.where(first, 0.0, x), axis=-1, keepdims=True)
        return jnp.where(first, a, b) * (1.0 / HEAD_DIM)

    ii = lax.broadcasted_iota(jnp.int32, (c, c), 0)
    jj = lax.broadcasted_iota(jnp.int32, (c, c), 1)
    dif = (ii - jj).astype(F32)

    def decay_mask(h0):
        lf = lgf[:, h0:h0 + 1]
        lb = lgb[:, h0:h0 + 1]
        return jnp.where(dif > 0, jnp.exp(lf * jnp.maximum(dif, 0.0)),
                         jnp.where(dif < 0, jnp.exp(lb * jnp.maximum(-dif, 0.0)), 2.0))

    dm = (decay_mask(0), decay_mask(HEAD_DIM))
    mfirst = first.astype(BF16)
    msecond = (1.0 - first.astype(F32)).astype(BF16)

    unroll = min(8, n_chunks)

    def incr(n, carry):
        sl = pl.ds(pl.multiple_of(n * c, c), c)
        kf = k_ref[0, sl, :].astype(F32)
        v = v_ref[0, sl, :]
        sfs_ref[n] = _dot_tn((kf * dkf).astype(BF16), v) * bd
        sbs_ref[n] = _dot_tn((kf * dkb).astype(BF16), v) * bd
        return carry

    lax.fori_loop(0, n_chunks, incr, 0, unroll=unroll)

    def scan_f(n, s):
        u = sfs_ref[n]
        sfs_ref[n] = s
        return cdf * s + u

    def scan_b(i, s):
        n = n_chunks - 1 - i
        u = sbs_ref[n]
        sbs_ref[n] = s
        return cdb * s + u

    sfo_ref[0, 0] = lax.fori_loop(0, n_chunks, scan_f, sf0_ref[0, 0])
    sbo_ref[0, 0] = lax.fori_loop(0, n_chunks, scan_b, sb0_ref[0, 0])

    def outp(n, carry):
        sl = pl.ds(pl.multiple_of(n * c, c), c)
        q = q_ref[0, sl, :]
        k = k_ref[0, sl, :]
        v = v_ref[0, sl, :]
        qf = q.astype(F32)
        o = None
        for hh, mk in enumerate((mfirst, msecond)):
            s = _dot_nt(q * mk, k) * dm[hh]
            oh = _dot(s.astype(BF16), v)
            o = oh if o is None else jnp.where(first, o, oh)
        qcat = jnp.concatenate([(qf * dqf).astype(BF16), (qf * dqb).astype(BF16)], axis=1)
        scat = jnp.concatenate([sfs_ref[n], sbs_ref[n]], axis=0).astype(BF16)
        o = o + _dot(qcat, scat)
        mu = head_mean(o)
        dlt = o - mu
        var = head_mean(dlt * dlt)
        on = dlt * lax.rsqrt(var + NORM_EPS) * gn_ref[0]
        y_ref[0, sl, :] = (_silu(g_ref[0, sl, :].astype(F32)) * on).astype(BF16)
        return carry

    lax.fori_loop(0, n_chunks, outp, 0, unroll=unroll)


def _retention(proj, lgt, gn, sf0, sb0):
    b, seq, _ = proj.shape
    col = lambda off: pl.BlockSpec((1, seq, LANES), lambda bi, p: (bi, 0, off + p))
    st = pl.BlockSpec((1, 1, LANES, LANES), lambda bi, p: (bi, p, 0, 0))
    return pl.pallas_call(
        functools.partial(_ret_kernel, seq=seq),
        grid=(b, N_PAIRS),
        in_specs=[col(CB_RQ), col(CB_RK), col(CB_RV), col(CB_RG),
                  pl.BlockSpec((1, 2, LANES), lambda bi, p: (p, 0, 0)),
                  pl.BlockSpec((1, 1, LANES), lambda bi, p: (p, 0, 0)),
                  st, st],
        out_specs=[pl.BlockSpec((1, seq, LANES), lambda bi, p: (bi, 0, p)), st, st],
        out_shape=[jax.ShapeDtypeStruct((b, seq, D_RET), BF16),
                   jax.ShapeDtypeStruct((b, N_PAIRS, LANES, LANES), F32),
                   jax.ShapeDtypeStruct((b, N_PAIRS, LANES, LANES), F32)],
        scratch_shapes=[pltpu.VMEM((seq // RET_CHUNK, LANES, LANES), F32),
                        pltpu.VMEM((seq // RET_CHUNK, LANES, LANES), F32)],
        compiler_params=_cp(("parallel", "parallel")),
        name="retention_s%d" % seq,
    )(proj, proj, proj, proj, lgt, gn, sf0, sb0)


def _conv_kernel(b_ref, c_ref, x_ref, w_ref, y_ref, *, seq):
    u = c_ref[0].astype(F32) * x_ref[0].astype(F32)
    row = lax.broadcasted_iota(jnp.int32, (seq, 1), 0)
    prev = jnp.where(row == 0, 0.0, pltpu.roll(u, 1, 0))
    nxt = jnp.where(row == seq - 1, 0.0, pltpu.roll(u, seq - 1, 0))
    y = w_ref[0:1, :] * prev + w_ref[1:2, :] * u + w_ref[2:3, :] * nxt
    y_ref[0] = (b_ref[0].astype(F32) * y).astype(BF16)


def _short_conv(proj, conv_w):
    b, seq, _ = proj.shape
    nblk = D_CONV // LANES
    col = lambda off: pl.BlockSpec((1, seq, LANES), lambda bi, j: (bi, 0, off + j))
    return pl.pallas_call(
        functools.partial(_conv_kernel, seq=seq),
        grid=(b, nblk),
        in_specs=[col(CB_CB), col(CB_CC), col(CB_CX),
                  pl.BlockSpec((3, LANES), lambda bi, j: (0, j))],
        out_specs=pl.BlockSpec((1, seq, LANES), lambda bi, j: (bi, 0, j)),
        out_shape=jax.ShapeDtypeStruct((b, seq, D_CONV), BF16),
        compiler_params=_cp(("parallel", "parallel")),
        name="short_conv_s%d" % seq,
    )(proj, proj, proj, conv_w)


def _na_bias_plan(rows):
    plan = np.full((3, NA_QROWS, NA_KROWS), -1, np.int64)
    for cl, rb in enumerate((0, NA_QROWS, rows - NA_QROWS)):
        ws = int(np.clip(rb - NA_ROWS // 2, 0, rows - NA_KROWS))
        for i in range(NA_QROWS):
            r = rb + i
            r0 = int(np.clip(r - NA_ROWS // 2, 0, rows - NA_ROWS))
            for j in range(NA_KROWS):
                kr = ws + j
                if r0 <= kr < r0 + NA_ROWS:
                    plan[cl, i, j] = kr - r + (NA_ROWS - 1)
    return plan


def _na_bias_kernel(rpb_ref, o_ref, *, plan):
    w = GRID_W
    ndr, ndc = 2 * NA_ROWS - 1, 2 * NA_COLS - 1
    base = pl.program_id(0) * (ndr * ndc)
    c = lax.broadcasted_iota(jnp.int32, (w, w), 0)
    kc = lax.broadcasted_iota(jnp.int32, (w, w), 1)
    dcidx = jnp.clip(kc - c, -(NA_COLS - 1), NA_COLS - 1) + (NA_COLS - 1)
    c0 = jnp.clip(c - NA_COLS // 2, 0, w - NA_COLS)
    col_in = jnp.logical_and(kc >= c0, kc < c0 + NA_COLS)
    neg = jnp.full((w, w), NEG_BIG, F32)
    tiles = []
    for dr in range(ndr):
        t = neg
        for dcv in range(ndc):
            t = jnp.where(dcidx == dcv, rpb_ref[base + dr * ndc + dcv], t)
        tiles.append(jnp.where(col_in, t * LOG2E, NEG_BIG))
    for cl in range(plan.shape[0]):
        for i in range(plan.shape[1]):
            for j in range(0, plan.shape[2], 2):
                pair = [tiles[int(d)] if d >= 0 else neg for d in plan[cl, i, j:j + 2]]
                o_ref[0, cl, i * w:(i + 1) * w, j * w:(j + 2) * w] = jnp.concatenate(pair, axis=1)


def _na_bias(rpb, rows):
    h = rpb.shape[0]
    tq, tk = NA_QROWS * GRID_W, NA_KROWS * GRID_W
    return pl.pallas_call(
        functools.partial(_na_bias_kernel, plan=_na_bias_plan(rows)),
        grid=(h,),
        in_specs=[pl.BlockSpec(memory_space=pltpu.SMEM)],
        out_specs=pl.BlockSpec((1, 3, tq, tk), lambda hh: (hh, 0, 0, 0)),
        out_shape=jax.ShapeDtypeStruct((h, 3, tq, tk), F32),
        compiler_params=_cp(("parallel",)),
        name="na_bias_expand",
    )(rpb.reshape(-1))


def _na_kernel(q_ref, k_ref, v_ref, kc_ref, vc_ref, bias_ref, o_ref, v1_ref, vc1_ref, *, rows):
    qb = pl.program_id(1)
    lane = lax.broadcasted_iota(jnp.int32, (1, LANES), 1)
    first = lane < HEAD_DIM

    @pl.when(qb == 0)
    def _():
        one = jnp.ones((1, LANES), BF16)
        for p in range(N_PAIRS):
            cols = slice(p * LANES, (p + 1) * LANES)
            v1_ref[2 * p] = jnp.where(first, v_ref[0, :, cols], one)
            v1_ref[2 * p + 1] = jnp.where(first, one, v_ref[0, :, cols])
            vc1_ref[2 * p] = jnp.where(first, vc_ref[0, :, cols], one)
            vc1_ref[2 * p + 1] = jnp.where(first, one, vc_ref[0, :, cols])

    ws = jnp.clip(qb * NA_QROWS - NA_ROWS // 2, 0, rows - NA_KROWS)
    sl = pl.ds(pl.multiple_of(ws * GRID_W, GRID_W), NA_KROWS * GRID_W)
    for p in range(N_PAIRS):
        cols = slice(p * LANES, (p + 1) * LANES)
        q = q_ref[0, :, cols]
        kw = k_ref[0, sl, cols]
        kc = kc_ref[0, :, cols]
        pv = []
        for hh in range(2):
            mk = (first if hh == 0 else jnp.logical_not(first)).astype(BF16)
            qh = q * mk
            s_loc = _dot_nt(qh, kw) + bias_ref[2 * p + hh, 0]
            s_ctx = _dot_nt(qh, kc)
            m = jnp.maximum(jnp.max(s_loc, axis=-1, keepdims=True), jnp.max(s_ctx, axis=-1, keepdims=True))
            p_loc = jnp.exp2(s_loc - m).astype(BF16)
            p_ctx = jnp.exp2(s_ctx - m).astype(BF16)
            pv.append(_dot(p_loc, v1_ref[2 * p + hh, sl, :]) + _dot(p_ctx, vc1_ref[2 * p + hh]))
        num = jnp.where(first, pv[0], pv[1])
        den = jnp.where(first, pltpu.roll(pv[0], HEAD_DIM, 1), pltpu.roll(pv[1], HEAD_DIM, 1))
        o_ref[0, :, cols] = (num / den).astype(BF16)


def _na(proj, proj_ctx, bias):
    b, seq, _ = proj.shape
    ctx_len = proj_ctx.shape[1]
    rows = seq // GRID_W
    nqb = rows // NA_QROWS
    tq = NA_QROWS * GRID_W
    tk = NA_KROWS * GRID_W
    nh = 2 * N_PAIRS
    cq, ck, cv = (CB_NQ * LANES // D_NA, CB_NK * LANES // D_NA, CB_NV * LANES // D_NA)

    def cls(qb):
        return jnp.where(qb == 0, 0, jnp.where(qb == nqb - 1, 2, 1))

    return pl.pallas_call(
        functools.partial(_na_kernel, rows=rows),
        grid=(b, nqb),
        in_specs=[pl.BlockSpec((1, tq, D_NA), lambda bi, qb: (bi, qb, cq)),
                  pl.BlockSpec((1, seq, D_NA), lambda bi, qb: (bi, 0, ck)),
                  pl.BlockSpec((1, seq, D_NA), lambda bi, qb: (bi, 0, cv)),
                  pl.BlockSpec((1, ctx_len, D_NA), lambda bi, qb: (bi, 0, ck)),
                  pl.BlockSpec((1, ctx_len, D_NA), lambda bi, qb: (bi, 0, cv)),
                  pl.BlockSpec((nh, 1, tq, tk), lambda bi, qb: (0, cls(qb), 0, 0))],
        out_specs=pl.BlockSpec((1, tq, D_NA), lambda bi, qb: (bi, qb, 0)),
        out_shape=jax.ShapeDtypeStruct((b, seq, D_NA), BF16),
        scratch_shapes=[pltpu.VMEM((nh, seq, LANES), BF16), pltpu.VMEM((nh, ctx_len, LANES), BF16)],
        compiler_params=_cp(("parallel", "arbitrary")),
        name="na_attn",
    )(proj, proj, proj, proj_ctx, proj_ctx, bias)


def _ctx_attn_kernel(q_ref, k_ref, v_ref, o_ref):
    q = q_ref[0]
    k = k_ref[0]
    v = v_ref[0]
    lane = lax.broadcasted_iota(jnp.int32, (1, LANES), 1)
    first = lane < HEAD_DIM
    o = None
    for hh in range(2):
        mk = (first if hh == 0 else jnp.logical_not(first)).astype(BF16)
        s = _dot_nt(q * mk, k)
        m = jnp.max(s, axis=-1, keepdims=True)
        p = jnp.exp(s - m)
        l = jnp.sum(p, axis=-1, keepdims=True)
        oh = _dot(p.astype(BF16), v) / l
        o = oh if o is None else jnp.where(first, o, oh)
    o_ref[0] = o.astype(BF16)


def _ctx_attn(proj_ctx):
    b, ctx_len, _ = proj_ctx.shape
    col = lambda off: pl.BlockSpec((1, ctx_len, LANES), lambda bi, p: (bi, 0, off + p))
    return pl.pallas_call(
        _ctx_attn_kernel,
        grid=(b, N_PAIRS),
        in_specs=[col(CB_NQ), col(CB_NK), col(CB_NV)],
        out_specs=pl.BlockSpec((1, ctx_len, LANES), lambda bi, p: (bi, 0, p)),
        out_shape=jax.ShapeDtypeStruct((b, ctx_len, D_NA), BF16),
        compiler_params=_cp(("parallel", "parallel")),
        name="ctx_attn",
    )(proj_ctx, proj_ctx, proj_ctx)


def _outproj_kernel(*refs, route):
    if route:
        (yr_ref, yc_ref, yn_ref, h_ref, w_ref, g1_ref, n2_ref, sh_ref, sc_ref, rw_ref, rb_ref,
         ho_ref, xn_ref, idx_ref, wt_ref) = refs
    else:
        yr_ref, yc_ref, yn_ref, h_ref, w_ref, g1_ref, n2_ref, sh_ref, sc_ref, ho_ref, xn_ref = refs
    tm = h_ref.shape[0]
    sub = min(OUT_PROJ_SUB, tm)
    if route:
        r_hi, r_lo = _split_bf16(rw_ref[...])
        r_both = jnp.concatenate([r_hi, r_lo], axis=0)
    for r in range(tm // sub):
        rs = slice(r * sub, (r + 1) * sub)
        ycat = jnp.concatenate([yr_ref[rs, :], yc_ref[rs, :], yn_ref[rs, :]], axis=-1)
        h = h_ref[rs, :] + g1_ref[0] * _dot(ycat, w_ref[...])
        ho_ref[rs, :] = h
        a = _norm_mod(h, n2_ref[...], sh_ref[0], sc_ref[0])
        xn_ref[rs, :] = a.astype(xn_ref.dtype)
        if route:
            a_hi, a_lo = _split_bf16(a)
            t_hi = _dot_nt(r_both, a_hi)
            logits = t_hi[:N_EXPERTS] + t_hi[N_EXPERTS:] + _dot_nt(r_hi, a_lo) + rb_ref[...]
            eidx = lax.broadcasted_iota(jnp.int32, logits.shape, 0)
            m1 = jnp.max(logits, axis=0, keepdims=True)
            i1 = jnp.min(jnp.where(logits == m1, eidx, N_EXPERTS), axis=0, keepdims=True)
            rest = jnp.where(eidx == i1, -jnp.inf, logits)
            m2 = jnp.max(rest, axis=0, keepdims=True)
            i2 = jnp.min(jnp.where(rest == m2, eidx, N_EXPERTS), axis=0, keepdims=True)
            e2 = jnp.exp(m2 - m1)
            den = 1.0 + e2
            idx_ref[:, rs] = jnp.concatenate([i1, i2], axis=0)
            wt_ref[:, rs] = jnp.concatenate([1.0 / den, e2 / den], axis=0)


def _out_proj(yr, yc, yn, h2, w_bf, g1, n2g, sh2, sc2, seq, tm, router=None):
    t, d = h2.shape
    tps = seq // tm
    route = router is not None
    row = lambda wdt: pl.BlockSpec((tm, wdt), lambda i: (i, 0))
    mod = pl.BlockSpec((1, 1, d), lambda i: (i // tps, 0, 0))
    in_specs = [row(D_RET), row(D_CONV), row(D_NA), row(d),
                pl.BlockSpec((d, d), lambda i: (0, 0)), mod,
                pl.BlockSpec((1, d), lambda i: (0, 0)), mod, mod]
    args = [yr, yc, yn, h2, w_bf, g1, n2g.reshape(1, d), sh2, sc2]
    out_specs = [row(d), row(d)]
    out_shape = [jax.ShapeDtypeStruct((t, d), F32), jax.ShapeDtypeStruct((t, d), F32 if route else BF16)]
    if route:
        rw, rb = router
        in_specs += [pl.BlockSpec((N_EXPERTS, d), lambda i: (0, 0)),
                     pl.BlockSpec((N_EXPERTS, 1), lambda i: (0, 0))]
        args += [rw.T, rb.reshape(N_EXPERTS, 1)]
        out_specs += [pl.BlockSpec((2, tm), lambda i: (0, i))] * 2
        out_shape += [jax.ShapeDtypeStruct((2, t), jnp.int32), jax.ShapeDtypeStruct((2, t), F32)]
    return pl.pallas_call(
        functools.partial(_outproj_kernel, route=route),
        grid=(t // tm,),
        in_specs=in_specs,
        out_specs=out_specs,
        out_shape=out_shape,
        compiler_params=_cp(("parallel",)),
        name="out_proj_route" if route else "out_proj_t%d" % t,
    )(*args)


def _swiglu_chunks(x, w_in_ref, w_out_ref, ff, n_chunks):
    tf = -(-ff // (n_chunks * MXU_COLS)) * MXU_COLS
    y = None
    for c0 in range(0, ff, tf):
        c1 = min(c0 + tf, ff)
        gate = _dot(x, w_in_ref[:, c0:c1])
        up = _dot(x, w_in_ref[:, ff + c0:ff + c1])
        part = _dot((_silu(gate) * up).astype(BF16), w_out_ref[c0:c1, :])
        y = part if y is None else y + part
    return y


def _ffn_kernel(x_ref, h_ref, wi_ref, wo_ref, g2_ref, o_ref, *, ff, n_chunks):
    y = _swiglu_chunks(x_ref[...], wi_ref, wo_ref, ff, n_chunks)
    o_ref[...] = h_ref[...] + g2_ref[0] * y


def _ffn(xn, h2, w_in_bf, w_out_bf, g2, seq, tm, n_chunks):
    t, d = h2.shape
    ff = w_out_bf.shape[0]
    tps = seq // tm
    resident = pl.Buffered(1)
    return pl.pallas_call(
        functools.partial(_ffn_kernel, ff=ff, n_chunks=n_chunks),
        grid=(t // tm,),
        in_specs=[pl.BlockSpec((tm, d), lambda i: (i, 0)),
                  pl.BlockSpec((tm, d), lambda i: (i, 0)),
                  pl.BlockSpec((d, 2 * ff), lambda i: (0, 0), pipeline_mode=resident),
                  pl.BlockSpec((ff, d), lambda i: (0, 0), pipeline_mode=resident),
                  pl.BlockSpec((1, 1, d), lambda i: (i // tps, 0, 0))],
        out_specs=pl.BlockSpec((tm, d), lambda i: (i, 0)),
        out_shape=jax.ShapeDtypeStruct((t, d), F32),
        compiler_params=_cp(("parallel",)),
        name="ffn_t%d" % t,
    )(xn, h2, w_in_bf, w_out_bf, g2)


def _dispatch_kernel(s0_ref, s1_ref, zs_ref, zv_ref, x_ref, xs_ref, zbuf_ref, sem, *, rows):
    @pl.when(pl.program_id(0) == 0)
    def _():
        zbuf_ref[...] = jnp.zeros_like(zbuf_ref)
        for e in range(2 * N_EXPERTS):
            @pl.when(zv_ref[e] != 0)
            def _():
                dst = xs_ref.at[pl.ds(pl.multiple_of(zs_ref[e], MOE_TS), MOE_TS)]
                pltpu.make_async_copy(zbuf_ref, dst, sem.at[1]).start()
        for e in range(2 * N_EXPERTS):
            @pl.when(zv_ref[e] != 0)
            def _():
                pltpu.make_async_copy(zbuf_ref, xs_ref.at[pl.ds(0, MOE_TS)], sem.at[1]).wait()

    def issue(g, carry):
        for u in range(DMA_UNROLL):
            r = g * DMA_UNROLL + u
            src = x_ref.at[pl.ds(r, 1)]
            pltpu.make_async_copy(src, xs_ref.at[pl.ds(s0_ref[r], 1)], sem.at[0]).start(priority=u % 2)
            pltpu.make_async_copy(src, xs_ref.at[pl.ds(s1_ref[r], 1)], sem.at[0]).start(priority=(u + 1) % 2)
        return carry

    lax.fori_loop(0, rows // DMA_UNROLL, issue, 0)
    for _ in range(2):
        pltpu.make_async_copy(x_ref, xs_ref.at[pl.ds(0, rows)], sem.at[0]).wait()


def _dispatch(xn, slot0, slot1, zplan, n_slots, rows):
    t, d = xn.shape
    zstart, zvalid = zplan
    return pl.pallas_call(
        functools.partial(_dispatch_kernel, rows=rows),
        grid=(t // rows,),
        in_specs=[pl.BlockSpec((rows,), lambda i: (i,), memory_space=pltpu.SMEM),
                  pl.BlockSpec((rows,), lambda i: (i,), memory_space=pltpu.SMEM),
                  pl.BlockSpec((2 * N_EXPERTS,), lambda i: (0,), memory_space=pltpu.SMEM),
                  pl.BlockSpec((2 * N_EXPERTS,), lambda i: (0,), memory_space=pltpu.SMEM),
                  pl.BlockSpec((rows, d), lambda i: (i, 0))],
        out_specs=pl.BlockSpec(memory_space=pl.ANY),
        out_shape=jax.ShapeDtypeStruct((n_slots, d), xn.dtype),
        scratch_shapes=[pltpu.VMEM((MOE_TS, d), xn.dtype), pltpu.SemaphoreType.DMA((2,))],
        compiler_params=_cp(("arbitrary",)),
        name="moe_dispatch",
    )(slot0, slot1, zstart, zvalid, xn)


def _expert_kernel(be_ref, nu_ref, x_ref, wi_ref, wo_ref, o_ref, *, ff, n_chunks):
    used = pl.program_id(0) < nu_ref[0]

    @pl.when(used)
    def _():
        o_ref[...] = _swiglu_chunks(x_ref[...].astype(BF16), wi_ref.at[0], wo_ref.at[0], ff, n_chunks)

    @pl.when(jnp.logical_not(used))
    def _():
        o_ref[...] = jnp.zeros_like(o_ref)


def _experts(xs, block_expert, n_used, w_in_bf, w_out_bf, n_chunks):
    n, d = xs.shape
    ff = w_out_bf.shape[1]
    ts = MOE_TS
    grid_spec = pltpu.PrefetchScalarGridSpec(
        num_scalar_prefetch=2,
        grid=(n // ts,),
        in_specs=[pl.BlockSpec((ts, d), lambda j, be, nu: (j, 0)),
                  pl.BlockSpec((1, d, 2 * ff), lambda j, be, nu: (be[j], 0, 0)),
                  pl.BlockSpec((1, ff, d), lambda j, be, nu: (be[j], 0, 0))],
        out_specs=pl.BlockSpec((ts, d), lambda j, be, nu: (j, 0)),
    )
    return pl.pallas_call(
        functools.partial(_expert_kernel, ff=ff, n_chunks=n_chunks),
        grid_spec=grid_spec,
        out_shape=jax.ShapeDtypeStruct((n, d), F32),
        compiler_params=_cp(("arbitrary",)),
        name="moe_experts",
    )(block_expert, n_used, xs, w_in_bf, w_out_bf)


def _combine_kernel(p0_ref, p1_ref, ys_ref, h_ref, w0_ref, w1_ref, g2_ref, fg_ref, o_ref,
                    y0_ref, y1_ref, sem, *, rows):
    def issue(g, carry):
        for u in range(DMA_UNROLL):
            r = g * DMA_UNROLL + u
            pltpu.make_async_copy(ys_ref.at[pl.ds(p0_ref[r], 1)], y0_ref.at[pl.ds(r, 1)],
                                  sem.at[0]).start(priority=u % 2)
            pltpu.make_async_copy(ys_ref.at[pl.ds(p1_ref[r], 1)], y1_ref.at[pl.ds(r, 1)],
                                  sem.at[1]).start(priority=(u + 1) % 2)
        return carry

    lax.fori_loop(0, rows // DMA_UNROLL, issue, 0)
    pltpu.make_async_copy(ys_ref.at[pl.ds(0, rows)], y0_ref, sem.at[0]).wait()
    pltpu.make_async_copy(ys_ref.at[pl.ds(0, rows)], y1_ref, sem.at[1]).wait()
    h = h_ref[...] + g2_ref[0] * (w0_ref[...] * y0_ref[...] + w1_ref[...] * y1_ref[...])
    ms = jnp.mean(h * h, axis=-1, keepdims=True)
    o_ref[...] = h * lax.rsqrt(ms + NORM_EPS) * fg_ref[...]


def _combine(pos0, pos1, ys, h2, w0, w1, g2, final_g, seq, rows):
    t, d = h2.shape
    tps = seq // rows
    blk = pl.BlockSpec((rows, d), lambda i: (i, 0))
    col = pl.BlockSpec((rows, 1), lambda i: (i, 0))
    return pl.pallas_call(
        functools.partial(_combine_kernel, rows=rows),
        grid=(t // rows,),
        in_specs=[pl.BlockSpec((rows,), lambda i: (i,), memory_space=pltpu.SMEM),
                  pl.BlockSpec((rows,), lambda i: (i,), memory_space=pltpu.SMEM),
                  pl.BlockSpec(memory_space=pl.ANY),
                  blk, col, col,
                  pl.BlockSpec((1, 1, d), lambda i: (i // tps, 0, 0)),
                  pl.BlockSpec((1, d), lambda i: (0, 0))],
        out_specs=blk,
        out_shape=jax.ShapeDtypeStruct((t, d), F32),
        scratch_shapes=[pltpu.VMEM((rows, d), F32), pltpu.VMEM((rows, d), F32),
                        pltpu.SemaphoreType.DMA((2,))],
        compiler_params=_cp(("arbitrary",)),
        name="moe_combine_norm",
    )(pos0, pos1, ys, h2, w0, w1, g2, final_g.reshape(1, d))


def _route_plan(idx):
    t = idx.shape[1]
    ts = MOE_TS
    n_slots = 2 * t + N_EXPERTS * ts
    e_flat = idx.reshape(-1)
    onehot = (e_flat[:, None] == jnp.arange(N_EXPERTS, dtype=jnp.int32)[None, :]).astype(jnp.int32)
    csum = jnp.cumsum(onehot, axis=0)
    rank = jnp.sum((csum - onehot) * onehot, axis=1)
    counts = csum[-1]
    padded = ((counts + ts - 1) // ts) * ts
    ends = jnp.cumsum(padded)
    offs = ends - padded
    slot = (jnp.sum(onehot * offs[None, :], axis=1) + rank).astype(jnp.int32)
    starts = jnp.arange(n_slots // ts, dtype=jnp.int32) * ts
    block_expert = jnp.minimum(jnp.sum((starts[:, None] >= ends[None, :]).astype(jnp.int32), axis=1),
                               N_EXPERTS - 1).astype(jnp.int32)
    n_used = (ends[-1] // ts).astype(jnp.int32).reshape(1)
    tail = ends[-1] + jnp.arange(N_EXPERTS, dtype=jnp.int32) * ts
    zstart = jnp.concatenate([ends - ts, tail])
    zvalid = jnp.concatenate([padded > 0, tail < n_slots]).astype(jnp.int32)
    zstart = jnp.where(zvalid != 0, zstart, 0).astype(jnp.int32)
    return slot[:t], slot[t:], block_expert, n_used, (zstart, zvalid), n_slots


def kernel(x, c, ctx, c_ctx, ada_w, ada_b, norm1_g, norm2_g, w_in, w_out, ret_decay_logit, ret_gn_g,
           conv_w, na_rpb, ffn_w_in, ffn_w_out, moe_router_w, moe_router_b, moe_w_in, moe_w_out, final_g):
    b, seq, d = x.shape
    ctx_len = ctx.shape[1]
    depth = ada_w.shape[0]
    assert depth == 2, "the final norm is fused into the last (MoE) layer's combine step"
    rows = seq // GRID_W
    t_lat, t_ctx = b * seq, b * ctx_len
    tm_lat = min(1024, seq)
    tm_ctx = ctx_len

    c8 = jnp.zeros((8, d), F32).at[:b].set(c).at[b].set(c_ctx)
    mods = _ada(c8, ada_w, ada_b)
    tables = _rope_tables(seq)

    h = x.reshape(t_lat, d)
    hc = ctx.reshape(t_ctx, d)
    zero_state = jnp.zeros((b, N_PAIRS, LANES, LANES), F32)
    out = None
    for layer in range(depth):
        update_ctx = layer < depth - 1
        m = mods[layer].reshape(8, 6, d)
        lat = [m[:b, i].reshape(b, 1, d) for i in range(6)]
        cx = [jnp.broadcast_to(m[b, i].reshape(1, 1, d), (b, 1, d)) for i in range(6)]
        w_in_bf = w_in[layer].astype(BF16)
        w_out_bf = w_out[layer].astype(BF16)
        lgt = ret_decay_logit[layer].reshape(2, N_PAIRS, LANES // HEAD_DIM)
        lgt = jnp.repeat(lgt, HEAD_DIM, axis=2).transpose(1, 0, 2)
        gn = ret_gn_g[layer].reshape(N_PAIRS, 1, LANES)

        proj_c = _in_proj(hc, norm1_g[layer], cx[0], cx[1], w_in_bf, None, ctx_len, tm_ctx)
        proj_c = proj_c.reshape(b, ctx_len, D_IN_PROJ)
        proj = _in_proj(h, norm1_g[layer], lat[0], lat[1], w_in_bf, tables, seq, tm_lat)
        proj = proj.reshape(b, seq, D_IN_PROJ)

        y_ret_c, s_fwd, s_bwd = _retention(proj_c, lgt, gn, zero_state, zero_state)
        y_ret, _, _ = _retention(proj, lgt, gn, s_fwd, s_bwd)
        y_conv = _short_conv(proj, conv_w[layer])
        y_na = _na(proj, proj_c, _na_bias(na_rpb[layer], rows))

        if layer % 2 == 0:
            h, xn = _out_proj(y_ret.reshape(t_lat, D_RET), y_conv.reshape(t_lat, D_CONV),
                              y_na.reshape(t_lat, D_NA), h, w_out_bf, lat[2], norm2_g[layer],
                              lat[3], lat[4], seq, tm_lat)
            fw_in = ffn_w_in[layer // 2].astype(BF16)
            fw_out = ffn_w_out[layer // 2].astype(BF16)
            h = _ffn(xn, h, fw_in, fw_out, lat[5], seq, min(512, seq), FFN_CHUNKS)
            if update_ctx:
                y_conv_c = _short_conv(proj_c, conv_w[layer])
                y_na_c = _ctx_attn(proj_c)
                hc, xnc = _out_proj(y_ret_c.reshape(t_ctx, D_RET), y_conv_c.reshape(t_ctx, D_CONV),
                                    y_na_c.reshape(t_ctx, D_NA), hc, w_out_bf, cx[2], norm2_g[layer],
                                    cx[3], cx[4], ctx_len, tm_ctx)
                hc = _ffn(xnc, hc, fw_in, fw_out, cx[5], ctx_len, tm_ctx, FFN_CHUNKS)
        else:
            e = layer // 2
            h, xn, idx, wts = _out_proj(y_ret.reshape(t_lat, D_RET), y_conv.reshape(t_lat, D_CONV),
                                        y_na.reshape(t_lat, D_NA), h, w_out_bf, lat[2], norm2_g[layer],
                                        lat[3], lat[4], seq, tm_lat,
                                        router=(moe_router_w[e], moe_router_b[e]))
            pos0, pos1, block_expert, n_used, zstart, n_slots = _route_plan(idx)
            xs = _dispatch(xn, pos0, pos1, zstart, n_slots, min(1024, seq))
            ys = _experts(xs, block_expert, n_used, moe_w_in[e].astype(BF16), moe_w_out[e].astype(BF16),
                          FFN_CHUNKS)
            out = _combine(pos0, pos1, ys, h, wts[0].reshape(t_lat, 1), wts[1].reshape(t_lat, 1),
                           lat[5], final_g, seq, min(512, seq))
            h = out
    return out.reshape(b, seq, d)
```

```python
import functools

import numpy as np
import jax
import jax.numpy as jnp
from jax import lax
from jax.experimental import pallas as pl
from jax.experimental.pallas import tpu as pltpu

F32 = jnp.float32
BF16 = jnp.bfloat16

LANES = 128
HEAD_DIM = 64
GRID_W = 64
N_CONV_GROUPS = 4
N_RET_HEADS = 6
N_NA_HEADS = 6
D_RET = N_RET_HEADS * HEAD_DIM
D_CONV = N_CONV_GROUPS * HEAD_DIM
D_NA = N_NA_HEADS * HEAD_DIM
D_IN_PROJ = 4 * D_RET + 3 * D_CONV + 3 * D_NA
N_PAIRS = D_RET // LANES
NA_ROWS = 8
NA_COLS = 16
N_EXPERTS = 8
ROPE_BASE = 10000.0
NORM_EPS = 1e-6
NEG_BIG = -1e30
LOG2E = 1.4426950408889634

RET_CHUNK = 256
NA_QROWS = 4
NA_KROWS = 12
MOE_TS = 512
DMA_UNROLL = 8
FFN_CHUNKS = 2
IN_PROJ_SUB = 512
OUT_PROJ_SUB = 512
MXU_COLS = 256
IN_PROJ_CHUNK = 3 * MXU_COLS
VMEM_LIMIT = 56 * 1024 * 1024

CB_RQ, CB_RK, CB_RV, CB_RG = 0, 3, 6, 9
CB_CB, CB_CC, CB_CX = 12, 14, 16
CB_NQ, CB_NK, CB_NV = 18, 21, 24


def _cp(sem, vmem=VMEM_LIMIT):
    return pltpu.CompilerParams(dimension_semantics=sem, vmem_limit_bytes=vmem)


def _silu(x):
    return x * (1.0 / (1.0 + jnp.exp(-x)))


def _dot(a, b):
    return jnp.dot(a, b, preferred_element_type=F32)


def _dot_nt(a, b):
    return lax.dot_general(a, b, (((1,), (1,)), ((), ())), preferred_element_type=F32)


def _dot_tn(a, b):
    return lax.dot_general(a, b, (((0,), (0,)), ((), ())), preferred_element_type=F32)


def _split_bf16(x):
    hi = x.astype(BF16)
    lo = (x - hi.astype(F32)).astype(BF16)
    return hi, lo


def _ada_kernel(c_ref, w_ref, b_ref, o_ref):
    x = _silu(c_ref[...]).astype(BF16)
    o_ref[0] = _dot(x, w_ref[0].astype(BF16)) + b_ref[0]


def _ada(c8, ada_w, ada_b):
    depth, d, n = ada_w.shape
    tn = n // 4
    return pl.pallas_call(
        _ada_kernel,
        grid=(depth, n // tn),
        in_specs=[pl.BlockSpec((8, d), lambda l, j: (0, 0)),
                  pl.BlockSpec((1, d, tn), lambda l, j: (l, 0, j)),
                  pl.BlockSpec((1, 1, tn), lambda l, j: (l, 0, j))],
        out_specs=pl.BlockSpec((1, 8, tn), lambda l, j: (l, 0, j)),
        out_shape=jax.ShapeDtypeStruct((depth, 8, n), F32),
        compiler_params=_cp(("parallel", "parallel")),
        name="ada_mod",
    )(c8, ada_w, ada_b.reshape(depth, 1, n))


def _norm_mod(x, g, sh, sc):
    ms = jnp.mean(x * x, axis=-1, keepdims=True)
    y = x * lax.rsqrt(ms + NORM_EPS) * g
    return y * (1.0 + sc) + sh


def _inproj_kernel(*refs, rope):
    if rope:
        h_ref, g_ref, sh_ref, sc_ref, w_ref, cos_ref, sa_ref, sb_ref, o_ref = refs
    else:
        h_ref, g_ref, sh_ref, sc_ref, w_ref, o_ref = refs
    tm = h_ref.shape[0]
    sub = min(IN_PROJ_SUB, tm)
    cw = IN_PROJ_CHUNK
    for r in range(tm // sub):
        rs = slice(r * sub, (r + 1) * sub)
        xn = _norm_mod(h_ref[rs, :], g_ref[...], sh_ref[0], sc_ref[0]).astype(BF16)
        for c0 in range(0, D_IN_PROJ, cw):
            c1 = min(c0 + cw, D_IN_PROJ)
            acc = _dot(xn, w_ref[:, c0:c1])
            for j in range((c1 - c0) // LANES):
                blk = acc[:, j * LANES:(j + 1) * LANES]
                cb = c0 // LANES + j
                if rope and cb < CB_RV:
                    blk = (blk * cos_ref[rs, :] + pltpu.roll(blk, 16, 1) * sa_ref[rs, :]
                           + pltpu.roll(blk, LANES - 16, 1) * sb_ref[rs, :])
                if CB_RK <= cb < CB_RV:
                    blk = blk * (HEAD_DIM ** -0.5)
                elif CB_NQ <= cb < CB_NK:
                    blk = blk * (HEAD_DIM ** -0.5 * (LOG2E if rope else 1.0))
                o_ref[rs, cb * LANES:(cb + 1) * LANES] = blk.astype(BF16)


def _in_proj(h2, g, sh, sc, w_bf, tables, seq, tm):
    t, d = h2.shape
    tiles_per_seq = seq // tm
    rope = tables is not None
    in_specs = [pl.BlockSpec((tm, d), lambda i: (i, 0)),
                pl.BlockSpec((1, d), lambda i: (0, 0)),
                pl.BlockSpec((1, 1, d), lambda i: (i // tiles_per_seq, 0, 0)),
                pl.BlockSpec((1, 1, d), lambda i: (i // tiles_per_seq, 0, 0)),
                pl.BlockSpec((d, D_IN_PROJ), lambda i: (0, 0))]
    args = [h2, g.reshape(1, d), sh, sc, w_bf]
    if rope:
        in_specs += [pl.BlockSpec((tm, LANES), lambda i: (i % tiles_per_seq, 0))] * 3
        args += list(tables)
    return pl.pallas_call(
        functools.partial(_inproj_kernel, rope=rope),
        grid=(t // tm,),
        in_specs=in_specs,
        out_specs=pl.BlockSpec((tm, D_IN_PROJ), lambda i: (i, 0)),
        out_shape=jax.ShapeDtypeStruct((t, D_IN_PROJ), BF16),
        compiler_params=_cp(("parallel",)),
        name="in_proj_rope" if rope else "in_proj_ctx",
    )(*args)


def _rope_tables(seq):
    t = np.arange(seq)
    row = (t // GRID_W).astype(np.float32)
    col = (t % GRID_W).astype(np.float32)
    n_freq = HEAD_DIM // 4
    inv_freq = (ROPE_BASE ** (-np.arange(n_freq, dtype=np.float32) / n_freq)).astype(np.float32)
    ang_r = row[:, None] * inv_freq
    ang_c = col[:, None] * inv_freq
    cos_h = np.concatenate([np.cos(ang_r), np.cos(ang_r), np.cos(ang_c), np.cos(ang_c)], axis=1)
    sin_h = np.concatenate([np.sin(ang_r), np.sin(ang_r), np.sin(ang_c), np.sin(ang_c)], axis=1)
    lane = np.arange(HEAD_DIM)
    second = (lane % 32) >= 16
    sa = np.where(second[None, :], sin_h, 0.0)
    sb = np.where(second[None, :], 0.0, -sin_h)
    tile2 = lambda a: jnp.asarray(np.concatenate([a, a], axis=1), F32)
    return tile2(cos_h), tile2(sa), tile2(sb)


def _ret_kernel(q_ref, k_ref, v_ref, g_ref, lgt_ref, gn_ref, sf0_ref, sb0_ref,
                y_ref, sfo_ref, sbo_ref, sfs_ref, sbs_ref, *, seq):
    c = RET_CHUNK
    n_chunks = seq // c
    lg = jax.nn.log_sigmoid(lgt_ref[0])
    lgf, lgb = lg[0:1, :], lg[1:2, :]
    pos = lax.broadcasted_iota(jnp.int32, (c, 1), 0).astype(F32)
    dkf = jnp.exp(lgf * (c - 1.0 - pos))
    dkb = jnp.exp(lgb * pos)
    dqf = jnp.exp(lgf * (pos + 1.0))
    dqb = jnp.exp(lgb * (c - pos))
    cdf = jnp.exp(lgf * float(c))
    cdb = jnp.exp(lgb * float(c))
    lane = lax.broadcasted_iota(jnp.int32, (1, LANES), 1)
    first = lane < HEAD_DIM
    ri = lax.broadcasted_iota(jnp.int32, (LANES, LANES), 0)
    ci = lax.broadcasted_iota(jnp.int32, (LANES, LANES), 1)
    same = (ri < HEAD_DIM) == (ci < HEAD_DIM)
    bd = same.astype(F32)

    def head_mean(x):
        a = jnp.sum(jnp.where(first, x, 0.0), axis=-1, keepdims=True)
        b = jnp.sum(jnp.where(first, 0.0, x), axis=-1, keepdims=True)
        return jnp.where(first, a, b) * (1.0 / HEAD_DIM)

    ii = lax.broadcasted_iota(jnp.int32, (c, c), 0)
    jj = lax.broadcasted_iota(jnp.int32, (c, c), 1)
    dif = (ii - jj).astype(F32)

    def decay_mask(h0):
        lf = lgf[:, h0:h0 + 1]
        lb = lgb[:, h0:h0 + 1]
        return jnp.where(dif > 0, jnp.exp(lf * jnp.maximum(dif, 0.0)),
                         jnp.where(dif < 0, jnp.exp(lb * jnp.maximum(-dif, 0.0)), 2.0))

    dm = (decay_mask(0), decay_mask(HEAD_DIM))
    mfirst = first.astype(BF16)
    msecond = (1.0 - first.astype(F32)).astype(BF16)

    unroll = min(8, n_chunks)

    def incr(n, carry):
        sl = pl.ds(pl.multiple_of(n * c, c), c)
        kf = k_ref[0, sl, :].astype(F32)
        v = v_ref[0, sl, :]
        sfs_ref[n] = _dot_tn((kf * dkf).astype(BF16), v) * bd
        sbs_ref[n] = _dot_tn((kf * dkb).astype(BF16), v) * bd
        return carry

    lax.fori_loop(0, n_chunks, incr, 0, unroll=unroll)

    def scan_f(n, s):
        u = sfs_ref[n]
        sfs_ref[n] = s
        return cdf * s + u

    def scan_b(i, s):
        n = n_chunks - 1 - i
        u = sbs_ref[n]
        sbs_ref[n] = s
        return cdb * s + u

    sfo_ref[0, 0] = lax.fori_loop(0, n_chunks, scan_f, sf0_ref[0, 0])
    sbo_ref[0, 0] = lax.fori_loop(0, n_chunks, scan_b, sb0_ref[0, 0])

    def outp(n, carry):
        sl = pl.ds(pl.multiple_of(n * c, c), c)
        q = q_ref[0, sl, :]
        k = k_ref[0, sl, :]
        v = v_ref[0, sl, :]
        qf = q.astype(F32)
        o = None
        for hh, mk in enumerate((mfirst, msecond)):
            s = _dot_nt(q * mk, k) * dm[hh]
            oh = _dot(s.astype(BF16), v)
            o = oh if o is None else jnp.where(first, o, oh)
        qcat = jnp.concatenate([(qf * dqf).astype(BF16), (qf * dqb).astype(BF16)], axis=1)
        scat = jnp.concatenate([sfs_ref[n], sbs_ref[n]], axis=0).astype(BF16)
        o = o + _dot(qcat, scat)
        mu = head_mean(o)
        dlt = o - mu
        var = head_mean(dlt * dlt)
        on = dlt * lax.rsqrt(var + NORM_EPS) * gn_ref[0]
        y_ref[0, sl, :] = (_silu(g_ref[0, sl, :].astype(F32)) * on).astype(BF16)
        return carry

    lax.fori_loop(0, n_chunks, outp, 0, unroll=unroll)


def _retention(proj, lgt, gn, sf0, sb0):
    b, seq, _ = proj.shape
    col = lambda off: pl.BlockSpec((1, seq, LANES), lambda bi, p: (bi, 0, off + p))
    st = pl.BlockSpec((1, 1, LANES, LANES), lambda bi, p: (bi, p, 0, 0))
    return pl.pallas_call(
        functools.partial(_ret_kernel, seq=seq),
        grid=(b, N_PAIRS),
        in_specs=[col(CB_RQ), col(CB_RK), col(CB_RV), col(CB_RG),
                  pl.BlockSpec((1, 2, LANES), lambda bi, p: (p, 0, 0)),
                  pl.BlockSpec((1, 1, LANES), lambda bi, p: (p, 0, 0)),
                  st, st],
        out_specs=[pl.BlockSpec((1, seq, LANES), lambda bi, p: (bi, 0, p)), st, st],
        out_shape=[jax.ShapeDtypeStruct((b, seq, D_RET), BF16),
                   jax.ShapeDtypeStruct((b, N_PAIRS, LANES, LANES), F32),
                   jax.ShapeDtypeStruct((b, N_PAIRS, LANES, LANES), F32)],
        scratch_shapes=[pltpu.VMEM((seq // RET_CHUNK, LANES, LANES), F32),
                        pltpu.VMEM((seq // RET_CHUNK, LANES, LANES), F32)],
        compiler_params=_cp(("parallel", "parallel")),
        name="retention_s%d" % seq,
    )(proj, proj, proj, proj, lgt, gn, sf0, sb0)


def _conv_kernel(b_ref, c_ref, x_ref, w_ref, y_ref, *, seq):
    u = c_ref[0].astype(F32) * x_ref[0].astype(F32)
    row = lax.broadcasted_iota(jnp.int32, (seq, 1), 0)
    prev = jnp.where(row == 0, 0.0, pltpu.roll(u, 1, 0))
    nxt = jnp.where(row == seq - 1, 0.0, pltpu.roll(u, seq - 1, 0))
    y = w_ref[0:1, :] * prev + w_ref[1:2, :] * u + w_ref[2:3, :] * nxt
    y_ref[0] = (b_ref[0].astype(F32) * y).astype(BF16)


def _short_conv(proj, conv_w):
    b, seq, _ = proj.shape
    nblk = D_CONV // LANES
    col = lambda off: pl.BlockSpec((1, seq, LANES), lambda bi, j: (bi, 0, off + j))
    return pl.pallas_call(
        functools.partial(_conv_kernel, seq=seq),
        grid=(b, nblk),
        in_specs=[col(CB_CB), col(CB_CC), col(CB_CX),
                  pl.BlockSpec((3, LANES), lambda bi, j: (0, j))],
        out_specs=pl.BlockSpec((1, seq, LANES), lambda bi, j: (bi, 0, j)),
        out_shape=jax.ShapeDtypeStruct((b, seq, D_CONV), BF16),
        compiler_params=_cp(("parallel", "parallel")),
        name="short_conv_s%d" % seq,
    )(proj, proj, proj, conv_w)


def _na_bias_plan(rows):
    plan = np.full((3, NA_QROWS, NA_KROWS), -1, np.int64)
    for cl, rb in enumerate((0, NA_QROWS, rows - NA_QROWS)):
        ws = int(np.clip(rb - NA_ROWS // 2, 0, rows - NA_KROWS))
        for i in range(NA_QROWS):
            r = rb + i
            r0 = int(np.clip(r - NA_ROWS // 2, 0, rows - NA_ROWS))
            for j in range(NA_KROWS):
                kr = ws + j
                if r0 <= kr < r0 + NA_ROWS:
                    plan[cl, i, j] = kr - r + (NA_ROWS - 1)
    return plan


def _na_bias_kernel(rpb_ref, o_ref, *, plan):
    w = GRID_W
    ndr, ndc = 2 * NA_ROWS - 1, 2 * NA_COLS - 1
    base = pl.program_id(0) * (ndr * ndc)
    c = lax.broadcasted_iota(jnp.int32, (w, w), 0)
    kc = lax.broadcasted_iota(jnp.int32, (w, w), 1)
    dcidx = jnp.clip(kc - c, -(NA_COLS - 1), NA_COLS - 1) + (NA_COLS - 1)
    c0 = jnp.clip(c - NA_COLS // 2, 0, w - NA_COLS)
    col_in = jnp.logical_and(kc >= c0, kc < c0 + NA_COLS)
    neg = jnp.full((w, w), NEG_BIG, F32)
    tiles = []
    for dr in range(ndr):
        t = neg
        for dcv in range(ndc):
            t = jnp.where(dcidx == dcv, rpb_ref[base + dr * ndc + dcv], t)
        tiles.append(jnp.where(col_in, t * LOG2E, NEG_BIG))
    for cl in range(plan.shape[0]):
        for i in range(plan.shape[1]):
            for j in range(0, plan.shape[2], 2):
                pair = [tiles[int(d)] if d >= 0 else neg for d in plan[cl, i, j:j + 2]]
                o_ref[0, cl, i * w:(i + 1) * w, j * w:(j + 2) * w] = jnp.concatenate(pair, axis=1)


def _na_bias(rpb, rows):
    h = rpb.shape[0]
    tq, tk = NA_QROWS * GRID_W, NA_KROWS * GRID_W
    return pl.pallas_call(
        functools.partial(_na_bias_kernel, plan=_na_bias_plan(rows)),
        grid=(h,),
        in_specs=[pl.BlockSpec(memory_space=pltpu.SMEM)],
        out_specs=pl.BlockSpec((1, 3, tq, tk), lambda hh: (hh, 0, 0, 0)),
        out_shape=jax.ShapeDtypeStruct((h, 3, tq, tk), F32),
        compiler_params=_cp(("parallel",)),
        name="na_bias_expand",
    )(rpb.reshape(-1))


def _na_kernel(q_ref, k_ref, v_ref, kc_ref, vc_ref, bias_ref, o_ref, v1_ref, vc1_ref, *, rows):
    qb = pl.program_id(1)
    lane = lax.broadcasted_iota(jnp.int32, (1, LANES), 1)
    first = lane < HEAD_DIM

    @pl.when(qb == 0)
    def _():
        one = jnp.ones((1, LANES), BF16)
        for p in range(N_PAIRS):
            cols = slice(p * LANES, (p + 1) * LANES)
            v1_ref[2 * p] = jnp.where(first, v_ref[0, :, cols], one)
            v1_ref[2 * p + 1] = jnp.where(first, one, v_ref[0, :, cols])
            vc1_ref[2 * p] = jnp.where(first, vc_ref[0, :, cols], one)
            vc1_ref[2 * p + 1] = jnp.where(first, one, vc_ref[0, :, cols])

    ws = jnp.clip(qb * NA_QROWS - NA_ROWS // 2, 0, rows - NA_KROWS)
    sl = pl.ds(pl.multiple_of(ws * GRID_W, GRID_W), NA_KROWS * GRID_W)
    for p in range(N_PAIRS):
        cols = slice(p * LANES, (p + 1) * LANES)
        q = q_ref[0, :, cols]
        kw = k_ref[0, sl, cols]
        kc = kc_ref[0, :, cols]
        pv = []
        for hh in range(2):
            mk = (first if hh == 0 else jnp.logical_not(first)).astype(BF16)
            qh = q * mk
            s_loc = _dot_nt(qh, kw) + bias_ref[2 * p + hh, 0]
            s_ctx = _dot_nt(qh, kc)
            m = jnp.maximum(jnp.max(s_loc, axis=-1, keepdims=True), jnp.max(s_ctx, axis=-1, keepdims=True))
            p_loc = jnp.exp2(s_loc - m).astype(BF16)
            p_ctx = jnp.exp2(s_ctx - m).astype(BF16)
            pv.append(_dot(p_loc, v1_ref[2 * p + hh, sl, :]) + _dot(p_ctx, vc1_ref[2 * p + hh]))
        num = jnp.where(first, pv[0], pv[1])
        den = jnp.where(first, pltpu.roll(pv[0], HEAD_DIM, 1), pltpu.roll(pv[1], HEAD_DIM, 1))
        o_ref[0, :, cols] = (num / den).astype(BF16)


def _na(proj, proj_ctx, bias):
    b, seq, _ = proj.shape
    ctx_len = proj_ctx.shape[1]
    rows = seq // GRID_W
    nqb = rows // NA_QROWS
    tq = NA_QROWS * GRID_W
    tk = NA_KROWS * GRID_W
    nh = 2 * N_PAIRS
    cq, ck, cv = (CB_NQ * LANES // D_NA, CB_NK * LANES // D_NA, CB_NV * LANES // D_NA)

    def cls(qb):
        return jnp.where(qb == 0, 0, jnp.where(qb == nqb - 1, 2, 1))

    return pl.pallas_call(
        functools.partial(_na_kernel, rows=rows),
        grid=(b, nqb),
        in_specs=[pl.BlockSpec((1, tq, D_NA), lambda bi, qb: (bi, qb, cq)),
                  pl.BlockSpec((1, seq, D_NA), lambda bi, qb: (bi, 0, ck)),
                  pl.BlockSpec((1, seq, D_NA), lambda bi, qb: (bi, 0, cv)),
                  pl.BlockSpec((1, ctx_len, D_NA), lambda bi, qb: (bi, 0, ck)),
                  pl.BlockSpec((1, ctx_len, D_NA), lambda bi, qb: (bi, 0, cv)),
                  pl.BlockSpec((nh, 1, tq, tk), lambda bi, qb: (0, cls(qb), 0, 0))],
        out_specs=pl.BlockSpec((1, tq, D_NA), lambda bi, qb: (bi, qb, 0)),
        out_shape=jax.ShapeDtypeStruct((b, seq, D_NA), BF16),
        scratch_shapes=[pltpu.VMEM((nh, seq, LANES), BF16), pltpu.VMEM((nh, ctx_len, LANES), BF16)],
        compiler_params=_cp(("parallel", "arbitrary")),
        name="na_attn",
    )(proj, proj, proj, proj_ctx, proj_ctx, bias)


def _ctx_attn_kernel(q_ref, k_ref, v_ref, o_ref):
    q = q_ref[0]
    k = k_ref[0]
    v = v_ref[0]
    lane = lax.broadcasted_iota(jnp.int32, (1, LANES), 1)
    first = lane < HEAD_DIM
    o = None
    for hh in range(2):
        mk = (first if hh == 0 else jnp.logical_not(first)).astype(BF16)
        s = _dot_nt(q * mk, k)
        m = jnp.max(s, axis=-1, keepdims=True)
        p = jnp.exp(s - m)
        l = jnp.sum(p, axis=-1, keepdims=True)
        oh = _dot(p.astype(BF16), v) / l
        o = oh if o is None else jnp.where(first, o, oh)
    o_ref[0] = o.astype(BF16)


def _ctx_attn(proj_ctx):
    b, ctx_len, _ = proj_ctx.shape
    col = lambda off: pl.BlockSpec((1, ctx_len, LANES), lambda bi, p: (bi, 0, off + p))
    return pl.pallas_call(
        _ctx_attn_kernel,
        grid=(b, N_PAIRS),
        in_specs=[col(CB_NQ), col(CB_NK), col(CB_NV)],
        out_specs=pl.BlockSpec((1, ctx_len, LANES), lambda bi, p: (bi, 0, p)),
        out_shape=jax.ShapeDtypeStruct((b, ctx_len, D_NA), BF16),
        compiler_params=_cp(("parallel", "parallel")),
        name="ctx_attn",
    )(proj_ctx, proj_ctx, proj_ctx)


def _outproj_kernel(*refs, route):
    if route:
        (yr_ref, yc_ref, yn_ref, h_ref, w_ref, g1_ref, n2_ref, sh_ref, sc_ref, rw_ref, rb_ref,
         ho_ref, xn_ref, idx_ref, wt_ref) = refs
    else:
        yr_ref, yc_ref, yn_ref, h_ref, w_ref, g1_ref, n2_ref, sh_ref, sc_ref, ho_ref, xn_ref = refs
    tm = h_ref.shape[0]
    sub = min(OUT_PROJ_SUB, tm)
    if route:
        r_hi, r_lo = _split_bf16(rw_ref[...])
        r_both = jnp.concatenate([r_hi, r_lo], axis=0)
    for r in range(tm // sub):
        rs = slice(r * sub, (r + 1) * sub)
        ycat = jnp.concatenate([yr_ref[rs, :], yc_ref[rs, :], yn_ref[rs, :]], axis=-1)
        h = h_ref[rs, :] + g1_ref[0] * _dot(ycat, w_ref[...])
        ho_ref[rs, :] = h
        a = _norm_mod(h, n2_ref[...], sh_ref[0], sc_ref[0])
        xn_ref[rs, :] = a.astype(xn_ref.dtype)
        if route:
            a_hi, a_lo = _split_bf16(a)
            t_hi = _dot_nt(r_both, a_hi)
            logits = t_hi[:N_EXPERTS] + t_hi[N_EXPERTS:] + _dot_nt(r_hi, a_lo) + rb_ref[...]
            eidx = lax.broadcasted_iota(jnp.int32, logits.shape, 0)
            m1 = jnp.max(logits, axis=0, keepdims=True)
            i1 = jnp.min(jnp.where(logits == m1, eidx, N_EXPERTS), axis=0, keepdims=True)
            rest = jnp.where(eidx == i1, -jnp.inf, logits)
            m2 = jnp.max(rest, axis=0, keepdims=True)
            i2 = jnp.min(jnp.where(rest == m2, eidx, N_EXPERTS), axis=0, keepdims=True)
            e2 = jnp.exp(m2 - m1)
            den = 1.0 + e2
            idx_ref[:, rs] = jnp.concatenate([i1, i2], axis=0)
            wt_ref[:, rs] = jnp.concatenate([1.0 / den, e2 / den], axis=0)


def _out_proj(yr, yc, yn, h2, w_bf, g1, n2g, sh2, sc2, seq, tm, router=None):
    t, d = h2.shape
    tps = seq // tm
    route = router is not None
    row = lambda wdt: pl.BlockSpec((tm, wdt), lambda i: (i, 0))
    mod = pl.BlockSpec((1, 1, d), lambda i: (i // tps, 0, 0))
    in_specs = [row(D_RET), row(D_CONV), row(D_NA), row(d),
                pl.BlockSpec((d, d), lambda i: (0, 0)), mod,
                pl.BlockSpec((1, d), lambda i: (0, 0)), mod, mod]
    args = [yr, yc, yn, h2, w_bf, g1, n2g.reshape(1, d), sh2, sc2]
    out_specs = [row(d), row(d)]
    out_shape = [jax.ShapeDtypeStruct((t, d), F32), jax.ShapeDtypeStruct((t, d), F32 if route else BF16)]
    if route:
        rw, rb = router
        in_specs += [pl.BlockSpec((N_EXPERTS, d), lambda i: (0, 0)),
                     pl.BlockSpec((N_EXPERTS, 1), lambda i: (0, 0))]
        args += [rw.T, rb.reshape(N_EXPERTS, 1)]
        out_specs += [pl.BlockSpec((2, tm), lambda i: (0, i))] * 2
        out_shape += [jax.ShapeDtypeStruct((2, t), jnp.int32), jax.ShapeDtypeStruct((2, t), F32)]
    return pl.pallas_call(
        functools.partial(_outproj_kernel, route=route),
        grid=(t // tm,),
        in_specs=in_specs,
        out_specs=out_specs,
        out_shape=out_shape,
        compiler_params=_cp(("parallel",)),
        name="out_proj_route" if route else "out_proj_t%d" % t,
    )(*args)


def _swiglu_chunks(x, w_in_ref, w_out_ref, ff, n_chunks):
    tf = -(-ff // (n_chunks * MXU_COLS)) * MXU_COLS
    y = None
    for c0 in range(0, ff, tf):
        c1 = min(c0 + tf, ff)
        gate = _dot(x, w_in_ref[:, c0:c1])
        up = _dot(x, w_in_ref[:, ff + c0:ff + c1])
        part = _dot((_silu(gate) * up).astype(BF16), w_out_ref[c0:c1, :])
        y = part if y is None else y + part
    return y


def _ffn_kernel(x_ref, h_ref, wi_ref, wo_ref, g2_ref, o_ref, *, ff, n_chunks):
    y = _swiglu_chunks(x_ref[...], wi_ref, wo_ref, ff, n_chunks)
    o_ref[...] = h_ref[...] + g2_ref[0] * y


def _ffn(xn, h2, w_in_bf, w_out_bf, g2, seq, tm, n_chunks):
    t, d = h2.shape
    ff = w_out_bf.shape[0]
    tps = seq // tm
    resident = pl.Buffered(1)
    return pl.pallas_call(
        functools.partial(_ffn_kernel, ff=ff, n_chunks=n_chunks),
        grid=(t // tm,),
        in_specs=[pl.BlockSpec((tm, d), lambda i: (i, 0)),
                  pl.BlockSpec((tm, d), lambda i: (i, 0)),
                  pl.BlockSpec((d, 2 * ff), lambda i: (0, 0), pipeline_mode=resident),
                  pl.BlockSpec((ff, d), lambda i: (0, 0), pipeline_mode=resident),
                  pl.BlockSpec((1, 1, d), lambda i: (i // tps, 0, 0))],
        out_specs=pl.BlockSpec((tm, d), lambda i: (i, 0)),
        out_shape=jax.ShapeDtypeStruct((t, d), F32),
        compiler_params=_cp(("parallel",)),
        name="ffn_t%d" % t,
    )(xn, h2, w_in_bf, w_out_bf, g2)


def _dispatch_kernel(s0_ref, s1_ref, zs_ref, zv_ref, x_ref, wi_ref, wo_ref, xs_ref, wib_ref, wob_ref,
                     zbuf_ref, sem, *, rows, n_cast):
    @pl.when(pl.program_id(0) == 0)
    def _():
        zbuf_ref[...] = jnp.zeros_like(zbuf_ref)
        for e in range(2 * N_EXPERTS):
            @pl.when(zv_ref[e] != 0)
            def _():
                dst = xs_ref.at[pl.ds(pl.multiple_of(zs_ref[e], MOE_TS), MOE_TS)]
                pltpu.make_async_copy(zbuf_ref, dst, sem.at[1]).start()
        for e in range(2 * N_EXPERTS):
            @pl.when(zv_ref[e] != 0)
            def _():
                pltpu.make_async_copy(zbuf_ref, xs_ref.at[pl.ds(0, MOE_TS)], sem.at[1]).wait()

    wo_rows = wo_ref.shape[0] // n_cast

    def issue(g):
        for u in range(DMA_UNROLL):
            r = g * DMA_UNROLL + u
            src = x_ref.at[pl.ds(r, 1)]
            pltpu.make_async_copy(src, xs_ref.at[pl.ds(s0_ref[r], 1)], sem.at[0]).start(priority=u % 2)
            pltpu.make_async_copy(src, xs_ref.at[pl.ds(s1_ref[r], 1)], sem.at[0]).start(priority=(u + 1) % 2)

    def issue_and_cast(g, carry):
        cs = pl.ds(pl.multiple_of(g * LANES, LANES), LANES)
        wib_ref[:, cs] = wi_ref[:, cs].astype(BF16)
        rs = pl.ds(pl.multiple_of(g * wo_rows, wo_rows), wo_rows)
        wob_ref[rs, :] = wo_ref[rs, :].astype(BF16)
        issue(g)
        return carry

    def issue_only(g, carry):
        issue(g)
        return carry

    n_iter = rows // DMA_UNROLL
    lax.fori_loop(0, n_cast, issue_and_cast, 0)
    lax.fori_loop(n_cast, n_iter, issue_only, 0)
    for _ in range(2):
        pltpu.make_async_copy(x_ref, xs_ref.at[pl.ds(0, rows)], sem.at[0]).wait()


def _dispatch(xn, slot0, slot1, zplan, n_slots, rows, w_in, w_out):
    t, d = xn.shape
    zstart, zvalid = zplan
    steps = t // rows
    n_exp, _, ff2 = w_in.shape
    ff = w_out.shape[1]
    wi2 = w_in.reshape(n_exp * d, ff2)
    wo2 = w_out.reshape(n_exp * ff, d)
    n_cast = ff2 // LANES
    wi_rows, wo_rows = n_exp * d // steps, n_exp * ff // steps
    assert n_cast <= rows // DMA_UNROLL and wo_rows % (n_cast * 16) == 0 and wi_rows % 16 == 0
    wib, wob = (jax.ShapeDtypeStruct(wi2.shape, BF16), jax.ShapeDtypeStruct(wo2.shape, BF16))
    xs, wib, wob = pl.pallas_call(
        functools.partial(_dispatch_kernel, rows=rows, n_cast=n_cast),
        grid=(steps,),
        in_specs=[pl.BlockSpec((rows,), lambda i: (i,), memory_space=pltpu.SMEM),
                  pl.BlockSpec((rows,), lambda i: (i,), memory_space=pltpu.SMEM),
                  pl.BlockSpec((2 * N_EXPERTS,), lambda i: (0,), memory_space=pltpu.SMEM),
                  pl.BlockSpec((2 * N_EXPERTS,), lambda i: (0,), memory_space=pltpu.SMEM),
                  pl.BlockSpec((rows, d), lambda i: (i, 0)),
                  pl.BlockSpec((wi_rows, ff2), lambda i: (i, 0)),
                  pl.BlockSpec((wo_rows, d), lambda i: (i, 0))],
        out_specs=[pl.BlockSpec(memory_space=pl.ANY),
                   pl.BlockSpec((wi_rows, ff2), lambda i: (i, 0)),
                   pl.BlockSpec((wo_rows, d), lambda i: (i, 0))],
        out_shape=[jax.ShapeDtypeStruct((n_slots, d), xn.dtype), wib, wob],
        scratch_shapes=[pltpu.VMEM((MOE_TS, d), xn.dtype), pltpu.SemaphoreType.DMA((2,))],
        compiler_params=_cp(("arbitrary",)),
        name="moe_dispatch",
    )(slot0, slot1, zstart, zvalid, xn, wi2, wo2)
    return xs, wib.reshape(n_exp, d, ff2), wob.reshape(n_exp, ff, d)


def _expert_kernel(be_ref, nu_ref, x_ref, wi_ref, wo_ref, o_ref, *, ff, n_chunks):
    used = pl.program_id(0) < nu_ref[0]

    @pl.when(used)
    def _():
        o_ref[...] = _swiglu_chunks(x_ref[...].astype(BF16), wi_ref.at[0], wo_ref.at[0], ff, n_chunks)

    @pl.when(jnp.logical_not(used))
    def _():
        o_ref[...] = jnp.zeros_like(o_ref)


def _experts(xs, block_expert, n_used, w_in_bf, w_out_bf, n_chunks):
    n, d = xs.shape
    ff = w_out_bf.shape[1]
    ts = MOE_TS
    grid_spec = pltpu.PrefetchScalarGridSpec(
        num_scalar_prefetch=2,
        grid=(n // ts,),
        in_specs=[pl.BlockSpec((ts, d), lambda j, be, nu: (j, 0)),
                  pl.BlockSpec((1, d, 2 * ff), lambda j, be, nu: (be[j], 0, 0)),
                  pl.BlockSpec((1, ff, d), lambda j, be, nu: (be[j], 0, 0))],
        out_specs=pl.BlockSpec((ts, d), lambda j, be, nu: (j, 0)),
    )
    return pl.pallas_call(
        functools.partial(_expert_kernel, ff=ff, n_chunks=n_chunks),
        grid_spec=grid_spec,
        out_shape=jax.ShapeDtypeStruct((n, d), F32),
        compiler_params=_cp(("arbitrary",)),
        name="moe_experts",
    )(block_expert, n_used, xs, w_in_bf, w_out_bf)


def _combine_kernel(p0_ref, p1_ref, ys_ref, h_ref, w0_ref, w1_ref, g2_ref, fg_ref, o_ref,
                    y0_ref, y1_ref, sem, *, rows):
    def issue(g, carry):
        for u in range(DMA_UNROLL):
            r = g * DMA_UNROLL + u
            pltpu.make_async_copy(ys_ref.at[pl.ds(p0_ref[r], 1)], y0_ref.at[pl.ds(r, 1)],
                                  sem.at[0]).start(priority=u % 2)
            pltpu.make_async_copy(ys_ref.at[pl.ds(p1_ref[r], 1)], y1_ref.at[pl.ds(r, 1)],
                                  sem.at[1]).start(priority=(u + 1) % 2)
        return carry

    lax.fori_loop(0, rows // DMA_UNROLL, issue, 0)
    pltpu.make_async_copy(ys_ref.at[pl.ds(0, rows)], y0_ref, sem.at[0]).wait()
    pltpu.make_async_copy(ys_ref.at[pl.ds(0, rows)], y1_ref, sem.at[1]).wait()
    h = h_ref[...] + g2_ref[0] * (w0_ref[...] * y0_ref[...] + w1_ref[...] * y1_ref[...])
    ms = jnp.mean(h * h, axis=-1, keepdims=True)
    o_ref[...] = h * lax.rsqrt(ms + NORM_EPS) * fg_ref[...]


def _combine(pos0, pos1, ys, h2, w0, w1, g2, final_g, seq, rows):
    t, d = h2.shape
    tps = seq // rows
    blk = pl.BlockSpec((rows, d), lambda i: (i, 0))
    col = pl.BlockSpec((rows, 1), lambda i: (i, 0))
    return pl.pallas_call(
        functools.partial(_combine_kernel, rows=rows),
        grid=(t // rows,),
        in_specs=[pl.BlockSpec((rows,), lambda i: (i,), memory_space=pltpu.SMEM),
                  pl.BlockSpec((rows,), lambda i: (i,), memory_space=pltpu.SMEM),
                  pl.BlockSpec(memory_space=pl.ANY),
                  blk, col, col,
                  pl.BlockSpec((1, 1, d), lambda i: (i // tps, 0, 0)),
                  pl.BlockSpec((1, d), lambda i: (0, 0))],
        out_specs=blk,
        out_shape=jax.ShapeDtypeStruct((t, d), F32),
        scratch_shapes=[pltpu.VMEM((rows, d), F32), pltpu.VMEM((rows, d), F32),
                        pltpu.SemaphoreType.DMA((2,))],
        compiler_params=_cp(("arbitrary",)),
        name="moe_combine_norm",
    )(pos0, pos1, ys, h2, w0, w1, g2, final_g.reshape(1, d))


def _route_plan(idx):
    t = idx.shape[1]
    ts = MOE_TS
    n_slots = 2 * t + N_EXPERTS * ts
    e_flat = idx.reshape(-1)
    onehot = (e_flat[:, None] == jnp.arange(N_EXPERTS, dtype=jnp.int32)[None, :]).astype(jnp.int32)
    csum = jnp.cumsum(onehot, axis=0)
    rank = jnp.sum((csum - onehot) * onehot, axis=1)
    counts = csum[-1]
    padded = ((counts + ts - 1) // ts) * ts
    ends = jnp.cumsum(padded)
    offs = ends - padded
    slot = (jnp.sum(onehot * offs[None, :], axis=1) + rank).astype(jnp.int32)
    starts = jnp.arange(n_slots // ts, dtype=jnp.int32) * ts
    block_expert = jnp.minimum(jnp.sum((starts[:, None] >= ends[None, :]).astype(jnp.int32), axis=1),
                               N_EXPERTS - 1).astype(jnp.int32)
    n_used = (ends[-1] // ts).astype(jnp.int32).reshape(1)
    tail = ends[-1] + jnp.arange(N_EXPERTS, dtype=jnp.int32) * ts
    zstart = jnp.concatenate([ends - ts, tail])
    zvalid = jnp.concatenate([padded > 0, tail < n_slots]).astype(jnp.int32)
    zstart = jnp.where(zvalid != 0, zstart, 0).astype(jnp.int32)
    return slot[:t], slot[t:], block_expert, n_used, (zstart, zvalid), n_slots


def kernel(x, c, ctx, c_ctx, ada_w, ada_b, norm1_g, norm2_g, w_in, w_out, ret_decay_logit, ret_gn_g,
           conv_w, na_rpb, ffn_w_in, ffn_w_out, moe_router_w, moe_router_b, moe_w_in, moe_w_out, final_g):
    b, seq, d = x.shape
    ctx_len = ctx.shape[1]
    depth = ada_w.shape[0]
    assert depth == 2, "the final norm is fused into the last (MoE) layer's combine step"
    rows = seq // GRID_W
    t_lat, t_ctx = b * seq, b * ctx_len
    tm_lat = min(1024, seq)
    tm_ctx = ctx_len

    c8 = jnp.zeros((8, d), F32).at[:b].set(c).at[b].set(c_ctx)
    mods = _ada(c8, ada_w, ada_b)
    tables = _rope_tables(seq)

    h = x.reshape(t_lat, d)
    hc = ctx.reshape(t_ctx, d)
    zero_state = jnp.zeros((b, N_PAIRS, LANES, LANES), F32)
    out = None
    for layer in range(depth):
        update_ctx = layer < depth - 1
        m = mods[layer].reshape(8, 6, d)
        lat = [m[:b, i].reshape(b, 1, d) for i in range(6)]
        cx = [jnp.broadcast_to(m[b, i].reshape(1, 1, d), (b, 1, d)) for i in range(6)]
        w_in_bf = w_in[layer].astype(BF16)
        w_out_bf = w_out[layer].astype(BF16)
        lgt = ret_decay_logit[layer].reshape(2, N_PAIRS, LANES // HEAD_DIM)
        lgt = jnp.repeat(lgt, HEAD_DIM, axis=2).transpose(1, 0, 2)
        gn = ret_gn_g[layer].reshape(N_PAIRS, 1, LANES)

        proj_c = _in_proj(hc, norm1_g[layer], cx[0], cx[1], w_in_bf, None, ctx_len, tm_ctx)
        proj_c = proj_c.reshape(b, ctx_len, D_IN_PROJ)
        proj = _in_proj(h, norm1_g[layer], lat[0], lat[1], w_in_bf, tables, seq, tm_lat)
        proj = proj.reshape(b, seq, D_IN_PROJ)

        y_ret_c, s_fwd, s_bwd = _retention(proj_c, lgt, gn, zero_state, zero_state)
        y_ret, _, _ = _retention(proj, lgt, gn, s_fwd, s_bwd)
        y_conv = _short_conv(proj, conv_w[layer])
        y_na = _na(proj, proj_c, _na_bias(na_rpb[layer], rows))

        if layer % 2 == 0:
            h, xn = _out_proj(y_ret.reshape(t_lat, D_RET), y_conv.reshape(t_lat, D_CONV),
                              y_na.reshape(t_lat, D_NA), h, w_out_bf, lat[2], norm2_g[layer],
                              lat[3], lat[4], seq, tm_lat)
            fw_in = ffn_w_in[layer // 2].astype(BF16)
            fw_out = ffn_w_out[layer // 2].astype(BF16)
            h = _ffn(xn, h, fw_in, fw_out, lat[5], seq, min(512, seq), FFN_CHUNKS)
            if update_ctx:
                y_conv_c = _short_conv(proj_c, conv_w[layer])
                y_na_c = _ctx_attn(proj_c)
                hc, xnc = _out_proj(y_ret_c.reshape(t_ctx, D_RET), y_conv_c.reshape(t_ctx, D_CONV),
                                    y_na_c.reshape(t_ctx, D_NA), hc, w_out_bf, cx[2], norm2_g[layer],
                                    cx[3], cx[4], ctx_len, tm_ctx)
                hc = _ffn(xnc, hc, fw_in, fw_out, cx[5], ctx_len, tm_ctx, FFN_CHUNKS)
        else:
            e = layer // 2
            h, xn, idx, wts = _out_proj(y_ret.reshape(t_lat, D_RET), y_conv.reshape(t_lat, D_CONV),
                                        y_na.reshape(t_lat, D_NA), h, w_out_bf, lat[2], norm2_g[layer],
                                        lat[3], lat[4], seq, tm_lat,
                                        router=(moe_router_w[e], moe_router_b[e]))
            pos0, pos1, block_expert, n_used, zstart, n_slots = _route_plan(idx)
            xs, mw_in_bf, mw_out_bf = _dispatch(xn, pos0, pos1, zstart, n_slots, min(512, seq),
                                                moe_w_in[e], moe_w_out[e])
            ys = _experts(xs, block_expert, n_used, mw_in_bf, mw_out_bf, FFN_CHUNKS)
            out = _combine(pos0, pos1, ys, h, wts[0].reshape(t_lat, 1), wts[1].reshape(t_lat, 1),
                           lat[5], final_g, seq, min(512, seq))
            h = out
    return out.reshape(b, seq, d)
```

```python
import functools

import numpy as np
import jax
import jax.numpy as jnp
from jax import lax
from jax.experimental import pallas as pl
from jax.experimental.pallas import tpu as pltpu

F32 = jnp.float32
BF16 = jnp.bfloat16

LANES = 128
HEAD_DIM = 64
GRID_W = 64
N_CONV_GROUPS = 4
N_RET_HEADS = 6
N_NA_HEADS = 6
D_RET = N_RET_HEADS * HEAD_DIM
D_CONV = N_CONV_GROUPS * HEAD_DIM
D_NA = N_NA_HEADS * HEAD_DIM
D_IN_PROJ = 4 * D_RET + 3 * D_CONV + 3 * D_NA
N_PAIRS = D_RET // LANES
NA_ROWS = 8
NA_COLS = 16
N_EXPERTS = 8
ROPE_BASE = 10000.0
NORM_EPS = 1e-6
NEG_BIG = -1e30
LOG2E = 1.4426950408889634

RET_CHUNK = 256
NA_QROWS = 4
NA_KROWS = 12
MOE_TS = 512
DMA_UNROLL = 8
FFN_CHUNKS = 2
IN_PROJ_SUB = 512
OUT_PROJ_SUB = 512
MXU_COLS = 256
IN_PROJ_CHUNK = 3 * MXU_COLS
VMEM_LIMIT = 56 * 1024 * 1024

CB_RQ, CB_RK, CB_RV, CB_RG = 0, 3, 6, 9
CB_CB, CB_CC, CB_CX = 12, 14, 16
CB_NQ, CB_NK, CB_NV = 18, 21, 24


def _cp(sem, vmem=VMEM_LIMIT):
    return pltpu.CompilerParams(dimension_semantics=sem, vmem_limit_bytes=vmem)


def _silu(x):
    return x * (1.0 / (1.0 + jnp.exp(-x)))


def _dot(a, b):
    return jnp.dot(a, b, preferred_element_type=F32)


def _dot_nt(a, b):
    return lax.dot_general(a, b, (((1,), (1,)), ((), ())), preferred_element_type=F32)


def _dot_tn(a, b):
    return lax.dot_general(a, b, (((0,), (0,)), ((), ())), preferred_element_type=F32)


def _split_bf16(x):
    hi = x.astype(BF16)
    lo = (x - hi.astype(F32)).astype(BF16)
    return hi, lo


def _ada_kernel(c_ref, w_ref, b_ref, o_ref):
    x = _silu(c_ref[...]).astype(BF16)
    o_ref[0] = _dot(x, w_ref[0].astype(BF16)) + b_ref[0]


def _ada(c8, ada_w, ada_b):
    depth, d, n = ada_w.shape
    tn = n // 4
    return pl.pallas_call(
        _ada_kernel,
        grid=(depth, n // tn),
        in_specs=[pl.BlockSpec((8, d), lambda l, j: (0, 0)),
                  pl.BlockSpec((1, d, tn), lambda l, j: (l, 0, j)),
                  pl.BlockSpec((1, 1, tn), lambda l, j: (l, 0, j))],
        out_specs=pl.BlockSpec((1, 8, tn), lambda l, j: (l, 0, j)),
        out_shape=jax.ShapeDtypeStruct((depth, 8, n), F32),
        compiler_params=_cp(("parallel", "parallel")),
        name="ada_mod",
    )(c8, ada_w, ada_b.reshape(depth, 1, n))


def _norm_mod(x, g, sh, sc):
    ms = jnp.mean(x * x, axis=-1, keepdims=True)
    y = x * lax.rsqrt(ms + NORM_EPS) * g
    return y * (1.0 + sc) + sh


def _inproj_kernel(*refs, rope):
    if rope:
        h_ref, g_ref, sh_ref, sc_ref, w_ref, cos_ref, sa_ref, sb_ref, o_ref = refs
    else:
        h_ref, g_ref, sh_ref, sc_ref, w_ref, o_ref = refs
    tm = h_ref.shape[0]
    sub = min(IN_PROJ_SUB, tm)
    cw = IN_PROJ_CHUNK
    for r in range(tm // sub):
        rs = slice(r * sub, (r + 1) * sub)
        xn = _norm_mod(h_ref[rs, :], g_ref[...], sh_ref[0], sc_ref[0]).astype(BF16)
        for c0 in range(0, D_IN_PROJ, cw):
            c1 = min(c0 + cw, D_IN_PROJ)
            acc = _dot(xn, w_ref[:, c0:c1])
            for j in range((c1 - c0) // LANES):
                blk = acc[:, j * LANES:(j + 1) * LANES]
                cb = c0 // LANES + j
                if rope and cb < CB_RV:
                    blk = (blk * cos_ref[rs, :] + pltpu.roll(blk, 16, 1) * sa_ref[rs, :]
                           + pltpu.roll(blk, LANES - 16, 1) * sb_ref[rs, :])
                if CB_RK <= cb < CB_RV:
                    blk = blk * (HEAD_DIM ** -0.5)
                elif CB_NQ <= cb < CB_NK:
                    blk = blk * (HEAD_DIM ** -0.5 * (LOG2E if rope else 1.0))
                o_ref[rs, cb * LANES:(cb + 1) * LANES] = blk.astype(BF16)


def _in_proj(h2, g, sh, sc, w_bf, tables, seq, tm):
    t, d = h2.shape
    tiles_per_seq = seq // tm
    rope = tables is not None
    in_specs = [pl.BlockSpec((tm, d), lambda i: (i, 0)),
                pl.BlockSpec((1, d), lambda i: (0, 0)),
                pl.BlockSpec((1, 1, d), lambda i: (i // tiles_per_seq, 0, 0)),
                pl.BlockSpec((1, 1, d), lambda i: (i // tiles_per_seq, 0, 0)),
                pl.BlockSpec((d, D_IN_PROJ), lambda i: (0, 0))]
    args = [h2, g.reshape(1, d), sh, sc, w_bf]
    if rope:
        in_specs += [pl.BlockSpec((tm, LANES), lambda i: (i % tiles_per_seq, 0))] * 3
        args += list(tables)
    return pl.pallas_call(
        functools.partial(_inproj_kernel, rope=rope),
        grid=(t // tm,),
        in_specs=in_specs,
        out_specs=pl.BlockSpec((tm, D_IN_PROJ), lambda i: (i, 0)),
        out_shape=jax.ShapeDtypeStruct((t, D_IN_PROJ), BF16),
        compiler_params=_cp(("parallel",)),
        name="in_proj_rope" if rope else "in_proj_ctx",
    )(*args)


def _rope_tables(seq):
    t = np.arange(seq)
    row = (t // GRID_W).astype(np.float32)
    col = (t % GRID_W).astype(np.float32)
    n_freq = HEAD_DIM // 4
    inv_freq = (ROPE_BASE ** (-np.arange(n_freq, dtype=np.float32) / n_freq)).astype(np.float32)
    ang_r = row[:, None] * inv_freq
    ang_c = col[:, None] * inv_freq
    cos_h = np.concatenate([np.cos(ang_r), np.cos(ang_r), np.cos(ang_c), np.cos(ang_c)], axis=1)
    sin_h = np.concatenate([np.sin(ang_r), np.sin(ang_r), np.sin(ang_c), np.sin(ang_c)], axis=1)
    lane = np.arange(HEAD_DIM)
    second = (lane % 32) >= 16
    sa = np.where(second[None, :], sin_h, 0.0)
    sb = np.where(second[None, :], 0.0, -sin_h)
    tile2 = lambda a: jnp.asarray(np.concatenate([a, a], axis=1), F32)
    return tile2(cos_h), tile2(sa), tile2(sb)


def _ret_kernel(q_ref, k_ref, v_ref, g_ref, lgt_ref, gn_ref, sf0_ref, sb0_ref,
                y_ref, sfo_ref, sbo_ref, sfs_ref, sbs_ref, *, seq):
    c = RET_CHUNK
    n_chunks = seq // c
    lg = jax.nn.log_sigmoid(lgt_ref[0])
    lgf, lgb = lg[0:1, :], lg[1:2, :]
    pos = lax.broadcasted_iota(jnp.int32, (c, 1), 0).astype(F32)
    dkf = jnp.exp(lgf * (c - 1.0 - pos))
    dkb = jnp.exp(lgb * pos)
    dqf = jnp.exp(lgf * (pos + 1.0))
    dqb = jnp.exp(lgb * (c - pos))
    cdf = jnp.exp(lgf * float(c))
    cdb = jnp.exp(lgb * float(c))
    lane = lax.broadcasted_iota(jnp.int32, (1, LANES), 1)
    first = lane < HEAD_DIM
    ri = lax.broadcasted_iota(jnp.int32, (LANES, LANES), 0)
    ci = lax.broadcasted_iota(jnp.int32, (LANES, LANES), 1)
    same = (ri < HEAD_DIM) == (ci < HEAD_DIM)
    bd = same.astype(F32)

    def head_mean(x):
        a = jnp.sum(jnp.where(first, x, 0.0), axis=-1, keepdims=True)
        b = jnp.sum(jnp.where(first, 0.0, x), axis=-1, keepdims=True)
        return jnp.where(first, a, b) * (1.0 / HEAD_DIM)

    ii = lax.broadcasted_iota(jnp.int32, (c, c), 0)
    jj = lax.broadcasted_iota(jnp.int32, (c, c), 1)
    dif = (ii - jj).astype(F32)

    def decay_mask(h0):
        lf = lgf[:, h0:h0 + 1]
        lb = lgb[:, h0:h0 + 1]
        return jnp.where(dif > 0, jnp.exp(lf * jnp.maximum(dif, 0.0)),
                         jnp.where(dif < 0, jnp.exp(lb * jnp.maximum(-dif, 0.0)), 2.0))

    dm = (decay_mask(0), decay_mask(HEAD_DIM))
    mfirst = first.astype(BF16)
    msecond = (1.0 - first.astype(F32)).astype(BF16)

    unroll = min(8, n_chunks)

    def incr(n, carry):
        sl = pl.ds(pl.multiple_of(n * c, c), c)
        kf = k_ref[0, sl, :].astype(F32)
        v = v_ref[0, sl, :]
        sfs_ref[n] = _dot_tn((kf * dkf).astype(BF16), v) * bd
        sbs_ref[n] = _dot_tn((kf * dkb).astype(BF16), v) * bd
        return carry

    lax.fori_loop(0, n_chunks, incr, 0, unroll=unroll)

    def scan_f(n, s):
        u = sfs_ref[n]
        sfs_ref[n] = s
        return cdf * s + u

    def scan_b(i, s):
        n = n_chunks - 1 - i
        u = sbs_ref[n]
        sbs_ref[n] = s
        return cdb * s + u

    sfo_ref[0, 0] = lax.fori_loop(0, n_chunks, scan_f, sf0_ref[0, 0])
    sbo_ref[0, 0] = lax.fori_loop(0, n_chunks, scan_b, sb0_ref[0, 0])

    def outp(n, carry):
        sl = pl.ds(pl.multiple_of(n * c, c), c)
        q = q_ref[0, sl, :]
        k = k_ref[0, sl, :]
        v = v_ref[0, sl, :]
        qf = q.astype(F32)
        o = None
        for hh, mk in enumerate((mfirst, msecond)):
            s = _dot_nt(q * mk, k) * dm[hh]
            oh = _dot(s.astype(BF16), v)
            o = oh if o is None else jnp.where(first, o, oh)
        qcat = jnp.concatenate([(qf * dqf).astype(BF16), (qf * dqb).astype(BF16)], axis=1)
        scat = jnp.concatenate([sfs_ref[n], sbs_ref[n]], axis=0).astype(BF16)
        o = o + _dot(qcat, scat)
        mu = head_mean(o)
        dlt = o - mu
        var = head_mean(dlt * dlt)
        on = dlt * lax.rsqrt(var + NORM_EPS) * gn_ref[0]
        y_ref[0, sl, :] = (_silu(g_ref[0, sl, :].astype(F32)) * on).astype(BF16)
        return carry

    lax.fori_loop(0, n_chunks, outp, 0, unroll=unroll)


def _retention(proj, lgt, gn, sf0, sb0):
    b, seq, _ = proj.shape
    col = lambda off: pl.BlockSpec((1, seq, LANES), lambda bi, p: (bi, 0, off + p))
    st = pl.BlockSpec((1, 1, LANES, LANES), lambda bi, p: (bi, p, 0, 0))
    return pl.pallas_call(
        functools.partial(_ret_kernel, seq=seq),
        grid=(b, N_PAIRS),
        in_specs=[col(CB_RQ), col(CB_RK), col(CB_RV), col(CB_RG),
                  pl.BlockSpec((1, 2, LANES), lambda bi, p: (p, 0, 0)),
                  pl.BlockSpec((1, 1, LANES), lambda bi, p: (p, 0, 0)),
                  st, st],
        out_specs=[pl.BlockSpec((1, seq, LANES), lambda bi, p: (bi, 0, p)), st, st],
        out_shape=[jax.ShapeDtypeStruct((b, seq, D_RET), BF16),
                   jax.ShapeDtypeStruct((b, N_PAIRS, LANES, LANES), F32),
                   jax.ShapeDtypeStruct((b, N_PAIRS, LANES, LANES), F32)],
        scratch_shapes=[pltpu.VMEM((seq // RET_CHUNK, LANES, LANES), F32),
                        pltpu.VMEM((seq // RET_CHUNK, LANES, LANES), F32)],
        compiler_params=_cp(("parallel", "parallel")),
        name="retention_s%d" % seq,
    )(proj, proj, proj, proj, lgt, gn, sf0, sb0)


def _conv_kernel(b_ref, c_ref, x_ref, w_ref, y_ref, *, seq):
    u = c_ref[0].astype(F32) * x_ref[0].astype(F32)
    row = lax.broadcasted_iota(jnp.int32, (seq, 1), 0)
    prev = jnp.where(row == 0, 0.0, pltpu.roll(u, 1, 0))
    nxt = jnp.where(row == seq - 1, 0.0, pltpu.roll(u, seq - 1, 0))
    y = w_ref[0:1, :] * prev + w_ref[1:2, :] * u + w_ref[2:3, :] * nxt
    y_ref[0] = (b_ref[0].astype(F32) * y).astype(BF16)


def _short_conv(proj, conv_w):
    b, seq, _ = proj.shape
    nblk = D_CONV // LANES
    col = lambda off: pl.BlockSpec((1, seq, LANES), lambda bi, j: (bi, 0, off + j))
    return pl.pallas_call(
        functools.partial(_conv_kernel, seq=seq),
        grid=(b, nblk),
        in_specs=[col(CB_CB), col(CB_CC), col(CB_CX),
                  pl.BlockSpec((3, LANES), lambda bi, j: (0, j))],
        out_specs=pl.BlockSpec((1, seq, LANES), lambda bi, j: (bi, 0, j)),
        out_shape=jax.ShapeDtypeStruct((b, seq, D_CONV), BF16),
        compiler_params=_cp(("parallel", "parallel")),
        name="short_conv_s%d" % seq,
    )(proj, proj, proj, conv_w)


def _na_bias_plan(rows):
    plan = np.full((3, NA_QROWS, NA_KROWS), -1, np.int64)
    for cl, rb in enumerate((0, NA_QROWS, rows - NA_QROWS)):
        ws = int(np.clip(rb - NA_ROWS // 2, 0, rows - NA_KROWS))
        for i in range(NA_QROWS):
            r = rb + i
            r0 = int(np.clip(r - NA_ROWS // 2, 0, rows - NA_ROWS))
            for j in range(NA_KROWS):
                kr = ws + j
                if r0 <= kr < r0 + NA_ROWS:
                    plan[cl, i, j] = kr - r + (NA_ROWS - 1)
    return plan


def _na_bias_kernel(rpb_ref, o_ref, *, plan):
    w = GRID_W
    ndr, ndc = 2 * NA_ROWS - 1, 2 * NA_COLS - 1
    base = pl.program_id(0) * (ndr * ndc)
    c = lax.broadcasted_iota(jnp.int32, (w, w), 0)
    kc = lax.broadcasted_iota(jnp.int32, (w, w), 1)
    dcidx = jnp.clip(kc - c, -(NA_COLS - 1), NA_COLS - 1) + (NA_COLS - 1)
    c0 = jnp.clip(c - NA_COLS // 2, 0, w - NA_COLS)
    col_in = jnp.logical_and(kc >= c0, kc < c0 + NA_COLS)
    neg = jnp.full((w, w), NEG_BIG, F32)
    tiles = []
    for dr in range(ndr):
        t = neg
        for dcv in range(ndc):
            t = jnp.where(dcidx == dcv, rpb_ref[base + dr * ndc + dcv], t)
        tiles.append(jnp.where(col_in, t * LOG2E, NEG_BIG))
    for cl in range(plan.shape[0]):
        for i in range(plan.shape[1]):
            for j in range(0, plan.shape[2], 2):
                pair = [tiles[int(d)] if d >= 0 else neg for d in plan[cl, i, j:j + 2]]
                o_ref[0, cl, i * w:(i + 1) * w, j * w:(j + 2) * w] = jnp.concatenate(pair, axis=1)


def _na_bias(rpb, rows):
    h = rpb.shape[0]
    tq, tk = NA_QROWS * GRID_W, NA_KROWS * GRID_W
    return pl.pallas_call(
        functools.partial(_na_bias_kernel, plan=_na_bias_plan(rows)),
        grid=(h,),
        in_specs=[pl.BlockSpec(memory_space=pltpu.SMEM)],
        out_specs=pl.BlockSpec((1, 3, tq, tk), lambda hh: (hh, 0, 0, 0)),
        out_shape=jax.ShapeDtypeStruct((h, 3, tq, tk), F32),
        compiler_params=_cp(("parallel",)),
        name="na_bias_expand",
    )(rpb.reshape(-1))


def _na_kernel(q_ref, k_ref, v_ref, kc_ref, vc_ref, bias_ref, o_ref, v1_ref, vc1_ref, *, rows):
    qb = pl.program_id(1)
    lane = lax.broadcasted_iota(jnp.int32, (1, LANES), 1)
    first = lane < HEAD_DIM

    @pl.when(qb == 0)
    def _():
        one = jnp.ones((1, LANES), BF16)
        for p in range(N_PAIRS):
            cols = slice(p * LANES, (p + 1) * LANES)
            v1_ref[2 * p] = jnp.where(first, v_ref[0, :, cols], one)
            v1_ref[2 * p + 1] = jnp.where(first, one, v_ref[0, :, cols])
            vc1_ref[2 * p] = jnp.where(first, vc_ref[0, :, cols], one)
            vc1_ref[2 * p + 1] = jnp.where(first, one, vc_ref[0, :, cols])

    ws = jnp.clip(qb * NA_QROWS - NA_ROWS // 2, 0, rows - NA_KROWS)
    sl = pl.ds(pl.multiple_of(ws * GRID_W, GRID_W), NA_KROWS * GRID_W)
    for p in range(N_PAIRS):
        cols = slice(p * LANES, (p + 1) * LANES)
        q = q_ref[0, :, cols]
        kw = k_ref[0, sl, cols]
        kc = kc_ref[0, :, cols]
        pv = []
        for hh in range(2):
            mk = (first if hh == 0 else jnp.logical_not(first)).astype(BF16)
            qh = q * mk
            s_loc = _dot_nt(qh, kw) + bias_ref[2 * p + hh, 0]
            s_ctx = _dot_nt(qh, kc)
            m = jnp.maximum(jnp.max(s_loc, axis=-1, keepdims=True), jnp.max(s_ctx, axis=-1, keepdims=True))
            p_loc = jnp.exp2(s_loc - m).astype(BF16)
            p_ctx = jnp.exp2(s_ctx - m).astype(BF16)
            pv.append(_dot(p_loc, v1_ref[2 * p + hh, sl, :]) + _dot(p_ctx, vc1_ref[2 * p + hh]))
        num = jnp.where(first, pv[0], pv[1])
        den = jnp.where(first, pltpu.roll(pv[0], HEAD_DIM, 1), pltpu.roll(pv[1], HEAD_DIM, 1))
        o_ref[0, :, cols] = (num / den).astype(BF16)


def _na(proj, proj_ctx, bias):
    b, seq, _ = proj.shape
    ctx_len = proj_ctx.shape[1]
    rows = seq // GRID_W
    nqb = rows // NA_QROWS
    tq = NA_QROWS * GRID_W
    tk = NA_KROWS * GRID_W
    nh = 2 * N_PAIRS
    cq, ck, cv = (CB_NQ * LANES // D_NA, CB_NK * LANES // D_NA, CB_NV * LANES // D_NA)

    def cls(qb):
        return jnp.where(qb == 0, 0, jnp.where(qb == nqb - 1, 2, 1))

    return pl.pallas_call(
        functools.partial(_na_kernel, rows=rows),
        grid=(b, nqb),
        in_specs=[pl.BlockSpec((1, tq, D_NA), lambda bi, qb: (bi, qb, cq)),
                  pl.BlockSpec((1, seq, D_NA), lambda bi, qb: (bi, 0, ck)),
                  pl.BlockSpec((1, seq, D_NA), lambda bi, qb: (bi, 0, cv)),
                  pl.BlockSpec((1, ctx_len, D_NA), lambda bi, qb: (bi, 0, ck)),
                  pl.BlockSpec((1, ctx_len, D_NA), lambda bi, qb: (bi, 0, cv)),
                  pl.BlockSpec((nh, 1, tq, tk), lambda bi, qb: (0, cls(qb), 0, 0))],
        out_specs=pl.BlockSpec((1, tq, D_NA), lambda bi, qb: (bi, qb, 0)),
        out_shape=jax.ShapeDtypeStruct((b, seq, D_NA), BF16),
        scratch_shapes=[pltpu.VMEM((nh, seq, LANES), BF16), pltpu.VMEM((nh, ctx_len, LANES), BF16)],
        compiler_params=_cp(("parallel", "arbitrary")),
        name="na_attn",
    )(proj, proj, proj, proj_ctx, proj_ctx, bias)


def _ctx_attn_kernel(q_ref, k_ref, v_ref, o_ref):
    q = q_ref[0]
    k = k_ref[0]
    v = v_ref[0]
    lane = lax.broadcasted_iota(jnp.int32, (1, LANES), 1)
    first = lane < HEAD_DIM
    o = None
    for hh in range(2):
        mk = (first if hh == 0 else jnp.logical_not(first)).astype(BF16)
        s = _dot_nt(q * mk, k)
        m = jnp.max(s, axis=-1, keepdims=True)
        p = jnp.exp(s - m)
        l = jnp.sum(p, axis=-1, keepdims=True)
        oh = _dot(p.astype(BF16), v) / l
        o = oh if o is None else jnp.where(first, o, oh)
    o_ref[0] = o.astype(BF16)


def _ctx_attn(proj_ctx):
    b, ctx_len, _ = proj_ctx.shape
    col = lambda off: pl.BlockSpec((1, ctx_len, LANES), lambda bi, p: (bi, 0, off + p))
    return pl.pallas_call(
        _ctx_attn_kernel,
        grid=(b, N_PAIRS),
        in_specs=[col(CB_NQ), col(CB_NK), col(CB_NV)],
        out_specs=pl.BlockSpec((1, ctx_len, LANES), lambda bi, p: (bi, 0, p)),
        out_shape=jax.ShapeDtypeStruct((b, ctx_len, D_NA), BF16),
        compiler_params=_cp(("parallel", "parallel")),
        name="ctx_attn",
    )(proj_ctx, proj_ctx, proj_ctx)


def _outproj_kernel(*refs, route):
    if route:
        (yr_ref, yc_ref, yn_ref, h_ref, w_ref, g1_ref, n2_ref, sh_ref, sc_ref, rw_ref, rb_ref,
         ho_ref, xn_ref, idx_ref, wt_ref) = refs
    else:
        yr_ref, yc_ref, yn_ref, h_ref, w_ref, g1_ref, n2_ref, sh_ref, sc_ref, ho_ref, xn_ref = refs
    tm = h_ref.shape[0]
    sub = min(OUT_PROJ_SUB, tm)
    if route:
        r_hi, r_lo = _split_bf16(rw_ref[...])
        r_both = jnp.concatenate([r_hi, r_lo], axis=0)
    for r in range(tm // sub):
        rs = slice(r * sub, (r + 1) * sub)
        ycat = jnp.concatenate([yr_ref[rs, :], yc_ref[rs, :], yn_ref[rs, :]], axis=-1)
        h = h_ref[rs, :] + g1_ref[0] * _dot(ycat, w_ref[...])
        ho_ref[rs, :] = h
        a = _norm_mod(h, n2_ref[...], sh_ref[0], sc_ref[0])
        xn_ref[rs, :] = a.astype(xn_ref.dtype)
        if route:
            a_hi, a_lo = _split_bf16(a)
            t_hi = _dot_nt(r_both, a_hi)
            logits = t_hi[:N_EXPERTS] + t_hi[N_EXPERTS:] + _dot_nt(r_hi, a_lo) + rb_ref[...]
            eidx = lax.broadcasted_iota(jnp.int32, logits.shape, 0)
            m1 = jnp.max(logits, axis=0, keepdims=True)
            i1 = jnp.min(jnp.where(logits == m1, eidx, N_EXPERTS), axis=0, keepdims=True)
            rest = jnp.where(eidx == i1, -jnp.inf, logits)
            m2 = jnp.max(rest, axis=0, keepdims=True)
            i2 = jnp.min(jnp.where(rest == m2, eidx, N_EXPERTS), axis=0, keepdims=True)
            e2 = jnp.exp(m2 - m1)
            den = 1.0 + e2
            idx_ref[:, rs] = jnp.concatenate([i1, i2], axis=0)
            wt_ref[:, rs] = jnp.concatenate([1.0 / den, e2 / den], axis=0)


def _out_proj(yr, yc, yn, h2, w_bf, g1, n2g, sh2, sc2, seq, tm, router=None):
    t, d = h2.shape
    tps = seq // tm
    route = router is not None
    row = lambda wdt: pl.BlockSpec((tm, wdt), lambda i: (i, 0))
    mod = pl.BlockSpec((1, 1, d), lambda i: (i // tps, 0, 0))
    in_specs = [row(D_RET), row(D_CONV), row(D_NA), row(d),
                pl.BlockSpec((d, d), lambda i: (0, 0)), mod,
                pl.BlockSpec((1, d), lambda i: (0, 0)), mod, mod]
    args = [yr, yc, yn, h2, w_bf, g1, n2g.reshape(1, d), sh2, sc2]
    out_specs = [row(d), row(d)]
    out_shape = [jax.ShapeDtypeStruct((t, d), F32), jax.ShapeDtypeStruct((t, d), F32 if route else BF16)]
    if route:
        rw, rb = router
        in_specs += [pl.BlockSpec((N_EXPERTS, d), lambda i: (0, 0)),
                     pl.BlockSpec((N_EXPERTS, 1), lambda i: (0, 0))]
        args += [rw.T, rb.reshape(N_EXPERTS, 1)]
        out_specs += [pl.BlockSpec((2, tm), lambda i: (0, i))] * 2
        out_shape += [jax.ShapeDtypeStruct((2, t), jnp.int32), jax.ShapeDtypeStruct((2, t), F32)]
    return pl.pallas_call(
        functools.partial(_outproj_kernel, route=route),
        grid=(t // tm,),
        in_specs=in_specs,
        out_specs=out_specs,
        out_shape=out_shape,
        compiler_params=_cp(("parallel",)),
        name="out_proj_route" if route else "out_proj_t%d" % t,
    )(*args)


def _swiglu_chunks(x, w_in_ref, w_out_ref, ff, n_chunks):
    tf = -(-ff // (n_chunks * MXU_COLS)) * MXU_COLS
    y = None
    for c0 in range(0, ff, tf):
        c1 = min(c0 + tf, ff)
        gate = _dot(x, w_in_ref[:, c0:c1])
        up = _dot(x, w_in_ref[:, ff + c0:ff + c1])
        part = _dot((_silu(gate) * up).astype(BF16), w_out_ref[c0:c1, :])
        y = part if y is None else y + part
    return y


def _ffn_kernel(*refs, ff, n_chunks, cast):
    if cast:
        x_ref, h_ref, wi_ref, wo_ref, g2_ref, c_ref, o_ref, cb_ref = refs
        cb_ref[...] = c_ref[...].astype(BF16)
    else:
        x_ref, h_ref, wi_ref, wo_ref, g2_ref, o_ref = refs
    y = _swiglu_chunks(x_ref[...], wi_ref, wo_ref, ff, n_chunks)
    o_ref[...] = h_ref[...] + g2_ref[0] * y


def _ffn(xn, h2, w_in_bf, w_out_bf, g2, seq, tm, n_chunks, cast=None):
    t, d = h2.shape
    ff = w_out_bf.shape[0]
    tps = seq // tm
    steps = t // tm
    resident = pl.Buffered(1)
    in_specs = [pl.BlockSpec((tm, d), lambda i: (i, 0)),
                pl.BlockSpec((tm, d), lambda i: (i, 0)),
                pl.BlockSpec((d, 2 * ff), lambda i: (0, 0), pipeline_mode=resident),
                pl.BlockSpec((ff, d), lambda i: (0, 0), pipeline_mode=resident),
                pl.BlockSpec((1, 1, d), lambda i: (i // tps, 0, 0))]
    out_specs = [pl.BlockSpec((tm, d), lambda i: (i, 0))]
    out_shape = [jax.ShapeDtypeStruct((t, d), F32)]
    args = [xn, h2, w_in_bf, w_out_bf, g2]
    if cast is not None:
        cr, cc = cast.shape
        assert cr % (steps * 16) == 0
        in_specs.append(pl.BlockSpec((cr // steps, cc), lambda i: (i, 0)))
        out_specs.append(pl.BlockSpec((cr // steps, cc), lambda i: (i, 0)))
        out_shape.append(jax.ShapeDtypeStruct(cast.shape, BF16))
        args.append(cast)
    res = pl.pallas_call(
        functools.partial(_ffn_kernel, ff=ff, n_chunks=n_chunks, cast=cast is not None),
        grid=(steps,),
        in_specs=in_specs,
        out_specs=out_specs,
        out_shape=out_shape,
        compiler_params=_cp(("parallel",)),
        name="ffn_t%d" % t,
    )(*args)
    return res if cast is not None else res[0]


def _dispatch_kernel(s0_ref, s1_ref, zs_ref, zv_ref, x_ref, wo_ref, xs_ref, wob_ref,
                     zbuf_ref, sem, *, rows, n_cast):
    @pl.when(pl.program_id(0) == 0)
    def _():
        zbuf_ref[...] = jnp.zeros_like(zbuf_ref)
        for e in range(2 * N_EXPERTS):
            @pl.when(zv_ref[e] != 0)
            def _():
                dst = xs_ref.at[pl.ds(pl.multiple_of(zs_ref[e], MOE_TS), MOE_TS)]
                pltpu.make_async_copy(zbuf_ref, dst, sem.at[1]).start()
        for e in range(2 * N_EXPERTS):
            @pl.when(zv_ref[e] != 0)
            def _():
                pltpu.make_async_copy(zbuf_ref, xs_ref.at[pl.ds(0, MOE_TS)], sem.at[1]).wait()

    wo_rows = wo_ref.shape[0] // n_cast

    def issue(g):
        for u in range(DMA_UNROLL):
            r = g * DMA_UNROLL + u
            src = x_ref.at[pl.ds(r, 1)]
            pltpu.make_async_copy(src, xs_ref.at[pl.ds(s0_ref[r], 1)], sem.at[0]).start(priority=u % 2)
            pltpu.make_async_copy(src, xs_ref.at[pl.ds(s1_ref[r], 1)], sem.at[0]).start(priority=(u + 1) % 2)

    def issue_and_cast(g, carry):
        rs = pl.ds(pl.multiple_of(g * wo_rows, wo_rows), wo_rows)
        wob_ref[rs, :] = wo_ref[rs, :].astype(BF16)
        issue(g)
        return carry

    def issue_only(g, carry):
        issue(g)
        return carry

    n_iter = rows // DMA_UNROLL
    lax.fori_loop(0, n_cast, issue_and_cast, 0)
    lax.fori_loop(n_cast, n_iter, issue_only, 0)
    for _ in range(2):
        pltpu.make_async_copy(x_ref, xs_ref.at[pl.ds(0, rows)], sem.at[0]).wait()


def _dispatch(xn, slot0, slot1, zplan, n_slots, rows, w_out):
    t, d = xn.shape
    zstart, zvalid = zplan
    steps = t // rows
    n_exp, ff, _ = w_out.shape
    wo2 = w_out.reshape(n_exp * ff, d)
    wo_rows = n_exp * ff // steps
    n_iter = rows // DMA_UNROLL
    cast_rows = 16 * -(-wo_rows // (16 * n_iter))
    n_cast = wo_rows // cast_rows
    assert n_cast * cast_rows == wo_rows and n_cast <= n_iter
    xs, wob = pl.pallas_call(
        functools.partial(_dispatch_kernel, rows=rows, n_cast=n_cast),
        grid=(steps,),
        in_specs=[pl.BlockSpec((rows,), lambda i: (i,), memory_space=pltpu.SMEM),
                  pl.BlockSpec((rows,), lambda i: (i,), memory_space=pltpu.SMEM),
                  pl.BlockSpec((2 * N_EXPERTS,), lambda i: (0,), memory_space=pltpu.SMEM),
                  pl.BlockSpec((2 * N_EXPERTS,), lambda i: (0,), memory_space=pltpu.SMEM),
                  pl.BlockSpec((rows, d), lambda i: (i, 0)),
                  pl.BlockSpec((wo_rows, d), lambda i: (i, 0))],
        out_specs=[pl.BlockSpec(memory_space=pl.ANY),
                   pl.BlockSpec((wo_rows, d), lambda i: (i, 0))],
        out_shape=[jax.ShapeDtypeStruct((n_slots, d), xn.dtype), jax.ShapeDtypeStruct(wo2.shape, BF16)],
        scratch_shapes=[pltpu.VMEM((MOE_TS, d), xn.dtype), pltpu.SemaphoreType.DMA((2,))],
        compiler_params=_cp(("arbitrary",)),
        name="moe_dispatch",
    )(slot0, slot1, zstart, zvalid, xn, wo2)
    return xs, wob.reshape(n_exp, ff, d)


def _expert_kernel(be_ref, nu_ref, x_ref, wi_ref, wo_ref, o_ref, *, ff, n_chunks):
    used = pl.program_id(0) < nu_ref[0]

    @pl.when(used)
    def _():
        o_ref[...] = _swiglu_chunks(x_ref[...].astype(BF16), wi_ref.at[0], wo_ref.at[0], ff, n_chunks)

    @pl.when(jnp.logical_not(used))
    def _():
        o_ref[...] = jnp.zeros_like(o_ref)


def _experts(xs, block_expert, n_used, w_in_bf, w_out_bf, n_chunks):
    n, d = xs.shape
    ff = w_out_bf.shape[1]
    ts = MOE_TS
    grid_spec = pltpu.PrefetchScalarGridSpec(
        num_scalar_prefetch=2,
        grid=(n // ts,),
        in_specs=[pl.BlockSpec((ts, d), lambda j, be, nu: (j, 0)),
                  pl.BlockSpec((1, d, 2 * ff), lambda j, be, nu: (be[j], 0, 0)),
                  pl.BlockSpec((1, ff, d), lambda j, be, nu: (be[j], 0, 0))],
        out_specs=pl.BlockSpec((ts, d), lambda j, be, nu: (j, 0)),
    )
    return pl.pallas_call(
        functools.partial(_expert_kernel, ff=ff, n_chunks=n_chunks),
        grid_spec=grid_spec,
        out_shape=jax.ShapeDtypeStruct((n, d), F32),
        compiler_params=_cp(("arbitrary",)),
        name="moe_experts",
    )(block_expert, n_used, xs, w_in_bf, w_out_bf)


def _combine_kernel(p0_ref, p1_ref, ys_ref, h_ref, w0_ref, w1_ref, g2_ref, fg_ref, o_ref,
                    y0_ref, y1_ref, sem, *, rows):
    def issue(g, carry):
        for u in range(DMA_UNROLL):
            r = g * DMA_UNROLL + u
            pltpu.make_async_copy(ys_ref.at[pl.ds(p0_ref[r], 1)], y0_ref.at[pl.ds(r, 1)],
                                  sem.at[0]).start(priority=u % 2)
            pltpu.make_async_copy(ys_ref.at[pl.ds(p1_ref[r], 1)], y1_ref.at[pl.ds(r, 1)],
                                  sem.at[1]).start(priority=(u + 1) % 2)
        return carry

    lax.fori_loop(0, rows // DMA_UNROLL, issue, 0)
    pltpu.make_async_copy(ys_ref.at[pl.ds(0, rows)], y0_ref, sem.at[0]).wait()
    pltpu.make_async_copy(ys_ref.at[pl.ds(0, rows)], y1_ref, sem.at[1]).wait()
    h = h_ref[...] + g2_ref[0] * (w0_ref[...] * y0_ref[...] + w1_ref[...] * y1_ref[...])
    ms = jnp.mean(h * h, axis=-1, keepdims=True)
    o_ref[...] = h * lax.rsqrt(ms + NORM_EPS) * fg_ref[...]


def _combine(pos0, pos1, ys, h2, w0, w1, g2, final_g, seq, rows):
    t, d = h2.shape
    tps = seq // rows
    blk = pl.BlockSpec((rows, d), lambda i: (i, 0))
    col = pl.BlockSpec((rows, 1), lambda i: (i, 0))
    return pl.pallas_call(
        functools.partial(_combine_kernel, rows=rows),
        grid=(t // rows,),
        in_specs=[pl.BlockSpec((rows,), lambda i: (i,), memory_space=pltpu.SMEM),
                  pl.BlockSpec((rows,), lambda i: (i,), memory_space=pltpu.SMEM),
                  pl.BlockSpec(memory_space=pl.ANY),
                  blk, col, col,
                  pl.BlockSpec((1, 1, d), lambda i: (i // tps, 0, 0)),
                  pl.BlockSpec((1, d), lambda i: (0, 0))],
        out_specs=blk,
        out_shape=jax.ShapeDtypeStruct((t, d), F32),
        scratch_shapes=[pltpu.VMEM((rows, d), F32), pltpu.VMEM((rows, d), F32),
                        pltpu.SemaphoreType.DMA((2,))],
        compiler_params=_cp(("arbitrary",)),
        name="moe_combine_norm",
    )(pos0, pos1, ys, h2, w0, w1, g2, final_g.reshape(1, d))


def _route_plan(idx):
    t = idx.shape[1]
    ts = MOE_TS
    n_slots = 2 * t + N_EXPERTS * ts
    e_flat = idx.reshape(-1)
    onehot = (e_flat[:, None] == jnp.arange(N_EXPERTS, dtype=jnp.int32)[None, :]).astype(jnp.int32)
    csum = jnp.cumsum(onehot, axis=0)
    rank = jnp.sum((csum - onehot) * onehot, axis=1)
    counts = csum[-1]
    padded = ((counts + ts - 1) // ts) * ts
    ends = jnp.cumsum(padded)
    offs = ends - padded
    slot = (jnp.sum(onehot * offs[None, :], axis=1) + rank).astype(jnp.int32)
    starts = jnp.arange(n_slots // ts, dtype=jnp.int32) * ts
    block_expert = jnp.minimum(jnp.sum((starts[:, None] >= ends[None, :]).astype(jnp.int32), axis=1),
                               N_EXPERTS - 1).astype(jnp.int32)
    n_used = (ends[-1] // ts).astype(jnp.int32).reshape(1)
    tail = ends[-1] + jnp.arange(N_EXPERTS, dtype=jnp.int32) * ts
    zstart = jnp.concatenate([ends - ts, tail])
    zvalid = jnp.concatenate([padded > 0, tail < n_slots]).astype(jnp.int32)
    zstart = jnp.where(zvalid != 0, zstart, 0).astype(jnp.int32)
    return slot[:t], slot[t:], block_expert, n_used, (zstart, zvalid), n_slots


def kernel(x, c, ctx, c_ctx, ada_w, ada_b, norm1_g, norm2_g, w_in, w_out, ret_decay_logit, ret_gn_g,
           conv_w, na_rpb, ffn_w_in, ffn_w_out, moe_router_w, moe_router_b, moe_w_in, moe_w_out, final_g):
    b, seq, d = x.shape
    ctx_len = ctx.shape[1]
    depth = ada_w.shape[0]
    assert depth == 2, "the final norm is fused into the last (MoE) layer's combine step"
    rows = seq // GRID_W
    t_lat, t_ctx = b * seq, b * ctx_len
    tm_lat = min(1024, seq)
    tm_ctx = ctx_len

    c8 = jnp.zeros((8, d), F32).at[:b].set(c).at[b].set(c_ctx)
    mods = _ada(c8, ada_w, ada_b)
    tables = _rope_tables(seq)

    h = x.reshape(t_lat, d)
    hc = ctx.reshape(t_ctx, d)
    zero_state = jnp.zeros((b, N_PAIRS, LANES, LANES), F32)
    out = None
    for layer in range(depth):
        update_ctx = layer < depth - 1
        m = mods[layer].reshape(8, 6, d)
        lat = [m[:b, i].reshape(b, 1, d) for i in range(6)]
        cx = [jnp.broadcast_to(m[b, i].reshape(1, 1, d), (b, 1, d)) for i in range(6)]
        w_in_bf = w_in[layer].astype(BF16)
        w_out_bf = w_out[layer].astype(BF16)
        lgt = ret_decay_logit[layer].reshape(2, N_PAIRS, LANES // HEAD_DIM)
        lgt = jnp.repeat(lgt, HEAD_DIM, axis=2).transpose(1, 0, 2)
        gn = ret_gn_g[layer].reshape(N_PAIRS, 1, LANES)

        proj_c = _in_proj(hc, norm1_g[layer], cx[0], cx[1], w_in_bf, None, ctx_len, tm_ctx)
        proj_c = proj_c.reshape(b, ctx_len, D_IN_PROJ)
        proj = _in_proj(h, norm1_g[layer], lat[0], lat[1], w_in_bf, tables, seq, tm_lat)
        proj = proj.reshape(b, seq, D_IN_PROJ)

        y_ret_c, s_fwd, s_bwd = _retention(proj_c, lgt, gn, zero_state, zero_state)
        y_ret, _, _ = _retention(proj, lgt, gn, s_fwd, s_bwd)
        y_conv = _short_conv(proj, conv_w[layer])
        y_na = _na(proj, proj_c, _na_bias(na_rpb[layer], rows))

        if layer % 2 == 0:
            h, xn = _out_proj(y_ret.reshape(t_lat, D_RET), y_conv.reshape(t_lat, D_CONV),
                              y_na.reshape(t_lat, D_NA), h, w_out_bf, lat[2], norm2_g[layer],
                              lat[3], lat[4], seq, tm_lat)
            fw_in = ffn_w_in[layer // 2].astype(BF16)
            fw_out = ffn_w_out[layer // 2].astype(BF16)
            mw = moe_w_in[(layer + 1) // 2]
            h, mw_in_bf = _ffn(xn, h, fw_in, fw_out, lat[5], seq, min(512, seq), FFN_CHUNKS,
                               cast=mw.reshape(-1, mw.shape[-1]))
            mw_in_bf = mw_in_bf.reshape(mw.shape)
            if update_ctx:
                y_conv_c = _short_conv(proj_c, conv_w[layer])
                y_na_c = _ctx_attn(proj_c)
                hc, xnc = _out_proj(y_ret_c.reshape(t_ctx, D_RET), y_conv_c.reshape(t_ctx, D_CONV),
                                    y_na_c.reshape(t_ctx, D_NA), hc, w_out_bf, cx[2], norm2_g[layer],
                                    cx[3], cx[4], ctx_len, tm_ctx)
                hc = _ffn(xnc, hc, fw_in, fw_out, cx[5], ctx_len, tm_ctx, FFN_CHUNKS)
        else:
            e = layer // 2
            h, xn, idx, wts = _out_proj(y_ret.reshape(t_lat, D_RET), y_conv.reshape(t_lat, D_CONV),
                                        y_na.reshape(t_lat, D_NA), h, w_out_bf, lat[2], norm2_g[layer],
                                        lat[3], lat[4], seq, tm_lat,
                                        router=(moe_router_w[e], moe_router_b[e]))
            pos0, pos1, block_expert, n_used, zstart, n_slots = _route_plan(idx)
            xs, mw_out_bf = _dispatch(xn, pos0, pos1, zstart, n_slots, min(512, seq), moe_w_out[e])
            ys = _experts(xs, block_expert, n_used, mw_in_bf, mw_out_bf, FFN_CHUNKS)
            out = _combine(pos0, pos1, ys, h, wts[0].reshape(t_lat, 1), wts[1].reshape(t_lat, 1),
                           lat[5], final_g, seq, min(512, seq))
            h = out
    return out.reshape(b, seq, d)
```

```python
import functools

import numpy as np
import jax
import jax.numpy as jnp
from jax import lax
from jax.experimental import pallas as pl
from jax.experimental.pallas import tpu as pltpu

F32 = jnp.float32
BF16 = jnp.bfloat16

LANES = 128
HEAD_DIM = 64
GRID_W = 64
N_CONV_GROUPS = 4
N_RET_HEADS = 6
N_NA_HEADS = 6
D_RET = N_RET_HEADS * HEAD_DIM
D_CONV = N_CONV_GROUPS * HEAD_DIM
D_NA = N_NA_HEADS * HEAD_DIM
D_IN_PROJ = 4 * D_RET + 3 * D_CONV + 3 * D_NA
N_PAIRS = D_RET // LANES
NA_ROWS = 8
NA_COLS = 16
N_EXPERTS = 8
ROPE_BASE = 10000.0
NORM_EPS = 1e-6
NEG_BIG = -1e30
LOG2E = 1.4426950408889634

RET_CHUNK = 256
NA_QROWS = 4
NA_KROWS = 12
MOE_TS = 512
DMA_UNROLL = 8
COMBINE_GROUP = 32
FFN_CHUNKS = 2
IN_PROJ_SUB = 512
OUT_PROJ_SUB = 512
MXU_COLS = 256
IN_PROJ_CHUNK = 3 * MXU_COLS
VMEM_LIMIT = 56 * 1024 * 1024

CB_RQ, CB_RK, CB_RV, CB_RG = 0, 3, 6, 9
CB_CB, CB_CC, CB_CX = 12, 14, 16
CB_NQ, CB_NK, CB_NV = 18, 21, 24


def _cp(sem, vmem=VMEM_LIMIT):
    return pltpu.CompilerParams(dimension_semantics=sem, vmem_limit_bytes=vmem)


def _silu(x):
    return x * (1.0 / (1.0 + jnp.exp(-x)))


def _dot(a, b):
    return jnp.dot(a, b, preferred_element_type=F32)


def _dot_nt(a, b):
    return lax.dot_general(a, b, (((1,), (1,)), ((), ())), preferred_element_type=F32)


def _dot_tn(a, b):
    return lax.dot_general(a, b, (((0,), (0,)), ((), ())), preferred_element_type=F32)


def _split_bf16(x):
    hi = x.astype(BF16)
    lo = (x - hi.astype(F32)).astype(BF16)
    return hi, lo


def _ada_kernel(c_ref, w_ref, b_ref, o_ref):
    x = _silu(c_ref[...]).astype(BF16)
    o_ref[0] = _dot(x, w_ref[0].astype(BF16)) + b_ref[0]


def _ada(c8, ada_w, ada_b):
    depth, d, n = ada_w.shape
    tn = n // 4
    return pl.pallas_call(
        _ada_kernel,
        grid=(depth, n // tn),
        in_specs=[pl.BlockSpec((8, d), lambda l, j: (0, 0)),
                  pl.BlockSpec((1, d, tn), lambda l, j: (l, 0, j)),
                  pl.BlockSpec((1, 1, tn), lambda l, j: (l, 0, j))],
        out_specs=pl.BlockSpec((1, 8, tn), lambda l, j: (l, 0, j)),
        out_shape=jax.ShapeDtypeStruct((depth, 8, n), F32),
        compiler_params=_cp(("parallel", "parallel")),
        name="ada_mod",
    )(c8, ada_w, ada_b.reshape(depth, 1, n))


def _norm_mod(x, g, sh, sc):
    ms = jnp.mean(x * x, axis=-1, keepdims=True)
    y = x * lax.rsqrt(ms + NORM_EPS) * g
    return y * (1.0 + sc) + sh


def _inproj_kernel(*refs, rope):
    if rope:
        h_ref, g_ref, sh_ref, sc_ref, w_ref, cos_ref, sa_ref, sb_ref, o_ref = refs
    else:
        h_ref, g_ref, sh_ref, sc_ref, w_ref, o_ref = refs
    tm = h_ref.shape[0]
    sub = min(IN_PROJ_SUB, tm)
    cw = IN_PROJ_CHUNK
    for r in range(tm // sub):
        rs = slice(r * sub, (r + 1) * sub)
        xn = _norm_mod(h_ref[rs, :], g_ref[...], sh_ref[0], sc_ref[0]).astype(BF16)
        for c0 in range(0, D_IN_PROJ, cw):
            c1 = min(c0 + cw, D_IN_PROJ)
            acc = _dot(xn, w_ref[:, c0:c1])
            for j in range((c1 - c0) // LANES):
                blk = acc[:, j * LANES:(j + 1) * LANES]
                cb = c0 // LANES + j
                if rope and cb < CB_RV:
                    blk = (blk * cos_ref[rs, :] + pltpu.roll(blk, 16, 1) * sa_ref[rs, :]
                           + pltpu.roll(blk, LANES - 16, 1) * sb_ref[rs, :])
                if CB_RK <= cb < CB_RV:
                    blk = blk * (HEAD_DIM ** -0.5)
                elif CB_NQ <= cb < CB_NK:
                    blk = blk * (HEAD_DIM ** -0.5 * (LOG2E if rope else 1.0))
                o_ref[rs, cb * LANES:(cb + 1) * LANES] = blk.astype(BF16)


def _in_proj(h2, g, sh, sc, w_bf, tables, seq, tm):
    t, d = h2.shape
    tiles_per_seq = seq // tm
    rope = tables is not None
    in_specs = [pl.BlockSpec((tm, d), lambda i: (i, 0)),
                pl.BlockSpec((1, d), lambda i: (0, 0)),
                pl.BlockSpec((1, 1, d), lambda i: (i // tiles_per_seq, 0, 0)),
                pl.BlockSpec((1, 1, d), lambda i: (i // tiles_per_seq, 0, 0)),
                pl.BlockSpec((d, D_IN_PROJ), lambda i: (0, 0))]
    args = [h2, g.reshape(1, d), sh, sc, w_bf]
    if rope:
        in_specs += [pl.BlockSpec((tm, LANES), lambda i: (i % tiles_per_seq, 0))] * 3
        args += list(tables)
    return pl.pallas_call(
        functools.partial(_inproj_kernel, rope=rope),
        grid=(t // tm,),
        in_specs=in_specs,
        out_specs=pl.BlockSpec((tm, D_IN_PROJ), lambda i: (i, 0)),
        out_shape=jax.ShapeDtypeStruct((t, D_IN_PROJ), BF16),
        compiler_params=_cp(("parallel",)),
        name="in_proj_rope" if rope else "in_proj_ctx",
    )(*args)


def _rope_tables(seq):
    t = np.arange(seq)
    row = (t // GRID_W).astype(np.float32)
    col = (t % GRID_W).astype(np.float32)
    n_freq = HEAD_DIM // 4
    inv_freq = (ROPE_BASE ** (-np.arange(n_freq, dtype=np.float32) / n_freq)).astype(np.float32)
    ang_r = row[:, None] * inv_freq
    ang_c = col[:, None] * inv_freq
    cos_h = np.concatenate([np.cos(ang_r), np.cos(ang_r), np.cos(ang_c), np.cos(ang_c)], axis=1)
    sin_h = np.concatenate([np.sin(ang_r), np.sin(ang_r), np.sin(ang_c), np.sin(ang_c)], axis=1)
    lane = np.arange(HEAD_DIM)
    second = (lane % 32) >= 16
    sa = np.where(second[None, :], sin_h, 0.0)
    sb = np.where(second[None, :], 0.0, -sin_h)
    tile2 = lambda a: jnp.asarray(np.concatenate([a, a], axis=1), F32)
    return tile2(cos_h), tile2(sa), tile2(sb)


def _ret_kernel(q_ref, k_ref, v_ref, g_ref, lgt_ref, gn_ref, sf0_ref, sb0_ref,
                y_ref, sfo_ref, sbo_ref, sfs_ref, sbs_ref, *, seq):
    c = RET_CHUNK
    n_chunks = seq // c
    lg = jax.nn.log_sigmoid(lgt_ref[0])
    lgf, lgb = lg[0:1, :], lg[1:2, :]
    pos = lax.broadcasted_iota(jnp.int32, (c, 1), 0).astype(F32)
    dkf = jnp.exp(lgf * (c - 1.0 - pos))
    dkb = jnp.exp(lgb * pos)
    dqf = jnp.exp(lgf * (pos + 1.0))
    dqb = jnp.exp(lgb * (c - pos))
    cdf = jnp.exp(lgf * float(c))
    cdb = jnp.exp(lgb * float(c))
    lane = lax.broadcasted_iota(jnp.int32, (1, LANES), 1)
    first = lane < HEAD_DIM
    ri = lax.broadcasted_iota(jnp.int32, (LANES, LANES), 0)
    ci = lax.broadcasted_iota(jnp.int32, (LANES, LANES), 1)
    same = (ri < HEAD_DIM) == (ci < HEAD_DIM)
    bd = same.astype(F32)

    def head_mean(x):
        a = jnp.sum(jnp.where(first, x, 0.0), axis=-1, keepdims=True)
        b = jnp.sum(jnp.where(first, 0.0, x), axis=-1, keepdims=True)
        return jnp.where(first, a, b) * (1.0 / HEAD_DIM)

    ii = lax.broadcasted_iota(jnp.int32, (c, c), 0)
    jj = lax.broadcasted_iota(jnp.int32, (c, c), 1)
    dif = (ii - jj).astype(F32)

    def decay_mask(h0):
        lf = lgf[:, h0:h0 + 1]
        lb = lgb[:, h0:h0 + 1]
        return jnp.where(dif > 0, jnp.exp(lf * jnp.maximum(dif, 0.0)),
                         jnp.where(dif < 0, jnp.exp(lb * jnp.maximum(-dif, 0.0)), 2.0))

    dm = (decay_mask(0), decay_mask(HEAD_DIM))
    mfirst = first.astype(BF16)
    msecond = (1.0 - first.astype(F32)).astype(BF16)

    unroll = min(8, n_chunks)

    def incr(n, carry):
        sl = pl.ds(pl.multiple_of(n * c, c), c)
        kf = k_ref[0, sl, :].astype(F32)
        v = v_ref[0, sl, :]
        sfs_ref[n] = _dot_tn((kf * dkf).astype(BF16), v) * bd
        sbs_ref[n] = _dot_tn((kf * dkb).astype(BF16), v) * bd
        return carry

    lax.fori_loop(0, n_chunks, incr, 0, unroll=unroll)

    def scan_f(n, s):
        u = sfs_ref[n]
        sfs_ref[n] = s
        return cdf * s + u

    def scan_b(i, s):
        n = n_chunks - 1 - i
        u = sbs_ref[n]
        sbs_ref[n] = s
        return cdb * s + u

    sfo_ref[0, 0] = lax.fori_loop(0, n_chunks, scan_f, sf0_ref[0, 0])
    sbo_ref[0, 0] = lax.fori_loop(0, n_chunks, scan_b, sb0_ref[0, 0])

    def outp(n, carry):
        sl = pl.ds(pl.multiple_of(n * c, c), c)
        q = q_ref[0, sl, :]
        k = k_ref[0, sl, :]
        v = v_ref[0, sl, :]
        qf = q.astype(F32)
        o = None
        for hh, mk in enumerate((mfirst, msecond)):
            s = _dot_nt(q * mk, k) * dm[hh]
            oh = _dot(s.astype(BF16), v)
            o = oh if o is None else jnp.where(first, o, oh)
        qcat = jnp.concatenate([(qf * dqf).astype(BF16), (qf * dqb).astype(BF16)], axis=1)
        scat = jnp.concatenate([sfs_ref[n], sbs_ref[n]], axis=0).astype(BF16)
        o = o + _dot(qcat, scat)
        mu = head_mean(o)
        dlt = o - mu
        var = head_mean(dlt * dlt)
        on = dlt * lax.rsqrt(var + NORM_EPS) * gn_ref[0]
        y_ref[0, sl, :] = (_silu(g_ref[0, sl, :].astype(F32)) * on).astype(BF16)
        return carry

    lax.fori_loop(0, n_chunks, outp, 0, unroll=unroll)


def _retention(proj, lgt, gn, sf0, sb0):
    b, seq, _ = proj.shape
    col = lambda off: pl.BlockSpec((1, seq, LANES), lambda bi, p: (bi, 0, off + p))
    st = pl.BlockSpec((1, 1, LANES, LANES), lambda bi, p: (bi, p, 0, 0))
    return pl.pallas_call(
        functools.partial(_ret_kernel, seq=seq),
        grid=(b, N_PAIRS),
        in_specs=[col(CB_RQ), col(CB_RK), col(CB_RV), col(CB_RG),
                  pl.BlockSpec((1, 2, LANES), lambda bi, p: (p, 0, 0)),
                  pl.BlockSpec((1, 1, LANES), lambda bi, p: (p, 0, 0)),
                  st, st],
        out_specs=[pl.BlockSpec((1, seq, LANES), lambda bi, p: (bi, 0, p)), st, st],
        out_shape=[jax.ShapeDtypeStruct((b, seq, D_RET), BF16),
                   jax.ShapeDtypeStruct((b, N_PAIRS, LANES, LANES), F32),
                   jax.ShapeDtypeStruct((b, N_PAIRS, LANES, LANES), F32)],
        scratch_shapes=[pltpu.VMEM((seq // RET_CHUNK, LANES, LANES), F32),
                        pltpu.VMEM((seq // RET_CHUNK, LANES, LANES), F32)],
        compiler_params=_cp(("parallel", "parallel")),
        name="retention_s%d" % seq,
    )(proj, proj, proj, proj, lgt, gn, sf0, sb0)


def _conv_kernel(b_ref, c_ref, x_ref, w_ref, y_ref, *, seq):
    u = c_ref[0].astype(F32) * x_ref[0].astype(F32)
    row = lax.broadcasted_iota(jnp.int32, (seq, 1), 0)
    prev = jnp.where(row == 0, 0.0, pltpu.roll(u, 1, 0))
    nxt = jnp.where(row == seq - 1, 0.0, pltpu.roll(u, seq - 1, 0))
    y = w_ref[0:1, :] * prev + w_ref[1:2, :] * u + w_ref[2:3, :] * nxt
    y_ref[0] = (b_ref[0].astype(F32) * y).astype(BF16)


def _short_conv(proj, conv_w):
    b, seq, _ = proj.shape
    nblk = D_CONV // LANES
    col = lambda off: pl.BlockSpec((1, seq, LANES), lambda bi, j: (bi, 0, off + j))
    return pl.pallas_call(
        functools.partial(_conv_kernel, seq=seq),
        grid=(b, nblk),
        in_specs=[col(CB_CB), col(CB_CC), col(CB_CX),
                  pl.BlockSpec((3, LANES), lambda bi, j: (0, j))],
        out_specs=pl.BlockSpec((1, seq, LANES), lambda bi, j: (bi, 0, j)),
        out_shape=jax.ShapeDtypeStruct((b, seq, D_CONV), BF16),
        compiler_params=_cp(("parallel", "parallel")),
        name="short_conv_s%d" % seq,
    )(proj, proj, proj, conv_w)


def _na_bias_plan(rows):
    plan = np.full((3, NA_QROWS, NA_KROWS), -1, np.int64)
    for cl, rb in enumerate((0, NA_QROWS, rows - NA_QROWS)):
        ws = int(np.clip(rb - NA_ROWS // 2, 0, rows - NA_KROWS))
        for i in range(NA_QROWS):
            r = rb + i
            r0 = int(np.clip(r - NA_ROWS // 2, 0, rows - NA_ROWS))
            for j in range(NA_KROWS):
                kr = ws + j
                if r0 <= kr < r0 + NA_ROWS:
                    plan[cl, i, j] = kr - r + (NA_ROWS - 1)
    return plan


def _na_bias_kernel(rpb_ref, o_ref, *, plan):
    w = GRID_W
    ndr, ndc = 2 * NA_ROWS - 1, 2 * NA_COLS - 1
    base = pl.program_id(0) * (ndr * ndc)
    c = lax.broadcasted_iota(jnp.int32, (w, w), 0)
    kc = lax.broadcasted_iota(jnp.int32, (w, w), 1)
    dcidx = jnp.clip(kc - c, -(NA_COLS - 1), NA_COLS - 1) + (NA_COLS - 1)
    c0 = jnp.clip(c - NA_COLS // 2, 0, w - NA_COLS)
    col_in = jnp.logical_and(kc >= c0, kc < c0 + NA_COLS)
    neg = jnp.full((w, w), NEG_BIG, F32)
    tiles = []
    for dr in range(ndr):
        t = neg
        for dcv in range(ndc):
            t = jnp.where(dcidx == dcv, rpb_ref[base + dr * ndc + dcv], t)
        tiles.append(jnp.where(col_in, t * LOG2E, NEG_BIG))
    for cl in range(plan.shape[0]):
        for i in range(plan.shape[1]):
            for j in range(0, plan.shape[2], 2):
                pair = [tiles[int(d)] if d >= 0 else neg for d in plan[cl, i, j:j + 2]]
                o_ref[0, cl, i * w:(i + 1) * w, j * w:(j + 2) * w] = jnp.concatenate(pair, axis=1)


def _na_bias(rpb, rows):
    h = rpb.shape[0]
    tq, tk = NA_QROWS * GRID_W, NA_KROWS * GRID_W
    return pl.pallas_call(
        functools.partial(_na_bias_kernel, plan=_na_bias_plan(rows)),
        grid=(h,),
        in_specs=[pl.BlockSpec(memory_space=pltpu.SMEM)],
        out_specs=pl.BlockSpec((1, 3, tq, tk), lambda hh: (hh, 0, 0, 0)),
        out_shape=jax.ShapeDtypeStruct((h, 3, tq, tk), F32),
        compiler_params=_cp(("parallel",)),
        name="na_bias_expand",
    )(rpb.reshape(-1))


def _na_kernel(q_ref, k_ref, v_ref, kc_ref, vc_ref, bias_ref, o_ref, v1_ref, vc1_ref, *, rows):
    qb = pl.program_id(1)
    lane = lax.broadcasted_iota(jnp.int32, (1, LANES), 1)
    first = lane < HEAD_DIM

    @pl.when(qb == 0)
    def _():
        one = jnp.ones((1, LANES), BF16)
        for p in range(N_PAIRS):
            cols = slice(p * LANES, (p + 1) * LANES)
            v1_ref[2 * p] = jnp.where(first, v_ref[0, :, cols], one)
            v1_ref[2 * p + 1] = jnp.where(first, one, v_ref[0, :, cols])
            vc1_ref[2 * p] = jnp.where(first, vc_ref[0, :, cols], one)
            vc1_ref[2 * p + 1] = jnp.where(first, one, vc_ref[0, :, cols])

    ws = jnp.clip(qb * NA_QROWS - NA_ROWS // 2, 0, rows - NA_KROWS)
    sl = pl.ds(pl.multiple_of(ws * GRID_W, GRID_W), NA_KROWS * GRID_W)
    for p in range(N_PAIRS):
        cols = slice(p * LANES, (p + 1) * LANES)
        q = q_ref[0, :, cols]
        kw = k_ref[0, sl, cols]
        kc = kc_ref[0, :, cols]
        pv = []
        for hh in range(2):
            mk = (first if hh == 0 else jnp.logical_not(first)).astype(BF16)
            qh = q * mk
            s_loc = _dot_nt(qh, kw) + bias_ref[2 * p + hh, 0]
            s_ctx = _dot_nt(qh, kc)
            m = jnp.maximum(jnp.max(s_loc, axis=-1, keepdims=True), jnp.max(s_ctx, axis=-1, keepdims=True))
            p_loc = jnp.exp2(s_loc - m).astype(BF16)
            p_ctx = jnp.exp2(s_ctx - m).astype(BF16)
            pv.append(_dot(p_loc, v1_ref[2 * p + hh, sl, :]) + _dot(p_ctx, vc1_ref[2 * p + hh]))
        num = jnp.where(first, pv[0], pv[1])
        den = jnp.where(first, pltpu.roll(pv[0], HEAD_DIM, 1), pltpu.roll(pv[1], HEAD_DIM, 1))
        o_ref[0, :, cols] = (num / den).astype(BF16)


def _na(proj, proj_ctx, bias):
    b, seq, _ = proj.shape
    ctx_len = proj_ctx.shape[1]
    rows = seq // GRID_W
    nqb = rows // NA_QROWS
    tq = NA_QROWS * GRID_W
    tk = NA_KROWS * GRID_W
    nh = 2 * N_PAIRS
    cq, ck, cv = (CB_NQ * LANES // D_NA, CB_NK * LANES // D_NA, CB_NV * LANES // D_NA)

    def cls(qb):
        return jnp.where(qb == 0, 0, jnp.where(qb == nqb - 1, 2, 1))

    return pl.pallas_call(
        functools.partial(_na_kernel, rows=rows),
        grid=(b, nqb),
        in_specs=[pl.BlockSpec((1, tq, D_NA), lambda bi, qb: (bi, qb, cq)),
                  pl.BlockSpec((1, seq, D_NA), lambda bi, qb: (bi, 0, ck)),
                  pl.BlockSpec((1, seq, D_NA), lambda bi, qb: (bi, 0, cv)),
                  pl.BlockSpec((1, ctx_len, D_NA), lambda bi, qb: (bi, 0, ck)),
                  pl.BlockSpec((1, ctx_len, D_NA), lambda bi, qb: (bi, 0, cv)),
                  pl.BlockSpec((nh, 1, tq, tk), lambda bi, qb: (0, cls(qb), 0, 0))],
        out_specs=pl.BlockSpec((1, tq, D_NA), lambda bi, qb: (bi, qb, 0)),
        out_shape=jax.ShapeDtypeStruct((b, seq, D_NA), BF16),
        scratch_shapes=[pltpu.VMEM((nh, seq, LANES), BF16), pltpu.VMEM((nh, ctx_len, LANES), BF16)],
        compiler_params=_cp(("parallel", "arbitrary")),
        name="na_attn",
    )(proj, proj, proj, proj_ctx, proj_ctx, bias)


def _ctx_attn_kernel(q_ref, k_ref, v_ref, o_ref):
    q = q_ref[0]
    k = k_ref[0]
    v = v_ref[0]
    lane = lax.broadcasted_iota(jnp.int32, (1, LANES), 1)
    first = lane < HEAD_DIM
    o = None
    for hh in range(2):
        mk = (first if hh == 0 else jnp.logical_not(first)).astype(BF16)
        s = _dot_nt(q * mk, k)
        m = jnp.max(s, axis=-1, keepdims=True)
        p = jnp.exp(s - m)
        l = jnp.sum(p, axis=-1, keepdims=True)
        oh = _dot(p.astype(BF16), v) / l
        o = oh if o is None else jnp.where(first, o, oh)
    o_ref[0] = o.astype(BF16)


def _ctx_attn(proj_ctx):
    b, ctx_len, _ = proj_ctx.shape
    col = lambda off: pl.BlockSpec((1, ctx_len, LANES), lambda bi, p: (bi, 0, off + p))
    return pl.pallas_call(
        _ctx_attn_kernel,
        grid=(b, N_PAIRS),
        in_specs=[col(CB_NQ), col(CB_NK), col(CB_NV)],
        out_specs=pl.BlockSpec((1, ctx_len, LANES), lambda bi, p: (bi, 0, p)),
        out_shape=jax.ShapeDtypeStruct((b, ctx_len, D_NA), BF16),
        compiler_params=_cp(("parallel", "parallel")),
        name="ctx_attn",
    )(proj_ctx, proj_ctx, proj_ctx)


def _outproj_kernel(*refs, route):
    if route:
        (yr_ref, yc_ref, yn_ref, h_ref, w_ref, g1_ref, n2_ref, sh_ref, sc_ref, rw_ref, rb_ref,
         ho_ref, xn_ref, idx_ref, wt_ref) = refs
    else:
        yr_ref, yc_ref, yn_ref, h_ref, w_ref, g1_ref, n2_ref, sh_ref, sc_ref, ho_ref, xn_ref = refs
    tm = h_ref.shape[0]
    sub = min(OUT_PROJ_SUB, tm)
    if route:
        r_hi, r_lo = _split_bf16(rw_ref[...])
        r_both = jnp.concatenate([r_hi, r_lo], axis=0)
    for r in range(tm // sub):
        rs = slice(r * sub, (r + 1) * sub)
        ycat = jnp.concatenate([yr_ref[rs, :], yc_ref[rs, :], yn_ref[rs, :]], axis=-1)
        h = h_ref[rs, :] + g1_ref[0] * _dot(ycat, w_ref[...])
        ho_ref[rs, :] = h
        a = _norm_mod(h, n2_ref[...], sh_ref[0], sc_ref[0])
        xn_ref[rs, :] = a.astype(xn_ref.dtype)
        if route:
            a_hi, a_lo = _split_bf16(a)
            t_hi = _dot_nt(r_both, a_hi)
            logits = t_hi[:N_EXPERTS] + t_hi[N_EXPERTS:] + _dot_nt(r_hi, a_lo) + rb_ref[...]
            eidx = lax.broadcasted_iota(jnp.int32, logits.shape, 0)
            m1 = jnp.max(logits, axis=0, keepdims=True)
            i1 = jnp.min(jnp.where(logits == m1, eidx, N_EXPERTS), axis=0, keepdims=True)
            rest = jnp.where(eidx == i1, -jnp.inf, logits)
            m2 = jnp.max(rest, axis=0, keepdims=True)
            i2 = jnp.min(jnp.where(rest == m2, eidx, N_EXPERTS), axis=0, keepdims=True)
            e2 = jnp.exp(m2 - m1)
            den = 1.0 + e2
            idx_ref[:, rs] = jnp.concatenate([i1, i2], axis=0)
            wt_ref[:, rs] = jnp.concatenate([1.0 / den, e2 / den], axis=0)


def _out_proj(yr, yc, yn, h2, w_bf, g1, n2g, sh2, sc2, seq, tm, router=None):
    t, d = h2.shape
    tps = seq // tm
    route = router is not None
    row = lambda wdt: pl.BlockSpec((tm, wdt), lambda i: (i, 0))
    mod = pl.BlockSpec((1, 1, d), lambda i: (i // tps, 0, 0))
    in_specs = [row(D_RET), row(D_CONV), row(D_NA), row(d),
                pl.BlockSpec((d, d), lambda i: (0, 0)), mod,
                pl.BlockSpec((1, d), lambda i: (0, 0)), mod, mod]
    args = [yr, yc, yn, h2, w_bf, g1, n2g.reshape(1, d), sh2, sc2]
    out_specs = [row(d), row(d)]
    out_shape = [jax.ShapeDtypeStruct((t, d), F32), jax.ShapeDtypeStruct((t, d), F32 if route else BF16)]
    if route:
        rw, rb = router
        in_specs += [pl.BlockSpec((N_EXPERTS, d), lambda i: (0, 0)),
                     pl.BlockSpec((N_EXPERTS, 1), lambda i: (0, 0))]
        args += [rw.T, rb.reshape(N_EXPERTS, 1)]
        out_specs += [pl.BlockSpec((2, tm), lambda i: (0, i))] * 2
        out_shape += [jax.ShapeDtypeStruct((2, t), jnp.int32), jax.ShapeDtypeStruct((2, t), F32)]
    return pl.pallas_call(
        functools.partial(_outproj_kernel, route=route),
        grid=(t // tm,),
        in_specs=in_specs,
        out_specs=out_specs,
        out_shape=out_shape,
        compiler_params=_cp(("parallel",)),
        name="out_proj_route" if route else "out_proj_t%d" % t,
    )(*args)


def _swiglu_chunks(x, w_in_ref, w_out_ref, ff, n_chunks):
    tf = -(-ff // (n_chunks * MXU_COLS)) * MXU_COLS
    y = None
    for c0 in range(0, ff, tf):
        c1 = min(c0 + tf, ff)
        gate = _dot(x, w_in_ref[:, c0:c1])
        up = _dot(x, w_in_ref[:, ff + c0:ff + c1])
        part = _dot((_silu(gate) * up).astype(BF16), w_out_ref[c0:c1, :])
        y = part if y is None else y + part
    return y


def _ffn_kernel(*refs, ff, n_chunks, cast):
    if cast:
        x_ref, h_ref, wi_ref, wo_ref, g2_ref, c_ref, o_ref, cb_ref = refs
        cb_ref[...] = c_ref[...].astype(BF16)
    else:
        x_ref, h_ref, wi_ref, wo_ref, g2_ref, o_ref = refs
    y = _swiglu_chunks(x_ref[...], wi_ref, wo_ref, ff, n_chunks)
    o_ref[...] = h_ref[...] + g2_ref[0] * y


def _ffn(xn, h2, w_in_bf, w_out_bf, g2, seq, tm, n_chunks, cast=None):
    t, d = h2.shape
    ff = w_out_bf.shape[0]
    tps = seq // tm
    steps = t // tm
    resident = pl.Buffered(1)
    in_specs = [pl.BlockSpec((tm, d), lambda i: (i, 0)),
                pl.BlockSpec((tm, d), lambda i: (i, 0)),
                pl.BlockSpec((d, 2 * ff), lambda i: (0, 0), pipeline_mode=resident),
                pl.BlockSpec((ff, d), lambda i: (0, 0), pipeline_mode=resident),
                pl.BlockSpec((1, 1, d), lambda i: (i // tps, 0, 0))]
    out_specs = [pl.BlockSpec((tm, d), lambda i: (i, 0))]
    out_shape = [jax.ShapeDtypeStruct((t, d), F32)]
    args = [xn, h2, w_in_bf, w_out_bf, g2]
    if cast is not None:
        cr, cc = cast.shape
        assert cr % (steps * 16) == 0
        in_specs.append(pl.BlockSpec((cr // steps, cc), lambda i: (i, 0)))
        out_specs.append(pl.BlockSpec((cr // steps, cc), lambda i: (i, 0)))
        out_shape.append(jax.ShapeDtypeStruct(cast.shape, BF16))
        args.append(cast)
    res = pl.pallas_call(
        functools.partial(_ffn_kernel, ff=ff, n_chunks=n_chunks, cast=cast is not None),
        grid=(steps,),
        in_specs=in_specs,
        out_specs=out_specs,
        out_shape=out_shape,
        compiler_params=_cp(("parallel",)),
        name="ffn_t%d" % t,
    )(*args)
    return res if cast is not None else res[0]


def _dispatch_kernel(s0_ref, s1_ref, zs_ref, zv_ref, x_ref, wo_ref, xs_ref, wob_ref,
                     zbuf_ref, sem, *, rows, n_cast):
    @pl.when(pl.program_id(0) == 0)
    def _():
        zbuf_ref[...] = jnp.zeros_like(zbuf_ref)
        for e in range(2 * N_EXPERTS):
            @pl.when(zv_ref[e] != 0)
            def _():
                dst = xs_ref.at[pl.ds(pl.multiple_of(zs_ref[e], MOE_TS), MOE_TS)]
                pltpu.make_async_copy(zbuf_ref, dst, sem.at[1]).start()
        for e in range(2 * N_EXPERTS):
            @pl.when(zv_ref[e] != 0)
            def _():
                pltpu.make_async_copy(zbuf_ref, xs_ref.at[pl.ds(0, MOE_TS)], sem.at[1]).wait()

    wo_rows = wo_ref.shape[0] // n_cast

    def issue(g):
        for u in range(DMA_UNROLL):
            r = g * DMA_UNROLL + u
            src = x_ref.at[pl.ds(r, 1)]
            pltpu.make_async_copy(src, xs_ref.at[pl.ds(s0_ref[r], 1)], sem.at[0]).start(priority=u % 2)
            pltpu.make_async_copy(src, xs_ref.at[pl.ds(s1_ref[r], 1)], sem.at[0]).start(priority=(u + 1) % 2)

    def issue_and_cast(g, carry):
        rs = pl.ds(pl.multiple_of(g * wo_rows, wo_rows), wo_rows)
        wob_ref[rs, :] = wo_ref[rs, :].astype(BF16)
        issue(g)
        return carry

    def issue_only(g, carry):
        issue(g)
        return carry

    n_iter = rows // DMA_UNROLL
    lax.fori_loop(0, n_cast, issue_and_cast, 0)
    lax.fori_loop(n_cast, n_iter, issue_only, 0)
    for _ in range(2):
        pltpu.make_async_copy(x_ref, xs_ref.at[pl.ds(0, rows)], sem.at[0]).wait()


def _dispatch(xn, slot0, slot1, zplan, n_slots, rows, w_out):
    t, d = xn.shape
    zstart, zvalid = zplan
    steps = t // rows
    n_exp, ff, _ = w_out.shape
    wo2 = w_out.reshape(n_exp * ff, d)
    wo_rows = n_exp * ff // steps
    n_iter = rows // DMA_UNROLL
    cast_rows = 16 * -(-wo_rows // (16 * n_iter))
    n_cast = wo_rows // cast_rows
    assert n_cast * cast_rows == wo_rows and n_cast <= n_iter
    xs, wob = pl.pallas_call(
        functools.partial(_dispatch_kernel, rows=rows, n_cast=n_cast),
        grid=(steps,),
        in_specs=[pl.BlockSpec((rows,), lambda i: (i,), memory_space=pltpu.SMEM),
                  pl.BlockSpec((rows,), lambda i: (i,), memory_space=pltpu.SMEM),
                  pl.BlockSpec((2 * N_EXPERTS,), lambda i: (0,), memory_space=pltpu.SMEM),
                  pl.BlockSpec((2 * N_EXPERTS,), lambda i: (0,), memory_space=pltpu.SMEM),
                  pl.BlockSpec((rows, d), lambda i: (i, 0)),
                  pl.BlockSpec((wo_rows, d), lambda i: (i, 0))],
        out_specs=[pl.BlockSpec(memory_space=pl.ANY),
                   pl.BlockSpec((wo_rows, d), lambda i: (i, 0))],
        out_shape=[jax.ShapeDtypeStruct((n_slots, d), xn.dtype), jax.ShapeDtypeStruct(wo2.shape, BF16)],
        scratch_shapes=[pltpu.VMEM((MOE_TS, d), xn.dtype), pltpu.SemaphoreType.DMA((2,))],
        compiler_params=_cp(("arbitrary",)),
        name="moe_dispatch",
    )(slot0, slot1, zstart, zvalid, xn, wo2)
    return xs, wob.reshape(n_exp, ff, d)


def _expert_kernel(be_ref, nu_ref, x_ref, wi_ref, wo_ref, o_ref, *, ff, n_chunks):
    used = pl.program_id(0) < nu_ref[0]

    @pl.when(used)
    def _():
        o_ref[...] = _swiglu_chunks(x_ref[...].astype(BF16), wi_ref.at[0], wo_ref.at[0], ff, n_chunks)

    @pl.when(jnp.logical_not(used))
    def _():
        o_ref[...] = jnp.zeros_like(o_ref)


def _experts(xs, block_expert, n_used, w_in_bf, w_out_bf, n_chunks):
    n, d = xs.shape
    ff = w_out_bf.shape[1]
    ts = MOE_TS
    grid_spec = pltpu.PrefetchScalarGridSpec(
        num_scalar_prefetch=2,
        grid=(n // ts,),
        in_specs=[pl.BlockSpec((ts, d), lambda j, be, nu: (j, 0)),
                  pl.BlockSpec((1, d, 2 * ff), lambda j, be, nu: (be[j], 0, 0)),
                  pl.BlockSpec((1, ff, d), lambda j, be, nu: (be[j], 0, 0))],
        out_specs=pl.BlockSpec((ts, d), lambda j, be, nu: (j, 0)),
    )
    return pl.pallas_call(
        functools.partial(_expert_kernel, ff=ff, n_chunks=n_chunks),
        grid_spec=grid_spec,
        out_shape=jax.ShapeDtypeStruct((n, d), F32),
        compiler_params=_cp(("arbitrary",)),
        name="moe_experts",
    )(block_expert, n_used, xs, w_in_bf, w_out_bf)


def _combine_kernel(p0_ref, p1_ref, q0_ref, q1_ref, ys_ref, h_ref, w0_ref, w1_ref, g2_ref, fg_ref, o_ref,
                    ya0_ref, ya1_ref, yb0_ref, yb1_ref, sem, *, rows):
    i = pl.program_id(0)
    n = pl.num_programs(0)
    n_iter = rows // COMBINE_GROUP
    bufs = ((ya0_ref, ya1_ref), (yb0_ref, yb1_ref))

    def issue_rows(pa_ref, pb_ref, b, g):
        for u in range(COMBINE_GROUP):
            r = g * COMBINE_GROUP + u
            pltpu.make_async_copy(ys_ref.at[pl.ds(pa_ref[r], 1)], bufs[b][0].at[pl.ds(r, 1)],
                                  sem.at[0, b]).start(priority=u % 2)
            pltpu.make_async_copy(ys_ref.at[pl.ds(pb_ref[r], 1)], bufs[b][1].at[pl.ds(r, 1)],
                                  sem.at[1, b]).start(priority=(u + 1) % 2)

    def finish_rows(b, g):
        rs = pl.ds(pl.multiple_of(g * COMBINE_GROUP, COMBINE_GROUP), COMBINE_GROUP)
        y = w0_ref[rs, :] * bufs[b][0][rs, :] + w1_ref[rs, :] * bufs[b][1][rs, :]
        h = h_ref[rs, :] + g2_ref[0] * y
        ms = jnp.mean(h * h, axis=-1, keepdims=True)
        o_ref[rs, :] = h * lax.rsqrt(ms + NORM_EPS) * fg_ref[...]

    @pl.when(i == 0)
    def _():
        def first(g, carry):
            issue_rows(p0_ref, p1_ref, 0, g)
            return carry
        lax.fori_loop(0, n_iter, first, 0)

    def step(b):
        pltpu.make_async_copy(ys_ref.at[pl.ds(0, rows)], bufs[b][0], sem.at[0, b]).wait()
        pltpu.make_async_copy(ys_ref.at[pl.ds(0, rows)], bufs[b][1], sem.at[1, b]).wait()

        @pl.when(i + 1 < n)
        def _():
            def both(g, carry):
                issue_rows(q0_ref, q1_ref, 1 - b, g)
                finish_rows(b, g)
                return carry
            lax.fori_loop(0, n_iter, both, 0)

        @pl.when(i + 1 == n)
        def _():
            def last(g, carry):
                finish_rows(b, g)
                return carry
            lax.fori_loop(0, n_iter, last, 0)

    for b in range(2):
        pl.when(i % 2 == b)(functools.partial(step, b))


def _combine(pos0, pos1, ys, h2, w0, w1, g2, final_g, seq, rows):
    t, d = h2.shape
    tps = seq // rows
    steps = t // rows
    blk = pl.BlockSpec((rows, d), lambda i: (i, 0))
    col = pl.BlockSpec((rows, 1), lambda i: (i, 0))
    here = pl.BlockSpec((rows,), lambda i: (i,), memory_space=pltpu.SMEM)
    ahead = pl.BlockSpec((rows,), lambda i: (jnp.minimum(i + 1, steps - 1),), memory_space=pltpu.SMEM)
    return pl.pallas_call(
        functools.partial(_combine_kernel, rows=rows),
        grid=(steps,),
        in_specs=[here, here, ahead, ahead,
                  pl.BlockSpec(memory_space=pl.ANY),
                  blk, col, col,
                  pl.BlockSpec((1, 1, d), lambda i: (i // tps, 0, 0)),
                  pl.BlockSpec((1, d), lambda i: (0, 0))],
        out_specs=blk,
        out_shape=jax.ShapeDtypeStruct((t, d), F32),
        scratch_shapes=[pltpu.VMEM((rows, d), F32)] * 4 + [pltpu.SemaphoreType.DMA((2, 2))],
        compiler_params=_cp(("arbitrary",)),
        name="moe_combine_norm",
    )(pos0, pos1, pos0, pos1, ys, h2, w0, w1, g2, final_g.reshape(1, d))


def _route_plan(idx):
    t = idx.shape[1]
    ts = MOE_TS
    n_slots = 2 * t + N_EXPERTS * ts
    e_flat = idx.reshape(-1)
    onehot = (e_flat[:, None] == jnp.arange(N_EXPERTS, dtype=jnp.int32)[None, :]).astype(jnp.int32)
    csum = jnp.cumsum(onehot, axis=0)
    rank = jnp.sum((csum - onehot) * onehot, axis=1)
    counts = csum[-1]
    padded = ((counts + ts - 1) // ts) * ts
    ends = jnp.cumsum(padded)
    offs = ends - padded
    slot = (jnp.sum(onehot * offs[None, :], axis=1) + rank).astype(jnp.int32)
    starts = jnp.arange(n_slots // ts, dtype=jnp.int32) * ts
    block_expert = jnp.minimum(jnp.sum((starts[:, None] >= ends[None, :]).astype(jnp.int32), axis=1),
                               N_EXPERTS - 1).astype(jnp.int32)
    n_used = (ends[-1] // ts).astype(jnp.int32).reshape(1)
    tail = ends[-1] + jnp.arange(N_EXPERTS, dtype=jnp.int32) * ts
    zstart = jnp.concatenate([ends - ts, tail])
    zvalid = jnp.concatenate([padded > 0, tail < n_slots]).astype(jnp.int32)
    zstart = jnp.where(zvalid != 0, zstart, 0).astype(jnp.int32)
    return slot[:t], slot[t:], block_expert, n_used, (zstart, zvalid), n_slots


def kernel(x, c, ctx, c_ctx, ada_w, ada_b, norm1_g, norm2_g, w_in, w_out, ret_decay_logit, ret_gn_g,
           conv_w, na_rpb, ffn_w_in, ffn_w_out, moe_router_w, moe_router_b, moe_w_in, moe_w_out, final_g):
    b, seq, d = x.shape
    ctx_len = ctx.shape[1]
    depth = ada_w.shape[0]
    assert depth == 2, "the final norm is fused into the last (MoE) layer's combine step"
    rows = seq // GRID_W
    t_lat, t_ctx = b * seq, b * ctx_len
    tm_lat = min(1024, seq)
    tm_ctx = ctx_len

    c8 = jnp.zeros((8, d), F32).at[:b].set(c).at[b].set(c_ctx)
    mods = _ada(c8, ada_w, ada_b)
    tables = _rope_tables(seq)

    h = x.reshape(t_lat, d)
    hc = ctx.reshape(t_ctx, d)
    zero_state = jnp.zeros((b, N_PAIRS, LANES, LANES), F32)
    out = None
    for layer in range(depth):
        update_ctx = layer < depth - 1
        m = mods[layer].reshape(8, 6, d)
        lat = [m[:b, i].reshape(b, 1, d) for i in range(6)]
        cx = [jnp.broadcast_to(m[b, i].reshape(1, 1, d), (b, 1, d)) for i in range(6)]
        w_in_bf = w_in[layer].astype(BF16)
        w_out_bf = w_out[layer].astype(BF16)
        lgt = ret_decay_logit[layer].reshape(2, N_PAIRS, LANES // HEAD_DIM)
        lgt = jnp.repeat(lgt, HEAD_DIM, axis=2).transpose(1, 0, 2)
        gn = ret_gn_g[layer].reshape(N_PAIRS, 1, LANES)

        proj_c = _in_proj(hc, norm1_g[layer], cx[0], cx[1], w_in_bf, None, ctx_len, tm_ctx)
        proj_c = proj_c.reshape(b, ctx_len, D_IN_PROJ)
        proj = _in_proj(h, norm1_g[layer], lat[0], lat[1], w_in_bf, tables, seq, tm_lat)
        proj = proj.reshape(b, seq, D_IN_PROJ)

        y_ret_c, s_fwd, s_bwd = _retention(proj_c, lgt, gn, zero_state, zero_state)
        y_ret, _, _ = _retention(proj, lgt, gn, s_fwd, s_bwd)
        y_conv = _short_conv(proj, conv_w[layer])
        y_na = _na(proj, proj_c, _na_bias(na_rpb[layer], rows))

        if layer % 2 == 0:
            h, xn = _out_proj(y_ret.reshape(t_lat, D_RET), y_conv.reshape(t_lat, D_CONV),
                              y_na.reshape(t_lat, D_NA), h, w_out_bf, lat[2], norm2_g[layer],
                              lat[3], lat[4], seq, tm_lat)
            fw_in = ffn_w_in[layer // 2].astype(BF16)
            fw_out = ffn_w_out[layer // 2].astype(BF16)
            mw = moe_w_in[(layer + 1) // 2]
            h, mw_in_bf = _ffn(xn, h, fw_in, fw_out, lat[5], seq, min(512, seq), FFN_CHUNKS,
                               cast=mw.reshape(-1, mw.shape[-1]))
            mw_in_bf = mw_in_bf.reshape(mw.shape)
            if update_ctx:
                y_conv_c = _short_conv(proj_c, conv_w[layer])
                y_na_c = _ctx_attn(proj_c)
                hc, xnc = _out_proj(y_ret_c.reshape(t_ctx, D_RET), y_conv_c.reshape(t_ctx, D_CONV),
                                    y_na_c.reshape(t_ctx, D_NA), hc, w_out_bf, cx[2], norm2_g[layer],
                                    cx[3], cx[4], ctx_len, tm_ctx)
                hc = _ffn(xnc, hc, fw_in, fw_out, cx[5], ctx_len, tm_ctx, FFN_CHUNKS)
        else:
            e = layer // 2
            h, xn, idx, wts = _out_proj(y_ret.reshape(t_lat, D_RET), y_conv.reshape(t_lat, D_CONV),
                                        y_na.reshape(t_lat, D_NA), h, w_out_bf, lat[2], norm2_g[layer],
                                        lat[3], lat[4], seq, tm_lat,
                                        router=(moe_router_w[e], moe_router_b[e]))
            pos0, pos1, block_expert, n_used, zstart, n_slots = _route_plan(idx)
            xs, mw_out_bf = _dispatch(xn, pos0, pos1, zstart, n_slots, min(512, seq), moe_w_out[e])
            ys = _experts(xs, block_expert, n_used, mw_in_bf, mw_out_bf, FFN_CHUNKS)
            out = _combine(pos0, pos1, ys, h, wts[0].reshape(t_lat, 1), wts[1].reshape(t_lat, 1),
                           lat[5], final_g, seq, min(512, seq))
            h = out
    return out.reshape(b, seq, d)
```

```python
import functools

import numpy as np
import jax
import jax.numpy as jnp
from jax import lax
from jax.experimental import pallas as pl
from jax.experimental.pallas import tpu as pltpu

F32 = jnp.float32
BF16 = jnp.bfloat16

LANES = 128
HEAD_DIM = 64
GRID_W = 64
N_CONV_GROUPS = 4
N_RET_HEADS = 6
N_NA_HEADS = 6
D_RET = N_RET_HEADS * HEAD_DIM
D_CONV = N_CONV_GROUPS * HEAD_DIM
D_NA = N_NA_HEADS * HEAD_DIM
D_IN_PROJ = 4 * D_RET + 3 * D_CONV + 3 * D_NA
N_PAIRS = D_RET // LANES
NA_ROWS = 8
NA_COLS = 16
N_EXPERTS = 8
ROPE_BASE = 10000.0
NORM_EPS = 1e-6
NEG_BIG = -1e30
LOG2E = 1.4426950408889634

RET_CHUNK = 256
NA_QROWS = 4
NA_KROWS = 12
MOE_TS = 512
DMA_UNROLL = 8
COMBINE_GROUP = 32
FFN_CHUNKS = 2
IN_PROJ_SUB = 512
OUT_PROJ_SUB = 512
MXU_COLS = 256
IN_PROJ_CHUNK = 3 * MXU_COLS
VMEM_LIMIT = 56 * 1024 * 1024

CB_RQ, CB_RK, CB_RV, CB_RG = 0, 3, 6, 9
CB_CB, CB_CC, CB_CX = 12, 14, 16
CB_NQ, CB_NK, CB_NV = 18, 21, 24


def _cp(sem, vmem=VMEM_LIMIT):
    return pltpu.CompilerParams(dimension_semantics=sem, vmem_limit_bytes=vmem)


def _silu(x):
    return x * (1.0 / (1.0 + jnp.exp(-x)))


def _dot(a, b):
    return jnp.dot(a, b, preferred_element_type=F32)


def _dot_nt(a, b):
    return lax.dot_general(a, b, (((1,), (1,)), ((), ())), preferred_element_type=F32)


def _dot_tn(a, b):
    return lax.dot_general(a, b, (((0,), (0,)), ((), ())), preferred_element_type=F32)


def _split_bf16(x):
    hi = x.astype(BF16)
    lo = (x - hi.astype(F32)).astype(BF16)
    return hi, lo


def _ada_kernel(c_ref, w_ref, b_ref, o_ref):
    x = _silu(c_ref[...]).astype(BF16)
    o_ref[0] = _dot(x, w_ref[0].astype(BF16)) + b_ref[0]


def _ada(c8, ada_w, ada_b):
    depth, d, n = ada_w.shape
    tn = n // 4
    return pl.pallas_call(
        _ada_kernel,
        grid=(depth, n // tn),
        in_specs=[pl.BlockSpec((8, d), lambda l, j: (0, 0)),
                  pl.BlockSpec((1, d, tn), lambda l, j: (l, 0, j)),
                  pl.BlockSpec((1, 1, tn), lambda l, j: (l, 0, j))],
        out_specs=pl.BlockSpec((1, 8, tn), lambda l, j: (l, 0, j)),
        out_shape=jax.ShapeDtypeStruct((depth, 8, n), F32),
        compiler_params=_cp(("parallel", "parallel")),
        name="ada_mod",
    )(c8, ada_w, ada_b.reshape(depth, 1, n))


def _norm_mod(x, g, sh, sc):
    ms = jnp.mean(x * x, axis=-1, keepdims=True)
    y = x * lax.rsqrt(ms + NORM_EPS) * g
    return y * (1.0 + sc) + sh


def _inproj_kernel(*refs, rope):
    if rope:
        h_ref, g_ref, sh_ref, sc_ref, w_ref, cos_ref, sa_ref, sb_ref, o_ref = refs
    else:
        h_ref, g_ref, sh_ref, sc_ref, w_ref, o_ref = refs
    tm = h_ref.shape[0]
    sub = min(IN_PROJ_SUB, tm)
    cw = IN_PROJ_CHUNK
    for r in range(tm // sub):
        rs = slice(r * sub, (r + 1) * sub)
        xn = _norm_mod(h_ref[rs, :], g_ref[...], sh_ref[0], sc_ref[0]).astype(BF16)
        for c0 in range(0, D_IN_PROJ, cw):
            c1 = min(c0 + cw, D_IN_PROJ)
            acc = _dot(xn, w_ref[:, c0:c1])
            for j in range((c1 - c0) // LANES):
                blk = acc[:, j * LANES:(j + 1) * LANES]
                cb = c0 // LANES + j
                if rope and cb < CB_RV:
                    blk = (blk * cos_ref[rs, :] + pltpu.roll(blk, 16, 1) * sa_ref[rs, :]
                           + pltpu.roll(blk, LANES - 16, 1) * sb_ref[rs, :])
                if CB_RK <= cb < CB_RV:
                    blk = blk * (HEAD_DIM ** -0.5)
                elif CB_NQ <= cb < CB_NK:
                    blk = blk * (HEAD_DIM ** -0.5 * (LOG2E if rope else 1.0))
                o_ref[rs, cb * LANES:(cb + 1) * LANES] = blk.astype(BF16)


def _in_proj(h2, g, sh, sc, w_bf, tables, seq, tm):
    t, d = h2.shape
    tiles_per_seq = seq // tm
    rope = tables is not None
    in_specs = [pl.BlockSpec((tm, d), lambda i: (i, 0)),
                pl.BlockSpec((1, d), lambda i: (0, 0)),
                pl.BlockSpec((1, 1, d), lambda i: (i // tiles_per_seq, 0, 0)),
                pl.BlockSpec((1, 1, d), lambda i: (i // tiles_per_seq, 0, 0)),
                pl.BlockSpec((d, D_IN_PROJ), lambda i: (0, 0))]
    args = [h2, g.reshape(1, d), sh, sc, w_bf]
    if rope:
        in_specs += [pl.BlockSpec((tm, LANES), lambda i: (i % tiles_per_seq, 0))] * 3
        args += list(tables)
    return pl.pallas_call(
        functools.partial(_inproj_kernel, rope=rope),
        grid=(t // tm,),
        in_specs=in_specs,
        out_specs=pl.BlockSpec((tm, D_IN_PROJ), lambda i: (i, 0)),
        out_shape=jax.ShapeDtypeStruct((t, D_IN_PROJ), BF16),
        compiler_params=_cp(("parallel",)),
        name="in_proj_rope" if rope else "in_proj_ctx",
    )(*args)


def _rope_tables(seq):
    t = np.arange(seq)
    row = (t // GRID_W).astype(np.float32)
    col = (t % GRID_W).astype(np.float32)
    n_freq = HEAD_DIM // 4
    inv_freq = (ROPE_BASE ** (-np.arange(n_freq, dtype=np.float32) / n_freq)).astype(np.float32)
    ang_r = row[:, None] * inv_freq
    ang_c = col[:, None] * inv_freq
    cos_h = np.concatenate([np.cos(ang_r), np.cos(ang_r), np.cos(ang_c), np.cos(ang_c)], axis=1)
    sin_h = np.concatenate([np.sin(ang_r), np.sin(ang_r), np.sin(ang_c), np.sin(ang_c)], axis=1)
    lane = np.arange(HEAD_DIM)
    second = (lane % 32) >= 16
    sa = np.where(second[None, :], sin_h, 0.0)
    sb = np.where(second[None, :], 0.0, -sin_h)
    tile2 = lambda a: jnp.asarray(np.concatenate([a, a], axis=1), F32)
    return tile2(cos_h), tile2(sa), tile2(sb)


def _ret_kernel(q_ref, k_ref, v_ref, g_ref, lgt_ref, gn_ref, sf0_ref, sb0_ref,
                y_ref, sfo_ref, sbo_ref, sfs_ref, sbs_ref, *, seq):
    c = RET_CHUNK
    n_chunks = seq // c
    lg = jax.nn.log_sigmoid(lgt_ref[0])
    lgf, lgb = lg[0:1, :], lg[1:2, :]
    pos = lax.broadcasted_iota(jnp.int32, (c, 1), 0).astype(F32)
    dkf = jnp.exp(lgf * (c - 1.0 - pos))
    dkb = jnp.exp(lgb * pos)
    dqf = jnp.exp(lgf * (pos + 1.0))
    dqb = jnp.exp(lgb * (c - pos))
    cdf = jnp.exp(lgf * float(c))
    cdb = jnp.exp(lgb * float(c))
    lane = lax.broadcasted_iota(jnp.int32, (1, LANES), 1)
    first = lane < HEAD_DIM
    ri = lax.broadcasted_iota(jnp.int32, (LANES, LANES), 0)
    ci = lax.broadcasted_iota(jnp.int32, (LANES, LANES), 1)
    same = (ri < HEAD_DIM) == (ci < HEAD_DIM)
    bd = same.astype(F32)

    def head_mean(x):
        a = jnp.sum(jnp.where(first, x, 0.0), axis=-1, keepdims=True)
        b = jnp.sum(jnp.where(first, 0.0, x), axis=-1, keepdims=True)
        return jnp.where(first, a, b) * (1.0 / HEAD_DIM)

    ii = lax.broadcasted_iota(jnp.int32, (c, c), 0)
    jj = lax.broadcasted_iota(jnp.int32, (c, c), 1)
    dif = (ii - jj).astype(F32)

    def decay_mask(h0):
        lf = lgf[:, h0:h0 + 1]
        lb = lgb[:, h0:h0 + 1]
        return jnp.where(dif > 0, jnp.exp(lf * jnp.maximum(dif, 0.0)),
                         jnp.where(dif < 0, jnp.exp(lb * jnp.maximum(-dif, 0.0)), 2.0))

    dm = (decay_mask(0), decay_mask(HEAD_DIM))
    mfirst = first.astype(BF16)
    msecond = (1.0 - first.astype(F32)).astype(BF16)

    unroll = min(8, n_chunks)

    def incr(n, carry):
        sl = pl.ds(pl.multiple_of(n * c, c), c)
        kf = k_ref[0, sl, :].astype(F32)
        v = v_ref[0, sl, :]
        sfs_ref[n] = _dot_tn((kf * dkf).astype(BF16), v) * bd
        sbs_ref[n] = _dot_tn((kf * dkb).astype(BF16), v) * bd
        return carry

    lax.fori_loop(0, n_chunks, incr, 0, unroll=unroll)

    def scan_f(n, s):
        u = sfs_ref[n]
        sfs_ref[n] = s
        return cdf * s + u

    def scan_b(i, s):
        n = n_chunks - 1 - i
        u = sbs_ref[n]
        sbs_ref[n] = s
        return cdb * s + u

    sfo_ref[0, 0] = lax.fori_loop(0, n_chunks, scan_f, sf0_ref[0, 0])
    sbo_ref[0, 0] = lax.fori_loop(0, n_chunks, scan_b, sb0_ref[0, 0])

    def outp(n, carry):
        sl = pl.ds(pl.multiple_of(n * c, c), c)
        q = q_ref[0, sl, :]
        k = k_ref[0, sl, :]
        v = v_ref[0, sl, :]
        qf = q.astype(F32)
        o = None
        for hh, mk in enumerate((mfirst, msecond)):
            s = _dot_nt(q * mk, k) * dm[hh]
            oh = _dot(s.astype(BF16), v)
            o = oh if o is None else jnp.where(first, o, oh)
        qcat = jnp.concatenate([(qf * dqf).astype(BF16), (qf * dqb).astype(BF16)], axis=1)
        scat = jnp.concatenate([sfs_ref[n], sbs_ref[n]], axis=0).astype(BF16)
        o = o + _dot(qcat, scat)
        mu = head_mean(o)
        dlt = o - mu
        var = head_mean(dlt * dlt)
        on = dlt * lax.rsqrt(var + NORM_EPS) * gn_ref[0]
        y_ref[0, sl, :] = (_silu(g_ref[0, sl, :].astype(F32)) * on).astype(BF16)
        return carry

    lax.fori_loop(0, n_chunks, outp, 0, unroll=unroll)


def _retention(proj, lgt, gn, sf0, sb0):
    b, seq, _ = proj.shape
    col = lambda off: pl.BlockSpec((1, seq, LANES), lambda bi, p: (bi, 0, off + p))
    st = pl.BlockSpec((1, 1, LANES, LANES), lambda bi, p: (bi, p, 0, 0))
    return pl.pallas_call(
        functools.partial(_ret_kernel, seq=seq),
        grid=(b, N_PAIRS),
        in_specs=[col(CB_RQ), col(CB_RK), col(CB_RV), col(CB_RG),
                  pl.BlockSpec((1, 2, LANES), lambda bi, p: (p, 0, 0)),
                  pl.BlockSpec((1, 1, LANES), lambda bi, p: (p, 0, 0)),
                  st, st],
        out_specs=[pl.BlockSpec((1, seq, LANES), lambda bi, p: (bi, 0, p)), st, st],
        out_shape=[jax.ShapeDtypeStruct((b, seq, D_RET), BF16),
                   jax.ShapeDtypeStruct((b, N_PAIRS, LANES, LANES), F32),
                   jax.ShapeDtypeStruct((b, N_PAIRS, LANES, LANES), F32)],
        scratch_shapes=[pltpu.VMEM((seq // RET_CHUNK, LANES, LANES), F32),
                        pltpu.VMEM((seq // RET_CHUNK, LANES, LANES), F32)],
        compiler_params=_cp(("parallel", "parallel")),
        name="retention_s%d" % seq,
    )(proj, proj, proj, proj, lgt, gn, sf0, sb0)


def _conv_kernel(b_ref, c_ref, x_ref, w_ref, y_ref, *, seq):
    u = c_ref[0].astype(F32) * x_ref[0].astype(F32)
    row = lax.broadcasted_iota(jnp.int32, (seq, 1), 0)
    prev = jnp.where(row == 0, 0.0, pltpu.roll(u, 1, 0))
    nxt = jnp.where(row == seq - 1, 0.0, pltpu.roll(u, seq - 1, 0))
    y = w_ref[0:1, :] * prev + w_ref[1:2, :] * u + w_ref[2:3, :] * nxt
    y_ref[0] = (b_ref[0].astype(F32) * y).astype(BF16)


def _short_conv(proj, conv_w):
    b, seq, _ = proj.shape
    nblk = D_CONV // LANES
    col = lambda off: pl.BlockSpec((1, seq, LANES), lambda bi, j: (bi, 0, off + j))
    return pl.pallas_call(
        functools.partial(_conv_kernel, seq=seq),
        grid=(b, nblk),
        in_specs=[col(CB_CB), col(CB_CC), col(CB_CX),
                  pl.BlockSpec((3, LANES), lambda bi, j: (0, j))],
        out_specs=pl.BlockSpec((1, seq, LANES), lambda bi, j: (bi, 0, j)),
        out_shape=jax.ShapeDtypeStruct((b, seq, D_CONV), BF16),
        compiler_params=_cp(("parallel", "parallel")),
        name="short_conv_s%d" % seq,
    )(proj, proj, proj, conv_w)


def _na_bias_plan(rows):
    plan = np.full((3, NA_QROWS, NA_KROWS), -1, np.int64)
    for cl, rb in enumerate((0, NA_QROWS, rows - NA_QROWS)):
        ws = int(np.clip(rb - NA_ROWS // 2, 0, rows - NA_KROWS))
        for i in range(NA_QROWS):
            r = rb + i
            r0 = int(np.clip(r - NA_ROWS // 2, 0, rows - NA_ROWS))
            for j in range(NA_KROWS):
                kr = ws + j
                if r0 <= kr < r0 + NA_ROWS:
                    plan[cl, i, j] = kr - r + (NA_ROWS - 1)
    return plan


def _na_bias_kernel(rpb_ref, o_ref, *, plan):
    w = GRID_W
    ndr, ndc = 2 * NA_ROWS - 1, 2 * NA_COLS - 1
    base = pl.program_id(0) * (ndr * ndc)
    c = lax.broadcasted_iota(jnp.int32, (w, w), 0)
    kc = lax.broadcasted_iota(jnp.int32, (w, w), 1)
    dcidx = jnp.clip(kc - c, -(NA_COLS - 1), NA_COLS - 1) + (NA_COLS - 1)
    c0 = jnp.clip(c - NA_COLS // 2, 0, w - NA_COLS)
    col_in = jnp.logical_and(kc >= c0, kc < c0 + NA_COLS)
    neg = jnp.full((w, w), NEG_BIG, F32)
    tiles = []
    for dr in range(ndr):
        t = neg
        for dcv in range(ndc):
            t = jnp.where(dcidx == dcv, rpb_ref[base + dr * ndc + dcv], t)
        tiles.append(jnp.where(col_in, t * LOG2E, NEG_BIG))
    for cl in range(plan.shape[0]):
        for i in range(plan.shape[1]):
            for j in range(0, plan.shape[2], 2):
                pair = [tiles[int(d)] if d >= 0 else neg for d in plan[cl, i, j:j + 2]]
                o_ref[0, cl, i * w:(i + 1) * w, j * w:(j + 2) * w] = jnp.concatenate(pair, axis=1)


def _na_bias(rpb, rows):
    h = rpb.shape[0]
    tq, tk = NA_QROWS * GRID_W, NA_KROWS * GRID_W
    return pl.pallas_call(
        functools.partial(_na_bias_kernel, plan=_na_bias_plan(rows)),
        grid=(h,),
        in_specs=[pl.BlockSpec(memory_space=pltpu.SMEM)],
        out_specs=pl.BlockSpec((1, 3, tq, tk), lambda hh: (hh, 0, 0, 0)),
        out_shape=jax.ShapeDtypeStruct((h, 3, tq, tk), F32),
        compiler_params=_cp(("parallel",)),
        name="na_bias_expand",
    )(rpb.reshape(-1))


def _na_kernel(q_ref, k_ref, v_ref, kc_ref, vc_ref, bias_ref, o_ref, v1_ref, vc1_ref, *, rows):
    qb = pl.program_id(1)
    lane = lax.broadcasted_iota(jnp.int32, (1, LANES), 1)
    first = lane < HEAD_DIM

    @pl.when(qb == 0)
    def _():
        one = jnp.ones((1, LANES), BF16)
        for p in range(N_PAIRS):
            cols = slice(p * LANES, (p + 1) * LANES)
            v1_ref[2 * p] = jnp.where(first, v_ref[0, :, cols], one)
            v1_ref[2 * p + 1] = jnp.where(first, one, v_ref[0, :, cols])
            vc1_ref[2 * p] = jnp.where(first, vc_ref[0, :, cols], one)
            vc1_ref[2 * p + 1] = jnp.where(first, one, vc_ref[0, :, cols])

    ws = jnp.clip(qb * NA_QROWS - NA_ROWS // 2, 0, rows - NA_KROWS)
    sl = pl.ds(pl.multiple_of(ws * GRID_W, GRID_W), NA_KROWS * GRID_W)
    for p in range(N_PAIRS):
        cols = slice(p * LANES, (p + 1) * LANES)
        q = q_ref[0, :, cols]
        kw = k_ref[0, sl, cols]
        kc = kc_ref[0, :, cols]
        pv = []
        for hh in range(2):
            mk = (first if hh == 0 else jnp.logical_not(first)).astype(BF16)
            qh = q * mk
            s_loc = _dot_nt(qh, kw) + bias_ref[2 * p + hh, 0]
            s_ctx = _dot_nt(qh, kc)
            m = jnp.maximum(jnp.max(s_loc, axis=-1, keepdims=True), jnp.max(s_ctx, axis=-1, keepdims=True))
            p_loc = jnp.exp2(s_loc - m).astype(BF16)
            p_ctx = jnp.exp2(s_ctx - m).astype(BF16)
            pv.append(_dot(p_loc, v1_ref[2 * p + hh, sl, :]) + _dot(p_ctx, vc1_ref[2 * p + hh]))
        num = jnp.where(first, pv[0], pv[1])
        den = jnp.where(first, pltpu.roll(pv[0], HEAD_DIM, 1), pltpu.roll(pv[1], HEAD_DIM, 1))
        o_ref[0, :, cols] = (num / den).astype(BF16)


def _na(proj, proj_ctx, bias):
    b, seq, _ = proj.shape
    ctx_len = proj_ctx.shape[1]
    rows = seq // GRID_W
    nqb = rows // NA_QROWS
    tq = NA_QROWS * GRID_W
    tk = NA_KROWS * GRID_W
    nh = 2 * N_PAIRS
    cq, ck, cv = (CB_NQ * LANES // D_NA, CB_NK * LANES // D_NA, CB_NV * LANES // D_NA)

    def cls(qb):
        return jnp.where(qb == 0, 0, jnp.where(qb == nqb - 1, 2, 1))

    return pl.pallas_call(
        functools.partial(_na_kernel, rows=rows),
        grid=(b, nqb),
        in_specs=[pl.BlockSpec((1, tq, D_NA), lambda bi, qb: (bi, qb, cq)),
                  pl.BlockSpec((1, seq, D_NA), lambda bi, qb: (bi, 0, ck)),
                  pl.BlockSpec((1, seq, D_NA), lambda bi, qb: (bi, 0, cv)),
                  pl.BlockSpec((1, ctx_len, D_NA), lambda bi, qb: (bi, 0, ck)),
                  pl.BlockSpec((1, ctx_len, D_NA), lambda bi, qb: (bi, 0, cv)),
                  pl.BlockSpec((nh, 1, tq, tk), lambda bi, qb: (0, cls(qb), 0, 0))],
        out_specs=pl.BlockSpec((1, tq, D_NA), lambda bi, qb: (bi, qb, 0)),
        out_shape=jax.ShapeDtypeStruct((b, seq, D_NA), BF16),
        scratch_shapes=[pltpu.VMEM((nh, seq, LANES), BF16), pltpu.VMEM((nh, ctx_len, LANES), BF16)],
        compiler_params=_cp(("parallel", "arbitrary")),
        name="na_attn",
    )(proj, proj, proj, proj_ctx, proj_ctx, bias)


def _ctx_attn_kernel(q_ref, k_ref, v_ref, o_ref):
    q = q_ref[0]
    k = k_ref[0]
    v = v_ref[0]
    lane = lax.broadcasted_iota(jnp.int32, (1, LANES), 1)
    first = lane < HEAD_DIM
    o = None
    for hh in range(2):
        mk = (first if hh == 0 else jnp.logical_not(first)).astype(BF16)
        s = _dot_nt(q * mk, k)
        m = jnp.max(s, axis=-1, keepdims=True)
        p = jnp.exp(s - m)
        l = jnp.sum(p, axis=-1, keepdims=True)
        oh = _dot(p.astype(BF16), v) / l
        o = oh if o is None else jnp.where(first, o, oh)
    o_ref[0] = o.astype(BF16)


def _ctx_attn(proj_ctx):
    b, ctx_len, _ = proj_ctx.shape
    col = lambda off: pl.BlockSpec((1, ctx_len, LANES), lambda bi, p: (bi, 0, off + p))
    return pl.pallas_call(
        _ctx_attn_kernel,
        grid=(b, N_PAIRS),
        in_specs=[col(CB_NQ), col(CB_NK), col(CB_NV)],
        out_specs=pl.BlockSpec((1, ctx_len, LANES), lambda bi, p: (bi, 0, p)),
        out_shape=jax.ShapeDtypeStruct((b, ctx_len, D_NA), BF16),
        compiler_params=_cp(("parallel", "parallel")),
        name="ctx_attn",
    )(proj_ctx, proj_ctx, proj_ctx)


def _outproj_kernel(*refs, route):
    if route:
        (yr_ref, yc_ref, yn_ref, h_ref, w_ref, g1_ref, n2_ref, sh_ref, sc_ref, rw_ref, rb_ref,
         ho_ref, xn_ref, idx_ref, wt_ref) = refs
    else:
        yr_ref, yc_ref, yn_ref, h_ref, w_ref, g1_ref, n2_ref, sh_ref, sc_ref, ho_ref, xn_ref = refs
    tm = h_ref.shape[0]
    sub = min(OUT_PROJ_SUB, tm)
    if route:
        r_hi, r_lo = _split_bf16(rw_ref[...])
        r_both = jnp.concatenate([r_hi, r_lo], axis=0)
    for r in range(tm // sub):
        rs = slice(r * sub, (r + 1) * sub)
        ycat = jnp.concatenate([yr_ref[rs, :], yc_ref[rs, :], yn_ref[rs, :]], axis=-1)
        h = h_ref[rs, :] + g1_ref[0] * _dot(ycat, w_ref[...])
        ho_ref[rs, :] = h
        a = _norm_mod(h, n2_ref[...], sh_ref[0], sc_ref[0])
        xn_ref[rs, :] = a.astype(xn_ref.dtype)
        if route:
            a_hi, a_lo = _split_bf16(a)
            t_hi = _dot_nt(r_both, a_hi)
            logits = t_hi[:N_EXPERTS] + t_hi[N_EXPERTS:] + _dot_nt(r_hi, a_lo) + rb_ref[...]
            eidx = lax.broadcasted_iota(jnp.int32, logits.shape, 0)
            m1 = jnp.max(logits, axis=0, keepdims=True)
            i1 = jnp.min(jnp.where(logits == m1, eidx, N_EXPERTS), axis=0, keepdims=True)
            rest = jnp.where(eidx == i1, -jnp.inf, logits)
            m2 = jnp.max(rest, axis=0, keepdims=True)
            i2 = jnp.min(jnp.where(rest == m2, eidx, N_EXPERTS), axis=0, keepdims=True)
            e2 = jnp.exp(m2 - m1)
            den = 1.0 + e2
            idx_ref[:, rs] = jnp.concatenate([i1, i2], axis=0)
            wt_ref[:, rs] = jnp.concatenate([1.0 / den, e2 / den], axis=0)


def _out_proj(yr, yc, yn, h2, w_bf, g1, n2g, sh2, sc2, seq, tm, router=None):
    t, d = h2.shape
    tps = seq // tm
    route = router is not None
    row = lambda wdt: pl.BlockSpec((tm, wdt), lambda i: (i, 0))
    mod = pl.BlockSpec((1, 1, d), lambda i: (i // tps, 0, 0))
    in_specs = [row(D_RET), row(D_CONV), row(D_NA), row(d),
                pl.BlockSpec((d, d), lambda i: (0, 0)), mod,
                pl.BlockSpec((1, d), lambda i: (0, 0)), mod, mod]
    args = [yr, yc, yn, h2, w_bf, g1, n2g.reshape(1, d), sh2, sc2]
    out_specs = [row(d), row(d)]
    out_shape = [jax.ShapeDtypeStruct((t, d), F32), jax.ShapeDtypeStruct((t, d), F32 if route else BF16)]
    if route:
        rw, rb = router
        in_specs += [pl.BlockSpec((N_EXPERTS, d), lambda i: (0, 0)),
                     pl.BlockSpec((N_EXPERTS, 1), lambda i: (0, 0))]
        args += [rw.T, rb.reshape(N_EXPERTS, 1)]
        out_specs += [pl.BlockSpec((2, tm), lambda i: (0, i))] * 2
        out_shape += [jax.ShapeDtypeStruct((2, t), jnp.int32), jax.ShapeDtypeStruct((2, t), F32)]
    return pl.pallas_call(
        functools.partial(_outproj_kernel, route=route),
        grid=(t // tm,),
        in_specs=in_specs,
        out_specs=out_specs,
        out_shape=out_shape,
        compiler_params=_cp(("parallel",)),
        name="out_proj_route" if route else "out_proj_t%d" % t,
    )(*args)


def _swiglu_chunks(x, w_in_ref, w_out_ref, ff, n_chunks):
    tf = -(-ff // (n_chunks * MXU_COLS)) * MXU_COLS
    y = None
    for c0 in range(0, ff, tf):
        c1 = min(c0 + tf, ff)
        gate = _dot(x, w_in_ref[:, c0:c1])
        up = _dot(x, w_in_ref[:, ff + c0:ff + c1])
        part = _dot((_silu(gate) * up).astype(BF16), w_out_ref[c0:c1, :])
        y = part if y is None else y + part
    return y


def _ffn_kernel(*refs, ff, n_chunks, cast):
    if cast:
        x_ref, h_ref, wi_ref, wo_ref, g2_ref, c_ref, o_ref, cb_ref = refs
        cb_ref[...] = c_ref[...].astype(BF16)
    else:
        x_ref, h_ref, wi_ref, wo_ref, g2_ref, o_ref = refs
    y = _swiglu_chunks(x_ref[...], wi_ref, wo_ref, ff, n_chunks)
    o_ref[...] = h_ref[...] + g2_ref[0] * y


def _ffn(xn, h2, w_in_bf, w_out_bf, g2, seq, tm, n_chunks, cast=None):
    t, d = h2.shape
    ff = w_out_bf.shape[0]
    tps = seq // tm
    steps = t // tm
    resident = pl.Buffered(1)
    in_specs = [pl.BlockSpec((tm, d), lambda i: (i, 0)),
                pl.BlockSpec((tm, d), lambda i: (i, 0)),
                pl.BlockSpec((d, 2 * ff), lambda i: (0, 0), pipeline_mode=resident),
                pl.BlockSpec((ff, d), lambda i: (0, 0), pipeline_mode=resident),
                pl.BlockSpec((1, 1, d), lambda i: (i // tps, 0, 0))]
    out_specs = [pl.BlockSpec((tm, d), lambda i: (i, 0))]
    out_shape = [jax.ShapeDtypeStruct((t, d), F32)]
    args = [xn, h2, w_in_bf, w_out_bf, g2]
    if cast is not None:
        cr, cc = cast.shape
        assert cr % (steps * 16) == 0
        in_specs.append(pl.BlockSpec((cr // steps, cc), lambda i: (i, 0)))
        out_specs.append(pl.BlockSpec((cr // steps, cc), lambda i: (i, 0)))
        out_shape.append(jax.ShapeDtypeStruct(cast.shape, BF16))
        args.append(cast)
    res = pl.pallas_call(
        functools.partial(_ffn_kernel, ff=ff, n_chunks=n_chunks, cast=cast is not None),
        grid=(steps,),
        in_specs=in_specs,
        out_specs=out_specs,
        out_shape=out_shape,
        compiler_params=_cp(("parallel",)),
        name="ffn_t%d" % t,
    )(*args)
    return res if cast is not None else res[0]


def _dispatch_kernel(s0_ref, s1_ref, zs_ref, zv_ref, x_ref, wo_ref, xs_ref, wob_ref,
                     slab_ref, zbuf_ref, sem, *, rows, n_cast):
    @pl.when(pl.program_id(0) == 0)
    def _():
        zbuf_ref[...] = jnp.zeros_like(zbuf_ref)
        for e in range(2 * N_EXPERTS):
            @pl.when(zv_ref[e] != 0)
            def _():
                dst = xs_ref.at[pl.ds(pl.multiple_of(zs_ref[e], MOE_TS), MOE_TS)]
                pltpu.make_async_copy(zbuf_ref, dst, sem.at[1]).start()
        for e in range(2 * N_EXPERTS):
            @pl.when(zv_ref[e] != 0)
            def _():
                pltpu.make_async_copy(zbuf_ref, xs_ref.at[pl.ds(0, MOE_TS)], sem.at[1]).wait()

    wo_rows = wo_ref.shape[0] // n_cast
    nk = slab_ref.shape[1]

    def issue(g):
        rs = pl.ds(pl.multiple_of(g * DMA_UNROLL, DMA_UNROLL), DMA_UNROLL)
        for k in range(nk):
            slab_ref[rs, k, :] = x_ref[rs, k * LANES:(k + 1) * LANES]
        for u in range(DMA_UNROLL):
            r = g * DMA_UNROLL + u
            src = slab_ref.at[r]
            pltpu.make_async_copy(src, xs_ref.at[s0_ref[r]], sem.at[0]).start(priority=u % 2)
            pltpu.make_async_copy(src, xs_ref.at[s1_ref[r]], sem.at[0]).start(priority=(u + 1) % 2)

    def issue_and_cast(g, carry):
        rs = pl.ds(pl.multiple_of(g * wo_rows, wo_rows), wo_rows)
        wob_ref[rs, :] = wo_ref[rs, :].astype(BF16)
        issue(g)
        return carry

    def issue_only(g, carry):
        issue(g)
        return carry

    n_iter = rows // DMA_UNROLL
    lax.fori_loop(0, n_cast, issue_and_cast, 0)
    lax.fori_loop(n_cast, n_iter, issue_only, 0)
    for _ in range(2):
        pltpu.make_async_copy(slab_ref, xs_ref.at[pl.ds(0, rows)], sem.at[0]).wait()


def _dispatch(xn, slot0, slot1, zplan, n_slots, rows, w_out):
    t, d = xn.shape
    slab = (d // LANES, LANES)
    zstart, zvalid = zplan
    steps = t // rows
    n_exp, ff, _ = w_out.shape
    wo2 = w_out.reshape(n_exp * ff, d)
    wo_rows = n_exp * ff // steps
    n_iter = rows // DMA_UNROLL
    cast_rows = 16 * -(-wo_rows // (16 * n_iter))
    n_cast = wo_rows // cast_rows
    assert n_cast * cast_rows == wo_rows and n_cast <= n_iter
    xs, wob = pl.pallas_call(
        functools.partial(_dispatch_kernel, rows=rows, n_cast=n_cast),
        grid=(steps,),
        in_specs=[pl.BlockSpec((rows,), lambda i: (i,), memory_space=pltpu.SMEM),
                  pl.BlockSpec((rows,), lambda i: (i,), memory_space=pltpu.SMEM),
                  pl.BlockSpec((2 * N_EXPERTS,), lambda i: (0,), memory_space=pltpu.SMEM),
                  pl.BlockSpec((2 * N_EXPERTS,), lambda i: (0,), memory_space=pltpu.SMEM),
                  pl.BlockSpec((rows, d), lambda i: (i, 0)),
                  pl.BlockSpec((wo_rows, d), lambda i: (i, 0))],
        out_specs=[pl.BlockSpec(memory_space=pl.ANY),
                   pl.BlockSpec((wo_rows, d), lambda i: (i, 0))],
        out_shape=[jax.ShapeDtypeStruct((n_slots,) + slab, xn.dtype), jax.ShapeDtypeStruct(wo2.shape, BF16)],
        scratch_shapes=[pltpu.VMEM((rows,) + slab, xn.dtype), pltpu.VMEM((MOE_TS,) + slab, xn.dtype),
                        pltpu.SemaphoreType.DMA((2,))],
        compiler_params=_cp(("arbitrary",)),
        name="moe_dispatch",
    )(slot0, slot1, zstart, zvalid, xn, wo2)
    return xs, wob.reshape(n_exp, ff, d)


def _expert_kernel(be_ref, nu_ref, xs_ref, wi_ref, wo_ref, o_ref, xa_ref, xb_ref, sem, *, ff, n_chunks):
    j = pl.program_id(0)
    n = pl.num_programs(0)
    ts, d = xa_ref.shape
    bufs = (xa_ref, xb_ref)

    def copies(blk, b):
        rows = pl.ds(pl.multiple_of(blk * ts, ts), ts)
        return [pltpu.make_async_copy(xs_ref.at[rows, k, :], bufs[b].at[:, k * LANES:(k + 1) * LANES],
                                      sem.at[b]) for k in range(d // LANES)]

    @pl.when(j == 0)
    def _():
        for c in copies(0, 0):
            c.start()

    def step(b):
        @pl.when(j + 1 < n)
        def _():
            for c in copies(j + 1, 1 - b):
                c.start()

        for c in copies(j, b):
            c.wait()

        @pl.when(j < nu_ref[0])
        def _():
            o_ref[...] = _swiglu_chunks(bufs[b][...].astype(BF16), wi_ref.at[0], wo_ref.at[0], ff, n_chunks)

        @pl.when(j >= nu_ref[0])
        def _():
            o_ref[...] = jnp.zeros_like(o_ref)

    for b in range(2):
        pl.when(j % 2 == b)(functools.partial(step, b))


def _experts(xs, block_expert, n_used, w_in_bf, w_out_bf, n_chunks):
    n, nk, _ = xs.shape
    d = nk * LANES
    ff = w_out_bf.shape[1]
    ts = MOE_TS
    grid_spec = pltpu.PrefetchScalarGridSpec(
        num_scalar_prefetch=2,
        grid=(n // ts,),
        in_specs=[pl.BlockSpec(memory_space=pl.ANY),
                  pl.BlockSpec((1, d, 2 * ff), lambda j, be, nu: (be[j], 0, 0)),
                  pl.BlockSpec((1, ff, d), lambda j, be, nu: (be[j], 0, 0))],
        out_specs=pl.BlockSpec((ts, d), lambda j, be, nu: (j, 0)),
        scratch_shapes=[pltpu.VMEM((ts, d), xs.dtype), pltpu.VMEM((ts, d), xs.dtype),
                        pltpu.SemaphoreType.DMA((2,))],
    )
    return pl.pallas_call(
        functools.partial(_expert_kernel, ff=ff, n_chunks=n_chunks),
        grid_spec=grid_spec,
        out_shape=jax.ShapeDtypeStruct((n, d), F32),
        compiler_params=_cp(("arbitrary",)),
        name="moe_experts",
    )(block_expert, n_used, xs, w_in_bf, w_out_bf)


def _combine_kernel(p0_ref, p1_ref, q0_ref, q1_ref, ys_ref, h_ref, w0_ref, w1_ref, g2_ref, fg_ref, o_ref,
                    ya0_ref, ya1_ref, yb0_ref, yb1_ref, sem, *, rows):
    i = pl.program_id(0)
    n = pl.num_programs(0)
    n_iter = rows // COMBINE_GROUP
    bufs = ((ya0_ref, ya1_ref), (yb0_ref, yb1_ref))

    def issue_rows(pa_ref, pb_ref, b, g):
        for u in range(COMBINE_GROUP):
            r = g * COMBINE_GROUP + u
            pltpu.make_async_copy(ys_ref.at[pl.ds(pa_ref[r], 1)], bufs[b][0].at[pl.ds(r, 1)],
                                  sem.at[0, b]).start(priority=u % 2)
            pltpu.make_async_copy(ys_ref.at[pl.ds(pb_ref[r], 1)], bufs[b][1].at[pl.ds(r, 1)],
                                  sem.at[1, b]).start(priority=(u + 1) % 2)

    def finish_rows(b, g):
        rs = pl.ds(pl.multiple_of(g * COMBINE_GROUP, COMBINE_GROUP), COMBINE_GROUP)
        y = w0_ref[rs, :] * bufs[b][0][rs, :] + w1_ref[rs, :] * bufs[b][1][rs, :]
        h = h_ref[rs, :] + g2_ref[0] * y
        ms = jnp.mean(h * h, axis=-1, keepdims=True)
        o_ref[rs, :] = h * lax.rsqrt(ms + NORM_EPS) * fg_ref[...]

    @pl.when(i == 0)
    def _():
        def first(g, carry):
            issue_rows(p0_ref, p1_ref, 0, g)
            return carry
        lax.fori_loop(0, n_iter, first, 0)

    def step(b):
        pltpu.make_async_copy(ys_ref.at[pl.ds(0, rows)], bufs[b][0], sem.at[0, b]).wait()
        pltpu.make_async_copy(ys_ref.at[pl.ds(0, rows)], bufs[b][1], sem.at[1, b]).wait()

        @pl.when(i + 1 < n)
        def _():
            def both(g, carry):
                issue_rows(q0_ref, q1_ref, 1 - b, g)
                finish_rows(b, g)
                return carry
            lax.fori_loop(0, n_iter, both, 0)

        @pl.when(i + 1 == n)
        def _():
            def last(g, carry):
                finish_rows(b, g)
                return carry
            lax.fori_loop(0, n_iter, last, 0)

    for b in range(2):
        pl.when(i % 2 == b)(functools.partial(step, b))


def _combine(pos0, pos1, ys, h2, w0, w1, g2, final_g, seq, rows):
    t, d = h2.shape
    tps = seq // rows
    steps = t // rows
    blk = pl.BlockSpec((rows, d), lambda i: (i, 0))
    col = pl.BlockSpec((rows, 1), lambda i: (i, 0))
    here = pl.BlockSpec((rows,), lambda i: (i,), memory_space=pltpu.SMEM)
    ahead = pl.BlockSpec((rows,), lambda i: (jnp.minimum(i + 1, steps - 1),), memory_space=pltpu.SMEM)
    return pl.pallas_call(
        functools.partial(_combine_kernel, rows=rows),
        grid=(steps,),
        in_specs=[here, here, ahead, ahead,
                  pl.BlockSpec(memory_space=pl.ANY),
                  blk, col, col,
                  pl.BlockSpec((1, 1, d), lambda i: (i // tps, 0, 0)),
                  pl.BlockSpec((1, d), lambda i: (0, 0))],
        out_specs=blk,
        out_shape=jax.ShapeDtypeStruct((t, d), F32),
        scratch_shapes=[pltpu.VMEM((rows, d), F32)] * 4 + [pltpu.SemaphoreType.DMA((2, 2))],
        compiler_params=_cp(("arbitrary",)),
        name="moe_combine_norm",
    )(pos0, pos1, pos0, pos1, ys, h2, w0, w1, g2, final_g.reshape(1, d))


def _route_plan(idx):
    t = idx.shape[1]
    ts = MOE_TS
    n_slots = 2 * t + N_EXPERTS * ts
    e_flat = idx.reshape(-1)
    onehot = (e_flat[:, None] == jnp.arange(N_EXPERTS, dtype=jnp.int32)[None, :]).astype(jnp.int32)
    csum = jnp.cumsum(onehot, axis=0)
    rank = jnp.sum((csum - onehot) * onehot, axis=1)
    counts = csum[-1]
    padded = ((counts + ts - 1) // ts) * ts
    ends = jnp.cumsum(padded)
    offs = ends - padded
    slot = (jnp.sum(onehot * offs[None, :], axis=1) + rank).astype(jnp.int32)
    starts = jnp.arange(n_slots // ts, dtype=jnp.int32) * ts
    block_expert = jnp.minimum(jnp.sum((starts[:, None] >= ends[None, :]).astype(jnp.int32), axis=1),
                               N_EXPERTS - 1).astype(jnp.int32)
    n_used = (ends[-1] // ts).astype(jnp.int32).reshape(1)
    tail = ends[-1] + jnp.arange(N_EXPERTS, dtype=jnp.int32) * ts
    zstart = jnp.concatenate([ends - ts, tail])
    zvalid = jnp.concatenate([padded > 0, tail < n_slots]).astype(jnp.int32)
    zstart = jnp.where(zvalid != 0, zstart, 0).astype(jnp.int32)
    return slot[:t], slot[t:], block_expert, n_used, (zstart, zvalid), n_slots


def kernel(x, c, ctx, c_ctx, ada_w, ada_b, norm1_g, norm2_g, w_in, w_out, ret_decay_logit, ret_gn_g,
           conv_w, na_rpb, ffn_w_in, ffn_w_out, moe_router_w, moe_router_b, moe_w_in, moe_w_out, final_g):
    b, seq, d = x.shape
    ctx_len = ctx.shape[1]
    depth = ada_w.shape[0]
    assert depth == 2, "the final norm is fused into the last (MoE) layer's combine step"
    rows = seq // GRID_W
    t_lat, t_ctx = b * seq, b * ctx_len
    tm_lat = min(1024, seq)
    tm_ctx = ctx_len

    c8 = jnp.zeros((8, d), F32).at[:b].set(c).at[b].set(c_ctx)
    mods = _ada(c8, ada_w, ada_b)
    tables = _rope_tables(seq)

    h = x.reshape(t_lat, d)
    hc = ctx.reshape(t_ctx, d)
    zero_state = jnp.zeros((b, N_PAIRS, LANES, LANES), F32)
    out = None
    for layer in range(depth):
        update_ctx = layer < depth - 1
        m = mods[layer].reshape(8, 6, d)
        lat = [m[:b, i].reshape(b, 1, d) for i in range(6)]
        cx = [jnp.broadcast_to(m[b, i].reshape(1, 1, d), (b, 1, d)) for i in range(6)]
        w_in_bf = w_in[layer].astype(BF16)
        w_out_bf = w_out[layer].astype(BF16)
        lgt = ret_decay_logit[layer].reshape(2, N_PAIRS, LANES // HEAD_DIM)
        lgt = jnp.repeat(lgt, HEAD_DIM, axis=2).transpose(1, 0, 2)
        gn = ret_gn_g[layer].reshape(N_PAIRS, 1, LANES)

        proj_c = _in_proj(hc, norm1_g[layer], cx[0], cx[1], w_in_bf, None, ctx_len, tm_ctx)
        proj_c = proj_c.reshape(b, ctx_len, D_IN_PROJ)
        proj = _in_proj(h, norm1_g[layer], lat[0], lat[1], w_in_bf, tables, seq, tm_lat)
        proj = proj.reshape(b, seq, D_IN_PROJ)

        y_ret_c, s_fwd, s_bwd = _retention(proj_c, lgt, gn, zero_state, zero_state)
        y_ret, _, _ = _retention(proj, lgt, gn, s_fwd, s_bwd)
        y_conv = _short_conv(proj, conv_w[layer])
        y_na = _na(proj, proj_c, _na_bias(na_rpb[layer], rows))

        if layer % 2 == 0:
            h, xn = _out_proj(y_ret.reshape(t_lat, D_RET), y_conv.reshape(t_lat, D_CONV),
                              y_na.reshape(t_lat, D_NA), h, w_out_bf, lat[2], norm2_g[layer],
                              lat[3], lat[4], seq, tm_lat)
            fw_in = ffn_w_in[layer // 2].astype(BF16)
            fw_out = ffn_w_out[layer // 2].astype(BF16)
            mw = moe_w_in[(layer + 1) // 2]
            h, mw_in_bf = _ffn(xn, h, fw_in, fw_out, lat[5], seq, min(512, seq), FFN_CHUNKS,
                               cast=mw.reshape(-1, mw.shape[-1]))
            mw_in_bf = mw_in_bf.reshape(mw.shape)
            if update_ctx:
                y_conv_c = _short_conv(proj_c, conv_w[layer])
                y_na_c = _ctx_attn(proj_c)
                hc, xnc = _out_proj(y_ret_c.reshape(t_ctx, D_RET), y_conv_c.reshape(t_ctx, D_CONV),
                                    y_na_c.reshape(t_ctx, D_NA), hc, w_out_bf, cx[2], norm2_g[layer],
                                    cx[3], cx[4], ctx_len, tm_ctx)
                hc = _ffn(xnc, hc, fw_in, fw_out, cx[5], ctx_len, tm_ctx, FFN_CHUNKS)
        else:
            e = layer // 2
            h, xn, idx, wts = _out_proj(y_ret.reshape(t_lat, D_RET), y_conv.reshape(t_lat, D_CONV),
                                        y_na.reshape(t_lat, D_NA), h, w_out_bf, lat[2], norm2_g[layer],
                                        lat[3], lat[4], seq, tm_lat,
                                        router=(moe_router_w[e], moe_router_b[e]))
            pos0, pos1, block_expert, n_used, zstart, n_slots = _route_plan(idx)
            xs, mw_out_bf = _dispatch(xn, pos0, pos1, zstart, n_slots, min(512, seq), moe_w_out[e])
            ys = _experts(xs, block_expert, n_used, mw_in_bf, mw_out_bf, FFN_CHUNKS)
            out = _combine(pos0, pos1, ys, h, wts[0].reshape(t_lat, 1), wts[1].reshape(t_lat, 1),
                           lat[5], final_g, seq, min(512, seq))
            h = out
    return out.reshape(b, seq, d)
```

```python
import functools

import numpy as np
import jax
import jax.numpy as jnp
from jax import lax
from jax.experimental import pallas as pl
from jax.experimental.pallas import tpu as pltpu

F32 = jnp.float32
BF16 = jnp.bfloat16

LANES = 128
HEAD_DIM = 64
GRID_W = 64
N_CONV_GROUPS = 4
N_RET_HEADS = 6
N_NA_HEADS = 6
D_RET = N_RET_HEADS * HEAD_DIM
D_CONV = N_CONV_GROUPS * HEAD_DIM
D_NA = N_NA_HEADS * HEAD_DIM
D_IN_PROJ = 4 * D_RET + 3 * D_CONV + 3 * D_NA
N_PAIRS = D_RET // LANES
NA_ROWS = 8
NA_COLS = 16
N_EXPERTS = 8
ROPE_BASE = 10000.0
NORM_EPS = 1e-6
NEG_BIG = -1e30
LOG2E = 1.4426950408889634

RET_CHUNK = 256
NA_QROWS = 4
NA_KROWS = 12
MOE_TS = 512
DMA_UNROLL = 8
COMBINE_GROUP = 32
FFN_CHUNKS = 2
IN_PROJ_SUB = 512
OUT_PROJ_SUB = 512
MXU_COLS = 256
IN_PROJ_CHUNK = 3 * MXU_COLS
VMEM_LIMIT = 56 * 1024 * 1024

CB_RQ, CB_RK, CB_RV, CB_RG = 0, 3, 6, 9
CB_CB, CB_CC, CB_CX = 12, 14, 16
CB_NQ, CB_NK, CB_NV = 18, 21, 24


def _cp(sem, vmem=VMEM_LIMIT):
    return pltpu.CompilerParams(dimension_semantics=sem, vmem_limit_bytes=vmem)


def _silu(x):
    return x * (1.0 / (1.0 + jnp.exp(-x)))


def _dot(a, b):
    return jnp.dot(a, b, preferred_element_type=F32)


def _dot_nt(a, b):
    return lax.dot_general(a, b, (((1,), (1,)), ((), ())), preferred_element_type=F32)


def _dot_tn(a, b):
    return lax.dot_general(a, b, (((0,), (0,)), ((), ())), preferred_element_type=F32)


def _split_bf16(x):
    hi = x.astype(BF16)
    lo = (x - hi.astype(F32)).astype(BF16)
    return hi, lo


def _ada_kernel(c_ref, w_ref, b_ref, o_ref):
    x = _silu(c_ref[...]).astype(BF16)
    o_ref[0] = _dot(x, w_ref[0].astype(BF16)) + b_ref[0]


def _ada(c8, ada_w, ada_b):
    depth, d, n = ada_w.shape
    tn = n // 4
    return pl.pallas_call(
        _ada_kernel,
        grid=(depth, n // tn),
        in_specs=[pl.BlockSpec((8, d), lambda l, j: (0, 0)),
                  pl.BlockSpec((1, d, tn), lambda l, j: (l, 0, j)),
                  pl.BlockSpec((1, 1, tn), lambda l, j: (l, 0, j))],
        out_specs=pl.BlockSpec((1, 8, tn), lambda l, j: (l, 0, j)),
        out_shape=jax.ShapeDtypeStruct((depth, 8, n), F32),
        compiler_params=_cp(("parallel", "parallel")),
        name="ada_mod",
    )(c8, ada_w, ada_b.reshape(depth, 1, n))


def _norm_mod(x, g, sh, sc):
    ms = jnp.mean(x * x, axis=-1, keepdims=True)
    y = x * lax.rsqrt(ms + NORM_EPS) * g
    return y * (1.0 + sc) + sh


def _inproj_kernel(*refs, rope):
    if rope:
        h_ref, g_ref, sh_ref, sc_ref, w_ref, cos_ref, sa_ref, sb_ref, o_ref = refs
    else:
        h_ref, g_ref, sh_ref, sc_ref, w_ref, o_ref = refs
    tm = h_ref.shape[0]
    sub = min(IN_PROJ_SUB, tm)
    cw = IN_PROJ_CHUNK
    for r in range(tm // sub):
        rs = slice(r * sub, (r + 1) * sub)
        xn = _norm_mod(h_ref[rs, :], g_ref[...], sh_ref[0], sc_ref[0]).astype(BF16)
        for c0 in range(0, D_IN_PROJ, cw):
            c1 = min(c0 + cw, D_IN_PROJ)
            acc = _dot(xn, w_ref[:, c0:c1])
            for j in range((c1 - c0) // LANES):
                blk = acc[:, j * LANES:(j + 1) * LANES]
                cb = c0 // LANES + j
                if rope and cb < CB_RV:
                    blk = (blk * cos_ref[rs, :] + pltpu.roll(blk, 16, 1) * sa_ref[rs, :]
                           + pltpu.roll(blk, LANES - 16, 1) * sb_ref[rs, :])
                if CB_RK <= cb < CB_RV:
                    blk = blk * (HEAD_DIM ** -0.5)
                elif CB_NQ <= cb < CB_NK:
                    blk = blk * (HEAD_DIM ** -0.5 * (LOG2E if rope else 1.0))
                o_ref[rs, cb * LANES:(cb + 1) * LANES] = blk.astype(BF16)


def _in_proj(h2, g, sh, sc, w_bf, tables, seq, tm):
    t, d = h2.shape
    tiles_per_seq = seq // tm
    rope = tables is not None
    in_specs = [pl.BlockSpec((tm, d), lambda i: (i, 0)),
                pl.BlockSpec((1, d), lambda i: (0, 0)),
                pl.BlockSpec((1, 1, d), lambda i: (i // tiles_per_seq, 0, 0)),
                pl.BlockSpec((1, 1, d), lambda i: (i // tiles_per_seq, 0, 0)),
                pl.BlockSpec((d, D_IN_PROJ), lambda i: (0, 0))]
    args = [h2, g.reshape(1, d), sh, sc, w_bf]
    if rope:
        in_specs += [pl.BlockSpec((tm, LANES), lambda i: (i % tiles_per_seq, 0))] * 3
        args += list(tables)
    return pl.pallas_call(
        functools.partial(_inproj_kernel, rope=rope),
        grid=(t // tm,),
        in_specs=in_specs,
        out_specs=pl.BlockSpec((tm, D_IN_PROJ), lambda i: (i, 0)),
        out_shape=jax.ShapeDtypeStruct((t, D_IN_PROJ), BF16),
        compiler_params=_cp(("parallel",)),
        name="in_proj_rope" if rope else "in_proj_ctx",
    )(*args)


def _rope_tables(seq):
    t = np.arange(seq)
    row = (t // GRID_W).astype(np.float32)
    col = (t % GRID_W).astype(np.float32)
    n_freq = HEAD_DIM // 4
    inv_freq = (ROPE_BASE ** (-np.arange(n_freq, dtype=np.float32) / n_freq)).astype(np.float32)
    ang_r = row[:, None] * inv_freq
    ang_c = col[:, None] * inv_freq
    cos_h = np.concatenate([np.cos(ang_r), np.cos(ang_r), np.cos(ang_c), np.cos(ang_c)], axis=1)
    sin_h = np.concatenate([np.sin(ang_r), np.sin(ang_r), np.sin(ang_c), np.sin(ang_c)], axis=1)
    lane = np.arange(HEAD_DIM)
    second = (lane % 32) >= 16
    sa = np.where(second[None, :], sin_h, 0.0)
    sb = np.where(second[None, :], 0.0, -sin_h)
    tile2 = lambda a: jnp.asarray(np.concatenate([a, a], axis=1), F32)
    return tile2(cos_h), tile2(sa), tile2(sb)


def _ret_kernel(q_ref, k_ref, v_ref, g_ref, lgt_ref, gn_ref, sf0_ref, sb0_ref,
                y_ref, sfo_ref, sbo_ref, sfs_ref, sbs_ref, *, seq):
    c = RET_CHUNK
    n_chunks = seq // c
    lg = jax.nn.log_sigmoid(lgt_ref[0])
    lgf, lgb = lg[0:1, :], lg[1:2, :]
    pos = lax.broadcasted_iota(jnp.int32, (c, 1), 0).astype(F32)
    dkf = jnp.exp(lgf * (c - 1.0 - pos))
    dkb = jnp.exp(lgb * pos)
    dqf = jnp.exp(lgf * (pos + 1.0))
    dqb = jnp.exp(lgb * (c - pos))
    cdf = jnp.exp(lgf * float(c))
    cdb = jnp.exp(lgb * float(c))
    lane = lax.broadcasted_iota(jnp.int32, (1, LANES), 1)
    first = lane < HEAD_DIM
    ri = lax.broadcasted_iota(jnp.int32, (LANES, LANES), 0)
    ci = lax.broadcasted_iota(jnp.int32, (LANES, LANES), 1)
    same = (ri < HEAD_DIM) == (ci < HEAD_DIM)
    bd = same.astype(F32)

    def head_mean(x):
        a = jnp.sum(jnp.where(first, x, 0.0), axis=-1, keepdims=True)
        b = jnp.sum(jnp.where(first, 0.0, x), axis=-1, keepdims=True)
        return jnp.where(first, a, b) * (1.0 / HEAD_DIM)

    ii = lax.broadcasted_iota(jnp.int32, (c, c), 0)
    jj = lax.broadcasted_iota(jnp.int32, (c, c), 1)
    dif = (ii - jj).astype(F32)

    def decay_mask(h0):
        lf = lgf[:, h0:h0 + 1]
        lb = lgb[:, h0:h0 + 1]
        return jnp.where(dif > 0, jnp.exp(lf * jnp.maximum(dif, 0.0)),
                         jnp.where(dif < 0, jnp.exp(lb * jnp.maximum(-dif, 0.0)), 2.0))

    dm = (decay_mask(0), decay_mask(HEAD_DIM))
    mfirst = first.astype(BF16)
    msecond = (1.0 - first.astype(F32)).astype(BF16)

    unroll = min(8, n_chunks)

    def incr(n, carry):
        sl = pl.ds(pl.multiple_of(n * c, c), c)
        kf = k_ref[0, sl, :].astype(F32)
        v = v_ref[0, sl, :]
        sfs_ref[n] = _dot_tn((kf * dkf).astype(BF16), v) * bd
        sbs_ref[n] = _dot_tn((kf * dkb).astype(BF16), v) * bd
        return carry

    lax.fori_loop(0, n_chunks, incr, 0, unroll=unroll)

    def scan_f(n, s):
        u = sfs_ref[n]
        sfs_ref[n] = s
        return cdf * s + u

    def scan_b(i, s):
        n = n_chunks - 1 - i
        u = sbs_ref[n]
        sbs_ref[n] = s
        return cdb * s + u

    sfo_ref[0, 0] = lax.fori_loop(0, n_chunks, scan_f, sf0_ref[0, 0])
    sbo_ref[0, 0] = lax.fori_loop(0, n_chunks, scan_b, sb0_ref[0, 0])

    def outp(n, carry):
        sl = pl.ds(pl.multiple_of(n * c, c), c)
        q = q_ref[0, sl, :]
        k = k_ref[0, sl, :]
        v = v_ref[0, sl, :]
        qf = q.astype(F32)
        o = None
        for hh, mk in enumerate((mfirst, msecond)):
            s = _dot_nt(q * mk, k) * dm[hh]
            oh = _dot(s.astype(BF16), v)
            o = oh if o is None else jnp.where(first, o, oh)
        qcat = jnp.concatenate([(qf * dqf).astype(BF16), (qf * dqb).astype(BF16)], axis=1)
        scat = jnp.concatenate([sfs_ref[n], sbs_ref[n]], axis=0).astype(BF16)
        o = o + _dot(qcat, scat)
        mu = head_mean(o)
        dlt = o - mu
        var = head_mean(dlt * dlt)
        on = dlt * lax.rsqrt(var + NORM_EPS) * gn_ref[0]
        y_ref[0, sl, :] = (_silu(g_ref[0, sl, :].astype(F32)) * on).astype(BF16)
        return carry

    lax.fori_loop(0, n_chunks, outp, 0, unroll=unroll)


def _retention(proj, lgt, gn, sf0, sb0):
    b, seq, _ = proj.shape
    col = lambda off: pl.BlockSpec((1, seq, LANES), lambda bi, p: (bi, 0, off + p))
    st = pl.BlockSpec((1, 1, LANES, LANES), lambda bi, p: (bi, p, 0, 0))
    return pl.pallas_call(
        functools.partial(_ret_kernel, seq=seq),
        grid=(b, N_PAIRS),
        in_specs=[col(CB_RQ), col(CB_RK), col(CB_RV), col(CB_RG),
                  pl.BlockSpec((1, 2, LANES), lambda bi, p: (p, 0, 0)),
                  pl.BlockSpec((1, 1, LANES), lambda bi, p: (p, 0, 0)),
                  st, st],
        out_specs=[pl.BlockSpec((1, seq, LANES), lambda bi, p: (bi, 0, p)), st, st],
        out_shape=[jax.ShapeDtypeStruct((b, seq, D_RET), BF16),
                   jax.ShapeDtypeStruct((b, N_PAIRS, LANES, LANES), F32),
                   jax.ShapeDtypeStruct((b, N_PAIRS, LANES, LANES), F32)],
        scratch_shapes=[pltpu.VMEM((seq // RET_CHUNK, LANES, LANES), F32),
                        pltpu.VMEM((seq // RET_CHUNK, LANES, LANES), F32)],
        compiler_params=_cp(("parallel", "parallel")),
        name="retention_s%d" % seq,
    )(proj, proj, proj, proj, lgt, gn, sf0, sb0)


def _conv_kernel(b_ref, c_ref, x_ref, w_ref, y_ref, *, seq):
    u = c_ref[0].astype(F32) * x_ref[0].astype(F32)
    row = lax.broadcasted_iota(jnp.int32, (seq, 1), 0)
    prev = jnp.where(row == 0, 0.0, pltpu.roll(u, 1, 0))
    nxt = jnp.where(row == seq - 1, 0.0, pltpu.roll(u, seq - 1, 0))
    y = w_ref[0:1, :] * prev + w_ref[1:2, :] * u + w_ref[2:3, :] * nxt
    y_ref[0] = (b_ref[0].astype(F32) * y).astype(BF16)


def _short_conv(proj, conv_w):
    b, seq, _ = proj.shape
    nblk = D_CONV // LANES
    col = lambda off: pl.BlockSpec((1, seq, LANES), lambda bi, j: (bi, 0, off + j))
    return pl.pallas_call(
        functools.partial(_conv_kernel, seq=seq),
        grid=(b, nblk),
        in_specs=[col(CB_CB), col(CB_CC), col(CB_CX),
                  pl.BlockSpec((3, LANES), lambda bi, j: (0, j))],
        out_specs=pl.BlockSpec((1, seq, LANES), lambda bi, j: (bi, 0, j)),
        out_shape=jax.ShapeDtypeStruct((b, seq, D_CONV), BF16),
        compiler_params=_cp(("parallel", "parallel")),
        name="short_conv_s%d" % seq,
    )(proj, proj, proj, conv_w)


def _na_bias_plan(rows):
    plan = np.full((3, NA_QROWS, NA_KROWS), -1, np.int64)
    for cl, rb in enumerate((0, NA_QROWS, rows - NA_QROWS)):
        ws = int(np.clip(rb - NA_ROWS // 2, 0, rows - NA_KROWS))
        for i in range(NA_QROWS):
            r = rb + i
            r0 = int(np.clip(r - NA_ROWS // 2, 0, rows - NA_ROWS))
            for j in range(NA_KROWS):
                kr = ws + j
                if r0 <= kr < r0 + NA_ROWS:
                    plan[cl, i, j] = kr - r + (NA_ROWS - 1)
    return plan


def _na_bias_kernel(rpb_ref, o_ref, *, plan):
    w = GRID_W
    ndr, ndc = 2 * NA_ROWS - 1, 2 * NA_COLS - 1
    base = pl.program_id(0) * (ndr * ndc)
    c = lax.broadcasted_iota(jnp.int32, (w, w), 0)
    kc = lax.broadcasted_iota(jnp.int32, (w, w), 1)
    dcidx = jnp.clip(kc - c, -(NA_COLS - 1), NA_COLS - 1) + (NA_COLS - 1)
    c0 = jnp.clip(c - NA_COLS // 2, 0, w - NA_COLS)
    col_in = jnp.logical_and(kc >= c0, kc < c0 + NA_COLS)
    neg = jnp.full((w, w), NEG_BIG, F32)
    tiles = []
    for dr in range(ndr):
        t = neg
        for dcv in range(ndc):
            t = jnp.where(dcidx == dcv, rpb_ref[base + dr * ndc + dcv], t)
        tiles.append(jnp.where(col_in, t * LOG2E, NEG_BIG))
    for cl in range(plan.shape[0]):
        for i in range(plan.shape[1]):
            for j in range(0, plan.shape[2], 2):
                pair = [tiles[int(d)] if d >= 0 else neg for d in plan[cl, i, j:j + 2]]
                o_ref[0, cl, i * w:(i + 1) * w, j * w:(j + 2) * w] = jnp.concatenate(pair, axis=1)


def _na_bias(rpb, rows):
    h = rpb.shape[0]
    tq, tk = NA_QROWS * GRID_W, NA_KROWS * GRID_W
    return pl.pallas_call(
        functools.partial(_na_bias_kernel, plan=_na_bias_plan(rows)),
        grid=(h,),
        in_specs=[pl.BlockSpec(memory_space=pltpu.SMEM)],
        out_specs=pl.BlockSpec((1, 3, tq, tk), lambda hh: (hh, 0, 0, 0)),
        out_shape=jax.ShapeDtypeStruct((h, 3, tq, tk), F32),
        compiler_params=_cp(("parallel",)),
        name="na_bias_expand",
    )(rpb.reshape(-1))


def _na_kernel(*refs, rows, cast):
    if cast:
        q_ref, k_ref, v_ref, kc_ref, vc_ref, bias_ref, c_ref, o_ref, cb_ref, v1_ref, vc1_ref = refs
        cb_ref[...] = c_ref[...].astype(BF16)
    else:
        q_ref, k_ref, v_ref, kc_ref, vc_ref, bias_ref, o_ref, v1_ref, vc1_ref = refs
    qb = pl.program_id(1)
    lane = lax.broadcasted_iota(jnp.int32, (1, LANES), 1)
    first = lane < HEAD_DIM

    @pl.when(qb == 0)
    def _():
        one = jnp.ones((1, LANES), BF16)
        for p in range(N_PAIRS):
            cols = slice(p * LANES, (p + 1) * LANES)
            v1_ref[2 * p] = jnp.where(first, v_ref[0, :, cols], one)
            v1_ref[2 * p + 1] = jnp.where(first, one, v_ref[0, :, cols])
            vc1_ref[2 * p] = jnp.where(first, vc_ref[0, :, cols], one)
            vc1_ref[2 * p + 1] = jnp.where(first, one, vc_ref[0, :, cols])

    ws = jnp.clip(qb * NA_QROWS - NA_ROWS // 2, 0, rows - NA_KROWS)
    sl = pl.ds(pl.multiple_of(ws * GRID_W, GRID_W), NA_KROWS * GRID_W)
    for p in range(N_PAIRS):
        cols = slice(p * LANES, (p + 1) * LANES)
        q = q_ref[0, :, cols]
        kw = k_ref[0, sl, cols]
        kc = kc_ref[0, :, cols]
        pv = []
        for hh in range(2):
            mk = (first if hh == 0 else jnp.logical_not(first)).astype(BF16)
            qh = q * mk
            s_loc = _dot_nt(qh, kw) + bias_ref[2 * p + hh, 0]
            s_ctx = _dot_nt(qh, kc)
            m = jnp.maximum(jnp.max(s_loc, axis=-1, keepdims=True), jnp.max(s_ctx, axis=-1, keepdims=True))
            p_loc = jnp.exp2(s_loc - m).astype(BF16)
            p_ctx = jnp.exp2(s_ctx - m).astype(BF16)
            pv.append(_dot(p_loc, v1_ref[2 * p + hh, sl, :]) + _dot(p_ctx, vc1_ref[2 * p + hh]))
        num = jnp.where(first, pv[0], pv[1])
        den = jnp.where(first, pltpu.roll(pv[0], HEAD_DIM, 1), pltpu.roll(pv[1], HEAD_DIM, 1))
        o_ref[0, :, cols] = (num / den).astype(BF16)


def _na(proj, proj_ctx, bias, cast=None):
    b, seq, _ = proj.shape
    ctx_len = proj_ctx.shape[1]
    rows = seq // GRID_W
    nqb = rows // NA_QROWS
    tq = NA_QROWS * GRID_W
    tk = NA_KROWS * GRID_W
    nh = 2 * N_PAIRS
    cq, ck, cv = (CB_NQ * LANES // D_NA, CB_NK * LANES // D_NA, CB_NV * LANES // D_NA)

    def cls(qb):
        return jnp.where(qb == 0, 0, jnp.where(qb == nqb - 1, 2, 1))

    in_specs = [pl.BlockSpec((1, tq, D_NA), lambda bi, qb: (bi, qb, cq)),
                pl.BlockSpec((1, seq, D_NA), lambda bi, qb: (bi, 0, ck)),
                pl.BlockSpec((1, seq, D_NA), lambda bi, qb: (bi, 0, cv)),
                pl.BlockSpec((1, ctx_len, D_NA), lambda bi, qb: (bi, 0, ck)),
                pl.BlockSpec((1, ctx_len, D_NA), lambda bi, qb: (bi, 0, cv)),
                pl.BlockSpec((nh, 1, tq, tk), lambda bi, qb: (0, cls(qb), 0, 0))]
    out_specs = [pl.BlockSpec((1, tq, D_NA), lambda bi, qb: (bi, qb, 0))]
    out_shape = [jax.ShapeDtypeStruct((b, seq, D_NA), BF16)]
    args = [proj, proj, proj, proj_ctx, proj_ctx, bias]
    if cast is not None:
        cr, cc = cast.shape
        steps = b * nqb
        assert cr % (steps * 16) == 0
        blk = pl.BlockSpec((cr // steps, cc), lambda bi, qb: (bi * nqb + qb, 0))
        in_specs.append(blk)
        out_specs.append(blk)
        out_shape.append(jax.ShapeDtypeStruct(cast.shape, BF16))
        args.append(cast)
    res = pl.pallas_call(
        functools.partial(_na_kernel, rows=rows, cast=cast is not None),
        grid=(b, nqb),
        in_specs=in_specs,
        out_specs=out_specs,
        out_shape=out_shape,
        scratch_shapes=[pltpu.VMEM((nh, seq, LANES), BF16), pltpu.VMEM((nh, ctx_len, LANES), BF16)],
        compiler_params=_cp(("parallel", "arbitrary")),
        name="na_attn",
    )(*args)
    return res if cast is not None else res[0]


def _ctx_attn_kernel(q_ref, k_ref, v_ref, o_ref):
    q = q_ref[0]
    k = k_ref[0]
    v = v_ref[0]
    lane = lax.broadcasted_iota(jnp.int32, (1, LANES), 1)
    first = lane < HEAD_DIM
    o = None
    for hh in range(2):
        mk = (first if hh == 0 else jnp.logical_not(first)).astype(BF16)
        s = _dot_nt(q * mk, k)
        m = jnp.max(s, axis=-1, keepdims=True)
        p = jnp.exp(s - m)
        l = jnp.sum(p, axis=-1, keepdims=True)
        oh = _dot(p.astype(BF16), v) / l
        o = oh if o is None else jnp.where(first, o, oh)
    o_ref[0] = o.astype(BF16)


def _ctx_attn(proj_ctx):
    b, ctx_len, _ = proj_ctx.shape
    col = lambda off: pl.BlockSpec((1, ctx_len, LANES), lambda bi, p: (bi, 0, off + p))
    return pl.pallas_call(
        _ctx_attn_kernel,
        grid=(b, N_PAIRS),
        in_specs=[col(CB_NQ), col(CB_NK), col(CB_NV)],
        out_specs=pl.BlockSpec((1, ctx_len, LANES), lambda bi, p: (bi, 0, p)),
        out_shape=jax.ShapeDtypeStruct((b, ctx_len, D_NA), BF16),
        compiler_params=_cp(("parallel", "parallel")),
        name="ctx_attn",
    )(proj_ctx, proj_ctx, proj_ctx)


def _outproj_kernel(*refs, route):
    if route:
        (yr_ref, yc_ref, yn_ref, h_ref, w_ref, g1_ref, n2_ref, sh_ref, sc_ref, rw_ref, rb_ref,
         ho_ref, xn_ref, idx_ref, wt_ref) = refs
    else:
        yr_ref, yc_ref, yn_ref, h_ref, w_ref, g1_ref, n2_ref, sh_ref, sc_ref, ho_ref, xn_ref = refs
    tm = h_ref.shape[0]
    sub = min(OUT_PROJ_SUB, tm)
    if route:
        r_hi, r_lo = _split_bf16(rw_ref[...])
        r_both = jnp.concatenate([r_hi, r_lo], axis=0)
    for r in range(tm // sub):
        rs = slice(r * sub, (r + 1) * sub)
        ycat = jnp.concatenate([yr_ref[rs, :], yc_ref[rs, :], yn_ref[rs, :]], axis=-1)
        h = h_ref[rs, :] + g1_ref[0] * _dot(ycat, w_ref[...])
        ho_ref[rs, :] = h
        a = _norm_mod(h, n2_ref[...], sh_ref[0], sc_ref[0])
        xn_ref[rs, :] = a.astype(xn_ref.dtype)
        if route:
            a_hi, a_lo = _split_bf16(a)
            t_hi = _dot_nt(r_both, a_hi)
            logits = t_hi[:N_EXPERTS] + t_hi[N_EXPERTS:] + _dot_nt(r_hi, a_lo) + rb_ref[...]
            eidx = lax.broadcasted_iota(jnp.int32, logits.shape, 0)
            m1 = jnp.max(logits, axis=0, keepdims=True)
            i1 = jnp.min(jnp.where(logits == m1, eidx, N_EXPERTS), axis=0, keepdims=True)
            rest = jnp.where(eidx == i1, -jnp.inf, logits)
            m2 = jnp.max(rest, axis=0, keepdims=True)
            i2 = jnp.min(jnp.where(rest == m2, eidx, N_EXPERTS), axis=0, keepdims=True)
            e2 = jnp.exp(m2 - m1)
            den = 1.0 + e2
            idx_ref[:, rs] = jnp.concatenate([i1, i2], axis=0)
            wt_ref[:, rs] = jnp.concatenate([1.0 / den, e2 / den], axis=0)


def _out_proj(yr, yc, yn, h2, w_bf, g1, n2g, sh2, sc2, seq, tm, router=None):
    t, d = h2.shape
    tps = seq // tm
    route = router is not None
    row = lambda wdt: pl.BlockSpec((tm, wdt), lambda i: (i, 0))
    mod = pl.BlockSpec((1, 1, d), lambda i: (i // tps, 0, 0))
    in_specs = [row(D_RET), row(D_CONV), row(D_NA), row(d),
                pl.BlockSpec((d, d), lambda i: (0, 0)), mod,
                pl.BlockSpec((1, d), lambda i: (0, 0)), mod, mod]
    args = [yr, yc, yn, h2, w_bf, g1, n2g.reshape(1, d), sh2, sc2]
    out_specs = [row(d), row(d)]
    out_shape = [jax.ShapeDtypeStruct((t, d), F32), jax.ShapeDtypeStruct((t, d), F32 if route else BF16)]
    if route:
        rw, rb = router
        in_specs += [pl.BlockSpec((N_EXPERTS, d), lambda i: (0, 0)),
                     pl.BlockSpec((N_EXPERTS, 1), lambda i: (0, 0))]
        args += [rw.T, rb.reshape(N_EXPERTS, 1)]
        out_specs += [pl.BlockSpec((2, tm), lambda i: (0, i))] * 2
        out_shape += [jax.ShapeDtypeStruct((2, t), jnp.int32), jax.ShapeDtypeStruct((2, t), F32)]
    return pl.pallas_call(
        functools.partial(_outproj_kernel, route=route),
        grid=(t // tm,),
        in_specs=in_specs,
        out_specs=out_specs,
        out_shape=out_shape,
        compiler_params=_cp(("parallel",)),
        name="out_proj_route" if route else "out_proj_t%d" % t,
    )(*args)


def _swiglu_chunks(x, w_in_ref, w_out_ref, ff, n_chunks):
    tf = -(-ff // (n_chunks * MXU_COLS)) * MXU_COLS
    y = None
    for c0 in range(0, ff, tf):
        c1 = min(c0 + tf, ff)
        gate = _dot(x, w_in_ref[:, c0:c1])
        up = _dot(x, w_in_ref[:, ff + c0:ff + c1])
        part = _dot((_silu(gate) * up).astype(BF16), w_out_ref[c0:c1, :])
        y = part if y is None else y + part
    return y


def _ffn_kernel(*refs, ff, n_chunks, cast):
    if cast:
        x_ref, h_ref, wi_ref, wo_ref, g2_ref, c_ref, o_ref, cb_ref = refs
        cb_ref[...] = c_ref[...].astype(BF16)
    else:
        x_ref, h_ref, wi_ref, wo_ref, g2_ref, o_ref = refs
    y = _swiglu_chunks(x_ref[...], wi_ref, wo_ref, ff, n_chunks)
    o_ref[...] = h_ref[...] + g2_ref[0] * y


def _ffn(xn, h2, w_in_bf, w_out_bf, g2, seq, tm, n_chunks, cast=None):
    t, d = h2.shape
    ff = w_out_bf.shape[0]
    tps = seq // tm
    steps = t // tm
    resident = pl.Buffered(1)
    in_specs = [pl.BlockSpec((tm, d), lambda i: (i, 0)),
                pl.BlockSpec((tm, d), lambda i: (i, 0)),
                pl.BlockSpec((d, 2 * ff), lambda i: (0, 0), pipeline_mode=resident),
                pl.BlockSpec((ff, d), lambda i: (0, 0), pipeline_mode=resident),
                pl.BlockSpec((1, 1, d), lambda i: (i // tps, 0, 0))]
    out_specs = [pl.BlockSpec((tm, d), lambda i: (i, 0))]
    out_shape = [jax.ShapeDtypeStruct((t, d), F32)]
    args = [xn, h2, w_in_bf, w_out_bf, g2]
    if cast is not None:
        cr, cc = cast.shape
        assert cr % (steps * 16) == 0
        in_specs.append(pl.BlockSpec((cr // steps, cc), lambda i: (i, 0)))
        out_specs.append(pl.BlockSpec((cr // steps, cc), lambda i: (i, 0)))
        out_shape.append(jax.ShapeDtypeStruct(cast.shape, BF16))
        args.append(cast)
    res = pl.pallas_call(
        functools.partial(_ffn_kernel, ff=ff, n_chunks=n_chunks, cast=cast is not None),
        grid=(steps,),
        in_specs=in_specs,
        out_specs=out_specs,
        out_shape=out_shape,
        compiler_params=_cp(("parallel",)),
        name="ffn_t%d" % t,
    )(*args)
    return res if cast is not None else res[0]


def _dispatch_kernel(s0_ref, s1_ref, zs_ref, zv_ref, x_ref, xs_ref, slab_ref, zbuf_ref, sem, *, rows):
    @pl.when(pl.program_id(0) == 0)
    def _():
        zbuf_ref[...] = jnp.zeros_like(zbuf_ref)
        for e in range(2 * N_EXPERTS):
            @pl.when(zv_ref[e] != 0)
            def _():
                dst = xs_ref.at[pl.ds(pl.multiple_of(zs_ref[e], MOE_TS), MOE_TS)]
                pltpu.make_async_copy(zbuf_ref, dst, sem.at[1]).start()
        for e in range(2 * N_EXPERTS):
            @pl.when(zv_ref[e] != 0)
            def _():
                pltpu.make_async_copy(zbuf_ref, xs_ref.at[pl.ds(0, MOE_TS)], sem.at[1]).wait()

    nk = slab_ref.shape[1]

    def issue(g, carry):
        rs = pl.ds(pl.multiple_of(g * DMA_UNROLL, DMA_UNROLL), DMA_UNROLL)
        for k in range(nk):
            slab_ref[rs, k, :] = x_ref[rs, k * LANES:(k + 1) * LANES]
        for u in range(DMA_UNROLL):
            r = g * DMA_UNROLL + u
            src = slab_ref.at[r]
            pltpu.make_async_copy(src, xs_ref.at[s0_ref[r]], sem.at[0]).start(priority=u % 2)
            pltpu.make_async_copy(src, xs_ref.at[s1_ref[r]], sem.at[0]).start(priority=(u + 1) % 2)
        return carry

    lax.fori_loop(0, rows // DMA_UNROLL, issue, 0)
    for _ in range(2):
        pltpu.make_async_copy(slab_ref, xs_ref.at[pl.ds(0, rows)], sem.at[0]).wait()


def _dispatch(xn, slot0, slot1, zplan, n_slots, rows):
    t, d = xn.shape
    slab = (d // LANES, LANES)
    zstart, zvalid = zplan
    return pl.pallas_call(
        functools.partial(_dispatch_kernel, rows=rows),
        grid=(t // rows,),
        in_specs=[pl.BlockSpec((rows,), lambda i: (i,), memory_space=pltpu.SMEM),
                  pl.BlockSpec((rows,), lambda i: (i,), memory_space=pltpu.SMEM),
                  pl.BlockSpec((2 * N_EXPERTS,), lambda i: (0,), memory_space=pltpu.SMEM),
                  pl.BlockSpec((2 * N_EXPERTS,), lambda i: (0,), memory_space=pltpu.SMEM),
                  pl.BlockSpec((rows, d), lambda i: (i, 0))],
        out_specs=pl.BlockSpec(memory_space=pl.ANY),
        out_shape=jax.ShapeDtypeStruct((n_slots,) + slab, xn.dtype),
        scratch_shapes=[pltpu.VMEM((rows,) + slab, xn.dtype), pltpu.VMEM((MOE_TS,) + slab, xn.dtype),
                        pltpu.SemaphoreType.DMA((2,))],
        compiler_params=_cp(("arbitrary",)),
        name="moe_dispatch",
    )(slot0, slot1, zstart, zvalid, xn)


def _expert_kernel(be_ref, nu_ref, xs_ref, wi_ref, wo_ref, o_ref, xa_ref, xb_ref, sem, *, ff, n_chunks):
    j = pl.program_id(0)
    n = pl.num_programs(0)
    ts, d = xa_ref.shape
    bufs = (xa_ref, xb_ref)

    def copies(blk, b):
        rows = pl.ds(pl.multiple_of(blk * ts, ts), ts)
        return [pltpu.make_async_copy(xs_ref.at[rows, k, :], bufs[b].at[:, k * LANES:(k + 1) * LANES],
                                      sem.at[b]) for k in range(d // LANES)]

    @pl.when(j == 0)
    def _():
        for c in copies(0, 0):
            c.start()

    def step(b):
        @pl.when(j + 1 < n)
        def _():
            for c in copies(j + 1, 1 - b):
                c.start()

        for c in copies(j, b):
            c.wait()

        @pl.when(j < nu_ref[0])
        def _():
            o_ref[...] = _swiglu_chunks(bufs[b][...].astype(BF16), wi_ref.at[0], wo_ref.at[0], ff, n_chunks)

        @pl.when(j >= nu_ref[0])
        def _():
            o_ref[...] = jnp.zeros_like(o_ref)

    for b in range(2):
        pl.when(j % 2 == b)(functools.partial(step, b))


def _experts(xs, block_expert, n_used, w_in_bf, w_out_bf, n_chunks):
    n, nk, _ = xs.shape
    d = nk * LANES
    ff = w_out_bf.shape[1]
    ts = MOE_TS
    grid_spec = pltpu.PrefetchScalarGridSpec(
        num_scalar_prefetch=2,
        grid=(n // ts,),
        in_specs=[pl.BlockSpec(memory_space=pl.ANY),
                  pl.BlockSpec((1, d, 2 * ff), lambda j, be, nu: (be[j], 0, 0)),
                  pl.BlockSpec((1, ff, d), lambda j, be, nu: (be[j], 0, 0))],
        out_specs=pl.BlockSpec((ts, d), lambda j, be, nu: (j, 0)),
        scratch_shapes=[pltpu.VMEM((ts, d), xs.dtype), pltpu.VMEM((ts, d), xs.dtype),
                        pltpu.SemaphoreType.DMA((2,))],
    )
    return pl.pallas_call(
        functools.partial(_expert_kernel, ff=ff, n_chunks=n_chunks),
        grid_spec=grid_spec,
        out_shape=jax.ShapeDtypeStruct((n, d), F32),
        compiler_params=_cp(("arbitrary",)),
        name="moe_experts",
    )(block_expert, n_used, xs, w_in_bf, w_out_bf)


def _combine_kernel(p0_ref, p1_ref, q0_ref, q1_ref, ys_ref, h_ref, w0_ref, w1_ref, g2_ref, fg_ref, o_ref,
                    ya0_ref, ya1_ref, yb0_ref, yb1_ref, sem, *, rows):
    i = pl.program_id(0)
    n = pl.num_programs(0)
    n_iter = rows // COMBINE_GROUP
    bufs = ((ya0_ref, ya1_ref), (yb0_ref, yb1_ref))

    def issue_rows(pa_ref, pb_ref, b, g):
        for u in range(COMBINE_GROUP):
            r = g * COMBINE_GROUP + u
            pltpu.make_async_copy(ys_ref.at[pl.ds(pa_ref[r], 1)], bufs[b][0].at[pl.ds(r, 1)],
                                  sem.at[0, b]).start(priority=u % 2)
            pltpu.make_async_copy(ys_ref.at[pl.ds(pb_ref[r], 1)], bufs[b][1].at[pl.ds(r, 1)],
                                  sem.at[1, b]).start(priority=(u + 1) % 2)

    def finish_rows(b, g):
        rs = pl.ds(pl.multiple_of(g * COMBINE_GROUP, COMBINE_GROUP), COMBINE_GROUP)
        y = w0_ref[rs, :] * bufs[b][0][rs, :] + w1_ref[rs, :] * bufs[b][1][rs, :]
        h = h_ref[rs, :] + g2_ref[0] * y
        ms = jnp.mean(h * h, axis=-1, keepdims=True)
        o_ref[rs, :] = h * lax.rsqrt(ms + NORM_EPS) * fg_ref[...]

    @pl.when(i == 0)
    def _():
        def first(g, carry):
            issue_rows(p0_ref, p1_ref, 0, g)
            return carry
        lax.fori_loop(0, n_iter, first, 0)

    def step(b):
        pltpu.make_async_copy(ys_ref.at[pl.ds(0, rows)], bufs[b][0], sem.at[0, b]).wait()
        pltpu.make_async_copy(ys_ref.at[pl.ds(0, rows)], bufs[b][1], sem.at[1, b]).wait()

        @pl.when(i + 1 < n)
        def _():
            def both(g, carry):
                issue_rows(q0_ref, q1_ref, 1 - b, g)
                finish_rows(b, g)
                return carry
            lax.fori_loop(0, n_iter, both, 0)

        @pl.when(i + 1 == n)
        def _():
            def last(g, carry):
                finish_rows(b, g)
                return carry
            lax.fori_loop(0, n_iter, last, 0)

    for b in range(2):
        pl.when(i % 2 == b)(functools.partial(step, b))


def _combine(pos0, pos1, ys, h2, w0, w1, g2, final_g, seq, rows):
    t, d = h2.shape
    tps = seq // rows
    steps = t // rows
    blk = pl.BlockSpec((rows, d), lambda i: (i, 0))
    col = pl.BlockSpec((rows, 1), lambda i: (i, 0))
    here = pl.BlockSpec((rows,), lambda i: (i,), memory_space=pltpu.SMEM)
    ahead = pl.BlockSpec((rows,), lambda i: (jnp.minimum(i + 1, steps - 1),), memory_space=pltpu.SMEM)
    return pl.pallas_call(
        functools.partial(_combine_kernel, rows=rows),
        grid=(steps,),
        in_specs=[here, here, ahead, ahead,
                  pl.BlockSpec(memory_space=pl.ANY),
                  blk, col, col,
                  pl.BlockSpec((1, 1, d), lambda i: (i // tps, 0, 0)),
                  pl.BlockSpec((1, d), lambda i: (0, 0))],
        out_specs=blk,
        out_shape=jax.ShapeDtypeStruct((t, d), F32),
        scratch_shapes=[pltpu.VMEM((rows, d), F32)] * 4 + [pltpu.SemaphoreType.DMA((2, 2))],
        compiler_params=_cp(("arbitrary",)),
        name="moe_combine_norm",
    )(pos0, pos1, pos0, pos1, ys, h2, w0, w1, g2, final_g.reshape(1, d))


def _route_plan(idx):
    t = idx.shape[1]
    ts = MOE_TS
    n_slots = 2 * t + N_EXPERTS * ts
    e_flat = idx.reshape(-1)
    onehot = (e_flat[:, None] == jnp.arange(N_EXPERTS, dtype=jnp.int32)[None, :]).astype(jnp.int32)
    csum = jnp.cumsum(onehot, axis=0)
    rank = jnp.sum((csum - onehot) * onehot, axis=1)
    counts = csum[-1]
    padded = ((counts + ts - 1) // ts) * ts
    ends = jnp.cumsum(padded)
    offs = ends - padded
    slot = (jnp.sum(onehot * offs[None, :], axis=1) + rank).astype(jnp.int32)
    starts = jnp.arange(n_slots // ts, dtype=jnp.int32) * ts
    block_expert = jnp.minimum(jnp.sum((starts[:, None] >= ends[None, :]).astype(jnp.int32), axis=1),
                               N_EXPERTS - 1).astype(jnp.int32)
    n_used = (ends[-1] // ts).astype(jnp.int32).reshape(1)
    tail = ends[-1] + jnp.arange(N_EXPERTS, dtype=jnp.int32) * ts
    zstart = jnp.concatenate([ends - ts, tail])
    zvalid = jnp.concatenate([padded > 0, tail < n_slots]).astype(jnp.int32)
    zstart = jnp.where(zvalid != 0, zstart, 0).astype(jnp.int32)
    return slot[:t], slot[t:], block_expert, n_used, (zstart, zvalid), n_slots


def kernel(x, c, ctx, c_ctx, ada_w, ada_b, norm1_g, norm2_g, w_in, w_out, ret_decay_logit, ret_gn_g,
           conv_w, na_rpb, ffn_w_in, ffn_w_out, moe_router_w, moe_router_b, moe_w_in, moe_w_out, final_g):
    b, seq, d = x.shape
    ctx_len = ctx.shape[1]
    depth = ada_w.shape[0]
    assert depth == 2, "the final norm is fused into the last (MoE) layer's combine step"
    rows = seq // GRID_W
    t_lat, t_ctx = b * seq, b * ctx_len
    tm_lat = min(1024, seq)
    tm_ctx = ctx_len

    c8 = jnp.zeros((8, d), F32).at[:b].set(c).at[b].set(c_ctx)
    mods = _ada(c8, ada_w, ada_b)
    tables = _rope_tables(seq)

    h = x.reshape(t_lat, d)
    hc = ctx.reshape(t_ctx, d)
    zero_state = jnp.zeros((b, N_PAIRS, LANES, LANES), F32)
    out = None
    for layer in range(depth):
        update_ctx = layer < depth - 1
        m = mods[layer].reshape(8, 6, d)
        lat = [m[:b, i].reshape(b, 1, d) for i in range(6)]
        cx = [jnp.broadcast_to(m[b, i].reshape(1, 1, d), (b, 1, d)) for i in range(6)]
        w_in_bf = w_in[layer].astype(BF16)
        w_out_bf = w_out[layer].astype(BF16)
        lgt = ret_decay_logit[layer].reshape(2, N_PAIRS, LANES // HEAD_DIM)
        lgt = jnp.repeat(lgt, HEAD_DIM, axis=2).transpose(1, 0, 2)
        gn = ret_gn_g[layer].reshape(N_PAIRS, 1, LANES)

        proj_c = _in_proj(hc, norm1_g[layer], cx[0], cx[1], w_in_bf, None, ctx_len, tm_ctx)
        proj_c = proj_c.reshape(b, ctx_len, D_IN_PROJ)
        proj = _in_proj(h, norm1_g[layer], lat[0], lat[1], w_in_bf, tables, seq, tm_lat)
        proj = proj.reshape(b, seq, D_IN_PROJ)

        y_ret_c, s_fwd, s_bwd = _retention(proj_c, lgt, gn, zero_state, zero_state)
        y_ret, _, _ = _retention(proj, lgt, gn, s_fwd, s_bwd)
        y_conv = _short_conv(proj, conv_w[layer])
        if layer % 2 == 1:
            mwo = moe_w_out[layer // 2]
            y_na, mw_out_bf = _na(proj, proj_c, _na_bias(na_rpb[layer], rows), cast=mwo.reshape(-1, mwo.shape[-1]))
            mw_out_bf = mw_out_bf.reshape(mwo.shape)
        else:
            y_na = _na(proj, proj_c, _na_bias(na_rpb[layer], rows))

        if layer % 2 == 0:
            h, xn = _out_proj(y_ret.reshape(t_lat, D_RET), y_conv.reshape(t_lat, D_CONV),
                              y_na.reshape(t_lat, D_NA), h, w_out_bf, lat[2], norm2_g[layer],
                              lat[3], lat[4], seq, tm_lat)
            fw_in = ffn_w_in[layer // 2].astype(BF16)
            fw_out = ffn_w_out[layer // 2].astype(BF16)
            mw = moe_w_in[(layer + 1) // 2]
            h, mw_in_bf = _ffn(xn, h, fw_in, fw_out, lat[5], seq, min(512, seq), FFN_CHUNKS,
                               cast=mw.reshape(-1, mw.shape[-1]))
            mw_in_bf = mw_in_bf.reshape(mw.shape)
            if update_ctx:
                y_conv_c = _short_conv(proj_c, conv_w[layer])
                y_na_c = _ctx_attn(proj_c)
                hc, xnc = _out_proj(y_ret_c.reshape(t_ctx, D_RET), y_conv_c.reshape(t_ctx, D_CONV),
                                    y_na_c.reshape(t_ctx, D_NA), hc, w_out_bf, cx[2], norm2_g[layer],
                                    cx[3], cx[4], ctx_len, tm_ctx)
                hc = _ffn(xnc, hc, fw_in, fw_out, cx[5], ctx_len, tm_ctx, FFN_CHUNKS)
        else:
            e = layer // 2
            h, xn, idx, wts = _out_proj(y_ret.reshape(t_lat, D_RET), y_conv.reshape(t_lat, D_CONV),
                                        y_na.reshape(t_lat, D_NA), h, w_out_bf, lat[2], norm2_g[layer],
                                        lat[3], lat[4], seq, tm_lat,
                                        router=(moe_router_w[e], moe_router_b[e]))
            pos0, pos1, block_expert, n_used, zstart, n_slots = _route_plan(idx)
            xs = _dispatch(xn, pos0, pos1, zstart, n_slots, min(1024, seq))
            ys = _experts(xs, block_expert, n_used, mw_in_bf, mw_out_bf, FFN_CHUNKS)
            out = _combine(pos0, pos1, ys, h, wts[0].reshape(t_lat, 1), wts[1].reshape(t_lat, 1),
                           lat[5], final_g, seq, min(512, seq))
            h = out
    return out.reshape(b, seq, d)
```

```python
import functools

import numpy as np
import jax
import jax.numpy as jnp
from jax import lax
from jax.experimental import pallas as pl
from jax.experimental.pallas import tpu as pltpu

F32 = jnp.float32
BF16 = jnp.bfloat16

LANES = 128
HEAD_DIM = 64
GRID_W = 64
N_CONV_GROUPS = 4
N_RET_HEADS = 6
N_NA_HEADS = 6
D_RET = N_RET_HEADS * HEAD_DIM
D_CONV = N_CONV_GROUPS * HEAD_DIM
D_NA = N_NA_HEADS * HEAD_DIM
D_IN_PROJ = 4 * D_RET + 3 * D_CONV + 3 * D_NA
N_PAIRS = D_RET // LANES
NA_ROWS = 8
NA_COLS = 16
N_EXPERTS = 8
ROPE_BASE = 10000.0
NORM_EPS = 1e-6
NEG_BIG = -1e30
LOG2E = 1.4426950408889634

RET_CHUNK = 256
NA_QROWS = 4
NA_KROWS = 12
MOE_TS = 512
DMA_UNROLL = 8
COMBINE_GROUP = 32
FFN_CHUNKS = 2
IN_PROJ_SUB = 512
OUT_PROJ_SUB = 512
MXU_COLS = 256
IN_PROJ_CHUNK = 3 * MXU_COLS
VMEM_LIMIT = 56 * 1024 * 1024

CB_RQ, CB_RK, CB_RV, CB_RG = 0, 3, 6, 9
CB_CB, CB_CC, CB_CX = 12, 14, 16
CB_NQ, CB_NK, CB_NV = 18, 21, 24


def _cp(sem, vmem=VMEM_LIMIT):
    return pltpu.CompilerParams(dimension_semantics=sem, vmem_limit_bytes=vmem)


def _silu(x):
    return x * (1.0 / (1.0 + jnp.exp(-x)))


def _dot(a, b):
    return jnp.dot(a, b, preferred_element_type=F32)


def _dot_nt(a, b):
    return lax.dot_general(a, b, (((1,), (1,)), ((), ())), preferred_element_type=F32)


def _dot_tn(a, b):
    return lax.dot_general(a, b, (((0,), (0,)), ((), ())), preferred_element_type=F32)


def _split_bf16(x):
    hi = x.astype(BF16)
    lo = (x - hi.astype(F32)).astype(BF16)
    return hi, lo


def _ada_kernel(c_ref, w_ref, b_ref, o_ref):
    x = _silu(c_ref[...]).astype(BF16)
    o_ref[0] = _dot(x, w_ref[0].astype(BF16)) + b_ref[0]


def _ada(c8, ada_w, ada_b):
    depth, d, n = ada_w.shape
    tn = n // 4
    return pl.pallas_call(
        _ada_kernel,
        grid=(depth, n // tn),
        in_specs=[pl.BlockSpec((8, d), lambda l, j: (0, 0)),
                  pl.BlockSpec((1, d, tn), lambda l, j: (l, 0, j)),
                  pl.BlockSpec((1, 1, tn), lambda l, j: (l, 0, j))],
        out_specs=pl.BlockSpec((1, 8, tn), lambda l, j: (l, 0, j)),
        out_shape=jax.ShapeDtypeStruct((depth, 8, n), F32),
        compiler_params=_cp(("parallel", "parallel")),
        name="ada_mod",
    )(c8, ada_w, ada_b.reshape(depth, 1, n))


def _norm_mod(x, g, sh, sc):
    ms = jnp.mean(x * x, axis=-1, keepdims=True)
    y = x * lax.rsqrt(ms + NORM_EPS) * g
    return y * (1.0 + sc) + sh


def _inproj_kernel(*refs, rope, n_cast):
    n_in = (8 if rope else 5) + n_cast
    for c_ref, cb_ref in zip(refs[n_in - n_cast:n_in], refs[n_in + 1:]):
        cb_ref[...] = c_ref[...].astype(BF16)
    if rope:
        h_ref, g_ref, sh_ref, sc_ref, w_ref, cos_ref, sa_ref, sb_ref = refs[:8]
    else:
        h_ref, g_ref, sh_ref, sc_ref, w_ref = refs[:5]
    o_ref = refs[n_in]
    tm = h_ref.shape[0]
    sub = min(IN_PROJ_SUB, tm)
    cw = IN_PROJ_CHUNK
    for r in range(tm // sub):
        rs = slice(r * sub, (r + 1) * sub)
        xn = _norm_mod(h_ref[rs, :], g_ref[...], sh_ref[0], sc_ref[0]).astype(BF16)
        for c0 in range(0, D_IN_PROJ, cw):
            c1 = min(c0 + cw, D_IN_PROJ)
            acc = _dot(xn, w_ref[:, c0:c1])
            for j in range((c1 - c0) // LANES):
                blk = acc[:, j * LANES:(j + 1) * LANES]
                cb = c0 // LANES + j
                if rope and cb < CB_RV:
                    blk = (blk * cos_ref[rs, :] + pltpu.roll(blk, 16, 1) * sa_ref[rs, :]
                           + pltpu.roll(blk, LANES - 16, 1) * sb_ref[rs, :])
                if CB_RK <= cb < CB_RV:
                    blk = blk * (HEAD_DIM ** -0.5)
                elif CB_NQ <= cb < CB_NK:
                    blk = blk * (HEAD_DIM ** -0.5 * (LOG2E if rope else 1.0))
                o_ref[rs, cb * LANES:(cb + 1) * LANES] = blk.astype(BF16)


def _in_proj(h2, g, sh, sc, w_bf, tables, seq, tm, casts=()):
    t, d = h2.shape
    tiles_per_seq = seq // tm
    steps = t // tm
    rope = tables is not None
    in_specs = [pl.BlockSpec((tm, d), lambda i: (i, 0)),
                pl.BlockSpec((1, d), lambda i: (0, 0)),
                pl.BlockSpec((1, 1, d), lambda i: (i // tiles_per_seq, 0, 0)),
                pl.BlockSpec((1, 1, d), lambda i: (i // tiles_per_seq, 0, 0)),
                pl.BlockSpec((d, D_IN_PROJ), lambda i: (0, 0))]
    args = [h2, g.reshape(1, d), sh, sc, w_bf]
    if rope:
        in_specs += [pl.BlockSpec((tm, LANES), lambda i: (i % tiles_per_seq, 0))] * 3
        args += list(tables)
    out_specs = [pl.BlockSpec((tm, D_IN_PROJ), lambda i: (i, 0))]
    out_shape = [jax.ShapeDtypeStruct((t, D_IN_PROJ), BF16)]
    for c in casts:
        cr, cc = c.shape
        assert cr % (steps * 16) == 0
        blk = pl.BlockSpec((cr // steps, cc), lambda i: (i, 0))
        in_specs.append(blk)
        out_specs.append(blk)
        out_shape.append(jax.ShapeDtypeStruct(c.shape, BF16))
        args.append(c)
    res = pl.pallas_call(
        functools.partial(_inproj_kernel, rope=rope, n_cast=len(casts)),
        grid=(steps,),
        in_specs=in_specs,
        out_specs=out_specs,
        out_shape=out_shape,
        compiler_params=_cp(("parallel",)),
        name="in_proj_rope" if rope else "in_proj_ctx",
    )(*args)
    return res if casts else res[0]


def _rope_tables(seq):
    t = np.arange(seq)
    row = (t // GRID_W).astype(np.float32)
    col = (t % GRID_W).astype(np.float32)
    n_freq = HEAD_DIM // 4
    inv_freq = (ROPE_BASE ** (-np.arange(n_freq, dtype=np.float32) / n_freq)).astype(np.float32)
    ang_r = row[:, None] * inv_freq
    ang_c = col[:, None] * inv_freq
    cos_h = np.concatenate([np.cos(ang_r), np.cos(ang_r), np.cos(ang_c), np.cos(ang_c)], axis=1)
    sin_h = np.concatenate([np.sin(ang_r), np.sin(ang_r), np.sin(ang_c), np.sin(ang_c)], axis=1)
    lane = np.arange(HEAD_DIM)
    second = (lane % 32) >= 16
    sa = np.where(second[None, :], sin_h, 0.0)
    sb = np.where(second[None, :], 0.0, -sin_h)
    tile2 = lambda a: jnp.asarray(np.concatenate([a, a], axis=1), F32)
    return tile2(cos_h), tile2(sa), tile2(sb)


def _ret_kernel(q_ref, k_ref, v_ref, g_ref, lgt_ref, gn_ref, sf0_ref, sb0_ref,
                y_ref, sfo_ref, sbo_ref, sfs_ref, sbs_ref, *, seq):
    c = RET_CHUNK
    n_chunks = seq // c
    lg = jax.nn.log_sigmoid(lgt_ref[0])
    lgf, lgb = lg[0:1, :], lg[1:2, :]
    pos = lax.broadcasted_iota(jnp.int32, (c, 1), 0).astype(F32)
    dkf = jnp.exp(lgf * (c - 1.0 - pos))
    dkb = jnp.exp(lgb * pos)
    dqf = jnp.exp(lgf * (pos + 1.0))
    dqb = jnp.exp(lgb * (c - pos))
    cdf = jnp.exp(lgf * float(c))
    cdb = jnp.exp(lgb * float(c))
    lane = lax.broadcasted_iota(jnp.int32, (1, LANES), 1)
    first = lane < HEAD_DIM
    ri = lax.broadcasted_iota(jnp.int32, (LANES, LANES), 0)
    ci = lax.broadcasted_iota(jnp.int32, (LANES, LANES), 1)
    same = (ri < HEAD_DIM) == (ci < HEAD_DIM)
    bd = same.astype(F32)

    def head_mean(x):
        a = jnp.sum(jnp.where(first, x, 0.0), axis=-1, keepdims=True)
        b = jnp.sum(jnp.where(first, 0.0, x), axis=-1, keepdims=True)
        return jnp.where(first, a, b) * (1.0 / HEAD_DIM)

    ii = lax.broadcasted_iota(jnp.int32, (c, c), 0)
    jj = lax.broadcasted_iota(jnp.int32, (c, c), 1)
    dif = (ii - jj).astype(F32)

    def decay_mask(h0):
        lf = lgf[:, h0:h0 + 1]
        lb = lgb[:, h0:h0 + 1]
        return jnp.where(dif > 0, jnp.exp(lf * jnp.maximum(dif, 0.0)),
                         jnp.where(dif < 0, jnp.exp(lb * jnp.maximum(-dif, 0.0)), 2.0))

    dm = (decay_mask(0), decay_mask(HEAD_DIM))
    mfirst = first.astype(BF16)
    msecond = (1.0 - first.astype(F32)).astype(BF16)

    unroll = min(8, n_chunks)

    def incr(n, carry):
        sl = pl.ds(pl.multiple_of(n * c, c), c)
        kf = k_ref[0, sl, :].astype(F32)
        v = v_ref[0, sl, :]
        sfs_ref[n] = _dot_tn((kf * dkf).astype(BF16), v) * bd
        sbs_ref[n] = _dot_tn((kf * dkb).astype(BF16), v) * bd
        return carry

    lax.fori_loop(0, n_chunks, incr, 0, unroll=unroll)

    def scan_f(n, s):
        u = sfs_ref[n]
        sfs_ref[n] = s
        return cdf * s + u

    def scan_b(i, s):
        n = n_chunks - 1 - i
        u = sbs_ref[n]
        sbs_ref[n] = s
        return cdb * s + u

    sfo_ref[0, 0] = lax.fori_loop(0, n_chunks, scan_f, sf0_ref[0, 0])
    sbo_ref[0, 0] = lax.fori_loop(0, n_chunks, scan_b, sb0_ref[0, 0])

    def outp(n, carry):
        sl = pl.ds(pl.multiple_of(n * c, c), c)
        q = q_ref[0, sl, :]
        k = k_ref[0, sl, :]
        v = v_ref[0, sl, :]
        qf = q.astype(F32)
        o = None
        for hh, mk in enumerate((mfirst, msecond)):
            s = _dot_nt(q * mk, k) * dm[hh]
            oh = _dot(s.astype(BF16), v)
            o = oh if o is None else jnp.where(first, o, oh)
        qcat = jnp.concatenate([(qf * dqf).astype(BF16), (qf * dqb).astype(BF16)], axis=1)
        scat = jnp.concatenate([sfs_ref[n], sbs_ref[n]], axis=0).astype(BF16)
        o = o + _dot(qcat, scat)
        mu = head_mean(o)
        dlt = o - mu
        var = head_mean(dlt * dlt)
        on = dlt * lax.rsqrt(var + NORM_EPS) * gn_ref[0]
        y_ref[0, sl, :] = (_silu(g_ref[0, sl, :].astype(F32)) * on).astype(BF16)
        return carry

    lax.fori_loop(0, n_chunks, outp, 0, unroll=unroll)


def _retention(proj, lgt, gn, sf0, sb0):
    b, seq, _ = proj.shape
    col = lambda off: pl.BlockSpec((1, seq, LANES), lambda bi, p: (bi, 0, off + p))
    st = pl.BlockSpec((1, 1, LANES, LANES), lambda bi, p: (bi, p, 0, 0))
    return pl.pallas_call(
        functools.partial(_ret_kernel, seq=seq),
        grid=(b, N_PAIRS),
        in_specs=[col(CB_RQ), col(CB_RK), col(CB_RV), col(CB_RG),
                  pl.BlockSpec((1, 2, LANES), lambda bi, p: (p, 0, 0)),
                  pl.BlockSpec((1, 1, LANES), lambda bi, p: (p, 0, 0)),
                  st, st],
        out_specs=[pl.BlockSpec((1, seq, LANES), lambda bi, p: (bi, 0, p)), st, st],
        out_shape=[jax.ShapeDtypeStruct((b, seq, D_RET), BF16),
                   jax.ShapeDtypeStruct((b, N_PAIRS, LANES, LANES), F32),
                   jax.ShapeDtypeStruct((b, N_PAIRS, LANES, LANES), F32)],
        scratch_shapes=[pltpu.VMEM((seq // RET_CHUNK, LANES, LANES), F32),
                        pltpu.VMEM((seq // RET_CHUNK, LANES, LANES), F32)],
        compiler_params=_cp(("parallel", "parallel")),
        name="retention_s%d" % seq,
    )(proj, proj, proj, proj, lgt, gn, sf0, sb0)


def _conv_kernel(b_ref, c_ref, x_ref, w_ref, y_ref, *, seq):
    u = c_ref[0].astype(F32) * x_ref[0].astype(F32)
    row = lax.broadcasted_iota(jnp.int32, (seq, 1), 0)
    prev = jnp.where(row == 0, 0.0, pltpu.roll(u, 1, 0))
    nxt = jnp.where(row == seq - 1, 0.0, pltpu.roll(u, seq - 1, 0))
    y = w_ref[0:1, :] * prev + w_ref[1:2, :] * u + w_ref[2:3, :] * nxt
    y_ref[0] = (b_ref[0].astype(F32) * y).astype(BF16)


def _short_conv(proj, conv_w):
    b, seq, _ = proj.shape
    nblk = D_CONV // LANES
    col = lambda off: pl.BlockSpec((1, seq, LANES), lambda bi, j: (bi, 0, off + j))
    return pl.pallas_call(
        functools.partial(_conv_kernel, seq=seq),
        grid=(b, nblk),
        in_specs=[col(CB_CB), col(CB_CC), col(CB_CX),
                  pl.BlockSpec((3, LANES), lambda bi, j: (0, j))],
        out_specs=pl.BlockSpec((1, seq, LANES), lambda bi, j: (bi, 0, j)),
        out_shape=jax.ShapeDtypeStruct((b, seq, D_CONV), BF16),
        compiler_params=_cp(("parallel", "parallel")),
        name="short_conv_s%d" % seq,
    )(proj, proj, proj, conv_w)


def _na_bias_plan(rows):
    plan = np.full((3, NA_QROWS, NA_KROWS), -1, np.int64)
    for cl, rb in enumerate((0, NA_QROWS, rows - NA_QROWS)):
        ws = int(np.clip(rb - NA_ROWS // 2, 0, rows - NA_KROWS))
        for i in range(NA_QROWS):
            r = rb + i
            r0 = int(np.clip(r - NA_ROWS // 2, 0, rows - NA_ROWS))
            for j in range(NA_KROWS):
                kr = ws + j
                if r0 <= kr < r0 + NA_ROWS:
                    plan[cl, i, j] = kr - r + (NA_ROWS - 1)
    return plan


def _na_bias_kernel(rpb_ref, o_ref, *, plan):
    w = GRID_W
    ndr, ndc = 2 * NA_ROWS - 1, 2 * NA_COLS - 1
    base = pl.program_id(0) * (ndr * ndc)
    c = lax.broadcasted_iota(jnp.int32, (w, w), 0)
    kc = lax.broadcasted_iota(jnp.int32, (w, w), 1)
    dcidx = jnp.clip(kc - c, -(NA_COLS - 1), NA_COLS - 1) + (NA_COLS - 1)
    c0 = jnp.clip(c - NA_COLS // 2, 0, w - NA_COLS)
    col_in = jnp.logical_and(kc >= c0, kc < c0 + NA_COLS)
    neg = jnp.full((w, w), NEG_BIG, F32)
    tiles = []
    for dr in range(ndr):
        t = neg
        for dcv in range(ndc):
            t = jnp.where(dcidx == dcv, rpb_ref[base + dr * ndc + dcv], t)
        tiles.append(jnp.where(col_in, t * LOG2E, NEG_BIG))
    for cl in range(plan.shape[0]):
        for i in range(plan.shape[1]):
            for j in range(0, plan.shape[2], 2):
                pair = [tiles[int(d)] if d >= 0 else neg for d in plan[cl, i, j:j + 2]]
                o_ref[0, cl, i * w:(i + 1) * w, j * w:(j + 2) * w] = jnp.concatenate(pair, axis=1)


def _na_bias(rpb, rows):
    h = rpb.shape[0]
    tq, tk = NA_QROWS * GRID_W, NA_KROWS * GRID_W
    return pl.pallas_call(
        functools.partial(_na_bias_kernel, plan=_na_bias_plan(rows)),
        grid=(h,),
        in_specs=[pl.BlockSpec(memory_space=pltpu.SMEM)],
        out_specs=pl.BlockSpec((1, 3, tq, tk), lambda hh: (hh, 0, 0, 0)),
        out_shape=jax.ShapeDtypeStruct((h, 3, tq, tk), F32),
        compiler_params=_cp(("parallel",)),
        name="na_bias_expand",
    )(rpb.reshape(-1))


def _na_kernel(*refs, rows, cast):
    if cast:
        q_ref, k_ref, v_ref, kc_ref, vc_ref, bias_ref, c_ref, o_ref, cb_ref, v1_ref, vc1_ref = refs
        cb_ref[...] = c_ref[...].astype(BF16)
    else:
        q_ref, k_ref, v_ref, kc_ref, vc_ref, bias_ref, o_ref, v1_ref, vc1_ref = refs
    qb = pl.program_id(1)
    lane = lax.broadcasted_iota(jnp.int32, (1, LANES), 1)
    first = lane < HEAD_DIM

    @pl.when(qb == 0)
    def _():
        one = jnp.ones((1, LANES), BF16)
        for p in range(N_PAIRS):
            cols = slice(p * LANES, (p + 1) * LANES)
            v1_ref[2 * p] = jnp.where(first, v_ref[0, :, cols], one)
            v1_ref[2 * p + 1] = jnp.where(first, one, v_ref[0, :, cols])
            vc1_ref[2 * p] = jnp.where(first, vc_ref[0, :, cols], one)
            vc1_ref[2 * p + 1] = jnp.where(first, one, vc_ref[0, :, cols])

    ws = jnp.clip(qb * NA_QROWS - NA_ROWS // 2, 0, rows - NA_KROWS)
    sl = pl.ds(pl.multiple_of(ws * GRID_W, GRID_W), NA_KROWS * GRID_W)
    for p in range(N_PAIRS):
        cols = slice(p * LANES, (p + 1) * LANES)
        q = q_ref[0, :, cols]
        kw = k_ref[0, sl, cols]
        kc = kc_ref[0, :, cols]
        pv = []
        for hh in range(2):
            mk = (first if hh == 0 else jnp.logical_not(first)).astype(BF16)
            qh = q * mk
            s_loc = _dot_nt(qh, kw) + bias_ref[2 * p + hh, 0]
            s_ctx = _dot_nt(qh, kc)
            m = jnp.maximum(jnp.max(s_loc, axis=-1, keepdims=True), jnp.max(s_ctx, axis=-1, keepdims=True))
            p_loc = jnp.exp2(s_loc - m).astype(BF16)
            p_ctx = jnp.exp2(s_ctx - m).astype(BF16)
            pv.append(_dot(p_loc, v1_ref[2 * p + hh, sl, :]) + _dot(p_ctx, vc1_ref[2 * p + hh]))
        num = jnp.where(first, pv[0], pv[1])
        den = jnp.where(first, pltpu.roll(pv[0], HEAD_DIM, 1), pltpu.roll(pv[1], HEAD_DIM, 1))
        o_ref[0, :, cols] = (num / den).astype(BF16)


def _na(proj, proj_ctx, bias, cast=None):
    b, seq, _ = proj.shape
    ctx_len = proj_ctx.shape[1]
    rows = seq // GRID_W
    nqb = rows // NA_QROWS
    tq = NA_QROWS * GRID_W
    tk = NA_KROWS * GRID_W
    nh = 2 * N_PAIRS
    cq, ck, cv = (CB_NQ * LANES // D_NA, CB_NK * LANES // D_NA, CB_NV * LANES // D_NA)

    def cls(qb):
        return jnp.where(qb == 0, 0, jnp.where(qb == nqb - 1, 2, 1))

    in_specs = [pl.BlockSpec((1, tq, D_NA), lambda bi, qb: (bi, qb, cq)),
                pl.BlockSpec((1, seq, D_NA), lambda bi, qb: (bi, 0, ck)),
                pl.BlockSpec((1, seq, D_NA), lambda bi, qb: (bi, 0, cv)),
                pl.BlockSpec((1, ctx_len, D_NA), lambda bi, qb: (bi, 0, ck)),
                pl.BlockSpec((1, ctx_len, D_NA), lambda bi, qb: (bi, 0, cv)),
                pl.BlockSpec((nh, 1, tq, tk), lambda bi, qb: (0, cls(qb), 0, 0))]
    out_specs = [pl.BlockSpec((1, tq, D_NA), lambda bi, qb: (bi, qb, 0))]
    out_shape = [jax.ShapeDtypeStruct((b, seq, D_NA), BF16)]
    args = [proj, proj, proj, proj_ctx, proj_ctx, bias]
    if cast is not None:
        cr, cc = cast.shape
        steps = b * nqb
        assert cr % (steps * 16) == 0
        blk = pl.BlockSpec((cr // steps, cc), lambda bi, qb: (bi * nqb + qb, 0))
        in_specs.append(blk)
        out_specs.append(blk)
        out_shape.append(jax.ShapeDtypeStruct(cast.shape, BF16))
        args.append(cast)
    res = pl.pallas_call(
        functools.partial(_na_kernel, rows=rows, cast=cast is not None),
        grid=(b, nqb),
        in_specs=in_specs,
        out_specs=out_specs,
        out_shape=out_shape,
        scratch_shapes=[pltpu.VMEM((nh, seq, LANES), BF16), pltpu.VMEM((nh, ctx_len, LANES), BF16)],
        compiler_params=_cp(("parallel", "arbitrary")),
        name="na_attn",
    )(*args)
    return res if cast is not None else res[0]


def _ctx_attn_kernel(q_ref, k_ref, v_ref, o_ref):
    q = q_ref[0]
    k = k_ref[0]
    v = v_ref[0]
    lane = lax.broadcasted_iota(jnp.int32, (1, LANES), 1)
    first = lane < HEAD_DIM
    o = None
    for hh in range(2):
        mk = (first if hh == 0 else jnp.logical_not(first)).astype(BF16)
        s = _dot_nt(q * mk, k)
        m = jnp.max(s, axis=-1, keepdims=True)
        p = jnp.exp(s - m)
        l = jnp.sum(p, axis=-1, keepdims=True)
        oh = _dot(p.astype(BF16), v) / l
        o = oh if o is None else jnp.where(first, o, oh)
    o_ref[0] = o.astype(BF16)


def _ctx_attn(proj_ctx):
    b, ctx_len, _ = proj_ctx.shape
    col = lambda off: pl.BlockSpec((1, ctx_len, LANES), lambda bi, p: (bi, 0, off + p))
    return pl.pallas_call(
        _ctx_attn_kernel,
        grid=(b, N_PAIRS),
        in_specs=[col(CB_NQ), col(CB_NK), col(CB_NV)],
        out_specs=pl.BlockSpec((1, ctx_len, LANES), lambda bi, p: (bi, 0, p)),
        out_shape=jax.ShapeDtypeStruct((b, ctx_len, D_NA), BF16),
        compiler_params=_cp(("parallel", "parallel")),
        name="ctx_attn",
    )(proj_ctx, proj_ctx, proj_ctx)


def _outproj_kernel(*refs, route):
    if route:
        (yr_ref, yc_ref, yn_ref, h_ref, w_ref, g1_ref, n2_ref, sh_ref, sc_ref, rw_ref, rb_ref,
         ho_ref, xn_ref, idx_ref, wt_ref) = refs
    else:
        yr_ref, yc_ref, yn_ref, h_ref, w_ref, g1_ref, n2_ref, sh_ref, sc_ref, ho_ref, xn_ref = refs
    tm = h_ref.shape[0]
    sub = min(OUT_PROJ_SUB, tm)
    if route:
        r_hi, r_lo = _split_bf16(rw_ref[...])
        r_both = jnp.concatenate([r_hi, r_lo], axis=0)
    for r in range(tm // sub):
        rs = slice(r * sub, (r + 1) * sub)
        ycat = jnp.concatenate([yr_ref[rs, :], yc_ref[rs, :], yn_ref[rs, :]], axis=-1)
        h = h_ref[rs, :] + g1_ref[0] * _dot(ycat, w_ref[...])
        ho_ref[rs, :] = h
        a = _norm_mod(h, n2_ref[...], sh_ref[0], sc_ref[0])
        xn_ref[rs, :] = a.astype(xn_ref.dtype)
        if route:
            a_hi, a_lo = _split_bf16(a)
            t_hi = _dot_nt(r_both, a_hi)
            logits = t_hi[:N_EXPERTS] + t_hi[N_EXPERTS:] + _dot_nt(r_hi, a_lo) + rb_ref[...]
            eidx = lax.broadcasted_iota(jnp.int32, logits.shape, 0)
            m1 = jnp.max(logits, axis=0, keepdims=True)
            i1 = jnp.min(jnp.where(logits == m1, eidx, N_EXPERTS), axis=0, keepdims=True)
            rest = jnp.where(eidx == i1, -jnp.inf, logits)
            m2 = jnp.max(rest, axis=0, keepdims=True)
            i2 = jnp.min(jnp.where(rest == m2, eidx, N_EXPERTS), axis=0, keepdims=True)
            e2 = jnp.exp(m2 - m1)
            den = 1.0 + e2
            idx_ref[:, rs] = jnp.concatenate([i1, i2], axis=0)
            wt_ref[:, rs] = jnp.concatenate([1.0 / den, e2 / den], axis=0)


def _out_proj(yr, yc, yn, h2, w_bf, g1, n2g, sh2, sc2, seq, tm, router=None):
    t, d = h2.shape
    tps = seq // tm
    route = router is not None
    row = lambda wdt: pl.BlockSpec((tm, wdt), lambda i: (i, 0))
    mod = pl.BlockSpec((1, 1, d), lambda i: (i // tps, 0, 0))
    in_specs = [row(D_RET), row(D_CONV), row(D_NA), row(d),
                pl.BlockSpec((d, d), lambda i: (0, 0)), mod,
                pl.BlockSpec((1, d), lambda i: (0, 0)), mod, mod]
    args = [yr, yc, yn, h2, w_bf, g1, n2g.reshape(1, d), sh2, sc2]
    out_specs = [row(d), row(d)]
    out_shape = [jax.ShapeDtypeStruct((t, d), F32), jax.ShapeDtypeStruct((t, d), BF16)]
    if route:
        rw, rb = router
        in_specs += [pl.BlockSpec((N_EXPERTS, d), lambda i: (0, 0)),
                     pl.BlockSpec((N_EXPERTS, 1), lambda i: (0, 0))]
        args += [rw.T, rb.reshape(N_EXPERTS, 1)]
        out_specs += [pl.BlockSpec((2, tm), lambda i: (0, i))] * 2
        out_shape += [jax.ShapeDtypeStruct((2, t), jnp.int32), jax.ShapeDtypeStruct((2, t), F32)]
    return pl.pallas_call(
        functools.partial(_outproj_kernel, route=route),
        grid=(t // tm,),
        in_specs=in_specs,
        out_specs=out_specs,
        out_shape=out_shape,
        compiler_params=_cp(("parallel",)),
        name="out_proj_route" if route else "out_proj_t%d" % t,
    )(*args)


def _swiglu_chunks(x, w_in_ref, w_out_ref, ff, n_chunks):
    tf = -(-ff // (n_chunks * MXU_COLS)) * MXU_COLS
    y = None
    for c0 in range(0, ff, tf):
        c1 = min(c0 + tf, ff)
        gate = _dot(x, w_in_ref[:, c0:c1])
        up = _dot(x, w_in_ref[:, ff + c0:ff + c1])
        part = _dot((_silu(gate) * up).astype(BF16), w_out_ref[c0:c1, :])
        y = part if y is None else y + part
    return y


def _ffn_kernel(*refs, ff, n_chunks, cast):
    if cast:
        x_ref, h_ref, wi_ref, wo_ref, g2_ref, c_ref, o_ref, cb_ref = refs
        cb_ref[...] = c_ref[...].astype(BF16)
    else:
        x_ref, h_ref, wi_ref, wo_ref, g2_ref, o_ref = refs
    y = _swiglu_chunks(x_ref[...], wi_ref, wo_ref, ff, n_chunks)
    o_ref[...] = h_ref[...] + g2_ref[0] * y


def _ffn(xn, h2, w_in_bf, w_out_bf, g2, seq, tm, n_chunks, cast=None):
    t, d = h2.shape
    ff = w_out_bf.shape[0]
    tps = seq // tm
    steps = t // tm
    resident = pl.Buffered(1)
    in_specs = [pl.BlockSpec((tm, d), lambda i: (i, 0)),
                pl.BlockSpec((tm, d), lambda i: (i, 0)),
                pl.BlockSpec((d, 2 * ff), lambda i: (0, 0), pipeline_mode=resident),
                pl.BlockSpec((ff, d), lambda i: (0, 0), pipeline_mode=resident),
                pl.BlockSpec((1, 1, d), lambda i: (i // tps, 0, 0))]
    out_specs = [pl.BlockSpec((tm, d), lambda i: (i, 0))]
    out_shape = [jax.ShapeDtypeStruct((t, d), F32)]
    args = [xn, h2, w_in_bf, w_out_bf, g2]
    if cast is not None:
        cr, cc = cast.shape
        assert cr % (steps * 16) == 0
        in_specs.append(pl.BlockSpec((cr // steps, cc), lambda i: (i, 0)))
        out_specs.append(pl.BlockSpec((cr // steps, cc), lambda i: (i, 0)))
        out_shape.append(jax.ShapeDtypeStruct(cast.shape, BF16))
        args.append(cast)
    res = pl.pallas_call(
        functools.partial(_ffn_kernel, ff=ff, n_chunks=n_chunks, cast=cast is not None),
        grid=(steps,),
        in_specs=in_specs,
        out_specs=out_specs,
        out_shape=out_shape,
        compiler_params=_cp(("parallel",)),
        name="ffn_t%d" % t,
    )(*args)
    return res if cast is not None else res[0]


def _dispatch_kernel(s0_ref, s1_ref, zs_ref, zv_ref, x_ref, xs_ref, slab_ref, zbuf_ref, sem, *, rows):
    @pl.when(pl.program_id(0) == 0)
    def _():
        zbuf_ref[...] = jnp.zeros_like(zbuf_ref)
        for e in range(2 * N_EXPERTS):
            @pl.when(zv_ref[e] != 0)
            def _():
                dst = xs_ref.at[pl.ds(pl.multiple_of(zs_ref[e], MOE_TS), MOE_TS)]
                pltpu.make_async_copy(zbuf_ref, dst, sem.at[1]).start()
        for e in range(2 * N_EXPERTS):
            @pl.when(zv_ref[e] != 0)
            def _():
                pltpu.make_async_copy(zbuf_ref, xs_ref.at[pl.ds(0, MOE_TS)], sem.at[1]).wait()

    nk = slab_ref.shape[1]

    def issue(g, carry):
        rs = pl.ds(pl.multiple_of(g * DMA_UNROLL, DMA_UNROLL), DMA_UNROLL)
        for k in range(nk):
            slab_ref[rs, k, :] = x_ref[rs, k * LANES:(k + 1) * LANES].astype(slab_ref.dtype)
        for u in range(DMA_UNROLL):
            r = g * DMA_UNROLL + u
            src = slab_ref.at[r]
            pltpu.make_async_copy(src, xs_ref.at[s0_ref[r]], sem.at[0]).start(priority=u % 2)
            pltpu.make_async_copy(src, xs_ref.at[s1_ref[r]], sem.at[0]).start(priority=(u + 1) % 2)
        return carry

    lax.fori_loop(0, rows // DMA_UNROLL, issue, 0)
    for _ in range(2):
        pltpu.make_async_copy(slab_ref, xs_ref.at[pl.ds(0, rows)], sem.at[0]).wait()


def _dispatch(xn, slot0, slot1, zplan, n_slots, rows):
    t, d = xn.shape
    slab = (d // LANES, LANES)
    zstart, zvalid = zplan
    return pl.pallas_call(
        functools.partial(_dispatch_kernel, rows=rows),
        grid=(t // rows,),
        in_specs=[pl.BlockSpec((rows,), lambda i: (i,), memory_space=pltpu.SMEM),
                  pl.BlockSpec((rows,), lambda i: (i,), memory_space=pltpu.SMEM),
                  pl.BlockSpec((2 * N_EXPERTS,), lambda i: (0,), memory_space=pltpu.SMEM),
                  pl.BlockSpec((2 * N_EXPERTS,), lambda i: (0,), memory_space=pltpu.SMEM),
                  pl.BlockSpec((rows, d), lambda i: (i, 0))],
        out_specs=pl.BlockSpec(memory_space=pl.ANY),
        out_shape=jax.ShapeDtypeStruct((n_slots,) + slab, F32),
        scratch_shapes=[pltpu.VMEM((rows,) + slab, F32), pltpu.VMEM((MOE_TS,) + slab, F32),
                        pltpu.SemaphoreType.DMA((2,))],
        compiler_params=_cp(("arbitrary",)),
        name="moe_dispatch",
    )(slot0, slot1, zstart, zvalid, xn)


def _expert_kernel(be_ref, nu_ref, xs_ref, wi_ref, wo_ref, o_ref, xa_ref, xb_ref, sem, *, ff, n_chunks):
    j = pl.program_id(0)
    n = pl.num_programs(0)
    ts, d = xa_ref.shape
    bufs = (xa_ref, xb_ref)

    def copies(blk, b):
        rows = pl.ds(pl.multiple_of(blk * ts, ts), ts)
        return [pltpu.make_async_copy(xs_ref.at[rows, k, :], bufs[b].at[:, k * LANES:(k + 1) * LANES],
                                      sem.at[b]) for k in range(d // LANES)]

    @pl.when(j == 0)
    def _():
        for c in copies(0, 0):
            c.start()

    def step(b):
        @pl.when(j + 1 < n)
        def _():
            for c in copies(j + 1, 1 - b):
                c.start()

        for c in copies(j, b):
            c.wait()

        @pl.when(j < nu_ref[0])
        def _():
            o_ref[...] = _swiglu_chunks(bufs[b][...].astype(BF16), wi_ref.at[0], wo_ref.at[0], ff, n_chunks)

        @pl.when(j >= nu_ref[0])
        def _():
            o_ref[...] = jnp.zeros_like(o_ref)

    for b in range(2):
        pl.when(j % 2 == b)(functools.partial(step, b))


def _experts(xs, block_expert, n_used, w_in_bf, w_out_bf, n_chunks):
    n, nk, _ = xs.shape
    d = nk * LANES
    ff = w_out_bf.shape[1]
    ts = MOE_TS
    grid_spec = pltpu.PrefetchScalarGridSpec(
        num_scalar_prefetch=2,
        grid=(n // ts,),
        in_specs=[pl.BlockSpec(memory_space=pl.ANY),
                  pl.BlockSpec((1, d, 2 * ff), lambda j, be, nu: (be[j], 0, 0)),
                  pl.BlockSpec((1, ff, d), lambda j, be, nu: (be[j], 0, 0))],
        out_specs=pl.BlockSpec((ts, d), lambda j, be, nu: (j, 0)),
        scratch_shapes=[pltpu.VMEM((ts, d), xs.dtype), pltpu.VMEM((ts, d), xs.dtype),
                        pltpu.SemaphoreType.DMA((2,))],
    )
    return pl.pallas_call(
        functools.partial(_expert_kernel, ff=ff, n_chunks=n_chunks),
        grid_spec=grid_spec,
        out_shape=jax.ShapeDtypeStruct((n, d), F32),
        compiler_params=_cp(("arbitrary",)),
        name="moe_experts",
    )(block_expert, n_used, xs, w_in_bf, w_out_bf)


def _combine_kernel(p0_ref, p1_ref, q0_ref, q1_ref, ys_ref, h_ref, w0_ref, w1_ref, g2_ref, fg_ref, o_ref,
                    ya0_ref, ya1_ref, yb0_ref, yb1_ref, sem, *, rows):
    i = pl.program_id(0)
    n = pl.num_programs(0)
    n_iter = rows // COMBINE_GROUP
    bufs = ((ya0_ref, ya1_ref), (yb0_ref, yb1_ref))

    def issue_rows(pa_ref, pb_ref, b, g):
        for u in range(COMBINE_GROUP):
            r = g * COMBINE_GROUP + u
            pltpu.make_async_copy(ys_ref.at[pl.ds(pa_ref[r], 1)], bufs[b][0].at[pl.ds(r, 1)],
                                  sem.at[0, b]).start(priority=u % 2)
            pltpu.make_async_copy(ys_ref.at[pl.ds(pb_ref[r], 1)], bufs[b][1].at[pl.ds(r, 1)],
                                  sem.at[1, b]).start(priority=(u + 1) % 2)

    def finish_rows(b, g):
        rs = pl.ds(pl.multiple_of(g * COMBINE_GROUP, COMBINE_GROUP), COMBINE_GROUP)
        y = w0_ref[rs, :] * bufs[b][0][rs, :] + w1_ref[rs, :] * bufs[b][1][rs, :]
        h = h_ref[rs, :] + g2_ref[0] * y
        ms = jnp.mean(h * h, axis=-1, keepdims=True)
        o_ref[rs, :] = h * lax.rsqrt(ms + NORM_EPS) * fg_ref[...]

    @pl.when(i == 0)
    def _():
        def first(g, carry):
            issue_rows(p0_ref, p1_ref, 0, g)
            return carry
        lax.fori_loop(0, n_iter, first, 0)

    def step(b):
        pltpu.make_async_copy(ys_ref.at[pl.ds(0, rows)], bufs[b][0], sem.at[0, b]).wait()
        pltpu.make_async_copy(ys_ref.at[pl.ds(0, rows)], bufs[b][1], sem.at[1, b]).wait()

        @pl.when(i + 1 < n)
        def _():
            def both(g, carry):
                issue_rows(q0_ref, q1_ref, 1 - b, g)
                finish_rows(b, g)
                return carry
            lax.fori_loop(0, n_iter, both, 0)

        @pl.when(i + 1 == n)
        def _():
            def last(g, carry):
                finish_rows(b, g)
                return carry
            lax.fori_loop(0, n_iter, last, 0)

    for b in range(2):
        pl.when(i % 2 == b)(functools.partial(step, b))


def _combine(pos0, pos1, ys, h2, w0, w1, g2, final_g, seq, rows):
    t, d = h2.shape
    tps = seq // rows
    steps = t // rows
    blk = pl.BlockSpec((rows, d), lambda i: (i, 0))
    col = pl.BlockSpec((rows, 1), lambda i: (i, 0))
    here = pl.BlockSpec((rows,), lambda i: (i,), memory_space=pltpu.SMEM)
    ahead = pl.BlockSpec((rows,), lambda i: (jnp.minimum(i + 1, steps - 1),), memory_space=pltpu.SMEM)
    return pl.pallas_call(
        functools.partial(_combine_kernel, rows=rows),
        grid=(steps,),
        in_specs=[here, here, ahead, ahead,
                  pl.BlockSpec(memory_space=pl.ANY),
                  blk, col, col,
                  pl.BlockSpec((1, 1, d), lambda i: (i // tps, 0, 0)),
                  pl.BlockSpec((1, d), lambda i: (0, 0))],
        out_specs=blk,
        out_shape=jax.ShapeDtypeStruct((t, d), F32),
        scratch_shapes=[pltpu.VMEM((rows, d), F32)] * 4 + [pltpu.SemaphoreType.DMA((2, 2))],
        compiler_params=_cp(("arbitrary",)),
        name="moe_combine_norm",
    )(pos0, pos1, pos0, pos1, ys, h2, w0, w1, g2, final_g.reshape(1, d))


def _route_plan(idx):
    t = idx.shape[1]
    ts = MOE_TS
    n_slots = 2 * t + N_EXPERTS * ts
    e_flat = idx.reshape(-1)
    onehot = (e_flat[:, None] == jnp.arange(N_EXPERTS, dtype=jnp.int32)[None, :]).astype(jnp.int32)
    csum = jnp.cumsum(onehot, axis=0)
    rank = jnp.sum((csum - onehot) * onehot, axis=1)
    counts = csum[-1]
    padded = ((counts + ts - 1) // ts) * ts
    ends = jnp.cumsum(padded)
    offs = ends - padded
    slot = (jnp.sum(onehot * offs[None, :], axis=1) + rank).astype(jnp.int32)
    starts = jnp.arange(n_slots // ts, dtype=jnp.int32) * ts
    block_expert = jnp.minimum(jnp.sum((starts[:, None] >= ends[None, :]).astype(jnp.int32), axis=1),
                               N_EXPERTS - 1).astype(jnp.int32)
    n_used = (ends[-1] // ts).astype(jnp.int32).reshape(1)
    tail = ends[-1] + jnp.arange(N_EXPERTS, dtype=jnp.int32) * ts
    zstart = jnp.concatenate([ends - ts, tail])
    zvalid = jnp.concatenate([padded > 0, tail < n_slots]).astype(jnp.int32)
    zstart = jnp.where(zvalid != 0, zstart, 0).astype(jnp.int32)
    return slot[:t], slot[t:], block_expert, n_used, (zstart, zvalid), n_slots


def kernel(x, c, ctx, c_ctx, ada_w, ada_b, norm1_g, norm2_g, w_in, w_out, ret_decay_logit, ret_gn_g,
           conv_w, na_rpb, ffn_w_in, ffn_w_out, moe_router_w, moe_router_b, moe_w_in, moe_w_out, final_g):
    b, seq, d = x.shape
    ctx_len = ctx.shape[1]
    depth = ada_w.shape[0]
    assert depth == 2, "the final norm is fused into the last (MoE) layer's combine step"
    rows = seq // GRID_W
    t_lat, t_ctx = b * seq, b * ctx_len
    tm_lat = min(1024, seq)
    tm_ctx = ctx_len

    c8 = jnp.zeros((8, d), F32).at[:b].set(c).at[b].set(c_ctx)
    mods = _ada(c8, ada_w, ada_b)
    tables = _rope_tables(seq)

    h = x.reshape(t_lat, d)
    hc = ctx.reshape(t_ctx, d)
    zero_state = jnp.zeros((b, N_PAIRS, LANES, LANES), F32)
    out = None
    for layer in range(depth):
        update_ctx = layer < depth - 1
        m = mods[layer].reshape(8, 6, d)
        lat = [m[:b, i].reshape(b, 1, d) for i in range(6)]
        cx = [jnp.broadcast_to(m[b, i].reshape(1, 1, d), (b, 1, d)) for i in range(6)]
        w_in_bf = w_in[layer].astype(BF16)
        w_out_bf = w_out[layer].astype(BF16)
        lgt = ret_decay_logit[layer].reshape(2, N_PAIRS, LANES // HEAD_DIM)
        lgt = jnp.repeat(lgt, HEAD_DIM, axis=2).transpose(1, 0, 2)
        gn = ret_gn_g[layer].reshape(N_PAIRS, 1, LANES)

        proj_c = _in_proj(hc, norm1_g[layer], cx[0], cx[1], w_in_bf, None, ctx_len, tm_ctx)
        proj_c = proj_c.reshape(b, ctx_len, D_IN_PROJ)
        if layer % 2 == 0:
            proj, fw_in, fw_out = _in_proj(h, norm1_g[layer], lat[0], lat[1], w_in_bf, tables, seq, tm_lat,
                                           casts=(ffn_w_in[layer // 2], ffn_w_out[layer // 2]))
        else:
            proj = _in_proj(h, norm1_g[layer], lat[0], lat[1], w_in_bf, tables, seq, tm_lat)
        proj = proj.reshape(b, seq, D_IN_PROJ)

        y_ret_c, s_fwd, s_bwd = _retention(proj_c, lgt, gn, zero_state, zero_state)
        y_ret, _, _ = _retention(proj, lgt, gn, s_fwd, s_bwd)
        y_conv = _short_conv(proj, conv_w[layer])
        if layer % 2 == 1:
            mwo = moe_w_out[layer // 2]
            y_na, mw_out_bf = _na(proj, proj_c, _na_bias(na_rpb[layer], rows), cast=mwo.reshape(-1, mwo.shape[-1]))
            mw_out_bf = mw_out_bf.reshape(mwo.shape)
        else:
            y_na = _na(proj, proj_c, _na_bias(na_rpb[layer], rows))

        if layer % 2 == 0:
            h, xn = _out_proj(y_ret.reshape(t_lat, D_RET), y_conv.reshape(t_lat, D_CONV),
                              y_na.reshape(t_lat, D_NA), h, w_out_bf, lat[2], norm2_g[layer],
                              lat[3], lat[4], seq, tm_lat)
            mw = moe_w_in[(layer + 1) // 2]
            h, mw_in_bf = _ffn(xn, h, fw_in, fw_out, lat[5], seq, min(512, seq), FFN_CHUNKS,
                               cast=mw.reshape(-1, mw.shape[-1]))
            mw_in_bf = mw_in_bf.reshape(mw.shape)
            if update_ctx:
                y_conv_c = _short_conv(proj_c, conv_w[layer])
                y_na_c = _ctx_attn(proj_c)
                hc, xnc = _out_proj(y_ret_c.reshape(t_ctx, D_RET), y_conv_c.reshape(t_ctx, D_CONV),
                                    y_na_c.reshape(t_ctx, D_NA), hc, w_out_bf, cx[2], norm2_g[layer],
                                    cx[3], cx[4], ctx_len, tm_ctx)
                hc = _ffn(xnc, hc, fw_in, fw_out, cx[5], ctx_len, tm_ctx, FFN_CHUNKS)
        else:
            e = layer // 2
            h, xn, idx, wts = _out_proj(y_ret.reshape(t_lat, D_RET), y_conv.reshape(t_lat, D_CONV),
                                        y_na.reshape(t_lat, D_NA), h, w_out_bf, lat[2], norm2_g[layer],
                                        lat[3], lat[4], seq, tm_lat,
                                        router=(moe_router_w[e], moe_router_b[e]))
            pos0, pos1, block_expert, n_used, zstart, n_slots = _route_plan(idx)
            xs = _dispatch(xn, pos0, pos1, zstart, n_slots, min(1024, seq))
            ys = _experts(xs, block_expert, n_used, mw_in_bf, mw_out_bf, FFN_CHUNKS)
            out = _combine(pos0, pos1, ys, h, wts[0].reshape(t_lat, 1), wts[1].reshape(t_lat, 1),
                           lat[5], final_g, seq, min(512, seq))
            h = out
    return out.reshape(b, seq, d)
```

```python
import functools

import numpy as np
import jax
import jax.numpy as jnp
from jax import lax
from jax.experimental import pallas as pl
from jax.experimental.pallas import tpu as pltpu

F32 = jnp.float32
BF16 = jnp.bfloat16

LANES = 128
HEAD_DIM = 64
GRID_W = 64
N_CONV_GROUPS = 4
N_RET_HEADS = 6
N_NA_HEADS = 6
D_RET = N_RET_HEADS * HEAD_DIM
D_CONV = N_CONV_GROUPS * HEAD_DIM
D_NA = N_NA_HEADS * HEAD_DIM
D_IN_PROJ = 4 * D_RET + 3 * D_CONV + 3 * D_NA
N_PAIRS = D_RET // LANES
NA_ROWS = 8
NA_COLS = 16
N_EXPERTS = 8
ROPE_BASE = 10000.0
NORM_EPS = 1e-6
NEG_BIG = -1e30
LOG2E = 1.4426950408889634

RET_CHUNK = 256
NA_QROWS = 4
NA_KROWS = 12
NA_STEP_BLOCKS = 2
MOE_TS = 512
DMA_UNROLL = 8
COMBINE_GROUP = 32
FFN_CHUNKS = 2
IN_PROJ_SUB = 512
OUT_PROJ_SUB = 512
MXU_COLS = 256
IN_PROJ_CHUNK = 3 * MXU_COLS
VMEM_LIMIT = 56 * 1024 * 1024

CB_RQ, CB_RK, CB_RV, CB_RG = 0, 3, 6, 9
CB_CB, CB_CC, CB_CX = 12, 14, 16
CB_NQ, CB_NK, CB_NV = 18, 21, 24


def _cp(sem, vmem=VMEM_LIMIT):
    return pltpu.CompilerParams(dimension_semantics=sem, vmem_limit_bytes=vmem)


def _silu(x):
    return x * (1.0 / (1.0 + jnp.exp(-x)))


def _dot(a, b):
    return jnp.dot(a, b, preferred_element_type=F32)


def _dot_nt(a, b):
    return lax.dot_general(a, b, (((1,), (1,)), ((), ())), preferred_element_type=F32)


def _dot_tn(a, b):
    return lax.dot_general(a, b, (((0,), (0,)), ((), ())), preferred_element_type=F32)


def _split_bf16(x):
    hi = x.astype(BF16)
    lo = (x - hi.astype(F32)).astype(BF16)
    return hi, lo


def _ada_kernel(c_ref, w_ref, b_ref, o_ref):
    x = _silu(c_ref[...]).astype(BF16)
    o_ref[0] = _dot(x, w_ref[0].astype(BF16)) + b_ref[0]


def _ada(c8, ada_w, ada_b):
    depth, d, n = ada_w.shape
    tn = n // 4
    return pl.pallas_call(
        _ada_kernel,
        grid=(depth, n // tn),
        in_specs=[pl.BlockSpec((8, d), lambda l, j: (0, 0)),
                  pl.BlockSpec((1, d, tn), lambda l, j: (l, 0, j)),
                  pl.BlockSpec((1, 1, tn), lambda l, j: (l, 0, j))],
        out_specs=pl.BlockSpec((1, 8, tn), lambda l, j: (l, 0, j)),
        out_shape=jax.ShapeDtypeStruct((depth, 8, n), F32),
        compiler_params=_cp(("parallel", "parallel")),
        name="ada_mod",
    )(c8, ada_w, ada_b.reshape(depth, 1, n))


def _norm_mod(x, g, sh, sc):
    ms = jnp.mean(x * x, axis=-1, keepdims=True)
    y = x * lax.rsqrt(ms + NORM_EPS) * g
    return y * (1.0 + sc) + sh


def _inproj_kernel(*refs, rope, n_cast):
    n_in = (8 if rope else 5) + n_cast
    for c_ref, cb_ref in zip(refs[n_in - n_cast:n_in], refs[n_in + 1:]):
        cb_ref[...] = c_ref[...].astype(BF16)
    if rope:
        h_ref, g_ref, sh_ref, sc_ref, w_ref, cos_ref, sa_ref, sb_ref = refs[:8]
    else:
        h_ref, g_ref, sh_ref, sc_ref, w_ref = refs[:5]
    o_ref = refs[n_in]
    tm = h_ref.shape[0]
    sub = min(IN_PROJ_SUB, tm)
    cw = IN_PROJ_CHUNK
    for r in range(tm // sub):
        rs = slice(r * sub, (r + 1) * sub)
        xn = _norm_mod(h_ref[rs, :], g_ref[...], sh_ref[0], sc_ref[0]).astype(BF16)
        for c0 in range(0, D_IN_PROJ, cw):
            c1 = min(c0 + cw, D_IN_PROJ)
            acc = _dot(xn, w_ref[:, c0:c1])
            for j in range((c1 - c0) // LANES):
                blk = acc[:, j * LANES:(j + 1) * LANES]
                cb = c0 // LANES + j
                if rope and cb < CB_RV:
                    blk = (blk * cos_ref[rs, :] + pltpu.roll(blk, 16, 1) * sa_ref[rs, :]
                           + pltpu.roll(blk, LANES - 16, 1) * sb_ref[rs, :])
                if CB_RK <= cb < CB_RV:
                    blk = blk * (HEAD_DIM ** -0.5)
                elif CB_NQ <= cb < CB_NK:
                    blk = blk * (HEAD_DIM ** -0.5 * (LOG2E if rope else 1.0))
                o_ref[rs, cb * LANES:(cb + 1) * LANES] = blk.astype(BF16)


def _in_proj(h2, g, sh, sc, w_bf, tables, seq, tm, casts=()):
    t, d = h2.shape
    tiles_per_seq = seq // tm
    steps = t // tm
    rope = tables is not None
    in_specs = [pl.BlockSpec((tm, d), lambda i: (i, 0)),
                pl.BlockSpec((1, d), lambda i: (0, 0)),
                pl.BlockSpec((1, 1, d), lambda i: (i // tiles_per_seq, 0, 0)),
                pl.BlockSpec((1, 1, d), lambda i: (i // tiles_per_seq, 0, 0)),
                pl.BlockSpec((d, D_IN_PROJ), lambda i: (0, 0))]
    args = [h2, g.reshape(1, d), sh, sc, w_bf]
    if rope:
        in_specs += [pl.BlockSpec((tm, LANES), lambda i: (i % tiles_per_seq, 0))] * 3
        args += list(tables)
    out_specs = [pl.BlockSpec((tm, D_IN_PROJ), lambda i: (i, 0))]
    out_shape = [jax.ShapeDtypeStruct((t, D_IN_PROJ), BF16)]
    for c in casts:
        cr, cc = c.shape
        assert cr % (steps * 16) == 0
        blk = pl.BlockSpec((cr // steps, cc), lambda i: (i, 0))
        in_specs.append(blk)
        out_specs.append(blk)
        out_shape.append(jax.ShapeDtypeStruct(c.shape, BF16))
        args.append(c)
    res = pl.pallas_call(
        functools.partial(_inproj_kernel, rope=rope, n_cast=len(casts)),
        grid=(steps,),
        in_specs=in_specs,
        out_specs=out_specs,
        out_shape=out_shape,
        compiler_params=_cp(("parallel",)),
        name="in_proj_rope" if rope else "in_proj_ctx",
    )(*args)
    return res if casts else res[0]


def _rope_tables(seq):
    t = np.arange(seq)
    row = (t // GRID_W).astype(np.float32)
    col = (t % GRID_W).astype(np.float32)
    n_freq = HEAD_DIM // 4
    inv_freq = (ROPE_BASE ** (-np.arange(n_freq, dtype=np.float32) / n_freq)).astype(np.float32)
    ang_r = row[:, None] * inv_freq
    ang_c = col[:, None] * inv_freq
    cos_h = np.concatenate([np.cos(ang_r), np.cos(ang_r), np.cos(ang_c), np.cos(ang_c)], axis=1)
    sin_h = np.concatenate([np.sin(ang_r), np.sin(ang_r), np.sin(ang_c), np.sin(ang_c)], axis=1)
    lane = np.arange(HEAD_DIM)
    second = (lane % 32) >= 16
    sa = np.where(second[None, :], sin_h, 0.0)
    sb = np.where(second[None, :], 0.0, -sin_h)
    tile2 = lambda a: jnp.asarray(np.concatenate([a, a], axis=1), F32)
    return tile2(cos_h), tile2(sa), tile2(sb)


def _ret_kernel(q_ref, k_ref, v_ref, g_ref, lgt_ref, gn_ref, sf0_ref, sb0_ref,
                y_ref, sfo_ref, sbo_ref, sfs_ref, sbs_ref, *, seq):
    c = RET_CHUNK
    n_chunks = seq // c
    lg = jax.nn.log_sigmoid(lgt_ref[0])
    lgf, lgb = lg[0:1, :], lg[1:2, :]
    pos = lax.broadcasted_iota(jnp.int32, (c, 1), 0).astype(F32)
    dkf = jnp.exp(lgf * (c - 1.0 - pos))
    dkb = jnp.exp(lgb * pos)
    dqf = jnp.exp(lgf * (pos + 1.0))
    dqb = jnp.exp(lgb * (c - pos))
    cdf = jnp.exp(lgf * float(c))
    cdb = jnp.exp(lgb * float(c))
    lane = lax.broadcasted_iota(jnp.int32, (1, LANES), 1)
    first = lane < HEAD_DIM
    ri = lax.broadcasted_iota(jnp.int32, (LANES, LANES), 0)
    ci = lax.broadcasted_iota(jnp.int32, (LANES, LANES), 1)
    same = (ri < HEAD_DIM) == (ci < HEAD_DIM)
    bd = same.astype(F32)

    def head_mean(x):
        a = jnp.sum(jnp.where(first, x, 0.0), axis=-1, keepdims=True)
        b = jnp.sum(jnp.where(first, 0.0, x), axis=-1, keepdims=True)
        return jnp.where(first, a, b) * (1.0 / HEAD_DIM)

    ii = lax.broadcasted_iota(jnp.int32, (c, c), 0)
    jj = lax.broadcasted_iota(jnp.int32, (c, c), 1)
    dif = (ii - jj).astype(F32)

    def decay_mask(h0):
        lf = lgf[:, h0:h0 + 1]
        lb = lgb[:, h0:h0 + 1]
        return jnp.where(dif > 0, jnp.exp(lf * jnp.maximum(dif, 0.0)),
                         jnp.where(dif < 0, jnp.exp(lb * jnp.maximum(-dif, 0.0)), 2.0))

    dm = (decay_mask(0), decay_mask(HEAD_DIM))
    mfirst = first.astype(BF16)
    msecond = (1.0 - first.astype(F32)).astype(BF16)

    unroll = min(8, n_chunks)

    def incr(n, carry):
        sl = pl.ds(pl.multiple_of(n * c, c), c)
        kf = k_ref[0, sl, :].astype(F32)
        v = v_ref[0, sl, :]
        sfs_ref[n] = _dot_tn((kf * dkf).astype(BF16), v) * bd
        sbs_ref[n] = _dot_tn((kf * dkb).astype(BF16), v) * bd
        return carry

    lax.fori_loop(0, n_chunks, incr, 0, unroll=unroll)

    def scan_f(n, s):
        u = sfs_ref[n]
        sfs_ref[n] = s
        return cdf * s + u

    def scan_b(i, s):
        n = n_chunks - 1 - i
        u = sbs_ref[n]
        sbs_ref[n] = s
        return cdb * s + u

    sfo_ref[0, 0] = lax.fori_loop(0, n_chunks, scan_f, sf0_ref[0, 0])
    sbo_ref[0, 0] = lax.fori_loop(0, n_chunks, scan_b, sb0_ref[0, 0])

    def outp(n, carry):
        sl = pl.ds(pl.multiple_of(n * c, c), c)
        q = q_ref[0, sl, :]
        k = k_ref[0, sl, :]
        v = v_ref[0, sl, :]
        qf = q.astype(F32)
        o = None
        for hh, mk in enumerate((mfirst, msecond)):
            s = _dot_nt(q * mk, k) * dm[hh]
            oh = _dot(s.astype(BF16), v)
            o = oh if o is None else jnp.where(first, o, oh)
        qcat = jnp.concatenate([(qf * dqf).astype(BF16), (qf * dqb).astype(BF16)], axis=1)
        scat = jnp.concatenate([sfs_ref[n], sbs_ref[n]], axis=0).astype(BF16)
        o = o + _dot(qcat, scat)
        mu = head_mean(o)
        dlt = o - mu
        var = head_mean(dlt * dlt)
        on = dlt * lax.rsqrt(var + NORM_EPS) * gn_ref[0]
        y_ref[0, sl, :] = (_silu(g_ref[0, sl, :].astype(F32)) * on).astype(BF16)
        return carry

    lax.fori_loop(0, n_chunks, outp, 0, unroll=unroll)


def _retention(proj, lgt, gn, sf0, sb0):
    b, seq, _ = proj.shape
    col = lambda off: pl.BlockSpec((1, seq, LANES), lambda bi, p: (bi, 0, off + p))
    st = pl.BlockSpec((1, 1, LANES, LANES), lambda bi, p: (bi, p, 0, 0))
    return pl.pallas_call(
        functools.partial(_ret_kernel, seq=seq),
        grid=(b, N_PAIRS),
        in_specs=[col(CB_RQ), col(CB_RK), col(CB_RV), col(CB_RG),
                  pl.BlockSpec((1, 2, LANES), lambda bi, p: (p, 0, 0)),
                  pl.BlockSpec((1, 1, LANES), lambda bi, p: (p, 0, 0)),
                  st, st],
        out_specs=[pl.BlockSpec((1, seq, LANES), lambda bi, p: (bi, 0, p)), st, st],
        out_shape=[jax.ShapeDtypeStruct((b, seq, D_RET), BF16),
                   jax.ShapeDtypeStruct((b, N_PAIRS, LANES, LANES), F32),
                   jax.ShapeDtypeStruct((b, N_PAIRS, LANES, LANES), F32)],
        scratch_shapes=[pltpu.VMEM((seq // RET_CHUNK, LANES, LANES), F32),
                        pltpu.VMEM((seq // RET_CHUNK, LANES, LANES), F32)],
        compiler_params=_cp(("parallel", "parallel")),
        name="retention_s%d" % seq,
    )(proj, proj, proj, proj, lgt, gn, sf0, sb0)


def _conv_kernel(b_ref, c_ref, x_ref, w_ref, y_ref, *, seq):
    u = c_ref[0].astype(F32) * x_ref[0].astype(F32)
    row = lax.broadcasted_iota(jnp.int32, (seq, 1), 0)
    prev = jnp.where(row == 0, 0.0, pltpu.roll(u, 1, 0))
    nxt = jnp.where(row == seq - 1, 0.0, pltpu.roll(u, seq - 1, 0))
    y = w_ref[0:1, :] * prev + w_ref[1:2, :] * u + w_ref[2:3, :] * nxt
    y_ref[0] = (b_ref[0].astype(F32) * y).astype(BF16)


def _short_conv(proj, conv_w):
    b, seq, _ = proj.shape
    nblk = D_CONV // LANES
    col = lambda off: pl.BlockSpec((1, seq, LANES), lambda bi, j: (bi, 0, off + j))
    return pl.pallas_call(
        functools.partial(_conv_kernel, seq=seq),
        grid=(b, nblk),
        in_specs=[col(CB_CB), col(CB_CC), col(CB_CX),
                  pl.BlockSpec((3, LANES), lambda bi, j: (0, j))],
        out_specs=pl.BlockSpec((1, seq, LANES), lambda bi, j: (bi, 0, j)),
        out_shape=jax.ShapeDtypeStruct((b, seq, D_CONV), BF16),
        compiler_params=_cp(("parallel", "parallel")),
        name="short_conv_s%d" % seq,
    )(proj, proj, proj, conv_w)


def _na_bias_plan(rows):
    plan = np.full((3, NA_QROWS, NA_KROWS), -1, np.int64)
    for cl, rb in enumerate((0, NA_QROWS, rows - NA_QROWS)):
        ws = int(np.clip(rb - NA_ROWS // 2, 0, rows - NA_KROWS))
        for i in range(NA_QROWS):
            r = rb + i
            r0 = int(np.clip(r - NA_ROWS // 2, 0, rows - NA_ROWS))
            for j in range(NA_KROWS):
                kr = ws + j
                if r0 <= kr < r0 + NA_ROWS:
                    plan[cl, i, j] = kr - r + (NA_ROWS - 1)
    return plan


def _na_bias_kernel(rpb_ref, o_ref, *, plan):
    w = GRID_W
    ndr, ndc = 2 * NA_ROWS - 1, 2 * NA_COLS - 1
    base = pl.program_id(0) * (ndr * ndc)
    c = lax.broadcasted_iota(jnp.int32, (w, w), 0)
    kc = lax.broadcasted_iota(jnp.int32, (w, w), 1)
    dcidx = jnp.clip(kc - c, -(NA_COLS - 1), NA_COLS - 1) + (NA_COLS - 1)
    c0 = jnp.clip(c - NA_COLS // 2, 0, w - NA_COLS)
    col_in = jnp.logical_and(kc >= c0, kc < c0 + NA_COLS)
    neg = jnp.full((w, w), NEG_BIG, F32)
    tiles = []
    for dr in range(ndr):
        t = neg
        for dcv in range(ndc):
            t = jnp.where(dcidx == dcv, rpb_ref[base + dr * ndc + dcv], t)
        tiles.append(jnp.where(col_in, t * LOG2E, NEG_BIG))
    for cl in range(plan.shape[0]):
        for i in range(plan.shape[1]):
            for j in range(0, plan.shape[2], 2):
                pair = [tiles[int(d)] if d >= 0 else neg for d in plan[cl, i, j:j + 2]]
                o_ref[0, cl, i * w:(i + 1) * w, j * w:(j + 2) * w] = jnp.concatenate(pair, axis=1)


def _na_bias(rpb, rows):
    h = rpb.shape[0]
    tq, tk = NA_QROWS * GRID_W, NA_KROWS * GRID_W
    return pl.pallas_call(
        functools.partial(_na_bias_kernel, plan=_na_bias_plan(rows)),
        grid=(h,),
        in_specs=[pl.BlockSpec(memory_space=pltpu.SMEM)],
        out_specs=pl.BlockSpec((1, 3, tq, tk), lambda hh: (hh, 0, 0, 0)),
        out_shape=jax.ShapeDtypeStruct((h, 3, tq, tk), F32),
        compiler_params=_cp(("parallel",)),
        name="na_bias_expand",
    )(rpb.reshape(-1))


def _na_kernel(*refs, rows, cast):
    if cast:
        q_ref, k_ref, v_ref, kc_ref, vc_ref, bias_ref, c_ref, o_ref, cb_ref, v1_ref, vc1_ref = refs
        cb_ref[...] = c_ref[...].astype(BF16)
    else:
        q_ref, k_ref, v_ref, kc_ref, vc_ref, bias_ref, o_ref, v1_ref, vc1_ref = refs
    step = pl.program_id(1)
    nqb = rows // NA_QROWS
    tq = NA_QROWS * GRID_W
    lane = lax.broadcasted_iota(jnp.int32, (1, LANES), 1)
    first = lane < HEAD_DIM

    @pl.when(step == 0)
    def _():
        one = jnp.ones((1, LANES), BF16)
        for p in range(N_PAIRS):
            cols = slice(p * LANES, (p + 1) * LANES)
            v1_ref[2 * p] = jnp.where(first, v_ref[0, :, cols], one)
            v1_ref[2 * p + 1] = jnp.where(first, one, v_ref[0, :, cols])
            vc1_ref[2 * p] = jnp.where(first, vc_ref[0, :, cols], one)
            vc1_ref[2 * p + 1] = jnp.where(first, one, vc_ref[0, :, cols])

    for sb in range(NA_STEP_BLOCKS):
        qb = step * NA_STEP_BLOCKS + sb
        cls = jnp.where(qb == 0, 0, jnp.where(qb == nqb - 1, 2, 1))
        ws = jnp.clip(qb * NA_QROWS - NA_ROWS // 2, 0, rows - NA_KROWS)
        sl = pl.ds(pl.multiple_of(ws * GRID_W, GRID_W), NA_KROWS * GRID_W)
        qrows = slice(sb * tq, (sb + 1) * tq)
        for p in range(N_PAIRS):
            cols = slice(p * LANES, (p + 1) * LANES)
            q = q_ref[0, qrows, cols]
            kw = k_ref[0, sl, cols]
            kc = kc_ref[0, :, cols]
            pv = []
            for hh in range(2):
                mk = (first if hh == 0 else jnp.logical_not(first)).astype(BF16)
                qh = q * mk
                s_loc = _dot_nt(qh, kw) + bias_ref[2 * p + hh, cls]
                s_ctx = _dot_nt(qh, kc)
                m = jnp.maximum(jnp.max(s_loc, axis=-1, keepdims=True), jnp.max(s_ctx, axis=-1, keepdims=True))
                p_loc = jnp.exp2(s_loc - m).astype(BF16)
                p_ctx = jnp.exp2(s_ctx - m).astype(BF16)
                pv.append(_dot(p_loc, v1_ref[2 * p + hh, sl, :]) + _dot(p_ctx, vc1_ref[2 * p + hh]))
            num = jnp.where(first, pv[0], pv[1])
            den = jnp.where(first, pltpu.roll(pv[0], HEAD_DIM, 1), pltpu.roll(pv[1], HEAD_DIM, 1))
            o_ref[0, qrows, cols] = (num / den).astype(BF16)


def _na(proj, proj_ctx, bias, cast=None):
    b, seq, _ = proj.shape
    ctx_len = proj_ctx.shape[1]
    rows = seq // GRID_W
    nqb = rows // NA_QROWS
    tq = NA_STEP_BLOCKS * NA_QROWS * GRID_W
    tk = NA_KROWS * GRID_W
    nh = 2 * N_PAIRS
    n_steps = nqb // NA_STEP_BLOCKS
    assert n_steps * NA_STEP_BLOCKS == nqb
    cq, ck, cv = (CB_NQ * LANES // D_NA, CB_NK * LANES // D_NA, CB_NV * LANES // D_NA)
    in_specs = [pl.BlockSpec((1, tq, D_NA), lambda bi, st: (bi, st, cq)),
                pl.BlockSpec((1, seq, D_NA), lambda bi, st: (bi, 0, ck)),
                pl.BlockSpec((1, seq, D_NA), lambda bi, st: (bi, 0, cv)),
                pl.BlockSpec((1, ctx_len, D_NA), lambda bi, st: (bi, 0, ck)),
                pl.BlockSpec((1, ctx_len, D_NA), lambda bi, st: (bi, 0, cv)),
                pl.BlockSpec(bias.shape, lambda bi, st: (0, 0, 0, 0), pipeline_mode=pl.Buffered(1))]
    out_specs = [pl.BlockSpec((1, tq, D_NA), lambda bi, st: (bi, st, 0))]
    out_shape = [jax.ShapeDtypeStruct((b, seq, D_NA), BF16)]
    args = [proj, proj, proj, proj_ctx, proj_ctx, bias]
    if cast is not None:
        cr, cc = cast.shape
        steps = b * n_steps
        assert cr % (steps * 16) == 0
        blk = pl.BlockSpec((cr // steps, cc), lambda bi, st: (bi * n_steps + st, 0))
        in_specs.append(blk)
        out_specs.append(blk)
        out_shape.append(jax.ShapeDtypeStruct(cast.shape, BF16))
        args.append(cast)
    res = pl.pallas_call(
        functools.partial(_na_kernel, rows=rows, cast=cast is not None),
        grid=(b, n_steps),
        in_specs=in_specs,
        out_specs=out_specs,
        out_shape=out_shape,
        scratch_shapes=[pltpu.VMEM((nh, seq, LANES), BF16), pltpu.VMEM((nh, ctx_len, LANES), BF16)],
        compiler_params=_cp(("parallel", "arbitrary")),
        name="na_attn",
    )(*args)
    return res if cast is not None else res[0]


def _ctx_attn_kernel(q_ref, k_ref, v_ref, o_ref):
    q = q_ref[0]
    k = k_ref[0]
    v = v_ref[0]
    lane = lax.broadcasted_iota(jnp.int32, (1, LANES), 1)
    first = lane < HEAD_DIM
    o = None
    for hh in range(2):
        mk = (first if hh == 0 else jnp.logical_not(first)).astype(BF16)
        s = _dot_nt(q * mk, k)
        m = jnp.max(s, axis=-1, keepdims=True)
        p = jnp.exp(s - m)
        l = jnp.sum(p, axis=-1, keepdims=True)
        oh = _dot(p.astype(BF16), v) / l
        o = oh if o is None else jnp.where(first, o, oh)
    o_ref[0] = o.astype(BF16)


def _ctx_attn(proj_ctx):
    b, ctx_len, _ = proj_ctx.shape
    col = lambda off: pl.BlockSpec((1, ctx_len, LANES), lambda bi, p: (bi, 0, off + p))
    return pl.pallas_call(
        _ctx_attn_kernel,
        grid=(b, N_PAIRS),
        in_specs=[col(CB_NQ), col(CB_NK), col(CB_NV)],
        out_specs=pl.BlockSpec((1, ctx_len, LANES), lambda bi, p: (bi, 0, p)),
        out_shape=jax.ShapeDtypeStruct((b, ctx_len, D_NA), BF16),
        compiler_params=_cp(("parallel", "parallel")),
        name="ctx_attn",
    )(proj_ctx, proj_ctx, proj_ctx)


def _outproj_kernel(*refs, route):
    if route:
        (yr_ref, yc_ref, yn_ref, h_ref, w_ref, g1_ref, n2_ref, sh_ref, sc_ref, rw_ref, rb_ref,
         ho_ref, xn_ref, idx_ref, wt_ref) = refs
    else:
        yr_ref, yc_ref, yn_ref, h_ref, w_ref, g1_ref, n2_ref, sh_ref, sc_ref, ho_ref, xn_ref = refs
    tm = h_ref.shape[0]
    sub = min(OUT_PROJ_SUB, tm)
    if route:
        r_hi, r_lo = _split_bf16(rw_ref[...])
        r_both = jnp.concatenate([r_hi, r_lo], axis=0)
    for r in range(tm // sub):
        rs = slice(r * sub, (r + 1) * sub)
        ycat = jnp.concatenate([yr_ref[rs, :], yc_ref[rs, :], yn_ref[rs, :]], axis=-1)
        h = h_ref[rs, :] + g1_ref[0] * _dot(ycat, w_ref[...])
        ho_ref[rs, :] = h
        a = _norm_mod(h, n2_ref[...], sh_ref[0], sc_ref[0])
        xn_ref[rs, :] = a.astype(xn_ref.dtype)
        if route:
            a_hi, a_lo = _split_bf16(a)
            t_hi = _dot_nt(r_both, a_hi)
            logits = t_hi[:N_EXPERTS] + t_hi[N_EXPERTS:] + _dot_nt(r_hi, a_lo) + rb_ref[...]
            eidx = lax.broadcasted_iota(jnp.int32, logits.shape, 0)
            m1 = jnp.max(logits, axis=0, keepdims=True)
            i1 = jnp.min(jnp.where(logits == m1, eidx, N_EXPERTS), axis=0, keepdims=True)
            rest = jnp.where(eidx == i1, -jnp.inf, logits)
            m2 = jnp.max(rest, axis=0, keepdims=True)
            i2 = jnp.min(jnp.where(rest == m2, eidx, N_EXPERTS), axis=0, keepdims=True)
            e2 = jnp.exp(m2 - m1)
            den = 1.0 + e2
            idx_ref[:, rs] = jnp.concatenate([i1, i2], axis=0)
            wt_ref[:, rs] = jnp.concatenate([1.0 / den, e2 / den], axis=0)


def _out_proj(yr, yc, yn, h2, w_bf, g1, n2g, sh2, sc2, seq, tm, router=None):
    t, d = h2.shape
    tps = seq // tm
    route = router is not None
    row = lambda wdt: pl.BlockSpec((tm, wdt), lambda i: (i, 0))
    mod = pl.BlockSpec((1, 1, d), lambda i: (i // tps, 0, 0))
    in_specs = [row(D_RET), row(D_CONV), row(D_NA), row(d),
                pl.BlockSpec((d, d), lambda i: (0, 0)), mod,
                pl.BlockSpec((1, d), lambda i: (0, 0)), mod, mod]
    args = [yr, yc, yn, h2, w_bf, g1, n2g.reshape(1, d), sh2, sc2]
    out_specs = [row(d), row(d)]
    out_shape = [jax.ShapeDtypeStruct((t, d), F32), jax.ShapeDtypeStruct((t, d), BF16)]
    if route:
        rw, rb = router
        in_specs += [pl.BlockSpec((N_EXPERTS, d), lambda i: (0, 0)),
                     pl.BlockSpec((N_EXPERTS, 1), lambda i: (0, 0))]
        args += [rw.T, rb.reshape(N_EXPERTS, 1)]
        out_specs += [pl.BlockSpec((2, tm), lambda i: (0, i))] * 2
        out_shape += [jax.ShapeDtypeStruct((2, t), jnp.int32), jax.ShapeDtypeStruct((2, t), F32)]
    return pl.pallas_call(
        functools.partial(_outproj_kernel, route=route),
        grid=(t // tm,),
        in_specs=in_specs,
        out_specs=out_specs,
        out_shape=out_shape,
        compiler_params=_cp(("parallel",)),
        name="out_proj_route" if route else "out_proj_t%d" % t,
    )(*args)


def _swiglu_chunks(x, w_in_ref, w_out_ref, ff, n_chunks):
    tf = -(-ff // (n_chunks * MXU_COLS)) * MXU_COLS
    y = None
    for c0 in range(0, ff, tf):
        c1 = min(c0 + tf, ff)
        gate = _dot(x, w_in_ref[:, c0:c1])
        up = _dot(x, w_in_ref[:, ff + c0:ff + c1])
        part = _dot((_silu(gate) * up).astype(BF16), w_out_ref[c0:c1, :])
        y = part if y is None else y + part
    return y


def _outproj_ffn_kernel(yr_ref, yc_ref, yn_ref, h_ref, wo_ref, g1_ref, n2_ref, sh_ref, sc_ref,
                        wi_ref, wf_ref, g2_ref, o_ref, *, ff, n_chunks):
    ycat = jnp.concatenate([yr_ref[...], yc_ref[...], yn_ref[...]], axis=-1)
    h1 = h_ref[...] + g1_ref[0] * _dot(ycat, wo_ref[...])
    a = _norm_mod(h1, n2_ref[...], sh_ref[0], sc_ref[0]).astype(BF16)
    o_ref[...] = h1 + g2_ref[0] * _swiglu_chunks(a, wi_ref, wf_ref, ff, n_chunks)


def _outproj_ffn(yr, yc, yn, h2, wo_bf, g1, n2g, sh2, sc2, fw_in_bf, fw_out_bf, g2, seq, tm, n_chunks):
    t, d = h2.shape
    ff = fw_out_bf.shape[0]
    tps = seq // tm
    resident = pl.Buffered(1)
    row = lambda wdt: pl.BlockSpec((tm, wdt), lambda i: (i, 0))
    mod = pl.BlockSpec((1, 1, d), lambda i: (i // tps, 0, 0))
    return pl.pallas_call(
        functools.partial(_outproj_ffn_kernel, ff=ff, n_chunks=n_chunks),
        grid=(t // tm,),
        in_specs=[row(D_RET), row(D_CONV), row(D_NA), row(d),
                  pl.BlockSpec((d, d), lambda i: (0, 0), pipeline_mode=resident), mod,
                  pl.BlockSpec((1, d), lambda i: (0, 0)), mod, mod,
                  pl.BlockSpec((d, 2 * ff), lambda i: (0, 0), pipeline_mode=resident),
                  pl.BlockSpec((ff, d), lambda i: (0, 0), pipeline_mode=resident), mod],
        out_specs=row(d),
        out_shape=jax.ShapeDtypeStruct((t, d), F32),
        compiler_params=_cp(("parallel",)),
        name="out_proj_ffn_t%d" % t,
    )(yr, yc, yn, h2, wo_bf, g1, n2g.reshape(1, d), sh2, sc2, fw_in_bf, fw_out_bf, g2)


def _dispatch_kernel(s0_ref, s1_ref, zs_ref, zv_ref, x_ref, xs_ref, slab_ref, zbuf_ref, sem, *, rows):
    @pl.when(pl.program_id(0) == 0)
    def _():
        zbuf_ref[...] = jnp.zeros_like(zbuf_ref)
        for e in range(2 * N_EXPERTS):
            @pl.when(zv_ref[e] != 0)
            def _():
                dst = xs_ref.at[pl.ds(pl.multiple_of(zs_ref[e], MOE_TS), MOE_TS)]
                pltpu.make_async_copy(zbuf_ref, dst, sem.at[1]).start()
        for e in range(2 * N_EXPERTS):
            @pl.when(zv_ref[e] != 0)
            def _():
                pltpu.make_async_copy(zbuf_ref, xs_ref.at[pl.ds(0, MOE_TS)], sem.at[1]).wait()

    nk = slab_ref.shape[1]

    def issue(g, carry):
        rs = pl.ds(pl.multiple_of(g * DMA_UNROLL, DMA_UNROLL), DMA_UNROLL)
        for k in range(nk):
            slab_ref[rs, k, :] = x_ref[rs, k * LANES:(k + 1) * LANES].astype(slab_ref.dtype)
        for u in range(DMA_UNROLL):
            r = g * DMA_UNROLL + u
            src = slab_ref.at[r]
            pltpu.make_async_copy(src, xs_ref.at[s0_ref[r]], sem.at[0]).start(priority=u % 2)
            pltpu.make_async_copy(src, xs_ref.at[s1_ref[r]], sem.at[0]).start(priority=(u + 1) % 2)
        return carry

    lax.fori_loop(0, rows // DMA_UNROLL, issue, 0)
    for _ in range(2):
        pltpu.make_async_copy(slab_ref, xs_ref.at[pl.ds(0, rows)], sem.at[0]).wait()


def _dispatch(xn, slot0, slot1, zplan, n_slots, rows):
    t, d = xn.shape
    slab = (d // LANES, LANES)
    zstart, zvalid = zplan
    return pl.pallas_call(
        functools.partial(_dispatch_kernel, rows=rows),
        grid=(t // rows,),
        in_specs=[pl.BlockSpec((rows,), lambda i: (i,), memory_space=pltpu.SMEM),
                  pl.BlockSpec((rows,), lambda i: (i,), memory_space=pltpu.SMEM),
                  pl.BlockSpec((2 * N_EXPERTS,), lambda i: (0,), memory_space=pltpu.SMEM),
                  pl.BlockSpec((2 * N_EXPERTS,), lambda i: (0,), memory_space=pltpu.SMEM),
                  pl.BlockSpec((rows, d), lambda i: (i, 0))],
        out_specs=pl.BlockSpec(memory_space=pl.ANY),
        out_shape=jax.ShapeDtypeStruct((n_slots,) + slab, F32),
        scratch_shapes=[pltpu.VMEM((rows,) + slab, F32), pltpu.VMEM((MOE_TS,) + slab, F32),
                        pltpu.SemaphoreType.DMA((2,))],
        compiler_params=_cp(("arbitrary",)),
        name="moe_dispatch",
    )(slot0, slot1, zstart, zvalid, xn)


def _expert_kernel(be_ref, nu_ref, xs_ref, wi_ref, wo_ref, o_ref, xa_ref, xb_ref, sem, *, ff, n_chunks):
    j = pl.program_id(0)
    n = pl.num_programs(0)
    ts, d = xa_ref.shape
    bufs = (xa_ref, xb_ref)

    def copies(blk, b):
        rows = pl.ds(pl.multiple_of(blk * ts, ts), ts)
        return [pltpu.make_async_copy(xs_ref.at[rows, k, :], bufs[b].at[:, k * LANES:(k + 1) * LANES],
                                      sem.at[b]) for k in range(d // LANES)]

    @pl.when(j == 0)
    def _():
        for c in copies(0, 0):
            c.start()

    def step(b):
        @pl.when(j + 1 < n)
        def _():
            for c in copies(j + 1, 1 - b):
                c.start()

        for c in copies(j, b):
            c.wait()

        @pl.when(j < nu_ref[0])
        def _():
            o_ref[...] = _swiglu_chunks(bufs[b][...].astype(BF16), wi_ref.at[0], wo_ref.at[0], ff, n_chunks)

        @pl.when(j >= nu_ref[0])
        def _():
            o_ref[...] = jnp.zeros_like(o_ref)

    for b in range(2):
        pl.when(j % 2 == b)(functools.partial(step, b))


def _experts(xs, block_expert, n_used, w_in_bf, w_out_bf, n_chunks):
    n, nk, _ = xs.shape
    d = nk * LANES
    ff = w_out_bf.shape[1]
    ts = MOE_TS
    grid_spec = pltpu.PrefetchScalarGridSpec(
        num_scalar_prefetch=2,
        grid=(n // ts,),
        in_specs=[pl.BlockSpec(memory_space=pl.ANY),
                  pl.BlockSpec((1, d, 2 * ff), lambda j, be, nu: (be[j], 0, 0)),
                  pl.BlockSpec((1, ff, d), lambda j, be, nu: (be[j], 0, 0))],
        out_specs=pl.BlockSpec((ts, d), lambda j, be, nu: (j, 0)),
        scratch_shapes=[pltpu.VMEM((ts, d), xs.dtype), pltpu.VMEM((ts, d), xs.dtype),
                        pltpu.SemaphoreType.DMA((2,))],
    )
    return pl.pallas_call(
        functools.partial(_expert_kernel, ff=ff, n_chunks=n_chunks),
        grid_spec=grid_spec,
        out_shape=jax.ShapeDtypeStruct((n, d), F32),
        compiler_params=_cp(("arbitrary",)),
        name="moe_experts",
    )(block_expert, n_used, xs, w_in_bf, w_out_bf)


def _combine_kernel(p0_ref, p1_ref, q0_ref, q1_ref, ys_ref, h_ref, w0_ref, w1_ref, g2_ref, fg_ref, o_ref,
                    ya0_ref, ya1_ref, yb0_ref, yb1_ref, sem, *, rows):
    i = pl.program_id(0)
    n = pl.num_programs(0)
    n_iter = rows // COMBINE_GROUP
    bufs = ((ya0_ref, ya1_ref), (yb0_ref, yb1_ref))

    def issue_rows(pa_ref, pb_ref, b, g):
        for u in range(COMBINE_GROUP):
            r = g * COMBINE_GROUP + u
            pltpu.make_async_copy(ys_ref.at[pl.ds(pa_ref[r], 1)], bufs[b][0].at[pl.ds(r, 1)],
                                  sem.at[0, b]).start(priority=u % 2)
            pltpu.make_async_copy(ys_ref.at[pl.ds(pb_ref[r], 1)], bufs[b][1].at[pl.ds(r, 1)],
                                  sem.at[1, b]).start(priority=(u + 1) % 2)

    def finish_rows(b, g):
        rs = pl.ds(pl.multiple_of(g * COMBINE_GROUP, COMBINE_GROUP), COMBINE_GROUP)
        y = w0_ref[rs, :] * bufs[b][0][rs, :] + w1_ref[rs, :] * bufs[b][1][rs, :]
        h = h_ref[rs, :] + g2_ref[0] * y
        ms = jnp.mean(h * h, axis=-1, keepdims=True)
        o_ref[rs, :] = h * lax.rsqrt(ms + NORM_EPS) * fg_ref[...]

    @pl.when(i == 0)
    def _():
        def first(g, carry):
            issue_rows(p0_ref, p1_ref, 0, g)
            return carry
        lax.fori_loop(0, n_iter, first, 0)

    def step(b):
        pltpu.make_async_copy(ys_ref.at[pl.ds(0, rows)], bufs[b][0], sem.at[0, b]).wait()
        pltpu.make_async_copy(ys_ref.at[pl.ds(0, rows)], bufs[b][1], sem.at[1, b]).wait()

        @pl.when(i + 1 < n)
        def _():
            def both(g, carry):
                issue_rows(q0_ref, q1_ref, 1 - b, g)
                finish_rows(b, g)
                return carry
            lax.fori_loop(0, n_iter, both, 0)

        @pl.when(i + 1 == n)
        def _():
            def last(g, carry):
                finish_rows(b, g)
                return carry
            lax.fori_loop(0, n_iter, last, 0)

    for b in range(2):
        pl.when(i % 2 == b)(functools.partial(step, b))


def _combine(pos0, pos1, ys, h2, w0, w1, g2, final_g, seq, rows):
    t, d = h2.shape
    tps = seq // rows
    steps = t // rows
    blk = pl.BlockSpec((rows, d), lambda i: (i, 0))
    col = pl.BlockSpec((rows, 1), lambda i: (i, 0))
    here = pl.BlockSpec((rows,), lambda i: (i,), memory_space=pltpu.SMEM)
    ahead = pl.BlockSpec((rows,), lambda i: (jnp.minimum(i + 1, steps - 1),), memory_space=pltpu.SMEM)
    return pl.pallas_call(
        functools.partial(_combine_kernel, rows=rows),
        grid=(steps,),
        in_specs=[here, here, ahead, ahead,
                  pl.BlockSpec(memory_space=pl.ANY),
                  blk, col, col,
                  pl.BlockSpec((1, 1, d), lambda i: (i // tps, 0, 0)),
                  pl.BlockSpec((1, d), lambda i: (0, 0))],
        out_specs=blk,
        out_shape=jax.ShapeDtypeStruct((t, d), F32),
        scratch_shapes=[pltpu.VMEM((rows, d), F32)] * 4 + [pltpu.SemaphoreType.DMA((2, 2))],
        compiler_params=_cp(("arbitrary",)),
        name="moe_combine_norm",
    )(pos0, pos1, pos0, pos1, ys, h2, w0, w1, g2, final_g.reshape(1, d))


def _route_plan(idx):
    t = idx.shape[1]
    ts = MOE_TS
    n_slots = 2 * t + N_EXPERTS * ts
    e_flat = idx.reshape(-1)
    onehot = (e_flat[:, None] == jnp.arange(N_EXPERTS, dtype=jnp.int32)[None, :]).astype(jnp.int32)
    csum = jnp.cumsum(onehot, axis=0)
    rank = jnp.sum((csum - onehot) * onehot, axis=1)
    counts = csum[-1]
    padded = ((counts + ts - 1) // ts) * ts
    ends = jnp.cumsum(padded)
    offs = ends - padded
    slot = (jnp.sum(onehot * offs[None, :], axis=1) + rank).astype(jnp.int32)
    starts = jnp.arange(n_slots // ts, dtype=jnp.int32) * ts
    block_expert = jnp.minimum(jnp.sum((starts[:, None] >= ends[None, :]).astype(jnp.int32), axis=1),
                               N_EXPERTS - 1).astype(jnp.int32)
    n_used = (ends[-1] // ts).astype(jnp.int32).reshape(1)
    tail = ends[-1] + jnp.arange(N_EXPERTS, dtype=jnp.int32) * ts
    zstart = jnp.concatenate([ends - ts, tail])
    zvalid = jnp.concatenate([padded > 0, tail < n_slots]).astype(jnp.int32)
    zstart = jnp.where(zvalid != 0, zstart, 0).astype(jnp.int32)
    return slot[:t], slot[t:], block_expert, n_used, (zstart, zvalid), n_slots


def kernel(x, c, ctx, c_ctx, ada_w, ada_b, norm1_g, norm2_g, w_in, w_out, ret_decay_logit, ret_gn_g,
           conv_w, na_rpb, ffn_w_in, ffn_w_out, moe_router_w, moe_router_b, moe_w_in, moe_w_out, final_g):
    b, seq, d = x.shape
    ctx_len = ctx.shape[1]
    depth = ada_w.shape[0]
    assert depth == 2, "the final norm is fused into the last (MoE) layer's combine step"
    rows = seq // GRID_W
    t_lat, t_ctx = b * seq, b * ctx_len
    tm_lat = min(1024, seq)
    tm_ctx = ctx_len

    c8 = jnp.zeros((8, d), F32).at[:b].set(c).at[b].set(c_ctx)
    mods = _ada(c8, ada_w, ada_b)
    tables = _rope_tables(seq)

    h = x.reshape(t_lat, d)
    hc = ctx.reshape(t_ctx, d)
    zero_state = jnp.zeros((b, N_PAIRS, LANES, LANES), F32)
    out = None
    for layer in range(depth):
        update_ctx = layer < depth - 1
        m = mods[layer].reshape(8, 6, d)
        lat = [m[:b, i].reshape(b, 1, d) for i in range(6)]
        cx = [jnp.broadcast_to(m[b, i].reshape(1, 1, d), (b, 1, d)) for i in range(6)]
        w_in_bf = w_in[layer].astype(BF16)
        w_out_bf = w_out[layer].astype(BF16)
        lgt = ret_decay_logit[layer].reshape(2, N_PAIRS, LANES // HEAD_DIM)
        lgt = jnp.repeat(lgt, HEAD_DIM, axis=2).transpose(1, 0, 2)
        gn = ret_gn_g[layer].reshape(N_PAIRS, 1, LANES)

        proj_c = _in_proj(hc, norm1_g[layer], cx[0], cx[1], w_in_bf, None, ctx_len, tm_ctx)
        proj_c = proj_c.reshape(b, ctx_len, D_IN_PROJ)
        if layer % 2 == 0:
            proj, fw_in, fw_out = _in_proj(h, norm1_g[layer], lat[0], lat[1], w_in_bf, tables, seq, tm_lat,
                                           casts=(ffn_w_in[layer // 2], ffn_w_out[layer // 2]))
        else:
            proj = _in_proj(h, norm1_g[layer], lat[0], lat[1], w_in_bf, tables, seq, tm_lat)
        proj = proj.reshape(b, seq, D_IN_PROJ)

        y_ret_c, s_fwd, s_bwd = _retention(proj_c, lgt, gn, zero_state, zero_state)
        y_ret, _, _ = _retention(proj, lgt, gn, s_fwd, s_bwd)
        y_conv = _short_conv(proj, conv_w[layer])
        mw = moe_w_in[(layer + 1) // 2] if layer % 2 == 0 else moe_w_out[layer // 2]
        y_na, mw_bf = _na(proj, proj_c, _na_bias(na_rpb[layer], rows), cast=mw.reshape(-1, mw.shape[-1]))
        if layer % 2 == 0:
            mw_in_bf = mw_bf.reshape(mw.shape)
        else:
            mw_out_bf = mw_bf.reshape(mw.shape)

        if layer % 2 == 0:
            h = _outproj_ffn(y_ret.reshape(t_lat, D_RET), y_conv.reshape(t_lat, D_CONV),
                             y_na.reshape(t_lat, D_NA), h, w_out_bf, lat[2], norm2_g[layer], lat[3], lat[4],
                             fw_in, fw_out, lat[5], seq, min(512, seq), FFN_CHUNKS)
            if update_ctx:
                y_conv_c = _short_conv(proj_c, conv_w[layer])
                y_na_c = _ctx_attn(proj_c)
                hc = _outproj_ffn(y_ret_c.reshape(t_ctx, D_RET), y_conv_c.reshape(t_ctx, D_CONV),
                                  y_na_c.reshape(t_ctx, D_NA), hc, w_out_bf, cx[2], norm2_g[layer], cx[3], cx[4],
                                  fw_in, fw_out, cx[5], ctx_len, tm_ctx, FFN_CHUNKS)
        else:
            e = layer // 2
            h, xn, idx, wts = _out_proj(y_ret.reshape(t_lat, D_RET), y_conv.reshape(t_lat, D_CONV),
                                        y_na.reshape(t_lat, D_NA), h, w_out_bf, lat[2], norm2_g[layer],
                                        lat[3], lat[4], seq, tm_lat,
                                        router=(moe_router_w[e], moe_router_b[e]))
            pos0, pos1, block_expert, n_used, zstart, n_slots = _route_plan(idx)
            xs = _dispatch(xn, pos0, pos1, zstart, n_slots, min(1024, seq))
            ys = _experts(xs, block_expert, n_used, mw_in_bf, mw_out_bf, FFN_CHUNKS)
            out = _combine(pos0, pos1, ys, h, wts[0].reshape(t_lat, 1), wts[1].reshape(t_lat, 1),
                           lat[5], final_g, seq, min(512, seq))
            h = out
    return out.reshape(b, seq, d)
```

```python
import functools

import numpy as np
import jax
import jax.numpy as jnp
from jax import lax
from jax.experimental import pallas as pl
from jax.experimental.pallas import tpu as pltpu

F32 = jnp.float32
BF16 = jnp.bfloat16

LANES = 128
HEAD_DIM = 64
GRID_W = 64
N_CONV_GROUPS = 4
N_RET_HEADS = 6
N_NA_HEADS = 6
D_RET = N_RET_HEADS * HEAD_DIM
D_CONV = N_CONV_GROUPS * HEAD_DIM
D_NA = N_NA_HEADS * HEAD_DIM
D_IN_PROJ = 4 * D_RET + 3 * D_CONV + 3 * D_NA
N_PAIRS = D_RET // LANES
NA_ROWS = 8
NA_COLS = 16
N_EXPERTS = 8
ROPE_BASE = 10000.0
NORM_EPS = 1e-6
NEG_BIG = -1e30
LOG2E = 1.4426950408889634

RET_CHUNK = 256
NA_QROWS = 4
NA_KROWS = 12
NA_STEP_BLOCKS = 2
MOE_TS = 512
DMA_UNROLL = 8
COMBINE_GROUP = 32
FFN_CHUNKS = 2
IN_PROJ_SUB = 512
OUT_PROJ_SUB = 512
MXU_COLS = 256
IN_PROJ_CHUNK = 3 * MXU_COLS
VMEM_LIMIT = 56 * 1024 * 1024

CB_RQ, CB_RK, CB_RV, CB_RG = 0, 3, 6, 9
CB_CB, CB_CC, CB_CX = 12, 14, 16
CB_NQ, CB_NK, CB_NV = 18, 21, 24


def _cp(sem, vmem=VMEM_LIMIT):
    return pltpu.CompilerParams(dimension_semantics=sem, vmem_limit_bytes=vmem)


def _silu(x):
    return x * (1.0 / (1.0 + jnp.exp(-x)))


def _dot(a, b):
    return jnp.dot(a, b, preferred_element_type=F32)


def _dot_nt(a, b):
    return lax.dot_general(a, b, (((1,), (1,)), ((), ())), preferred_element_type=F32)


def _dot_tn(a, b):
    return lax.dot_general(a, b, (((0,), (0,)), ((), ())), preferred_element_type=F32)


def _split_bf16(x):
    hi = x.astype(BF16)
    lo = (x - hi.astype(F32)).astype(BF16)
    return hi, lo


def _ada_kernel(c_ref, w_ref, b_ref, o_ref):
    x = _silu(c_ref[...]).astype(BF16)
    o_ref[0] = _dot(x, w_ref[0].astype(BF16)) + b_ref[0]


def _ada(c8, ada_w, ada_b):
    depth, d, n = ada_w.shape
    tn = n // 4
    return pl.pallas_call(
        _ada_kernel,
        grid=(depth, n // tn),
        in_specs=[pl.BlockSpec((8, d), lambda l, j: (0, 0)),
                  pl.BlockSpec((1, d, tn), lambda l, j: (l, 0, j)),
                  pl.BlockSpec((1, 1, tn), lambda l, j: (l, 0, j))],
        out_specs=pl.BlockSpec((1, 8, tn), lambda l, j: (l, 0, j)),
        out_shape=jax.ShapeDtypeStruct((depth, 8, n), F32),
        compiler_params=_cp(("parallel", "parallel")),
        name="ada_mod",
    )(c8, ada_w, ada_b.reshape(depth, 1, n))


def _norm_mod(x, g, sh, sc):
    ms = jnp.mean(x * x, axis=-1, keepdims=True)
    y = x * lax.rsqrt(ms + NORM_EPS) * g
    return y * (1.0 + sc) + sh


def _inproj_kernel(*refs, rope, n_cast):
    n_in = (8 if rope else 5) + n_cast
    for c_ref, cb_ref in zip(refs[n_in - n_cast:n_in], refs[n_in + 1:]):
        cb_ref[...] = c_ref[...].astype(BF16)
    if rope:
        h_ref, g_ref, sh_ref, sc_ref, w_ref, cos_ref, sa_ref, sb_ref = refs[:8]
    else:
        h_ref, g_ref, sh_ref, sc_ref, w_ref = refs[:5]
    o_ref = refs[n_in]
    tm = h_ref.shape[0]
    sub = min(IN_PROJ_SUB, tm)
    cw = IN_PROJ_CHUNK
    for r in range(tm // sub):
        rs = slice(r * sub, (r + 1) * sub)
        xn = _norm_mod(h_ref[rs, :], g_ref[...], sh_ref[0], sc_ref[0]).astype(BF16)
        for c0 in range(0, D_IN_PROJ, cw):
            c1 = min(c0 + cw, D_IN_PROJ)
            acc = _dot(xn, w_ref[:, c0:c1])
            for j in range((c1 - c0) // LANES):
                blk = acc[:, j * LANES:(j + 1) * LANES]
                cb = c0 // LANES + j
                if rope and cb < CB_RV:
                    blk = (blk * cos_ref[rs, :] + pltpu.roll(blk, 16, 1) * sa_ref[rs, :]
                           + pltpu.roll(blk, LANES - 16, 1) * sb_ref[rs, :])
                if CB_RK <= cb < CB_RV:
                    blk = blk * (HEAD_DIM ** -0.5)
                elif CB_NQ <= cb < CB_NK:
                    blk = blk * (HEAD_DIM ** -0.5 * (LOG2E if rope else 1.0))
                o_ref[rs, cb * LANES:(cb + 1) * LANES] = blk.astype(BF16)


def _in_proj(h2, g, sh, sc, w_bf, tables, seq, tm, casts=()):
    t, d = h2.shape
    tiles_per_seq = seq // tm
    steps = t // tm
    rope = tables is not None
    in_specs = [pl.BlockSpec((tm, d), lambda i: (i, 0)),
                pl.BlockSpec((1, d), lambda i: (0, 0)),
                pl.BlockSpec((1, 1, d), lambda i: (i // tiles_per_seq, 0, 0)),
                pl.BlockSpec((1, 1, d), lambda i: (i // tiles_per_seq, 0, 0)),
                pl.BlockSpec((d, D_IN_PROJ), lambda i: (0, 0))]
    args = [h2, g.reshape(1, d), sh, sc, w_bf]
    if rope:
        in_specs += [pl.BlockSpec((tm, LANES), lambda i: (i % tiles_per_seq, 0))] * 3
        args += list(tables)
    out_specs = [pl.BlockSpec((tm, D_IN_PROJ), lambda i: (i, 0))]
    out_shape = [jax.ShapeDtypeStruct((t, D_IN_PROJ), BF16)]
    for c in casts:
        cr, cc = c.shape
        assert cr % (steps * 16) == 0
        blk = pl.BlockSpec((cr // steps, cc), lambda i: (i, 0))
        in_specs.append(blk)
        out_specs.append(blk)
        out_shape.append(jax.ShapeDtypeStruct(c.shape, BF16))
        args.append(c)
    res = pl.pallas_call(
        functools.partial(_inproj_kernel, rope=rope, n_cast=len(casts)),
        grid=(steps,),
        in_specs=in_specs,
        out_specs=out_specs,
        out_shape=out_shape,
        compiler_params=_cp(("parallel",)),
        name="in_proj_rope" if rope else "in_proj_ctx",
    )(*args)
    return res if casts else res[0]


def _rope_tables(seq):
    t = np.arange(seq)
    row = (t // GRID_W).astype(np.float32)
    col = (t % GRID_W).astype(np.float32)
    n_freq = HEAD_DIM // 4
    inv_freq = (ROPE_BASE ** (-np.arange(n_freq, dtype=np.float32) / n_freq)).astype(np.float32)
    ang_r = row[:, None] * inv_freq
    ang_c = col[:, None] * inv_freq
    cos_h = np.concatenate([np.cos(ang_r), np.cos(ang_r), np.cos(ang_c), np.cos(ang_c)], axis=1)
    sin_h = np.concatenate([np.sin(ang_r), np.sin(ang_r), np.sin(ang_c), np.sin(ang_c)], axis=1)
    lane = np.arange(HEAD_DIM)
    second = (lane % 32) >= 16
    sa = np.where(second[None, :], sin_h, 0.0)
    sb = np.where(second[None, :], 0.0, -sin_h)
    tile2 = lambda a: jnp.asarray(np.concatenate([a, a], axis=1), F32)
    return tile2(cos_h), tile2(sa), tile2(sb)


def _ret_kernel(q_ref, k_ref, v_ref, g_ref, lgt_ref, gn_ref, sf0_ref, sb0_ref,
                y_ref, sfo_ref, sbo_ref, sfs_ref, sbs_ref, *, seq):
    c = RET_CHUNK
    n_chunks = seq // c
    lg = jax.nn.log_sigmoid(lgt_ref[0])
    lgf, lgb = lg[0:1, :], lg[1:2, :]
    pos = lax.broadcasted_iota(jnp.int32, (c, 1), 0).astype(F32)
    dkf = jnp.exp(lgf * (c - 1.0 - pos))
    dkb = jnp.exp(lgb * pos)
    dqf = jnp.exp(lgf * (pos + 1.0))
    dqb = jnp.exp(lgb * (c - pos))
    cdf = jnp.exp(lgf * float(c))
    cdb = jnp.exp(lgb * float(c))
    lane = lax.broadcasted_iota(jnp.int32, (1, LANES), 1)
    first = lane < HEAD_DIM
    ri = lax.broadcasted_iota(jnp.int32, (LANES, LANES), 0)
    ci = lax.broadcasted_iota(jnp.int32, (LANES, LANES), 1)
    same = (ri < HEAD_DIM) == (ci < HEAD_DIM)
    bd = same.astype(F32)

    def head_mean(x):
        a = jnp.sum(jnp.where(first, x, 0.0), axis=-1, keepdims=True)
        b = jnp.sum(jnp.where(first, 0.0, x), axis=-1, keepdims=True)
        return jnp.where(first, a, b) * (1.0 / HEAD_DIM)

    ii = lax.broadcasted_iota(jnp.int32, (c, c), 0)
    jj = lax.broadcasted_iota(jnp.int32, (c, c), 1)
    dif = (ii - jj).astype(F32)

    def decay_mask(h0):
        lf = lgf[:, h0:h0 + 1]
        lb = lgb[:, h0:h0 + 1]
        return jnp.where(dif > 0, jnp.exp(lf * jnp.maximum(dif, 0.0)),
                         jnp.where(dif < 0, jnp.exp(lb * jnp.maximum(-dif, 0.0)), 2.0))

    dm = (decay_mask(0), decay_mask(HEAD_DIM))
    mfirst = first.astype(BF16)
    msecond = (1.0 - first.astype(F32)).astype(BF16)

    unroll = min(8, n_chunks)

    def incr(n, carry):
        sl = pl.ds(pl.multiple_of(n * c, c), c)
        kf = k_ref[0, sl, :].astype(F32)
        v = v_ref[0, sl, :]
        sfs_ref[n] = _dot_tn((kf * dkf).astype(BF16), v) * bd
        sbs_ref[n] = _dot_tn((kf * dkb).astype(BF16), v) * bd
        return carry

    lax.fori_loop(0, n_chunks, incr, 0, unroll=unroll)

    def scan_f(n, s):
        u = sfs_ref[n]
        sfs_ref[n] = s
        return cdf * s + u

    def scan_b(i, s):
        n = n_chunks - 1 - i
        u = sbs_ref[n]
        sbs_ref[n] = s
        return cdb * s + u

    sfo_ref[0, 0] = lax.fori_loop(0, n_chunks, scan_f, sf0_ref[0, 0])
    sbo_ref[0, 0] = lax.fori_loop(0, n_chunks, scan_b, sb0_ref[0, 0])

    def outp(n, carry):
        sl = pl.ds(pl.multiple_of(n * c, c), c)
        q = q_ref[0, sl, :]
        k = k_ref[0, sl, :]
        v = v_ref[0, sl, :]
        qf = q.astype(F32)
        o = None
        for hh, mk in enumerate((mfirst, msecond)):
            s = _dot_nt(q * mk, k) * dm[hh]
            oh = _dot(s.astype(BF16), v)
            o = oh if o is None else jnp.where(first, o, oh)
        qcat = jnp.concatenate([(qf * dqf).astype(BF16), (qf * dqb).astype(BF16)], axis=1)
        scat = jnp.concatenate([sfs_ref[n], sbs_ref[n]], axis=0).astype(BF16)
        o = o + _dot(qcat, scat)
        mu = head_mean(o)
        dlt = o - mu
        var = head_mean(dlt * dlt)
        on = dlt * lax.rsqrt(var + NORM_EPS) * gn_ref[0]
        y_ref[0, sl, :] = (_silu(g_ref[0, sl, :].astype(F32)) * on).astype(BF16)
        return carry

    lax.fori_loop(0, n_chunks, outp, 0, unroll=unroll)


def _retention(proj, lgt, gn, sf0, sb0):
    b, seq, _ = proj.shape
    col = lambda off: pl.BlockSpec((1, seq, LANES), lambda bi, p: (bi, 0, off + p))
    st = pl.BlockSpec((1, 1, LANES, LANES), lambda bi, p: (bi, p, 0, 0))
    return pl.pallas_call(
        functools.partial(_ret_kernel, seq=seq),
        grid=(b, N_PAIRS),
        in_specs=[col(CB_RQ), col(CB_RK), col(CB_RV), col(CB_RG),
                  pl.BlockSpec((1, 2, LANES), lambda bi, p: (p, 0, 0)),
                  pl.BlockSpec((1, 1, LANES), lambda bi, p: (p, 0, 0)),
                  st, st],
        out_specs=[pl.BlockSpec((1, seq, LANES), lambda bi, p: (bi, 0, p)), st, st],
        out_shape=[jax.ShapeDtypeStruct((b, seq, D_RET), BF16),
                   jax.ShapeDtypeStruct((b, N_PAIRS, LANES, LANES), F32),
                   jax.ShapeDtypeStruct((b, N_PAIRS, LANES, LANES), F32)],
        scratch_shapes=[pltpu.VMEM((seq // RET_CHUNK, LANES, LANES), F32),
                        pltpu.VMEM((seq // RET_CHUNK, LANES, LANES), F32)],
        compiler_params=_cp(("parallel", "parallel")),
        name="retention_s%d" % seq,
    )(proj, proj, proj, proj, lgt, gn, sf0, sb0)


def _conv_kernel(b_ref, c_ref, x_ref, w_ref, y_ref, *, seq):
    u = c_ref[0].astype(F32) * x_ref[0].astype(F32)
    row = lax.broadcasted_iota(jnp.int32, (seq, 1), 0)
    prev = jnp.where(row == 0, 0.0, pltpu.roll(u, 1, 0))
    nxt = jnp.where(row == seq - 1, 0.0, pltpu.roll(u, seq - 1, 0))
    y = w_ref[0:1, :] * prev + w_ref[1:2, :] * u + w_ref[2:3, :] * nxt
    y_ref[0] = (b_ref[0].astype(F32) * y).astype(BF16)


def _short_conv(proj, conv_w):
    b, seq, _ = proj.shape
    nblk = D_CONV // LANES
    col = lambda off: pl.BlockSpec((1, seq, LANES), lambda bi, j: (bi, 0, off + j))
    return pl.pallas_call(
        functools.partial(_conv_kernel, seq=seq),
        grid=(b, nblk),
        in_specs=[col(CB_CB), col(CB_CC), col(CB_CX),
                  pl.BlockSpec((3, LANES), lambda bi, j: (0, j))],
        out_specs=pl.BlockSpec((1, seq, LANES), lambda bi, j: (bi, 0, j)),
        out_shape=jax.ShapeDtypeStruct((b, seq, D_CONV), BF16),
        compiler_params=_cp(("parallel", "parallel")),
        name="short_conv_s%d" % seq,
    )(proj, proj, proj, conv_w)


def _na_bias_plan(rows):
    plan = np.full((3, NA_QROWS, NA_KROWS), -1, np.int64)
    for cl, rb in enumerate((0, NA_QROWS, rows - NA_QROWS)):
        ws = int(np.clip(rb - NA_ROWS // 2, 0, rows - NA_KROWS))
        for i in range(NA_QROWS):
            r = rb + i
            r0 = int(np.clip(r - NA_ROWS // 2, 0, rows - NA_ROWS))
            for j in range(NA_KROWS):
                kr = ws + j
                if r0 <= kr < r0 + NA_ROWS:
                    plan[cl, i, j] = kr - r + (NA_ROWS - 1)
    return plan


def _na_bias_kernel(rpb_ref, o_ref, *, plan):
    w = GRID_W
    ndr, ndc = 2 * NA_ROWS - 1, 2 * NA_COLS - 1
    base = pl.program_id(0) * (ndr * ndc)
    c = lax.broadcasted_iota(jnp.int32, (w, w), 0)
    kc = lax.broadcasted_iota(jnp.int32, (w, w), 1)
    dcidx = jnp.clip(kc - c, -(NA_COLS - 1), NA_COLS - 1) + (NA_COLS - 1)
    c0 = jnp.clip(c - NA_COLS // 2, 0, w - NA_COLS)
    col_in = jnp.logical_and(kc >= c0, kc < c0 + NA_COLS)
    neg = jnp.full((w, w), NEG_BIG, F32)
    tiles = []
    for dr in range(ndr):
        t = neg
        for dcv in range(ndc):
            t = jnp.where(dcidx == dcv, rpb_ref[base + dr * ndc + dcv], t)
        tiles.append(jnp.where(col_in, t * LOG2E, NEG_BIG))
    for cl in range(plan.shape[0]):
        for i in range(plan.shape[1]):
            for j in range(0, plan.shape[2], 2):
                pair = [tiles[int(d)] if d >= 0 else neg for d in plan[cl, i, j:j + 2]]
                o_ref[0, cl, i * w:(i + 1) * w, j * w:(j + 2) * w] = jnp.concatenate(pair, axis=1)


def _na_bias(rpb, rows):
    h = rpb.shape[0]
    tq, tk = NA_QROWS * GRID_W, NA_KROWS * GRID_W
    return pl.pallas_call(
        functools.partial(_na_bias_kernel, plan=_na_bias_plan(rows)),
        grid=(h,),
        in_specs=[pl.BlockSpec(memory_space=pltpu.SMEM)],
        out_specs=pl.BlockSpec((1, 3, tq, tk), lambda hh: (hh, 0, 0, 0)),
        out_shape=jax.ShapeDtypeStruct((h, 3, tq, tk), F32),
        compiler_params=_cp(("parallel",)),
        name="na_bias_expand",
    )(rpb.reshape(-1))


def _na_kernel(*refs, rows, cast):
    if cast:
        q_ref, k_ref, v_ref, kc_ref, vc_ref, bias_ref, c_ref, o_ref, cb_ref, v1_ref, vc1_ref = refs
        cb_ref[...] = c_ref[...].astype(BF16)
    else:
        q_ref, k_ref, v_ref, kc_ref, vc_ref, bias_ref, o_ref, v1_ref, vc1_ref = refs
    step = pl.program_id(1)
    nqb = rows // NA_QROWS
    tq = NA_QROWS * GRID_W
    lane = lax.broadcasted_iota(jnp.int32, (1, LANES), 1)
    first = lane < HEAD_DIM

    @pl.when(step == 0)
    def _():
        one = jnp.ones((1, LANES), BF16)
        for p in range(N_PAIRS):
            cols = slice(p * LANES, (p + 1) * LANES)
            v1_ref[2 * p] = jnp.where(first, v_ref[0, :, cols], one)
            v1_ref[2 * p + 1] = jnp.where(first, one, v_ref[0, :, cols])
            vc1_ref[2 * p] = jnp.where(first, vc_ref[0, :, cols], one)
            vc1_ref[2 * p + 1] = jnp.where(first, one, vc_ref[0, :, cols])

    for sb in range(NA_STEP_BLOCKS):
        qb = step * NA_STEP_BLOCKS + sb
        cls = jnp.where(qb == 0, 0, jnp.where(qb == nqb - 1, 2, 1))
        ws = jnp.clip(qb * NA_QROWS - NA_ROWS // 2, 0, rows - NA_KROWS)
        sl = pl.ds(pl.multiple_of(ws * GRID_W, GRID_W), NA_KROWS * GRID_W)
        qrows = slice(sb * tq, (sb + 1) * tq)
        for p in range(N_PAIRS):
            cols = slice(p * LANES, (p + 1) * LANES)
            q = q_ref[0, qrows, cols]
            kw = k_ref[0, sl, cols]
            kc = kc_ref[0, :, cols]
            pv = []
            for hh in range(2):
                mk = (first if hh == 0 else jnp.logical_not(first)).astype(BF16)
                qh = q * mk
                s_loc = _dot_nt(qh, kw) + bias_ref[2 * p + hh, cls]
                s_ctx = _dot_nt(qh, kc)
                m = jnp.maximum(jnp.max(s_loc, axis=-1, keepdims=True), jnp.max(s_ctx, axis=-1, keepdims=True))
                p_loc = jnp.exp2(s_loc - m).astype(BF16)
                p_ctx = jnp.exp2(s_ctx - m).astype(BF16)
                pv.append(_dot(p_loc, v1_ref[2 * p + hh, sl, :]) + _dot(p_ctx, vc1_ref[2 * p + hh]))
            num = jnp.where(first, pv[0], pv[1])
            den = jnp.where(first, pltpu.roll(pv[0], HEAD_DIM, 1), pltpu.roll(pv[1], HEAD_DIM, 1))
            o_ref[0, qrows, cols] = (num / den).astype(BF16)


def _na(proj, proj_ctx, bias, cast=None):
    b, seq, _ = proj.shape
    ctx_len = proj_ctx.shape[1]
    rows = seq // GRID_W
    nqb = rows // NA_QROWS
    tq = NA_STEP_BLOCKS * NA_QROWS * GRID_W
    tk = NA_KROWS * GRID_W
    nh = 2 * N_PAIRS
    n_steps = nqb // NA_STEP_BLOCKS
    assert n_steps * NA_STEP_BLOCKS == nqb
    cq, ck, cv = (CB_NQ * LANES // D_NA, CB_NK * LANES // D_NA, CB_NV * LANES // D_NA)
    in_specs = [pl.BlockSpec((1, tq, D_NA), lambda bi, st: (bi, st, cq)),
                pl.BlockSpec((1, seq, D_NA), lambda bi, st: (bi, 0, ck)),
                pl.BlockSpec((1, seq, D_NA), lambda bi, st: (bi, 0, cv)),
                pl.BlockSpec((1, ctx_len, D_NA), lambda bi, st: (bi, 0, ck)),
                pl.BlockSpec((1, ctx_len, D_NA), lambda bi, st: (bi, 0, cv)),
                pl.BlockSpec(bias.shape, lambda bi, st: (0, 0, 0, 0), pipeline_mode=pl.Buffered(1))]
    out_specs = [pl.BlockSpec((1, tq, D_NA), lambda bi, st: (bi, st, 0))]
    out_shape = [jax.ShapeDtypeStruct((b, seq, D_NA), BF16)]
    args = [proj, proj, proj, proj_ctx, proj_ctx, bias]
    if cast is not None:
        cr, cc = cast.shape
        steps = b * n_steps
        assert cr % (steps * 16) == 0
        blk = pl.BlockSpec((cr // steps, cc), lambda bi, st: (bi * n_steps + st, 0))
        in_specs.append(blk)
        out_specs.append(blk)
        out_shape.append(jax.ShapeDtypeStruct(cast.shape, BF16))
        args.append(cast)
    res = pl.pallas_call(
        functools.partial(_na_kernel, rows=rows, cast=cast is not None),
        grid=(b, n_steps),
        in_specs=in_specs,
        out_specs=out_specs,
        out_shape=out_shape,
        scratch_shapes=[pltpu.VMEM((nh, seq, LANES), BF16), pltpu.VMEM((nh, ctx_len, LANES), BF16)],
        compiler_params=_cp(("parallel", "arbitrary")),
        name="na_attn",
    )(*args)
    return res if cast is not None else res[0]


def _ctx_attn_kernel(q_ref, k_ref, v_ref, o_ref):
    q = q_ref[0]
    k = k_ref[0]
    v = v_ref[0]
    lane = lax.broadcasted_iota(jnp.int32, (1, LANES), 1)
    first = lane < HEAD_DIM
    o = None
    for hh in range(2):
        mk = (first if hh == 0 else jnp.logical_not(first)).astype(BF16)
        s = _dot_nt(q * mk, k)
        m = jnp.max(s, axis=-1, keepdims=True)
        p = jnp.exp(s - m)
        l = jnp.sum(p, axis=-1, keepdims=True)
        oh = _dot(p.astype(BF16), v) / l
        o = oh if o is None else jnp.where(first, o, oh)
    o_ref[0] = o.astype(BF16)


def _ctx_attn(proj_ctx):
    b, ctx_len, _ = proj_ctx.shape
    col = lambda off: pl.BlockSpec((1, ctx_len, LANES), lambda bi, p: (bi, 0, off + p))
    return pl.pallas_call(
        _ctx_attn_kernel,
        grid=(b, N_PAIRS),
        in_specs=[col(CB_NQ), col(CB_NK), col(CB_NV)],
        out_specs=pl.BlockSpec((1, ctx_len, LANES), lambda bi, p: (bi, 0, p)),
        out_shape=jax.ShapeDtypeStruct((b, ctx_len, D_NA), BF16),
        compiler_params=_cp(("parallel", "parallel")),
        name="ctx_attn",
    )(proj_ctx, proj_ctx, proj_ctx)


def _outproj_kernel(*refs, route):
    if route:
        (yr_ref, yc_ref, yn_ref, h_ref, w_ref, g1_ref, n2_ref, sh_ref, sc_ref, rw_ref, rb_ref,
         ho_ref, xn_ref, idx_ref, wt_ref) = refs
    else:
        yr_ref, yc_ref, yn_ref, h_ref, w_ref, g1_ref, n2_ref, sh_ref, sc_ref, ho_ref, xn_ref = refs
    tm = h_ref.shape[0]
    sub = min(OUT_PROJ_SUB, tm)
    if route:
        r_hi, r_lo = _split_bf16(rw_ref[...])
        r_both = jnp.concatenate([r_hi, r_lo], axis=0)
    for r in range(tm // sub):
        rs = slice(r * sub, (r + 1) * sub)
        ycat = jnp.concatenate([yr_ref[rs, :], yc_ref[rs, :], yn_ref[rs, :]], axis=-1)
        h = h_ref[rs, :] + g1_ref[0] * _dot(ycat, w_ref[...])
        ho_ref[rs, :] = h
        a = _norm_mod(h, n2_ref[...], sh_ref[0], sc_ref[0])
        xn_ref[rs, :] = a.astype(xn_ref.dtype)
        if route:
            a_hi, a_lo = _split_bf16(a)
            t_hi = _dot_nt(r_both, a_hi)
            logits = t_hi[:N_EXPERTS] + t_hi[N_EXPERTS:] + _dot_nt(r_hi, a_lo) + rb_ref[...]
            eidx = lax.broadcasted_iota(jnp.int32, logits.shape, 0)
            m1 = jnp.max(logits, axis=0, keepdims=True)
            i1 = jnp.min(jnp.where(logits == m1, eidx, N_EXPERTS), axis=0, keepdims=True)
            rest = jnp.where(eidx == i1, -jnp.inf, logits)
            m2 = jnp.max(rest, axis=0, keepdims=True)
            i2 = jnp.min(jnp.where(rest == m2, eidx, N_EXPERTS), axis=0, keepdims=True)
            e2 = jnp.exp(m2 - m1)
            den = 1.0 + e2
            idx_ref[:, rs] = jnp.concatenate([i1, i2], axis=0)
            wt_ref[:, rs] = jnp.concatenate([1.0 / den, e2 / den], axis=0)


def _out_proj(yr, yc, yn, h2, w_bf, g1, n2g, sh2, sc2, seq, tm, router=None):
    t, d = h2.shape
    tps = seq // tm
    route = router is not None
    row = lambda wdt: pl.BlockSpec((tm, wdt), lambda i: (i, 0))
    mod = pl.BlockSpec((1, 1, d), lambda i: (i // tps, 0, 0))
    in_specs = [row(D_RET), row(D_CONV), row(D_NA), row(d),
                pl.BlockSpec((d, d), lambda i: (0, 0)), mod,
                pl.BlockSpec((1, d), lambda i: (0, 0)), mod, mod]
    args = [yr, yc, yn, h2, w_bf, g1, n2g.reshape(1, d), sh2, sc2]
    out_specs = [row(d), row(d)]
    out_shape = [jax.ShapeDtypeStruct((t, d), F32), jax.ShapeDtypeStruct((t, d), BF16)]
    if route:
        rw, rb = router
        in_specs += [pl.BlockSpec((N_EXPERTS, d), lambda i: (0, 0)),
                     pl.BlockSpec((N_EXPERTS, 1), lambda i: (0, 0))]
        args += [rw.T, rb.reshape(N_EXPERTS, 1)]
        out_specs += [pl.BlockSpec((2, tm), lambda i: (0, i))] * 2
        out_shape += [jax.ShapeDtypeStruct((2, t), jnp.int32), jax.ShapeDtypeStruct((2, t), F32)]
    return pl.pallas_call(
        functools.partial(_outproj_kernel, route=route),
        grid=(t // tm,),
        in_specs=in_specs,
        out_specs=out_specs,
        out_shape=out_shape,
        compiler_params=_cp(("parallel",)),
        name="out_proj_route" if route else "out_proj_t%d" % t,
    )(*args)


def _swiglu_chunks(x, w_in_ref, w_out_ref, ff, n_chunks):
    tf = -(-ff // (n_chunks * MXU_COLS)) * MXU_COLS
    y = None
    for c0 in range(0, ff, tf):
        c1 = min(c0 + tf, ff)
        gate = _dot(x, w_in_ref[:, c0:c1])
        up = _dot(x, w_in_ref[:, ff + c0:ff + c1])
        part = _dot((_silu(gate) * up).astype(BF16), w_out_ref[c0:c1, :])
        y = part if y is None else y + part
    return y


def _outproj_ffn_kernel(yr_ref, yc_ref, yn_ref, h_ref, wo_ref, g1_ref, n2_ref, sh_ref, sc_ref,
                        wi_ref, wf_ref, g2_ref, o_ref, *, ff, n_chunks):
    ycat = jnp.concatenate([yr_ref[...], yc_ref[...], yn_ref[...]], axis=-1)
    h1 = h_ref[...] + g1_ref[0] * _dot(ycat, wo_ref[...])
    a = _norm_mod(h1, n2_ref[...], sh_ref[0], sc_ref[0]).astype(BF16)
    o_ref[...] = h1 + g2_ref[0] * _swiglu_chunks(a, wi_ref, wf_ref, ff, n_chunks)


def _outproj_ffn(yr, yc, yn, h2, wo_bf, g1, n2g, sh2, sc2, fw_in_bf, fw_out_bf, g2, seq, tm, n_chunks):
    t, d = h2.shape
    ff = fw_out_bf.shape[0]
    tps = seq // tm
    resident = pl.Buffered(1)
    row = lambda wdt: pl.BlockSpec((tm, wdt), lambda i: (i, 0))
    mod = pl.BlockSpec((1, 1, d), lambda i: (i // tps, 0, 0))
    return pl.pallas_call(
        functools.partial(_outproj_ffn_kernel, ff=ff, n_chunks=n_chunks),
        grid=(t // tm,),
        in_specs=[row(D_RET), row(D_CONV), row(D_NA), row(d),
                  pl.BlockSpec((d, d), lambda i: (0, 0), pipeline_mode=resident), mod,
                  pl.BlockSpec((1, d), lambda i: (0, 0)), mod, mod,
                  pl.BlockSpec((d, 2 * ff), lambda i: (0, 0), pipeline_mode=resident),
                  pl.BlockSpec((ff, d), lambda i: (0, 0), pipeline_mode=resident), mod],
        out_specs=row(d),
        out_shape=jax.ShapeDtypeStruct((t, d), F32),
        compiler_params=_cp(("parallel",)),
        name="out_proj_ffn_t%d" % t,
    )(yr, yc, yn, h2, wo_bf, g1, n2g.reshape(1, d), sh2, sc2, fw_in_bf, fw_out_bf, g2)


def _dispatch_kernel(s0_ref, s1_ref, zs_ref, zv_ref, x_ref, xs_ref, slab_ref, zbuf_ref, sem, *, rows):
    @pl.when(pl.program_id(0) == 0)
    def _():
        zbuf_ref[...] = jnp.zeros_like(zbuf_ref)
        for e in range(2 * N_EXPERTS):
            @pl.when(zv_ref[e] != 0)
            def _():
                dst = xs_ref.at[pl.ds(pl.multiple_of(zs_ref[e], MOE_TS), MOE_TS)]
                pltpu.make_async_copy(zbuf_ref, dst, sem.at[1]).start()
        for e in range(2 * N_EXPERTS):
            @pl.when(zv_ref[e] != 0)
            def _():
                pltpu.make_async_copy(zbuf_ref, xs_ref.at[pl.ds(0, MOE_TS)], sem.at[1]).wait()

    nk = slab_ref.shape[1]

    def issue(g, carry):
        rs = pl.ds(pl.multiple_of(g * DMA_UNROLL, DMA_UNROLL), DMA_UNROLL)
        for k in range(nk):
            slab_ref[rs, k, :] = x_ref[rs, k * LANES:(k + 1) * LANES].astype(slab_ref.dtype)
        for u in range(DMA_UNROLL):
            r = g * DMA_UNROLL + u
            src = slab_ref.at[r]
            pltpu.make_async_copy(src, xs_ref.at[s0_ref[r]], sem.at[0]).start(priority=u % 2)
            pltpu.make_async_copy(src, xs_ref.at[s1_ref[r]], sem.at[0]).start(priority=(u + 1) % 2)
        return carry

    lax.fori_loop(0, rows // DMA_UNROLL, issue, 0)
    for _ in range(2):
        pltpu.make_async_copy(slab_ref, xs_ref.at[pl.ds(0, rows)], sem.at[0]).wait()


def _dispatch(xn, slot0, slot1, zplan, n_slots, rows):
    t, d = xn.shape
    slab = (d // LANES, LANES)
    zstart, zvalid = zplan
    return pl.pallas_call(
        functools.partial(_dispatch_kernel, rows=rows),
        grid=(t // rows,),
        in_specs=[pl.BlockSpec((rows,), lambda i: (i,), memory_space=pltpu.SMEM),
                  pl.BlockSpec((rows,), lambda i: (i,), memory_space=pltpu.SMEM),
                  pl.BlockSpec((2 * N_EXPERTS,), lambda i: (0,), memory_space=pltpu.SMEM),
                  pl.BlockSpec((2 * N_EXPERTS,), lambda i: (0,), memory_space=pltpu.SMEM),
                  pl.BlockSpec((rows, d), lambda i: (i, 0))],
        out_specs=pl.BlockSpec(memory_space=pl.ANY),
        out_shape=jax.ShapeDtypeStruct((n_slots,) + slab, F32),
        scratch_shapes=[pltpu.VMEM((rows,) + slab, F32), pltpu.VMEM((MOE_TS,) + slab, F32),
                        pltpu.SemaphoreType.DMA((2,))],
        compiler_params=_cp(("arbitrary",)),
        name="moe_dispatch",
    )(slot0, slot1, zstart, zvalid, xn)


def _expert_kernel(be_ref, nu_ref, xs_ref, wi_ref, wo_ref, o_ref, xa_ref, xb_ref, sem, *, ff, n_chunks):
    j = pl.program_id(0)
    n = pl.num_programs(0)
    ts, d = xa_ref.shape
    bufs = (xa_ref, xb_ref)

    def copies(blk, b):
        rows = pl.ds(pl.multiple_of(blk * ts, ts), ts)
        return [pltpu.make_async_copy(xs_ref.at[rows, k, :], bufs[b].at[:, k * LANES:(k + 1) * LANES],
                                      sem.at[b]) for k in range(d // LANES)]

    @pl.when(j == 0)
    def _():
        for c in copies(0, 0):
            c.start()

    def step(b):
        @pl.when(j + 1 < n)
        def _():
            for c in copies(j + 1, 1 - b):
                c.start()

        for c in copies(j, b):
            c.wait()

        @pl.when(j < nu_ref[0])
        def _():
            o_ref[...] = _swiglu_chunks(bufs[b][...].astype(BF16), wi_ref.at[0], wo_ref.at[0], ff, n_chunks)

        @pl.when(j >= nu_ref[0])
        def _():
            o_ref[...] = jnp.zeros_like(o_ref)

    for b in range(2):
        pl.when(j % 2 == b)(functools.partial(step, b))


def _experts(xs, block_expert, n_used, w_in_bf, w_out_bf, n_chunks):
    n, nk, _ = xs.shape
    d = nk * LANES
    ff = w_out_bf.shape[1]
    ts = MOE_TS
    grid_spec = pltpu.PrefetchScalarGridSpec(
        num_scalar_prefetch=2,
        grid=(n // ts,),
        in_specs=[pl.BlockSpec(memory_space=pl.ANY),
                  pl.BlockSpec((1, d, 2 * ff), lambda j, be, nu: (be[j], 0, 0)),
                  pl.BlockSpec((1, ff, d), lambda j, be, nu: (be[j], 0, 0))],
        out_specs=pl.BlockSpec((ts, d), lambda j, be, nu: (j, 0)),
        scratch_shapes=[pltpu.VMEM((ts, d), xs.dtype), pltpu.VMEM((ts, d), xs.dtype),
                        pltpu.SemaphoreType.DMA((2,))],
    )
    return pl.pallas_call(
        functools.partial(_expert_kernel, ff=ff, n_chunks=n_chunks),
        grid_spec=grid_spec,
        out_shape=jax.ShapeDtypeStruct((n, d), F32),
        compiler_params=_cp(("arbitrary",)),
        name="moe_experts",
    )(block_expert, n_used, xs, w_in_bf, w_out_bf)


def _combine_kernel(p0_ref, p1_ref, q0_ref, q1_ref, ys_ref, h_ref, w0_ref, w1_ref, g2_ref, fg_ref, o_ref,
                    ya0_ref, ya1_ref, yb0_ref, yb1_ref, sem, *, rows):
    i = pl.program_id(0)
    n = pl.num_programs(0)
    n_iter = rows // COMBINE_GROUP
    bufs = ((ya0_ref, ya1_ref), (yb0_ref, yb1_ref))

    def issue_rows(pa_ref, pb_ref, b, g):
        for u in range(COMBINE_GROUP):
            r = g * COMBINE_GROUP + u
            pltpu.make_async_copy(ys_ref.at[pl.ds(pa_ref[r], 1)], bufs[b][0].at[pl.ds(r, 1)],
                                  sem.at[0, b]).start(priority=u % 2)
            pltpu.make_async_copy(ys_ref.at[pl.ds(pb_ref[r], 1)], bufs[b][1].at[pl.ds(r, 1)],
                                  sem.at[1, b]).start(priority=(u + 1) % 2)

    def finish_rows(b, g):
        rs = pl.ds(pl.multiple_of(g * COMBINE_GROUP, COMBINE_GROUP), COMBINE_GROUP)
        y = w0_ref[rs, :] * bufs[b][0][rs, :] + w1_ref[rs, :] * bufs[b][1][rs, :]
        h = h_ref[rs, :] + g2_ref[0] * y
        ms = jnp.mean(h * h, axis=-1, keepdims=True)
        o_ref[rs, :] = h * lax.rsqrt(ms + NORM_EPS) * fg_ref[...]

    @pl.when(i == 0)
    def _():
        def first(g, carry):
            issue_rows(p0_ref, p1_ref, 0, g)
            return carry
        lax.fori_loop(0, n_iter, first, 0)

    def step(b):
        pltpu.make_async_copy(ys_ref.at[pl.ds(0, rows)], bufs[b][0], sem.at[0, b]).wait()
        pltpu.make_async_copy(ys_ref.at[pl.ds(0, rows)], bufs[b][1], sem.at[1, b]).wait()

        @pl.when(i + 1 < n)
        def _():
            def both(g, carry):
                issue_rows(q0_ref, q1_ref, 1 - b, g)
                finish_rows(b, g)
                return carry
            lax.fori_loop(0, n_iter, both, 0)

        @pl.when(i + 1 == n)
        def _():
            def last(g, carry):
                finish_rows(b, g)
                return carry
            lax.fori_loop(0, n_iter, last, 0)

    for b in range(2):
        pl.when(i % 2 == b)(functools.partial(step, b))


def _combine(pos0, pos1, ys, h2, w0, w1, g2, final_g, seq, rows):
    t, d = h2.shape
    tps = seq // rows
    steps = t // rows
    blk = pl.BlockSpec((rows, d), lambda i: (i, 0))
    col = pl.BlockSpec((rows, 1), lambda i: (i, 0))
    here = pl.BlockSpec((rows,), lambda i: (i,), memory_space=pltpu.SMEM)
    ahead = pl.BlockSpec((rows,), lambda i: (jnp.minimum(i + 1, steps - 1),), memory_space=pltpu.SMEM)
    return pl.pallas_call(
        functools.partial(_combine_kernel, rows=rows),
        grid=(steps,),
        in_specs=[here, here, ahead, ahead,
                  pl.BlockSpec(memory_space=pl.ANY),
                  blk, col, col,
                  pl.BlockSpec((1, 1, d), lambda i: (i // tps, 0, 0)),
                  pl.BlockSpec((1, d), lambda i: (0, 0))],
        out_specs=blk,
        out_shape=jax.ShapeDtypeStruct((t, d), F32),
        scratch_shapes=[pltpu.VMEM((rows, d), F32)] * 4 + [pltpu.SemaphoreType.DMA((2, 2))],
        compiler_params=_cp(("arbitrary",)),
        name="moe_combine_norm",
    )(pos0, pos1, pos0, pos1, ys, h2, w0, w1, g2, final_g.reshape(1, d))


def _route_plan(idx):
    t = idx.shape[1]
    ts = MOE_TS
    n_slots = 2 * t + N_EXPERTS * ts
    e_flat = idx.reshape(-1)
    onehot = (e_flat[:, None] == jnp.arange(N_EXPERTS, dtype=jnp.int32)[None, :]).astype(jnp.int32)
    csum = jnp.cumsum(onehot, axis=0)
    rank = jnp.sum((csum - onehot) * onehot, axis=1)
    counts = csum[-1]
    padded = ((counts + ts - 1) // ts) * ts
    ends = jnp.cumsum(padded)
    offs = ends - padded
    slot = (jnp.sum(onehot * offs[None, :], axis=1) + rank).astype(jnp.int32)
    starts = jnp.arange(n_slots // ts, dtype=jnp.int32) * ts
    block_expert = jnp.minimum(jnp.sum((starts[:, None] >= ends[None, :]).astype(jnp.int32), axis=1),
                               N_EXPERTS - 1).astype(jnp.int32)
    n_used = (ends[-1] // ts).astype(jnp.int32).reshape(1)
    tail = ends[-1] + jnp.arange(N_EXPERTS, dtype=jnp.int32) * ts
    zstart = jnp.concatenate([ends - ts, tail])
    zvalid = jnp.concatenate([padded > 0, tail < n_slots]).astype(jnp.int32)
    zstart = jnp.where(zvalid != 0, zstart, 0).astype(jnp.int32)
    return slot[:t], slot[t:], block_expert, n_used, (zstart, zvalid), n_slots


def kernel(x, c, ctx, c_ctx, ada_w, ada_b, norm1_g, norm2_g, w_in, w_out, ret_decay_logit, ret_gn_g,
           conv_w, na_rpb, ffn_w_in, ffn_w_out, moe_router_w, moe_router_b, moe_w_in, moe_w_out, final_g):
    b, seq, d = x.shape
    ctx_len = ctx.shape[1]
    depth = ada_w.shape[0]
    assert depth == 2, "the final norm is fused into the last (MoE) layer's combine step"
    rows = seq // GRID_W
    t_lat, t_ctx = b * seq, b * ctx_len
    tm_lat = min(1024, seq)
    tm_ctx = min(1024, t_ctx)

    c8 = jnp.zeros((8, d), F32).at[:b].set(c).at[b].set(c_ctx)
    mods = _ada(c8, ada_w, ada_b)
    tables = _rope_tables(seq)

    h = x.reshape(t_lat, d)
    hc = ctx.reshape(t_ctx, d)
    zero_state = jnp.zeros((b, N_PAIRS, LANES, LANES), F32)
    out = None
    for layer in range(depth):
        update_ctx = layer < depth - 1
        m = mods[layer].reshape(8, 6, d)
        lat = [m[:b, i].reshape(b, 1, d) for i in range(6)]
        cx = [jnp.broadcast_to(m[b, i].reshape(1, 1, d), (b, 1, d)) for i in range(6)]
        w_in_bf = w_in[layer].astype(BF16)
        w_out_bf = w_out[layer].astype(BF16)
        lgt = ret_decay_logit[layer].reshape(2, N_PAIRS, LANES // HEAD_DIM)
        lgt = jnp.repeat(lgt, HEAD_DIM, axis=2).transpose(1, 0, 2)
        gn = ret_gn_g[layer].reshape(N_PAIRS, 1, LANES)

        proj_c = _in_proj(hc, norm1_g[layer], cx[0], cx[1], w_in_bf, None, t_ctx, tm_ctx)
        proj_c = proj_c.reshape(b, ctx_len, D_IN_PROJ)
        if layer % 2 == 0:
            proj, fw_in, fw_out = _in_proj(h, norm1_g[layer], lat[0], lat[1], w_in_bf, tables, seq, tm_lat,
                                           casts=(ffn_w_in[layer // 2], ffn_w_out[layer // 2]))
        else:
            proj = _in_proj(h, norm1_g[layer], lat[0], lat[1], w_in_bf, tables, seq, tm_lat)
        proj = proj.reshape(b, seq, D_IN_PROJ)

        y_ret_c, s_fwd, s_bwd = _retention(proj_c, lgt, gn, zero_state, zero_state)
        y_ret, _, _ = _retention(proj, lgt, gn, s_fwd, s_bwd)
        y_conv = _short_conv(proj, conv_w[layer])
        mw = moe_w_in[(layer + 1) // 2] if layer % 2 == 0 else moe_w_out[layer // 2]
        y_na, mw_bf = _na(proj, proj_c, _na_bias(na_rpb[layer], rows), cast=mw.reshape(-1, mw.shape[-1]))
        if layer % 2 == 0:
            mw_in_bf = mw_bf.reshape(mw.shape)
        else:
            mw_out_bf = mw_bf.reshape(mw.shape)

        if layer % 2 == 0:
            h = _outproj_ffn(y_ret.reshape(t_lat, D_RET), y_conv.reshape(t_lat, D_CONV),
                             y_na.reshape(t_lat, D_NA), h, w_out_bf, lat[2], norm2_g[layer], lat[3], lat[4],
                             fw_in, fw_out, lat[5], seq, min(512, seq), FFN_CHUNKS)
            if update_ctx:
                y_conv_c = _short_conv(proj_c, conv_w[layer])
                y_na_c = _ctx_attn(proj_c)
                hc = _outproj_ffn(y_ret_c.reshape(t_ctx, D_RET), y_conv_c.reshape(t_ctx, D_CONV),
                                  y_na_c.reshape(t_ctx, D_NA), hc, w_out_bf, cx[2], norm2_g[layer], cx[3], cx[4],
                                  fw_in, fw_out, cx[5], t_ctx, min(512, t_ctx), FFN_CHUNKS)
        else:
            e = layer // 2
            h, xn, idx, wts = _out_proj(y_ret.reshape(t_lat, D_RET), y_conv.reshape(t_lat, D_CONV),
                                        y_na.reshape(t_lat, D_NA), h, w_out_bf, lat[2], norm2_g[layer],
                                        lat[3], lat[4], seq, tm_lat,
                                        router=(moe_router_w[e], moe_router_b[e]))
            pos0, pos1, block_expert, n_used, zstart, n_slots = _route_plan(idx)
            xs = _dispatch(xn, pos0, pos1, zstart, n_slots, min(1024, seq))
            ys = _experts(xs, block_expert, n_used, mw_in_bf, mw_out_bf, FFN_CHUNKS)
            out = _combine(pos0, pos1, ys, h, wts[0].reshape(t_lat, 1), wts[1].reshape(t_lat, 1),
                           lat[5], final_g, seq, min(512, seq))
            h = out
    return out.reshape(b, seq, d)
```

```python
import functools

import numpy as np
import jax
import jax.numpy as jnp
from jax import lax
from jax.experimental import pallas as pl
from jax.experimental.pallas import tpu as pltpu

F32 = jnp.float32
BF16 = jnp.bfloat16

LANES = 128
HEAD_DIM = 64
GRID_W = 64
N_CONV_GROUPS = 4
N_RET_HEADS = 6
N_NA_HEADS = 6
D_RET = N_RET_HEADS * HEAD_DIM
D_CONV = N_CONV_GROUPS * HEAD_DIM
D_NA = N_NA_HEADS * HEAD_DIM
D_IN_PROJ = 4 * D_RET + 3 * D_CONV + 3 * D_NA
N_PAIRS = D_RET // LANES
NA_ROWS = 8
NA_COLS = 16
N_EXPERTS = 8
ROPE_BASE = 10000.0
NORM_EPS = 1e-6
NEG_BIG = -1e30
LOG2E = 1.4426950408889634

RET_CHUNK = 256
NA_QROWS = 4
NA_KROWS = 12
NA_STEP_BLOCKS = 2
MOE_TS = 512
DMA_UNROLL = 8
COMBINE_GROUP = 32
FFN_CHUNKS = 2
IN_PROJ_SUB = 512
OUT_PROJ_SUB = 512
MXU_COLS = 256
IN_PROJ_CHUNK = 3 * MXU_COLS
VMEM_LIMIT = 56 * 1024 * 1024

CB_RQ, CB_RK, CB_RV, CB_RG = 0, 3, 6, 9
CB_CB, CB_CC, CB_CX = 12, 14, 16
CB_NQ, CB_NK, CB_NV = 18, 21, 24


def _cp(sem, vmem=VMEM_LIMIT):
    return pltpu.CompilerParams(dimension_semantics=sem, vmem_limit_bytes=vmem)


def _silu(x):
    return x * (1.0 / (1.0 + jnp.exp(-x)))


def _dot(a, b):
    return jnp.dot(a, b, preferred_element_type=F32)


def _dot_nt(a, b):
    return lax.dot_general(a, b, (((1,), (1,)), ((), ())), preferred_element_type=F32)


def _dot_tn(a, b):
    return lax.dot_general(a, b, (((0,), (0,)), ((), ())), preferred_element_type=F32)


def _split_bf16(x):
    hi = x.astype(BF16)
    lo = (x - hi.astype(F32)).astype(BF16)
    return hi, lo


def _ada_kernel(c_ref, w_ref, b_ref, o_ref):
    x = _silu(c_ref[...]).astype(BF16)
    o_ref[0] = _dot(x, w_ref[0].astype(BF16)) + b_ref[0]


def _ada(c8, ada_w, ada_b):
    depth, d, n = ada_w.shape
    tn = n // 4
    return pl.pallas_call(
        _ada_kernel,
        grid=(depth, n // tn),
        in_specs=[pl.BlockSpec((8, d), lambda l, j: (0, 0)),
                  pl.BlockSpec((1, d, tn), lambda l, j: (l, 0, j)),
                  pl.BlockSpec((1, 1, tn), lambda l, j: (l, 0, j))],
        out_specs=pl.BlockSpec((1, 8, tn), lambda l, j: (l, 0, j)),
        out_shape=jax.ShapeDtypeStruct((depth, 8, n), F32),
        compiler_params=_cp(("parallel", "parallel")),
        name="ada_mod",
    )(c8, ada_w, ada_b.reshape(depth, 1, n))


def _norm_mod(x, g, sh, sc):
    ms = jnp.mean(x * x, axis=-1, keepdims=True)
    y = x * lax.rsqrt(ms + NORM_EPS) * g
    return y * (1.0 + sc) + sh


def _project_rows(x, g_ref, sh_ref, sc_ref, w_ref, rope_refs, o_ref, rs):
    rope = rope_refs is not None
    xn = _norm_mod(x, g_ref[...], sh_ref[0], sc_ref[0]).astype(BF16)
    cw = IN_PROJ_CHUNK
    for c0 in range(0, D_IN_PROJ, cw):
        c1 = min(c0 + cw, D_IN_PROJ)
        acc = _dot(xn, w_ref[:, c0:c1])
        for j in range((c1 - c0) // LANES):
            blk = acc[:, j * LANES:(j + 1) * LANES]
            cb = c0 // LANES + j
            if rope and cb < CB_RV:
                cos_ref, sa_ref, sb_ref = rope_refs
                blk = (blk * cos_ref[rs, :] + pltpu.roll(blk, 16, 1) * sa_ref[rs, :]
                       + pltpu.roll(blk, LANES - 16, 1) * sb_ref[rs, :])
            if CB_RK <= cb < CB_RV:
                blk = blk * (HEAD_DIM ** -0.5)
            elif CB_NQ <= cb < CB_NK:
                blk = blk * (HEAD_DIM ** -0.5 * (LOG2E if rope else 1.0))
            o_ref[rs, cb * LANES:(cb + 1) * LANES] = blk.astype(BF16)


def _inproj_kernel(*refs, rope, n_cast):
    n_in = (8 if rope else 5) + n_cast
    for c_ref, cb_ref in zip(refs[n_in - n_cast:n_in], refs[n_in + 1:]):
        cb_ref[...] = c_ref[...].astype(BF16)
    h_ref, g_ref, sh_ref, sc_ref, w_ref = refs[:5]
    rope_refs = refs[5:8] if rope else None
    o_ref = refs[n_in]
    tm = h_ref.shape[0]
    sub = min(IN_PROJ_SUB, tm)
    for r in range(tm // sub):
        rs = slice(r * sub, (r + 1) * sub)
        _project_rows(h_ref[rs, :], g_ref, sh_ref, sc_ref, w_ref, rope_refs, o_ref, rs)


def _in_proj(h2, g, sh, sc, w_bf, tables, seq, tm, casts=()):
    t, d = h2.shape
    tiles_per_seq = seq // tm
    steps = t // tm
    rope = tables is not None
    in_specs = [pl.BlockSpec((tm, d), lambda i: (i, 0)),
                pl.BlockSpec((1, d), lambda i: (0, 0)),
                pl.BlockSpec((1, 1, d), lambda i: (i // tiles_per_seq, 0, 0)),
                pl.BlockSpec((1, 1, d), lambda i: (i // tiles_per_seq, 0, 0)),
                pl.BlockSpec((d, D_IN_PROJ), lambda i: (0, 0))]
    args = [h2, g.reshape(1, d), sh, sc, w_bf]
    if rope:
        in_specs += [pl.BlockSpec((tm, LANES), lambda i: (i % tiles_per_seq, 0))] * 3
        args += list(tables)
    out_specs = [pl.BlockSpec((tm, D_IN_PROJ), lambda i: (i, 0))]
    out_shape = [jax.ShapeDtypeStruct((t, D_IN_PROJ), BF16)]
    for c in casts:
        cr, cc = c.shape
        assert cr % (steps * 16) == 0
        blk = pl.BlockSpec((cr // steps, cc), lambda i: (i, 0))
        in_specs.append(blk)
        out_specs.append(blk)
        out_shape.append(jax.ShapeDtypeStruct(c.shape, BF16))
        args.append(c)
    res = pl.pallas_call(
        functools.partial(_inproj_kernel, rope=rope, n_cast=len(casts)),
        grid=(steps,),
        in_specs=in_specs,
        out_specs=out_specs,
        out_shape=out_shape,
        compiler_params=_cp(("parallel",)),
        name="in_proj_rope" if rope else "in_proj_ctx",
    )(*args)
    return res if casts else res[0]


def _rope_tables(seq):
    t = np.arange(seq)
    row = (t // GRID_W).astype(np.float32)
    col = (t % GRID_W).astype(np.float32)
    n_freq = HEAD_DIM // 4
    inv_freq = (ROPE_BASE ** (-np.arange(n_freq, dtype=np.float32) / n_freq)).astype(np.float32)
    ang_r = row[:, None] * inv_freq
    ang_c = col[:, None] * inv_freq
    cos_h = np.concatenate([np.cos(ang_r), np.cos(ang_r), np.cos(ang_c), np.cos(ang_c)], axis=1)
    sin_h = np.concatenate([np.sin(ang_r), np.sin(ang_r), np.sin(ang_c), np.sin(ang_c)], axis=1)
    lane = np.arange(HEAD_DIM)
    second = (lane % 32) >= 16
    sa = np.where(second[None, :], sin_h, 0.0)
    sb = np.where(second[None, :], 0.0, -sin_h)
    tile2 = lambda a: jnp.asarray(np.concatenate([a, a], axis=1), F32)
    return tile2(cos_h), tile2(sa), tile2(sb)


def _ret_kernel(q_ref, k_ref, v_ref, g_ref, lgt_ref, gn_ref, sf0_ref, sb0_ref,
                y_ref, sfo_ref, sbo_ref, sfs_ref, sbs_ref, *, seq):
    c = RET_CHUNK
    n_chunks = seq // c
    lg = jax.nn.log_sigmoid(lgt_ref[0])
    lgf, lgb = lg[0:1, :], lg[1:2, :]
    pos = lax.broadcasted_iota(jnp.int32, (c, 1), 0).astype(F32)
    dkf = jnp.exp(lgf * (c - 1.0 - pos))
    dkb = jnp.exp(lgb * pos)
    dqf = jnp.exp(lgf * (pos + 1.0))
    dqb = jnp.exp(lgb * (c - pos))
    cdf = jnp.exp(lgf * float(c))
    cdb = jnp.exp(lgb * float(c))
    lane = lax.broadcasted_iota(jnp.int32, (1, LANES), 1)
    first = lane < HEAD_DIM
    ri = lax.broadcasted_iota(jnp.int32, (LANES, LANES), 0)
    ci = lax.broadcasted_iota(jnp.int32, (LANES, LANES), 1)
    same = (ri < HEAD_DIM) == (ci < HEAD_DIM)
    bd = same.astype(F32)

    def head_mean(x):
        a = jnp.sum(jnp.where(first, x, 0.0), axis=-1, keepdims=True)
        b = jnp.sum(jnp.where(first, 0.0, x), axis=-1, keepdims=True)
        return jnp.where(first, a, b) * (1.0 / HEAD_DIM)

    ii = lax.broadcasted_iota(jnp.int32, (c, c), 0)
    jj = lax.broadcasted_iota(jnp.int32, (c, c), 1)
    dif = (ii - jj).astype(F32)

    def decay_mask(h0):
        lf = lgf[:, h0:h0 + 1]
        lb = lgb[:, h0:h0 + 1]
        return jnp.where(dif > 0, jnp.exp(lf * jnp.maximum(dif, 0.0)),
                         jnp.where(dif < 0, jnp.exp(lb * jnp.maximum(-dif, 0.0)), 2.0))

    dm = (decay_mask(0), decay_mask(HEAD_DIM))
    mfirst = first.astype(BF16)
    msecond = (1.0 - first.astype(F32)).astype(BF16)

    unroll = min(8, n_chunks)

    def incr(n, carry):
        sl = pl.ds(pl.multiple_of(n * c, c), c)
        kf = k_ref[0, sl, :].astype(F32)
        v = v_ref[0, sl, :]
        sfs_ref[n] = _dot_tn((kf * dkf).astype(BF16), v) * bd
        sbs_ref[n] = _dot_tn((kf * dkb).astype(BF16), v) * bd
        return carry

    lax.fori_loop(0, n_chunks, incr, 0, unroll=unroll)

    def scan_f(n, s):
        u = sfs_ref[n]
        sfs_ref[n] = s
        return cdf * s + u

    def scan_b(i, s):
        n = n_chunks - 1 - i
        u = sbs_ref[n]
        sbs_ref[n] = s
        return cdb * s + u

    sfo_ref[0, 0] = lax.fori_loop(0, n_chunks, scan_f, sf0_ref[0, 0])
    sbo_ref[0, 0] = lax.fori_loop(0, n_chunks, scan_b, sb0_ref[0, 0])

    def outp(n, carry):
        sl = pl.ds(pl.multiple_of(n * c, c), c)
        q = q_ref[0, sl, :]
        k = k_ref[0, sl, :]
        v = v_ref[0, sl, :]
        qf = q.astype(F32)
        o = None
        for hh, mk in enumerate((mfirst, msecond)):
            s = _dot_nt(q * mk, k) * dm[hh]
            oh = _dot(s.astype(BF16), v)
            o = oh if o is None else jnp.where(first, o, oh)
        qcat = jnp.concatenate([(qf * dqf).astype(BF16), (qf * dqb).astype(BF16)], axis=1)
        scat = jnp.concatenate([sfs_ref[n], sbs_ref[n]], axis=0).astype(BF16)
        o = o + _dot(qcat, scat)
        mu = head_mean(o)
        dlt = o - mu
        var = head_mean(dlt * dlt)
        on = dlt * lax.rsqrt(var + NORM_EPS) * gn_ref[0]
        y_ref[0, sl, :] = (_silu(g_ref[0, sl, :].astype(F32)) * on).astype(BF16)
        return carry

    lax.fori_loop(0, n_chunks, outp, 0, unroll=unroll)


def _retention(proj, lgt, gn, sf0, sb0):
    b, seq, _ = proj.shape
    col = lambda off: pl.BlockSpec((1, seq, LANES), lambda bi, p: (bi, 0, off + p))
    st = pl.BlockSpec((1, 1, LANES, LANES), lambda bi, p: (bi, p, 0, 0))
    return pl.pallas_call(
        functools.partial(_ret_kernel, seq=seq),
        grid=(b, N_PAIRS),
        in_specs=[col(CB_RQ), col(CB_RK), col(CB_RV), col(CB_RG),
                  pl.BlockSpec((1, 2, LANES), lambda bi, p: (p, 0, 0)),
                  pl.BlockSpec((1, 1, LANES), lambda bi, p: (p, 0, 0)),
                  st, st],
        out_specs=[pl.BlockSpec((1, seq, LANES), lambda bi, p: (bi, 0, p)), st, st],
        out_shape=[jax.ShapeDtypeStruct((b, seq, D_RET), BF16),
                   jax.ShapeDtypeStruct((b, N_PAIRS, LANES, LANES), F32),
                   jax.ShapeDtypeStruct((b, N_PAIRS, LANES, LANES), F32)],
        scratch_shapes=[pltpu.VMEM((seq // RET_CHUNK, LANES, LANES), F32),
                        pltpu.VMEM((seq // RET_CHUNK, LANES, LANES), F32)],
        compiler_params=_cp(("parallel", "parallel")),
        name="retention_s%d" % seq,
    )(proj, proj, proj, proj, lgt, gn, sf0, sb0)


def _conv_kernel(b_ref, c_ref, x_ref, w_ref, y_ref, *, seq):
    u = c_ref[0].astype(F32) * x_ref[0].astype(F32)
    row = lax.broadcasted_iota(jnp.int32, (seq, 1), 0)
    prev = jnp.where(row == 0, 0.0, pltpu.roll(u, 1, 0))
    nxt = jnp.where(row == seq - 1, 0.0, pltpu.roll(u, seq - 1, 0))
    y = w_ref[0:1, :] * prev + w_ref[1:2, :] * u + w_ref[2:3, :] * nxt
    y_ref[0] = (b_ref[0].astype(F32) * y).astype(BF16)


def _short_conv(proj, conv_w):
    b, seq, _ = proj.shape
    nblk = D_CONV // LANES
    col = lambda off: pl.BlockSpec((1, seq, LANES), lambda bi, j: (bi, 0, off + j))
    return pl.pallas_call(
        functools.partial(_conv_kernel, seq=seq),
        grid=(b, nblk),
        in_specs=[col(CB_CB), col(CB_CC), col(CB_CX),
                  pl.BlockSpec((3, LANES), lambda bi, j: (0, j))],
        out_specs=pl.BlockSpec((1, seq, LANES), lambda bi, j: (bi, 0, j)),
        out_shape=jax.ShapeDtypeStruct((b, seq, D_CONV), BF16),
        compiler_params=_cp(("parallel", "parallel")),
        name="short_conv_s%d" % seq,
    )(proj, proj, proj, conv_w)


def _na_bias_plan(rows):
    plan = np.full((3, NA_QROWS, NA_KROWS), -1, np.int64)
    for cl, rb in enumerate((0, NA_QROWS, rows - NA_QROWS)):
        ws = int(np.clip(rb - NA_ROWS // 2, 0, rows - NA_KROWS))
        for i in range(NA_QROWS):
            r = rb + i
            r0 = int(np.clip(r - NA_ROWS // 2, 0, rows - NA_ROWS))
            for j in range(NA_KROWS):
                kr = ws + j
                if r0 <= kr < r0 + NA_ROWS:
                    plan[cl, i, j] = kr - r + (NA_ROWS - 1)
    return plan


def _na_bias_kernel(rpb_ref, o_ref, *, plan):
    w = GRID_W
    ndr, ndc = 2 * NA_ROWS - 1, 2 * NA_COLS - 1
    base = pl.program_id(0) * (ndr * ndc)
    c = lax.broadcasted_iota(jnp.int32, (w, w), 0)
    kc = lax.broadcasted_iota(jnp.int32, (w, w), 1)
    dcidx = jnp.clip(kc - c, -(NA_COLS - 1), NA_COLS - 1) + (NA_COLS - 1)
    c0 = jnp.clip(c - NA_COLS // 2, 0, w - NA_COLS)
    col_in = jnp.logical_and(kc >= c0, kc < c0 + NA_COLS)
    neg = jnp.full((w, w), NEG_BIG, F32)
    tiles = []
    for dr in range(ndr):
        t = neg
        for dcv in range(ndc):
            t = jnp.where(dcidx == dcv, rpb_ref[base + dr * ndc + dcv], t)
        tiles.append(jnp.where(col_in, t * LOG2E, NEG_BIG))
    for cl in range(plan.shape[0]):
        for i in range(plan.shape[1]):
            for j in range(0, plan.shape[2], 2):
                pair = [tiles[int(d)] if d >= 0 else neg for d in plan[cl, i, j:j + 2]]
                o_ref[0, cl, i * w:(i + 1) * w, j * w:(j + 2) * w] = jnp.concatenate(pair, axis=1)


def _na_bias(rpb, rows):
    h = rpb.shape[0]
    tq, tk = NA_QROWS * GRID_W, NA_KROWS * GRID_W
    return pl.pallas_call(
        functools.partial(_na_bias_kernel, plan=_na_bias_plan(rows)),
        grid=(h,),
        in_specs=[pl.BlockSpec(memory_space=pltpu.SMEM)],
        out_specs=pl.BlockSpec((1, 3, tq, tk), lambda hh: (hh, 0, 0, 0)),
        out_shape=jax.ShapeDtypeStruct((h, 3, tq, tk), F32),
        compiler_params=_cp(("parallel",)),
        name="na_bias_expand",
    )(rpb.reshape(-1))


def _na_kernel(*refs, rows, cast):
    if cast:
        q_ref, k_ref, v_ref, kc_ref, vc_ref, bias_ref, c_ref, o_ref, cb_ref, v1_ref, vc1_ref = refs
        cb_ref[...] = c_ref[...].astype(BF16)
    else:
        q_ref, k_ref, v_ref, kc_ref, vc_ref, bias_ref, o_ref, v1_ref, vc1_ref = refs
    step = pl.program_id(1)
    nqb = rows // NA_QROWS
    tq = NA_QROWS * GRID_W
    lane = lax.broadcasted_iota(jnp.int32, (1, LANES), 1)
    first = lane < HEAD_DIM

    @pl.when(step == 0)
    def _():
        one = jnp.ones((1, LANES), BF16)
        for p in range(N_PAIRS):
            cols = slice(p * LANES, (p + 1) * LANES)
            v1_ref[2 * p] = jnp.where(first, v_ref[0, :, cols], one)
            v1_ref[2 * p + 1] = jnp.where(first, one, v_ref[0, :, cols])
            vc1_ref[2 * p] = jnp.where(first, vc_ref[0, :, cols], one)
            vc1_ref[2 * p + 1] = jnp.where(first, one, vc_ref[0, :, cols])

    for sb in range(NA_STEP_BLOCKS):
        qb = step * NA_STEP_BLOCKS + sb
        cls = jnp.where(qb == 0, 0, jnp.where(qb == nqb - 1, 2, 1))
        ws = jnp.clip(qb * NA_QROWS - NA_ROWS // 2, 0, rows - NA_KROWS)
        sl = pl.ds(pl.multiple_of(ws * GRID_W, GRID_W), NA_KROWS * GRID_W)
        qrows = slice(sb * tq, (sb + 1) * tq)
        for p in range(N_PAIRS):
            cols = slice(p * LANES, (p + 1) * LANES)
            q = q_ref[0, qrows, cols]
            kw = k_ref[0, sl, cols]
            kc = kc_ref[0, :, cols]
            pv = []
            for hh in range(2):
                mk = (first if hh == 0 else jnp.logical_not(first)).astype(BF16)
                qh = q * mk
                s_loc = _dot_nt(qh, kw) + bias_ref[2 * p + hh, cls]
                s_ctx = _dot_nt(qh, kc)
                m = jnp.maximum(jnp.max(s_loc, axis=-1, keepdims=True), jnp.max(s_ctx, axis=-1, keepdims=True))
                p_loc = jnp.exp2(s_loc - m).astype(BF16)
                p_ctx = jnp.exp2(s_ctx - m).astype(BF16)
                pv.append(_dot(p_loc, v1_ref[2 * p + hh, sl, :]) + _dot(p_ctx, vc1_ref[2 * p + hh]))
            num = jnp.where(first, pv[0], pv[1])
            den = jnp.where(first, pltpu.roll(pv[0], HEAD_DIM, 1), pltpu.roll(pv[1], HEAD_DIM, 1))
            o_ref[0, qrows, cols] = (num / den).astype(BF16)


def _na(proj, proj_ctx, bias, cast=None):
    b, seq, _ = proj.shape
    ctx_len = proj_ctx.shape[1]
    rows = seq // GRID_W
    nqb = rows // NA_QROWS
    tq = NA_STEP_BLOCKS * NA_QROWS * GRID_W
    tk = NA_KROWS * GRID_W
    nh = 2 * N_PAIRS
    n_steps = nqb // NA_STEP_BLOCKS
    assert n_steps * NA_STEP_BLOCKS == nqb
    cq, ck, cv = (CB_NQ * LANES // D_NA, CB_NK * LANES // D_NA, CB_NV * LANES // D_NA)
    in_specs = [pl.BlockSpec((1, tq, D_NA), lambda bi, st: (bi, st, cq)),
                pl.BlockSpec((1, seq, D_NA), lambda bi, st: (bi, 0, ck)),
                pl.BlockSpec((1, seq, D_NA), lambda bi, st: (bi, 0, cv)),
                pl.BlockSpec((1, ctx_len, D_NA), lambda bi, st: (bi, 0, ck)),
                pl.BlockSpec((1, ctx_len, D_NA), lambda bi, st: (bi, 0, cv)),
                pl.BlockSpec(bias.shape, lambda bi, st: (0, 0, 0, 0), pipeline_mode=pl.Buffered(1))]
    out_specs = [pl.BlockSpec((1, tq, D_NA), lambda bi, st: (bi, st, 0))]
    out_shape = [jax.ShapeDtypeStruct((b, seq, D_NA), BF16)]
    args = [proj, proj, proj, proj_ctx, proj_ctx, bias]
    if cast is not None:
        cr, cc = cast.shape
        steps = b * n_steps
        assert cr % (steps * 16) == 0
        blk = pl.BlockSpec((cr // steps, cc), lambda bi, st: (bi * n_steps + st, 0))
        in_specs.append(blk)
        out_specs.append(blk)
        out_shape.append(jax.ShapeDtypeStruct(cast.shape, BF16))
        args.append(cast)
    res = pl.pallas_call(
        functools.partial(_na_kernel, rows=rows, cast=cast is not None),
        grid=(b, n_steps),
        in_specs=in_specs,
        out_specs=out_specs,
        out_shape=out_shape,
        scratch_shapes=[pltpu.VMEM((nh, seq, LANES), BF16), pltpu.VMEM((nh, ctx_len, LANES), BF16)],
        compiler_params=_cp(("parallel", "arbitrary")),
        name="na_attn",
    )(*args)
    return res if cast is not None else res[0]


def _ctx_attn_kernel(q_ref, k_ref, v_ref, o_ref):
    q = q_ref[0]
    k = k_ref[0]
    v = v_ref[0]
    lane = lax.broadcasted_iota(jnp.int32, (1, LANES), 1)
    first = lane < HEAD_DIM
    o = None
    for hh in range(2):
        mk = (first if hh == 0 else jnp.logical_not(first)).astype(BF16)
        s = _dot_nt(q * mk, k)
        m = jnp.max(s, axis=-1, keepdims=True)
        p = jnp.exp(s - m)
        l = jnp.sum(p, axis=-1, keepdims=True)
        oh = _dot(p.astype(BF16), v) / l
        o = oh if o is None else jnp.where(first, o, oh)
    o_ref[0] = o.astype(BF16)


def _ctx_attn(proj_ctx):
    b, ctx_len, _ = proj_ctx.shape
    col = lambda off: pl.BlockSpec((1, ctx_len, LANES), lambda bi, p: (bi, 0, off + p))
    return pl.pallas_call(
        _ctx_attn_kernel,
        grid=(b, N_PAIRS),
        in_specs=[col(CB_NQ), col(CB_NK), col(CB_NV)],
        out_specs=pl.BlockSpec((1, ctx_len, LANES), lambda bi, p: (bi, 0, p)),
        out_shape=jax.ShapeDtypeStruct((b, ctx_len, D_NA), BF16),
        compiler_params=_cp(("parallel", "parallel")),
        name="ctx_attn",
    )(proj_ctx, proj_ctx, proj_ctx)


def _outproj_kernel(*refs, route):
    if route:
        (yr_ref, yc_ref, yn_ref, h_ref, w_ref, g1_ref, n2_ref, sh_ref, sc_ref, rw_ref, rb_ref,
         ho_ref, xn_ref, idx_ref, wt_ref) = refs
    else:
        yr_ref, yc_ref, yn_ref, h_ref, w_ref, g1_ref, n2_ref, sh_ref, sc_ref, ho_ref, xn_ref = refs
    tm = h_ref.shape[0]
    sub = min(OUT_PROJ_SUB, tm)
    if route:
        r_hi, r_lo = _split_bf16(rw_ref[...])
        r_both = jnp.concatenate([r_hi, r_lo], axis=0)
    for r in range(tm // sub):
        rs = slice(r * sub, (r + 1) * sub)
        ycat = jnp.concatenate([yr_ref[rs, :], yc_ref[rs, :], yn_ref[rs, :]], axis=-1)
        h = h_ref[rs, :] + g1_ref[0] * _dot(ycat, w_ref[...])
        ho_ref[rs, :] = h
        a = _norm_mod(h, n2_ref[...], sh_ref[0], sc_ref[0])
        xn_ref[rs, :] = a.astype(xn_ref.dtype)
        if route:
            a_hi, a_lo = _split_bf16(a)
            t_hi = _dot_nt(r_both, a_hi)
            logits = t_hi[:N_EXPERTS] + t_hi[N_EXPERTS:] + _dot_nt(r_hi, a_lo) + rb_ref[...]
            eidx = lax.broadcasted_iota(jnp.int32, logits.shape, 0)
            m1 = jnp.max(logits, axis=0, keepdims=True)
            i1 = jnp.min(jnp.where(logits == m1, eidx, N_EXPERTS), axis=0, keepdims=True)
            rest = jnp.where(eidx == i1, -jnp.inf, logits)
            m2 = jnp.max(rest, axis=0, keepdims=True)
            i2 = jnp.min(jnp.where(rest == m2, eidx, N_EXPERTS), axis=0, keepdims=True)
            e2 = jnp.exp(m2 - m1)
            den = 1.0 + e2
            idx_ref[:, rs] = jnp.concatenate([i1, i2], axis=0)
            wt_ref[:, rs] = jnp.concatenate([1.0 / den, e2 / den], axis=0)


def _out_proj(yr, yc, yn, h2, w_bf, g1, n2g, sh2, sc2, seq, tm, router=None):
    t, d = h2.shape
    tps = seq // tm
    route = router is not None
    row = lambda wdt: pl.BlockSpec((tm, wdt), lambda i: (i, 0))
    mod = pl.BlockSpec((1, 1, d), lambda i: (i // tps, 0, 0))
    in_specs = [row(D_RET), row(D_CONV), row(D_NA), row(d),
                pl.BlockSpec((d, d), lambda i: (0, 0)), mod,
                pl.BlockSpec((1, d), lambda i: (0, 0)), mod, mod]
    args = [yr, yc, yn, h2, w_bf, g1, n2g.reshape(1, d), sh2, sc2]
    out_specs = [row(d), row(d)]
    out_shape = [jax.ShapeDtypeStruct((t, d), F32), jax.ShapeDtypeStruct((t, d), BF16)]
    if route:
        rw, rb = router
        in_specs += [pl.BlockSpec((N_EXPERTS, d), lambda i: (0, 0)),
                     pl.BlockSpec((N_EXPERTS, 1), lambda i: (0, 0))]
        args += [rw.T, rb.reshape(N_EXPERTS, 1)]
        out_specs += [pl.BlockSpec((2, tm), lambda i: (0, i))] * 2
        out_shape += [jax.ShapeDtypeStruct((2, t), jnp.int32), jax.ShapeDtypeStruct((2, t), F32)]
    return pl.pallas_call(
        functools.partial(_outproj_kernel, route=route),
        grid=(t // tm,),
        in_specs=in_specs,
        out_specs=out_specs,
        out_shape=out_shape,
        compiler_params=_cp(("parallel",)),
        name="out_proj_route" if route else "out_proj_t%d" % t,
    )(*args)


def _swiglu_chunks(x, w_in_ref, w_out_ref, ff, n_chunks):
    tf = -(-ff // (n_chunks * MXU_COLS)) * MXU_COLS
    y = None
    for c0 in range(0, ff, tf):
        c1 = min(c0 + tf, ff)
        gate = _dot(x, w_in_ref[:, c0:c1])
        up = _dot(x, w_in_ref[:, ff + c0:ff + c1])
        part = _dot((_silu(gate) * up).astype(BF16), w_out_ref[c0:c1, :])
        y = part if y is None else y + part
    return y


def _outproj_ffn_kernel(*refs, ff, n_chunks, proj_next):
    (yr_ref, yc_ref, yn_ref, h_ref, wo_ref, g1_ref, n2_ref, sh_ref, sc_ref, wi_ref, wf_ref, g2_ref) = refs[:12]
    o_ref = refs[19] if proj_next else refs[12]
    ycat = jnp.concatenate([yr_ref[...], yc_ref[...], yn_ref[...]], axis=-1)
    h1 = h_ref[...] + g1_ref[0] * _dot(ycat, wo_ref[...])
    a = _norm_mod(h1, n2_ref[...], sh_ref[0], sc_ref[0]).astype(BF16)
    out = h1 + g2_ref[0] * _swiglu_chunks(a, wi_ref, wf_ref, ff, n_chunks)
    o_ref[...] = out
    if proj_next:
        g_ref, sh1_ref, sc1_ref, w_ref = refs[12:16]
        _project_rows(out, g_ref, sh1_ref, sc1_ref, w_ref, refs[16:19], refs[20], slice(None))


def _outproj_ffn(yr, yc, yn, h2, wo_bf, g1, n2g, sh2, sc2, fw_in_bf, fw_out_bf, g2, seq, tm, n_chunks,
                 proj_next=None):
    t, d = h2.shape
    ff = fw_out_bf.shape[0]
    tps = seq // tm
    resident = pl.Buffered(1)
    row = lambda wdt: pl.BlockSpec((tm, wdt), lambda i: (i, 0))
    mod = pl.BlockSpec((1, 1, d), lambda i: (i // tps, 0, 0))
    in_specs = [row(D_RET), row(D_CONV), row(D_NA), row(d),
                pl.BlockSpec((d, d), lambda i: (0, 0), pipeline_mode=resident), mod,
                pl.BlockSpec((1, d), lambda i: (0, 0)), mod, mod,
                pl.BlockSpec((d, 2 * ff), lambda i: (0, 0), pipeline_mode=resident),
                pl.BlockSpec((ff, d), lambda i: (0, 0), pipeline_mode=resident), mod]
    args = [yr, yc, yn, h2, wo_bf, g1, n2g.reshape(1, d), sh2, sc2, fw_in_bf, fw_out_bf, g2]
    out_specs = [row(d)]
    out_shape = [jax.ShapeDtypeStruct((t, d), F32)]
    if proj_next is not None:
        n1g, sh1, sc1, w_in_bf, tables = proj_next
        in_specs += [pl.BlockSpec((1, d), lambda i: (0, 0)), mod, mod,
                     pl.BlockSpec((d, D_IN_PROJ), lambda i: (0, 0), pipeline_mode=resident)]
        in_specs += [pl.BlockSpec((tm, LANES), lambda i: (i % tps, 0))] * 3
        args += [n1g.reshape(1, d), sh1, sc1, w_in_bf] + list(tables)
        out_specs.append(row(D_IN_PROJ))
        out_shape.append(jax.ShapeDtypeStruct((t, D_IN_PROJ), BF16))
    res = pl.pallas_call(
        functools.partial(_outproj_ffn_kernel, ff=ff, n_chunks=n_chunks, proj_next=proj_next is not None),
        grid=(t // tm,),
        in_specs=in_specs,
        out_specs=out_specs,
        out_shape=out_shape,
        compiler_params=_cp(("parallel",)),
        name="out_proj_ffn_t%d" % t,
    )(*args)
    return res if proj_next is not None else res[0]


def _dispatch_kernel(s0_ref, s1_ref, zs_ref, zv_ref, x_ref, xs_ref, slab_ref, zbuf_ref, sem, *, rows):
    @pl.when(pl.program_id(0) == 0)
    def _():
        zbuf_ref[...] = jnp.zeros_like(zbuf_ref)
        for e in range(2 * N_EXPERTS):
            @pl.when(zv_ref[e] != 0)
            def _():
                dst = xs_ref.at[pl.ds(pl.multiple_of(zs_ref[e], MOE_TS), MOE_TS)]
                pltpu.make_async_copy(zbuf_ref, dst, sem.at[1]).start()
        for e in range(2 * N_EXPERTS):
            @pl.when(zv_ref[e] != 0)
            def _():
                pltpu.make_async_copy(zbuf_ref, xs_ref.at[pl.ds(0, MOE_TS)], sem.at[1]).wait()

    nk = slab_ref.shape[1]

    def issue(g, carry):
        rs = pl.ds(pl.multiple_of(g * DMA_UNROLL, DMA_UNROLL), DMA_UNROLL)
        for k in range(nk):
            slab_ref[rs, k, :] = x_ref[rs, k * LANES:(k + 1) * LANES].astype(slab_ref.dtype)
        for u in range(DMA_UNROLL):
            r = g * DMA_UNROLL + u
            src = slab_ref.at[r]
            pltpu.make_async_copy(src, xs_ref.at[s0_ref[r]], sem.at[0]).start(priority=u % 2)
            pltpu.make_async_copy(src, xs_ref.at[s1_ref[r]], sem.at[0]).start(priority=(u + 1) % 2)
        return carry

    lax.fori_loop(0, rows // DMA_UNROLL, issue, 0)
    for _ in range(2):
        pltpu.make_async_copy(slab_ref, xs_ref.at[pl.ds(0, rows)], sem.at[0]).wait()


def _dispatch(xn, slot0, slot1, zplan, n_slots, rows):
    t, d = xn.shape
    slab = (d // LANES, LANES)
    zstart, zvalid = zplan
    return pl.pallas_call(
        functools.partial(_dispatch_kernel, rows=rows),
        grid=(t // rows,),
        in_specs=[pl.BlockSpec((rows,), lambda i: (i,), memory_space=pltpu.SMEM),
                  pl.BlockSpec((rows,), lambda i: (i,), memory_space=pltpu.SMEM),
                  pl.BlockSpec((2 * N_EXPERTS,), lambda i: (0,), memory_space=pltpu.SMEM),
                  pl.BlockSpec((2 * N_EXPERTS,), lambda i: (0,), memory_space=pltpu.SMEM),
                  pl.BlockSpec((rows, d), lambda i: (i, 0))],
        out_specs=pl.BlockSpec(memory_space=pl.ANY),
        out_shape=jax.ShapeDtypeStruct((n_slots,) + slab, F32),
        scratch_shapes=[pltpu.VMEM((rows,) + slab, F32), pltpu.VMEM((MOE_TS,) + slab, F32),
                        pltpu.SemaphoreType.DMA((2,))],
        compiler_params=_cp(("arbitrary",)),
        name="moe_dispatch",
    )(slot0, slot1, zstart, zvalid, xn)


def _expert_kernel(be_ref, nu_ref, xs_ref, wi_ref, wo_ref, o_ref, xa_ref, xb_ref, sem, *, ff, n_chunks):
    j = pl.program_id(0)
    n = pl.num_programs(0)
    ts, d = xa_ref.shape
    bufs = (xa_ref, xb_ref)

    def copies(blk, b):
        rows = pl.ds(pl.multiple_of(blk * ts, ts), ts)
        return [pltpu.make_async_copy(xs_ref.at[rows, k, :], bufs[b].at[:, k * LANES:(k + 1) * LANES],
                                      sem.at[b]) for k in range(d // LANES)]

    @pl.when(j == 0)
    def _():
        for c in copies(0, 0):
            c.start()

    def step(b):
        @pl.when(j + 1 < n)
        def _():
            for c in copies(j + 1, 1 - b):
                c.start()

        for c in copies(j, b):
            c.wait()

        @pl.when(j < nu_ref[0])
        def _():
            o_ref[...] = _swiglu_chunks(bufs[b][...].astype(BF16), wi_ref.at[0], wo_ref.at[0], ff, n_chunks)

        @pl.when(j >= nu_ref[0])
        def _():
            o_ref[...] = jnp.zeros_like(o_ref)

    for b in range(2):
        pl.when(j % 2 == b)(functools.partial(step, b))


def _experts(xs, block_expert, n_used, w_in_bf, w_out_bf, n_chunks):
    n, nk, _ = xs.shape
    d = nk * LANES
    ff = w_out_bf.shape[1]
    ts = MOE_TS
    grid_spec = pltpu.PrefetchScalarGridSpec(
        num_scalar_prefetch=2,
        grid=(n // ts,),
        in_specs=[pl.BlockSpec(memory_space=pl.ANY),
                  pl.BlockSpec((1, d, 2 * ff), lambda j, be, nu: (be[j], 0, 0)),
                  pl.BlockSpec((1, ff, d), lambda j, be, nu: (be[j], 0, 0))],
        out_specs=pl.BlockSpec((ts, d), lambda j, be, nu: (j, 0)),
        scratch_shapes=[pltpu.VMEM((ts, d), xs.dtype), pltpu.VMEM((ts, d), xs.dtype),
                        pltpu.SemaphoreType.DMA((2,))],
    )
    return pl.pallas_call(
        functools.partial(_expert_kernel, ff=ff, n_chunks=n_chunks),
        grid_spec=grid_spec,
        out_shape=jax.ShapeDtypeStruct((n, d), F32),
        compiler_params=_cp(("arbitrary",)),
        name="moe_experts",
    )(block_expert, n_used, xs, w_in_bf, w_out_bf)


def _combine_kernel(p0_ref, p1_ref, q0_ref, q1_ref, ys_ref, h_ref, w0_ref, w1_ref, g2_ref, fg_ref, o_ref,
                    ya0_ref, ya1_ref, yb0_ref, yb1_ref, sem, *, rows):
    i = pl.program_id(0)
    n = pl.num_programs(0)
    n_iter = rows // COMBINE_GROUP
    bufs = ((ya0_ref, ya1_ref), (yb0_ref, yb1_ref))

    def issue_rows(pa_ref, pb_ref, b, g):
        for u in range(COMBINE_GROUP):
            r = g * COMBINE_GROUP + u
            pltpu.make_async_copy(ys_ref.at[pl.ds(pa_ref[r], 1)], bufs[b][0].at[pl.ds(r, 1)],
                                  sem.at[0, b]).start(priority=u % 2)
            pltpu.make_async_copy(ys_ref.at[pl.ds(pb_ref[r], 1)], bufs[b][1].at[pl.ds(r, 1)],
                                  sem.at[1, b]).start(priority=(u + 1) % 2)

    def finish_rows(b, g):
        rs = pl.ds(pl.multiple_of(g * COMBINE_GROUP, COMBINE_GROUP), COMBINE_GROUP)
        y = w0_ref[rs, :] * bufs[b][0][rs, :] + w1_ref[rs, :] * bufs[b][1][rs, :]
        h = h_ref[rs, :] + g2_ref[0] * y
        ms = jnp.mean(h * h, axis=-1, keepdims=True)
        o_ref[rs, :] = h * lax.rsqrt(ms + NORM_EPS) * fg_ref[...]

    @pl.when(i == 0)
    def _():
        def first(g, carry):
            issue_rows(p0_ref, p1_ref, 0, g)
            return carry
        lax.fori_loop(0, n_iter, first, 0)

    def step(b):
        pltpu.make_async_copy(ys_ref.at[pl.ds(0, rows)], bufs[b][0], sem.at[0, b]).wait()
        pltpu.make_async_copy(ys_ref.at[pl.ds(0, rows)], bufs[b][1], sem.at[1, b]).wait()

        @pl.when(i + 1 < n)
        def _():
            def both(g, carry):
                issue_rows(q0_ref, q1_ref, 1 - b, g)
                finish_rows(b, g)
                return carry
            lax.fori_loop(0, n_iter, both, 0)

        @pl.when(i + 1 == n)
        def _():
            def last(g, carry):
                finish_rows(b, g)
                return carry
            lax.fori_loop(0, n_iter, last, 0)

    for b in range(2):
        pl.when(i % 2 == b)(functools.partial(step, b))


def _combine(pos0, pos1, ys, h2, w0, w1, g2, final_g, seq, rows):
    t, d = h2.shape
    tps = seq // rows
    steps = t // rows
    blk = pl.BlockSpec((rows, d), lambda i: (i, 0))
    col = pl.BlockSpec((rows, 1), lambda i: (i, 0))
    here = pl.BlockSpec((rows,), lambda i: (i,), memory_space=pltpu.SMEM)
    ahead = pl.BlockSpec((rows,), lambda i: (jnp.minimum(i + 1, steps - 1),), memory_space=pltpu.SMEM)
    return pl.pallas_call(
        functools.partial(_combine_kernel, rows=rows),
        grid=(steps,),
        in_specs=[here, here, ahead, ahead,
                  pl.BlockSpec(memory_space=pl.ANY),
                  blk, col, col,
                  pl.BlockSpec((1, 1, d), lambda i: (i // tps, 0, 0)),
                  pl.BlockSpec((1, d), lambda i: (0, 0))],
        out_specs=blk,
        out_shape=jax.ShapeDtypeStruct((t, d), F32),
        scratch_shapes=[pltpu.VMEM((rows, d), F32)] * 4 + [pltpu.SemaphoreType.DMA((2, 2))],
        compiler_params=_cp(("arbitrary",)),
        name="moe_combine_norm",
    )(pos0, pos1, pos0, pos1, ys, h2, w0, w1, g2, final_g.reshape(1, d))


def _route_plan(idx):
    t = idx.shape[1]
    ts = MOE_TS
    n_slots = 2 * t + N_EXPERTS * ts
    e_flat = idx.reshape(-1)
    onehot = (e_flat[:, None] == jnp.arange(N_EXPERTS, dtype=jnp.int32)[None, :]).astype(jnp.int32)
    csum = jnp.cumsum(onehot, axis=0)
    rank = jnp.sum((csum - onehot) * onehot, axis=1)
    counts = csum[-1]
    padded = ((counts + ts - 1) // ts) * ts
    ends = jnp.cumsum(padded)
    offs = ends - padded
    slot = (jnp.sum(onehot * offs[None, :], axis=1) + rank).astype(jnp.int32)
    starts = jnp.arange(n_slots // ts, dtype=jnp.int32) * ts
    block_expert = jnp.minimum(jnp.sum((starts[:, None] >= ends[None, :]).astype(jnp.int32), axis=1),
                               N_EXPERTS - 1).astype(jnp.int32)
    n_used = (ends[-1] // ts).astype(jnp.int32).reshape(1)
    tail = ends[-1] + jnp.arange(N_EXPERTS, dtype=jnp.int32) * ts
    zstart = jnp.concatenate([ends - ts, tail])
    zvalid = jnp.concatenate([padded > 0, tail < n_slots]).astype(jnp.int32)
    zstart = jnp.where(zvalid != 0, zstart, 0).astype(jnp.int32)
    return slot[:t], slot[t:], block_expert, n_used, (zstart, zvalid), n_slots


def kernel(x, c, ctx, c_ctx, ada_w, ada_b, norm1_g, norm2_g, w_in, w_out, ret_decay_logit, ret_gn_g,
           conv_w, na_rpb, ffn_w_in, ffn_w_out, moe_router_w, moe_router_b, moe_w_in, moe_w_out, final_g):
    b, seq, d = x.shape
    ctx_len = ctx.shape[1]
    depth = ada_w.shape[0]
    assert depth == 2, "the final norm is fused into the last (MoE) layer's combine step"
    rows = seq // GRID_W
    t_lat, t_ctx = b * seq, b * ctx_len
    tm_lat = min(1024, seq)
    tm_ctx = ctx_len

    c8 = jnp.zeros((8, d), F32).at[:b].set(c).at[b].set(c_ctx)
    mods = _ada(c8, ada_w, ada_b)
    tables = _rope_tables(seq)

    h = x.reshape(t_lat, d)
    hc = ctx.reshape(t_ctx, d)
    zero_state = jnp.zeros((b, N_PAIRS, LANES, LANES), F32)
    out = None
    w_in_bfs = [w_in[layer].astype(BF16) for layer in range(depth)]
    lats = [[mods[layer].reshape(8, 6, d)[:b, i].reshape(b, 1, d) for i in range(6)] for layer in range(depth)]
    proj_ahead = None
    for layer in range(depth):
        update_ctx = layer < depth - 1
        m = mods[layer].reshape(8, 6, d)
        lat = lats[layer]
        cx = [jnp.broadcast_to(m[b, i].reshape(1, 1, d), (b, 1, d)) for i in range(6)]
        w_in_bf = w_in_bfs[layer]
        w_out_bf = w_out[layer].astype(BF16)
        lgt = ret_decay_logit[layer].reshape(2, N_PAIRS, LANES // HEAD_DIM)
        lgt = jnp.repeat(lgt, HEAD_DIM, axis=2).transpose(1, 0, 2)
        gn = ret_gn_g[layer].reshape(N_PAIRS, 1, LANES)

        proj_c = _in_proj(hc, norm1_g[layer], cx[0], cx[1], w_in_bf, None, ctx_len, tm_ctx)
        proj_c = proj_c.reshape(b, ctx_len, D_IN_PROJ)
        if layer % 2 == 0:
            proj, fw_in, fw_out = _in_proj(h, norm1_g[layer], lat[0], lat[1], w_in_bf, tables, seq, tm_lat,
                                           casts=(ffn_w_in[layer // 2], ffn_w_out[layer // 2]))
        elif proj_ahead is not None:
            proj = proj_ahead
        else:
            proj = _in_proj(h, norm1_g[layer], lat[0], lat[1], w_in_bf, tables, seq, tm_lat)
        proj = proj.reshape(b, seq, D_IN_PROJ)

        y_ret_c, s_fwd, s_bwd = _retention(proj_c, lgt, gn, zero_state, zero_state)
        y_ret, _, _ = _retention(proj, lgt, gn, s_fwd, s_bwd)
        y_conv = _short_conv(proj, conv_w[layer])
        mw = moe_w_in[(layer + 1) // 2] if layer % 2 == 0 else moe_w_out[layer // 2]
        y_na, mw_bf = _na(proj, proj_c, _na_bias(na_rpb[layer], rows), cast=mw.reshape(-1, mw.shape[-1]))
        if layer % 2 == 0:
            mw_in_bf = mw_bf.reshape(mw.shape)
        else:
            mw_out_bf = mw_bf.reshape(mw.shape)

        if layer % 2 == 0:
            nxt = None
            if layer + 1 < depth:
                nxt = (norm1_g[layer + 1], lats[layer + 1][0], lats[layer + 1][1], w_in_bfs[layer + 1], tables)
            res = _outproj_ffn(y_ret.reshape(t_lat, D_RET), y_conv.reshape(t_lat, D_CONV),
                               y_na.reshape(t_lat, D_NA), h, w_out_bf, lat[2], norm2_g[layer], lat[3], lat[4],
                               fw_in, fw_out, lat[5], seq, min(512, seq), FFN_CHUNKS, proj_next=nxt)
            h, proj_ahead = res if nxt is not None else (res, None)
            if update_ctx:
                y_conv_c = _short_conv(proj_c, conv_w[layer])
                y_na_c = _ctx_attn(proj_c)
                hc = _outproj_ffn(y_ret_c.reshape(t_ctx, D_RET), y_conv_c.reshape(t_ctx, D_CONV),
                                  y_na_c.reshape(t_ctx, D_NA), hc, w_out_bf, cx[2], norm2_g[layer], cx[3], cx[4],
                                  fw_in, fw_out, cx[5], ctx_len, tm_ctx, FFN_CHUNKS)
        else:
            e = layer // 2
            h, xn, idx, wts = _out_proj(y_ret.reshape(t_lat, D_RET), y_conv.reshape(t_lat, D_CONV),
                                        y_na.reshape(t_lat, D_NA), h, w_out_bf, lat[2], norm2_g[layer],
                                        lat[3], lat[4], seq, tm_lat,
                                        router=(moe_router_w[e], moe_router_b[e]))
            pos0, pos1, block_expert, n_used, zstart, n_slots = _route_plan(idx)
            xs = _dispatch(xn, pos0, pos1, zstart, n_slots, min(1024, seq))
            ys = _experts(xs, block_expert, n_used, mw_in_bf, mw_out_bf, FFN_CHUNKS)
            out = _combine(pos0, pos1, ys, h, wts[0].reshape(t_lat, 1), wts[1].reshape(t_lat, 1),
                           lat[5], final_g, seq, min(512, seq))
            h = out
    return out.reshape(b, seq, d)
```

```python
import functools

import numpy as np
import jax
import jax.numpy as jnp
from jax import lax
from jax.experimental import pallas as pl
from jax.experimental.pallas import tpu as pltpu

F32 = jnp.float32
BF16 = jnp.bfloat16

LANES = 128
BF16_ROWS = 16
HEAD_DIM = 64
GRID_W = 64
N_CONV_GROUPS = 4
N_RET_HEADS = 6
N_NA_HEADS = 6
D_RET = N_RET_HEADS * HEAD_DIM
D_CONV = N_CONV_GROUPS * HEAD_DIM
D_NA = N_NA_HEADS * HEAD_DIM
D_IN_PROJ = 4 * D_RET + 3 * D_CONV + 3 * D_NA
N_PAIRS = D_RET // LANES
NA_ROWS = 8
NA_COLS = 16
N_EXPERTS = 8
ROPE_BASE = 10000.0
NORM_EPS = 1e-6
NEG_BIG = -1e30
LOG2E = 1.4426950408889634

RET_CHUNK = 256
NA_QROWS = 4
NA_KROWS = 12
NA_STEP_BLOCKS = 2
MOE_TS = 512
DMA_UNROLL = 8
COMBINE_GROUP = 32
FFN_CHUNKS = 2
IN_PROJ_SUB = 512
OUT_PROJ_SUB = 512
MXU_COLS = 256
IN_PROJ_CHUNK = 3 * MXU_COLS
VMEM_LIMIT = 56 * 1024 * 1024

CB_RQ, CB_RK, CB_RV, CB_RG = 0, 3, 6, 9
CB_CB, CB_CC, CB_CX = 12, 14, 16
CB_NQ, CB_NK, CB_NV = 18, 21, 24


def _cp(sem, vmem=VMEM_LIMIT):
    return pltpu.CompilerParams(dimension_semantics=sem, vmem_limit_bytes=vmem)


def _silu(x):
    return x * (1.0 / (1.0 + jnp.exp(-x)))


def _dot(a, b):
    return jnp.dot(a, b, preferred_element_type=F32)


def _dot_nt(a, b):
    return lax.dot_general(a, b, (((1,), (1,)), ((), ())), preferred_element_type=F32)


def _dot_tn(a, b):
    return lax.dot_general(a, b, (((0,), (0,)), ((), ())), preferred_element_type=F32)


def _split_bf16(x):
    hi = x.astype(BF16)
    lo = (x - hi.astype(F32)).astype(BF16)
    return hi, lo


def _ada_kernel(c_ref, w_ref, b_ref, o_ref):
    x = _silu(c_ref[...]).astype(BF16)
    o_ref[0] = _dot(x, w_ref[0].astype(BF16)) + b_ref[0]


def _ada(c8, ada_w, ada_b):
    depth, d, n = ada_w.shape
    tn = n // 4
    return pl.pallas_call(
        _ada_kernel,
        grid=(depth, n // tn),
        in_specs=[pl.BlockSpec((8, d), lambda l, j: (0, 0)),
                  pl.BlockSpec((1, d, tn), lambda l, j: (l, 0, j)),
                  pl.BlockSpec((1, 1, tn), lambda l, j: (l, 0, j))],
        out_specs=pl.BlockSpec((1, 8, tn), lambda l, j: (l, 0, j)),
        out_shape=jax.ShapeDtypeStruct((depth, 8, n), F32),
        compiler_params=_cp(("parallel", "parallel")),
        name="ada_mod",
    )(c8, ada_w, ada_b.reshape(depth, 1, n))


def _norm_mod(x, g, sh, sc):
    ms = jnp.mean(x * x, axis=-1, keepdims=True)
    y = x * lax.rsqrt(ms + NORM_EPS) * g
    return y * (1.0 + sc) + sh


def _inproj_kernel(*refs, rope, n_cast):
    n_in = (8 if rope else 5) + n_cast
    for c_ref, cb_ref in zip(refs[n_in - n_cast:n_in], refs[n_in + 1:]):
        cb_ref[...] = c_ref[...].astype(BF16)
    if rope:
        h_ref, g_ref, sh_ref, sc_ref, w_ref, cos_ref, sa_ref, sb_ref = refs[:8]
    else:
        h_ref, g_ref, sh_ref, sc_ref, w_ref = refs[:5]
    o_ref = refs[n_in]
    tm = h_ref.shape[0]
    sub = min(IN_PROJ_SUB, tm)
    cw = IN_PROJ_CHUNK
    for r in range(tm // sub):
        rs = slice(r * sub, (r + 1) * sub)
        xn = _norm_mod(h_ref[rs, :], g_ref[...], sh_ref[0], sc_ref[0]).astype(BF16)
        for c0 in range(0, D_IN_PROJ, cw):
            c1 = min(c0 + cw, D_IN_PROJ)
            acc = _dot(xn, w_ref[:, c0:c1])
            for j in range((c1 - c0) // LANES):
                blk = acc[:, j * LANES:(j + 1) * LANES]
                cb = c0 // LANES + j
                if rope and cb < CB_RV:
                    blk = (blk * cos_ref[rs, :] + pltpu.roll(blk, 16, 1) * sa_ref[rs, :]
                           + pltpu.roll(blk, LANES - 16, 1) * sb_ref[rs, :])
                if CB_RK <= cb < CB_RV:
                    blk = blk * (HEAD_DIM ** -0.5)
                elif CB_NQ <= cb < CB_NK:
                    blk = blk * (HEAD_DIM ** -0.5 * (LOG2E if rope else 1.0))
                o_ref[rs, cb * LANES:(cb + 1) * LANES] = blk.astype(BF16)


def _in_proj(h2, g, sh, sc, w_bf, tables, seq, tm, casts=()):
    t, d = h2.shape
    tiles_per_seq = seq // tm
    steps = t // tm
    rope = tables is not None
    in_specs = [pl.BlockSpec((tm, d), lambda i: (i, 0)),
                pl.BlockSpec((1, d), lambda i: (0, 0)),
                pl.BlockSpec((1, 1, d), lambda i: (i // tiles_per_seq, 0, 0)),
                pl.BlockSpec((1, 1, d), lambda i: (i // tiles_per_seq, 0, 0)),
                pl.BlockSpec((d, D_IN_PROJ), lambda i: (0, 0))]
    args = [h2, g.reshape(1, d), sh, sc, w_bf]
    if rope:
        in_specs += [pl.BlockSpec((tm, LANES), lambda i: (i % tiles_per_seq, 0))] * 3
        args += list(tables)
    out_specs = [pl.BlockSpec((tm, D_IN_PROJ), lambda i: (i, 0))]
    out_shape = [jax.ShapeDtypeStruct((t, D_IN_PROJ), BF16)]
    for c in casts:
        cr, cc = c.shape
        assert cr % (steps * BF16_ROWS) == 0
        blk = pl.BlockSpec((cr // steps, cc), lambda i: (i, 0))
        in_specs.append(blk)
        out_specs.append(blk)
        out_shape.append(jax.ShapeDtypeStruct(c.shape, BF16))
        args.append(c)
    res = pl.pallas_call(
        functools.partial(_inproj_kernel, rope=rope, n_cast=len(casts)),
        grid=(steps,),
        in_specs=in_specs,
        out_specs=out_specs,
        out_shape=out_shape,
        compiler_params=_cp(("parallel",)),
        name="in_proj_rope" if rope else "in_proj_ctx",
    )(*args)
    return res if casts else res[0]


def _rope_tables(seq):
    t = np.arange(seq)
    row = (t // GRID_W).astype(np.float32)
    col = (t % GRID_W).astype(np.float32)
    n_freq = HEAD_DIM // 4
    inv_freq = (ROPE_BASE ** (-np.arange(n_freq, dtype=np.float32) / n_freq)).astype(np.float32)
    ang_r = row[:, None] * inv_freq
    ang_c = col[:, None] * inv_freq
    cos_h = np.concatenate([np.cos(ang_r), np.cos(ang_r), np.cos(ang_c), np.cos(ang_c)], axis=1)
    sin_h = np.concatenate([np.sin(ang_r), np.sin(ang_r), np.sin(ang_c), np.sin(ang_c)], axis=1)
    lane = np.arange(HEAD_DIM)
    second = (lane % 32) >= 16
    sa = np.where(second[None, :], sin_h, 0.0)
    sb = np.where(second[None, :], 0.0, -sin_h)
    tile2 = lambda a: jnp.asarray(np.concatenate([a, a], axis=1), F32)
    return tile2(cos_h), tile2(sa), tile2(sb)


def _ret_kernel(q_ref, k_ref, v_ref, g_ref, lgt_ref, gn_ref, sf0_ref, sb0_ref,
                y_ref, sfo_ref, sbo_ref, sfs_ref, sbs_ref, *, seq):
    c = RET_CHUNK
    n_chunks = seq // c
    lg = jax.nn.log_sigmoid(lgt_ref[0])
    lgf, lgb = lg[0:1, :], lg[1:2, :]
    pos = lax.broadcasted_iota(jnp.int32, (c, 1), 0).astype(F32)
    dkf = jnp.exp(lgf * (c - 1.0 - pos))
    dkb = jnp.exp(lgb * pos)
    dqf = jnp.exp(lgf * (pos + 1.0))
    dqb = jnp.exp(lgb * (c - pos))
    cdf = jnp.exp(lgf * float(c))
    cdb = jnp.exp(lgb * float(c))
    lane = lax.broadcasted_iota(jnp.int32, (1, LANES), 1)
    first = lane < HEAD_DIM
    ri = lax.broadcasted_iota(jnp.int32, (LANES, LANES), 0)
    ci = lax.broadcasted_iota(jnp.int32, (LANES, LANES), 1)
    same = (ri < HEAD_DIM) == (ci < HEAD_DIM)
    bd = same.astype(F32)

    def head_mean(x):
        a = jnp.sum(jnp.where(first, x, 0.0), axis=-1, keepdims=True)
        b = jnp.sum(jnp.where(first, 0.0, x), axis=-1, keepdims=True)
        return jnp.where(first, a, b) * (1.0 / HEAD_DIM)

    ii = lax.broadcasted_iota(jnp.int32, (c, c), 0)
    jj = lax.broadcasted_iota(jnp.int32, (c, c), 1)
    dif = (ii - jj).astype(F32)

    def decay_mask(h0):
        lf = lgf[:, h0:h0 + 1]
        lb = lgb[:, h0:h0 + 1]
        return jnp.where(dif > 0, jnp.exp(lf * jnp.maximum(dif, 0.0)),
                         jnp.where(dif < 0, jnp.exp(lb * jnp.maximum(-dif, 0.0)), 2.0))

    dm = (decay_mask(0), decay_mask(HEAD_DIM))
    mfirst = first.astype(BF16)
    msecond = (1.0 - first.astype(F32)).astype(BF16)

    unroll = min(8, n_chunks)

    def incr(n, carry):
        sl = pl.ds(pl.multiple_of(n * c, c), c)
        kf = k_ref[0, sl, :].astype(F32)
        v = v_ref[0, sl, :]
        sfs_ref[n] = _dot_tn((kf * dkf).astype(BF16), v) * bd
        sbs_ref[n] = _dot_tn((kf * dkb).astype(BF16), v) * bd
        return carry

    lax.fori_loop(0, n_chunks, incr, 0, unroll=unroll)

    def scan_f(n, s):
        u = sfs_ref[n]
        sfs_ref[n] = s
        return cdf * s + u

    def scan_b(i, s):
        n = n_chunks - 1 - i
        u = sbs_ref[n]
        sbs_ref[n] = s
        return cdb * s + u

    sfo_ref[0, 0] = lax.fori_loop(0, n_chunks, scan_f, sf0_ref[0, 0])
    sbo_ref[0, 0] = lax.fori_loop(0, n_chunks, scan_b, sb0_ref[0, 0])

    def outp(n, carry):
        sl = pl.ds(pl.multiple_of(n * c, c), c)
        q = q_ref[0, sl, :]
        k = k_ref[0, sl, :]
        v = v_ref[0, sl, :]
        qf = q.astype(F32)
        o = None
        for hh, mk in enumerate((mfirst, msecond)):
            s = _dot_nt(q * mk, k) * dm[hh]
            oh = _dot(s.astype(BF16), v)
            o = oh if o is None else jnp.where(first, o, oh)
        qcat = jnp.concatenate([(qf * dqf).astype(BF16), (qf * dqb).astype(BF16)], axis=1)
        scat = jnp.concatenate([sfs_ref[n], sbs_ref[n]], axis=0).astype(BF16)
        o = o + _dot(qcat, scat)
        mu = head_mean(o)
        dlt = o - mu
        var = head_mean(dlt * dlt)
        on = dlt * lax.rsqrt(var + NORM_EPS) * gn_ref[0]
        y_ref[0, sl, :] = (_silu(g_ref[0, sl, :].astype(F32)) * on).astype(BF16)
        return carry

    lax.fori_loop(0, n_chunks, outp, 0, unroll=unroll)


def _retention(proj, lgt, gn, sf0, sb0):
    b, seq, _ = proj.shape
    col = lambda off: pl.BlockSpec((1, seq, LANES), lambda bi, p: (bi, 0, off + p))
    st = pl.BlockSpec((1, 1, LANES, LANES), lambda bi, p: (bi, p, 0, 0))
    return pl.pallas_call(
        functools.partial(_ret_kernel, seq=seq),
        grid=(b, N_PAIRS),
        in_specs=[col(CB_RQ), col(CB_RK), col(CB_RV), col(CB_RG),
                  pl.BlockSpec((1, 2, LANES), lambda bi, p: (p, 0, 0)),
                  pl.BlockSpec((1, 1, LANES), lambda bi, p: (p, 0, 0)),
                  st, st],
        out_specs=[pl.BlockSpec((1, seq, LANES), lambda bi, p: (bi, 0, p)), st, st],
        out_shape=[jax.ShapeDtypeStruct((b, seq, D_RET), BF16),
                   jax.ShapeDtypeStruct((b, N_PAIRS, LANES, LANES), F32),
                   jax.ShapeDtypeStruct((b, N_PAIRS, LANES, LANES), F32)],
        scratch_shapes=[pltpu.VMEM((seq // RET_CHUNK, LANES, LANES), F32),
                        pltpu.VMEM((seq // RET_CHUNK, LANES, LANES), F32)],
        compiler_params=_cp(("parallel", "parallel")),
        name="retention_s%d" % seq,
    )(proj, proj, proj, proj, lgt, gn, sf0, sb0)


def _conv_kernel(b_ref, c_ref, x_ref, w_ref, y_ref, *, seq):
    u = c_ref[0].astype(F32) * x_ref[0].astype(F32)
    row = lax.broadcasted_iota(jnp.int32, (seq, 1), 0)
    prev = jnp.where(row == 0, 0.0, pltpu.roll(u, 1, 0))
    nxt = jnp.where(row == seq - 1, 0.0, pltpu.roll(u, seq - 1, 0))
    y = w_ref[0:1, :] * prev + w_ref[1:2, :] * u + w_ref[2:3, :] * nxt
    y_ref[0] = (b_ref[0].astype(F32) * y).astype(BF16)


def _short_conv(proj, conv_w):
    b, seq, _ = proj.shape
    nblk = D_CONV // LANES
    col = lambda off: pl.BlockSpec((1, seq, LANES), lambda bi, j: (bi, 0, off + j))
    return pl.pallas_call(
        functools.partial(_conv_kernel, seq=seq),
        grid=(b, nblk),
        in_specs=[col(CB_CB), col(CB_CC), col(CB_CX),
                  pl.BlockSpec((3, LANES), lambda bi, j: (0, j))],
        out_specs=pl.BlockSpec((1, seq, LANES), lambda bi, j: (bi, 0, j)),
        out_shape=jax.ShapeDtypeStruct((b, seq, D_CONV), BF16),
        compiler_params=_cp(("parallel", "parallel")),
        name="short_conv_s%d" % seq,
    )(proj, proj, proj, conv_w)


def _na_bias_plan(rows):
    plan = np.full((3, NA_QROWS, NA_KROWS), -1, np.int64)
    for cl, rb in enumerate((0, NA_QROWS, rows - NA_QROWS)):
        ws = int(np.clip(rb - NA_ROWS // 2, 0, rows - NA_KROWS))
        for i in range(NA_QROWS):
            r = rb + i
            r0 = int(np.clip(r - NA_ROWS // 2, 0, rows - NA_ROWS))
            for j in range(NA_KROWS):
                kr = ws + j
                if r0 <= kr < r0 + NA_ROWS:
                    plan[cl, i, j] = kr - r + (NA_ROWS - 1)
    return plan


def _na_bias_kernel(rpb_ref, o_ref, *, plan):
    w = GRID_W
    ndr, ndc = 2 * NA_ROWS - 1, 2 * NA_COLS - 1
    base = pl.program_id(0) * (ndr * ndc)
    c = lax.broadcasted_iota(jnp.int32, (w, w), 0)
    kc = lax.broadcasted_iota(jnp.int32, (w, w), 1)
    dcidx = jnp.clip(kc - c, -(NA_COLS - 1), NA_COLS - 1) + (NA_COLS - 1)
    c0 = jnp.clip(c - NA_COLS // 2, 0, w - NA_COLS)
    col_in = jnp.logical_and(kc >= c0, kc < c0 + NA_COLS)
    neg = jnp.full((w, w), NEG_BIG, F32)
    tiles = []
    for dr in range(ndr):
        t = neg
        for dcv in range(ndc):
            t = jnp.where(dcidx == dcv, rpb_ref[base + dr * ndc + dcv], t)
        tiles.append(jnp.where(col_in, t * LOG2E, NEG_BIG))
    for cl in range(plan.shape[0]):
        for i in range(plan.shape[1]):
            for j in range(0, plan.shape[2], 2):
                pair = [tiles[int(d)] if d >= 0 else neg for d in plan[cl, i, j:j + 2]]
                o_ref[0, cl, i * w:(i + 1) * w, j * w:(j + 2) * w] = jnp.concatenate(pair, axis=1)


def _na_bias(rpb, rows):
    h = rpb.shape[0]
    tq, tk = NA_QROWS * GRID_W, NA_KROWS * GRID_W
    return pl.pallas_call(
        functools.partial(_na_bias_kernel, plan=_na_bias_plan(rows)),
        grid=(h,),
        in_specs=[pl.BlockSpec(memory_space=pltpu.SMEM)],
        out_specs=pl.BlockSpec((1, 3, tq, tk), lambda hh: (hh, 0, 0, 0)),
        out_shape=jax.ShapeDtypeStruct((h, 3, tq, tk), F32),
        compiler_params=_cp(("parallel",)),
        name="na_bias_expand",
    )(rpb.reshape(-1))


def _na_kernel(q_ref, k_ref, v_ref, kc_ref, vc_ref, bias_ref, c_ref, o_ref, cb_ref, v1_ref, vc1_ref, *, rows):
    cb_ref[...] = c_ref[...].astype(BF16)
    step = pl.program_id(1)
    nqb = rows // NA_QROWS
    tq = NA_QROWS * GRID_W
    lane = lax.broadcasted_iota(jnp.int32, (1, LANES), 1)
    first = lane < HEAD_DIM

    @pl.when(step == 0)
    def _():
        one = jnp.ones((1, LANES), BF16)
        for p in range(N_PAIRS):
            cols = slice(p * LANES, (p + 1) * LANES)
            v1_ref[2 * p] = jnp.where(first, v_ref[0, :, cols], one)
            v1_ref[2 * p + 1] = jnp.where(first, one, v_ref[0, :, cols])
            vc1_ref[2 * p] = jnp.where(first, vc_ref[0, :, cols], one)
            vc1_ref[2 * p + 1] = jnp.where(first, one, vc_ref[0, :, cols])

    for sb in range(NA_STEP_BLOCKS):
        qb = step * NA_STEP_BLOCKS + sb
        cls = jnp.where(qb == 0, 0, jnp.where(qb == nqb - 1, 2, 1))
        ws = jnp.clip(qb * NA_QROWS - NA_ROWS // 2, 0, rows - NA_KROWS)
        sl = pl.ds(pl.multiple_of(ws * GRID_W, GRID_W), NA_KROWS * GRID_W)
        qrows = slice(sb * tq, (sb + 1) * tq)
        for p in range(N_PAIRS):
            cols = slice(p * LANES, (p + 1) * LANES)
            q = q_ref[0, qrows, cols]
            kw = k_ref[0, sl, cols]
            kc = kc_ref[0, :, cols]
            pv = []
            for hh in range(2):
                mk = (first if hh == 0 else jnp.logical_not(first)).astype(BF16)
                qh = q * mk
                s_loc = _dot_nt(qh, kw) + bias_ref[2 * p + hh, cls]
                s_ctx = _dot_nt(qh, kc)
                m = jnp.maximum(jnp.max(s_loc, axis=-1, keepdims=True), jnp.max(s_ctx, axis=-1, keepdims=True))
                p_loc = jnp.exp2(s_loc - m).astype(BF16)
                p_ctx = jnp.exp2(s_ctx - m).astype(BF16)
                pv.append(_dot(p_loc, v1_ref[2 * p + hh, sl, :]) + _dot(p_ctx, vc1_ref[2 * p + hh]))
            num = jnp.where(first, pv[0], pv[1])
            den = jnp.where(first, pltpu.roll(pv[0], HEAD_DIM, 1), pltpu.roll(pv[1], HEAD_DIM, 1))
            o_ref[0, qrows, cols] = (num / den).astype(BF16)


def _na(proj, proj_ctx, bias, cast):
    b, seq, _ = proj.shape
    ctx_len = proj_ctx.shape[1]
    rows = seq // GRID_W
    nqb = rows // NA_QROWS
    tq = NA_STEP_BLOCKS * NA_QROWS * GRID_W
    tk = NA_KROWS * GRID_W
    nh = 2 * N_PAIRS
    n_steps = nqb // NA_STEP_BLOCKS
    assert n_steps * NA_STEP_BLOCKS == nqb
    cq, ck, cv = (CB_NQ * LANES // D_NA, CB_NK * LANES // D_NA, CB_NV * LANES // D_NA)
    in_specs = [pl.BlockSpec((1, tq, D_NA), lambda bi, st: (bi, st, cq)),
                pl.BlockSpec((1, seq, D_NA), lambda bi, st: (bi, 0, ck)),
                pl.BlockSpec((1, seq, D_NA), lambda bi, st: (bi, 0, cv)),
                pl.BlockSpec((1, ctx_len, D_NA), lambda bi, st: (bi, 0, ck)),
                pl.BlockSpec((1, ctx_len, D_NA), lambda bi, st: (bi, 0, cv)),
                pl.BlockSpec(bias.shape, lambda bi, st: (0, 0, 0, 0), pipeline_mode=pl.Buffered(1))]
    out_specs = [pl.BlockSpec((1, tq, D_NA), lambda bi, st: (bi, st, 0))]
    out_shape = [jax.ShapeDtypeStruct((b, seq, D_NA), BF16)]
    cr, cc = cast.shape
    steps = b * n_steps
    assert cr % (steps * BF16_ROWS) == 0
    blk = pl.BlockSpec((cr // steps, cc), lambda bi, st: (bi * n_steps + st, 0))
    in_specs.append(blk)
    out_specs.append(blk)
    out_shape.append(jax.ShapeDtypeStruct(cast.shape, BF16))
    return pl.pallas_call(
        functools.partial(_na_kernel, rows=rows),
        grid=(b, n_steps),
        in_specs=in_specs,
        out_specs=out_specs,
        out_shape=out_shape,
        scratch_shapes=[pltpu.VMEM((nh, seq, LANES), BF16), pltpu.VMEM((nh, ctx_len, LANES), BF16)],
        compiler_params=_cp(("parallel", "arbitrary")),
        name="na_attn",
    )(proj, proj, proj, proj_ctx, proj_ctx, bias, cast)


def _ctx_attn_kernel(q_ref, k_ref, v_ref, o_ref):
    q = q_ref[0]
    k = k_ref[0]
    v = v_ref[0]
    lane = lax.broadcasted_iota(jnp.int32, (1, LANES), 1)
    first = lane < HEAD_DIM
    o = None
    for hh in range(2):
        mk = (first if hh == 0 else jnp.logical_not(first)).astype(BF16)
        s = _dot_nt(q * mk, k)
        m = jnp.max(s, axis=-1, keepdims=True)
        p = jnp.exp(s - m)
        l = jnp.sum(p, axis=-1, keepdims=True)
        oh = _dot(p.astype(BF16), v) / l
        o = oh if o is None else jnp.where(first, o, oh)
    o_ref[0] = o.astype(BF16)


def _ctx_attn(proj_ctx):
    b, ctx_len, _ = proj_ctx.shape
    col = lambda off: pl.BlockSpec((1, ctx_len, LANES), lambda bi, p: (bi, 0, off + p))
    return pl.pallas_call(
        _ctx_attn_kernel,
        grid=(b, N_PAIRS),
        in_specs=[col(CB_NQ), col(CB_NK), col(CB_NV)],
        out_specs=pl.BlockSpec((1, ctx_len, LANES), lambda bi, p: (bi, 0, p)),
        out_shape=jax.ShapeDtypeStruct((b, ctx_len, D_NA), BF16),
        compiler_params=_cp(("parallel", "parallel")),
        name="ctx_attn",
    )(proj_ctx, proj_ctx, proj_ctx)


def _outproj_route_kernel(yr_ref, yc_ref, yn_ref, h_ref, w_ref, g1_ref, n2_ref, sh_ref, sc_ref, rw_ref, rb_ref,
                          ho_ref, xn_ref, idx_ref, wt_ref):
    tm = h_ref.shape[0]
    sub = min(OUT_PROJ_SUB, tm)
    r_hi, r_lo = _split_bf16(rw_ref[...])
    r_both = jnp.concatenate([r_hi, r_lo], axis=0)
    for r in range(tm // sub):
        rs = slice(r * sub, (r + 1) * sub)
        ycat = jnp.concatenate([yr_ref[rs, :], yc_ref[rs, :], yn_ref[rs, :]], axis=-1)
        h = h_ref[rs, :] + g1_ref[0] * _dot(ycat, w_ref[...])
        ho_ref[rs, :] = h
        a = _norm_mod(h, n2_ref[...], sh_ref[0], sc_ref[0])
        xn_ref[rs, :] = a.astype(xn_ref.dtype)
        a_hi, a_lo = _split_bf16(a)
        t_hi = _dot_nt(r_both, a_hi)
        logits = t_hi[:N_EXPERTS] + t_hi[N_EXPERTS:] + _dot_nt(r_hi, a_lo) + rb_ref[...]
        eidx = lax.broadcasted_iota(jnp.int32, logits.shape, 0)
        m1 = jnp.max(logits, axis=0, keepdims=True)
        i1 = jnp.min(jnp.where(logits == m1, eidx, N_EXPERTS), axis=0, keepdims=True)
        rest = jnp.where(eidx == i1, -jnp.inf, logits)
        m2 = jnp.max(rest, axis=0, keepdims=True)
        i2 = jnp.min(jnp.where(rest == m2, eidx, N_EXPERTS), axis=0, keepdims=True)
        e2 = jnp.exp(m2 - m1)
        den = 1.0 + e2
        idx_ref[:, rs] = jnp.concatenate([i1, i2], axis=0)
        wt_ref[:, rs] = jnp.concatenate([1.0 / den, e2 / den], axis=0)


def _out_proj_route(yr, yc, yn, h2, w_bf, g1, n2g, sh2, sc2, seq, tm, router_w, router_b):
    t, d = h2.shape
    tps = seq // tm
    row = lambda wdt: pl.BlockSpec((tm, wdt), lambda i: (i, 0))
    mod = pl.BlockSpec((1, 1, d), lambda i: (i // tps, 0, 0))
    return pl.pallas_call(
        _outproj_route_kernel,
        grid=(t // tm,),
        in_specs=[row(D_RET), row(D_CONV), row(D_NA), row(d),
                  pl.BlockSpec((d, d), lambda i: (0, 0)), mod,
                  pl.BlockSpec((1, d), lambda i: (0, 0)), mod, mod,
                  pl.BlockSpec((N_EXPERTS, d), lambda i: (0, 0)),
                  pl.BlockSpec((N_EXPERTS, 1), lambda i: (0, 0))],
        out_specs=[row(d), row(d), pl.BlockSpec((2, tm), lambda i: (0, i)), pl.BlockSpec((2, tm), lambda i: (0, i))],
        out_shape=[jax.ShapeDtypeStruct((t, d), F32), jax.ShapeDtypeStruct((t, d), BF16),
                   jax.ShapeDtypeStruct((2, t), jnp.int32), jax.ShapeDtypeStruct((2, t), F32)],
        compiler_params=_cp(("parallel",)),
        name="out_proj_route",
    )(yr, yc, yn, h2, w_bf, g1, n2g.reshape(1, d), sh2, sc2, router_w.T, router_b.reshape(N_EXPERTS, 1))


def _swiglu_chunks(x, w_in_ref, w_out_ref, ff, n_chunks):
    tf = -(-ff // (n_chunks * MXU_COLS)) * MXU_COLS
    y = None
    for c0 in range(0, ff, tf):
        c1 = min(c0 + tf, ff)
        gate = _dot(x, w_in_ref[:, c0:c1])
        up = _dot(x, w_in_ref[:, ff + c0:ff + c1])
        part = _dot((_silu(gate) * up).astype(BF16), w_out_ref[c0:c1, :])
        y = part if y is None else y + part
    return y


def _outproj_ffn_kernel(yr_ref, yc_ref, yn_ref, h_ref, wo_ref, g1_ref, n2_ref, sh_ref, sc_ref,
                        wi_ref, wf_ref, g2_ref, o_ref, *, ff, n_chunks):
    ycat = jnp.concatenate([yr_ref[...], yc_ref[...], yn_ref[...]], axis=-1)
    h1 = h_ref[...] + g1_ref[0] * _dot(ycat, wo_ref[...])
    a = _norm_mod(h1, n2_ref[...], sh_ref[0], sc_ref[0]).astype(BF16)
    o_ref[...] = h1 + g2_ref[0] * _swiglu_chunks(a, wi_ref, wf_ref, ff, n_chunks)


def _outproj_ffn(yr, yc, yn, h2, wo_bf, g1, n2g, sh2, sc2, fw_in_bf, fw_out_bf, g2, seq, tm, n_chunks):
    t, d = h2.shape
    ff = fw_out_bf.shape[0]
    tps = seq // tm
    resident = pl.Buffered(1)
    row = lambda wdt: pl.BlockSpec((tm, wdt), lambda i: (i, 0))
    mod = pl.BlockSpec((1, 1, d), lambda i: (i // tps, 0, 0))
    return pl.pallas_call(
        functools.partial(_outproj_ffn_kernel, ff=ff, n_chunks=n_chunks),
        grid=(t // tm,),
        in_specs=[row(D_RET), row(D_CONV), row(D_NA), row(d),
                  pl.BlockSpec((d, d), lambda i: (0, 0), pipeline_mode=resident), mod,
                  pl.BlockSpec((1, d), lambda i: (0, 0)), mod, mod,
                  pl.BlockSpec((d, 2 * ff), lambda i: (0, 0), pipeline_mode=resident),
                  pl.BlockSpec((ff, d), lambda i: (0, 0), pipeline_mode=resident), mod],
        out_specs=row(d),
        out_shape=jax.ShapeDtypeStruct((t, d), F32),
        compiler_params=_cp(("parallel",)),
        name="out_proj_ffn_t%d" % t,
    )(yr, yc, yn, h2, wo_bf, g1, n2g.reshape(1, d), sh2, sc2, fw_in_bf, fw_out_bf, g2)


def _dispatch_kernel(s0_ref, s1_ref, zs_ref, zv_ref, x_ref, xs_ref, slab_ref, zbuf_ref, sem, *, rows):
    @pl.when(pl.program_id(0) == 0)
    def _():
        zbuf_ref[...] = jnp.zeros_like(zbuf_ref)
        for e in range(2 * N_EXPERTS):
            @pl.when(zv_ref[e] != 0)
            def _():
                dst = xs_ref.at[pl.ds(pl.multiple_of(zs_ref[e], MOE_TS), MOE_TS)]
                pltpu.make_async_copy(zbuf_ref, dst, sem.at[1]).start()
        for e in range(2 * N_EXPERTS):
            @pl.when(zv_ref[e] != 0)
            def _():
                pltpu.make_async_copy(zbuf_ref, xs_ref.at[pl.ds(0, MOE_TS)], sem.at[1]).wait()

    nk = slab_ref.shape[1]

    def issue(g, carry):
        rs = pl.ds(pl.multiple_of(g * DMA_UNROLL, DMA_UNROLL), DMA_UNROLL)
        for k in range(nk):
            slab_ref[rs, k, :] = x_ref[rs, k * LANES:(k + 1) * LANES].astype(slab_ref.dtype)
        for u in range(DMA_UNROLL):
            r = g * DMA_UNROLL + u
            src = slab_ref.at[r]
            pltpu.make_async_copy(src, xs_ref.at[s0_ref[r]], sem.at[0]).start(priority=u % 2)
            pltpu.make_async_copy(src, xs_ref.at[s1_ref[r]], sem.at[0]).start(priority=(u + 1) % 2)
        return carry

    lax.fori_loop(0, rows // DMA_UNROLL, issue, 0)
    for _ in range(2):
        pltpu.make_async_copy(slab_ref, xs_ref.at[pl.ds(0, rows)], sem.at[0]).wait()


def _dispatch(xn, slot0, slot1, zplan, n_slots, rows):
    t, d = xn.shape
    slab = (d // LANES, LANES)
    zstart, zvalid = zplan
    return pl.pallas_call(
        functools.partial(_dispatch_kernel, rows=rows),
        grid=(t // rows,),
        in_specs=[pl.BlockSpec((rows,), lambda i: (i,), memory_space=pltpu.SMEM),
                  pl.BlockSpec((rows,), lambda i: (i,), memory_space=pltpu.SMEM),
                  pl.BlockSpec((2 * N_EXPERTS,), lambda i: (0,), memory_space=pltpu.SMEM),
                  pl.BlockSpec((2 * N_EXPERTS,), lambda i: (0,), memory_space=pltpu.SMEM),
                  pl.BlockSpec((rows, d), lambda i: (i, 0))],
        out_specs=pl.BlockSpec(memory_space=pl.ANY),
        out_shape=jax.ShapeDtypeStruct((n_slots,) + slab, F32),
        scratch_shapes=[pltpu.VMEM((rows,) + slab, F32), pltpu.VMEM((MOE_TS,) + slab, F32),
                        pltpu.SemaphoreType.DMA((2,))],
        compiler_params=_cp(("arbitrary",)),
        name="moe_dispatch",
    )(slot0, slot1, zstart, zvalid, xn)


def _expert_kernel(be_ref, nu_ref, xs_ref, wi_ref, wo_ref, o_ref, xa_ref, xb_ref, sem, *, ff, n_chunks):
    j = pl.program_id(0)
    n = pl.num_programs(0)
    ts, d = xa_ref.shape
    bufs = (xa_ref, xb_ref)

    def copies(blk, b):
        rows = pl.ds(pl.multiple_of(blk * ts, ts), ts)
        return [pltpu.make_async_copy(xs_ref.at[rows, k, :], bufs[b].at[:, k * LANES:(k + 1) * LANES],
                                      sem.at[b]) for k in range(d // LANES)]

    @pl.when(j == 0)
    def _():
        for c in copies(0, 0):
            c.start()

    def step(b):
        @pl.when(j + 1 < n)
        def _():
            for c in copies(j + 1, 1 - b):
                c.start()

        for c in copies(j, b):
            c.wait()

        @pl.when(j < nu_ref[0])
        def _():
            o_ref[...] = _swiglu_chunks(bufs[b][...].astype(BF16), wi_ref.at[0], wo_ref.at[0], ff, n_chunks)

        @pl.when(j >= nu_ref[0])
        def _():
            o_ref[...] = jnp.zeros_like(o_ref)

    for b in range(2):
        pl.when(j % 2 == b)(functools.partial(step, b))


def _experts(xs, block_expert, n_used, w_in_bf, w_out_bf, n_chunks):
    n, nk, _ = xs.shape
    d = nk * LANES
    ff = w_out_bf.shape[1]
    ts = MOE_TS
    grid_spec = pltpu.PrefetchScalarGridSpec(
        num_scalar_prefetch=2,
        grid=(n // ts,),
        in_specs=[pl.BlockSpec(memory_space=pl.ANY),
                  pl.BlockSpec((1, d, 2 * ff), lambda j, be, nu: (be[j], 0, 0)),
                  pl.BlockSpec((1, ff, d), lambda j, be, nu: (be[j], 0, 0))],
        out_specs=pl.BlockSpec((ts, d), lambda j, be, nu: (j, 0)),
        scratch_shapes=[pltpu.VMEM((ts, d), xs.dtype), pltpu.VMEM((ts, d), xs.dtype),
                        pltpu.SemaphoreType.DMA((2,))],
    )
    return pl.pallas_call(
        functools.partial(_expert_kernel, ff=ff, n_chunks=n_chunks),
        grid_spec=grid_spec,
        out_shape=jax.ShapeDtypeStruct((n, d), F32),
        compiler_params=_cp(("arbitrary",)),
        name="moe_experts",
    )(block_expert, n_used, xs, w_in_bf, w_out_bf)


def _combine_kernel(p0_ref, p1_ref, q0_ref, q1_ref, ys_ref, h_ref, w0_ref, w1_ref, g2_ref, fg_ref, o_ref,
                    ya0_ref, ya1_ref, yb0_ref, yb1_ref, sem, *, rows):
    i = pl.program_id(0)
    n = pl.num_programs(0)
    n_iter = rows // COMBINE_GROUP
    bufs = ((ya0_ref, ya1_ref), (yb0_ref, yb1_ref))

    def issue_rows(pa_ref, pb_ref, b, g):
        for u in range(COMBINE_GROUP):
            r = g * COMBINE_GROUP + u
            pltpu.make_async_copy(ys_ref.at[pl.ds(pa_ref[r], 1)], bufs[b][0].at[pl.ds(r, 1)],
                                  sem.at[0, b]).start(priority=u % 2)
            pltpu.make_async_copy(ys_ref.at[pl.ds(pb_ref[r], 1)], bufs[b][1].at[pl.ds(r, 1)],
                                  sem.at[1, b]).start(priority=(u + 1) % 2)

    def finish_rows(b, g):
        rs = pl.ds(pl.multiple_of(g * COMBINE_GROUP, COMBINE_GROUP), COMBINE_GROUP)
        y = w0_ref[rs, :] * bufs[b][0][rs, :] + w1_ref[rs, :] * bufs[b][1][rs, :]
        h = h_ref[rs, :] + g2_ref[0] * y
        ms = jnp.mean(h * h, axis=-1, keepdims=True)
        o_ref[rs, :] = h * lax.rsqrt(ms + NORM_EPS) * fg_ref[...]

    @pl.when(i == 0)
    def _():
        def first(g, carry):
            issue_rows(p0_ref, p1_ref, 0, g)
            return carry
        lax.fori_loop(0, n_iter, first, 0)

    def step(b):
        pltpu.make_async_copy(ys_ref.at[pl.ds(0, rows)], bufs[b][0], sem.at[0, b]).wait()
        pltpu.make_async_copy(ys_ref.at[pl.ds(0, rows)], bufs[b][1], sem.at[1, b]).wait()

        @pl.when(i + 1 < n)
        def _():
            def both(g, carry):
                issue_rows(q0_ref, q1_ref, 1 - b, g)
                finish_rows(b, g)
                return carry
            lax.fori_loop(0, n_iter, both, 0)

        @pl.when(i + 1 == n)
        def _():
            def last(g, carry):
                finish_rows(b, g)
                return carry
            lax.fori_loop(0, n_iter, last, 0)

    for b in range(2):
        pl.when(i % 2 == b)(functools.partial(step, b))


def _combine(pos0, pos1, ys, h2, w0, w1, g2, final_g, seq, rows):
    t, d = h2.shape
    tps = seq // rows
    steps = t // rows
    blk = pl.BlockSpec((rows, d), lambda i: (i, 0))
    col = pl.BlockSpec((rows, 1), lambda i: (i, 0))
    here = pl.BlockSpec((rows,), lambda i: (i,), memory_space=pltpu.SMEM)
    ahead = pl.BlockSpec((rows,), lambda i: (jnp.minimum(i + 1, steps - 1),), memory_space=pltpu.SMEM)
    return pl.pallas_call(
        functools.partial(_combine_kernel, rows=rows),
        grid=(steps,),
        in_specs=[here, here, ahead, ahead,
                  pl.BlockSpec(memory_space=pl.ANY),
                  blk, col, col,
                  pl.BlockSpec((1, 1, d), lambda i: (i // tps, 0, 0)),
                  pl.BlockSpec((1, d), lambda i: (0, 0))],
        out_specs=blk,
        out_shape=jax.ShapeDtypeStruct((t, d), F32),
        scratch_shapes=[pltpu.VMEM((rows, d), F32)] * 4 + [pltpu.SemaphoreType.DMA((2, 2))],
        compiler_params=_cp(("arbitrary",)),
        name="moe_combine_norm",
    )(pos0, pos1, pos0, pos1, ys, h2, w0, w1, g2, final_g.reshape(1, d))


def _route_plan(idx):
    t = idx.shape[1]
    ts = MOE_TS
    n_slots = 2 * t + N_EXPERTS * ts
    e_flat = idx.reshape(-1)
    onehot = (e_flat[:, None] == jnp.arange(N_EXPERTS, dtype=jnp.int32)[None, :]).astype(jnp.int32)
    csum = jnp.cumsum(onehot, axis=0)
    rank = jnp.sum((csum - onehot) * onehot, axis=1)
    counts = csum[-1]
    padded = ((counts + ts - 1) // ts) * ts
    ends = jnp.cumsum(padded)
    offs = ends - padded
    slot = (jnp.sum(onehot * offs[None, :], axis=1) + rank).astype(jnp.int32)
    starts = jnp.arange(n_slots // ts, dtype=jnp.int32) * ts
    block_expert = jnp.minimum(jnp.sum((starts[:, None] >= ends[None, :]).astype(jnp.int32), axis=1),
                               N_EXPERTS - 1).astype(jnp.int32)
    n_used = (ends[-1] // ts).astype(jnp.int32).reshape(1)
    tail = ends[-1] + jnp.arange(N_EXPERTS, dtype=jnp.int32) * ts
    zstart = jnp.concatenate([ends - ts, tail])
    zvalid = jnp.concatenate([padded > 0, tail < n_slots]).astype(jnp.int32)
    zstart = jnp.where(zvalid != 0, zstart, 0).astype(jnp.int32)
    return slot[:t], slot[t:], block_expert, n_used, (zstart, zvalid), n_slots


def kernel(x, c, ctx, c_ctx, ada_w, ada_b, norm1_g, norm2_g, w_in, w_out, ret_decay_logit, ret_gn_g,
           conv_w, na_rpb, ffn_w_in, ffn_w_out, moe_router_w, moe_router_b, moe_w_in, moe_w_out, final_g):
    b, seq, d = x.shape
    ctx_len = ctx.shape[1]
    depth = ada_w.shape[0]
    assert depth == 2, "the final norm is fused into the last (MoE) layer's combine step"
    rows = seq // GRID_W
    t_lat, t_ctx = b * seq, b * ctx_len
    tm_lat = min(1024, seq)
    tm_ctx = ctx_len

    c8 = jnp.zeros((8, d), F32).at[:b].set(c).at[b].set(c_ctx)
    mods = _ada(c8, ada_w, ada_b)
    tables = _rope_tables(seq)

    h = x.reshape(t_lat, d)
    hc = ctx.reshape(t_ctx, d)
    zero_state = jnp.zeros((b, N_PAIRS, LANES, LANES), F32)
    out = None
    for layer in range(depth):
        update_ctx = layer < depth - 1
        m = mods[layer].reshape(8, 6, d)
        lat = [m[:b, i].reshape(b, 1, d) for i in range(6)]
        cx = [jnp.broadcast_to(m[b, i].reshape(1, 1, d), (b, 1, d)) for i in range(6)]
        w_in_bf = w_in[layer].astype(BF16)
        w_out_bf = w_out[layer].astype(BF16)
        lgt = ret_decay_logit[layer].reshape(2, N_PAIRS, LANES // HEAD_DIM)
        lgt = jnp.repeat(lgt, HEAD_DIM, axis=2).transpose(1, 0, 2)
        gn = ret_gn_g[layer].reshape(N_PAIRS, 1, LANES)

        proj_c = _in_proj(hc, norm1_g[layer], cx[0], cx[1], w_in_bf, None, ctx_len, tm_ctx)
        proj_c = proj_c.reshape(b, ctx_len, D_IN_PROJ)
        if layer % 2 == 0:
            proj, fw_in, fw_out = _in_proj(h, norm1_g[layer], lat[0], lat[1], w_in_bf, tables, seq, tm_lat,
                                           casts=(ffn_w_in[layer // 2], ffn_w_out[layer // 2]))
        else:
            proj = _in_proj(h, norm1_g[layer], lat[0], lat[1], w_in_bf, tables, seq, tm_lat)
        proj = proj.reshape(b, seq, D_IN_PROJ)

        y_ret_c, s_fwd, s_bwd = _retention(proj_c, lgt, gn, zero_state, zero_state)
        y_ret, _, _ = _retention(proj, lgt, gn, s_fwd, s_bwd)
        y_conv = _short_conv(proj, conv_w[layer])
        mw = moe_w_in[(layer + 1) // 2] if layer % 2 == 0 else moe_w_out[layer // 2]
        y_na, mw_bf = _na(proj, proj_c, _na_bias(na_rpb[layer], rows), cast=mw.reshape(-1, mw.shape[-1]))
        if layer % 2 == 0:
            mw_in_bf = mw_bf.reshape(mw.shape)
        else:
            mw_out_bf = mw_bf.reshape(mw.shape)

        if layer % 2 == 0:
            h = _outproj_ffn(y_ret.reshape(t_lat, D_RET), y_conv.reshape(t_lat, D_CONV),
                             y_na.reshape(t_lat, D_NA), h, w_out_bf, lat[2], norm2_g[layer], lat[3], lat[4],
                             fw_in, fw_out, lat[5], seq, min(512, seq), FFN_CHUNKS)
            if update_ctx:
                y_conv_c = _short_conv(proj_c, conv_w[layer])
                y_na_c = _ctx_attn(proj_c)
                hc = _outproj_ffn(y_ret_c.reshape(t_ctx, D_RET), y_conv_c.reshape(t_ctx, D_CONV),
                                  y_na_c.reshape(t_ctx, D_NA), hc, w_out_bf, cx[2], norm2_g[layer], cx[3], cx[4],
                                  fw_in, fw_out, cx[5], ctx_len, tm_ctx, FFN_CHUNKS)
        else:
            e = layer // 2
            h, xn, idx, wts = _out_proj_route(y_ret.reshape(t_lat, D_RET), y_conv.reshape(t_lat, D_CONV),
                                              y_na.reshape(t_lat, D_NA), h, w_out_bf, lat[2], norm2_g[layer],
                                              lat[3], lat[4], seq, tm_lat, moe_router_w[e], moe_router_b[e])
            pos0, pos1, block_expert, n_used, zstart, n_slots = _route_plan(idx)
            xs = _dispatch(xn, pos0, pos1, zstart, n_slots, min(1024, seq))
            ys = _experts(xs, block_expert, n_used, mw_in_bf, mw_out_bf, FFN_CHUNKS)
            out = _combine(pos0, pos1, ys, h, wts[0].reshape(t_lat, 1), wts[1].reshape(t_lat, 1),
                           lat[5], final_g, seq, min(512, seq))
            h = out
    return out.reshape(b, seq, d)
```

```python
import functools

import numpy as np
import jax
import jax.numpy as jnp
from jax import lax
from jax.experimental import pallas as pl
from jax.experimental.pallas import tpu as pltpu

F32 = jnp.float32
BF16 = jnp.bfloat16

LANES = 128
BF16_ROWS = 16
HEAD_DIM = 64
GRID_W = 64
N_CONV_GROUPS = 4
N_RET_HEADS = 6
N_NA_HEADS = 6
D_RET = N_RET_HEADS * HEAD_DIM
D_CONV = N_CONV_GROUPS * HEAD_DIM
D_NA = N_NA_HEADS * HEAD_DIM
D_IN_PROJ = 4 * D_RET + 3 * D_CONV + 3 * D_NA
N_PAIRS = D_RET // LANES
NA_ROWS = 8
NA_COLS = 16
N_EXPERTS = 8
ROPE_BASE = 10000.0
NORM_EPS = 1e-6
NEG_BIG = -1e30
LOG2E = 1.4426950408889634

RET_CHUNK = 256
NA_QROWS = 4
NA_KROWS = 12
NA_STEP_BLOCKS = 2
MOE_TS = 512
DMA_UNROLL = 8
COMBINE_GROUP = 32
FFN_CHUNKS = 2
IN_PROJ_SUB = 512
OUT_PROJ_SUB = 512
MXU_COLS = 256
IN_PROJ_CHUNK = 3 * MXU_COLS
VMEM_LIMIT = 56 * 1024 * 1024

CB_RQ, CB_RK, CB_RV, CB_RG = 0, 3, 6, 9
CB_CB, CB_CC, CB_CX = 12, 14, 16
CB_NQ, CB_NK, CB_NV = 18, 21, 24


def _cp(sem, vmem=VMEM_LIMIT):
    return pltpu.CompilerParams(dimension_semantics=sem, vmem_limit_bytes=vmem)


def _silu(x):
    return x * (1.0 / (1.0 + jnp.exp(-x)))


def _dot(a, b):
    return jnp.dot(a, b, preferred_element_type=F32)


def _dot_nt(a, b):
    return lax.dot_general(a, b, (((1,), (1,)), ((), ())), preferred_element_type=F32)


def _dot_tn(a, b):
    return lax.dot_general(a, b, (((0,), (0,)), ((), ())), preferred_element_type=F32)


def _split_bf16(x):
    hi = x.astype(BF16)
    lo = (x - hi.astype(F32)).astype(BF16)
    return hi, lo


def _ada_kernel(c_ref, w_ref, b_ref, o_ref):
    x = _silu(c_ref[...]).astype(BF16)
    o_ref[0] = _dot(x, w_ref[0].astype(BF16)) + b_ref[0]


def _ada(c8, ada_w, ada_b):
    depth, d, n = ada_w.shape
    tn = n // 4
    return pl.pallas_call(
        _ada_kernel,
        grid=(depth, n // tn),
        in_specs=[pl.BlockSpec((8, d), lambda l, j: (0, 0)),
                  pl.BlockSpec((1, d, tn), lambda l, j: (l, 0, j)),
                  pl.BlockSpec((1, 1, tn), lambda l, j: (l, 0, j))],
        out_specs=pl.BlockSpec((1, 8, tn), lambda l, j: (l, 0, j)),
        out_shape=jax.ShapeDtypeStruct((depth, 8, n), F32),
        compiler_params=_cp(("parallel", "parallel")),
        name="ada_mod",
    )(c8, ada_w, ada_b.reshape(depth, 1, n))


def _norm_mod(x, g, sh, sc):
    ms = jnp.mean(x * x, axis=-1, keepdims=True)
    y = x * lax.rsqrt(ms + NORM_EPS) * g
    return y * (1.0 + sc) + sh


def _inproj_kernel(*refs, rope, n_cast):
    n_in = (8 if rope else 5) + n_cast
    for c_ref, cb_ref in zip(refs[n_in - n_cast:n_in], refs[n_in + 1:]):
        cb_ref[...] = c_ref[...].astype(BF16)
    if rope:
        h_ref, g_ref, sh_ref, sc_ref, w_ref, cos_ref, sa_ref, sb_ref = refs[:8]
    else:
        h_ref, g_ref, sh_ref, sc_ref, w_ref = refs[:5]
    o_ref = refs[n_in]
    tm = h_ref.shape[0]
    sub = min(IN_PROJ_SUB, tm)
    cw = IN_PROJ_CHUNK
    for r in range(tm // sub):
        rs = slice(r * sub, (r + 1) * sub)
        xn = _norm_mod(h_ref[rs, :], g_ref[...], sh_ref[0], sc_ref[0]).astype(BF16)
        for c0 in range(0, D_IN_PROJ, cw):
            c1 = min(c0 + cw, D_IN_PROJ)
            acc = _dot(xn, w_ref[:, c0:c1])
            for j in range((c1 - c0) // LANES):
                blk = acc[:, j * LANES:(j + 1) * LANES]
                cb = c0 // LANES + j
                if rope and cb < CB_RV:
                    blk = (blk * cos_ref[rs, :] + pltpu.roll(blk, 16, 1) * sa_ref[rs, :]
                           + pltpu.roll(blk, LANES - 16, 1) * sb_ref[rs, :])
                if CB_RK <= cb < CB_RV:
                    blk = blk * (HEAD_DIM ** -0.5)
                elif CB_NQ <= cb < CB_NK:
                    blk = blk * (HEAD_DIM ** -0.5 * (LOG2E if rope else 1.0))
                o_ref[rs, cb * LANES:(cb + 1) * LANES] = blk.astype(BF16)


def _in_proj(h2, g, sh, sc, w_bf, tables, seq, tm, casts=()):
    t, d = h2.shape
    tiles_per_seq = seq // tm
    steps = t // tm
    rope = tables is not None
    in_specs = [pl.BlockSpec((tm, d), lambda i: (i, 0)),
                pl.BlockSpec((1, d), lambda i: (0, 0)),
                pl.BlockSpec((1, 1, d), lambda i: (i // tiles_per_seq, 0, 0)),
                pl.BlockSpec((1, 1, d), lambda i: (i // tiles_per_seq, 0, 0)),
                pl.BlockSpec((d, D_IN_PROJ), lambda i: (0, 0))]
    args = [h2, g.reshape(1, d), sh, sc, w_bf]
    if rope:
        in_specs += [pl.BlockSpec((tm, LANES), lambda i: (i % tiles_per_seq, 0))] * 3
        args += list(tables)
    out_specs = [pl.BlockSpec((tm, D_IN_PROJ), lambda i: (i, 0))]
    out_shape = [jax.ShapeDtypeStruct((t, D_IN_PROJ), BF16)]
    for c in casts:
        cr, cc = c.shape
        assert cr % (steps * BF16_ROWS) == 0
        blk = pl.BlockSpec((cr // steps, cc), lambda i: (i, 0))
        in_specs.append(blk)
        out_specs.append(blk)
        out_shape.append(jax.ShapeDtypeStruct(c.shape, BF16))
        args.append(c)
    res = pl.pallas_call(
        functools.partial(_inproj_kernel, rope=rope, n_cast=len(casts)),
        grid=(steps,),
        in_specs=in_specs,
        out_specs=out_specs,
        out_shape=out_shape,
        compiler_params=_cp(("parallel",)),
        name="in_proj_rope" if rope else "in_proj_ctx",
    )(*args)
    return res if casts else res[0]


def _rope_tables(seq):
    t = np.arange(seq)
    row = (t // GRID_W).astype(np.float32)
    col = (t % GRID_W).astype(np.float32)
    n_freq = HEAD_DIM // 4
    inv_freq = (ROPE_BASE ** (-np.arange(n_freq, dtype=np.float32) / n_freq)).astype(np.float32)
    ang_r = row[:, None] * inv_freq
    ang_c = col[:, None] * inv_freq
    cos_h = np.concatenate([np.cos(ang_r), np.cos(ang_r), np.cos(ang_c), np.cos(ang_c)], axis=1)
    sin_h = np.concatenate([np.sin(ang_r), np.sin(ang_r), np.sin(ang_c), np.sin(ang_c)], axis=1)
    lane = np.arange(HEAD_DIM)
    second = (lane % 32) >= 16
    sa = np.where(second[None, :], sin_h, 0.0)
    sb = np.where(second[None, :], 0.0, -sin_h)
    tile2 = lambda a: jnp.asarray(np.concatenate([a, a], axis=1), F32)
    return tile2(cos_h), tile2(sa), tile2(sb)


def _ret_kernel(q_ref, k_ref, v_ref, g_ref, lgt_ref, gn_ref, sf0_ref, sb0_ref,
                y_ref, sfo_ref, sbo_ref, sfs_ref, sbs_ref, *, seq):
    c = RET_CHUNK
    n_chunks = seq // c
    lg = jax.nn.log_sigmoid(lgt_ref[0])
    lgf, lgb = lg[0:1, :], lg[1:2, :]
    pos = lax.broadcasted_iota(jnp.int32, (c, 1), 0).astype(F32)
    dkf = jnp.exp(lgf * (c - 1.0 - pos))
    dkb = jnp.exp(lgb * pos)
    dqf = jnp.exp(lgf * (pos + 1.0))
    dqb = jnp.exp(lgb * (c - pos))
    cdf = jnp.exp(lgf * float(c))
    cdb = jnp.exp(lgb * float(c))
    lane = lax.broadcasted_iota(jnp.int32, (1, LANES), 1)
    first = lane < HEAD_DIM
    ri = lax.broadcasted_iota(jnp.int32, (LANES, LANES), 0)
    ci = lax.broadcasted_iota(jnp.int32, (LANES, LANES), 1)
    same = (ri < HEAD_DIM) == (ci < HEAD_DIM)
    bd = same.astype(F32)

    def head_mean(x):
        a = jnp.sum(jnp.where(first, x, 0.0), axis=-1, keepdims=True)
        b = jnp.sum(jnp.where(first, 0.0, x), axis=-1, keepdims=True)
        return jnp.where(first, a, b) * (1.0 / HEAD_DIM)

    ii = lax.broadcasted_iota(jnp.int32, (c, c), 0)
    jj = lax.broadcasted_iota(jnp.int32, (c, c), 1)
    dif = (ii - jj).astype(F32)

    def decay_mask(h0):
        lf = lgf[:, h0:h0 + 1]
        lb = lgb[:, h0:h0 + 1]
        return jnp.where(dif > 0, jnp.exp(lf * jnp.maximum(dif, 0.0)),
                         jnp.where(dif < 0, jnp.exp(lb * jnp.maximum(-dif, 0.0)), 2.0))

    dm = (decay_mask(0), decay_mask(HEAD_DIM))
    mfirst = first.astype(BF16)
    msecond = (1.0 - first.astype(F32)).astype(BF16)

    unroll = min(8, n_chunks)

    def incr(n, carry):
        sl = pl.ds(pl.multiple_of(n * c, c), c)
        kf = k_ref[0, sl, :].astype(F32)
        v = v_ref[0, sl, :]
        sfs_ref[n] = _dot_tn((kf * dkf).astype(BF16), v) * bd
        sbs_ref[n] = _dot_tn((kf * dkb).astype(BF16), v) * bd
        return carry

    lax.fori_loop(0, n_chunks, incr, 0, unroll=unroll)

    def scan_f(n, s):
        u = sfs_ref[n]
        sfs_ref[n] = s
        return cdf * s + u

    def scan_b(i, s):
        n = n_chunks - 1 - i
        u = sbs_ref[n]
        sbs_ref[n] = s
        return cdb * s + u

    sfo_ref[0, 0] = lax.fori_loop(0, n_chunks, scan_f, sf0_ref[0, 0])
    sbo_ref[0, 0] = lax.fori_loop(0, n_chunks, scan_b, sb0_ref[0, 0])

    def outp(n, carry):
        sl = pl.ds(pl.multiple_of(n * c, c), c)
        q = q_ref[0, sl, :]
        k = k_ref[0, sl, :]
        v = v_ref[0, sl, :]
        qf = q.astype(F32)
        o = None
        for hh, mk in enumerate((mfirst, msecond)):
            s = _dot_nt(q * mk, k) * dm[hh]
            oh = _dot(s.astype(BF16), v)
            o = oh if o is None else jnp.where(first, o, oh)
        qcat = jnp.concatenate([(qf * dqf).astype(BF16), (qf * dqb).astype(BF16)], axis=1)
        scat = jnp.concatenate([sfs_ref[n], sbs_ref[n]], axis=0).astype(BF16)
        o = o + _dot(qcat, scat)
        mu = head_mean(o)
        dlt = o - mu
        var = head_mean(dlt * dlt)
        on = dlt * lax.rsqrt(var + NORM_EPS) * gn_ref[0]
        y_ref[0, sl, :] = (_silu(g_ref[0, sl, :].astype(F32)) * on).astype(BF16)
        return carry

    lax.fori_loop(0, n_chunks, outp, 0, unroll=unroll)


def _retention(proj, lgt, gn, sf0, sb0):
    b, seq, _ = proj.shape
    col = lambda off: pl.BlockSpec((1, seq, LANES), lambda bi, p: (bi, 0, off + p))
    st = pl.BlockSpec((1, 1, LANES, LANES), lambda bi, p: (bi, p, 0, 0))
    return pl.pallas_call(
        functools.partial(_ret_kernel, seq=seq),
        grid=(b, N_PAIRS),
        in_specs=[col(CB_RQ), col(CB_RK), col(CB_RV), col(CB_RG),
                  pl.BlockSpec((1, 2, LANES), lambda bi, p: (p, 0, 0)),
                  pl.BlockSpec((1, 1, LANES), lambda bi, p: (p, 0, 0)),
                  st, st],
        out_specs=[pl.BlockSpec((1, seq, LANES), lambda bi, p: (bi, 0, p)), st, st],
        out_shape=[jax.ShapeDtypeStruct((b, seq, D_RET), BF16),
                   jax.ShapeDtypeStruct((b, N_PAIRS, LANES, LANES), F32),
                   jax.ShapeDtypeStruct((b, N_PAIRS, LANES, LANES), F32)],
        scratch_shapes=[pltpu.VMEM((seq // RET_CHUNK, LANES, LANES), F32),
                        pltpu.VMEM((seq // RET_CHUNK, LANES, LANES), F32)],
        compiler_params=_cp(("parallel", "parallel")),
        name="retention_s%d" % seq,
    )(proj, proj, proj, proj, lgt, gn, sf0, sb0)


def _conv_kernel(b_ref, c_ref, x_ref, w_ref, y_ref, *, seq):
    u = c_ref[0].astype(F32) * x_ref[0].astype(F32)
    row = lax.broadcasted_iota(jnp.int32, (seq, 1), 0)
    prev = jnp.where(row == 0, 0.0, pltpu.roll(u, 1, 0))
    nxt = jnp.where(row == seq - 1, 0.0, pltpu.roll(u, seq - 1, 0))
    y = w_ref[0:1, :] * prev + w_ref[1:2, :] * u + w_ref[2:3, :] * nxt
    y_ref[0] = (b_ref[0].astype(F32) * y).astype(BF16)


def _short_conv(proj, conv_w):
    b, seq, _ = proj.shape
    nblk = D_CONV // LANES
    col = lambda off: pl.BlockSpec((1, seq, LANES), lambda bi, j: (bi, 0, off + j))
    return pl.pallas_call(
        functools.partial(_conv_kernel, seq=seq),
        grid=(b, nblk),
        in_specs=[col(CB_CB), col(CB_CC), col(CB_CX),
                  pl.BlockSpec((3, LANES), lambda bi, j: (0, j))],
        out_specs=pl.BlockSpec((1, seq, LANES), lambda bi, j: (bi, 0, j)),
        out_shape=jax.ShapeDtypeStruct((b, seq, D_CONV), BF16),
        compiler_params=_cp(("parallel", "parallel")),
        name="short_conv_s%d" % seq,
    )(proj, proj, proj, conv_w)


def _na_bias_plan(rows):
    plan = np.full((3, NA_QROWS, NA_KROWS), -1, np.int64)
    for cl, rb in enumerate((0, NA_QROWS, rows - NA_QROWS)):
        ws = int(np.clip(rb - NA_ROWS // 2, 0, rows - NA_KROWS))
        for i in range(NA_QROWS):
            r = rb + i
            r0 = int(np.clip(r - NA_ROWS // 2, 0, rows - NA_ROWS))
            for j in range(NA_KROWS):
                kr = ws + j
                if r0 <= kr < r0 + NA_ROWS:
                    plan[cl, i, j] = kr - r + (NA_ROWS - 1)
    return plan


def _na_bias_kernel(rpb_ref, o_ref, *, plan):
    w = GRID_W
    ndr, ndc = 2 * NA_ROWS - 1, 2 * NA_COLS - 1
    base = pl.program_id(0) * (ndr * ndc)
    c = lax.broadcasted_iota(jnp.int32, (w, w), 0)
    kc = lax.broadcasted_iota(jnp.int32, (w, w), 1)
    dcidx = jnp.clip(kc - c, -(NA_COLS - 1), NA_COLS - 1) + (NA_COLS - 1)
    c0 = jnp.clip(c - NA_COLS // 2, 0, w - NA_COLS)
    col_in = jnp.logical_and(kc >= c0, kc < c0 + NA_COLS)
    neg = jnp.full((w, w), NEG_BIG, F32)
    tiles = []
    for dr in range(ndr):
        t = neg
        for dcv in range(ndc):
            t = jnp.where(dcidx == dcv, rpb_ref[base + dr * ndc + dcv], t)
        tiles.append(jnp.where(col_in, t * LOG2E, NEG_BIG))
    for cl in range(plan.shape[0]):
        for i in range(plan.shape[1]):
            for j in range(0, plan.shape[2], 2):
                pair = [tiles[int(d)] if d >= 0 else neg for d in plan[cl, i, j:j + 2]]
                o_ref[0, cl, i * w:(i + 1) * w, j * w:(j + 2) * w] = jnp.concatenate(pair, axis=1)


def _na_bias(rpb, rows):
    h = rpb.shape[0]
    tq, tk = NA_QROWS * GRID_W, NA_KROWS * GRID_W
    return pl.pallas_call(
        functools.partial(_na_bias_kernel, plan=_na_bias_plan(rows)),
        grid=(h,),
        in_specs=[pl.BlockSpec(memory_space=pltpu.SMEM)],
        out_specs=pl.BlockSpec((1, 3, tq, tk), lambda hh: (hh, 0, 0, 0)),
        out_shape=jax.ShapeDtypeStruct((h, 3, tq, tk), F32),
        compiler_params=_cp(("parallel",)),
        name="na_bias_expand",
    )(rpb.reshape(-1))


def _na_kernel(q_ref, k_ref, v_ref, kc_ref, vc_ref, bias_ref, c_ref, o_ref, cb_ref, v1_ref, vc1_ref, *, rows):
    cb_ref[...] = c_ref[...].astype(BF16)
    step = pl.program_id(1)
    nqb = rows // NA_QROWS
    tq = NA_QROWS * GRID_W
    lane = lax.broadcasted_iota(jnp.int32, (1, LANES), 1)
    first = lane < HEAD_DIM

    @pl.when(step == 0)
    def _():
        one = jnp.ones((1, LANES), BF16)
        for p in range(N_PAIRS):
            cols = slice(p * LANES, (p + 1) * LANES)
            v1_ref[2 * p] = jnp.where(first, v_ref[0, :, cols], one)
            v1_ref[2 * p + 1] = jnp.where(first, one, v_ref[0, :, cols])
            vc1_ref[2 * p] = jnp.where(first, vc_ref[0, :, cols], one)
            vc1_ref[2 * p + 1] = jnp.where(first, one, vc_ref[0, :, cols])

    for sb in range(NA_STEP_BLOCKS):
        qb = step * NA_STEP_BLOCKS + sb
        cls = jnp.where(qb == 0, 0, jnp.where(qb == nqb - 1, 2, 1))
        ws = jnp.clip(qb * NA_QROWS - NA_ROWS // 2, 0, rows - NA_KROWS)
        sl = pl.ds(pl.multiple_of(ws * GRID_W, GRID_W), NA_KROWS * GRID_W)
        qrows = slice(sb * tq, (sb + 1) * tq)
        for p in range(N_PAIRS):
            cols = slice(p * LANES, (p + 1) * LANES)
            q = q_ref[0, qrows, cols]
            kw = k_ref[0, sl, cols]
            kc = kc_ref[0, :, cols]
            pv = []
            for hh in range(2):
                mk = (first if hh == 0 else jnp.logical_not(first)).astype(BF16)
                qh = q * mk
                s_loc = _dot_nt(qh, kw) + bias_ref[2 * p + hh, cls]
                s_ctx = _dot_nt(qh, kc)
                m = jnp.maximum(jnp.max(s_loc, axis=-1, keepdims=True), jnp.max(s_ctx, axis=-1, keepdims=True))
                p_loc = jnp.exp2(s_loc - m).astype(BF16)
                p_ctx = jnp.exp2(s_ctx - m).astype(BF16)
                pv.append(_dot(p_loc, v1_ref[2 * p + hh, sl, :]) + _dot(p_ctx, vc1_ref[2 * p + hh]))
            num = jnp.where(first, pv[0], pv[1])
            den = jnp.where(first, pltpu.roll(pv[0], HEAD_DIM, 1), pltpu.roll(pv[1], HEAD_DIM, 1))
            o_ref[0, qrows, cols] = (num / den).astype(BF16)


def _na(proj, proj_ctx, bias, cast):
    b, seq, _ = proj.shape
    ctx_len = proj_ctx.shape[1]
    rows = seq // GRID_W
    nqb = rows // NA_QROWS
    tq = NA_STEP_BLOCKS * NA_QROWS * GRID_W
    tk = NA_KROWS * GRID_W
    nh = 2 * N_PAIRS
    n_steps = nqb // NA_STEP_BLOCKS
    assert n_steps * NA_STEP_BLOCKS == nqb
    cq, ck, cv = (CB_NQ * LANES // D_NA, CB_NK * LANES // D_NA, CB_NV * LANES // D_NA)
    in_specs = [pl.BlockSpec((1, tq, D_NA), lambda bi, st: (bi, st, cq)),
                pl.BlockSpec((1, seq, D_NA), lambda bi, st: (bi, 0, ck)),
                pl.BlockSpec((1, seq, D_NA), lambda bi, st: (bi, 0, cv)),
                pl.BlockSpec((1, ctx_len, D_NA), lambda bi, st: (bi, 0, ck)),
                pl.BlockSpec((1, ctx_len, D_NA), lambda bi, st: (bi, 0, cv)),
                pl.BlockSpec(bias.shape, lambda bi, st: (0, 0, 0, 0), pipeline_mode=pl.Buffered(1))]
    out_specs = [pl.BlockSpec((1, tq, D_NA), lambda bi, st: (bi, st, 0))]
    out_shape = [jax.ShapeDtypeStruct((b, seq, D_NA), BF16)]
    cr, cc = cast.shape
    steps = b * n_steps
    assert cr % (steps * BF16_ROWS) == 0
    blk = pl.BlockSpec((cr // steps, cc), lambda bi, st: (bi * n_steps + st, 0))
    in_specs.append(blk)
    out_specs.append(blk)
    out_shape.append(jax.ShapeDtypeStruct(cast.shape, BF16))
    return pl.pallas_call(
        functools.partial(_na_kernel, rows=rows),
        grid=(b, n_steps),
        in_specs=in_specs,
        out_specs=out_specs,
        out_shape=out_shape,
        scratch_shapes=[pltpu.VMEM((nh, seq, LANES), BF16), pltpu.VMEM((nh, ctx_len, LANES), BF16)],
        compiler_params=_cp(("parallel", "arbitrary")),
        name="na_attn",
    )(proj, proj, proj, proj_ctx, proj_ctx, bias, cast)


def _ctx_attn_kernel(q_ref, k_ref, v_ref, o_ref):
    q = q_ref[0]
    k = k_ref[0]
    v = v_ref[0]
    lane = lax.broadcasted_iota(jnp.int32, (1, LANES), 1)
    first = lane < HEAD_DIM
    o = None
    for hh in range(2):
        mk = (first if hh == 0 else jnp.logical_not(first)).astype(BF16)
        s = _dot_nt(q * mk, k)
        m = jnp.max(s, axis=-1, keepdims=True)
        p = jnp.exp(s - m)
        l = jnp.sum(p, axis=-1, keepdims=True)
        oh = _dot(p.astype(BF16), v) / l
        o = oh if o is None else jnp.where(first, o, oh)
    o_ref[0] = o.astype(BF16)


def _ctx_attn(proj_ctx):
    b, ctx_len, _ = proj_ctx.shape
    col = lambda off: pl.BlockSpec((1, ctx_len, LANES), lambda bi, p: (bi, 0, off + p))
    return pl.pallas_call(
        _ctx_attn_kernel,
        grid=(b, N_PAIRS),
        in_specs=[col(CB_NQ), col(CB_NK), col(CB_NV)],
        out_specs=pl.BlockSpec((1, ctx_len, LANES), lambda bi, p: (bi, 0, p)),
        out_shape=jax.ShapeDtypeStruct((b, ctx_len, D_NA), BF16),
        compiler_params=_cp(("parallel", "parallel")),
        name="ctx_attn",
    )(proj_ctx, proj_ctx, proj_ctx)


def _outproj_route_kernel(yr_ref, yc_ref, yn_ref, h_ref, w_ref, g1_ref, n2_ref, sh_ref, sc_ref, rw_ref, rb_ref,
                          ho_ref, xn_ref, idx_ref, wt_ref):
    tm = h_ref.shape[0]
    sub = min(OUT_PROJ_SUB, tm)
    r_hi, r_lo = _split_bf16(rw_ref[...])
    r_both = jnp.concatenate([r_hi, r_lo], axis=0)
    for r in range(tm // sub):
        rs = slice(r * sub, (r + 1) * sub)
        ycat = jnp.concatenate([yr_ref[rs, :], yc_ref[rs, :], yn_ref[rs, :]], axis=-1)
        h = h_ref[rs, :] + g1_ref[0] * _dot(ycat, w_ref[...])
        ho_ref[rs, :] = h
        a = _norm_mod(h, n2_ref[...], sh_ref[0], sc_ref[0])
        xn_ref[rs, :] = a.astype(xn_ref.dtype)
        a_hi, a_lo = _split_bf16(a)
        t_hi = _dot_nt(r_both, a_hi)
        logits = t_hi[:N_EXPERTS] + t_hi[N_EXPERTS:] + _dot_nt(r_hi, a_lo) + rb_ref[...]
        eidx = lax.broadcasted_iota(jnp.int32, logits.shape, 0)
        m1 = jnp.max(logits, axis=0, keepdims=True)
        i1 = jnp.min(jnp.where(logits == m1, eidx, N_EXPERTS), axis=0, keepdims=True)
        rest = jnp.where(eidx == i1, -jnp.inf, logits)
        m2 = jnp.max(rest, axis=0, keepdims=True)
        i2 = jnp.min(jnp.where(rest == m2, eidx, N_EXPERTS), axis=0, keepdims=True)
        e2 = jnp.exp(m2 - m1)
        den = 1.0 + e2
        idx_ref[:, rs] = jnp.concatenate([i1, i2], axis=0)
        wt_ref[:, rs] = jnp.concatenate([1.0 / den, e2 / den], axis=0)


def _out_proj_route(yr, yc, yn, h2, w_bf, g1, n2g, sh2, sc2, seq, tm, router_w, router_b):
    t, d = h2.shape
    tps = seq // tm
    row = lambda wdt: pl.BlockSpec((tm, wdt), lambda i: (i, 0))
    mod = pl.BlockSpec((1, 1, d), lambda i: (i // tps, 0, 0))
    return pl.pallas_call(
        _outproj_route_kernel,
        grid=(t // tm,),
        in_specs=[row(D_RET), row(D_CONV), row(D_NA), row(d),
                  pl.BlockSpec((d, d), lambda i: (0, 0)), mod,
                  pl.BlockSpec((1, d), lambda i: (0, 0)), mod, mod,
                  pl.BlockSpec((N_EXPERTS, d), lambda i: (0, 0)),
                  pl.BlockSpec((N_EXPERTS, 1), lambda i: (0, 0))],
        out_specs=[row(d), row(d), pl.BlockSpec((2, tm), lambda i: (0, i)), pl.BlockSpec((2, tm), lambda i: (0, i))],
        out_shape=[jax.ShapeDtypeStruct((t, d), F32), jax.ShapeDtypeStruct((t, d), BF16),
                   jax.ShapeDtypeStruct((2, t), jnp.int32), jax.ShapeDtypeStruct((2, t), F32)],
        compiler_params=_cp(("parallel",)),
        name="out_proj_route",
    )(yr, yc, yn, h2, w_bf, g1, n2g.reshape(1, d), sh2, sc2, router_w.T, router_b.reshape(N_EXPERTS, 1))


def _swiglu_chunks(x, w_in_ref, w_out_ref, ff, n_chunks):
    tf = -(-ff // (n_chunks * MXU_COLS)) * MXU_COLS
    y = None
    for c0 in range(0, ff, tf):
        c1 = min(c0 + tf, ff)
        gate = _dot(x, w_in_ref[:, c0:c1])
        up = _dot(x, w_in_ref[:, ff + c0:ff + c1])
        part = _dot((_silu(gate) * up).astype(BF16), w_out_ref[c0:c1, :])
        y = part if y is None else y + part
    return y


def _outproj_ffn_kernel(yr_ref, yc_ref, yn_ref, h_ref, wo_ref, g1_ref, n2_ref, sh_ref, sc_ref,
                        wi_ref, wf_ref, g2_ref, o_ref, *, ff, n_chunks):
    ycat = jnp.concatenate([yr_ref[...], yc_ref[...], yn_ref[...]], axis=-1)
    h1 = h_ref[...] + g1_ref[0] * _dot(ycat, wo_ref[...])
    a = _norm_mod(h1, n2_ref[...], sh_ref[0], sc_ref[0]).astype(BF16)
    o_ref[...] = h1 + g2_ref[0] * _swiglu_chunks(a, wi_ref, wf_ref, ff, n_chunks)


def _outproj_ffn(yr, yc, yn, h2, wo_bf, g1, n2g, sh2, sc2, fw_in_bf, fw_out_bf, g2, seq, tm, n_chunks):
    t, d = h2.shape
    ff = fw_out_bf.shape[0]
    tps = seq // tm
    resident = pl.Buffered(1)
    row = lambda wdt: pl.BlockSpec((tm, wdt), lambda i: (i, 0))
    mod = pl.BlockSpec((1, 1, d), lambda i: (i // tps, 0, 0))
    return pl.pallas_call(
        functools.partial(_outproj_ffn_kernel, ff=ff, n_chunks=n_chunks),
        grid=(t // tm,),
        in_specs=[row(D_RET), row(D_CONV), row(D_NA), row(d),
                  pl.BlockSpec((d, d), lambda i: (0, 0), pipeline_mode=resident), mod,
                  pl.BlockSpec((1, d), lambda i: (0, 0)), mod, mod,
                  pl.BlockSpec((d, 2 * ff), lambda i: (0, 0), pipeline_mode=resident),
                  pl.BlockSpec((ff, d), lambda i: (0, 0), pipeline_mode=resident), mod],
        out_specs=row(d),
        out_shape=jax.ShapeDtypeStruct((t, d), F32),
        compiler_params=_cp(("parallel",)),
        name="out_proj_ffn_t%d" % t,
    )(yr, yc, yn, h2, wo_bf, g1, n2g.reshape(1, d), sh2, sc2, fw_in_bf, fw_out_bf, g2)


def _dispatch_kernel(s0_ref, s1_ref, zs_ref, zv_ref, x_ref, xs_ref, slab_ref, zbuf_ref, sem, *, rows):
    @pl.when(pl.program_id(0) == 0)
    def _():
        zbuf_ref[...] = jnp.zeros_like(zbuf_ref)
        for e in range(2 * N_EXPERTS):
            @pl.when(zv_ref[e] != 0)
            def _():
                dst = xs_ref.at[pl.ds(pl.multiple_of(zs_ref[e], MOE_TS), MOE_TS)]
                pltpu.make_async_copy(zbuf_ref, dst, sem.at[1]).start()
        for e in range(2 * N_EXPERTS):
            @pl.when(zv_ref[e] != 0)
            def _():
                pltpu.make_async_copy(zbuf_ref, xs_ref.at[pl.ds(0, MOE_TS)], sem.at[1]).wait()

    nk = slab_ref.shape[1]

    def issue(g, carry):
        rs = pl.ds(pl.multiple_of(g * DMA_UNROLL, DMA_UNROLL), DMA_UNROLL)
        for k in range(nk):
            slab_ref[rs, k, :] = x_ref[rs, k * LANES:(k + 1) * LANES].astype(slab_ref.dtype)
        for u in range(DMA_UNROLL):
            r = g * DMA_UNROLL + u
            src = slab_ref.at[r]
            pltpu.make_async_copy(src, xs_ref.at[s0_ref[r]], sem.at[0]).start(priority=u % 2)
            pltpu.make_async_copy(src, xs_ref.at[s1_ref[r]], sem.at[0]).start(priority=(u + 1) % 2)
        return carry

    lax.fori_loop(0, rows // DMA_UNROLL, issue, 0)
    for _ in range(2):
        pltpu.make_async_copy(slab_ref, xs_ref.at[pl.ds(0, rows)], sem.at[0]).wait()


def _dispatch(xn, slot0, slot1, zplan, n_slots, rows):
    t, d = xn.shape
    slab = (d // LANES, LANES)
    zstart, zvalid = zplan
    return pl.pallas_call(
        functools.partial(_dispatch_kernel, rows=rows),
        grid=(t // rows,),
        in_specs=[pl.BlockSpec((rows,), lambda i: (i,), memory_space=pltpu.SMEM),
                  pl.BlockSpec((rows,), lambda i: (i,), memory_space=pltpu.SMEM),
                  pl.BlockSpec((2 * N_EXPERTS,), lambda i: (0,), memory_space=pltpu.SMEM),
                  pl.BlockSpec((2 * N_EXPERTS,), lambda i: (0,), memory_space=pltpu.SMEM),
                  pl.BlockSpec((rows, d), lambda i: (i, 0))],
        out_specs=pl.BlockSpec(memory_space=pl.ANY),
        out_shape=jax.ShapeDtypeStruct((n_slots,) + slab, F32),
        scratch_shapes=[pltpu.VMEM((rows,) + slab, F32), pltpu.VMEM((MOE_TS,) + slab, F32),
                        pltpu.SemaphoreType.DMA((2,))],
        compiler_params=_cp(("arbitrary",)),
        name="moe_dispatch",
    )(slot0, slot1, zstart, zvalid, xn)


def _expert_kernel(be_ref, nu_ref, xs_ref, wi_ref, wo_ref, o_ref, xa_ref, xb_ref, sem, *, ff, n_chunks):
    j = pl.program_id(0)
    n = pl.num_programs(0)
    ts, d = xa_ref.shape
    bufs = (xa_ref, xb_ref)

    def copies(blk, b):
        rows = pl.ds(pl.multiple_of(blk * ts, ts), ts)
        return [pltpu.make_async_copy(xs_ref.at[rows, k, :], bufs[b].at[:, k * LANES:(k + 1) * LANES],
                                      sem.at[b]) for k in range(d // LANES)]

    @pl.when(j == 0)
    def _():
        for c in copies(0, 0):
            c.start()

    def step(b):
        @pl.when(j + 1 < n)
        def _():
            for c in copies(j + 1, 1 - b):
                c.start()

        for c in copies(j, b):
            c.wait()

        @pl.when(j < nu_ref[0])
        def _():
            o_ref[...] = _swiglu_chunks(bufs[b][...].astype(BF16), wi_ref.at[0], wo_ref.at[0], ff, n_chunks)

        @pl.when(j >= nu_ref[0])
        def _():
            o_ref[...] = jnp.zeros_like(o_ref)

    for b in range(2):
        pl.when(j % 2 == b)(functools.partial(step, b))


def _experts(xs, block_expert, n_used, w_in_bf, w_out_bf, n_chunks):
    n, nk, _ = xs.shape
    d = nk * LANES
    ff = w_out_bf.shape[1]
    ts = MOE_TS
    grid_spec = pltpu.PrefetchScalarGridSpec(
        num_scalar_prefetch=2,
        grid=(n // ts,),
        in_specs=[pl.BlockSpec(memory_space=pl.ANY),
                  pl.BlockSpec((1, d, 2 * ff), lambda j, be, nu: (be[j], 0, 0)),
                  pl.BlockSpec((1, ff, d), lambda j, be, nu: (be[j], 0, 0))],
        out_specs=pl.BlockSpec((ts, d), lambda j, be, nu: (j, 0)),
        scratch_shapes=[pltpu.VMEM((ts, d), xs.dtype), pltpu.VMEM((ts, d), xs.dtype),
                        pltpu.SemaphoreType.DMA((2,))],
    )
    return pl.pallas_call(
        functools.partial(_expert_kernel, ff=ff, n_chunks=n_chunks),
        grid_spec=grid_spec,
        out_shape=jax.ShapeDtypeStruct((n, d), F32),
        compiler_params=_cp(("arbitrary",)),
        name="moe_experts",
    )(block_expert, n_used, xs, w_in_bf, w_out_bf)


def _combine_kernel(p0_ref, p1_ref, q0_ref, q1_ref, ys_ref, h_ref, w0_ref, w1_ref, g2_ref, fg_ref, o_ref,
                    ya0_ref, ya1_ref, yb0_ref, yb1_ref, sem, *, rows):
    i = pl.program_id(0)
    n = pl.num_programs(0)
    n_iter = rows // COMBINE_GROUP
    bufs = ((ya0_ref, ya1_ref), (yb0_ref, yb1_ref))

    def issue_rows(pa_ref, pb_ref, b, g):
        for u in range(COMBINE_GROUP):
            r = g * COMBINE_GROUP + u
            pltpu.make_async_copy(ys_ref.at[pl.ds(pa_ref[r], 1)], bufs[b][0].at[pl.ds(r, 1)],
                                  sem.at[0, b]).start(priority=u % 2)
            pltpu.make_async_copy(ys_ref.at[pl.ds(pb_ref[r], 1)], bufs[b][1].at[pl.ds(r, 1)],
                                  sem.at[1, b]).start(priority=(u + 1) % 2)

    def finish_rows(b, g):
        rs = pl.ds(pl.multiple_of(g * COMBINE_GROUP, COMBINE_GROUP), COMBINE_GROUP)
        y = w0_ref[rs, :] * bufs[b][0][rs, :] + w1_ref[rs, :] * bufs[b][1][rs, :]
        h = h_ref[rs, :] + g2_ref[0] * y
        ms = jnp.mean(h * h, axis=-1, keepdims=True)
        o_ref[rs, :] = h * lax.rsqrt(ms + NORM_EPS) * fg_ref[...]

    @pl.when(i == 0)
    def _():
        def first(g, carry):
            issue_rows(p0_ref, p1_ref, 0, g)
            return carry
        lax.fori_loop(0, n_iter, first, 0)

    def step(b):
        pltpu.make_async_copy(ys_ref.at[pl.ds(0, rows)], bufs[b][0], sem.at[0, b]).wait()
        pltpu.make_async_copy(ys_ref.at[pl.ds(0, rows)], bufs[b][1], sem.at[1, b]).wait()

        @pl.when(i + 1 < n)
        def _():
            def both(g, carry):
                issue_rows(q0_ref, q1_ref, 1 - b, g)
                finish_rows(b, g)
                return carry
            lax.fori_loop(0, n_iter, both, 0)

        @pl.when(i + 1 == n)
        def _():
            def last(g, carry):
                finish_rows(b, g)
                return carry
            lax.fori_loop(0, n_iter, last, 0)

    for b in range(2):
        pl.when(i % 2 == b)(functools.partial(step, b))


def _combine(pos0, pos1, ys, h2, w0, w1, g2, final_g, seq, rows):
    t, d = h2.shape
    tps = seq // rows
    steps = t // rows
    blk = pl.BlockSpec((rows, d), lambda i: (i, 0))
    col = pl.BlockSpec((rows, 1), lambda i: (i, 0))
    here = pl.BlockSpec((rows,), lambda i: (i,), memory_space=pltpu.SMEM)
    ahead = pl.BlockSpec((rows,), lambda i: (jnp.minimum(i + 1, steps - 1),), memory_space=pltpu.SMEM)
    return pl.pallas_call(
        functools.partial(_combine_kernel, rows=rows),
        grid=(steps,),
        in_specs=[here, here, ahead, ahead,
                  pl.BlockSpec(memory_space=pl.ANY),
                  blk, col, col,
                  pl.BlockSpec((1, 1, d), lambda i: (i // tps, 0, 0)),
                  pl.BlockSpec((1, d), lambda i: (0, 0))],
        out_specs=blk,
        out_shape=jax.ShapeDtypeStruct((t, d), F32),
        scratch_shapes=[pltpu.VMEM((rows, d), F32)] * 4 + [pltpu.SemaphoreType.DMA((2, 2))],
        compiler_params=_cp(("arbitrary",)),
        name="moe_combine_norm",
    )(pos0, pos1, pos0, pos1, ys, h2, w0, w1, g2, final_g.reshape(1, d))


def _route_plan(idx):
    t = idx.shape[1]
    ts = MOE_TS
    n_slots = 2 * t + N_EXPERTS * ts
    e_flat = idx.reshape(-1)
    onehot = (e_flat[:, None] == jnp.arange(N_EXPERTS, dtype=jnp.int32)[None, :]).astype(jnp.int32)
    csum = jnp.cumsum(onehot, axis=0)
    rank = jnp.sum((csum - onehot) * onehot, axis=1)
    counts = csum[-1]
    padded = ((counts + ts - 1) // ts) * ts
    ends = jnp.cumsum(padded)
    offs = ends - padded
    slot = (jnp.sum(onehot * offs[None, :], axis=1) + rank).astype(jnp.int32)
    starts = jnp.arange(n_slots // ts, dtype=jnp.int32) * ts
    block_expert = jnp.minimum(jnp.sum((starts[:, None] >= ends[None, :]).astype(jnp.int32), axis=1),
                               N_EXPERTS - 1).astype(jnp.int32)
    n_used = (ends[-1] // ts).astype(jnp.int32).reshape(1)
    tail = ends[-1] + jnp.arange(N_EXPERTS, dtype=jnp.int32) * ts
    zstart = jnp.concatenate([ends - ts, tail])
    zvalid = jnp.concatenate([padded > 0, tail < n_slots]).astype(jnp.int32)
    zstart = jnp.where(zvalid != 0, zstart, 0).astype(jnp.int32)
    return slot[:t], slot[t:], block_expert, n_used, (zstart, zvalid), n_slots


def kernel(x, c, ctx, c_ctx, ada_w, ada_b, norm1_g, norm2_g, w_in, w_out, ret_decay_logit, ret_gn_g,
           conv_w, na_rpb, ffn_w_in, ffn_w_out, moe_router_w, moe_router_b, moe_w_in, moe_w_out, final_g):
    b, seq, d = x.shape
    ctx_len = ctx.shape[1]
    depth = ada_w.shape[0]
    assert depth == 2, "the final norm is fused into the last (MoE) layer's combine step"
    rows = seq // GRID_W
    t_lat, t_ctx = b * seq, b * ctx_len
    tm_lat = min(1024, seq)
    tm_ctx = ctx_len

    c8 = jnp.zeros((8, d), F32).at[:b].set(c).at[b].set(c_ctx)
    mods = _ada(c8, ada_w, ada_b)
    tables = _rope_tables(seq)

    h = x.reshape(t_lat, d)
    hc = ctx.reshape(t_ctx, d)
    zero_state = jnp.zeros((b, N_PAIRS, LANES, LANES), F32)
    out = None
    for layer in range(depth):
        update_ctx = layer < depth - 1
        m = mods[layer].reshape(8, 6, d)
        lat = [m[:b, i].reshape(b, 1, d) for i in range(6)]
        cx = [jnp.broadcast_to(m[b, i].reshape(1, 1, d), (b, 1, d)) for i in range(6)]
        w_in_bf = w_in[layer].astype(BF16)
        w_out_bf = w_out[layer].astype(BF16)
        lgt = ret_decay_logit[layer].reshape(2, N_PAIRS, LANES // HEAD_DIM)
        lgt = jnp.repeat(lgt, HEAD_DIM, axis=2).transpose(1, 0, 2)
        gn = ret_gn_g[layer].reshape(N_PAIRS, 1, LANES)

        proj_c = _in_proj(hc, norm1_g[layer], cx[0], cx[1], w_in_bf, None, ctx_len, tm_ctx)
        proj_c = proj_c.reshape(b, ctx_len, D_IN_PROJ)
        if layer % 2 == 0:
            proj, fw_in, fw_out = _in_proj(h, norm1_g[layer], lat[0], lat[1], w_in_bf, tables, seq, tm_lat,
                                           casts=(ffn_w_in[layer // 2], ffn_w_out[layer // 2]))
        else:
            proj = _in_proj(h, norm1_g[layer], lat[0], lat[1], w_in_bf, tables, seq, tm_lat)
        proj = proj.reshape(b, seq, D_IN_PROJ)

        y_ret_c, s_fwd, s_bwd = _retention(proj_c, lgt, gn, zero_state, zero_state)
        y_ret, _, _ = _retention(proj, lgt, gn, s_fwd, s_bwd)
        y_conv = _short_conv(proj, conv_w[layer])
        mw = moe_w_in[(layer + 1) // 2] if layer % 2 == 0 else moe_w_out[layer // 2]
        y_na, mw_bf = _na(proj, proj_c, _na_bias(na_rpb[layer], rows), cast=mw.reshape(-1, mw.shape[-1]))
        if layer % 2 == 0:
            mw_in_bf = mw_bf.reshape(mw.shape)
        else:
            mw_out_bf = mw_bf.reshape(mw.shape)

        if layer % 2 == 0:
            h = _outproj_ffn(y_ret.reshape(t_lat, D_RET), y_conv.reshape(t_lat, D_CONV),
                             y_na.reshape(t_lat, D_NA), h, w_out_bf, lat[2], norm2_g[layer], lat[3], lat[4],
                             fw_in, fw_out, lat[5], seq, min(1024, seq), FFN_CHUNKS)
            if update_ctx:
                y_conv_c = _short_conv(proj_c, conv_w[layer])
                y_na_c = _ctx_attn(proj_c)
                hc = _outproj_ffn(y_ret_c.reshape(t_ctx, D_RET), y_conv_c.reshape(t_ctx, D_CONV),
                                  y_na_c.reshape(t_ctx, D_NA), hc, w_out_bf, cx[2], norm2_g[layer], cx[3], cx[4],
                                  fw_in, fw_out, cx[5], ctx_len, tm_ctx, FFN_CHUNKS)
        else:
            e = layer // 2
            h, xn, idx, wts = _out_proj_route(y_ret.reshape(t_lat, D_RET), y_conv.reshape(t_lat, D_CONV),
                                              y_na.reshape(t_lat, D_NA), h, w_out_bf, lat[2], norm2_g[layer],
                                              lat[3], lat[4], seq, tm_lat, moe_router_w[e], moe_router_b[e])
            pos0, pos1, block_expert, n_used, zstart, n_slots = _route_plan(idx)
            xs = _dispatch(xn, pos0, pos1, zstart, n_slots, min(1024, seq))
            ys = _experts(xs, block_expert, n_used, mw_in_bf, mw_out_bf, FFN_CHUNKS)
            out = _combine(pos0, pos1, ys, h, wts[0].reshape(t_lat, 1), wts[1].reshape(t_lat, 1),
                           lat[5], final_g, seq, min(512, seq))
            h = out
    return out.reshape(b, seq, d)
```

```python
import functools

import numpy as np
import jax
import jax.numpy as jnp
from jax import lax
from jax.experimental import pallas as pl
from jax.experimental.pallas import tpu as pltpu

F32 = jnp.float32
BF16 = jnp.bfloat16

LANES = 128
BF16_ROWS = 16
HEAD_DIM = 64
GRID_W = 64
N_CONV_GROUPS = 4
N_RET_HEADS = 6
N_NA_HEADS = 6
D_RET = N_RET_HEADS * HEAD_DIM
D_CONV = N_CONV_GROUPS * HEAD_DIM
D_NA = N_NA_HEADS * HEAD_DIM
D_IN_PROJ = 4 * D_RET + 3 * D_CONV + 3 * D_NA
N_PAIRS = D_RET // LANES
NA_ROWS = 8
NA_COLS = 16
N_EXPERTS = 8
ROPE_BASE = 10000.0
NORM_EPS = 1e-6
NEG_BIG = -1e30
LOG2E = 1.4426950408889634

RET_CHUNK = 256
NA_QROWS = 4
NA_KROWS = 12
NA_STEP_BLOCKS = 2
MOE_TS = 512
DMA_UNROLL = 8
COMBINE_GROUP = 32
FFN_CHUNKS = 2
IN_PROJ_SUB = 512
OUT_PROJ_SUB = 512
MXU_COLS = 256
IN_PROJ_CHUNK = 3 * MXU_COLS
VMEM_LIMIT = 56 * 1024 * 1024

CB_RQ, CB_RK, CB_RV, CB_RG = 0, 3, 6, 9
CB_CB, CB_CC, CB_CX = 12, 14, 16
CB_NQ, CB_NK, CB_NV = 18, 21, 24


def _cp(sem, vmem=VMEM_LIMIT):
    return pltpu.CompilerParams(dimension_semantics=sem, vmem_limit_bytes=vmem)


def _silu(x):
    return x * (1.0 / (1.0 + jnp.exp(-x)))


def _dot(a, b):
    return jnp.dot(a, b, preferred_element_type=F32)


def _dot_nt(a, b):
    return lax.dot_general(a, b, (((1,), (1,)), ((), ())), preferred_element_type=F32)


def _dot_tn(a, b):
    return lax.dot_general(a, b, (((0,), (0,)), ((), ())), preferred_element_type=F32)


def _split_bf16(x):
    hi = x.astype(BF16)
    lo = (x - hi.astype(F32)).astype(BF16)
    return hi, lo


def _ada_kernel(c_ref, w_ref, b_ref, o_ref):
    x = _silu(c_ref[...]).astype(BF16)
    o_ref[0] = _dot(x, w_ref[0].astype(BF16)) + b_ref[0]


def _ada(c8, ada_w, ada_b):
    depth, d, n = ada_w.shape
    tn = n // 4
    return pl.pallas_call(
        _ada_kernel,
        grid=(depth, n // tn),
        in_specs=[pl.BlockSpec((8, d), lambda l, j: (0, 0)),
                  pl.BlockSpec((1, d, tn), lambda l, j: (l, 0, j)),
                  pl.BlockSpec((1, 1, tn), lambda l, j: (l, 0, j))],
        out_specs=pl.BlockSpec((1, 8, tn), lambda l, j: (l, 0, j)),
        out_shape=jax.ShapeDtypeStruct((depth, 8, n), F32),
        compiler_params=_cp(("parallel", "parallel")),
        name="ada_mod",
    )(c8, ada_w, ada_b.reshape(depth, 1, n))


def _norm_mod(x, g, sh, sc):
    ms = jnp.mean(x * x, axis=-1, keepdims=True)
    y = x * lax.rsqrt(ms + NORM_EPS) * g
    return y * (1.0 + sc) + sh


def _inproj_kernel(*refs, rope, n_cast):
    n_in = (8 if rope else 5) + n_cast
    for c_ref, cb_ref in zip(refs[n_in - n_cast:n_in], refs[n_in + 1:]):
        cb_ref[...] = c_ref[...].astype(BF16)
    if rope:
        h_ref, g_ref, sh_ref, sc_ref, w_ref, cos_ref, sa_ref, sb_ref = refs[:8]
    else:
        h_ref, g_ref, sh_ref, sc_ref, w_ref = refs[:5]
    o_ref = refs[n_in]
    tm = h_ref.shape[0]
    sub = min(IN_PROJ_SUB, tm)
    cw = IN_PROJ_CHUNK
    for r in range(tm // sub):
        rs = slice(r * sub, (r + 1) * sub)
        xn = _norm_mod(h_ref[rs, :], g_ref[...], sh_ref[0], sc_ref[0]).astype(BF16)
        for c0 in range(0, D_IN_PROJ, cw):
            c1 = min(c0 + cw, D_IN_PROJ)
            acc = _dot(xn, w_ref[:, c0:c1])
            for j in range((c1 - c0) // LANES):
                blk = acc[:, j * LANES:(j + 1) * LANES]
                cb = c0 // LANES + j
                if rope and cb < CB_RV:
                    blk = (blk * cos_ref[rs, :] + pltpu.roll(blk, 16, 1) * sa_ref[rs, :]
                           + pltpu.roll(blk, LANES - 16, 1) * sb_ref[rs, :])
                if CB_RK <= cb < CB_RV:
                    blk = blk * (HEAD_DIM ** -0.5)
                elif CB_NQ <= cb < CB_NK:
                    blk = blk * (HEAD_DIM ** -0.5 * (LOG2E if rope else 1.0))
                o_ref[rs, cb * LANES:(cb + 1) * LANES] = blk.astype(BF16)


def _in_proj(h2, g, sh, sc, w_bf, tables, seq, tm, casts=()):
    t, d = h2.shape
    tiles_per_seq = seq // tm
    steps = t // tm
    rope = tables is not None
    in_specs = [pl.BlockSpec((tm, d), lambda i: (i, 0)),
                pl.BlockSpec((1, d), lambda i: (0, 0)),
                pl.BlockSpec((1, 1, d), lambda i: (i // tiles_per_seq, 0, 0)),
                pl.BlockSpec((1, 1, d), lambda i: (i // tiles_per_seq, 0, 0)),
                pl.BlockSpec((d, D_IN_PROJ), lambda i: (0, 0))]
    args = [h2, g.reshape(1, d), sh, sc, w_bf]
    if rope:
        in_specs += [pl.BlockSpec((tm, LANES), lambda i: (i % tiles_per_seq, 0))] * 3
        args += list(tables)
    out_specs = [pl.BlockSpec((tm, D_IN_PROJ), lambda i: (i, 0))]
    out_shape = [jax.ShapeDtypeStruct((t, D_IN_PROJ), BF16)]
    for c in casts:
        cr, cc = c.shape
        assert cr % (steps * BF16_ROWS) == 0
        blk = pl.BlockSpec((cr // steps, cc), lambda i: (i, 0))
        in_specs.append(blk)
        out_specs.append(blk)
        out_shape.append(jax.ShapeDtypeStruct(c.shape, BF16))
        args.append(c)
    res = pl.pallas_call(
        functools.partial(_inproj_kernel, rope=rope, n_cast=len(casts)),
        grid=(steps,),
        in_specs=in_specs,
        out_specs=out_specs,
        out_shape=out_shape,
        compiler_params=_cp(("parallel",)),
        name="in_proj_rope" if rope else "in_proj_ctx",
    )(*args)
    return res if casts else res[0]


def _rope_tables(seq):
    t = np.arange(seq)
    row = (t // GRID_W).astype(np.float32)
    col = (t % GRID_W).astype(np.float32)
    n_freq = HEAD_DIM // 4
    inv_freq = (ROPE_BASE ** (-np.arange(n_freq, dtype=np.float32) / n_freq)).astype(np.float32)
    ang_r = row[:, None] * inv_freq
    ang_c = col[:, None] * inv_freq
    cos_h = np.concatenate([np.cos(ang_r), np.cos(ang_r), np.cos(ang_c), np.cos(ang_c)], axis=1)
    sin_h = np.concatenate([np.sin(ang_r), np.sin(ang_r), np.sin(ang_c), np.sin(ang_c)], axis=1)
    lane = np.arange(HEAD_DIM)
    second = (lane % 32) >= 16
    sa = np.where(second[None, :], sin_h, 0.0)
    sb = np.where(second[None, :], 0.0, -sin_h)
    tile2 = lambda a: jnp.asarray(np.concatenate([a, a], axis=1), F32)
    return tile2(cos_h), tile2(sa), tile2(sb)


def _ret_kernel(q_ref, k_ref, v_ref, g_ref, lgt_ref, gn_ref, sf0_ref, sb0_ref,
                y_ref, sfo_ref, sbo_ref, sfs_ref, sbs_ref, *, seq):
    c = RET_CHUNK
    n_chunks = seq // c
    lg = jax.nn.log_sigmoid(lgt_ref[0])
    lgf, lgb = lg[0:1, :], lg[1:2, :]
    pos = lax.broadcasted_iota(jnp.int32, (c, 1), 0).astype(F32)
    dkf = jnp.exp(lgf * (c - 1.0 - pos))
    dkb = jnp.exp(lgb * pos)
    dqf = jnp.exp(lgf * (pos + 1.0))
    dqb = jnp.exp(lgb * (c - pos))
    cdf = jnp.exp(lgf * float(c))
    cdb = jnp.exp(lgb * float(c))
    lane = lax.broadcasted_iota(jnp.int32, (1, LANES), 1)
    first = lane < HEAD_DIM
    ri = lax.broadcasted_iota(jnp.int32, (LANES, LANES), 0)
    ci = lax.broadcasted_iota(jnp.int32, (LANES, LANES), 1)
    same = (ri < HEAD_DIM) == (ci < HEAD_DIM)
    bd = same.astype(F32)

    def head_mean(x):
        a = jnp.sum(jnp.where(first, x, 0.0), axis=-1, keepdims=True)
        b = jnp.sum(jnp.where(first, 0.0, x), axis=-1, keepdims=True)
        return jnp.where(first, a, b) * (1.0 / HEAD_DIM)

    ii = lax.broadcasted_iota(jnp.int32, (c, c), 0)
    jj = lax.broadcasted_iota(jnp.int32, (c, c), 1)
    dif = (ii - jj).astype(F32)

    def decay_mask(h0):
        lf = lgf[:, h0:h0 + 1]
        lb = lgb[:, h0:h0 + 1]
        return jnp.where(dif > 0, jnp.exp(lf * jnp.maximum(dif, 0.0)),
                         jnp.where(dif < 0, jnp.exp(lb * jnp.maximum(-dif, 0.0)), 2.0))

    dm = (decay_mask(0), decay_mask(HEAD_DIM))
    mfirst = first.astype(BF16)
    msecond = (1.0 - first.astype(F32)).astype(BF16)

    unroll = min(8, n_chunks)

    def incr(n, carry):
        sl = pl.ds(pl.multiple_of(n * c, c), c)
        kf = k_ref[0, sl, :].astype(F32)
        v = v_ref[0, sl, :]
        sfs_ref[n] = _dot_tn((kf * dkf).astype(BF16), v) * bd
        sbs_ref[n] = _dot_tn((kf * dkb).astype(BF16), v) * bd
        return carry

    lax.fori_loop(0, n_chunks, incr, 0, unroll=unroll)

    def scan_f(n, s):
        u = sfs_ref[n]
        sfs_ref[n] = s
        return cdf * s + u

    def scan_b(i, s):
        n = n_chunks - 1 - i
        u = sbs_ref[n]
        sbs_ref[n] = s
        return cdb * s + u

    sfo_ref[0, 0] = lax.fori_loop(0, n_chunks, scan_f, sf0_ref[0, 0])
    sbo_ref[0, 0] = lax.fori_loop(0, n_chunks, scan_b, sb0_ref[0, 0])

    def outp(n, carry):
        sl = pl.ds(pl.multiple_of(n * c, c), c)
        q = q_ref[0, sl, :]
        k = k_ref[0, sl, :]
        v = v_ref[0, sl, :]
        qf = q.astype(F32)
        o = None
        for hh, mk in enumerate((mfirst, msecond)):
            s = _dot_nt(q * mk, k) * dm[hh]
            oh = _dot(s.astype(BF16), v)
            o = oh if o is None else jnp.where(first, o, oh)
        qcat = jnp.concatenate([(qf * dqf).astype(BF16), (qf * dqb).astype(BF16)], axis=1)
        scat = jnp.concatenate([sfs_ref[n], sbs_ref[n]], axis=0).astype(BF16)
        o = o + _dot(qcat, scat)
        mu = head_mean(o)
        dlt = o - mu
        var = head_mean(dlt * dlt)
        on = dlt * lax.rsqrt(var + NORM_EPS) * gn_ref[0]
        y_ref[0, sl, :] = (_silu(g_ref[0, sl, :].astype(F32)) * on).astype(BF16)
        return carry

    lax.fori_loop(0, n_chunks, outp, 0, unroll=unroll)


def _retention(proj, lgt, gn, sf0, sb0):
    b, seq, _ = proj.shape
    col = lambda off: pl.BlockSpec((1, seq, LANES), lambda bi, p: (bi, 0, off + p))
    st = pl.BlockSpec((1, 1, LANES, LANES), lambda bi, p: (bi, p, 0, 0))
    return pl.pallas_call(
        functools.partial(_ret_kernel, seq=seq),
        grid=(b, N_PAIRS),
        in_specs=[col(CB_RQ), col(CB_RK), col(CB_RV), col(CB_RG),
                  pl.BlockSpec((1, 2, LANES), lambda bi, p: (p, 0, 0)),
                  pl.BlockSpec((1, 1, LANES), lambda bi, p: (p, 0, 0)),
                  st, st],
        out_specs=[pl.BlockSpec((1, seq, LANES), lambda bi, p: (bi, 0, p)), st, st],
        out_shape=[jax.ShapeDtypeStruct((b, seq, D_RET), BF16),
                   jax.ShapeDtypeStruct((b, N_PAIRS, LANES, LANES), F32),
                   jax.ShapeDtypeStruct((b, N_PAIRS, LANES, LANES), F32)],
        scratch_shapes=[pltpu.VMEM((seq // RET_CHUNK, LANES, LANES), F32),
                        pltpu.VMEM((seq // RET_CHUNK, LANES, LANES), F32)],
        compiler_params=_cp(("parallel", "parallel")),
        name="retention_s%d" % seq,
    )(proj, proj, proj, proj, lgt, gn, sf0, sb0)


def _conv_kernel(b_ref, c_ref, x_ref, w_ref, y_ref, *, seq):
    u = c_ref[0].astype(F32) * x_ref[0].astype(F32)
    row = lax.broadcasted_iota(jnp.int32, (seq, 1), 0)
    prev = jnp.where(row == 0, 0.0, pltpu.roll(u, 1, 0))
    nxt = jnp.where(row == seq - 1, 0.0, pltpu.roll(u, seq - 1, 0))
    y = w_ref[0:1, :] * prev + w_ref[1:2, :] * u + w_ref[2:3, :] * nxt
    y_ref[0] = (b_ref[0].astype(F32) * y).astype(BF16)


def _short_conv(proj, conv_w):
    b, seq, _ = proj.shape
    nblk = D_CONV // LANES
    col = lambda off: pl.BlockSpec((1, seq, LANES), lambda bi, j: (bi, 0, off + j))
    return pl.pallas_call(
        functools.partial(_conv_kernel, seq=seq),
        grid=(b, nblk),
        in_specs=[col(CB_CB), col(CB_CC), col(CB_CX),
                  pl.BlockSpec((3, LANES), lambda bi, j: (0, j))],
        out_specs=pl.BlockSpec((1, seq, LANES), lambda bi, j: (bi, 0, j)),
        out_shape=jax.ShapeDtypeStruct((b, seq, D_CONV), BF16),
        compiler_params=_cp(("parallel", "parallel")),
        name="short_conv_s%d" % seq,
    )(proj, proj, proj, conv_w)


def _na_bias_plan(rows):
    plan = np.full((3, NA_QROWS, NA_KROWS), -1, np.int64)
    for cl, rb in enumerate((0, NA_QROWS, rows - NA_QROWS)):
        ws = int(np.clip(rb - NA_ROWS // 2, 0, rows - NA_KROWS))
        for i in range(NA_QROWS):
            r = rb + i
            r0 = int(np.clip(r - NA_ROWS // 2, 0, rows - NA_ROWS))
            for j in range(NA_KROWS):
                kr = ws + j
                if r0 <= kr < r0 + NA_ROWS:
                    plan[cl, i, j] = kr - r + (NA_ROWS - 1)
    return plan


def _na_bias_kernel(rpb_ref, o_ref, *, plan):
    w = GRID_W
    ndr, ndc = 2 * NA_ROWS - 1, 2 * NA_COLS - 1
    base = pl.program_id(0) * (ndr * ndc)
    c = lax.broadcasted_iota(jnp.int32, (w, w), 0)
    kc = lax.broadcasted_iota(jnp.int32, (w, w), 1)
    dcidx = jnp.clip(kc - c, -(NA_COLS - 1), NA_COLS - 1) + (NA_COLS - 1)
    c0 = jnp.clip(c - NA_COLS // 2, 0, w - NA_COLS)
    col_in = jnp.logical_and(kc >= c0, kc < c0 + NA_COLS)
    neg = jnp.full((w, w), NEG_BIG, F32)
    tiles = []
    for dr in range(ndr):
        t = neg
        for dcv in range(ndc):
            t = jnp.where(dcidx == dcv, rpb_ref[base + dr * ndc + dcv], t)
        tiles.append(jnp.where(col_in, t * LOG2E, NEG_BIG))
    for cl in range(plan.shape[0]):
        for i in range(plan.shape[1]):
            for j in range(0, plan.shape[2], 2):
                pair = [tiles[int(d)] if d >= 0 else neg for d in plan[cl, i, j:j + 2]]
                o_ref[0, cl, i * w:(i + 1) * w, j * w:(j + 2) * w] = jnp.concatenate(pair, axis=1)


def _na_bias(rpb, rows):
    h = rpb.shape[0]
    tq, tk = NA_QROWS * GRID_W, NA_KROWS * GRID_W
    return pl.pallas_call(
        functools.partial(_na_bias_kernel, plan=_na_bias_plan(rows)),
        grid=(h,),
        in_specs=[pl.BlockSpec(memory_space=pltpu.SMEM)],
        out_specs=pl.BlockSpec((1, 3, tq, tk), lambda hh: (hh, 0, 0, 0)),
        out_shape=jax.ShapeDtypeStruct((h, 3, tq, tk), F32),
        compiler_params=_cp(("parallel",)),
        name="na_bias_expand",
    )(rpb.reshape(-1))


def _na_kernel(q_ref, k_ref, v_ref, kc_ref, vc_ref, bias_ref, c_ref, o_ref, cb_ref, v1_ref, vc1_ref, *, rows):
    cb_ref[...] = c_ref[...].astype(BF16)
    step = pl.program_id(1)
    nqb = rows // NA_QROWS
    tq = NA_QROWS * GRID_W
    lane = lax.broadcasted_iota(jnp.int32, (1, LANES), 1)
    first = lane < HEAD_DIM

    @pl.when(step == 0)
    def _():
        one = jnp.ones((1, LANES), BF16)
        for p in range(N_PAIRS):
            cols = slice(p * LANES, (p + 1) * LANES)
            v1_ref[2 * p] = jnp.where(first, v_ref[0, :, cols], one)
            v1_ref[2 * p + 1] = jnp.where(first, one, v_ref[0, :, cols])
            vc1_ref[2 * p] = jnp.where(first, vc_ref[0, :, cols], one)
            vc1_ref[2 * p + 1] = jnp.where(first, one, vc_ref[0, :, cols])

    for sb in range(NA_STEP_BLOCKS):
        qb = step * NA_STEP_BLOCKS + sb
        cls = jnp.where(qb == 0, 0, jnp.where(qb == nqb - 1, 2, 1))
        ws = jnp.clip(qb * NA_QROWS - NA_ROWS // 2, 0, rows - NA_KROWS)
        sl = pl.ds(pl.multiple_of(ws * GRID_W, GRID_W), NA_KROWS * GRID_W)
        qrows = slice(sb * tq, (sb + 1) * tq)
        for p in range(N_PAIRS):
            cols = slice(p * LANES, (p + 1) * LANES)
            q = q_ref[0, qrows, cols]
            kw = k_ref[0, sl, cols]
            kc = kc_ref[0, :, cols]
            pv = []
            for hh in range(2):
                mk = (first if hh == 0 else jnp.logical_not(first)).astype(BF16)
                qh = q * mk
                s_loc = _dot_nt(qh, kw) + bias_ref[2 * p + hh, cls]
                s_ctx = _dot_nt(qh, kc)
                m = jnp.maximum(jnp.max(s_loc, axis=-1, keepdims=True), jnp.max(s_ctx, axis=-1, keepdims=True))
                p_loc = jnp.exp2(s_loc - m).astype(BF16)
                p_ctx = jnp.exp2(s_ctx - m).astype(BF16)
                pv.append(_dot(p_loc, v1_ref[2 * p + hh, sl, :]) + _dot(p_ctx, vc1_ref[2 * p + hh]))
            num = jnp.where(first, pv[0], pv[1])
            den = jnp.where(first, pltpu.roll(pv[0], HEAD_DIM, 1), pltpu.roll(pv[1], HEAD_DIM, 1))
            o_ref[0, qrows, cols] = (num / den).astype(BF16)


def _na(proj, proj_ctx, bias, cast):
    b, seq, _ = proj.shape
    ctx_len = proj_ctx.shape[1]
    rows = seq // GRID_W
    nqb = rows // NA_QROWS
    tq = NA_STEP_BLOCKS * NA_QROWS * GRID_W
    tk = NA_KROWS * GRID_W
    nh = 2 * N_PAIRS
    n_steps = nqb // NA_STEP_BLOCKS
    assert n_steps * NA_STEP_BLOCKS == nqb
    cq, ck, cv = (CB_NQ * LANES // D_NA, CB_NK * LANES // D_NA, CB_NV * LANES // D_NA)
    in_specs = [pl.BlockSpec((1, tq, D_NA), lambda bi, st: (bi, st, cq)),
                pl.BlockSpec((1, seq, D_NA), lambda bi, st: (bi, 0, ck)),
                pl.BlockSpec((1, seq, D_NA), lambda bi, st: (bi, 0, cv)),
                pl.BlockSpec((1, ctx_len, D_NA), lambda bi, st: (bi, 0, ck)),
                pl.BlockSpec((1, ctx_len, D_NA), lambda bi, st: (bi, 0, cv)),
                pl.BlockSpec(bias.shape, lambda bi, st: (0, 0, 0, 0), pipeline_mode=pl.Buffered(1))]
    out_specs = [pl.BlockSpec((1, tq, D_NA), lambda bi, st: (bi, st, 0))]
    out_shape = [jax.ShapeDtypeStruct((b, seq, D_NA), BF16)]
    cr, cc = cast.shape
    steps = b * n_steps
    assert cr % (steps * BF16_ROWS) == 0
    blk = pl.BlockSpec((cr // steps, cc), lambda bi, st: (bi * n_steps + st, 0))
    in_specs.append(blk)
    out_specs.append(blk)
    out_shape.append(jax.ShapeDtypeStruct(cast.shape, BF16))
    return pl.pallas_call(
        functools.partial(_na_kernel, rows=rows),
        grid=(b, n_steps),
        in_specs=in_specs,
        out_specs=out_specs,
        out_shape=out_shape,
        scratch_shapes=[pltpu.VMEM((nh, seq, LANES), BF16), pltpu.VMEM((nh, ctx_len, LANES), BF16)],
        compiler_params=_cp(("parallel", "arbitrary")),
        name="na_attn",
    )(proj, proj, proj, proj_ctx, proj_ctx, bias, cast)


def _ctx_attn_kernel(q_ref, k_ref, v_ref, o_ref):
    q = q_ref[0]
    k = k_ref[0]
    v = v_ref[0]
    lane = lax.broadcasted_iota(jnp.int32, (1, LANES), 1)
    first = lane < HEAD_DIM
    o = None
    for hh in range(2):
        mk = (first if hh == 0 else jnp.logical_not(first)).astype(BF16)
        s = _dot_nt(q * mk, k)
        m = jnp.max(s, axis=-1, keepdims=True)
        p = jnp.exp(s - m)
        l = jnp.sum(p, axis=-1, keepdims=True)
        oh = _dot(p.astype(BF16), v) / l
        o = oh if o is None else jnp.where(first, o, oh)
    o_ref[0] = o.astype(BF16)


def _ctx_attn(proj_ctx):
    b, ctx_len, _ = proj_ctx.shape
    col = lambda off: pl.BlockSpec((1, ctx_len, LANES), lambda bi, p: (bi, 0, off + p))
    return pl.pallas_call(
        _ctx_attn_kernel,
        grid=(b, N_PAIRS),
        in_specs=[col(CB_NQ), col(CB_NK), col(CB_NV)],
        out_specs=pl.BlockSpec((1, ctx_len, LANES), lambda bi, p: (bi, 0, p)),
        out_shape=jax.ShapeDtypeStruct((b, ctx_len, D_NA), BF16),
        compiler_params=_cp(("parallel", "parallel")),
        name="ctx_attn",
    )(proj_ctx, proj_ctx, proj_ctx)


def _outproj_route_kernel(yr_ref, yc_ref, yn_ref, h_ref, w_ref, g1_ref, n2_ref, sh_ref, sc_ref, rw_ref, rb_ref,
                          ho_ref, xn_ref, idx_ref, wt_ref):
    tm = h_ref.shape[0]
    sub = min(OUT_PROJ_SUB, tm)
    r_hi, r_lo = _split_bf16(rw_ref[...])
    r_both = jnp.concatenate([r_hi, r_lo], axis=0)
    for r in range(tm // sub):
        rs = slice(r * sub, (r + 1) * sub)
        ycat = jnp.concatenate([yr_ref[rs, :], yc_ref[rs, :], yn_ref[rs, :]], axis=-1)
        h = h_ref[rs, :] + g1_ref[0] * _dot(ycat, w_ref[...])
        ho_ref[rs, :] = h
        a = _norm_mod(h, n2_ref[...], sh_ref[0], sc_ref[0])
        xn_ref[rs, :] = a.astype(xn_ref.dtype)
        a_hi, a_lo = _split_bf16(a)
        t_hi = _dot_nt(r_both, a_hi)
        logits = t_hi[:N_EXPERTS] + t_hi[N_EXPERTS:] + _dot_nt(r_hi, a_lo) + rb_ref[...]
        eidx = lax.broadcasted_iota(jnp.int32, logits.shape, 0)
        m1 = jnp.max(logits, axis=0, keepdims=True)
        i1 = jnp.min(jnp.where(logits == m1, eidx, N_EXPERTS), axis=0, keepdims=True)
        rest = jnp.where(eidx == i1, -jnp.inf, logits)
        m2 = jnp.max(rest, axis=0, keepdims=True)
        i2 = jnp.min(jnp.where(rest == m2, eidx, N_EXPERTS), axis=0, keepdims=True)
        e2 = jnp.exp(m2 - m1)
        den = 1.0 + e2
        idx_ref[:, rs] = jnp.concatenate([i1, i2], axis=0)
        wt_ref[:, rs] = jnp.concatenate([1.0 / den, e2 / den], axis=0)


def _out_proj_route(yr, yc, yn, h2, w_bf, g1, n2g, sh2, sc2, seq, tm, router_w, router_b):
    t, d = h2.shape
    tps = seq // tm
    row = lambda wdt: pl.BlockSpec((tm, wdt), lambda i: (i, 0))
    mod = pl.BlockSpec((1, 1, d), lambda i: (i // tps, 0, 0))
    return pl.pallas_call(
        _outproj_route_kernel,
        grid=(t // tm,),
        in_specs=[row(D_RET), row(D_CONV), row(D_NA), row(d),
                  pl.BlockSpec((d, d), lambda i: (0, 0)), mod,
                  pl.BlockSpec((1, d), lambda i: (0, 0)), mod, mod,
                  pl.BlockSpec((N_EXPERTS, d), lambda i: (0, 0)),
                  pl.BlockSpec((N_EXPERTS, 1), lambda i: (0, 0))],
        out_specs=[row(d), row(d), pl.BlockSpec((2, tm), lambda i: (0, i)), pl.BlockSpec((2, tm), lambda i: (0, i))],
        out_shape=[jax.ShapeDtypeStruct((t, d), F32), jax.ShapeDtypeStruct((t, d), BF16),
                   jax.ShapeDtypeStruct((2, t), jnp.int32), jax.ShapeDtypeStruct((2, t), F32)],
        compiler_params=_cp(("parallel",)),
        name="out_proj_route",
    )(yr, yc, yn, h2, w_bf, g1, n2g.reshape(1, d), sh2, sc2, router_w.T, router_b.reshape(N_EXPERTS, 1))


def _swiglu_chunks(x, w_in_ref, w_out_ref, ff, n_chunks):
    tf = -(-ff // (n_chunks * MXU_COLS)) * MXU_COLS
    y = None
    for c0 in range(0, ff, tf):
        c1 = min(c0 + tf, ff)
        gate = _dot(x, w_in_ref[:, c0:c1])
        up = _dot(x, w_in_ref[:, ff + c0:ff + c1])
        part = _dot((_silu(gate) * up).astype(BF16), w_out_ref[c0:c1, :])
        y = part if y is None else y + part
    return y


def _outproj_ffn_kernel(yr_ref, yc_ref, yn_ref, h_ref, wo_ref, g1_ref, n2_ref, sh_ref, sc_ref,
                        wi_ref, wf_ref, g2_ref, o_ref, *, ff, n_chunks):
    ycat = jnp.concatenate([yr_ref[...], yc_ref[...], yn_ref[...]], axis=-1)
    h1 = h_ref[...] + g1_ref[0] * _dot(ycat, wo_ref[...])
    a = _norm_mod(h1, n2_ref[...], sh_ref[0], sc_ref[0]).astype(BF16)
    o_ref[...] = h1 + g2_ref[0] * _swiglu_chunks(a, wi_ref, wf_ref, ff, n_chunks)


def _outproj_ffn(yr, yc, yn, h2, wo_bf, g1, n2g, sh2, sc2, fw_in_bf, fw_out_bf, g2, seq, tm, n_chunks):
    t, d = h2.shape
    ff = fw_out_bf.shape[0]
    tps = seq // tm
    resident = pl.Buffered(1)
    row = lambda wdt: pl.BlockSpec((tm, wdt), lambda i: (i, 0))
    mod = pl.BlockSpec((1, 1, d), lambda i: (i // tps, 0, 0))
    return pl.pallas_call(
        functools.partial(_outproj_ffn_kernel, ff=ff, n_chunks=n_chunks),
        grid=(t // tm,),
        in_specs=[row(D_RET), row(D_CONV), row(D_NA), row(d),
                  pl.BlockSpec((d, d), lambda i: (0, 0), pipeline_mode=resident), mod,
                  pl.BlockSpec((1, d), lambda i: (0, 0)), mod, mod,
                  pl.BlockSpec((d, 2 * ff), lambda i: (0, 0), pipeline_mode=resident),
                  pl.BlockSpec((ff, d), lambda i: (0, 0), pipeline_mode=resident), mod],
        out_specs=row(d),
        out_shape=jax.ShapeDtypeStruct((t, d), F32),
        compiler_params=_cp(("parallel",)),
        name="out_proj_ffn_t%d" % t,
    )(yr, yc, yn, h2, wo_bf, g1, n2g.reshape(1, d), sh2, sc2, fw_in_bf, fw_out_bf, g2)


def _dispatch_kernel(s0_ref, s1_ref, zs_ref, zv_ref, x_ref, xs_ref, slab_ref, zbuf_ref, sem, *, rows):
    @pl.when(pl.program_id(0) == 0)
    def _():
        zbuf_ref[...] = jnp.zeros_like(zbuf_ref)
        for e in range(2 * N_EXPERTS):
            @pl.when(zv_ref[e] != 0)
            def _():
                dst = xs_ref.at[pl.ds(pl.multiple_of(zs_ref[e], MOE_TS), MOE_TS)]
                pltpu.make_async_copy(zbuf_ref, dst, sem.at[1]).start()
        for e in range(2 * N_EXPERTS):
            @pl.when(zv_ref[e] != 0)
            def _():
                pltpu.make_async_copy(zbuf_ref, xs_ref.at[pl.ds(0, MOE_TS)], sem.at[1]).wait()

    nk = slab_ref.shape[1]

    def issue(g, carry):
        rs = pl.ds(pl.multiple_of(g * DMA_UNROLL, DMA_UNROLL), DMA_UNROLL)
        for k in range(nk):
            slab_ref[rs, k, :] = x_ref[rs, k * LANES:(k + 1) * LANES].astype(slab_ref.dtype)
        for u in range(DMA_UNROLL):
            r = g * DMA_UNROLL + u
            src = slab_ref.at[r]
            pltpu.make_async_copy(src, xs_ref.at[s0_ref[r]], sem.at[0]).start(priority=u % 2)
            pltpu.make_async_copy(src, xs_ref.at[s1_ref[r]], sem.at[0]).start(priority=(u + 1) % 2)
        return carry

    lax.fori_loop(0, rows // DMA_UNROLL, issue, 0)
    for _ in range(2):
        pltpu.make_async_copy(slab_ref, xs_ref.at[pl.ds(0, rows)], sem.at[0]).wait()


def _dispatch(xn, slot0, slot1, zplan, n_slots, rows):
    t, d = xn.shape
    slab = (d // LANES, LANES)
    zstart, zvalid = zplan
    return pl.pallas_call(
        functools.partial(_dispatch_kernel, rows=rows),
        grid=(t // rows,),
        in_specs=[pl.BlockSpec((rows,), lambda i: (i,), memory_space=pltpu.SMEM),
                  pl.BlockSpec((rows,), lambda i: (i,), memory_space=pltpu.SMEM),
                  pl.BlockSpec((2 * N_EXPERTS,), lambda i: (0,), memory_space=pltpu.SMEM),
                  pl.BlockSpec((2 * N_EXPERTS,), lambda i: (0,), memory_space=pltpu.SMEM),
                  pl.BlockSpec((rows, d), lambda i: (i, 0))],
        out_specs=pl.BlockSpec(memory_space=pl.ANY),
        out_shape=jax.ShapeDtypeStruct((n_slots,) + slab, F32),
        scratch_shapes=[pltpu.VMEM((rows,) + slab, F32), pltpu.VMEM((MOE_TS,) + slab, F32),
                        pltpu.SemaphoreType.DMA((2,))],
        compiler_params=_cp(("arbitrary",)),
        name="moe_dispatch",
    )(slot0, slot1, zstart, zvalid, xn)


def _expert_kernel(be_ref, nu_ref, xs_ref, wi_ref, wo_ref, o_ref, xa_ref, xb_ref, sem, *, ff, n_chunks):
    j = pl.program_id(0)
    n = pl.num_programs(0)
    ts, d = xa_ref.shape
    bufs = (xa_ref, xb_ref)

    def copies(blk, b):
        rows = pl.ds(pl.multiple_of(blk * ts, ts), ts)
        return [pltpu.make_async_copy(xs_ref.at[rows, k, :], bufs[b].at[:, k * LANES:(k + 1) * LANES],
                                      sem.at[b]) for k in range(d // LANES)]

    @pl.when(j == 0)
    def _():
        for c in copies(0, 0):
            c.start()

    def step(b):
        @pl.when(j + 1 < n)
        def _():
            for c in copies(j + 1, 1 - b):
                c.start()

        for c in copies(j, b):
            c.wait()

        @pl.when(j < nu_ref[0])
        def _():
            o_ref[...] = _swiglu_chunks(bufs[b][...].astype(BF16), wi_ref.at[0], wo_ref.at[0], ff, n_chunks)

        @pl.when(j >= nu_ref[0])
        def _():
            o_ref[...] = jnp.zeros_like(o_ref)

    for b in range(2):
        pl.when(j % 2 == b)(functools.partial(step, b))


def _experts(xs, block_expert, n_used, w_in_bf, w_out_bf, n_chunks):
    n, nk, _ = xs.shape
    d = nk * LANES
    ff = w_out_bf.shape[1]
    ts = MOE_TS
    grid_spec = pltpu.PrefetchScalarGridSpec(
        num_scalar_prefetch=2,
        grid=(n // ts,),
        in_specs=[pl.BlockSpec(memory_space=pl.ANY),
                  pl.BlockSpec((1, d, 2 * ff), lambda j, be, nu: (be[j], 0, 0)),
                  pl.BlockSpec((1, ff, d), lambda j, be, nu: (be[j], 0, 0))],
        out_specs=pl.BlockSpec((ts, d), lambda j, be, nu: (j, 0)),
        scratch_shapes=[pltpu.VMEM((ts, d), xs.dtype), pltpu.VMEM((ts, d), xs.dtype),
                        pltpu.SemaphoreType.DMA((2,))],
    )
    return pl.pallas_call(
        functools.partial(_expert_kernel, ff=ff, n_chunks=n_chunks),
        grid_spec=grid_spec,
        out_shape=jax.ShapeDtypeStruct((n, d), F32),
        compiler_params=_cp(("arbitrary",)),
        name="moe_experts",
    )(block_expert, n_used, xs, w_in_bf, w_out_bf)


def _combine_kernel(p0_ref, p1_ref, q0_ref, q1_ref, ys_ref, h_ref, w0_ref, w1_ref, g2_ref, fg_ref, o_ref,
                    ya0_ref, ya1_ref, yb0_ref, yb1_ref, sem, *, rows):
    i = pl.program_id(0)
    n = pl.num_programs(0)
    n_iter = rows // COMBINE_GROUP
    bufs = ((ya0_ref, ya1_ref), (yb0_ref, yb1_ref))

    def issue_rows(pa_ref, pb_ref, b, g):
        for u in range(COMBINE_GROUP):
            r = g * COMBINE_GROUP + u
            pltpu.make_async_copy(ys_ref.at[pl.ds(pa_ref[r], 1)], bufs[b][0].at[pl.ds(r, 1)],
                                  sem.at[0, b]).start(priority=u % 2)
            pltpu.make_async_copy(ys_ref.at[pl.ds(pb_ref[r], 1)], bufs[b][1].at[pl.ds(r, 1)],
                                  sem.at[1, b]).start(priority=(u + 1) % 2)

    def finish_rows(b, g):
        rs = pl.ds(pl.multiple_of(g * COMBINE_GROUP, COMBINE_GROUP), COMBINE_GROUP)
        y = w0_ref[rs, :] * bufs[b][0][rs, :] + w1_ref[rs, :] * bufs[b][1][rs, :]
        h = h_ref[rs, :] + g2_ref[0] * y
        ms = jnp.mean(h * h, axis=-1, keepdims=True)
        o_ref[rs, :] = h * lax.rsqrt(ms + NORM_EPS) * fg_ref[...]

    @pl.when(i == 0)
    def _():
        def first(g, carry):
            issue_rows(p0_ref, p1_ref, 0, g)
            return carry
        lax.fori_loop(0, n_iter, first, 0)

    def step(b):
        pltpu.make_async_copy(ys_ref.at[pl.ds(0, rows)], bufs[b][0], sem.at[0, b]).wait()
        pltpu.make_async_copy(ys_ref.at[pl.ds(0, rows)], bufs[b][1], sem.at[1, b]).wait()

        @pl.when(i + 1 < n)
        def _():
            def both(g, carry):
                issue_rows(q0_ref, q1_ref, 1 - b, g)
                finish_rows(b, g)
                return carry
            lax.fori_loop(0, n_iter, both, 0)

        @pl.when(i + 1 == n)
        def _():
            def last(g, carry):
                finish_rows(b, g)
                return carry
            lax.fori_loop(0, n_iter, last, 0)

    for b in range(2):
        pl.when(i % 2 == b)(functools.partial(step, b))


def _combine(pos0, pos1, ys, h2, w0, w1, g2, final_g, seq, rows):
    t, d = h2.shape
    tps = seq // rows
    steps = t // rows
    blk = pl.BlockSpec((rows, d), lambda i: (i, 0))
    col = pl.BlockSpec((rows, 1), lambda i: (i, 0))
    here = pl.BlockSpec((rows,), lambda i: (i,), memory_space=pltpu.SMEM)
    ahead = pl.BlockSpec((rows,), lambda i: (jnp.minimum(i + 1, steps - 1),), memory_space=pltpu.SMEM)
    return pl.pallas_call(
        functools.partial(_combine_kernel, rows=rows),
        grid=(steps,),
        in_specs=[here, here, ahead, ahead,
                  pl.BlockSpec(memory_space=pl.ANY),
                  blk, col, col,
                  pl.BlockSpec((1, 1, d), lambda i: (i // tps, 0, 0)),
                  pl.BlockSpec((1, d), lambda i: (0, 0))],
        out_specs=blk,
        out_shape=jax.ShapeDtypeStruct((t, d), F32),
        scratch_shapes=[pltpu.VMEM((rows, d), F32)] * 4 + [pltpu.SemaphoreType.DMA((2, 2))],
        compiler_params=_cp(("arbitrary",)),
        name="moe_combine_norm",
    )(pos0, pos1, pos0, pos1, ys, h2, w0, w1, g2, final_g.reshape(1, d))


def _route_plan(idx):
    t = idx.shape[1]
    ts = MOE_TS
    n_slots = 2 * t + N_EXPERTS * ts
    e_flat = idx.reshape(-1)
    onehot = (e_flat[:, None] == jnp.arange(N_EXPERTS, dtype=jnp.int32)[None, :]).astype(jnp.int32)
    csum = jnp.cumsum(onehot, axis=0)
    rank = jnp.sum((csum - onehot) * onehot, axis=1)
    counts = csum[-1]
    padded = ((counts + ts - 1) // ts) * ts
    ends = jnp.cumsum(padded)
    offs = ends - padded
    slot = (jnp.sum(onehot * offs[None, :], axis=1) + rank).astype(jnp.int32)
    starts = jnp.arange(n_slots // ts, dtype=jnp.int32) * ts
    block_expert = jnp.minimum(jnp.sum((starts[:, None] >= ends[None, :]).astype(jnp.int32), axis=1),
                               N_EXPERTS - 1).astype(jnp.int32)
    n_used = (ends[-1] // ts).astype(jnp.int32).reshape(1)
    tail = ends[-1] + jnp.arange(N_EXPERTS, dtype=jnp.int32) * ts
    zstart = jnp.concatenate([ends - ts, tail])
    zvalid = jnp.concatenate([padded > 0, tail < n_slots]).astype(jnp.int32)
    zstart = jnp.where(zvalid != 0, zstart, 0).astype(jnp.int32)
    return slot[:t], slot[t:], block_expert, n_used, (zstart, zvalid), n_slots


def kernel(x, c, ctx, c_ctx, ada_w, ada_b, norm1_g, norm2_g, w_in, w_out, ret_decay_logit, ret_gn_g,
           conv_w, na_rpb, ffn_w_in, ffn_w_out, moe_router_w, moe_router_b, moe_w_in, moe_w_out, final_g):
    b, seq, d = x.shape
    ctx_len = ctx.shape[1]
    depth = ada_w.shape[0]
    assert depth == 2, "the final norm is fused into the last (MoE) layer's combine step"
    rows = seq // GRID_W
    t_lat, t_ctx = b * seq, b * ctx_len
    tm_lat = min(1024, seq)
    tm_ctx = ctx_len

    c8 = jnp.zeros((8, d), F32).at[:b].set(c).at[b].set(c_ctx)
    mods = _ada(c8, ada_w, ada_b)
    tables = _rope_tables(seq)

    h = x.reshape(t_lat, d)
    hc = ctx.reshape(t_ctx, d)
    zero_state = jnp.zeros((b, N_PAIRS, LANES, LANES), F32)
    out = None
    for layer in range(depth):
        update_ctx = layer < depth - 1
        m = mods[layer].reshape(8, 6, d)
        lat = [m[:b, i].reshape(b, 1, d) for i in range(6)]
        cx = [jnp.broadcast_to(m[b, i].reshape(1, 1, d), (b, 1, d)) for i in range(6)]
        w_in_bf = w_in[layer].astype(BF16)
        w_out_bf = w_out[layer].astype(BF16)
        lgt = ret_decay_logit[layer].reshape(2, N_PAIRS, LANES // HEAD_DIM)
        lgt = jnp.repeat(lgt, HEAD_DIM, axis=2).transpose(1, 0, 2)
        gn = ret_gn_g[layer].reshape(N_PAIRS, 1, LANES)

        proj_c = _in_proj(hc, norm1_g[layer], cx[0], cx[1], w_in_bf, None, ctx_len, tm_ctx)
        proj_c = proj_c.reshape(b, ctx_len, D_IN_PROJ)
        if layer % 2 == 0:
            proj, fw_in, fw_out = _in_proj(h, norm1_g[layer], lat[0], lat[1], w_in_bf, tables, seq, tm_lat,
                                           casts=(ffn_w_in[layer // 2], ffn_w_out[layer // 2]))
        else:
            proj = _in_proj(h, norm1_g[layer], lat[0], lat[1], w_in_bf, tables, seq, tm_lat)
        proj = proj.reshape(b, seq, D_IN_PROJ)

        y_ret_c, s_fwd, s_bwd = _retention(proj_c, lgt, gn, zero_state, zero_state)
        y_ret, _, _ = _retention(proj, lgt, gn, s_fwd, s_bwd)
        y_conv = _short_conv(proj, conv_w[layer])
        mw = moe_w_in[(layer + 1) // 2] if layer % 2 == 0 else moe_w_out[layer // 2]
        y_na, mw_bf = _na(proj, proj_c, _na_bias(na_rpb[layer], rows), cast=mw.reshape(-1, mw.shape[-1]))
        if layer % 2 == 0:
            mw_in_bf = mw_bf.reshape(mw.shape)
        else:
            mw_out_bf = mw_bf.reshape(mw.shape)

        if layer % 2 == 0:
            h = _outproj_ffn(y_ret.reshape(t_lat, D_RET), y_conv.reshape(t_lat, D_CONV),
                             y_na.reshape(t_lat, D_NA), h, w_out_bf, lat[2], norm2_g[layer], lat[3], lat[4],
                             fw_in, fw_out, lat[5], seq, min(1024, seq), FFN_CHUNKS)
            if update_ctx:
                y_conv_c = _short_conv(proj_c, conv_w[layer])
                y_na_c = _ctx_attn(proj_c)
                hc = _outproj_ffn(y_ret_c.reshape(t_ctx, D_RET), y_conv_c.reshape(t_ctx, D_CONV),
                                  y_na_c.reshape(t_ctx, D_NA), hc, w_out_bf, cx[2], norm2_g[layer], cx[3], cx[4],
                                  fw_in, fw_out, cx[5], ctx_len, tm_ctx, FFN_CHUNKS)
        else:
            e = layer // 2
            h, xn, idx, wts = _out_proj_route(y_ret.reshape(t_lat, D_RET), y_conv.reshape(t_lat, D_CONV),
                                              y_na.reshape(t_lat, D_NA), h, w_out_bf, lat[2], norm2_g[layer],
                                              lat[3], lat[4], seq, tm_lat, moe_router_w[e], moe_router_b[e])
            pos0, pos1, block_expert, n_used, zstart, n_slots = _route_plan(idx)
            xs = _dispatch(xn, pos0, pos1, zstart, n_slots, min(1024, seq))
            ys = _experts(xs, block_expert, n_used, mw_in_bf, mw_out_bf, FFN_CHUNKS)
            out = _combine(pos0, pos1, ys, h, wts[0].reshape(t_lat, 1), wts[1].reshape(t_lat, 1),
                           lat[5], final_g, seq, min(1024, seq))
            h = out
    return out.reshape(b, seq, d)
```

```python
import functools

import numpy as np
import jax
import jax.numpy as jnp
from jax import lax
from jax.experimental import pallas as pl
from jax.experimental.pallas import tpu as pltpu

F32 = jnp.float32
BF16 = jnp.bfloat16

LANES = 128
BF16_ROWS = 16
HEAD_DIM = 64
GRID_W = 64
N_CONV_GROUPS = 4
N_RET_HEADS = 6
N_NA_HEADS = 6
D_RET = N_RET_HEADS * HEAD_DIM
D_CONV = N_CONV_GROUPS * HEAD_DIM
D_NA = N_NA_HEADS * HEAD_DIM
D_IN_PROJ = 4 * D_RET + 3 * D_CONV + 3 * D_NA
N_PAIRS = D_RET // LANES
NA_ROWS = 8
NA_COLS = 16
N_EXPERTS = 8
ROPE_BASE = 10000.0
NORM_EPS = 1e-6
NEG_BIG = -1e30
LOG2E = 1.4426950408889634

RET_CHUNK = 256
NA_QROWS = 4
NA_KROWS = 12
NA_STEP_BLOCKS = 2
MOE_TS = 512
DMA_UNROLL = 8
COMBINE_GROUP = 32
FFN_CHUNKS = 2
IN_PROJ_SUB = 512
OUT_PROJ_SUB = 512
MXU_COLS = 256
IN_PROJ_CHUNK = 3 * MXU_COLS
VMEM_LIMIT = 56 * 1024 * 1024

CB_RQ, CB_RK, CB_RV, CB_RG = 0, 3, 6, 9
CB_CB, CB_CC, CB_CX = 12, 14, 16
CB_NQ, CB_NK, CB_NV = 18, 21, 24


def _cp(sem, vmem=VMEM_LIMIT):
    return pltpu.CompilerParams(dimension_semantics=sem, vmem_limit_bytes=vmem)


def _silu(x):
    return x * (1.0 / (1.0 + jnp.exp(-x)))


def _dot(a, b):
    return jnp.dot(a, b, preferred_element_type=F32)


def _dot_nt(a, b):
    return lax.dot_general(a, b, (((1,), (1,)), ((), ())), preferred_element_type=F32)


def _dot_tn(a, b):
    return lax.dot_general(a, b, (((0,), (0,)), ((), ())), preferred_element_type=F32)


def _split_bf16(x):
    hi = x.astype(BF16)
    lo = (x - hi.astype(F32)).astype(BF16)
    return hi, lo


def _ada_kernel(c_ref, w_ref, b_ref, o_ref):
    x = _silu(c_ref[...]).astype(BF16)
    o_ref[0] = _dot(x, w_ref[0].astype(BF16)) + b_ref[0]


def _ada(c8, ada_w, ada_b):
    depth, d, n = ada_w.shape
    tn = n // 4
    return pl.pallas_call(
        _ada_kernel,
        grid=(depth, n // tn),
        in_specs=[pl.BlockSpec((8, d), lambda l, j: (0, 0)),
                  pl.BlockSpec((1, d, tn), lambda l, j: (l, 0, j)),
                  pl.BlockSpec((1, 1, tn), lambda l, j: (l, 0, j))],
        out_specs=pl.BlockSpec((1, 8, tn), lambda l, j: (l, 0, j)),
        out_shape=jax.ShapeDtypeStruct((depth, 8, n), F32),
        compiler_params=_cp(("parallel", "parallel")),
        name="ada_mod",
    )(c8, ada_w, ada_b.reshape(depth, 1, n))


def _norm_mod(x, g, sh, sc):
    ms = jnp.mean(x * x, axis=-1, keepdims=True)
    y = x * lax.rsqrt(ms + NORM_EPS) * g
    return y * (1.0 + sc) + sh


def _inproj_kernel(*refs, rope, n_cast):
    n_in = (8 if rope else 5) + n_cast
    for c_ref, cb_ref in zip(refs[n_in - n_cast:n_in], refs[n_in + 1:]):
        cb_ref[...] = c_ref[...].astype(BF16)
    if rope:
        h_ref, g_ref, sh_ref, sc_ref, w_ref, cos_ref, sa_ref, sb_ref = refs[:8]
    else:
        h_ref, g_ref, sh_ref, sc_ref, w_ref = refs[:5]
    o_ref = refs[n_in]
    tm = h_ref.shape[0]
    sub = min(IN_PROJ_SUB, tm)
    cw = IN_PROJ_CHUNK
    for r in range(tm // sub):
        rs = slice(r * sub, (r + 1) * sub)
        xn = _norm_mod(h_ref[rs, :], g_ref[...], sh_ref[0], sc_ref[0]).astype(BF16)
        for c0 in range(0, D_IN_PROJ, cw):
            c1 = min(c0 + cw, D_IN_PROJ)
            acc = _dot(xn, w_ref[:, c0:c1])
            for j in range((c1 - c0) // LANES):
                blk = acc[:, j * LANES:(j + 1) * LANES]
                cb = c0 // LANES + j
                if rope and cb < CB_RV:
                    blk = (blk * cos_ref[rs, :] + pltpu.roll(blk, 16, 1) * sa_ref[rs, :]
                           + pltpu.roll(blk, LANES - 16, 1) * sb_ref[rs, :])
                if CB_RK <= cb < CB_RV:
                    blk = blk * (HEAD_DIM ** -0.5)
                elif CB_NQ <= cb < CB_NK:
                    blk = blk * (HEAD_DIM ** -0.5 * (LOG2E if rope else 1.0))
                o_ref[rs, cb * LANES:(cb + 1) * LANES] = blk.astype(BF16)


def _in_proj(h2, g, sh, sc, w_bf, tables, seq, tm, casts=()):
    t, d = h2.shape
    tiles_per_seq = seq // tm
    steps = t // tm
    rope = tables is not None
    in_specs = [pl.BlockSpec((tm, d), lambda i: (i, 0)),
                pl.BlockSpec((1, d), lambda i: (0, 0)),
                pl.BlockSpec((1, 1, d), lambda i: (i // tiles_per_seq, 0, 0)),
                pl.BlockSpec((1, 1, d), lambda i: (i // tiles_per_seq, 0, 0)),
                pl.BlockSpec((d, D_IN_PROJ), lambda i: (0, 0))]
    args = [h2, g.reshape(1, d), sh, sc, w_bf]
    if rope:
        in_specs += [pl.BlockSpec((tm, LANES), lambda i: (i % tiles_per_seq, 0))] * 3
        args += list(tables)
    out_specs = [pl.BlockSpec((tm, D_IN_PROJ), lambda i: (i, 0))]
    out_shape = [jax.ShapeDtypeStruct((t, D_IN_PROJ), BF16)]
    for c in casts:
        cr, cc = c.shape
        assert cr % (steps * BF16_ROWS) == 0
        blk = pl.BlockSpec((cr // steps, cc), lambda i: (i, 0))
        in_specs.append(blk)
        out_specs.append(blk)
        out_shape.append(jax.ShapeDtypeStruct(c.shape, BF16))
        args.append(c)
    res = pl.pallas_call(
        functools.partial(_inproj_kernel, rope=rope, n_cast=len(casts)),
        grid=(steps,),
        in_specs=in_specs,
        out_specs=out_specs,
        out_shape=out_shape,
        compiler_params=_cp(("parallel",)),
        name="in_proj_rope" if rope else "in_proj_ctx",
    )(*args)
    return res if casts else res[0]


def _rope_tables(seq):
    t = np.arange(seq)
    row = (t // GRID_W).astype(np.float32)
    col = (t % GRID_W).astype(np.float32)
    n_freq = HEAD_DIM // 4
    inv_freq = (ROPE_BASE ** (-np.arange(n_freq, dtype=np.float32) / n_freq)).astype(np.float32)
    ang_r = row[:, None] * inv_freq
    ang_c = col[:, None] * inv_freq
    cos_h = np.concatenate([np.cos(ang_r), np.cos(ang_r), np.cos(ang_c), np.cos(ang_c)], axis=1)
    sin_h = np.concatenate([np.sin(ang_r), np.sin(ang_r), np.sin(ang_c), np.sin(ang_c)], axis=1)
    lane = np.arange(HEAD_DIM)
    second = (lane % 32) >= 16
    sa = np.where(second[None, :], sin_h, 0.0)
    sb = np.where(second[None, :], 0.0, -sin_h)
    tile2 = lambda a: jnp.asarray(np.concatenate([a, a], axis=1), F32)
    return tile2(cos_h), tile2(sa), tile2(sb)


def _ret_kernel(q_ref, k_ref, v_ref, g_ref, lgt_ref, gn_ref, sf0_ref, sb0_ref,
                y_ref, sfo_ref, sbo_ref, sfs_ref, sbs_ref, *, seq):
    c = RET_CHUNK
    n_chunks = seq // c
    lg = jax.nn.log_sigmoid(lgt_ref[0])
    lgf, lgb = lg[0:1, :], lg[1:2, :]
    pos = lax.broadcasted_iota(jnp.int32, (c, 1), 0).astype(F32)
    dkf = jnp.exp(lgf * (c - 1.0 - pos))
    dkb = jnp.exp(lgb * pos)
    dqf = jnp.exp(lgf * (pos + 1.0))
    dqb = jnp.exp(lgb * (c - pos))
    cdf = jnp.exp(lgf * float(c))
    cdb = jnp.exp(lgb * float(c))
    lane = lax.broadcasted_iota(jnp.int32, (1, LANES), 1)
    first = lane < HEAD_DIM
    ri = lax.broadcasted_iota(jnp.int32, (LANES, LANES), 0)
    ci = lax.broadcasted_iota(jnp.int32, (LANES, LANES), 1)
    same = (ri < HEAD_DIM) == (ci < HEAD_DIM)
    bd = same.astype(F32)

    def head_mean(x):
        a = jnp.sum(jnp.where(first, x, 0.0), axis=-1, keepdims=True)
        b = jnp.sum(jnp.where(first, 0.0, x), axis=-1, keepdims=True)
        return jnp.where(first, a, b) * (1.0 / HEAD_DIM)

    ii = lax.broadcasted_iota(jnp.int32, (c, c), 0)
    jj = lax.broadcasted_iota(jnp.int32, (c, c), 1)
    dif = (ii - jj).astype(F32)

    def decay_mask(h0):
        lf = lgf[:, h0:h0 + 1]
        lb = lgb[:, h0:h0 + 1]
        return jnp.where(dif > 0, jnp.exp(lf * jnp.maximum(dif, 0.0)),
                         jnp.where(dif < 0, jnp.exp(lb * jnp.maximum(-dif, 0.0)), 2.0))

    dm = (decay_mask(0), decay_mask(HEAD_DIM))
    mfirst = first.astype(BF16)
    msecond = (1.0 - first.astype(F32)).astype(BF16)

    unroll = min(8, n_chunks)

    def incr(n, carry):
        sl = pl.ds(pl.multiple_of(n * c, c), c)
        kf = k_ref[0, sl, :].astype(F32)
        v = v_ref[0, sl, :]
        sfs_ref[n] = _dot_tn((kf * dkf).astype(BF16), v) * bd
        sbs_ref[n] = _dot_tn((kf * dkb).astype(BF16), v) * bd
        return carry

    lax.fori_loop(0, n_chunks, incr, 0, unroll=unroll)

    def scan_f(n, s):
        u = sfs_ref[n]
        sfs_ref[n] = s
        return cdf * s + u

    def scan_b(i, s):
        n = n_chunks - 1 - i
        u = sbs_ref[n]
        sbs_ref[n] = s
        return cdb * s + u

    sfo_ref[0, 0] = lax.fori_loop(0, n_chunks, scan_f, sf0_ref[0, 0])
    sbo_ref[0, 0] = lax.fori_loop(0, n_chunks, scan_b, sb0_ref[0, 0])

    def outp(n, carry):
        sl = pl.ds(pl.multiple_of(n * c, c), c)
        q = q_ref[0, sl, :]
        k = k_ref[0, sl, :]
        v = v_ref[0, sl, :]
        qf = q.astype(F32)
        o = None
        for hh, mk in enumerate((mfirst, msecond)):
            s = _dot_nt(q * mk, k) * dm[hh]
            oh = _dot(s.astype(BF16), v)
            o = oh if o is None else jnp.where(first, o, oh)
        qcat = jnp.concatenate([(qf * dqf).astype(BF16), (qf * dqb).astype(BF16)], axis=1)
        scat = jnp.concatenate([sfs_ref[n], sbs_ref[n]], axis=0).astype(BF16)
        o = o + _dot(qcat, scat)
        mu = head_mean(o)
        dlt = o - mu
        var = head_mean(dlt * dlt)
        on = dlt * lax.rsqrt(var + NORM_EPS) * gn_ref[0]
        y_ref[0, sl, :] = (_silu(g_ref[0, sl, :].astype(F32)) * on).astype(BF16)
        return carry

    lax.fori_loop(0, n_chunks, outp, 0, unroll=unroll)


def _retention(proj, lgt, gn, sf0, sb0):
    b, seq, _ = proj.shape
    col = lambda off: pl.BlockSpec((1, seq, LANES), lambda bi, p: (bi, 0, off + p))
    st = pl.BlockSpec((1, 1, LANES, LANES), lambda bi, p: (bi, p, 0, 0))
    return pl.pallas_call(
        functools.partial(_ret_kernel, seq=seq),
        grid=(b, N_PAIRS),
        in_specs=[col(CB_RQ), col(CB_RK), col(CB_RV), col(CB_RG),
                  pl.BlockSpec((1, 2, LANES), lambda bi, p: (p, 0, 0)),
                  pl.BlockSpec((1, 1, LANES), lambda bi, p: (p, 0, 0)),
                  st, st],
        out_specs=[pl.BlockSpec((1, seq, LANES), lambda bi, p: (bi, 0, p)), st, st],
        out_shape=[jax.ShapeDtypeStruct((b, seq, D_RET), BF16),
                   jax.ShapeDtypeStruct((b, N_PAIRS, LANES, LANES), F32),
                   jax.ShapeDtypeStruct((b, N_PAIRS, LANES, LANES), F32)],
        scratch_shapes=[pltpu.VMEM((seq // RET_CHUNK, LANES, LANES), F32),
                        pltpu.VMEM((seq // RET_CHUNK, LANES, LANES), F32)],
        compiler_params=_cp(("parallel", "parallel")),
        name="retention_s%d" % seq,
    )(proj, proj, proj, proj, lgt, gn, sf0, sb0)


def _conv_kernel(b_ref, c_ref, x_ref, w_ref, y_ref, *, seq):
    u = c_ref[0].astype(F32) * x_ref[0].astype(F32)
    row = lax.broadcasted_iota(jnp.int32, (seq, 1), 0)
    prev = jnp.where(row == 0, 0.0, pltpu.roll(u, 1, 0))
    nxt = jnp.where(row == seq - 1, 0.0, pltpu.roll(u, seq - 1, 0))
    y = w_ref[0:1, :] * prev + w_ref[1:2, :] * u + w_ref[2:3, :] * nxt
    y_ref[0] = (b_ref[0].astype(F32) * y).astype(BF16)


def _short_conv(proj, conv_w):
    b, seq, _ = proj.shape
    nblk = D_CONV // LANES
    col = lambda off: pl.BlockSpec((1, seq, LANES), lambda bi, j: (bi, 0, off + j))
    return pl.pallas_call(
        functools.partial(_conv_kernel, seq=seq),
        grid=(b, nblk),
        in_specs=[col(CB_CB), col(CB_CC), col(CB_CX),
                  pl.BlockSpec((3, LANES), lambda bi, j: (0, j))],
        out_specs=pl.BlockSpec((1, seq, LANES), lambda bi, j: (bi, 0, j)),
        out_shape=jax.ShapeDtypeStruct((b, seq, D_CONV), BF16),
        compiler_params=_cp(("parallel", "parallel")),
        name="short_conv_s%d" % seq,
    )(proj, proj, proj, conv_w)


def _na_bias_plan(rows):
    plan = np.full((3, NA_QROWS, NA_KROWS), -1, np.int64)
    for cl, rb in enumerate((0, NA_QROWS, rows - NA_QROWS)):
        ws = int(np.clip(rb - NA_ROWS // 2, 0, rows - NA_KROWS))
        for i in range(NA_QROWS):
            r = rb + i
            r0 = int(np.clip(r - NA_ROWS // 2, 0, rows - NA_ROWS))
            for j in range(NA_KROWS):
                kr = ws + j
                if r0 <= kr < r0 + NA_ROWS:
                    plan[cl, i, j] = kr - r + (NA_ROWS - 1)
    return plan


def _na_bias_kernel(rpb_ref, o_ref, *, plan):
    w = GRID_W
    ndr, ndc = 2 * NA_ROWS - 1, 2 * NA_COLS - 1
    base = pl.program_id(0) * (ndr * ndc)
    c = lax.broadcasted_iota(jnp.int32, (w, w), 0)
    kc = lax.broadcasted_iota(jnp.int32, (w, w), 1)
    dcidx = jnp.clip(kc - c, -(NA_COLS - 1), NA_COLS - 1) + (NA_COLS - 1)
    c0 = jnp.clip(c - NA_COLS // 2, 0, w - NA_COLS)
    col_in = jnp.logical_and(kc >= c0, kc < c0 + NA_COLS)
    neg = jnp.full((w, w), NEG_BIG, F32)
    tiles = []
    for dr in range(ndr):
        t = neg
        for dcv in range(ndc):
            t = jnp.where(dcidx == dcv, rpb_ref[base + dr * ndc + dcv], t)
        tiles.append(jnp.where(col_in, t * LOG2E, NEG_BIG))
    for cl in range(plan.shape[0]):
        for i in range(plan.shape[1]):
            for j in range(0, plan.shape[2], 2):
                pair = [tiles[int(d)] if d >= 0 else neg for d in plan[cl, i, j:j + 2]]
                o_ref[0, cl, i * w:(i + 1) * w, j * w:(j + 2) * w] = jnp.concatenate(pair, axis=1)


def _na_bias(rpb, rows):
    h = rpb.shape[0]
    tq, tk = NA_QROWS * GRID_W, NA_KROWS * GRID_W
    return pl.pallas_call(
        functools.partial(_na_bias_kernel, plan=_na_bias_plan(rows)),
        grid=(h,),
        in_specs=[pl.BlockSpec(memory_space=pltpu.SMEM)],
        out_specs=pl.BlockSpec((1, 3, tq, tk), lambda hh: (hh, 0, 0, 0)),
        out_shape=jax.ShapeDtypeStruct((h, 3, tq, tk), F32),
        compiler_params=_cp(("parallel",)),
        name="na_bias_expand",
    )(rpb.reshape(-1))


def _na_kernel(q_ref, k_ref, v_ref, kc_ref, vc_ref, bias_ref, c_ref, o_ref, cb_ref, v1_ref, vc1_ref, *, rows):
    cb_ref[...] = c_ref[...].astype(BF16)
    step = pl.program_id(1)
    nqb = rows // NA_QROWS
    tq = NA_QROWS * GRID_W
    lane = lax.broadcasted_iota(jnp.int32, (1, LANES), 1)
    first = lane < HEAD_DIM

    @pl.when(step == 0)
    def _():
        one = jnp.ones((1, LANES), BF16)
        for p in range(N_PAIRS):
            cols = slice(p * LANES, (p + 1) * LANES)
            v1_ref[2 * p] = jnp.where(first, v_ref[0, :, cols], one)
            v1_ref[2 * p + 1] = jnp.where(first, one, v_ref[0, :, cols])
            vc1_ref[2 * p] = jnp.where(first, vc_ref[0, :, cols], one)
            vc1_ref[2 * p + 1] = jnp.where(first, one, vc_ref[0, :, cols])

    for sb in range(NA_STEP_BLOCKS):
        qb = step * NA_STEP_BLOCKS + sb
        cls = jnp.where(qb == 0, 0, jnp.where(qb == nqb - 1, 2, 1))
        ws = jnp.clip(qb * NA_QROWS - NA_ROWS // 2, 0, rows - NA_KROWS)
        sl = pl.ds(pl.multiple_of(ws * GRID_W, GRID_W), NA_KROWS * GRID_W)
        qrows = slice(sb * tq, (sb + 1) * tq)
        for p in range(N_PAIRS):
            cols = slice(p * LANES, (p + 1) * LANES)
            q = q_ref[0, qrows, cols]
            kw = k_ref[0, sl, cols]
            kc = kc_ref[0, :, cols]
            pv = []
            for hh in range(2):
                mk = (first if hh == 0 else jnp.logical_not(first)).astype(BF16)
                qh = q * mk
                s_loc = _dot_nt(qh, kw) + bias_ref[2 * p + hh, cls]
                s_ctx = _dot_nt(qh, kc)
                m = jnp.maximum(jnp.max(s_loc, axis=-1, keepdims=True), jnp.max(s_ctx, axis=-1, keepdims=True))
                p_loc = jnp.exp2(s_loc - m).astype(BF16)
                p_ctx = jnp.exp2(s_ctx - m).astype(BF16)
                pv.append(_dot(p_loc, v1_ref[2 * p + hh, sl, :]) + _dot(p_ctx, vc1_ref[2 * p + hh]))
            num = jnp.where(first, pv[0], pv[1])
            den = jnp.where(first, pltpu.roll(pv[0], HEAD_DIM, 1), pltpu.roll(pv[1], HEAD_DIM, 1))
            o_ref[0, qrows, cols] = (num / den).astype(BF16)


def _na(proj, proj_ctx, bias, cast):
    b, seq, _ = proj.shape
    ctx_len = proj_ctx.shape[1]
    rows = seq // GRID_W
    nqb = rows // NA_QROWS
    tq = NA_STEP_BLOCKS * NA_QROWS * GRID_W
    tk = NA_KROWS * GRID_W
    nh = 2 * N_PAIRS
    n_steps = nqb // NA_STEP_BLOCKS
    assert n_steps * NA_STEP_BLOCKS == nqb
    cq, ck, cv = (CB_NQ * LANES // D_NA, CB_NK * LANES // D_NA, CB_NV * LANES // D_NA)
    in_specs = [pl.BlockSpec((1, tq, D_NA), lambda bi, st: (bi, st, cq)),
                pl.BlockSpec((1, seq, D_NA), lambda bi, st: (bi, 0, ck)),
                pl.BlockSpec((1, seq, D_NA), lambda bi, st: (bi, 0, cv)),
                pl.BlockSpec((1, ctx_len, D_NA), lambda bi, st: (bi, 0, ck)),
                pl.BlockSpec((1, ctx_len, D_NA), lambda bi, st: (bi, 0, cv)),
                pl.BlockSpec(bias.shape, lambda bi, st: (0, 0, 0, 0), pipeline_mode=pl.Buffered(1))]
    out_specs = [pl.BlockSpec((1, tq, D_NA), lambda bi, st: (bi, st, 0))]
    out_shape = [jax.ShapeDtypeStruct((b, seq, D_NA), BF16)]
    cr, cc = cast.shape
    steps = b * n_steps
    assert cr % (steps * BF16_ROWS) == 0
    blk = pl.BlockSpec((cr // steps, cc), lambda bi, st: (bi * n_steps + st, 0))
    in_specs.append(blk)
    out_specs.append(blk)
    out_shape.append(jax.ShapeDtypeStruct(cast.shape, BF16))
    return pl.pallas_call(
        functools.partial(_na_kernel, rows=rows),
        grid=(b, n_steps),
        in_specs=in_specs,
        out_specs=out_specs,
        out_shape=out_shape,
        scratch_shapes=[pltpu.VMEM((nh, seq, LANES), BF16), pltpu.VMEM((nh, ctx_len, LANES), BF16)],
        compiler_params=_cp(("parallel", "arbitrary")),
        name="na_attn",
    )(proj, proj, proj, proj_ctx, proj_ctx, bias, cast)


def _ctx_attn_kernel(q_ref, k_ref, v_ref, o_ref):
    q = q_ref[0]
    k = k_ref[0]
    v = v_ref[0]
    lane = lax.broadcasted_iota(jnp.int32, (1, LANES), 1)
    first = lane < HEAD_DIM
    o = None
    for hh in range(2):
        mk = (first if hh == 0 else jnp.logical_not(first)).astype(BF16)
        s = _dot_nt(q * mk, k)
        m = jnp.max(s, axis=-1, keepdims=True)
        p = jnp.exp(s - m)
        l = jnp.sum(p, axis=-1, keepdims=True)
        oh = _dot(p.astype(BF16), v) / l
        o = oh if o is None else jnp.where(first, o, oh)
    o_ref[0] = o.astype(BF16)


def _ctx_attn(proj_ctx):
    b, ctx_len, _ = proj_ctx.shape
    col = lambda off: pl.BlockSpec((1, ctx_len, LANES), lambda bi, p: (bi, 0, off + p))
    return pl.pallas_call(
        _ctx_attn_kernel,
        grid=(b, N_PAIRS),
        in_specs=[col(CB_NQ), col(CB_NK), col(CB_NV)],
        out_specs=pl.BlockSpec((1, ctx_len, LANES), lambda bi, p: (bi, 0, p)),
        out_shape=jax.ShapeDtypeStruct((b, ctx_len, D_NA), BF16),
        compiler_params=_cp(("parallel", "parallel")),
        name="ctx_attn",
    )(proj_ctx, proj_ctx, proj_ctx)


def _outproj_route_kernel(yr_ref, yc_ref, yn_ref, h_ref, w_ref, g1_ref, n2_ref, sh_ref, sc_ref, rw_ref, rb_ref,
                          ho_ref, xn_ref, idx_ref, wt_ref):
    tm = h_ref.shape[0]
    sub = min(OUT_PROJ_SUB, tm)
    r_hi, r_lo = _split_bf16(rw_ref[...])
    r_both = jnp.concatenate([r_hi, r_lo], axis=0)
    for r in range(tm // sub):
        rs = slice(r * sub, (r + 1) * sub)
        ycat = jnp.concatenate([yr_ref[rs, :], yc_ref[rs, :], yn_ref[rs, :]], axis=-1)
        h = h_ref[rs, :] + g1_ref[0] * _dot(ycat, w_ref[...])
        ho_ref[rs, :] = h
        a = _norm_mod(h, n2_ref[...], sh_ref[0], sc_ref[0])
        xn_ref[rs, :] = a.astype(xn_ref.dtype)
        a_hi, a_lo = _split_bf16(a)
        t_hi = _dot_nt(r_both, a_hi)
        logits = t_hi[:N_EXPERTS] + t_hi[N_EXPERTS:] + _dot_nt(r_hi, a_lo) + rb_ref[...]
        eidx = lax.broadcasted_iota(jnp.int32, logits.shape, 0)
        m1 = jnp.max(logits, axis=0, keepdims=True)
        i1 = jnp.min(jnp.where(logits == m1, eidx, N_EXPERTS), axis=0, keepdims=True)
        rest = jnp.where(eidx == i1, -jnp.inf, logits)
        m2 = jnp.max(rest, axis=0, keepdims=True)
        i2 = jnp.min(jnp.where(rest == m2, eidx, N_EXPERTS), axis=0, keepdims=True)
        e2 = jnp.exp(m2 - m1)
        den = 1.0 + e2
        idx_ref[:, rs] = jnp.concatenate([i1, i2], axis=0)
        wt_ref[:, rs] = jnp.concatenate([1.0 / den, e2 / den], axis=0)


def _out_proj_route(yr, yc, yn, h2, w_bf, g1, n2g, sh2, sc2, seq, tm, router_w, router_b):
    t, d = h2.shape
    tps = seq // tm
    row = lambda wdt: pl.BlockSpec((tm, wdt), lambda i: (i, 0))
    mod = pl.BlockSpec((1, 1, d), lambda i: (i // tps, 0, 0))
    return pl.pallas_call(
        _outproj_route_kernel,
        grid=(t // tm,),
        in_specs=[row(D_RET), row(D_CONV), row(D_NA), row(d),
                  pl.BlockSpec((d, d), lambda i: (0, 0)), mod,
                  pl.BlockSpec((1, d), lambda i: (0, 0)), mod, mod,
                  pl.BlockSpec((N_EXPERTS, d), lambda i: (0, 0)),
                  pl.BlockSpec((N_EXPERTS, 1), lambda i: (0, 0))],
        out_specs=[row(d), row(d), pl.BlockSpec((2, tm), lambda i: (0, i)), pl.BlockSpec((2, tm), lambda i: (0, i))],
        out_shape=[jax.ShapeDtypeStruct((t, d), F32), jax.ShapeDtypeStruct((t, d), BF16),
                   jax.ShapeDtypeStruct((2, t), jnp.int32), jax.ShapeDtypeStruct((2, t), F32)],
        compiler_params=_cp(("parallel",)),
        name="out_proj_route",
    )(yr, yc, yn, h2, w_bf, g1, n2g.reshape(1, d), sh2, sc2, router_w.T, router_b.reshape(N_EXPERTS, 1))


def _swiglu_chunks(x, w_in_ref, w_out_ref, ff, n_chunks):
    tf = -(-ff // (n_chunks * MXU_COLS)) * MXU_COLS
    y = None
    for c0 in range(0, ff, tf):
        c1 = min(c0 + tf, ff)
        gate = _dot(x, w_in_ref[:, c0:c1])
        up = _dot(x, w_in_ref[:, ff + c0:ff + c1])
        part = _dot((_silu(gate) * up).astype(BF16), w_out_ref[c0:c1, :])
        y = part if y is None else y + part
    return y


def _outproj_ffn_kernel(yr_ref, yc_ref, yn_ref, h_ref, wo_ref, g1_ref, n2_ref, sh_ref, sc_ref,
                        wi_ref, wf_ref, g2_ref, o_ref, *, ff, n_chunks):
    ycat = jnp.concatenate([yr_ref[...], yc_ref[...], yn_ref[...]], axis=-1)
    h1 = h_ref[...] + g1_ref[0] * _dot(ycat, wo_ref[...])
    a = _norm_mod(h1, n2_ref[...], sh_ref[0], sc_ref[0]).astype(BF16)
    o_ref[...] = h1 + g2_ref[0] * _swiglu_chunks(a, wi_ref, wf_ref, ff, n_chunks)


def _outproj_ffn(yr, yc, yn, h2, wo_bf, g1, n2g, sh2, sc2, fw_in_bf, fw_out_bf, g2, seq, tm, n_chunks):
    t, d = h2.shape
    ff = fw_out_bf.shape[0]
    tps = seq // tm
    resident = pl.Buffered(1)
    row = lambda wdt: pl.BlockSpec((tm, wdt), lambda i: (i, 0))
    mod = pl.BlockSpec((1, 1, d), lambda i: (i // tps, 0, 0))
    return pl.pallas_call(
        functools.partial(_outproj_ffn_kernel, ff=ff, n_chunks=n_chunks),
        grid=(t // tm,),
        in_specs=[row(D_RET), row(D_CONV), row(D_NA), row(d),
                  pl.BlockSpec((d, d), lambda i: (0, 0), pipeline_mode=resident), mod,
                  pl.BlockSpec((1, d), lambda i: (0, 0)), mod, mod,
                  pl.BlockSpec((d, 2 * ff), lambda i: (0, 0), pipeline_mode=resident),
                  pl.BlockSpec((ff, d), lambda i: (0, 0), pipeline_mode=resident), mod],
        out_specs=row(d),
        out_shape=jax.ShapeDtypeStruct((t, d), F32),
        compiler_params=_cp(("parallel",)),
        name="out_proj_ffn_t%d" % t,
    )(yr, yc, yn, h2, wo_bf, g1, n2g.reshape(1, d), sh2, sc2, fw_in_bf, fw_out_bf, g2)


def _dispatch_kernel(s0_ref, s1_ref, zs_ref, zv_ref, x_ref, xs_ref, slab_ref, zbuf_ref, sem, *, rows):
    @pl.when(pl.program_id(0) == 0)
    def _():
        zbuf_ref[...] = jnp.zeros_like(zbuf_ref)
        for e in range(2 * N_EXPERTS):
            @pl.when(zv_ref[e] != 0)
            def _():
                dst = xs_ref.at[pl.ds(pl.multiple_of(zs_ref[e], MOE_TS), MOE_TS)]
                pltpu.make_async_copy(zbuf_ref, dst, sem.at[1]).start()
        for e in range(2 * N_EXPERTS):
            @pl.when(zv_ref[e] != 0)
            def _():
                pltpu.make_async_copy(zbuf_ref, xs_ref.at[pl.ds(0, MOE_TS)], sem.at[1]).wait()

    nk = slab_ref.shape[1]

    def issue(g, carry):
        rs = pl.ds(pl.multiple_of(g * DMA_UNROLL, DMA_UNROLL), DMA_UNROLL)
        for k in range(nk):
            slab_ref[rs, k, :] = x_ref[rs, k * LANES:(k + 1) * LANES].astype(slab_ref.dtype)
        for u in range(DMA_UNROLL):
            r = g * DMA_UNROLL + u
            src = slab_ref.at[r]
            pltpu.make_async_copy(src, xs_ref.at[s0_ref[r]], sem.at[0]).start(priority=u % 2)
            pltpu.make_async_copy(src, xs_ref.at[s1_ref[r]], sem.at[0]).start(priority=(u + 1) % 2)
        return carry

    lax.fori_loop(0, rows // DMA_UNROLL, issue, 0)
    for _ in range(2):
        pltpu.make_async_copy(slab_ref, xs_ref.at[pl.ds(0, rows)], sem.at[0]).wait()


def _dispatch(xn, slot0, slot1, zplan, n_slots, rows):
    t, d = xn.shape
    slab = (d // LANES, LANES)
    zstart, zvalid = zplan
    return pl.pallas_call(
        functools.partial(_dispatch_kernel, rows=rows),
        grid=(t // rows,),
        in_specs=[pl.BlockSpec((rows,), lambda i: (i,), memory_space=pltpu.SMEM),
                  pl.BlockSpec((rows,), lambda i: (i,), memory_space=pltpu.SMEM),
                  pl.BlockSpec((2 * N_EXPERTS,), lambda i: (0,), memory_space=pltpu.SMEM),
                  pl.BlockSpec((2 * N_EXPERTS,), lambda i: (0,), memory_space=pltpu.SMEM),
                  pl.BlockSpec((rows, d), lambda i: (i, 0))],
        out_specs=pl.BlockSpec(memory_space=pl.ANY),
        out_shape=jax.ShapeDtypeStruct((n_slots,) + slab, F32),
        scratch_shapes=[pltpu.VMEM((rows,) + slab, F32), pltpu.VMEM((MOE_TS,) + slab, F32),
                        pltpu.SemaphoreType.DMA((2,))],
        compiler_params=_cp(("arbitrary",)),
        name="moe_dispatch",
    )(slot0, slot1, zstart, zvalid, xn)


def _expert_kernel(be_ref, nu_ref, xs_ref, wi_ref, wo_ref, o_ref, xa_ref, xb_ref, sem, *, ff, n_chunks):
    j = pl.program_id(0)
    n = pl.num_programs(0)
    ts, d = xa_ref.shape
    bufs = (xa_ref, xb_ref)

    def copies(blk, b):
        rows = pl.ds(pl.multiple_of(blk * ts, ts), ts)
        return [pltpu.make_async_copy(xs_ref.at[rows, k, :], bufs[b].at[:, k * LANES:(k + 1) * LANES],
                                      sem.at[b]) for k in range(d // LANES)]

    @pl.when(j == 0)
    def _():
        for c in copies(0, 0):
            c.start()

    def step(b):
        @pl.when(j + 1 < n)
        def _():
            for c in copies(j + 1, 1 - b):
                c.start()

        for c in copies(j, b):
            c.wait()

        @pl.when(j < nu_ref[0])
        def _():
            o_ref[...] = _swiglu_chunks(bufs[b][...].astype(BF16), wi_ref.at[0], wo_ref.at[0], ff, n_chunks)

        @pl.when(j >= nu_ref[0])
        def _():
            o_ref[...] = jnp.zeros_like(o_ref)

    for b in range(2):
        pl.when(j % 2 == b)(functools.partial(step, b))


def _experts(xs, block_expert, n_used, w_in_bf, w_out_bf, n_chunks):
    n, nk, _ = xs.shape
    d = nk * LANES
    ff = w_out_bf.shape[1]
    ts = MOE_TS
    grid_spec = pltpu.PrefetchScalarGridSpec(
        num_scalar_prefetch=2,
        grid=(n // ts,),
        in_specs=[pl.BlockSpec(memory_space=pl.ANY),
                  pl.BlockSpec((1, d, 2 * ff), lambda j, be, nu: (be[j], 0, 0)),
                  pl.BlockSpec((1, ff, d), lambda j, be, nu: (be[j], 0, 0))],
        out_specs=pl.BlockSpec((ts, d), lambda j, be, nu: (j, 0)),
        scratch_shapes=[pltpu.VMEM((ts, d), xs.dtype), pltpu.VMEM((ts, d), xs.dtype),
                        pltpu.SemaphoreType.DMA((2,))],
    )
    return pl.pallas_call(
        functools.partial(_expert_kernel, ff=ff, n_chunks=n_chunks),
        grid_spec=grid_spec,
        out_shape=jax.ShapeDtypeStruct((n, d), F32),
        compiler_params=_cp(("arbitrary",)),
        name="moe_experts",
    )(block_expert, n_used, xs, w_in_bf, w_out_bf)


def _combine_kernel(p0_ref, p1_ref, q0_ref, q1_ref, ys_ref, h_ref, w0_ref, w1_ref, g2_ref, fg_ref, o_ref,
                    ya0_ref, ya1_ref, yb0_ref, yb1_ref, sem, *, rows):
    i = pl.program_id(0)
    n = pl.num_programs(0)
    n_iter = rows // COMBINE_GROUP
    bufs = ((ya0_ref, ya1_ref), (yb0_ref, yb1_ref))

    def issue_rows(pa_ref, pb_ref, b, g):
        for u in range(COMBINE_GROUP):
            r = g * COMBINE_GROUP + u
            pltpu.make_async_copy(ys_ref.at[pl.ds(pa_ref[r], 1)], bufs[b][0].at[pl.ds(r, 1)],
                                  sem.at[0, b]).start(priority=u % 2)
            pltpu.make_async_copy(ys_ref.at[pl.ds(pb_ref[r], 1)], bufs[b][1].at[pl.ds(r, 1)],
                                  sem.at[1, b]).start(priority=(u + 1) % 2)

    def finish_rows(b, g):
        rs = pl.ds(pl.multiple_of(g * COMBINE_GROUP, COMBINE_GROUP), COMBINE_GROUP)
        y = w0_ref[rs, :] * bufs[b][0][rs, :] + w1_ref[rs, :] * bufs[b][1][rs, :]
        h = h_ref[rs, :] + g2_ref[0] * y
        ms = jnp.mean(h * h, axis=-1, keepdims=True)
        o_ref[rs, :] = h * lax.rsqrt(ms + NORM_EPS) * fg_ref[...]

    @pl.when(i == 0)
    def _():
        def first(g, carry):
            issue_rows(p0_ref, p1_ref, 0, g)
            return carry
        lax.fori_loop(0, n_iter, first, 0)

    def step(b):
        pltpu.make_async_copy(ys_ref.at[pl.ds(0, rows)], bufs[b][0], sem.at[0, b]).wait()
        pltpu.make_async_copy(ys_ref.at[pl.ds(0, rows)], bufs[b][1], sem.at[1, b]).wait()

        @pl.when(i + 1 < n)
        def _():
            def both(g, carry):
                issue_rows(q0_ref, q1_ref, 1 - b, g)
                finish_rows(b, g)
                return carry
            lax.fori_loop(0, n_iter, both, 0)

        @pl.when(i + 1 == n)
        def _():
            def last(g, carry):
                finish_rows(b, g)
                return carry
            lax.fori_loop(0, n_iter, last, 0)

    for b in range(2):
        pl.when(i % 2 == b)(functools.partial(step, b))


def _combine(pos0, pos1, ys, h2, w0, w1, g2, final_g, seq, rows):
    t, d = h2.shape
    tps = seq // rows
    steps = t // rows
    blk = pl.BlockSpec((rows, d), lambda i: (i, 0))
    col = pl.BlockSpec((rows, 1), lambda i: (i, 0))
    here = pl.BlockSpec((rows,), lambda i: (i,), memory_space=pltpu.SMEM)
    ahead = pl.BlockSpec((rows,), lambda i: (jnp.minimum(i + 1, steps - 1),), memory_space=pltpu.SMEM)
    return pl.pallas_call(
        functools.partial(_combine_kernel, rows=rows),
        grid=(steps,),
        in_specs=[here, here, ahead, ahead,
                  pl.BlockSpec(memory_space=pl.ANY),
                  blk, col, col,
                  pl.BlockSpec((1, 1, d), lambda i: (i // tps, 0, 0)),
                  pl.BlockSpec((1, d), lambda i: (0, 0))],
        out_specs=blk,
        out_shape=jax.ShapeDtypeStruct((t, d), F32),
        scratch_shapes=[pltpu.VMEM((rows, d), F32)] * 4 + [pltpu.SemaphoreType.DMA((2, 2))],
        compiler_params=_cp(("arbitrary",)),
        name="moe_combine_norm",
    )(pos0, pos1, pos0, pos1, ys, h2, w0, w1, g2, final_g.reshape(1, d))


def _route_plan(idx):
    t = idx.shape[1]
    ts = MOE_TS
    n_slots = 2 * t + N_EXPERTS * ts
    e_flat = idx.reshape(-1)
    onehot = (e_flat[:, None] == jnp.arange(N_EXPERTS, dtype=jnp.int32)[None, :]).astype(jnp.int32)
    csum = jnp.cumsum(onehot, axis=0)
    rank = jnp.sum((csum - onehot) * onehot, axis=1)
    counts = csum[-1]
    padded = ((counts + ts - 1) // ts) * ts
    ends = jnp.cumsum(padded)
    offs = ends - padded
    slot = (jnp.sum(onehot * offs[None, :], axis=1) + rank).astype(jnp.int32)
    starts = jnp.arange(n_slots // ts, dtype=jnp.int32) * ts
    block_expert = jnp.minimum(jnp.sum((starts[:, None] >= ends[None, :]).astype(jnp.int32), axis=1),
                               N_EXPERTS - 1).astype(jnp.int32)
    n_used = (ends[-1] // ts).astype(jnp.int32).reshape(1)
    tail = ends[-1] + jnp.arange(N_EXPERTS, dtype=jnp.int32) * ts
    zstart = jnp.concatenate([ends - ts, tail])
    zvalid = jnp.concatenate([padded > 0, tail < n_slots]).astype(jnp.int32)
    zstart = jnp.where(zvalid != 0, zstart, 0).astype(jnp.int32)
    return slot[:t], slot[t:], block_expert, n_used, (zstart, zvalid), n_slots


def kernel(x, c, ctx, c_ctx, ada_w, ada_b, norm1_g, norm2_g, w_in, w_out, ret_decay_logit, ret_gn_g,
           conv_w, na_rpb, ffn_w_in, ffn_w_out, moe_router_w, moe_router_b, moe_w_in, moe_w_out, final_g):
    b, seq, d = x.shape
    ctx_len = ctx.shape[1]
    depth = ada_w.shape[0]
    assert depth == 2, "the final norm is fused into the last (MoE) layer's combine step"
    rows = seq // GRID_W
    t_lat, t_ctx = b * seq, b * ctx_len
    tm_lat = min(1024, seq)
    tm_ctx = ctx_len

    c8 = jnp.zeros((8, d), F32).at[:b].set(c).at[b].set(c_ctx)
    mods = _ada(c8, ada_w, ada_b)
    tables = _rope_tables(seq)

    h = x.reshape(t_lat, d)
    hc = ctx.reshape(t_ctx, d)
    zero_state = jnp.zeros((b, N_PAIRS, LANES, LANES), F32)
    out = None
    for layer in range(depth):
        update_ctx = layer < depth - 1
        m = mods[layer].reshape(8, 6, d)
        lat = [m[:b, i].reshape(b, 1, d) for i in range(6)]
        cx = [jnp.broadcast_to(m[b, i].reshape(1, 1, d), (b, 1, d)) for i in range(6)]
        w_in_bf = w_in[layer].astype(BF16)
        w_out_bf = w_out[layer].astype(BF16)
        lgt = ret_decay_logit[layer].reshape(2, N_PAIRS, LANES // HEAD_DIM)
        lgt = jnp.repeat(lgt, HEAD_DIM, axis=2).transpose(1, 0, 2)
        gn = ret_gn_g[layer].reshape(N_PAIRS, 1, LANES)

        proj_c = _in_proj(hc, norm1_g[layer], cx[0], cx[1], w_in_bf, None, ctx_len, tm_ctx)
        proj_c = proj_c.reshape(b, ctx_len, D_IN_PROJ)
        if layer % 2 == 0:
            proj, fw_in, fw_out = _in_proj(h, norm1_g[layer], lat[0], lat[1], w_in_bf, tables, seq, tm_lat,
                                           casts=(ffn_w_in[layer // 2], ffn_w_out[layer // 2]))
        else:
            proj = _in_proj(h, norm1_g[layer], lat[0], lat[1], w_in_bf, tables, seq, tm_lat)
        proj = proj.reshape(b, seq, D_IN_PROJ)

        y_ret_c, s_fwd, s_bwd = _retention(proj_c, lgt, gn, zero_state, zero_state)
        y_ret, _, _ = _retention(proj, lgt, gn, s_fwd, s_bwd)
        y_conv = _short_conv(proj, conv_w[layer])
        mw = moe_w_in[(layer + 1) // 2] if layer % 2 == 0 else moe_w_out[layer // 2]
        y_na, mw_bf = _na(proj, proj_c, _na_bias(na_rpb[layer], rows), cast=mw.reshape(-1, mw.shape[-1]))
        if layer % 2 == 0:
            mw_in_bf = mw_bf.reshape(mw.shape)
        else:
            mw_out_bf = mw_bf.reshape(mw.shape)

        if layer % 2 == 0:
            h = _outproj_ffn(y_ret.reshape(t_lat, D_RET), y_conv.reshape(t_lat, D_CONV),
                             y_na.reshape(t_lat, D_NA), h, w_out_bf, lat[2], norm2_g[layer], lat[3], lat[4],
                             fw_in, fw_out, lat[5], seq, min(1024, seq), FFN_CHUNKS)
            if update_ctx:
                y_conv_c = _short_conv(proj_c, conv_w[layer])
                y_na_c = _ctx_attn(proj_c)
                hc = _outproj_ffn(y_ret_c.reshape(t_ctx, D_RET), y_conv_c.reshape(t_ctx, D_CONV),
                                  y_na_c.reshape(t_ctx, D_NA), hc, w_out_bf, cx[2], norm2_g[layer], cx[3], cx[4],
                                  fw_in, fw_out, cx[5], ctx_len, tm_ctx, FFN_CHUNKS)
        else:
            e = layer // 2
            h, xn, idx, wts = _out_proj_route(y_ret.reshape(t_lat, D_RET), y_conv.reshape(t_lat, D_CONV),
                                              y_na.reshape(t_lat, D_NA), h, w_out_bf, lat[2], norm2_g[layer],
                                              lat[3], lat[4], seq, tm_lat, moe_router_w[e], moe_router_b[e])
            pos0, pos1, block_expert, n_used, zstart, n_slots = _route_plan(idx)
            xs = _dispatch(xn, pos0, pos1, zstart, n_slots, min(2048, seq))
            ys = _experts(xs, block_expert, n_used, mw_in_bf, mw_out_bf, FFN_CHUNKS)
            out = _combine(pos0, pos1, ys, h, wts[0].reshape(t_lat, 1), wts[1].reshape(t_lat, 1),
                           lat[5], final_g, seq, min(1024, seq))
            h = out
    return out.reshape(b, seq, d)
```

```python
import functools

import numpy as np
import jax
import jax.numpy as jnp
from jax import lax
from jax.experimental import pallas as pl
from jax.experimental.pallas import tpu as pltpu

F32 = jnp.float32
BF16 = jnp.bfloat16

LANES = 128
BF16_ROWS = 16
HEAD_DIM = 64
GRID_W = 64
N_CONV_GROUPS = 4
N_RET_HEADS = 6
N_NA_HEADS = 6
D_RET = N_RET_HEADS * HEAD_DIM
D_CONV = N_CONV_GROUPS * HEAD_DIM
D_NA = N_NA_HEADS * HEAD_DIM
D_IN_PROJ = 4 * D_RET + 3 * D_CONV + 3 * D_NA
N_PAIRS = D_RET // LANES
NA_ROWS = 8
NA_COLS = 16
N_EXPERTS = 8
ROPE_BASE = 10000.0
NORM_EPS = 1e-6
NEG_BIG = -1e30
LOG2E = 1.4426950408889634

RET_CHUNK = 256
NA_QROWS = 4
NA_KROWS = 12
NA_STEP_BLOCKS = 2
MOE_TS = 512
DMA_UNROLL = 8
COMBINE_GROUP = 32
FFN_CHUNKS = 2
IN_PROJ_SUB = 512
OUT_PROJ_SUB = 512
MXU_COLS = 256
IN_PROJ_CHUNK = 3 * MXU_COLS
VMEM_LIMIT = 56 * 1024 * 1024

CB_RQ, CB_RK, CB_RV, CB_RG = 0, 3, 6, 9
CB_CB, CB_CC, CB_CX = 12, 14, 16
CB_NQ, CB_NK, CB_NV = 18, 21, 24


def _cp(sem, vmem=VMEM_LIMIT):
    return pltpu.CompilerParams(dimension_semantics=sem, vmem_limit_bytes=vmem)


def _silu(x):
    return x * (1.0 / (1.0 + jnp.exp(-x)))


def _dot(a, b):
    return jnp.dot(a, b, preferred_element_type=F32)


def _dot_nt(a, b):
    return lax.dot_general(a, b, (((1,), (1,)), ((), ())), preferred_element_type=F32)


def _dot_tn(a, b):
    return lax.dot_general(a, b, (((0,), (0,)), ((), ())), preferred_element_type=F32)


def _split_bf16(x):
    hi = x.astype(BF16)
    lo = (x - hi.astype(F32)).astype(BF16)
    return hi, lo


def _ada_kernel(c_ref, w_ref, b_ref, o_ref):
    x = _silu(c_ref[...]).astype(BF16)
    o_ref[0] = _dot(x, w_ref[0].astype(BF16)) + b_ref[0]


def _ada(c8, ada_w, ada_b):
    depth, d, n = ada_w.shape
    tn = n // 4
    return pl.pallas_call(
        _ada_kernel,
        grid=(depth, n // tn),
        in_specs=[pl.BlockSpec((8, d), lambda l, j: (0, 0)),
                  pl.BlockSpec((1, d, tn), lambda l, j: (l, 0, j)),
                  pl.BlockSpec((1, 1, tn), lambda l, j: (l, 0, j))],
        out_specs=pl.BlockSpec((1, 8, tn), lambda l, j: (l, 0, j)),
        out_shape=jax.ShapeDtypeStruct((depth, 8, n), F32),
        compiler_params=_cp(("parallel", "parallel")),
        name="ada_mod",
    )(c8, ada_w, ada_b.reshape(depth, 1, n))


def _norm_mod(x, g, sh, sc):
    ms = jnp.mean(x * x, axis=-1, keepdims=True)
    y = x * lax.rsqrt(ms + NORM_EPS) * g
    return y * (1.0 + sc) + sh


def _inproj_kernel(*refs, rope, n_cast):
    n_in = (8 if rope else 5) + n_cast
    for c_ref, cb_ref in zip(refs[n_in - n_cast:n_in], refs[n_in + 1:]):
        cb_ref[...] = c_ref[...].astype(BF16)
    if rope:
        h_ref, g_ref, sh_ref, sc_ref, w_ref, cos_ref, sa_ref, sb_ref = refs[:8]
    else:
        h_ref, g_ref, sh_ref, sc_ref, w_ref = refs[:5]
    o_ref = refs[n_in]
    tm = h_ref.shape[0]
    sub = min(IN_PROJ_SUB, tm)
    cw = IN_PROJ_CHUNK
    for r in range(tm // sub):
        rs = slice(r * sub, (r + 1) * sub)
        xn = _norm_mod(h_ref[rs, :], g_ref[...], sh_ref[0], sc_ref[0]).astype(BF16)
        for c0 in range(0, D_IN_PROJ, cw):
            c1 = min(c0 + cw, D_IN_PROJ)
            acc = _dot(xn, w_ref[:, c0:c1])
            for j in range((c1 - c0) // LANES):
                blk = acc[:, j * LANES:(j + 1) * LANES]
                cb = c0 // LANES + j
                if rope and cb < CB_RV:
                    blk = (blk * cos_ref[rs, :] + pltpu.roll(blk, 16, 1) * sa_ref[rs, :]
                           + pltpu.roll(blk, LANES - 16, 1) * sb_ref[rs, :])
                if CB_RK <= cb < CB_RV:
                    blk = blk * (HEAD_DIM ** -0.5)
                elif CB_NQ <= cb < CB_NK:
                    blk = blk * (HEAD_DIM ** -0.5 * (LOG2E if rope else 1.0))
                o_ref[rs, cb * LANES:(cb + 1) * LANES] = blk.astype(BF16)


def _in_proj(h2, g, sh, sc, w_bf, tables, seq, tm, casts=()):
    t, d = h2.shape
    tiles_per_seq = seq // tm
    steps = t // tm
    rope = tables is not None
    in_specs = [pl.BlockSpec((tm, d), lambda i: (i, 0)),
                pl.BlockSpec((1, d), lambda i: (0, 0)),
                pl.BlockSpec((1, 1, d), lambda i: (i // tiles_per_seq, 0, 0)),
                pl.BlockSpec((1, 1, d), lambda i: (i // tiles_per_seq, 0, 0)),
                pl.BlockSpec((d, D_IN_PROJ), lambda i: (0, 0))]
    args = [h2, g.reshape(1, d), sh, sc, w_bf]
    if rope:
        in_specs += [pl.BlockSpec((tm, LANES), lambda i: (i % tiles_per_seq, 0))] * 3
        args += list(tables)
    out_specs = [pl.BlockSpec((tm, D_IN_PROJ), lambda i: (i, 0))]
    out_shape = [jax.ShapeDtypeStruct((t, D_IN_PROJ), BF16)]
    for c in casts:
        cr, cc = c.shape
        assert cr % (steps * BF16_ROWS) == 0
        blk = pl.BlockSpec((cr // steps, cc), lambda i: (i, 0))
        in_specs.append(blk)
        out_specs.append(blk)
        out_shape.append(jax.ShapeDtypeStruct(c.shape, BF16))
        args.append(c)
    res = pl.pallas_call(
        functools.partial(_inproj_kernel, rope=rope, n_cast=len(casts)),
        grid=(steps,),
        in_specs=in_specs,
        out_specs=out_specs,
        out_shape=out_shape,
        compiler_params=_cp(("parallel",)),
        name="in_proj_rope" if rope else "in_proj_ctx",
    )(*args)
    return res if casts else res[0]


def _rope_tables(seq):
    t = np.arange(seq)
    row = (t // GRID_W).astype(np.float32)
    col = (t % GRID_W).astype(np.float32)
    n_freq = HEAD_DIM // 4
    inv_freq = (ROPE_BASE ** (-np.arange(n_freq, dtype=np.float32) / n_freq)).astype(np.float32)
    ang_r = row[:, None] * inv_freq
    ang_c = col[:, None] * inv_freq
    cos_h = np.concatenate([np.cos(ang_r), np.cos(ang_r), np.cos(ang_c), np.cos(ang_c)], axis=1)
    sin_h = np.concatenate([np.sin(ang_r), np.sin(ang_r), np.sin(ang_c), np.sin(ang_c)], axis=1)
    lane = np.arange(HEAD_DIM)
    second = (lane % 32) >= 16
    sa = np.where(second[None, :], sin_h, 0.0)
    sb = np.where(second[None, :], 0.0, -sin_h)
    tile2 = lambda a: jnp.asarray(np.concatenate([a, a], axis=1), F32)
    return tile2(cos_h), tile2(sa), tile2(sb)


def _ret_kernel(q_ref, k_ref, v_ref, g_ref, lgt_ref, gn_ref, sf0_ref, sb0_ref,
                y_ref, sfo_ref, sbo_ref, sfs_ref, sbs_ref, *, seq):
    c = RET_CHUNK
    n_chunks = seq // c
    lg = jax.nn.log_sigmoid(lgt_ref[0])
    lgf, lgb = lg[0:1, :], lg[1:2, :]
    pos = lax.broadcasted_iota(jnp.int32, (c, 1), 0).astype(F32)
    dkf = jnp.exp(lgf * (c - 1.0 - pos))
    dkb = jnp.exp(lgb * pos)
    dqf = jnp.exp(lgf * (pos + 1.0))
    dqb = jnp.exp(lgb * (c - pos))
    cdf = jnp.exp(lgf * float(c))
    cdb = jnp.exp(lgb * float(c))
    lane = lax.broadcasted_iota(jnp.int32, (1, LANES), 1)
    first = lane < HEAD_DIM
    ri = lax.broadcasted_iota(jnp.int32, (LANES, LANES), 0)
    ci = lax.broadcasted_iota(jnp.int32, (LANES, LANES), 1)
    same = (ri < HEAD_DIM) == (ci < HEAD_DIM)
    bd = same.astype(F32)

    def head_mean(x):
        a = jnp.sum(jnp.where(first, x, 0.0), axis=-1, keepdims=True)
        b = jnp.sum(jnp.where(first, 0.0, x), axis=-1, keepdims=True)
        return jnp.where(first, a, b) * (1.0 / HEAD_DIM)

    ii = lax.broadcasted_iota(jnp.int32, (c, c), 0)
    jj = lax.broadcasted_iota(jnp.int32, (c, c), 1)
    dif = (ii - jj).astype(F32)

    def decay_mask(h0):
        lf = lgf[:, h0:h0 + 1]
        lb = lgb[:, h0:h0 + 1]
        return jnp.where(dif > 0, jnp.exp(lf * jnp.maximum(dif, 0.0)),
                         jnp.where(dif < 0, jnp.exp(lb * jnp.maximum(-dif, 0.0)), 2.0))

    dm = (decay_mask(0), decay_mask(HEAD_DIM))
    mfirst = first.astype(BF16)
    msecond = (1.0 - first.astype(F32)).astype(BF16)

    unroll = min(16, n_chunks)

    def incr(n, carry):
        sl = pl.ds(pl.multiple_of(n * c, c), c)
        kf = k_ref[0, sl, :].astype(F32)
        v = v_ref[0, sl, :]
        sfs_ref[n] = _dot_tn((kf * dkf).astype(BF16), v) * bd
        sbs_ref[n] = _dot_tn((kf * dkb).astype(BF16), v) * bd
        return carry

    lax.fori_loop(0, n_chunks, incr, 0, unroll=unroll)

    def scan_f(n, s):
        u = sfs_ref[n]
        sfs_ref[n] = s
        return cdf * s + u

    def scan_b(i, s):
        n = n_chunks - 1 - i
        u = sbs_ref[n]
        sbs_ref[n] = s
        return cdb * s + u

    sfo_ref[0, 0] = lax.fori_loop(0, n_chunks, scan_f, sf0_ref[0, 0])
    sbo_ref[0, 0] = lax.fori_loop(0, n_chunks, scan_b, sb0_ref[0, 0])

    def outp(n, carry):
        sl = pl.ds(pl.multiple_of(n * c, c), c)
        q = q_ref[0, sl, :]
        k = k_ref[0, sl, :]
        v = v_ref[0, sl, :]
        qf = q.astype(F32)
        o = None
        for hh, mk in enumerate((mfirst, msecond)):
            s = _dot_nt(q * mk, k) * dm[hh]
            oh = _dot(s.astype(BF16), v)
            o = oh if o is None else jnp.where(first, o, oh)
        qcat = jnp.concatenate([(qf * dqf).astype(BF16), (qf * dqb).astype(BF16)], axis=1)
        scat = jnp.concatenate([sfs_ref[n], sbs_ref[n]], axis=0).astype(BF16)
        o = o + _dot(qcat, scat)
        mu = head_mean(o)
        dlt = o - mu
        var = head_mean(dlt * dlt)
        on = dlt * lax.rsqrt(var + NORM_EPS) * gn_ref[0]
        y_ref[0, sl, :] = (_silu(g_ref[0, sl, :].astype(F32)) * on).astype(BF16)
        return carry

    lax.fori_loop(0, n_chunks, outp, 0, unroll=unroll)


def _retention(proj, lgt, gn, sf0, sb0):
    b, seq, _ = proj.shape
    col = lambda off: pl.BlockSpec((1, seq, LANES), lambda bi, p: (bi, 0, off + p))
    st = pl.BlockSpec((1, 1, LANES, LANES), lambda bi, p: (bi, p, 0, 0))
    return pl.pallas_call(
        functools.partial(_ret_kernel, seq=seq),
        grid=(b, N_PAIRS),
        in_specs=[col(CB_RQ), col(CB_RK), col(CB_RV), col(CB_RG),
                  pl.BlockSpec((1, 2, LANES), lambda bi, p: (p, 0, 0)),
                  pl.BlockSpec((1, 1, LANES), lambda bi, p: (p, 0, 0)),
                  st, st],
        out_specs=[pl.BlockSpec((1, seq, LANES), lambda bi, p: (bi, 0, p)), st, st],
        out_shape=[jax.ShapeDtypeStruct((b, seq, D_RET), BF16),
                   jax.ShapeDtypeStruct((b, N_PAIRS, LANES, LANES), F32),
                   jax.ShapeDtypeStruct((b, N_PAIRS, LANES, LANES), F32)],
        scratch_shapes=[pltpu.VMEM((seq // RET_CHUNK, LANES, LANES), F32),
                        pltpu.VMEM((seq // RET_CHUNK, LANES, LANES), F32)],
        compiler_params=_cp(("parallel", "parallel")),
        name="retention_s%d" % seq,
    )(proj, proj, proj, proj, lgt, gn, sf0, sb0)


def _conv_kernel(b_ref, c_ref, x_ref, w_ref, y_ref, *, seq):
    u = c_ref[0].astype(F32) * x_ref[0].astype(F32)
    row = lax.broadcasted_iota(jnp.int32, (seq, 1), 0)
    prev = jnp.where(row == 0, 0.0, pltpu.roll(u, 1, 0))
    nxt = jnp.where(row == seq - 1, 0.0, pltpu.roll(u, seq - 1, 0))
    y = w_ref[0:1, :] * prev + w_ref[1:2, :] * u + w_ref[2:3, :] * nxt
    y_ref[0] = (b_ref[0].astype(F32) * y).astype(BF16)


def _short_conv(proj, conv_w):
    b, seq, _ = proj.shape
    nblk = D_CONV // LANES
    col = lambda off: pl.BlockSpec((1, seq, LANES), lambda bi, j: (bi, 0, off + j))
    return pl.pallas_call(
        functools.partial(_conv_kernel, seq=seq),
        grid=(b, nblk),
        in_specs=[col(CB_CB), col(CB_CC), col(CB_CX),
                  pl.BlockSpec((3, LANES), lambda bi, j: (0, j))],
        out_specs=pl.BlockSpec((1, seq, LANES), lambda bi, j: (bi, 0, j)),
        out_shape=jax.ShapeDtypeStruct((b, seq, D_CONV), BF16),
        compiler_params=_cp(("parallel", "parallel")),
        name="short_conv_s%d" % seq,
    )(proj, proj, proj, conv_w)


def _na_bias_plan(rows):
    plan = np.full((3, NA_QROWS, NA_KROWS), -1, np.int64)
    for cl, rb in enumerate((0, NA_QROWS, rows - NA_QROWS)):
        ws = int(np.clip(rb - NA_ROWS // 2, 0, rows - NA_KROWS))
        for i in range(NA_QROWS):
            r = rb + i
            r0 = int(np.clip(r - NA_ROWS // 2, 0, rows - NA_ROWS))
            for j in range(NA_KROWS):
                kr = ws + j
                if r0 <= kr < r0 + NA_ROWS:
                    plan[cl, i, j] = kr - r + (NA_ROWS - 1)
    return plan


def _na_bias_kernel(rpb_ref, o_ref, *, plan):
    w = GRID_W
    ndr, ndc = 2 * NA_ROWS - 1, 2 * NA_COLS - 1
    base = pl.program_id(0) * (ndr * ndc)
    c = lax.broadcasted_iota(jnp.int32, (w, w), 0)
    kc = lax.broadcasted_iota(jnp.int32, (w, w), 1)
    dcidx = jnp.clip(kc - c, -(NA_COLS - 1), NA_COLS - 1) + (NA_COLS - 1)
    c0 = jnp.clip(c - NA_COLS // 2, 0, w - NA_COLS)
    col_in = jnp.logical_and(kc >= c0, kc < c0 + NA_COLS)
    neg = jnp.full((w, w), NEG_BIG, F32)
    tiles = []
    for dr in range(ndr):
        t = neg
        for dcv in range(ndc):
            t = jnp.where(dcidx == dcv, rpb_ref[base + dr * ndc + dcv], t)
        tiles.append(jnp.where(col_in, t * LOG2E, NEG_BIG))
    for cl in range(plan.shape[0]):
        for i in range(plan.shape[1]):
            for j in range(0, plan.shape[2], 2):
                pair = [tiles[int(d)] if d >= 0 else neg for d in plan[cl, i, j:j + 2]]
                o_ref[0, cl, i * w:(i + 1) * w, j * w:(j + 2) * w] = jnp.concatenate(pair, axis=1)


def _na_bias(rpb, rows):
    h = rpb.shape[0]
    tq, tk = NA_QROWS * GRID_W, NA_KROWS * GRID_W
    return pl.pallas_call(
        functools.partial(_na_bias_kernel, plan=_na_bias_plan(rows)),
        grid=(h,),
        in_specs=[pl.BlockSpec(memory_space=pltpu.SMEM)],
        out_specs=pl.BlockSpec((1, 3, tq, tk), lambda hh: (hh, 0, 0, 0)),
        out_shape=jax.ShapeDtypeStruct((h, 3, tq, tk), F32),
        compiler_params=_cp(("parallel",)),
        name="na_bias_expand",
    )(rpb.reshape(-1))


def _na_kernel(q_ref, k_ref, v_ref, kc_ref, vc_ref, bias_ref, c_ref, o_ref, cb_ref, v1_ref, vc1_ref, *, rows):
    cb_ref[...] = c_ref[...].astype(BF16)
    step = pl.program_id(1)
    nqb = rows // NA_QROWS
    tq = NA_QROWS * GRID_W
    lane = lax.broadcasted_iota(jnp.int32, (1, LANES), 1)
    first = lane < HEAD_DIM

    @pl.when(step == 0)
    def _():
        one = jnp.ones((1, LANES), BF16)
        for p in range(N_PAIRS):
            cols = slice(p * LANES, (p + 1) * LANES)
            v1_ref[2 * p] = jnp.where(first, v_ref[0, :, cols], one)
            v1_ref[2 * p + 1] = jnp.where(first, one, v_ref[0, :, cols])
            vc1_ref[2 * p] = jnp.where(first, vc_ref[0, :, cols], one)
            vc1_ref[2 * p + 1] = jnp.where(first, one, vc_ref[0, :, cols])

    for sb in range(NA_STEP_BLOCKS):
        qb = step * NA_STEP_BLOCKS + sb
        cls = jnp.where(qb == 0, 0, jnp.where(qb == nqb - 1, 2, 1))
        ws = jnp.clip(qb * NA_QROWS - NA_ROWS // 2, 0, rows - NA_KROWS)
        sl = pl.ds(pl.multiple_of(ws * GRID_W, GRID_W), NA_KROWS * GRID_W)
        qrows = slice(sb * tq, (sb + 1) * tq)
        for p in range(N_PAIRS):
            cols = slice(p * LANES, (p + 1) * LANES)
            q = q_ref[0, qrows, cols]
            kw = k_ref[0, sl, cols]
            kc = kc_ref[0, :, cols]
            pv = []
            for hh in range(2):
                mk = (first if hh == 0 else jnp.logical_not(first)).astype(BF16)
                qh = q * mk
                s_loc = _dot_nt(qh, kw) + bias_ref[2 * p + hh, cls]
                s_ctx = _dot_nt(qh, kc)
                m = jnp.maximum(jnp.max(s_loc, axis=-1, keepdims=True), jnp.max(s_ctx, axis=-1, keepdims=True))
                p_loc = jnp.exp2(s_loc - m).astype(BF16)
                p_ctx = jnp.exp2(s_ctx - m).astype(BF16)
                pv.append(_dot(p_loc, v1_ref[2 * p + hh, sl, :]) + _dot(p_ctx, vc1_ref[2 * p + hh]))
            num = jnp.where(first, pv[0], pv[1])
            den = jnp.where(first, pltpu.roll(pv[0], HEAD_DIM, 1), pltpu.roll(pv[1], HEAD_DIM, 1))
            o_ref[0, qrows, cols] = (num / den).astype(BF16)


def _na(proj, proj_ctx, bias, cast):
    b, seq, _ = proj.shape
    ctx_len = proj_ctx.shape[1]
    rows = seq // GRID_W
    nqb = rows // NA_QROWS
    tq = NA_STEP_BLOCKS * NA_QROWS * GRID_W
    tk = NA_KROWS * GRID_W
    nh = 2 * N_PAIRS
    n_steps = nqb // NA_STEP_BLOCKS
    assert n_steps * NA_STEP_BLOCKS == nqb
    cq, ck, cv = (CB_NQ * LANES // D_NA, CB_NK * LANES // D_NA, CB_NV * LANES // D_NA)
    in_specs = [pl.BlockSpec((1, tq, D_NA), lambda bi, st: (bi, st, cq)),
                pl.BlockSpec((1, seq, D_NA), lambda bi, st: (bi, 0, ck)),
                pl.BlockSpec((1, seq, D_NA), lambda bi, st: (bi, 0, cv)),
                pl.BlockSpec((1, ctx_len, D_NA), lambda bi, st: (bi, 0, ck)),
                pl.BlockSpec((1, ctx_len, D_NA), lambda bi, st: (bi, 0, cv)),
                pl.BlockSpec(bias.shape, lambda bi, st: (0, 0, 0, 0), pipeline_mode=pl.Buffered(1))]
    out_specs = [pl.BlockSpec((1, tq, D_NA), lambda bi, st: (bi, st, 0))]
    out_shape = [jax.ShapeDtypeStruct((b, seq, D_NA), BF16)]
    cr, cc = cast.shape
    steps = b * n_steps
    assert cr % (steps * BF16_ROWS) == 0
    blk = pl.BlockSpec((cr // steps, cc), lambda bi, st: (bi * n_steps + st, 0))
    in_specs.append(blk)
    out_specs.append(blk)
    out_shape.append(jax.ShapeDtypeStruct(cast.shape, BF16))
    return pl.pallas_call(
        functools.partial(_na_kernel, rows=rows),
        grid=(b, n_steps),
        in_specs=in_specs,
        out_specs=out_specs,
        out_shape=out_shape,
        scratch_shapes=[pltpu.VMEM((nh, seq, LANES), BF16), pltpu.VMEM((nh, ctx_len, LANES), BF16)],
        compiler_params=_cp(("parallel", "arbitrary")),
        name="na_attn",
    )(proj, proj, proj, proj_ctx, proj_ctx, bias, cast)


def _ctx_attn_kernel(q_ref, k_ref, v_ref, o_ref):
    q = q_ref[0]
    k = k_ref[0]
    v = v_ref[0]
    lane = lax.broadcasted_iota(jnp.int32, (1, LANES), 1)
    first = lane < HEAD_DIM
    o = None
    for hh in range(2):
        mk = (first if hh == 0 else jnp.logical_not(first)).astype(BF16)
        s = _dot_nt(q * mk, k)
        m = jnp.max(s, axis=-1, keepdims=True)
        p = jnp.exp(s - m)
        l = jnp.sum(p, axis=-1, keepdims=True)
        oh = _dot(p.astype(BF16), v) / l
        o = oh if o is None else jnp.where(first, o, oh)
    o_ref[0] = o.astype(BF16)


def _ctx_attn(proj_ctx):
    b, ctx_len, _ = proj_ctx.shape
    col = lambda off: pl.BlockSpec((1, ctx_len, LANES), lambda bi, p: (bi, 0, off + p))
    return pl.pallas_call(
        _ctx_attn_kernel,
        grid=(b, N_PAIRS),
        in_specs=[col(CB_NQ), col(CB_NK), col(CB_NV)],
        out_specs=pl.BlockSpec((1, ctx_len, LANES), lambda bi, p: (bi, 0, p)),
        out_shape=jax.ShapeDtypeStruct((b, ctx_len, D_NA), BF16),
        compiler_params=_cp(("parallel", "parallel")),
        name="ctx_attn",
    )(proj_ctx, proj_ctx, proj_ctx)


def _outproj_route_kernel(yr_ref, yc_ref, yn_ref, h_ref, w_ref, g1_ref, n2_ref, sh_ref, sc_ref, rw_ref, rb_ref,
                          ho_ref, xn_ref, idx_ref, wt_ref):
    tm = h_ref.shape[0]
    sub = min(OUT_PROJ_SUB, tm)
    r_hi, r_lo = _split_bf16(rw_ref[...])
    r_both = jnp.concatenate([r_hi, r_lo], axis=0)
    for r in range(tm // sub):
        rs = slice(r * sub, (r + 1) * sub)
        ycat = jnp.concatenate([yr_ref[rs, :], yc_ref[rs, :], yn_ref[rs, :]], axis=-1)
        h = h_ref[rs, :] + g1_ref[0] * _dot(ycat, w_ref[...])
        ho_ref[rs, :] = h
        a = _norm_mod(h, n2_ref[...], sh_ref[0], sc_ref[0])
        xn_ref[rs, :] = a.astype(xn_ref.dtype)
        a_hi, a_lo = _split_bf16(a)
        t_hi = _dot_nt(r_both, a_hi)
        logits = t_hi[:N_EXPERTS] + t_hi[N_EXPERTS:] + _dot_nt(r_hi, a_lo) + rb_ref[...]
        eidx = lax.broadcasted_iota(jnp.int32, logits.shape, 0)
        m1 = jnp.max(logits, axis=0, keepdims=True)
        i1 = jnp.min(jnp.where(logits == m1, eidx, N_EXPERTS), axis=0, keepdims=True)
        rest = jnp.where(eidx == i1, -jnp.inf, logits)
        m2 = jnp.max(rest, axis=0, keepdims=True)
        i2 = jnp.min(jnp.where(rest == m2, eidx, N_EXPERTS), axis=0, keepdims=True)
        e2 = jnp.exp(m2 - m1)
        den = 1.0 + e2
        idx_ref[:, rs] = jnp.concatenate([i1, i2], axis=0)
        wt_ref[:, rs] = jnp.concatenate([1.0 / den, e2 / den], axis=0)


def _out_proj_route(yr, yc, yn, h2, w_bf, g1, n2g, sh2, sc2, seq, tm, router_w, router_b):
    t, d = h2.shape
    tps = seq // tm
    row = lambda wdt: pl.BlockSpec((tm, wdt), lambda i: (i, 0))
    mod = pl.BlockSpec((1, 1, d), lambda i: (i // tps, 0, 0))
    return pl.pallas_call(
        _outproj_route_kernel,
        grid=(t // tm,),
        in_specs=[row(D_RET), row(D_CONV), row(D_NA), row(d),
                  pl.BlockSpec((d, d), lambda i: (0, 0)), mod,
                  pl.BlockSpec((1, d), lambda i: (0, 0)), mod, mod,
                  pl.BlockSpec((N_EXPERTS, d), lambda i: (0, 0)),
                  pl.BlockSpec((N_EXPERTS, 1), lambda i: (0, 0))],
        out_specs=[row(d), row(d), pl.BlockSpec((2, tm), lambda i: (0, i)), pl.BlockSpec((2, tm), lambda i: (0, i))],
        out_shape=[jax.ShapeDtypeStruct((t, d), F32), jax.ShapeDtypeStruct((t, d), BF16),
                   jax.ShapeDtypeStruct((2, t), jnp.int32), jax.ShapeDtypeStruct((2, t), F32)],
        compiler_params=_cp(("parallel",)),
        name="out_proj_route",
    )(yr, yc, yn, h2, w_bf, g1, n2g.reshape(1, d), sh2, sc2, router_w.T, router_b.reshape(N_EXPERTS, 1))


def _swiglu_chunks(x, w_in_ref, w_out_ref, ff, n_chunks):
    tf = -(-ff // (n_chunks * MXU_COLS)) * MXU_COLS
    y = None
    for c0 in range(0, ff, tf):
        c1 = min(c0 + tf, ff)
        gate = _dot(x, w_in_ref[:, c0:c1])
        up = _dot(x, w_in_ref[:, ff + c0:ff + c1])
        part = _dot((_silu(gate) * up).astype(BF16), w_out_ref[c0:c1, :])
        y = part if y is None else y + part
    return y


def _outproj_ffn_kernel(yr_ref, yc_ref, yn_ref, h_ref, wo_ref, g1_ref, n2_ref, sh_ref, sc_ref,
                        wi_ref, wf_ref, g2_ref, o_ref, *, ff, n_chunks):
    ycat = jnp.concatenate([yr_ref[...], yc_ref[...], yn_ref[...]], axis=-1)
    h1 = h_ref[...] + g1_ref[0] * _dot(ycat, wo_ref[...])
    a = _norm_mod(h1, n2_ref[...], sh_ref[0], sc_ref[0]).astype(BF16)
    o_ref[...] = h1 + g2_ref[0] * _swiglu_chunks(a, wi_ref, wf_ref, ff, n_chunks)


def _outproj_ffn(yr, yc, yn, h2, wo_bf, g1, n2g, sh2, sc2, fw_in_bf, fw_out_bf, g2, seq, tm, n_chunks):
    t, d = h2.shape
    ff = fw_out_bf.shape[0]
    tps = seq // tm
    resident = pl.Buffered(1)
    row = lambda wdt: pl.BlockSpec((tm, wdt), lambda i: (i, 0))
    mod = pl.BlockSpec((1, 1, d), lambda i: (i // tps, 0, 0))
    return pl.pallas_call(
        functools.partial(_outproj_ffn_kernel, ff=ff, n_chunks=n_chunks),
        grid=(t // tm,),
        in_specs=[row(D_RET), row(D_CONV), row(D_NA), row(d),
                  pl.BlockSpec((d, d), lambda i: (0, 0), pipeline_mode=resident), mod,
                  pl.BlockSpec((1, d), lambda i: (0, 0)), mod, mod,
                  pl.BlockSpec((d, 2 * ff), lambda i: (0, 0), pipeline_mode=resident),
                  pl.BlockSpec((ff, d), lambda i: (0, 0), pipeline_mode=resident), mod],
        out_specs=row(d),
        out_shape=jax.ShapeDtypeStruct((t, d), F32),
        compiler_params=_cp(("parallel",)),
        name="out_proj_ffn_t%d" % t,
    )(yr, yc, yn, h2, wo_bf, g1, n2g.reshape(1, d), sh2, sc2, fw_in_bf, fw_out_bf, g2)


def _dispatch_kernel(s0_ref, s1_ref, zs_ref, zv_ref, x_ref, xs_ref, slab_ref, zbuf_ref, sem, *, rows):
    @pl.when(pl.program_id(0) == 0)
    def _():
        zbuf_ref[...] = jnp.zeros_like(zbuf_ref)
        for e in range(2 * N_EXPERTS):
            @pl.when(zv_ref[e] != 0)
            def _():
                dst = xs_ref.at[pl.ds(pl.multiple_of(zs_ref[e], MOE_TS), MOE_TS)]
                pltpu.make_async_copy(zbuf_ref, dst, sem.at[1]).start()
        for e in range(2 * N_EXPERTS):
            @pl.when(zv_ref[e] != 0)
            def _():
                pltpu.make_async_copy(zbuf_ref, xs_ref.at[pl.ds(0, MOE_TS)], sem.at[1]).wait()

    nk = slab_ref.shape[1]

    def issue(g, carry):
        rs = pl.ds(pl.multiple_of(g * DMA_UNROLL, DMA_UNROLL), DMA_UNROLL)
        for k in range(nk):
            slab_ref[rs, k, :] = x_ref[rs, k * LANES:(k + 1) * LANES].astype(slab_ref.dtype)
        for u in range(DMA_UNROLL):
            r = g * DMA_UNROLL + u
            src = slab_ref.at[r]
            pltpu.make_async_copy(src, xs_ref.at[s0_ref[r]], sem.at[0]).start(priority=u % 2)
            pltpu.make_async_copy(src, xs_ref.at[s1_ref[r]], sem.at[0]).start(priority=(u + 1) % 2)
        return carry

    lax.fori_loop(0, rows // DMA_UNROLL, issue, 0)
    for _ in range(2):
        pltpu.make_async_copy(slab_ref, xs_ref.at[pl.ds(0, rows)], sem.at[0]).wait()


def _dispatch(xn, slot0, slot1, zplan, n_slots, rows):
    t, d = xn.shape
    slab = (d // LANES, LANES)
    zstart, zvalid = zplan
    return pl.pallas_call(
        functools.partial(_dispatch_kernel, rows=rows),
        grid=(t // rows,),
        in_specs=[pl.BlockSpec((rows,), lambda i: (i,), memory_space=pltpu.SMEM),
                  pl.BlockSpec((rows,), lambda i: (i,), memory_space=pltpu.SMEM),
                  pl.BlockSpec((2 * N_EXPERTS,), lambda i: (0,), memory_space=pltpu.SMEM),
                  pl.BlockSpec((2 * N_EXPERTS,), lambda i: (0,), memory_space=pltpu.SMEM),
                  pl.BlockSpec((rows, d), lambda i: (i, 0))],
        out_specs=pl.BlockSpec(memory_space=pl.ANY),
        out_shape=jax.ShapeDtypeStruct((n_slots,) + slab, F32),
        scratch_shapes=[pltpu.VMEM((rows,) + slab, F32), pltpu.VMEM((MOE_TS,) + slab, F32),
                        pltpu.SemaphoreType.DMA((2,))],
        compiler_params=_cp(("arbitrary",)),
        name="moe_dispatch",
    )(slot0, slot1, zstart, zvalid, xn)


def _expert_kernel(be_ref, nu_ref, xs_ref, wi_ref, wo_ref, o_ref, xa_ref, xb_ref, sem, *, ff, n_chunks):
    j = pl.program_id(0)
    n = pl.num_programs(0)
    ts, d = xa_ref.shape
    bufs = (xa_ref, xb_ref)

    def copies(blk, b):
        rows = pl.ds(pl.multiple_of(blk * ts, ts), ts)
        return [pltpu.make_async_copy(xs_ref.at[rows, k, :], bufs[b].at[:, k * LANES:(k + 1) * LANES],
                                      sem.at[b]) for k in range(d // LANES)]

    @pl.when(j == 0)
    def _():
        for c in copies(0, 0):
            c.start()

    def step(b):
        @pl.when(j + 1 < n)
        def _():
            for c in copies(j + 1, 1 - b):
                c.start()

        for c in copies(j, b):
            c.wait()

        @pl.when(j < nu_ref[0])
        def _():
            o_ref[...] = _swiglu_chunks(bufs[b][...].astype(BF16), wi_ref.at[0], wo_ref.at[0], ff, n_chunks)

        @pl.when(j >= nu_ref[0])
        def _():
            o_ref[...] = jnp.zeros_like(o_ref)

    for b in range(2):
        pl.when(j % 2 == b)(functools.partial(step, b))


def _experts(xs, block_expert, n_used, w_in_bf, w_out_bf, n_chunks):
    n, nk, _ = xs.shape
    d = nk * LANES
    ff = w_out_bf.shape[1]
    ts = MOE_TS
    grid_spec = pltpu.PrefetchScalarGridSpec(
        num_scalar_prefetch=2,
        grid=(n // ts,),
        in_specs=[pl.BlockSpec(memory_space=pl.ANY),
                  pl.BlockSpec((1, d, 2 * ff), lambda j, be, nu: (be[j], 0, 0)),
                  pl.BlockSpec((1, ff, d), lambda j, be, nu: (be[j], 0, 0))],
        out_specs=pl.BlockSpec((ts, d), lambda j, be, nu: (j, 0)),
        scratch_shapes=[pltpu.VMEM((ts, d), xs.dtype), pltpu.VMEM((ts, d), xs.dtype),
                        pltpu.SemaphoreType.DMA((2,))],
    )
    return pl.pallas_call(
        functools.partial(_expert_kernel, ff=ff, n_chunks=n_chunks),
        grid_spec=grid_spec,
        out_shape=jax.ShapeDtypeStruct((n, d), F32),
        compiler_params=_cp(("arbitrary",)),
        name="moe_experts",
    )(block_expert, n_used, xs, w_in_bf, w_out_bf)


def _combine_kernel(p0_ref, p1_ref, q0_ref, q1_ref, ys_ref, h_ref, w0_ref, w1_ref, g2_ref, fg_ref, o_ref,
                    ya0_ref, ya1_ref, yb0_ref, yb1_ref, sem, *, rows):
    i = pl.program_id(0)
    n = pl.num_programs(0)
    n_iter = rows // COMBINE_GROUP
    bufs = ((ya0_ref, ya1_ref), (yb0_ref, yb1_ref))

    def issue_rows(pa_ref, pb_ref, b, g):
        for u in range(COMBINE_GROUP):
            r = g * COMBINE_GROUP + u
            pltpu.make_async_copy(ys_ref.at[pl.ds(pa_ref[r], 1)], bufs[b][0].at[pl.ds(r, 1)],
                                  sem.at[0, b]).start(priority=u % 2)
            pltpu.make_async_copy(ys_ref.at[pl.ds(pb_ref[r], 1)], bufs[b][1].at[pl.ds(r, 1)],
                                  sem.at[1, b]).start(priority=(u + 1) % 2)

    def finish_rows(b, g):
        rs = pl.ds(pl.multiple_of(g * COMBINE_GROUP, COMBINE_GROUP), COMBINE_GROUP)
        y = w0_ref[rs, :] * bufs[b][0][rs, :] + w1_ref[rs, :] * bufs[b][1][rs, :]
        h = h_ref[rs, :] + g2_ref[0] * y
        ms = jnp.mean(h * h, axis=-1, keepdims=True)
        o_ref[rs, :] = h * lax.rsqrt(ms + NORM_EPS) * fg_ref[...]

    @pl.when(i == 0)
    def _():
        def first(g, carry):
            issue_rows(p0_ref, p1_ref, 0, g)
            return carry
        lax.fori_loop(0, n_iter, first, 0)

    def step(b):
        pltpu.make_async_copy(ys_ref.at[pl.ds(0, rows)], bufs[b][0], sem.at[0, b]).wait()
        pltpu.make_async_copy(ys_ref.at[pl.ds(0, rows)], bufs[b][1], sem.at[1, b]).wait()

        @pl.when(i + 1 < n)
        def _():
            def both(g, carry):
                issue_rows(q0_ref, q1_ref, 1 - b, g)
                finish_rows(b, g)
                return carry
            lax.fori_loop(0, n_iter, both, 0)

        @pl.when(i + 1 == n)
        def _():
            def last(g, carry):
                finish_rows(b, g)
                return carry
            lax.fori_loop(0, n_iter, last, 0)

    for b in range(2):
        pl.when(i % 2 == b)(functools.partial(step, b))


def _combine(pos0, pos1, ys, h2, w0, w1, g2, final_g, seq, rows):
    t, d = h2.shape
    tps = seq // rows
    steps = t // rows
    blk = pl.BlockSpec((rows, d), lambda i: (i, 0))
    col = pl.BlockSpec((rows, 1), lambda i: (i, 0))
    here = pl.BlockSpec((rows,), lambda i: (i,), memory_space=pltpu.SMEM)
    ahead = pl.BlockSpec((rows,), lambda i: (jnp.minimum(i + 1, steps - 1),), memory_space=pltpu.SMEM)
    return pl.pallas_call(
        functools.partial(_combine_kernel, rows=rows),
        grid=(steps,),
        in_specs=[here, here, ahead, ahead,
                  pl.BlockSpec(memory_space=pl.ANY),
                  blk, col, col,
                  pl.BlockSpec((1, 1, d), lambda i: (i // tps, 0, 0)),
                  pl.BlockSpec((1, d), lambda i: (0, 0))],
        out_specs=blk,
        out_shape=jax.ShapeDtypeStruct((t, d), F32),
        scratch_shapes=[pltpu.VMEM((rows, d), F32)] * 4 + [pltpu.SemaphoreType.DMA((2, 2))],
        compiler_params=_cp(("arbitrary",)),
        name="moe_combine_norm",
    )(pos0, pos1, pos0, pos1, ys, h2, w0, w1, g2, final_g.reshape(1, d))


def _route_plan(idx):
    t = idx.shape[1]
    ts = MOE_TS
    n_slots = 2 * t + N_EXPERTS * ts
    e_flat = idx.reshape(-1)
    onehot = (e_flat[:, None] == jnp.arange(N_EXPERTS, dtype=jnp.int32)[None, :]).astype(jnp.int32)
    csum = jnp.cumsum(onehot, axis=0)
    rank = jnp.sum((csum - onehot) * onehot, axis=1)
    counts = csum[-1]
    padded = ((counts + ts - 1) // ts) * ts
    ends = jnp.cumsum(padded)
    offs = ends - padded
    slot = (jnp.sum(onehot * offs[None, :], axis=1) + rank).astype(jnp.int32)
    starts = jnp.arange(n_slots // ts, dtype=jnp.int32) * ts
    block_expert = jnp.minimum(jnp.sum((starts[:, None] >= ends[None, :]).astype(jnp.int32), axis=1),
                               N_EXPERTS - 1).astype(jnp.int32)
    n_used = (ends[-1] // ts).astype(jnp.int32).reshape(1)
    tail = ends[-1] + jnp.arange(N_EXPERTS, dtype=jnp.int32) * ts
    zstart = jnp.concatenate([ends - ts, tail])
    zvalid = jnp.concatenate([padded > 0, tail < n_slots]).astype(jnp.int32)
    zstart = jnp.where(zvalid != 0, zstart, 0).astype(jnp.int32)
    return slot[:t], slot[t:], block_expert, n_used, (zstart, zvalid), n_slots


def kernel(x, c, ctx, c_ctx, ada_w, ada_b, norm1_g, norm2_g, w_in, w_out, ret_decay_logit, ret_gn_g,
           conv_w, na_rpb, ffn_w_in, ffn_w_out, moe_router_w, moe_router_b, moe_w_in, moe_w_out, final_g):
    b, seq, d = x.shape
    ctx_len = ctx.shape[1]
    depth = ada_w.shape[0]
    assert depth == 2, "the final norm is fused into the last (MoE) layer's combine step"
    rows = seq // GRID_W
    t_lat, t_ctx = b * seq, b * ctx_len
    tm_lat = min(1024, seq)
    tm_ctx = ctx_len

    c8 = jnp.zeros((8, d), F32).at[:b].set(c).at[b].set(c_ctx)
    mods = _ada(c8, ada_w, ada_b)
    tables = _rope_tables(seq)

    h = x.reshape(t_lat, d)
    hc = ctx.reshape(t_ctx, d)
    zero_state = jnp.zeros((b, N_PAIRS, LANES, LANES), F32)
    out = None
    for layer in range(depth):
        update_ctx = layer < depth - 1
        m = mods[layer].reshape(8, 6, d)
        lat = [m[:b, i].reshape(b, 1, d) for i in range(6)]
        cx = [jnp.broadcast_to(m[b, i].reshape(1, 1, d), (b, 1, d)) for i in range(6)]
        w_in_bf = w_in[layer].astype(BF16)
        w_out_bf = w_out[layer].astype(BF16)
        lgt = ret_decay_logit[layer].reshape(2, N_PAIRS, LANES // HEAD_DIM)
        lgt = jnp.repeat(lgt, HEAD_DIM, axis=2).transpose(1, 0, 2)
        gn = ret_gn_g[layer].reshape(N_PAIRS, 1, LANES)

        proj_c = _in_proj(hc, norm1_g[layer], cx[0], cx[1], w_in_bf, None, ctx_len, tm_ctx)
        proj_c = proj_c.reshape(b, ctx_len, D_IN_PROJ)
        if layer % 2 == 0:
            proj, fw_in, fw_out = _in_proj(h, norm1_g[layer], lat[0], lat[1], w_in_bf, tables, seq, tm_lat,
                                           casts=(ffn_w_in[layer // 2], ffn_w_out[layer // 2]))
        else:
            proj = _in_proj(h, norm1_g[layer], lat[0], lat[1], w_in_bf, tables, seq, tm_lat)
        proj = proj.reshape(b, seq, D_IN_PROJ)

        y_ret_c, s_fwd, s_bwd = _retention(proj_c, lgt, gn, zero_state, zero_state)
        y_ret, _, _ = _retention(proj, lgt, gn, s_fwd, s_bwd)
        y_conv = _short_conv(proj, conv_w[layer])
        mw = moe_w_in[(layer + 1) // 2] if layer % 2 == 0 else moe_w_out[layer // 2]
        y_na, mw_bf = _na(proj, proj_c, _na_bias(na_rpb[layer], rows), cast=mw.reshape(-1, mw.shape[-1]))
        if layer % 2 == 0:
            mw_in_bf = mw_bf.reshape(mw.shape)
        else:
            mw_out_bf = mw_bf.reshape(mw.shape)

        if layer % 2 == 0:
            h = _outproj_ffn(y_ret.reshape(t_lat, D_RET), y_conv.reshape(t_lat, D_CONV),
                             y_na.reshape(t_lat, D_NA), h, w_out_bf, lat[2], norm2_g[layer], lat[3], lat[4],
                             fw_in, fw_out, lat[5], seq, min(1024, seq), FFN_CHUNKS)
            if update_ctx:
                y_conv_c = _short_conv(proj_c, conv_w[layer])
                y_na_c = _ctx_attn(proj_c)
                hc = _outproj_ffn(y_ret_c.reshape(t_ctx, D_RET), y_conv_c.reshape(t_ctx, D_CONV),
                                  y_na_c.reshape(t_ctx, D_NA), hc, w_out_bf, cx[2], norm2_g[layer], cx[3], cx[4],
                                  fw_in, fw_out, cx[5], ctx_len, tm_ctx, FFN_CHUNKS)
        else:
            e = layer // 2
            h, xn, idx, wts = _out_proj_route(y_ret.reshape(t_lat, D_RET), y_conv.reshape(t_lat, D_CONV),
                                              y_na.reshape(t_lat, D_NA), h, w_out_bf, lat[2], norm2_g[layer],
                                              lat[3], lat[4], seq, tm_lat, moe_router_w[e], moe_router_b[e])
            pos0, pos1, block_expert, n_used, zstart, n_slots = _route_plan(idx)
            xs = _dispatch(xn, pos0, pos1, zstart, n_slots, min(1024, seq))
            ys = _experts(xs, block_expert, n_used, mw_in_bf, mw_out_bf, FFN_CHUNKS)
            out = _combine(pos0, pos1, ys, h, wts[0].reshape(t_lat, 1), wts[1].reshape(t_lat, 1),
                           lat[5], final_g, seq, min(1024, seq))
            h = out
    return out.reshape(b, seq, d)
```
